```python
import math
import jax, jax.numpy as jnp
from jax import lax
import numpy as np

D_MODEL = 1024
BATCH = 16
SEQ = 4096
DEPTH = 4

MIX_WIDTH = D_MODEL
A_WIDTH = MIX_WIDTH // 2
A_GROUPS = 8
A_GROUP_DIM = A_WIDTH // A_GROUPS
CHUNK = 128
B_HEADS = 8
HEAD_DIM = 64
B_KV_HEADS = 2
B_GROUP = B_HEADS // B_KV_HEADS
B_WIDTH = B_HEADS * HEAD_DIM
KV_WIDTH = B_KV_HEADS * HEAD_DIM
WINDOW = 128
BLOCK = 128
REL_BUCKETS = 32
REL_MAX_DIST = 128
NORM_EPS = 1e-6
IN_WIDTH = 3 * A_WIDTH + 2 * B_WIDTH + 2 * KV_WIDTH

kernel_name = "hybrid_gmlp_swa_sink_sandwich"


def rms_norm(x, g):
    xf = x.astype(jnp.float32)
    y = xf * lax.rsqrt(jnp.mean(xf * xf, axis=-1, keepdims=True) + NORM_EPS)
    return (y * g.astype(jnp.float32)).astype(x.dtype)


def layer_norm(x, g, b):
    xf = x.astype(jnp.float32)
    mu = jnp.mean(xf, axis=-1, keepdims=True)
    xc = xf - mu
    y = xc * lax.rsqrt(jnp.mean(xc * xc, axis=-1, keepdims=True) + NORM_EPS)
    return (y * g.astype(jnp.float32) + b.astype(jnp.float32)).astype(x.dtype)


def t5_bucket(dist):
    max_exact = REL_BUCKETS // 2
    safe = jnp.maximum(dist, 1).astype(jnp.float32)
    large = max_exact + (jnp.log(safe / max_exact) / math.log(REL_MAX_DIST / max_exact)
                         * (REL_BUCKETS - max_exact)).astype(jnp.int32)
    large = jnp.minimum(large, REL_BUCKETS - 1)
    return jnp.where(dist < max_exact, dist, large)


def chunked_spatial_gating(u, v, ln_g, ln_b, w_s, b_s):
    bsz, seq, _ = v.shape
    n_chunks = seq // CHUNK
    vn = layer_norm(v, ln_g, ln_b).reshape(bsz, n_chunks, CHUNK, A_GROUPS, A_GROUP_DIM)
    causal = jnp.tril(jnp.ones((CHUNK, CHUNK), dtype=bool))
    w = jnp.where(causal[None], w_s, jnp.zeros_like(w_s))
    mixed = jnp.einsum('hij,bcjhd->bcihd', w, vn) + b_s.T[None, None, :, :, None]
    return u * mixed.reshape(bsz, seq, A_WIDTH)


def sliding_window_sink_attention(q, k, v, sinks, rel_bias):
    bsz, seq = q.shape[0], q.shape[1]
    nb = seq // BLOCK
    qb = q.reshape(bsz, nb, BLOCK, B_KV_HEADS, B_GROUP, HEAD_DIM)
    pad = jnp.zeros((bsz, BLOCK, B_KV_HEADS, HEAD_DIM), k.dtype)
    kp = jnp.concatenate([pad, k], axis=1).reshape(bsz, nb + 1, BLOCK, B_KV_HEADS, HEAD_DIM)
    vp = jnp.concatenate([pad, v], axis=1).reshape(bsz, nb + 1, BLOCK, B_KV_HEADS, HEAD_DIM)
    kb = jnp.concatenate([kp[:, :-1], kp[:, 1:]], axis=2)
    vb = jnp.concatenate([vp[:, :-1], vp[:, 1:]], axis=2)

    logits = jnp.einsum('bnqkgd,bnskd->bnkgqs', qb, kb).astype(jnp.float32) * (HEAD_DIM ** -0.5)

    q_loc = jnp.arange(BLOCK)[:, None]
    k_loc = jnp.arange(2 * BLOCK)[None, :]
    dist = q_loc + BLOCK - k_loc
    in_window = (dist >= 0) & (dist < WINDOW)
    key_pos = (jnp.arange(nb)[:, None] - 1) * BLOCK + jnp.arange(2 * BLOCK)[None, :]
    mask = in_window[None] & (key_pos >= 0)[:, None, :]

    bias = rel_bias.astype(jnp.float32)[t5_bucket(jnp.maximum(dist, 0))]
    bias = bias.transpose(2, 0, 1).reshape(B_KV_HEADS, B_GROUP, BLOCK, 2 * BLOCK)
    logits = jnp.where(mask[None, :, None, None], logits + bias, jnp.float32(-1e30))

    sink = sinks.astype(jnp.float32).reshape(B_KV_HEADS, B_GROUP)[None, None, :, :, None, None]
    m = jnp.maximum(jnp.max(logits, axis=-1, keepdims=True), sink)
    e = jnp.exp(logits - m)
    probs = e / (jnp.sum(e, axis=-1, keepdims=True) + jnp.exp(sink - m))
    out = jnp.einsum('bnkgqs,bnskd->bnqkgd', probs.astype(vb.dtype), vb)
    return out.reshape(bsz, seq, B_WIDTH)


def hybrid_layer(x, pre_g, w_in, ln_g, ln_b, w_s, b_s, sinks, rel_bias, w_out, post_g):
    bsz, seq, _ = x.shape
    h = rms_norm(x, pre_g)
    proj = h @ w_in
    splits = [2 * A_WIDTH, 3 * A_WIDTH, 3 * A_WIDTH + B_WIDTH,
              3 * A_WIDTH + B_WIDTH + KV_WIDTH, 3 * A_WIDTH + B_WIDTH + 2 * KV_WIDTH]
    a_uv, a_z, q, k, v, b_z = jnp.split(proj, splits, axis=-1)

    a_uv = jax.nn.gelu(a_uv)
    u, vv = jnp.split(a_uv, 2, axis=-1)
    y_a = chunked_spatial_gating(u, vv, ln_g, ln_b, w_s, b_s) * jax.nn.silu(a_z)

    q = q.reshape(bsz, seq, B_HEADS, HEAD_DIM)
    k = k.reshape(bsz, seq, B_KV_HEADS, HEAD_DIM)
    v = v.reshape(bsz, seq, B_KV_HEADS, HEAD_DIM)
    y_b = sliding_window_sink_attention(q, k, v, sinks, rel_bias) * jax.nn.silu(b_z)

    y = jnp.concatenate([y_a, y_b], axis=-1) @ w_out
    return x + rms_norm(y, post_g)


def _fwd_setup_inputs(seed: int = 0) -> dict:
    key = jax.random.key(seed)
    ks = jax.random.split(key, 12)
    f32 = jnp.float32
    x = jax.random.normal(ks[0], (BATCH, SEQ, D_MODEL), f32)
    pre_norm_g = 1.0 + 0.05 * jax.random.normal(ks[1], (DEPTH, D_MODEL), f32)
    w_in = jax.random.normal(ks[2], (DEPTH, D_MODEL, IN_WIDTH), f32) * (D_MODEL ** -0.5)
    ln_v_g = 1.0 + 0.05 * jax.random.normal(ks[3], (DEPTH, A_WIDTH), f32)
    ln_v_b = 0.02 * jax.random.normal(ks[4], (DEPTH, A_WIDTH), f32)
    w_spatial = jax.random.normal(ks[5], (DEPTH, A_GROUPS, CHUNK, CHUNK), f32) * (CHUNK ** -0.5)
    b_spatial = 1.0 + 0.1 * jax.random.normal(ks[6], (DEPTH, A_GROUPS, CHUNK), f32)
    sinks = 0.5 * jax.random.normal(ks[7], (DEPTH, B_HEADS), f32)
    rel_bias = 0.5 * jax.random.normal(ks[8], (REL_BUCKETS, B_HEADS), f32)
    w_out = jax.random.normal(ks[9], (DEPTH, MIX_WIDTH, D_MODEL), f32) * (MIX_WIDTH ** -0.5)
    post_norm_g = 1.0 + 0.05 * jax.random.normal(ks[10], (DEPTH, D_MODEL), f32)
    return {"x": x, "pre_norm_g": pre_norm_g, "w_in": w_in, "ln_v_g": ln_v_g, "ln_v_b": ln_v_b,
            "w_spatial": w_spatial, "b_spatial": b_spatial, "sinks": sinks, "rel_bias": rel_bias,
            "w_out": w_out, "post_norm_g": post_norm_g}


def _fwd_reference(x, pre_norm_g, w_in, ln_v_g, ln_v_b, w_spatial, b_spatial, sinks, rel_bias,
              w_out, post_norm_g):
    for layer in range(DEPTH):
        x = hybrid_layer(x, pre_norm_g[layer], w_in[layer], ln_v_g[layer], ln_v_b[layer],
                         w_spatial[layer], b_spatial[layer], sinks[layer], rel_bias,
                         w_out[layer], post_norm_g[layer])
    return x


import jax as _jax
import jax.numpy as _jnp

TWIN_FORMAT = 'train_step'
FWD_PARAMS = ['x', 'pre_norm_g', 'w_in', 'ln_v_g', 'ln_v_b', 'w_spatial', 'b_spatial', 'sinks', 'rel_bias', 'w_out', 'post_norm_g']
TWIN_WEIGHTS = ['pre_norm_g', 'w_in', 'ln_v_g', 'ln_v_b', 'w_spatial', 'b_spatial', 'sinks', 'rel_bias', 'w_out', 'post_norm_g']
TWIN_DIFF_INPUT = 'x'
TWIN_INPUTS = ['x', 'pre_norm_g', 'w_in', 'ln_v_g', 'ln_v_b', 'w_spatial', 'b_spatial', 'sinks', 'rel_bias', 'w_out', 'post_norm_g', 'loss_target', 'm_pre_norm_g', 'm_w_in', 'm_ln_v_g', 'm_ln_v_b', 'm_w_spatial', 'm_b_spatial', 'm_sinks', 'm_rel_bias', 'm_w_out', 'm_post_norm_g', 'v_pre_norm_g', 'v_w_in', 'v_ln_v_g', 'v_ln_v_b', 'v_w_spatial', 'v_b_spatial', 'v_sinks', 'v_rel_bias', 'v_w_out', 'v_post_norm_g']
TWIN_OUTPUTS = ['loss', 'grad_x', 'grad_pre_norm_g', 'grad_w_in', 'grad_ln_v_g', 'grad_ln_v_b', 'grad_w_spatial', 'grad_b_spatial', 'grad_sinks', 'grad_rel_bias', 'grad_w_out', 'grad_post_norm_g', 'delta_pre_norm_g', 'delta_w_in', 'delta_ln_v_g', 'delta_ln_v_b', 'delta_w_spatial', 'delta_b_spatial', 'delta_sinks', 'delta_rel_bias', 'delta_w_out', 'delta_post_norm_g', 'new_m_pre_norm_g', 'new_m_w_in', 'new_m_ln_v_g', 'new_m_ln_v_b', 'new_m_w_spatial', 'new_m_b_spatial', 'new_m_sinks', 'new_m_rel_bias', 'new_m_w_out', 'new_m_post_norm_g', 'new_v_pre_norm_g', 'new_v_w_in', 'new_v_ln_v_g', 'new_v_ln_v_b', 'new_v_w_spatial', 'new_v_b_spatial', 'new_v_sinks', 'new_v_rel_bias', 'new_v_w_out', 'new_v_post_norm_g']
TWIN_LEAF_KINDS = {'loss': 'loss', 'grad_x': 'grad_x', 'grad_pre_norm_g': 'grad_w', 'grad_w_in': 'grad_w', 'grad_ln_v_g': 'grad_w', 'grad_ln_v_b': 'grad_w', 'grad_w_spatial': 'grad_w', 'grad_b_spatial': 'grad_w', 'grad_sinks': 'grad_w', 'grad_rel_bias': 'grad_w', 'grad_w_out': 'grad_w', 'grad_post_norm_g': 'grad_w', 'delta_pre_norm_g': 'delta_w', 'delta_w_in': 'delta_w', 'delta_ln_v_g': 'delta_w', 'delta_ln_v_b': 'delta_w', 'delta_w_spatial': 'delta_w', 'delta_b_spatial': 'delta_w', 'delta_sinks': 'delta_w', 'delta_rel_bias': 'delta_w', 'delta_w_out': 'delta_w', 'delta_post_norm_g': 'delta_w', 'new_m_pre_norm_g': 'new_m', 'new_m_w_in': 'new_m', 'new_m_ln_v_g': 'new_m', 'new_m_ln_v_b': 'new_m', 'new_m_w_spatial': 'new_m', 'new_m_b_spatial': 'new_m', 'new_m_sinks': 'new_m', 'new_m_rel_bias': 'new_m', 'new_m_w_out': 'new_m', 'new_m_post_norm_g': 'new_m', 'new_v_pre_norm_g': 'new_v', 'new_v_w_in': 'new_v', 'new_v_ln_v_g': 'new_v', 'new_v_ln_v_b': 'new_v', 'new_v_w_spatial': 'new_v', 'new_v_b_spatial': 'new_v', 'new_v_sinks': 'new_v', 'new_v_rel_bias': 'new_v', 'new_v_w_out': 'new_v', 'new_v_post_norm_g': 'new_v'}


def _forward(args):
    return _fwd_reference(*[args[k] for k in FWD_PARAMS])


def _output_shape():
    out = _jax.eval_shape(lambda: _forward(_fwd_setup_inputs(0)))
    return out.shape, out.dtype

N_MICROBATCH = 1
ADAM_LR = 0.001
ADAM_B1 = 0.9
ADAM_B2 = 0.999
ADAM_EPS = 1e-08
ADAM_WD = 0.01
ADAM_STEP = 10
PER_EXAMPLE_BATCH_AXIS = {'x': 0, 'loss_target': 0}
SHARED_INPUTS = []
_WEIGHT_DTYPES = {'pre_norm_g': _jnp.float32, 'w_in': _jnp.float32, 'ln_v_g': _jnp.float32, 'ln_v_b': _jnp.float32, 'w_spatial': _jnp.float32, 'b_spatial': _jnp.float32, 'sinks': _jnp.float32, 'rel_bias': _jnp.float32, 'w_out': _jnp.float32, 'post_norm_g': _jnp.float32}
MOMENT_SCALE = {'pre_norm_g': 1.739701e+00, 'w_in': 1.086404e+00, 'ln_v_g': 7.499862e-01, 'ln_v_b': 8.021370e-01, 'w_spatial': 5.484066e-01, 'b_spatial': 8.534466e-01, 'sinks': 2.705760e-01, 'rel_bias': 1.070050e+00, 'w_out': 1.684053e+00, 'post_norm_g': 6.354616e+01}


def _to_microbatches(a, axis):
    t = _jnp.moveaxis(a, axis, 0)
    t = t.reshape((N_MICROBATCH, t.shape[0] // N_MICROBATCH) + t.shape[1:])
    return _jnp.moveaxis(t, 1, axis + 1)


def setup_inputs(seed: int = 0) -> dict:
    inp = _fwd_setup_inputs(seed)
    key = _jax.random.fold_in(_jax.random.key(seed), 7919)
    shape, _ = _output_shape()
    out = dict(inp)
    out["loss_target"] = _jax.random.normal(_jax.random.fold_in(key, 0), shape, _jnp.float32)
    for i, name in enumerate(TWIN_WEIGHTS):
        w = inp[name].astype(_jnp.float32)
        if MOMENT_SCALE is None:
            s = _jnp.sqrt(_jnp.mean(_jnp.square(w)) + 1e-30)
        else:
            s = MOMENT_SCALE[name]
        km, kv = _jax.random.split(_jax.random.fold_in(key, i + 1))
        out[name] = w
        out["m_" + name] = s * _jax.random.normal(km, w.shape, _jnp.float32)
        out["v_" + name] = (s * s) * _jax.random.uniform(kv, w.shape, _jnp.float32, 0.5, 1.5)
    if N_MICROBATCH > 1:
        for name, axis in PER_EXAMPLE_BATCH_AXIS.items():
            out[name] = _to_microbatches(out[name], axis)
    return {'x': out['x'], 'pre_norm_g': out['pre_norm_g'], 'w_in': out['w_in'], 'ln_v_g': out['ln_v_g'], 'ln_v_b': out['ln_v_b'], 'w_spatial': out['w_spatial'], 'b_spatial': out['b_spatial'], 'sinks': out['sinks'], 'rel_bias': out['rel_bias'], 'w_out': out['w_out'], 'post_norm_g': out['post_norm_g'], 'loss_target': out['loss_target'], 'm_pre_norm_g': out['m_pre_norm_g'], 'm_w_in': out['m_w_in'], 'm_ln_v_g': out['m_ln_v_g'], 'm_ln_v_b': out['m_ln_v_b'], 'm_w_spatial': out['m_w_spatial'], 'm_b_spatial': out['m_b_spatial'], 'm_sinks': out['m_sinks'], 'm_rel_bias': out['m_rel_bias'], 'm_w_out': out['m_w_out'], 'm_post_norm_g': out['m_post_norm_g'], 'v_pre_norm_g': out['v_pre_norm_g'], 'v_w_in': out['v_w_in'], 'v_ln_v_g': out['v_ln_v_g'], 'v_ln_v_b': out['v_ln_v_b'], 'v_w_spatial': out['v_w_spatial'], 'v_b_spatial': out['v_b_spatial'], 'v_sinks': out['v_sinks'], 'v_rel_bias': out['v_rel_bias'], 'v_w_out': out['v_w_out'], 'v_post_norm_g': out['v_post_norm_g']}


def _loss(weights, diff, rest, loss_target):
    with _jax.named_scope("forward"):
        args = {**rest, TWIN_DIFF_INPUT: diff, **{k: w.astype(_WEIGHT_DTYPES[k]) for k, w in weights.items()}}
        y = _forward(args)
    with _jax.named_scope("loss_head"):
        err = _jnp.square(y.astype(_jnp.float32) - loss_target)
        return 0.5 * _jnp.sum(_jnp.mean(err, axis=-1)) if err.ndim else 0.5 * err


def _adamw(w, g, m, v):
    m = ADAM_B1 * m + (1.0 - ADAM_B1) * g
    v = ADAM_B2 * v + (1.0 - ADAM_B2) * _jnp.square(g)
    m_hat = m / (1.0 - ADAM_B1 ** ADAM_STEP)
    v_hat = v / (1.0 - ADAM_B2 ** ADAM_STEP)
    delta = -ADAM_LR * (m_hat / (_jnp.sqrt(v_hat) + ADAM_EPS) + ADAM_WD * w)
    return delta, m, v


def reference(x, pre_norm_g, w_in, ln_v_g, ln_v_b, w_spatial, b_spatial, sinks, rel_bias, w_out, post_norm_g, loss_target, m_pre_norm_g, m_w_in, m_ln_v_g, m_ln_v_b, m_w_spatial, m_b_spatial, m_sinks, m_rel_bias, m_w_out, m_post_norm_g, v_pre_norm_g, v_w_in, v_ln_v_g, v_ln_v_b, v_w_spatial, v_b_spatial, v_sinks, v_rel_bias, v_w_out, v_post_norm_g):
    given = dict(x=x, pre_norm_g=pre_norm_g, w_in=w_in, ln_v_g=ln_v_g, ln_v_b=ln_v_b, w_spatial=w_spatial, b_spatial=b_spatial, sinks=sinks, rel_bias=rel_bias, w_out=w_out, post_norm_g=post_norm_g, loss_target=loss_target, m_pre_norm_g=m_pre_norm_g, m_w_in=m_w_in, m_ln_v_g=m_ln_v_g, m_ln_v_b=m_ln_v_b, m_w_spatial=m_w_spatial, m_b_spatial=m_b_spatial, m_sinks=m_sinks, m_rel_bias=m_rel_bias, m_w_out=m_w_out, m_post_norm_g=m_post_norm_g, v_pre_norm_g=v_pre_norm_g, v_w_in=v_w_in, v_ln_v_g=v_ln_v_g, v_ln_v_b=v_ln_v_b, v_w_spatial=v_w_spatial, v_b_spatial=v_b_spatial, v_sinks=v_sinks, v_rel_bias=v_rel_bias, v_w_out=v_w_out, v_post_norm_g=v_post_norm_g)
    weights = {n: given[n] for n in TWIN_WEIGHTS}
    shared = {n: given[n] for n in SHARED_INPUTS}
    per_example = {n: given[n] for n in ['x']}
    grad_fn = _jax.value_and_grad(_loss, argnums=(0, 1))

    def one_microbatch(ex, loss_target):
        ex = dict(ex)
        diff = ex.pop(TWIN_DIFF_INPUT)
        return grad_fn(weights, diff, {**shared, **ex}, loss_target)

    if N_MICROBATCH == 1:
        loss, (grad_w, grad_x) = one_microbatch(per_example, given["loss_target"])
    else:
        def body(carry, xs):
            loss_sum, grad_sum = carry
            l_k, (gw_k, gx_k) = one_microbatch(xs[0], xs[1])
            with _jax.named_scope("update"):
                return (loss_sum + l_k, _jax.tree.map(_jnp.add, grad_sum, gw_k)), gx_k

        init = (_jnp.zeros((), _jnp.float32), _jax.tree.map(_jnp.zeros_like, weights))
        (loss, grad_w), grad_x = _jax.lax.scan(body, init, (per_example, given["loss_target"]))
    with _jax.named_scope("update"):
        delta_w, new_m, new_v = {}, {}, {}
        for n in TWIN_WEIGHTS:
            delta_w[n], new_m[n], new_v[n] = _adamw(weights[n], grad_w[n], given["m_" + n], given["v_" + n])
    return (loss, grad_x, *[grad_w[n] for n in TWIN_WEIGHTS], *[delta_w[n] for n in TWIN_WEIGHTS],
            *[new_m[n] for n in TWIN_WEIGHTS], *[new_v[n] for n in TWIN_WEIGHTS])
```

```python
import math

import numpy as np
import jax
import jax.numpy as jnp
from jax import lax
from jax.experimental import pallas as pl
from jax.experimental.pallas import tpu as pltpu

F32 = jnp.float32
BF16 = jnp.bfloat16

D_MODEL = 1024
A_WIDTH = 512
A_GROUPS = 8
CHUNK = 128
B_HEADS = 8
HEAD_DIM = 64
B_WIDTH = 512
KV_WIDTH = 128
BLOCK = 128
REL_BUCKETS = 32
REL_MAX_DIST = 128
NORM_EPS = 1e-6
PA_WIDTH = 3 * A_WIDTH
PB_WIDTH = 2 * B_WIDTH + 2 * KV_WIDTH
IN_WIDTH = PA_WIDTH + PB_WIDTH
K_OFF, V_OFF, BZ_OFF = B_WIDTH, B_WIDTH + KV_WIDTH, B_WIDTH + 2 * KV_WIDTH
SCALE = HEAD_DIM ** -0.5
NEG = -1e30
N_DEV = 8
LANES = 128

ADAM_LR = 0.001
ADAM_B1 = 0.9
ADAM_B2 = 0.999
ADAM_EPS = 1e-08
ADAM_WD = 0.01
ADAM_STEP = 10

V7X_VMEM_BYTES = 64 * 1024 * 1024
VMEM_TEMP_BYTES = 12 * 1024 * 1024
MESH = pl.DeviceIdType.MESH


def _vmem_limit(block_bytes, scratch_bytes=0):
    need = 2 * sum(block_bytes) + scratch_bytes + VMEM_TEMP_BYTES
    return int(min(need, V7X_VMEM_BYTES - 8 * 1024 * 1024))


def _nbytes(shape, dtype):
    return int(np.prod(shape)) * jnp.dtype(dtype).itemsize


def _token_tile(tokens):
    tile = min(512, tokens // 2)
    assert tokens % tile == 0 and tile % CHUNK == 0, tokens
    return tile


def _row_tile(rows, cap=512):
    best = 8
    for t in range(8, cap + 1, 8):
        if rows % t == 0:
            best = t
    assert rows % best == 0, rows
    return best


def _mm(a, b):
    return lax.dot_general(a, b, (((1,), (0,)), ((), ())), preferred_element_type=F32)


def _mm_nt(a, b):
    return lax.dot_general(a, b, (((1,), (1,)), ((), ())), preferred_element_type=F32)


def _mm_tn(a, b):
    return lax.dot_general(a, b, (((0,), (0,)), ((), ())), preferred_element_type=F32)


_GELU_C = math.sqrt(2.0 / math.pi)


def _gelu(x):
    return 0.5 * x * (1.0 + jnp.tanh(_GELU_C * (x + 0.044715 * (x * x * x))))


def _gelu_grad(x):
    t = jnp.tanh(_GELU_C * (x + 0.044715 * (x * x * x)))
    return 0.5 * (1.0 + t) + 0.5 * x * (1.0 - t * t) * (_GELU_C * (1.0 + 3.0 * 0.044715 * (x * x)))


def _sigmoid(x):
    return 1.0 / (1.0 + jnp.exp(-x))


def _bucket_table():
    q = np.arange(BLOCK)[:, None]
    k = np.arange(2 * BLOCK)[None, :]
    dist = q + BLOCK - k
    max_exact = REL_BUCKETS // 2
    safe = np.maximum(dist, 1).astype(np.float32)
    large = max_exact + (np.log(safe / np.float32(max_exact)) / np.float32(math.log(REL_MAX_DIST / max_exact))
                         * np.float32(REL_BUCKETS - max_exact)).astype(np.int32)
    large = np.minimum(large, REL_BUCKETS - 1)
    bucket = np.where(dist < max_exact, dist, large)
    return np.where((dist >= 0) & (dist < BLOCK), bucket, -1).astype(np.int32)


def _slot(px, py, pc):
    return 4 * px + 2 * py + pc


def _all_gather(arrs, name):
    n = len(arrs)

    def body(*refs):
        ins, outs = refs[:n], refs[n:2 * n]
        send_sems, recv_sems, local_sems = refs[2 * n:]
        x, y, c = lax.axis_index("x"), lax.axis_index("y"), lax.axis_index("c")
        me, sibling = (x, y, c), (x, y, 1 - c)
        chips = [(1 - x, y), (x, 1 - y), (1 - x, 1 - y)]

        def copy(a, k, block, to, src=None):
            dst = outs[a].at[_slot(*block)]
            return pltpu.make_async_remote_copy(
                src_ref=dst if src is None else src, dst_ref=dst,
                send_sem=send_sems.at[7 * a + k], recv_sem=recv_sems.at[7 * a + k],
                device_id=to, device_id_type=MESH)

        mine = [pltpu.make_async_copy(ins[a], outs[a].at[_slot(*me)], local_sems.at[a]) for a in range(n)]
        for cp in mine:
            cp.start()
        first = []
        for a in range(n):
            first.append(copy(a, 0, me, sibling, src=ins[a]))
            first += [copy(a, 1 + j, me, (*chip, c), src=ins[a]) for j, chip in enumerate(chips)]
        for cp in first:
            cp.start()
        passed = []
        for j, chip in enumerate(chips):
            for a in range(n):
                copy(a, 1 + j, (*chip, c), me).wait_recv()
                fwd = copy(a, 4 + j, (*chip, c), sibling)
                fwd.start()
                passed.append(fwd)
        for a in range(n):
            copy(a, 0, sibling, me).wait_recv()
            for j, chip in enumerate(chips):
                copy(a, 4 + j, (*chip, 1 - c), me).wait_recv()
        for cp in first + passed:
            cp.wait_send()
        for cp in mine:
            cp.wait()

    any_spec = pl.BlockSpec(memory_space=pl.ANY)
    return pl.pallas_call(
        body, name=name,
        out_shape=[jax.ShapeDtypeStruct((N_DEV,) + a.shape, a.dtype) for a in arrs],
        in_specs=[any_spec] * n, out_specs=[any_spec] * n,
        scratch_shapes=[pltpu.SemaphoreType.DMA((7 * n,)), pltpu.SemaphoreType.DMA((7 * n,)),
                        pltpu.SemaphoreType.DMA((n,))],
    )(*arrs)


def _all_to_all(arrs, name):
    n = len(arrs)

    def body(*refs):
        ins, outs = refs[:n], refs[n:2 * n]
        send_sems, recv_sems, local_sems = refs[2 * n:]
        x, y, c = lax.axis_index("x"), lax.axis_index("y"), lax.axis_index("c")
        me = (x, y, c)
        peers = []
        for k in range(1, N_DEV):
            fx, fy, fc = (k >> 2) & 1, (k >> 1) & 1, k & 1
            peers.append((1 - x if fx else x, 1 - y if fy else y, 1 - c if fc else c))

        def copy(a, k, peer):
            return pltpu.make_async_remote_copy(
                src_ref=ins[a].at[_slot(*peer)], dst_ref=outs[a].at[_slot(*me)],
                send_sem=send_sems.at[7 * a + k], recv_sem=recv_sems.at[7 * a + k],
                device_id=peer, device_id_type=MESH)

        def arrival(a, k, peer):
            return pltpu.make_async_remote_copy(
                src_ref=ins[a].at[_slot(*peer)], dst_ref=outs[a].at[_slot(*peer)],
                send_sem=send_sems.at[7 * a + k], recv_sem=recv_sems.at[7 * a + k],
                device_id=peer, device_id_type=MESH)

        mine = [pltpu.make_async_copy(ins[a].at[_slot(*me)], outs[a].at[_slot(*me)], local_sems.at[a])
                for a in range(n)]
        for cp in mine:
            cp.start()
        sends = [copy(a, k, peer) for a in range(n) for k, peer in enumerate(peers)]
        for cp in sends:
            cp.start()
        for a in range(n):
            for k, peer in enumerate(peers):
                arrival(a, k, peer).wait_recv()
        for cp in sends:
            cp.wait_send()
        for cp in mine:
            cp.wait()

    any_spec = pl.BlockSpec(memory_space=pl.ANY)
    return pl.pallas_call(
        body, name=name,
        out_shape=[jax.ShapeDtypeStruct(a.shape, a.dtype) for a in arrs],
        in_specs=[any_spec] * n, out_specs=[any_spec] * n,
        scratch_shapes=[pltpu.SemaphoreType.DMA((7 * n,)), pltpu.SemaphoreType.DMA((7 * n,)),
                        pltpu.SemaphoreType.DMA((n,))],
    )(*arrs)


def _bias_table(rel_bias, bucket):
    def body(rel_ref, bucket_ref, out_ref):
        bk = bucket_ref[...]
        for h in range(B_HEADS):
            def pick(b, acc, h=h):
                return jnp.where(bk == b, rel_ref[b, h], acc)
            acc = lax.fori_loop(0, REL_BUCKETS, pick, jnp.zeros((BLOCK, 2 * BLOCK), F32))
            out_ref[h] = jnp.where(bk < 0, NEG, acc)

    return pl.pallas_call(
        body, name="bias_table",
        out_shape=jax.ShapeDtypeStruct((B_HEADS, BLOCK, 2 * BLOCK), F32),
        in_specs=[pl.BlockSpec(memory_space=pltpu.SMEM), pl.BlockSpec(memory_space=pltpu.VMEM)],
        out_specs=pl.BlockSpec(memory_space=pltpu.VMEM),
    )(rel_bias, bucket)


def _inproj_fwd(x2, pre_g3, wa, wb, layer):
    tokens = x2.shape[0]
    tm = _token_tile(tokens)

    def body(x_ref, g_ref, wa_ref, wb_ref, h_ref, pa_ref, pb_ref):
        x = x_ref[...]
        r = lax.rsqrt(jnp.mean(x * x, axis=-1, keepdims=True) + NORM_EPS)
        h = (x * r * g_ref[...]).astype(BF16)
        h_ref[...] = h
        pa_ref[...] = _mm(h, wa_ref[...]).astype(BF16)
        pb_ref[...] = _mm(h, wb_ref[...]).astype(BF16)

    blocks = [_nbytes((tm, D_MODEL), F32), _nbytes((D_MODEL, PA_WIDTH), BF16), _nbytes((D_MODEL, PB_WIDTH), BF16),
              _nbytes((tm, D_MODEL), BF16), _nbytes((tm, PA_WIDTH), BF16), _nbytes((tm, PB_WIDTH), BF16)]
    return pl.pallas_call(
        body, name=f"inproj_fwd_{layer}", grid=(tokens // tm,),
        in_specs=[pl.BlockSpec((tm, D_MODEL), lambda i: (i, 0)),
                  pl.BlockSpec((None, 1, D_MODEL), lambda i: (layer, 0, 0)),
                  pl.BlockSpec((None, D_MODEL, PA_WIDTH), lambda i: (layer, 0, 0)),
                  pl.BlockSpec((None, D_MODEL, PB_WIDTH), lambda i: (layer, 0, 0))],
        out_specs=[pl.BlockSpec((tm, D_MODEL), lambda i: (i, 0)),
                   pl.BlockSpec((tm, PA_WIDTH), lambda i: (i, 0)),
                   pl.BlockSpec((tm, PB_WIDTH), lambda i: (i, 0))],
        out_shape=[jax.ShapeDtypeStruct((tokens, D_MODEL), BF16),
                   jax.ShapeDtypeStruct((tokens, PA_WIDTH), BF16),
                   jax.ShapeDtypeStruct((tokens, PB_WIDTH), BF16)],
        compiler_params=pltpu.CompilerParams(dimension_semantics=("parallel",),
                                             vmem_limit_bytes=_vmem_limit(blocks)),
    )(x2, pre_g3, wa, wb)


def _gmlp_forward_chunk(pa, ln_g, ln_b, wm_ref, bsb):
    pu = pa[:, 0:A_WIDTH].astype(F32)
    pv = pa[:, A_WIDTH:2 * A_WIDTH].astype(F32)
    pz = pa[:, 2 * A_WIDTH:3 * A_WIDTH].astype(F32)
    u = _gelu(pu)
    vv = _gelu(pv)
    mu = jnp.mean(vv, axis=-1, keepdims=True)
    xc = vv - mu
    rstd = lax.rsqrt(jnp.mean(xc * xc, axis=-1, keepdims=True) + NORM_EPS)
    vhat = xc * rstd
    vnb = (vhat * ln_g + ln_b).astype(BF16)
    low = lax.broadcasted_iota(jnp.int32, (CHUNK, LANES), 1) < HEAD_DIM
    parts = []
    for p in range(A_GROUPS // 2):
        vp = vnb[:, LANES * p:LANES * (p + 1)]
        parts.append(jnp.where(low, _mm(wm_ref[2 * p], vp), _mm(wm_ref[2 * p + 1], vp)))
    mixed = jnp.concatenate(parts, axis=1) + bsb
    sg = _sigmoid(pz)
    return pu, pv, pz, u, vhat, rstd, vnb, mixed, sg


def _gmlp_fwd(pa, ln_g3, ln_b3, wm, bsb, layer):
    tokens = pa.shape[0]
    tc = _token_tile(tokens)

    def body(pa_ref, lg_ref, lb_ref, wm_ref, bsb_ref, ya_ref):
        def chunk(ci, carry):
            rows = pl.ds(pl.multiple_of(ci * CHUNK, CHUNK), CHUNK)
            _, _, pz, u, _, _, _, mixed, sg = _gmlp_forward_chunk(
                pa_ref[rows, :], lg_ref[...], lb_ref[...], wm_ref, bsb_ref[...])
            ya_ref[rows, :] = (u * mixed * (pz * sg)).astype(BF16)
            return carry
        lax.fori_loop(0, tc // CHUNK, chunk, 0)

    blocks = [_nbytes((tc, PA_WIDTH), BF16), _nbytes((A_GROUPS, CHUNK, CHUNK), BF16),
              _nbytes((CHUNK, A_WIDTH), F32), _nbytes((tc, A_WIDTH), BF16)]
    return pl.pallas_call(
        body, name=f"gmlp_fwd_{layer}", grid=(tokens // tc,),
        in_specs=[pl.BlockSpec((tc, PA_WIDTH), lambda i: (i, 0)),
                  pl.BlockSpec((None, 1, A_WIDTH), lambda i: (layer, 0, 0)),
                  pl.BlockSpec((None, 1, A_WIDTH), lambda i: (layer, 0, 0)),
                  pl.BlockSpec((None, A_GROUPS, CHUNK, CHUNK), lambda i: (layer, 0, 0, 0)),
                  pl.BlockSpec((None, CHUNK, A_WIDTH), lambda i: (layer, 0, 0))],
        out_specs=pl.BlockSpec((tc, A_WIDTH), lambda i: (i, 0)),
        out_shape=jax.ShapeDtypeStruct((tokens, A_WIDTH), BF16),
        compiler_params=pltpu.CompilerParams(dimension_semantics=("parallel",),
                                             vmem_limit_bytes=_vmem_limit(blocks)),
    )(pa, ln_g3, ln_b3, wm, bsb)


def _kv_variants(prev, cur, off):
    t = jnp.concatenate([prev[:, off:off + KV_WIDTH], cur[:, off:off + KV_WIDTH]], axis=0).astype(F32)
    rolled = pltpu.roll(t, HEAD_DIM, 1)
    low = lax.broadcasted_iota(jnp.int32, t.shape, 1) < HEAD_DIM
    zero = jnp.zeros_like(t)
    head0 = (jnp.where(low, t, zero).astype(BF16), jnp.where(low, zero, rolled).astype(BF16))
    head1 = (jnp.where(low, rolled, zero).astype(BF16), jnp.where(low, zero, t).astype(BF16))
    return head0, head1


def _attn_probs(qp, k_op, bias_h, sink, kill):
    s = _mm_nt(qp, k_op) * SCALE + bias_h
    s = jnp.where(kill, NEG, s)
    m = jnp.maximum(jnp.max(s, axis=-1, keepdims=True), sink)
    e = jnp.exp(s - m)
    es = jnp.exp(sink - m)
    inv = 1.0 / (jnp.sum(e, axis=-1, keepdims=True) + es)
    return e * inv, es * inv


def _attn_fwd(pb, bias, sinks, b_loc, layer):
    tokens = pb.shape[0]
    nb = tokens // b_loc // BLOCK

    def body(sink_ref, cur_ref, prev_ref, bias_ref, yb_ref):
        n = pl.program_id(1)
        cur = cur_ref[...]
        prev = prev_ref[...]
        k_ops = _kv_variants(prev, cur, K_OFF)
        v_ops = _kv_variants(prev, cur, V_OFF)
        col = lax.broadcasted_iota(jnp.int32, (BLOCK, 2 * BLOCK), 1)
        kill = jnp.logical_and(n == 0, col < BLOCK)
        for p in range(B_HEADS // 2):
            kvh = p // (B_HEADS // 4)
            qp = cur[:, LANES * p:LANES * (p + 1)]
            out = jnp.zeros((BLOCK, LANES), F32)
            for j in range(2):
                hd = 2 * p + j
                probs, _ = _attn_probs(qp, k_ops[kvh][j], bias_ref[hd], sink_ref[layer, hd], kill)
                out = out + _mm(probs.astype(BF16), v_ops[kvh][j])
            bz = cur[:, BZ_OFF + LANES * p:BZ_OFF + LANES * (p + 1)].astype(F32)
            yb_ref[:, LANES * p:LANES * (p + 1)] = (out * (bz * _sigmoid(bz))).astype(BF16)

    blocks = [2 * _nbytes((BLOCK, PB_WIDTH), BF16), _nbytes((B_HEADS, BLOCK, 2 * BLOCK), F32),
              _nbytes((BLOCK, B_WIDTH), BF16)]
    return pl.pallas_call(
        body, name=f"attn_fwd_{layer}", grid=(b_loc, nb),
        in_specs=[pl.BlockSpec(memory_space=pltpu.SMEM),
                  pl.BlockSpec((BLOCK, PB_WIDTH), lambda b, n: (b * nb + n, 0)),
                  pl.BlockSpec((BLOCK, PB_WIDTH), lambda b, n: (b * nb + jnp.maximum(n - 1, 0), 0)),
                  pl.BlockSpec((B_HEADS, BLOCK, 2 * BLOCK), lambda b, n: (0, 0, 0))],
        out_specs=pl.BlockSpec((BLOCK, B_WIDTH), lambda b, n: (b * nb + n, 0)),
        out_shape=jax.ShapeDtypeStruct((tokens, B_WIDTH), BF16),
        compiler_params=pltpu.CompilerParams(dimension_semantics=("parallel", "parallel"),
                                             vmem_limit_bytes=_vmem_limit(blocks)),
    )(sinks, pb, pb, bias)


def _outproj_fwd(ya, yb, wo, x2, post_g3, layer):
    tokens = x2.shape[0]
    tm = _token_tile(tokens)

    def body(ya_ref, yb_ref, woa_ref, wob_ref, x_ref, g_ref, y_ref, xn_ref):
        y = _mm(ya_ref[...], woa_ref[...]) + _mm(yb_ref[...], wob_ref[...])
        r = lax.rsqrt(jnp.mean(y * y, axis=-1, keepdims=True) + NORM_EPS)
        y_ref[...] = y.astype(BF16)
        xn_ref[...] = x_ref[...] + y * r * g_ref[...]

    half = D_MODEL // 2
    blocks = [2 * _nbytes((tm, half), BF16), 2 * _nbytes((half, D_MODEL), BF16), 2 * _nbytes((tm, D_MODEL), F32),
              _nbytes((tm, D_MODEL), BF16)]
    return pl.pallas_call(
        body, name=f"outproj_fwd_{layer}", grid=(tokens // tm,),
        in_specs=[pl.BlockSpec((tm, half), lambda i: (i, 0)),
                  pl.BlockSpec((tm, half), lambda i: (i, 0)),
                  pl.BlockSpec((None, half, D_MODEL), lambda i: (layer, 0, 0)),
                  pl.BlockSpec((None, half, D_MODEL), lambda i: (layer, 1, 0)),
                  pl.BlockSpec((tm, D_MODEL), lambda i: (i, 0)),
                  pl.BlockSpec((None, 1, D_MODEL), lambda i: (layer, 0, 0))],
        out_specs=[pl.BlockSpec((tm, D_MODEL), lambda i: (i, 0)),
                   pl.BlockSpec((tm, D_MODEL), lambda i: (i, 0))],
        out_shape=[jax.ShapeDtypeStruct((tokens, D_MODEL), BF16),
                   jax.ShapeDtypeStruct((tokens, D_MODEL), F32)],
        compiler_params=pltpu.CompilerParams(dimension_semantics=("parallel",),
                                             vmem_limit_bytes=_vmem_limit(blocks)),
    )(ya, yb, wo, wo, x2, post_g3)


def _loss_head(xl, tgt):
    tokens = xl.shape[0]
    tm = _token_tile(tokens)
    steps = tokens // tm

    def body(x_ref, t_ref, g_ref, loss_ref, acc_ref):
        i = pl.program_id(0)

        @pl.when(i == 0)
        def _():
            acc_ref[...] = jnp.zeros_like(acc_ref)

        err = x_ref[...] - t_ref[...]
        g_ref[...] = err * (1.0 / D_MODEL)
        acc_ref[...] += jnp.sum(err * err, axis=0, keepdims=True)

        @pl.when(i == steps - 1)
        def _():
            total = jnp.sum(acc_ref[...], axis=-1, keepdims=True) * (0.5 / D_MODEL)
            loss_ref[...] = jnp.broadcast_to(total, loss_ref.shape)

    blocks = [3 * _nbytes((tm, D_MODEL), F32)]
    return pl.pallas_call(
        body, name="loss_head", grid=(steps,),
        in_specs=[pl.BlockSpec((tm, D_MODEL), lambda i: (i, 0)),
                  pl.BlockSpec((tm, D_MODEL), lambda i: (i, 0))],
        out_specs=[pl.BlockSpec((tm, D_MODEL), lambda i: (i, 0)),
                   pl.BlockSpec((1, LANES), lambda i: (0, 0))],
        out_shape=[jax.ShapeDtypeStruct((tokens, D_MODEL), F32),
                   jax.ShapeDtypeStruct((1, LANES), F32)],
        scratch_shapes=[pltpu.VMEM((1, D_MODEL), F32)],
        compiler_params=pltpu.CompilerParams(dimension_semantics=("arbitrary",),
                                             vmem_limit_bytes=_vmem_limit(blocks)),
    )(xl, tgt)


def _outproj_bwd(g, y, ya, yb, wo, post_g3, layer):
    tokens = g.shape[0]
    tm = _token_tile(tokens)
    half = D_MODEL // 2

    def body(g_ref, y_ref, ya_ref, yb_ref, woa_ref, wob_ref, pg_ref, dya_ref, dyb_ref, dwo_ref, dpg_ref):
        @pl.when(pl.program_id(0) == 0)
        def _():
            dwo_ref[...] = jnp.zeros_like(dwo_ref)
            dpg_ref[...] = jnp.zeros_like(dpg_ref)

        gv = g_ref[...]
        yf = y_ref[...].astype(F32)
        r = lax.rsqrt(jnp.mean(yf * yf, axis=-1, keepdims=True) + NORM_EPS)
        yhat = yf * r
        gg = gv * pg_ref[...]
        dy = r * (gg - yhat * jnp.mean(gg * yhat, axis=-1, keepdims=True))
        dpg_ref[...] += jnp.sum(gv * yhat, axis=0, keepdims=True)
        dyb16 = dy.astype(BF16)
        dya_ref[...] = _mm_nt(dyb16, woa_ref[...]).astype(BF16)
        dyb_ref[...] = _mm_nt(dyb16, wob_ref[...]).astype(BF16)
        dwo_ref[0:half, :] += _mm_tn(ya_ref[...], dyb16)
        dwo_ref[half:D_MODEL, :] += _mm_tn(yb_ref[...], dyb16)

    blocks = [_nbytes((tm, D_MODEL), F32), _nbytes((tm, D_MODEL), BF16), 4 * _nbytes((tm, half), BF16),
              2 * _nbytes((half, D_MODEL), BF16), _nbytes((D_MODEL, D_MODEL), F32)]
    return pl.pallas_call(
        body, name=f"outproj_bwd_{layer}", grid=(tokens // tm,),
        in_specs=[pl.BlockSpec((tm, D_MODEL), lambda i: (i, 0)),
                  pl.BlockSpec((tm, D_MODEL), lambda i: (i, 0)),
                  pl.BlockSpec((tm, half), lambda i: (i, 0)),
                  pl.BlockSpec((tm, half), lambda i: (i, 0)),
                  pl.BlockSpec((None, half, D_MODEL), lambda i: (layer, 0, 0)),
                  pl.BlockSpec((None, half, D_MODEL), lambda i: (layer, 1, 0)),
                  pl.BlockSpec((None, 1, D_MODEL), lambda i: (layer, 0, 0))],
        out_specs=[pl.BlockSpec((tm, half), lambda i: (i, 0)),
                   pl.BlockSpec((tm, half), lambda i: (i, 0)),
                   pl.BlockSpec((D_MODEL, D_MODEL), lambda i: (0, 0)),
                   pl.BlockSpec((1, D_MODEL), lambda i: (0, 0))],
        out_shape=[jax.ShapeDtypeStruct((tokens, half), BF16),
                   jax.ShapeDtypeStruct((tokens, half), BF16),
                   jax.ShapeDtypeStruct((D_MODEL, D_MODEL), F32),
                   jax.ShapeDtypeStruct((1, D_MODEL), F32)],
        compiler_params=pltpu.CompilerParams(dimension_semantics=("arbitrary",),
                                             vmem_limit_bytes=_vmem_limit(blocks)),
    )(g, y, ya, yb, wo, wo, post_g3)


def _gmlp_bwd(pa, dya, ln_g3, ln_b3, wm, wmt, bsb, layer):
    tokens = pa.shape[0]
    tc = _token_tile(tokens)
    steps = tokens // tc

    def body(pa_ref, dya_ref, lg_ref, lb_ref, wm_ref, wmt_ref, bsb_ref,
             da_ref, dws_ref, dbs_ref, dlg_ref, dlb_ref, dbsb_ref):
        i = pl.program_id(0)

        @pl.when(i == 0)
        def _():
            dws_ref[...] = jnp.zeros_like(dws_ref)
            dlg_ref[...] = jnp.zeros_like(dlg_ref)
            dlb_ref[...] = jnp.zeros_like(dlb_ref)
            dbsb_ref[...] = jnp.zeros_like(dbsb_ref)

        ln_g = lg_ref[...]
        low = lax.broadcasted_iota(jnp.int32, (CHUNK, LANES), 1) < HEAD_DIM

        def chunk(ci, carry):
            rows = pl.ds(pl.multiple_of(ci * CHUNK, CHUNK), CHUNK)
            pu, pv, pz, u, vhat, rstd, vnb, mixed, sg = _gmlp_forward_chunk(
                pa_ref[rows, :], ln_g, lb_ref[...], wm_ref, bsb_ref[...])
            dy = dya_ref[rows, :].astype(F32)
            sz = pz * sg
            du = dy * mixed * sz
            dmixed = dy * u * sz
            dz = dy * u * mixed * (sg * (1.0 + pz * (1.0 - sg)))
            dbsb_ref[...] += dmixed
            dmb = dmixed.astype(BF16)
            zero = jnp.zeros((CHUNK, LANES), BF16)
            parts = []
            for p in range(A_GROUPS // 2):
                dmp = dmb[:, LANES * p:LANES * (p + 1)]
                vp = vnb[:, LANES * p:LANES * (p + 1)]
                parts.append(jnp.where(low, _mm(wmt_ref[2 * p], dmp), _mm(wmt_ref[2 * p + 1], dmp)))
                dws_ref[2 * p] += _mm_nt(jnp.where(low, dmp, zero), vp)
                dws_ref[2 * p + 1] += _mm_nt(jnp.where(low, zero, dmp), vp)
            dvn = jnp.concatenate(parts, axis=1)
            dlg_ref[...] += jnp.sum(dvn * vhat, axis=0, keepdims=True)
            dlb_ref[...] += jnp.sum(dvn, axis=0, keepdims=True)
            dvh = dvn * ln_g
            dvv = rstd * (dvh - jnp.mean(dvh, axis=-1, keepdims=True)
                          - vhat * jnp.mean(dvh * vhat, axis=-1, keepdims=True))
            da_ref[rows, 0:A_WIDTH] = (du * _gelu_grad(pu)).astype(BF16)
            da_ref[rows, A_WIDTH:2 * A_WIDTH] = (dvv * _gelu_grad(pv)).astype(BF16)
            da_ref[rows, 2 * A_WIDTH:3 * A_WIDTH] = dz.astype(BF16)
            return carry

        lax.fori_loop(0, tc // CHUNK, chunk, 0)

        @pl.when(i == steps - 1)
        def _():
            causal = (lax.broadcasted_iota(jnp.int32, (CHUNK, CHUNK), 0)
                      >= lax.broadcasted_iota(jnp.int32, (CHUNK, CHUNK), 1))
            for h in range(A_GROUPS):
                dws_ref[h] = jnp.where(causal, dws_ref[h], 0.0)
            acc = dbsb_ref[...]
            lane_full = lax.broadcasted_iota(jnp.int32, (CHUNK, A_WIDTH), 1)
            lane_out = lax.broadcasted_iota(jnp.int32, (CHUNK, LANES), 1)
            out = jnp.zeros((CHUNK, LANES), F32)
            for h in range(A_GROUPS):
                in_group = jnp.logical_and(lane_full >= HEAD_DIM * h, lane_full < HEAD_DIM * (h + 1))
                s = jnp.sum(jnp.where(in_group, acc, 0.0), axis=-1, keepdims=True)
                out = jnp.where(lane_out == h, s, out)
            dbs_ref[...] = out

    blocks = [_nbytes((tc, PA_WIDTH), BF16), _nbytes((tc, A_WIDTH), BF16), 2 * _nbytes((A_GROUPS, CHUNK, CHUNK), BF16),
              _nbytes((CHUNK, A_WIDTH), F32), _nbytes((tc, PA_WIDTH), BF16), _nbytes((A_GROUPS, CHUNK, CHUNK), F32),
              _nbytes((CHUNK, LANES), F32)]
    return pl.pallas_call(
        body, name=f"gmlp_bwd_{layer}", grid=(steps,),
        in_specs=[pl.BlockSpec((tc, PA_WIDTH), lambda i: (i, 0)),
                  pl.BlockSpec((tc, A_WIDTH), lambda i: (i, 0)),
                  pl.BlockSpec((None, 1, A_WIDTH), lambda i: (layer, 0, 0)),
                  pl.BlockSpec((None, 1, A_WIDTH), lambda i: (layer, 0, 0)),
                  pl.BlockSpec((None, A_GROUPS, CHUNK, CHUNK), lambda i: (layer, 0, 0, 0)),
                  pl.BlockSpec((None, A_GROUPS, CHUNK, CHUNK), lambda i: (layer, 0, 0, 0)),
                  pl.BlockSpec((None, CHUNK, A_WIDTH), lambda i: (layer, 0, 0))],
        out_specs=[pl.BlockSpec((tc, PA_WIDTH), lambda i: (i, 0)),
                   pl.BlockSpec((A_GROUPS, CHUNK, CHUNK), lambda i: (0, 0, 0)),
                   pl.BlockSpec((CHUNK, LANES), lambda i: (0, 0)),
                   pl.BlockSpec((1, A_WIDTH), lambda i: (0, 0)),
                   pl.BlockSpec((1, A_WIDTH), lambda i: (0, 0))],
        out_shape=[jax.ShapeDtypeStruct((tokens, PA_WIDTH), BF16),
                   jax.ShapeDtypeStruct((A_GROUPS, CHUNK, CHUNK), F32),
                   jax.ShapeDtypeStruct((CHUNK, LANES), F32),
                   jax.ShapeDtypeStruct((1, A_WIDTH), F32),
                   jax.ShapeDtypeStruct((1, A_WIDTH), F32)],
        scratch_shapes=[pltpu.VMEM((CHUNK, A_WIDTH), F32)],
        compiler_params=pltpu.CompilerParams(dimension_semantics=("arbitrary",),
                                             vmem_limit_bytes=_vmem_limit(blocks, _nbytes((CHUNK, A_WIDTH), F32))),
    )(pa, dya, ln_g3, ln_b3, wm, wmt, bsb)


def _attn_bwd(pb, dyb, bias, sinks, b_loc, layer):
    tokens = pb.shape[0]
    nb = tokens // b_loc // BLOCK
    qz_width = 2 * B_WIDTH

    def body(sink_ref, cur_ref, prev_ref, dyb_ref, bias_ref, dqz_ref, dkv_ref, dbias_ref, dsink_ref, carry_ref):
        b = pl.program_id(0)
        n = pl.program_id(1)

        @pl.when(jnp.logical_and(b == 0, n == 0))
        def _():
            dbias_ref[...] = jnp.zeros_like(dbias_ref)
            dsink_ref[...] = jnp.zeros_like(dsink_ref)

        @pl.when(n == 0)
        def _():
            carry_ref[...] = jnp.zeros_like(carry_ref)

        @pl.when(n < nb)
        def _():
            cur = cur_ref[...]
            prev = prev_ref[...]
            k_ops = _kv_variants(prev, cur, K_OFF)
            v_ops = _kv_variants(prev, cur, V_OFF)
            col = lax.broadcasted_iota(jnp.int32, (BLOCK, 2 * BLOCK), 1)
            kill = jnp.logical_and(n == 0, col < BLOCK)
            low = lax.broadcasted_iota(jnp.int32, (BLOCK, LANES), 1) < HEAD_DIM
            low_kv = lax.broadcasted_iota(jnp.int32, (2 * BLOCK, LANES), 1) < HEAD_DIM
            dk_acc = [[jnp.zeros((2 * BLOCK, LANES), F32) for _ in range(2)] for _ in range(2)]
            dv_acc = [[jnp.zeros((2 * BLOCK, LANES), F32) for _ in range(2)] for _ in range(2)]
            for p in range(B_HEADS // 2):
                kvh = p // (B_HEADS // 4)
                qp = cur[:, LANES * p:LANES * (p + 1)]
                bz = cur[:, BZ_OFF + LANES * p:BZ_OFF + LANES * (p + 1)].astype(F32)
                sg = _sigmoid(bz)
                dyp = dyb_ref[:, LANES * p:LANES * (p + 1)].astype(F32)
                d_out = dyp * (bz * sg)
                d_out16 = d_out.astype(BF16)
                probs, sink_p, probs16 = [], [], []
                out = jnp.zeros((BLOCK, LANES), F32)
                for j in range(2):
                    hd = 2 * p + j
                    pr, sp = _attn_probs(qp, k_ops[kvh][j], bias_ref[hd], sink_ref[layer, hd], kill)
                    probs.append(pr)
                    sink_p.append(sp)
                    probs16.append(pr.astype(BF16))
                    out = out + _mm(probs16[j], v_ops[kvh][j])
                dqz_ref[:, B_WIDTH + LANES * p:B_WIDTH + LANES * (p + 1)] = (
                    dyp * out * (sg * (1.0 + bz * (1.0 - sg)))).astype(BF16)
                dod = d_out * out
                dq = jnp.zeros((BLOCK, LANES), F32)
                for j in range(2):
                    hd = 2 * p + j
                    half = jnp.where(low, dod, 0.0) if j == 0 else jnp.where(low, 0.0, dod)
                    delta = jnp.sum(half, axis=-1, keepdims=True)
                    dp = _mm_nt(d_out16, v_ops[kvh][j])
                    ds = probs[j] * (dp - delta)
                    dsink_ref[hd:hd + 1, :] += jnp.sum(-sink_p[j] * delta, axis=0, keepdims=True)
                    dbias_ref[hd] += ds
                    ds16 = ds.astype(BF16)
                    dq = dq + _mm(ds16, k_ops[kvh][j])
                    dk_acc[kvh][j] = dk_acc[kvh][j] + _mm_tn(ds16, qp)
                    dv_acc[kvh][j] = dv_acc[kvh][j] + _mm_tn(probs16[j], d_out16)
                dqz_ref[:, LANES * p:LANES * (p + 1)] = (dq * SCALE).astype(BF16)

            def fold(acc):
                return jnp.where(low_kv,
                                 acc[0][0] + pltpu.roll(acc[0][1], HEAD_DIM, 1),
                                 pltpu.roll(acc[1][0], HEAD_DIM, 1) + acc[1][1])

            dkv = jnp.concatenate([fold(dk_acc) * SCALE, fold(dv_acc)], axis=1)
            dkv_ref[...] = (carry_ref[...] + dkv[0:BLOCK, :]).astype(BF16)
            carry_ref[...] = dkv[BLOCK:2 * BLOCK, :]

        @pl.when(n == nb)
        def _():
            dkv_ref[...] = carry_ref[...].astype(BF16)

    def cur_map(b, n):
        return (b * nb + jnp.minimum(n, nb - 1), 0)

    def prev_map(b, n):
        return (b * nb + jnp.maximum(jnp.minimum(n, nb - 1) - 1, 0), 0)

    def kv_out_map(b, n):
        return (b * nb + jnp.maximum(n - 1, 0), 0)

    blocks = [2 * _nbytes((BLOCK, PB_WIDTH), BF16), _nbytes((BLOCK, B_WIDTH), BF16),
              2 * _nbytes((B_HEADS, BLOCK, 2 * BLOCK), F32), _nbytes((BLOCK, qz_width), BF16),
              _nbytes((BLOCK, 2 * KV_WIDTH), BF16), _nbytes((B_HEADS, LANES), F32)]
    return pl.pallas_call(
        body, name=f"attn_bwd_{layer}", grid=(b_loc, nb + 1),
        in_specs=[pl.BlockSpec(memory_space=pltpu.SMEM),
                  pl.BlockSpec((BLOCK, PB_WIDTH), cur_map),
                  pl.BlockSpec((BLOCK, PB_WIDTH), prev_map),
                  pl.BlockSpec((BLOCK, B_WIDTH), cur_map),
                  pl.BlockSpec((B_HEADS, BLOCK, 2 * BLOCK), lambda b, n: (0, 0, 0))],
        out_specs=[pl.BlockSpec((BLOCK, qz_width), cur_map),
                   pl.BlockSpec((BLOCK, 2 * KV_WIDTH), kv_out_map),
                   pl.BlockSpec((B_HEADS, BLOCK, 2 * BLOCK), lambda b, n: (0, 0, 0)),
                   pl.BlockSpec((B_HEADS, LANES), lambda b, n: (0, 0))],
        out_shape=[jax.ShapeDtypeStruct((tokens, qz_width), BF16),
                   jax.ShapeDtypeStruct((tokens, 2 * KV_WIDTH), BF16),
                   jax.ShapeDtypeStruct((B_HEADS, BLOCK, 2 * BLOCK), F32),
                   jax.ShapeDtypeStruct((B_HEADS, LANES), F32)],
        scratch_shapes=[pltpu.VMEM((BLOCK, 2 * KV_WIDTH), F32)],
        compiler_params=pltpu.CompilerParams(dimension_semantics=("arbitrary", "arbitrary"),
                                             vmem_limit_bytes=_vmem_limit(blocks, _nbytes((BLOCK, 2 * KV_WIDTH), F32))),
    )(sinks, pb, pb, dyb, bias)


def _inproj_bwd_dx(da, dqz, dkv, wa, wb, x2, g, pre_g3, layer):
    tokens = x2.shape[0]
    tm = _token_tile(tokens)

    def body(da_ref, dqz_ref, dkv_ref, wa_ref, wb_ref, x_ref, g_ref, pg_ref, gn_ref, dpg_ref):
        @pl.when(pl.program_id(0) == 0)
        def _():
            dpg_ref[...] = jnp.zeros_like(dpg_ref)

        dh = _mm_nt(da_ref[...], wa_ref[...])
        dh += _mm_nt(dqz_ref[:, 0:B_WIDTH], wb_ref[:, 0:B_WIDTH])
        dh += _mm_nt(dkv_ref[...], wb_ref[:, K_OFF:BZ_OFF])
        dh += _mm_nt(dqz_ref[:, B_WIDTH:2 * B_WIDTH], wb_ref[:, BZ_OFF:PB_WIDTH])
        x = x_ref[...]
        r = lax.rsqrt(jnp.mean(x * x, axis=-1, keepdims=True) + NORM_EPS)
        xhat = x * r
        dhg = dh * pg_ref[...]
        dpg_ref[...] += jnp.sum(dh * xhat, axis=0, keepdims=True)
        gn_ref[...] = g_ref[...] + r * (dhg - xhat * jnp.mean(dhg * xhat, axis=-1, keepdims=True))

    blocks = [_nbytes((tm, PA_WIDTH), BF16), _nbytes((tm, 2 * B_WIDTH), BF16), _nbytes((tm, 2 * KV_WIDTH), BF16),
              _nbytes((D_MODEL, IN_WIDTH), BF16), 3 * _nbytes((tm, D_MODEL), F32)]
    return pl.pallas_call(
        body, name=f"inproj_bwd_dx_{layer}", grid=(tokens // tm,),
        in_specs=[pl.BlockSpec((tm, PA_WIDTH), lambda i: (i, 0)),
                  pl.BlockSpec((tm, 2 * B_WIDTH), lambda i: (i, 0)),
                  pl.BlockSpec((tm, 2 * KV_WIDTH), lambda i: (i, 0)),
                  pl.BlockSpec((None, D_MODEL, PA_WIDTH), lambda i: (layer, 0, 0)),
                  pl.BlockSpec((None, D_MODEL, PB_WIDTH), lambda i: (layer, 0, 0)),
                  pl.BlockSpec((tm, D_MODEL), lambda i: (i, 0)),
                  pl.BlockSpec((tm, D_MODEL), lambda i: (i, 0)),
                  pl.BlockSpec((None, 1, D_MODEL), lambda i: (layer, 0, 0))],
        out_specs=[pl.BlockSpec((tm, D_MODEL), lambda i: (i, 0)),
                   pl.BlockSpec((1, D_MODEL), lambda i: (0, 0))],
        out_shape=[jax.ShapeDtypeStruct((tokens, D_MODEL), F32),
                   jax.ShapeDtypeStruct((1, D_MODEL), F32)],
        compiler_params=pltpu.CompilerParams(dimension_semantics=("arbitrary",),
                                             vmem_limit_bytes=_vmem_limit(blocks)),
    )(da, dqz, dkv, wa, wb, x2, g, pre_g3)


def _inproj_bwd_dw(h, da, dqz, dkv, layer):
    tokens = h.shape[0]
    tm = _token_tile(tokens)

    def body(h_ref, da_ref, dqz_ref, dkv_ref, dwa_ref, dwb_ref):
        @pl.when(pl.program_id(0) == 0)
        def _():
            dwa_ref[...] = jnp.zeros_like(dwa_ref)
            dwb_ref[...] = jnp.zeros_like(dwb_ref)

        hv = h_ref[...]
        dwa_ref[...] += _mm_tn(hv, da_ref[...])
        dwb_ref[:, 0:B_WIDTH] += _mm_tn(hv, dqz_ref[:, 0:B_WIDTH])
        dwb_ref[:, K_OFF:BZ_OFF] += _mm_tn(hv, dkv_ref[...])
        dwb_ref[:, BZ_OFF:PB_WIDTH] += _mm_tn(hv, dqz_ref[:, B_WIDTH:2 * B_WIDTH])

    blocks = [_nbytes((tm, D_MODEL), BF16), _nbytes((tm, PA_WIDTH), BF16), _nbytes((tm, 2 * B_WIDTH), BF16),
              _nbytes((tm, 2 * KV_WIDTH), BF16), _nbytes((D_MODEL, IN_WIDTH), F32)]
    return pl.pallas_call(
        body, name=f"inproj_bwd_dw_{layer}", grid=(tokens // tm,),
        in_specs=[pl.BlockSpec((tm, D_MODEL), lambda i: (i, 0)),
                  pl.BlockSpec((tm, PA_WIDTH), lambda i: (i, 0)),
                  pl.BlockSpec((tm, 2 * B_WIDTH), lambda i: (i, 0)),
                  pl.BlockSpec((tm, 2 * KV_WIDTH), lambda i: (i, 0))],
        out_specs=[pl.BlockSpec((D_MODEL, PA_WIDTH), lambda i: (0, 0)),
                   pl.BlockSpec((D_MODEL, PB_WIDTH), lambda i: (0, 0))],
        out_shape=[jax.ShapeDtypeStruct((D_MODEL, PA_WIDTH), F32),
                   jax.ShapeDtypeStruct((D_MODEL, PB_WIDTH), F32)],
        compiler_params=pltpu.CompilerParams(dimension_semantics=("arbitrary",),
                                             vmem_limit_bytes=_vmem_limit(blocks)),
    )(h, da, dqz, dkv)


def _rel_bias_grad(dbias_layers, bucket):
    n = len(dbias_layers)

    def body(*refs):
        bucket_ref, out_ref = refs[n], refs[n + 1]
        bk = bucket_ref[...]
        row = lax.broadcasted_iota(jnp.int32, (REL_BUCKETS, LANES), 0)
        lane = lax.broadcasted_iota(jnp.int32, (REL_BUCKETS, LANES), 1)
        out = jnp.zeros((REL_BUCKETS, LANES), F32)
        for h in range(B_HEADS):
            tot = refs[0][h]
            for a in range(1, n):
                tot = tot + refs[a][h]

            def one(b, acc, tot=tot, h=h):
                s = jnp.sum(jnp.where(bk == b, tot, 0.0), axis=-1, keepdims=True)
                s = jnp.sum(s, axis=0, keepdims=True)
                return jnp.where(jnp.logical_and(row == b, lane == h), s, acc)

            out = lax.fori_loop(0, REL_BUCKETS, one, out)
        out_ref[...] = out

    vm = pl.BlockSpec(memory_space=pltpu.VMEM)
    return pl.pallas_call(
        body, name="rel_bias_grad",
        out_shape=jax.ShapeDtypeStruct((REL_BUCKETS, LANES), F32),
        in_specs=[vm] * (n + 1), out_specs=vm,
    )(*dbias_layers, bucket)


def _reduce_adamw(slots, w, m, v, name):
    rows, cols = w.shape
    tr = _row_tile(rows)
    c1 = 1.0 / (1.0 - ADAM_B1 ** ADAM_STEP)
    c2 = 1.0 / (1.0 - ADAM_B2 ** ADAM_STEP)

    def body(s_ref, w_ref, m_ref, v_ref, g_ref, d_ref, nm_ref, nv_ref):
        g = s_ref[0].astype(F32)
        for i in range(1, N_DEV):
            g = g + s_ref[i].astype(F32)
        nm = ADAM_B1 * m_ref[...] + (1.0 - ADAM_B1) * g
        nv = ADAM_B2 * v_ref[...] + (1.0 - ADAM_B2) * (g * g)
        g_ref[...] = g
        nm_ref[...] = nm
        nv_ref[...] = nv
        d_ref[...] = -ADAM_LR * ((nm * c1) / (jnp.sqrt(nv * c2) + ADAM_EPS) + ADAM_WD * w_ref[...])

    blocks = [_nbytes((N_DEV, tr, cols), slots.dtype), 7 * _nbytes((tr, cols), F32)]
    tile = pl.BlockSpec((tr, cols), lambda i: (i, 0))
    return pl.pallas_call(
        body, name=name, grid=(rows // tr,),
        in_specs=[pl.BlockSpec((N_DEV, tr, cols), lambda i: (0, i, 0)), tile, tile, tile],
        out_specs=[tile] * 4,
        out_shape=[jax.ShapeDtypeStruct((rows, cols), F32)] * 4,
        compiler_params=pltpu.CompilerParams(dimension_semantics=("parallel",),
                                             vmem_limit_bytes=_vmem_limit(blocks)),
    )(slots, w, m, v)


_SMALL = ("w_spatial", "pre_norm_g", "post_norm_g", "ln_v_g", "ln_v_b", "b_spatial", "sinks", "rel_bias")


def _pack_small(parts):
    slabs = []
    for name in _SMALL:
        flat = parts[name].astype(F32).reshape(-1)
        pad = (-flat.shape[0]) % (8 * LANES)
        slabs.append(jnp.pad(flat, (0, pad)).reshape(-1, LANES))
    return jnp.concatenate(slabs, axis=0)


def _unpack_small(slab, shapes):
    out, row = {}, 0
    for name in _SMALL:
        size = int(np.prod(shapes[name]))
        rows = -(-size // (8 * LANES)) * 8
        out[name] = slab[row:row + rows].reshape(-1)[:size].reshape(shapes[name])
        row += rows
    return out


def kernel(x, pre_norm_g, w_in, ln_v_g, ln_v_b, w_spatial, b_spatial, sinks, rel_bias, w_out, post_norm_g, loss_target, m_pre_norm_g, m_w_in, m_ln_v_g, m_ln_v_b, m_w_spatial, m_b_spatial, m_sinks, m_rel_bias, m_w_out, m_post_norm_g, v_pre_norm_g, v_w_in, v_ln_v_g, v_ln_v_b, v_w_spatial, v_b_spatial, v_sinks, v_rel_bias, v_w_out, v_post_norm_g):
    b_loc, seq, _ = x.shape
    tokens = b_loc * seq
    depth = w_in.shape[0]
    in_shard = w_in.shape[2]
    out_shard = w_out.shape[1]
    assert in_shard * N_DEV == IN_WIDTH and out_shard * N_DEV == D_MODEL and seq % BLOCK == 0

    gin, gout = _all_gather([w_in.astype(BF16).reshape(depth * D_MODEL, in_shard),
                             w_out.astype(BF16).reshape(depth * out_shard, D_MODEL)], "weights_all_gather")
    w_full = gin.reshape(N_DEV, depth, D_MODEL, in_shard).transpose(1, 2, 0, 3).reshape(depth, D_MODEL, IN_WIDTH)
    wa, wb = w_full[:, :, :PA_WIDTH], w_full[:, :, PA_WIDTH:]
    wo = gout.reshape(N_DEV, depth, out_shard, D_MODEL).transpose(1, 0, 2, 3).reshape(depth, D_MODEL, D_MODEL)

    causal = jnp.tril(jnp.ones((CHUNK, CHUNK), dtype=bool))
    wm = jnp.where(causal, w_spatial, 0.0).astype(BF16)
    wmt = jnp.swapaxes(wm, -1, -2)
    bsb = jnp.repeat(jnp.swapaxes(b_spatial, -1, -2), HEAD_DIM, axis=-1)
    pre_g3 = pre_norm_g.reshape(depth, 1, D_MODEL)
    post_g3 = post_norm_g.reshape(depth, 1, D_MODEL)
    ln_g3 = ln_v_g.reshape(depth, 1, A_WIDTH)
    ln_b3 = ln_v_b.reshape(depth, 1, A_WIDTH)
    bucket = jnp.asarray(_bucket_table())
    bias = _bias_table(rel_bias, bucket)

    xs, saved = [x.reshape(tokens, D_MODEL)], []
    for layer in range(depth):
        h, pa, pb = _inproj_fwd(xs[-1], pre_g3, wa, wb, layer)
        ya = _gmlp_fwd(pa, ln_g3, ln_b3, wm, bsb, layer)
        yb = _attn_fwd(pb, bias, sinks, b_loc, layer)
        y, x_next = _outproj_fwd(ya, yb, wo, xs[-1], post_g3, layer)
        saved.append((h, pa, pb, ya, yb, y))
        xs.append(x_next)
    g, loss_part = _loss_head(xs[-1], loss_target.reshape(tokens, D_MODEL))
    loss = lax.psum(loss_part[0, 0], ("x", "y", "c"))

    grads = {name: [None] * depth for name in ("w_spatial", "pre_norm_g", "post_norm_g", "ln_v_g", "ln_v_b",
                                               "b_spatial", "sinks", "wa", "wb", "wo", "dbias")}
    for layer in reversed(range(depth)):
        h, pa, pb, ya, yb, y = saved[layer]
        dya, dyb, grads["wo"][layer], dpost = _outproj_bwd(g, y, ya, yb, wo, post_g3, layer)
        da, dws, dbs, dlg, dlb = _gmlp_bwd(pa, dya, ln_g3, ln_b3, wm, wmt, bsb, layer)
        dqz, dkv, grads["dbias"][layer], dsink = _attn_bwd(pb, dyb, bias, sinks, b_loc, layer)
        g, dpre = _inproj_bwd_dx(da, dqz, dkv, wa, wb, xs[layer], g, pre_g3, layer)
        grads["wa"][layer], grads["wb"][layer] = _inproj_bwd_dw(h, da, dqz, dkv, layer)
        grads["w_spatial"][layer] = dws
        grads["b_spatial"][layer] = dbs[:, :A_GROUPS].T
        grads["ln_v_g"][layer] = dlg[0]
        grads["ln_v_b"][layer] = dlb[0]
        grads["sinks"][layer] = dsink[:, 0]
        grads["pre_norm_g"][layer] = dpre[0]
        grads["post_norm_g"][layer] = dpost[0]
    grad_x = g.reshape(x.shape)
    drel = _rel_bias_grad(grads["dbias"], bucket)[:, :B_HEADS]

    dw_in = jnp.concatenate([jnp.stack(grads["wa"]), jnp.stack(grads["wb"])], axis=-1)
    send_in = dw_in.astype(BF16).reshape(depth, D_MODEL, N_DEV, in_shard).transpose(2, 0, 1, 3)
    send_out = jnp.stack(grads["wo"]).astype(BF16).reshape(depth, N_DEV, out_shard, D_MODEL).transpose(1, 0, 2, 3)
    recv_in, recv_out = _all_to_all([send_in.reshape(N_DEV, depth * D_MODEL, in_shard),
                                     send_out.reshape(N_DEV, depth * out_shard, D_MODEL)], "grads_all_to_all")
    res_in = _reduce_adamw(recv_in, w_in.reshape(-1, in_shard), m_w_in.reshape(-1, in_shard),
                           v_w_in.reshape(-1, in_shard), "adamw_w_in")
    res_out = _reduce_adamw(recv_out, w_out.reshape(-1, D_MODEL), m_w_out.reshape(-1, D_MODEL),
                            v_w_out.reshape(-1, D_MODEL), "adamw_w_out")
    res_in = [r.reshape(w_in.shape) for r in res_in]
    res_out = [r.reshape(w_out.shape) for r in res_out]

    small_w = dict(w_spatial=w_spatial, pre_norm_g=pre_norm_g, post_norm_g=post_norm_g, ln_v_g=ln_v_g,
                   ln_v_b=ln_v_b, b_spatial=b_spatial, sinks=sinks, rel_bias=rel_bias)
    small_m = dict(w_spatial=m_w_spatial, pre_norm_g=m_pre_norm_g, post_norm_g=m_post_norm_g, ln_v_g=m_ln_v_g,
                   ln_v_b=m_ln_v_b, b_spatial=m_b_spatial, sinks=m_sinks, rel_bias=m_rel_bias)
    small_v = dict(w_spatial=v_w_spatial, pre_norm_g=v_pre_norm_g, post_norm_g=v_post_norm_g, ln_v_g=v_ln_v_g,
                   ln_v_b=v_ln_v_b, b_spatial=v_b_spatial, sinks=v_sinks, rel_bias=v_rel_bias)
    small_g = {name: jnp.stack(grads[name]) for name in _SMALL if name != "rel_bias"}
    small_g["rel_bias"] = drel
    shapes = {name: small_w[name].shape for name in _SMALL}
    (slots,) = _all_gather([_pack_small(small_g)], "small_grads_all_gather")
    res_small = _reduce_adamw(slots, _pack_small(small_w), _pack_small(small_m), _pack_small(small_v), "adamw_small")
    res_small = [_unpack_small(r, shapes) for r in res_small]

    order = ("pre_norm_g", "w_in", "ln_v_g", "ln_v_b", "w_spatial", "b_spatial", "sinks", "rel_bias", "w_out",
             "post_norm_g")
    outs = [loss, grad_x]
    for kind in range(4):
        for name in order:
            if name == "w_in":
                outs.append(res_in[kind])
            elif name == "w_out":
                outs.append(res_out[kind])
            else:
                outs.append(res_small[kind][name])
    return tuple(outs)
```

```python
import math

import numpy as np
import jax
import jax.numpy as jnp
from jax import lax
from jax.experimental import pallas as pl
from jax.experimental.pallas import tpu as pltpu

F32 = jnp.float32
BF16 = jnp.bfloat16

D_MODEL = 1024
A_WIDTH = 512
A_GROUPS = 8
CHUNK = 128
B_HEADS = 8
HEAD_DIM = 64
B_WIDTH = 512
KV_WIDTH = 128
BLOCK = 128
REL_BUCKETS = 32
REL_MAX_DIST = 128
NORM_EPS = 1e-6
PA_WIDTH = 3 * A_WIDTH
PB_WIDTH = 2 * B_WIDTH + 2 * KV_WIDTH
IN_WIDTH = PA_WIDTH + PB_WIDTH
K_OFF, V_OFF, BZ_OFF = B_WIDTH, B_WIDTH + KV_WIDTH, B_WIDTH + 2 * KV_WIDTH
SCALE = HEAD_DIM ** -0.5
NEG = -1e30
N_DEV = 8
LANES = 128

ADAM_LR = 0.001
ADAM_B1 = 0.9
ADAM_B2 = 0.999
ADAM_EPS = 1e-08
ADAM_WD = 0.01
ADAM_STEP = 10

V7X_VMEM_BYTES = 64 * 1024 * 1024
VMEM_TEMP_BYTES = 12 * 1024 * 1024
MESH = pl.DeviceIdType.MESH


def _vmem_limit(block_bytes, scratch_bytes=0):
    need = 2 * sum(block_bytes) + scratch_bytes + VMEM_TEMP_BYTES
    return int(min(need, V7X_VMEM_BYTES - 8 * 1024 * 1024))


def _nbytes(shape, dtype):
    return int(np.prod(shape)) * jnp.dtype(dtype).itemsize


def _token_tile(tokens):
    tile = min(512, tokens // 2)
    assert tokens % tile == 0 and tile % CHUNK == 0, tokens
    return tile


def _row_tile(rows, cap=512):
    best = 8
    for t in range(8, cap + 1, 8):
        if rows % t == 0:
            best = t
    assert rows % best == 0, rows
    return best


def _mm(a, b):
    return lax.dot_general(a, b, (((1,), (0,)), ((), ())), preferred_element_type=F32)


def _mm_nt(a, b):
    return lax.dot_general(a, b, (((1,), (1,)), ((), ())), preferred_element_type=F32)


def _mm_tn(a, b):
    return lax.dot_general(a, b, (((0,), (0,)), ((), ())), preferred_element_type=F32)


_GELU_C = math.sqrt(2.0 / math.pi)


def _gelu(x):
    return 0.5 * x * (1.0 + jnp.tanh(_GELU_C * (x + 0.044715 * (x * x * x))))


def _gelu_grad(x):
    t = jnp.tanh(_GELU_C * (x + 0.044715 * (x * x * x)))
    return 0.5 * (1.0 + t) + 0.5 * x * (1.0 - t * t) * (_GELU_C * (1.0 + 3.0 * 0.044715 * (x * x)))


def _sigmoid(x):
    return 1.0 / (1.0 + jnp.exp(-x))


def _bucket_table():
    q = np.arange(BLOCK)[:, None]
    k = np.arange(BLOCK)[None, :]
    dist = np.where(k <= q, q - k, q + BLOCK - k)
    max_exact = REL_BUCKETS // 2
    safe = np.maximum(dist, 1).astype(np.float32)
    large = max_exact + (np.log(safe / np.float32(max_exact)) / np.float32(math.log(REL_MAX_DIST / max_exact))
                         * np.float32(REL_BUCKETS - max_exact)).astype(np.int32)
    large = np.minimum(large, REL_BUCKETS - 1)
    assert dist.min() >= 0 and dist.max() < BLOCK
    return np.where(dist < max_exact, dist, large).astype(np.int32)


def _hbm(x):
    return pltpu.with_memory_space_constraint(x, pltpu.HBM)


def _slot(px, py, pc):
    return 4 * px + 2 * py + pc


def _all_gather(arrs, name):
    n = len(arrs)

    def body(*refs):
        ins, outs = refs[:n], refs[n:2 * n]
        send_sems, recv_sems, local_sems = refs[2 * n:]
        x, y, c = lax.axis_index("x"), lax.axis_index("y"), lax.axis_index("c")
        me, sibling = (x, y, c), (x, y, 1 - c)
        chips = [(1 - x, y), (x, 1 - y), (1 - x, 1 - y)]

        def copy(a, k, block, to, src=None):
            dst = outs[a].at[_slot(*block)]
            return pltpu.make_async_remote_copy(
                src_ref=dst if src is None else src, dst_ref=dst,
                send_sem=send_sems.at[7 * a + k], recv_sem=recv_sems.at[7 * a + k],
                device_id=to, device_id_type=MESH)

        mine = [pltpu.make_async_copy(ins[a], outs[a].at[_slot(*me)], local_sems.at[a]) for a in range(n)]
        for cp in mine:
            cp.start()
        first = []
        for a in range(n):
            first.append(copy(a, 0, me, sibling, src=ins[a]))
            first += [copy(a, 1 + j, me, (*chip, c), src=ins[a]) for j, chip in enumerate(chips)]
        for cp in first:
            cp.start()
        passed = []
        for j, chip in enumerate(chips):
            for a in range(n):
                copy(a, 1 + j, (*chip, c), me).wait_recv()
                fwd = copy(a, 4 + j, (*chip, c), sibling)
                fwd.start()
                passed.append(fwd)
        for a in range(n):
            copy(a, 0, sibling, me).wait_recv()
            for j, chip in enumerate(chips):
                copy(a, 4 + j, (*chip, 1 - c), me).wait_recv()
        for cp in first + passed:
            cp.wait_send()
        for cp in mine:
            cp.wait()

    any_spec = pl.BlockSpec(memory_space=pl.ANY)
    return pl.pallas_call(
        body, name=name,
        out_shape=[jax.ShapeDtypeStruct((N_DEV,) + a.shape, a.dtype) for a in arrs],
        in_specs=[any_spec] * n, out_specs=[any_spec] * n,
        scratch_shapes=[pltpu.SemaphoreType.DMA((7 * n,)), pltpu.SemaphoreType.DMA((7 * n,)),
                        pltpu.SemaphoreType.DMA((n,))],
    )(*arrs)


def _all_to_all(arrs, name):
    n = len(arrs)

    def body(*refs):
        ins, outs = refs[:n], refs[n:2 * n]
        send_sems, recv_sems, local_sems = refs[2 * n:]
        x, y, c = lax.axis_index("x"), lax.axis_index("y"), lax.axis_index("c")
        me = (x, y, c)
        peers = []
        for k in range(1, N_DEV):
            fx, fy, fc = (k >> 2) & 1, (k >> 1) & 1, k & 1
            peers.append((1 - x if fx else x, 1 - y if fy else y, 1 - c if fc else c))

        def copy(a, k, peer):
            return pltpu.make_async_remote_copy(
                src_ref=ins[a].at[_slot(*peer)], dst_ref=outs[a].at[_slot(*me)],
                send_sem=send_sems.at[7 * a + k], recv_sem=recv_sems.at[7 * a + k],
                device_id=peer, device_id_type=MESH)

        def arrival(a, k, peer):
            return pltpu.make_async_remote_copy(
                src_ref=ins[a].at[_slot(*peer)], dst_ref=outs[a].at[_slot(*peer)],
                send_sem=send_sems.at[7 * a + k], recv_sem=recv_sems.at[7 * a + k],
                device_id=peer, device_id_type=MESH)

        mine = [pltpu.make_async_copy(ins[a].at[_slot(*me)], outs[a].at[_slot(*me)], local_sems.at[a])
                for a in range(n)]
        for cp in mine:
            cp.start()
        sends = [copy(a, k, peer) for a in range(n) for k, peer in enumerate(peers)]
        for cp in sends:
            cp.start()
        for a in range(n):
            for k, peer in enumerate(peers):
                arrival(a, k, peer).wait_recv()
        for cp in sends:
            cp.wait_send()
        for cp in mine:
            cp.wait()

    any_spec = pl.BlockSpec(memory_space=pl.ANY)
    return pl.pallas_call(
        body, name=name,
        out_shape=[jax.ShapeDtypeStruct(a.shape, a.dtype) for a in arrs],
        in_specs=[any_spec] * n, out_specs=[any_spec] * n,
        scratch_shapes=[pltpu.SemaphoreType.DMA((7 * n,)), pltpu.SemaphoreType.DMA((7 * n,)),
                        pltpu.SemaphoreType.DMA((n,))],
    )(*arrs)


def _bias_table(rel_bias, bucket):
    def body(rel_ref, bucket_ref, out_ref):
        bk = bucket_ref[...]
        for h in range(B_HEADS):
            def pick(b, acc, h=h):
                return jnp.where(bk == b, rel_ref[b, h], acc)
            out_ref[h] = lax.fori_loop(0, REL_BUCKETS, pick, jnp.zeros((BLOCK, BLOCK), F32))

    return pl.pallas_call(
        body, name="bias_table",
        out_shape=jax.ShapeDtypeStruct((B_HEADS, BLOCK, BLOCK), F32),
        in_specs=[pl.BlockSpec(memory_space=pltpu.SMEM), pl.BlockSpec(memory_space=pltpu.VMEM)],
        out_specs=pl.BlockSpec(memory_space=pltpu.VMEM),
    )(rel_bias, bucket)


def _inproj_fwd(x2, pre_g3, wa, wb, layer):
    tokens = x2.shape[0]
    tm = _token_tile(tokens)

    def body(x_ref, g_ref, wa_ref, wb_ref, h_ref, pa_ref, pb_ref):
        x = x_ref[...]
        r = lax.rsqrt(jnp.mean(x * x, axis=-1, keepdims=True) + NORM_EPS)
        h = (x * r * g_ref[...]).astype(BF16)
        h_ref[...] = h
        pa_ref[...] = _mm(h, wa_ref[...]).astype(BF16)
        pb_ref[...] = _mm(h, wb_ref[...]).astype(BF16)

    blocks = [_nbytes((tm, D_MODEL), F32), _nbytes((D_MODEL, PA_WIDTH), BF16), _nbytes((D_MODEL, PB_WIDTH), BF16),
              _nbytes((tm, D_MODEL), BF16), _nbytes((tm, PA_WIDTH), BF16), _nbytes((tm, PB_WIDTH), BF16)]
    return pl.pallas_call(
        body, name=f"inproj_fwd_{layer}", grid=(tokens // tm,),
        in_specs=[pl.BlockSpec((tm, D_MODEL), lambda i: (i, 0)),
                  pl.BlockSpec((None, 1, D_MODEL), lambda i: (layer, 0, 0)),
                  pl.BlockSpec((None, D_MODEL, PA_WIDTH), lambda i: (layer, 0, 0)),
                  pl.BlockSpec((None, D_MODEL, PB_WIDTH), lambda i: (layer, 0, 0))],
        out_specs=[pl.BlockSpec((tm, D_MODEL), lambda i: (i, 0)),
                   pl.BlockSpec((tm, PA_WIDTH), lambda i: (i, 0)),
                   pl.BlockSpec((tm, PB_WIDTH), lambda i: (i, 0))],
        out_shape=[pltpu.HBM((tokens,D_MODEL), BF16),
                   pltpu.HBM((tokens,PA_WIDTH), BF16),
                   pltpu.HBM((tokens,PB_WIDTH), BF16)],
        compiler_params=pltpu.CompilerParams(dimension_semantics=("parallel",),
                                             vmem_limit_bytes=_vmem_limit(blocks)),
    )(_hbm(x2), pre_g3, _hbm(wa), _hbm(wb))


def _gmlp_forward_chunk(pa, ln_g, ln_b, wm_ref, bsb):
    pu = pa[:, 0:A_WIDTH].astype(F32)
    pv = pa[:, A_WIDTH:2 * A_WIDTH].astype(F32)
    pz = pa[:, 2 * A_WIDTH:3 * A_WIDTH].astype(F32)
    u = _gelu(pu)
    vv = _gelu(pv)
    mu = jnp.mean(vv, axis=-1, keepdims=True)
    xc = vv - mu
    rstd = lax.rsqrt(jnp.mean(xc * xc, axis=-1, keepdims=True) + NORM_EPS)
    vhat = xc * rstd
    vnb = (vhat * ln_g + ln_b).astype(BF16)
    low = lax.broadcasted_iota(jnp.int32, (CHUNK, LANES), 1) < HEAD_DIM
    parts = []
    for p in range(A_GROUPS // 2):
        vp = vnb[:, LANES * p:LANES * (p + 1)]
        parts.append(jnp.where(low, _mm(wm_ref[2 * p], vp), _mm(wm_ref[2 * p + 1], vp)))
    mixed = jnp.concatenate(parts, axis=1) + bsb
    sg = _sigmoid(pz)
    return pu, pv, pz, u, vhat, rstd, vnb, mixed, sg


def _gmlp_fwd(pa, ln_g3, ln_b3, wm, bsb, layer):
    tokens = pa.shape[0]
    tc = _token_tile(tokens)

    def body(pa_ref, lg_ref, lb_ref, wm_ref, bsb_ref, ya_ref):
        def chunk(ci, carry):
            rows = pl.ds(pl.multiple_of(ci * CHUNK, CHUNK), CHUNK)
            _, _, pz, u, _, _, _, mixed, sg = _gmlp_forward_chunk(
                pa_ref[rows, :], lg_ref[...], lb_ref[...], wm_ref, bsb_ref[...])
            ya_ref[rows, :] = (u * mixed * (pz * sg)).astype(BF16)
            return carry
        lax.fori_loop(0, tc // CHUNK, chunk, 0)

    blocks = [_nbytes((tc, PA_WIDTH), BF16), _nbytes((A_GROUPS, CHUNK, CHUNK), BF16),
              _nbytes((CHUNK, A_WIDTH), F32), _nbytes((tc, A_WIDTH), BF16)]
    return pl.pallas_call(
        body, name=f"gmlp_fwd_{layer}", grid=(tokens // tc,),
        in_specs=[pl.BlockSpec((tc, PA_WIDTH), lambda i: (i, 0)),
                  pl.BlockSpec((None, 1, A_WIDTH), lambda i: (layer, 0, 0)),
                  pl.BlockSpec((None, 1, A_WIDTH), lambda i: (layer, 0, 0)),
                  pl.BlockSpec((None, A_GROUPS, CHUNK, CHUNK), lambda i: (layer, 0, 0, 0)),
                  pl.BlockSpec((None, CHUNK, A_WIDTH), lambda i: (layer, 0, 0))],
        out_specs=pl.BlockSpec((tc, A_WIDTH), lambda i: (i, 0)),
        out_shape=pltpu.HBM((tokens,A_WIDTH), BF16),
        compiler_params=pltpu.CompilerParams(dimension_semantics=("parallel",),
                                             vmem_limit_bytes=_vmem_limit(blocks)),
    )(_hbm(pa), ln_g3, ln_b3, wm, bsb)


def _kv_variants(kv):
    t = kv.astype(F32)
    rolled = pltpu.roll(t, HEAD_DIM, 1)
    low = lax.broadcasted_iota(jnp.int32, t.shape, 1) < HEAD_DIM
    zero = jnp.zeros_like(t)
    head0 = (jnp.where(low, t, zero).astype(BF16), jnp.where(low, zero, rolled).astype(BF16))
    head1 = (jnp.where(low, rolled, zero).astype(BF16), jnp.where(low, zero, t).astype(BF16))
    return head0, head1


def _band_masks():
    row = lax.broadcasted_iota(jnp.int32, (BLOCK, BLOCK), 0)
    col = lax.broadcasted_iota(jnp.int32, (BLOCK, BLOCK), 1)
    return col <= row


def _wrap(full, tri):
    return jnp.where(tri, full[:, BLOCK:2 * BLOCK], full[:, 0:BLOCK])


def _attn_probs(sf, bias_h, sink, tri, kill):
    s = _wrap(sf, tri) * SCALE + bias_h
    s = jnp.where(kill, NEG, s)
    m = jnp.maximum(jnp.max(s, axis=-1, keepdims=True), sink)
    e = jnp.exp(s - m)
    es = jnp.exp(sink - m)
    inv = 1.0 / (jnp.sum(e, axis=-1, keepdims=True) + es)
    return e * inv, es * inv


def _unwrap16(p, tri):
    zero = jnp.zeros_like(p)
    return jnp.concatenate([jnp.where(tri, zero, p), jnp.where(tri, p, zero)], axis=1).astype(BF16)


def _fill_kv(kv_ref, prev_ref, cur_ref):
    kv_ref[0:BLOCK, :] = prev_ref[...]
    kv_ref[BLOCK:, :] = cur_ref[:, K_OFF:K_OFF + 2 * KV_WIDTH]


def _attn_fwd(pb, bias, sinks, b_loc, layer):
    tokens = pb.shape[0]
    nb = tokens // b_loc // BLOCK
    tq = _token_tile(tokens)
    per_tile = tq // BLOCK

    def body(sink_ref, cur_ref, prev_ref, bias_ref, yb_ref, kv_ref):
        t = pl.program_id(0)
        _fill_kv(kv_ref, prev_ref, cur_ref)
        tri = _band_masks()

        def block(i, carry):
            start = pl.multiple_of(i * BLOCK, BLOCK)
            rows = pl.ds(start, BLOCK)
            first = lax.rem(t * per_tile + i, nb) == 0
            kill = jnp.logical_and(first, jnp.logical_not(tri))
            kv = kv_ref[pl.ds(start, 2 * BLOCK), :]
            k_ops = _kv_variants(kv[:, 0:KV_WIDTH])
            v_ops = _kv_variants(kv[:, KV_WIDTH:2 * KV_WIDTH])
            p16 = {}
            for kvh in range(2):
                pairs = (2 * kvh, 2 * kvh + 1)
                qs = jnp.concatenate([cur_ref[rows, LANES * p:LANES * (p + 1)] for p in pairs], axis=0)
                for j in range(2):
                    sf = _mm_nt(qs, k_ops[kvh][j])
                    for r, p in enumerate(pairs):
                        hd = 2 * p + j
                        probs, _ = _attn_probs(sf[BLOCK * r:BLOCK * (r + 1)], bias_ref[hd], sink_ref[layer, hd],
                                               tri, kill)
                        p16[hd] = _unwrap16(probs, tri)
            for kvh in range(2):
                pairs = (2 * kvh, 2 * kvh + 1)
                out = jnp.zeros((2 * BLOCK, LANES), F32)
                for j in range(2):
                    out = out + _mm(jnp.concatenate([p16[2 * p + j] for p in pairs], axis=0), v_ops[kvh][j])
                for r, p in enumerate(pairs):
                    bz = cur_ref[rows, BZ_OFF + LANES * p:BZ_OFF + LANES * (p + 1)].astype(F32)
                    yb_ref[rows, LANES * p:LANES * (p + 1)] = (
                        out[BLOCK * r:BLOCK * (r + 1)] * (bz * _sigmoid(bz))).astype(BF16)
            return carry

        lax.fori_loop(0, per_tile, block, 0)

    blocks = [_nbytes((tq, PB_WIDTH), BF16), _nbytes((BLOCK, 2 * KV_WIDTH), BF16),
              _nbytes((B_HEADS, BLOCK, BLOCK), F32), _nbytes((tq, B_WIDTH), BF16)]
    scratch = _nbytes((tq + BLOCK, 2 * KV_WIDTH), BF16)
    return pl.pallas_call(
        body, name=f"attn_fwd_{layer}", grid=(tokens // tq,),
        in_specs=[pl.BlockSpec(memory_space=pltpu.SMEM),
                  pl.BlockSpec((tq, PB_WIDTH), lambda t: (t, 0)),
                  pl.BlockSpec((BLOCK, 2 * KV_WIDTH),
                               lambda t: (jnp.maximum(t * per_tile - 1, 0), K_OFF // (2 * KV_WIDTH))),
                  pl.BlockSpec((B_HEADS, BLOCK, BLOCK), lambda t: (0, 0, 0))],
        out_specs=pl.BlockSpec((tq, B_WIDTH), lambda t: (t, 0)),
        out_shape=pltpu.HBM((tokens, B_WIDTH), BF16),
        scratch_shapes=[pltpu.VMEM((tq + BLOCK, 2 * KV_WIDTH), BF16)],
        compiler_params=pltpu.CompilerParams(dimension_semantics=("parallel",),
                                             vmem_limit_bytes=_vmem_limit(blocks, scratch)),
    )(sinks, _hbm(pb), _hbm(pb), _hbm(bias))


def _outproj_fwd(ya, yb, wo, x2, post_g3, layer):
    tokens = x2.shape[0]
    tm = _token_tile(tokens)

    def body(ya_ref, yb_ref, woa_ref, wob_ref, x_ref, g_ref, y_ref, xn_ref):
        y = _mm(ya_ref[...], woa_ref[...]) + _mm(yb_ref[...], wob_ref[...])
        r = lax.rsqrt(jnp.mean(y * y, axis=-1, keepdims=True) + NORM_EPS)
        y_ref[...] = y.astype(BF16)
        xn_ref[...] = x_ref[...] + y * r * g_ref[...]

    half = D_MODEL // 2
    blocks = [2 * _nbytes((tm, half), BF16), 2 * _nbytes((half, D_MODEL), BF16), 2 * _nbytes((tm, D_MODEL), F32),
              _nbytes((tm, D_MODEL), BF16)]
    return pl.pallas_call(
        body, name=f"outproj_fwd_{layer}", grid=(tokens // tm,),
        in_specs=[pl.BlockSpec((tm, half), lambda i: (i, 0)),
                  pl.BlockSpec((tm, half), lambda i: (i, 0)),
                  pl.BlockSpec((None, half, D_MODEL), lambda i: (layer, 0, 0)),
                  pl.BlockSpec((None, half, D_MODEL), lambda i: (layer, 1, 0)),
                  pl.BlockSpec((tm, D_MODEL), lambda i: (i, 0)),
                  pl.BlockSpec((None, 1, D_MODEL), lambda i: (layer, 0, 0))],
        out_specs=[pl.BlockSpec((tm, D_MODEL), lambda i: (i, 0)),
                   pl.BlockSpec((tm, D_MODEL), lambda i: (i, 0))],
        out_shape=[pltpu.HBM((tokens,D_MODEL), BF16),
                   pltpu.HBM((tokens,D_MODEL), F32)],
        compiler_params=pltpu.CompilerParams(dimension_semantics=("parallel",),
                                             vmem_limit_bytes=_vmem_limit(blocks)),
    )(_hbm(ya), _hbm(yb), _hbm(wo), _hbm(wo), _hbm(x2), post_g3)


def _loss_head(xl, tgt):
    tokens = xl.shape[0]
    tm = _token_tile(tokens)
    steps = tokens // tm

    def body(x_ref, t_ref, g_ref, loss_ref, acc_ref):
        i = pl.program_id(0)

        @pl.when(i == 0)
        def _():
            acc_ref[...] = jnp.zeros_like(acc_ref)

        err = x_ref[...] - t_ref[...]
        g_ref[...] = err * (1.0 / D_MODEL)
        acc_ref[...] += jnp.sum(err * err, axis=0, keepdims=True)

        @pl.when(i == steps - 1)
        def _():
            total = jnp.sum(acc_ref[...], axis=-1, keepdims=True) * (0.5 / D_MODEL)
            loss_ref[...] = jnp.broadcast_to(total, loss_ref.shape)

    blocks = [3 * _nbytes((tm, D_MODEL), F32)]
    return pl.pallas_call(
        body, name="loss_head", grid=(steps,),
        in_specs=[pl.BlockSpec((tm, D_MODEL), lambda i: (i, 0)),
                  pl.BlockSpec((tm, D_MODEL), lambda i: (i, 0))],
        out_specs=[pl.BlockSpec((tm, D_MODEL), lambda i: (i, 0)),
                   pl.BlockSpec((1, LANES), lambda i: (0, 0))],
        out_shape=[pltpu.HBM((tokens,D_MODEL), F32),
                   pltpu.HBM((1,LANES), F32)],
        scratch_shapes=[pltpu.VMEM((1, D_MODEL), F32)],
        compiler_params=pltpu.CompilerParams(dimension_semantics=("arbitrary",),
                                             vmem_limit_bytes=_vmem_limit(blocks)),
    )(_hbm(xl), _hbm(tgt))


def _outproj_bwd(g, y, ya, yb, wo, post_g3, layer):
    tokens = g.shape[0]
    tm = _token_tile(tokens)
    half = D_MODEL // 2

    def body(g_ref, y_ref, ya_ref, yb_ref, woa_ref, wob_ref, pg_ref, dya_ref, dyb_ref, dwo_ref, dpg_ref):
        @pl.when(pl.program_id(0) == 0)
        def _():
            dwo_ref[...] = jnp.zeros_like(dwo_ref)
            dpg_ref[...] = jnp.zeros_like(dpg_ref)

        gv = g_ref[...]
        yf = y_ref[...].astype(F32)
        r = lax.rsqrt(jnp.mean(yf * yf, axis=-1, keepdims=True) + NORM_EPS)
        yhat = yf * r
        gg = gv * pg_ref[...]
        dy = r * (gg - yhat * jnp.mean(gg * yhat, axis=-1, keepdims=True))
        dpg_ref[...] += jnp.sum(gv * yhat, axis=0, keepdims=True)
        dyb16 = dy.astype(BF16)
        dya_ref[...] = _mm_nt(dyb16, woa_ref[...]).astype(BF16)
        dyb_ref[...] = _mm_nt(dyb16, wob_ref[...]).astype(BF16)
        dwo_ref[0:half, :] += _mm_tn(ya_ref[...], dyb16)
        dwo_ref[half:D_MODEL, :] += _mm_tn(yb_ref[...], dyb16)

    blocks = [_nbytes((tm, D_MODEL), F32), _nbytes((tm, D_MODEL), BF16), 4 * _nbytes((tm, half), BF16),
              2 * _nbytes((half, D_MODEL), BF16), _nbytes((D_MODEL, D_MODEL), F32)]
    return pl.pallas_call(
        body, name=f"outproj_bwd_{layer}", grid=(tokens // tm,),
        in_specs=[pl.BlockSpec((tm, D_MODEL), lambda i: (i, 0)),
                  pl.BlockSpec((tm, D_MODEL), lambda i: (i, 0)),
                  pl.BlockSpec((tm, half), lambda i: (i, 0)),
                  pl.BlockSpec((tm, half), lambda i: (i, 0)),
                  pl.BlockSpec((None, half, D_MODEL), lambda i: (layer, 0, 0)),
                  pl.BlockSpec((None, half, D_MODEL), lambda i: (layer, 1, 0)),
                  pl.BlockSpec((None, 1, D_MODEL), lambda i: (layer, 0, 0))],
        out_specs=[pl.BlockSpec((tm, half), lambda i: (i, 0)),
                   pl.BlockSpec((tm, half), lambda i: (i, 0)),
                   pl.BlockSpec((D_MODEL, D_MODEL), lambda i: (0, 0)),
                   pl.BlockSpec((1, D_MODEL), lambda i: (0, 0))],
        out_shape=[pltpu.HBM((tokens,half), BF16),
                   pltpu.HBM((tokens,half), BF16),
                   pltpu.HBM((D_MODEL,D_MODEL), F32),
                   pltpu.HBM((1,D_MODEL), F32)],
        compiler_params=pltpu.CompilerParams(dimension_semantics=("arbitrary",),
                                             vmem_limit_bytes=_vmem_limit(blocks)),
    )(_hbm(g), _hbm(y), _hbm(ya), _hbm(yb), _hbm(wo), _hbm(wo), post_g3)


def _gmlp_bwd(pa, dya, ln_g3, ln_b3, wm, wmt, bsb, layer):
    tokens = pa.shape[0]
    tc = _token_tile(tokens)
    steps = tokens // tc

    def body(pa_ref, dya_ref, lg_ref, lb_ref, wm_ref, wmt_ref, bsb_ref,
             da_ref, dws_ref, dbs_ref, dlg_ref, dlb_ref, dbsb_ref):
        i = pl.program_id(0)

        @pl.when(i == 0)
        def _():
            dws_ref[...] = jnp.zeros_like(dws_ref)
            dlg_ref[...] = jnp.zeros_like(dlg_ref)
            dlb_ref[...] = jnp.zeros_like(dlb_ref)
            dbsb_ref[...] = jnp.zeros_like(dbsb_ref)

        ln_g = lg_ref[...]
        low = lax.broadcasted_iota(jnp.int32, (CHUNK, LANES), 1) < HEAD_DIM

        def chunk(ci, carry):
            rows = pl.ds(pl.multiple_of(ci * CHUNK, CHUNK), CHUNK)
            pu, pv, pz, u, vhat, rstd, vnb, mixed, sg = _gmlp_forward_chunk(
                pa_ref[rows, :], ln_g, lb_ref[...], wm_ref, bsb_ref[...])
            dy = dya_ref[rows, :].astype(F32)
            sz = pz * sg
            du = dy * mixed * sz
            dmixed = dy * u * sz
            dz = dy * u * mixed * (sg * (1.0 + pz * (1.0 - sg)))
            dbsb_ref[...] += dmixed
            dmb = dmixed.astype(BF16)
            zero = jnp.zeros((CHUNK, LANES), BF16)
            parts = []
            for p in range(A_GROUPS // 2):
                dmp = dmb[:, LANES * p:LANES * (p + 1)]
                vp = vnb[:, LANES * p:LANES * (p + 1)]
                parts.append(jnp.where(low, _mm(wmt_ref[2 * p], dmp), _mm(wmt_ref[2 * p + 1], dmp)))
                dws_ref[2 * p] += _mm_nt(jnp.where(low, dmp, zero), vp)
                dws_ref[2 * p + 1] += _mm_nt(jnp.where(low, zero, dmp), vp)
            dvn = jnp.concatenate(parts, axis=1)
            dlg_ref[...] += jnp.sum(dvn * vhat, axis=0, keepdims=True)
            dlb_ref[...] += jnp.sum(dvn, axis=0, keepdims=True)
            dvh = dvn * ln_g
            dvv = rstd * (dvh - jnp.mean(dvh, axis=-1, keepdims=True)
                          - vhat * jnp.mean(dvh * vhat, axis=-1, keepdims=True))
            da_ref[rows, 0:A_WIDTH] = (du * _gelu_grad(pu)).astype(BF16)
            da_ref[rows, A_WIDTH:2 * A_WIDTH] = (dvv * _gelu_grad(pv)).astype(BF16)
            da_ref[rows, 2 * A_WIDTH:3 * A_WIDTH] = dz.astype(BF16)
            return carry

        lax.fori_loop(0, tc // CHUNK, chunk, 0)

        @pl.when(i == steps - 1)
        def _():
            causal = (lax.broadcasted_iota(jnp.int32, (CHUNK, CHUNK), 0)
                      >= lax.broadcasted_iota(jnp.int32, (CHUNK, CHUNK), 1))
            for h in range(A_GROUPS):
                dws_ref[h] = jnp.where(causal, dws_ref[h], 0.0)
            acc = dbsb_ref[...]
            lane_full = lax.broadcasted_iota(jnp.int32, (CHUNK, A_WIDTH), 1)
            lane_out = lax.broadcasted_iota(jnp.int32, (CHUNK, LANES), 1)
            out = jnp.zeros((CHUNK, LANES), F32)
            for h in range(A_GROUPS):
                in_group = jnp.logical_and(lane_full >= HEAD_DIM * h, lane_full < HEAD_DIM * (h + 1))
                s = jnp.sum(jnp.where(in_group, acc, 0.0), axis=-1, keepdims=True)
                out = jnp.where(lane_out == h, s, out)
            dbs_ref[...] = out

    blocks = [_nbytes((tc, PA_WIDTH), BF16), _nbytes((tc, A_WIDTH), BF16), 2 * _nbytes((A_GROUPS, CHUNK, CHUNK), BF16),
              _nbytes((CHUNK, A_WIDTH), F32), _nbytes((tc, PA_WIDTH), BF16), _nbytes((A_GROUPS, CHUNK, CHUNK), F32),
              _nbytes((CHUNK, LANES), F32)]
    return pl.pallas_call(
        body, name=f"gmlp_bwd_{layer}", grid=(steps,),
        in_specs=[pl.BlockSpec((tc, PA_WIDTH), lambda i: (i, 0)),
                  pl.BlockSpec((tc, A_WIDTH), lambda i: (i, 0)),
                  pl.BlockSpec((None, 1, A_WIDTH), lambda i: (layer, 0, 0)),
                  pl.BlockSpec((None, 1, A_WIDTH), lambda i: (layer, 0, 0)),
                  pl.BlockSpec((None, A_GROUPS, CHUNK, CHUNK), lambda i: (layer, 0, 0, 0)),
                  pl.BlockSpec((None, A_GROUPS, CHUNK, CHUNK), lambda i: (layer, 0, 0, 0)),
                  pl.BlockSpec((None, CHUNK, A_WIDTH), lambda i: (layer, 0, 0))],
        out_specs=[pl.BlockSpec((tc, PA_WIDTH), lambda i: (i, 0)),
                   pl.BlockSpec((A_GROUPS, CHUNK, CHUNK), lambda i: (0, 0, 0)),
                   pl.BlockSpec((CHUNK, LANES), lambda i: (0, 0)),
                   pl.BlockSpec((1, A_WIDTH), lambda i: (0, 0)),
                   pl.BlockSpec((1, A_WIDTH), lambda i: (0, 0))],
        out_shape=[pltpu.HBM((tokens,PA_WIDTH), BF16),
                   pltpu.HBM((A_GROUPS, CHUNK, CHUNK), F32),
                   pltpu.HBM((CHUNK, LANES), F32),
                   pltpu.HBM((1,A_WIDTH), F32),
                   pltpu.HBM((1,A_WIDTH), F32)],
        scratch_shapes=[pltpu.VMEM((CHUNK, A_WIDTH), F32)],
        compiler_params=pltpu.CompilerParams(dimension_semantics=("arbitrary",),
                                             vmem_limit_bytes=_vmem_limit(blocks, _nbytes((CHUNK, A_WIDTH), F32))),
    )(_hbm(pa), _hbm(dya), ln_g3, ln_b3, wm, wmt, bsb)


def _attn_bwd(pb, dyb, bias, sinks, b_loc, layer):
    tokens = pb.shape[0]
    nb = tokens // b_loc // BLOCK
    qz_width = 2 * B_WIDTH
    tq = _token_tile(tokens)
    per_tile = tq // BLOCK
    nt = tokens // tq

    def body(sink_ref, cur_ref, prev_ref, dyb_ref, bias_ref, dqz_ref, dkv_ref, dbias_ref, dsink_ref,
             kv_ref, acc_ref):
        t = pl.program_id(0)

        @pl.when(t == 0)
        def _():
            dbias_ref[...] = jnp.zeros_like(dbias_ref)
            dsink_ref[...] = jnp.zeros_like(dsink_ref)
            acc_ref[0:BLOCK, :] = jnp.zeros((BLOCK, 2 * KV_WIDTH), F32)

        @pl.when(t < nt)
        def _():
            acc_ref[BLOCK:, :] = jnp.zeros((tq, 2 * KV_WIDTH), F32)
            _fill_kv(kv_ref, prev_ref, cur_ref)
            tri = _band_masks()
            low = lax.broadcasted_iota(jnp.int32, (BLOCK, LANES), 1) < HEAD_DIM
            low_kv = lax.broadcasted_iota(jnp.int32, (2 * BLOCK, LANES), 1) < HEAD_DIM

            def block(i, carry):
                start = pl.multiple_of(i * BLOCK, BLOCK)
                rows = pl.ds(start, BLOCK)
                first = lax.rem(t * per_tile + i, nb) == 0
                kill = jnp.logical_and(first, jnp.logical_not(tri))
                kv = kv_ref[pl.ds(start, 2 * BLOCK), :]
                k_ops = _kv_variants(kv[:, 0:KV_WIDTH])
                v_ops = _kv_variants(kv[:, KV_WIDTH:2 * KV_WIDTH])
                probs, sink_p, p16, qs, dos, delta, ds16 = {}, {}, {}, {}, {}, {}, {}
                for kvh in range(2):
                    pairs = (2 * kvh, 2 * kvh + 1)
                    qs[kvh] = jnp.concatenate([cur_ref[rows, LANES * p:LANES * (p + 1)] for p in pairs], axis=0)
                    for j in range(2):
                        sf = _mm_nt(qs[kvh], k_ops[kvh][j])
                        for r, p in enumerate(pairs):
                            hd = 2 * p + j
                            probs[hd], sink_p[hd] = _attn_probs(sf[BLOCK * r:BLOCK * (r + 1)], bias_ref[hd],
                                                                sink_ref[layer, hd], tri, kill)
                            p16[hd] = _unwrap16(probs[hd], tri)
                for kvh in range(2):
                    pairs = (2 * kvh, 2 * kvh + 1)
                    out = jnp.zeros((2 * BLOCK, LANES), F32)
                    for j in range(2):
                        out = out + _mm(jnp.concatenate([p16[2 * p + j] for p in pairs], axis=0), v_ops[kvh][j])
                    d_outs = []
                    for r, p in enumerate(pairs):
                        bz = cur_ref[rows, BZ_OFF + LANES * p:BZ_OFF + LANES * (p + 1)].astype(F32)
                        sg = _sigmoid(bz)
                        dyp = dyb_ref[rows, LANES * p:LANES * (p + 1)].astype(F32)
                        out_p = out[BLOCK * r:BLOCK * (r + 1)]
                        d_out = dyp * (bz * sg)
                        dqz_ref[rows, B_WIDTH + LANES * p:B_WIDTH + LANES * (p + 1)] = (
                            dyp * out_p * (sg * (1.0 + bz * (1.0 - sg)))).astype(BF16)
                        dod = d_out * out_p
                        delta[2 * p] = jnp.sum(jnp.where(low, dod, 0.0), axis=-1, keepdims=True)
                        delta[2 * p + 1] = jnp.sum(jnp.where(low, 0.0, dod), axis=-1, keepdims=True)
                        d_outs.append(d_out.astype(BF16))
                    dos[kvh] = jnp.concatenate(d_outs, axis=0)
                for kvh in range(2):
                    pairs = (2 * kvh, 2 * kvh + 1)
                    for j in range(2):
                        dpf = _mm_nt(dos[kvh], v_ops[kvh][j])
                        for r, p in enumerate(pairs):
                            hd = 2 * p + j
                            ds = probs[hd] * (_wrap(dpf[BLOCK * r:BLOCK * (r + 1)], tri) - delta[hd])
                            dsink_ref[hd:hd + 1, :] += jnp.sum(-sink_p[hd] * delta[hd], axis=0, keepdims=True)
                            dbias_ref[hd] += ds
                            ds16[hd] = _unwrap16(ds, tri)
                dk_acc = [[None, None], [None, None]]
                dv_acc = [[None, None], [None, None]]
                for kvh in range(2):
                    pairs = (2 * kvh, 2 * kvh + 1)
                    dq = jnp.zeros((2 * BLOCK, LANES), F32)
                    for j in range(2):
                        dss = jnp.concatenate([ds16[2 * p + j] for p in pairs], axis=0)
                        pss = jnp.concatenate([p16[2 * p + j] for p in pairs], axis=0)
                        dq = dq + _mm(dss, k_ops[kvh][j])
                        dk_acc[kvh][j] = _mm_tn(dss, qs[kvh])
                        dv_acc[kvh][j] = _mm_tn(pss, dos[kvh])
                    for r, p in enumerate(pairs):
                        dqz_ref[rows, LANES * p:LANES * (p + 1)] = (dq[BLOCK * r:BLOCK * (r + 1)] * SCALE).astype(BF16)

                def fold(acc):
                    return jnp.where(low_kv,
                                     acc[0][0] + pltpu.roll(acc[0][1], HEAD_DIM, 1),
                                     pltpu.roll(acc[1][0], HEAD_DIM, 1) + acc[1][1])

                acc_ref[pl.ds(start, 2 * BLOCK), :] += jnp.concatenate(
                    [fold(dk_acc) * SCALE, fold(dv_acc)], axis=1)
                return carry

            lax.fori_loop(0, per_tile, block, 0)
            dkv_ref[...] = acc_ref[0:tq, :].astype(BF16)
            acc_ref[0:BLOCK, :] = acc_ref[tq:tq + BLOCK, :]

        @pl.when(t == nt)
        def _():
            dkv_ref[0:BLOCK, :] = acc_ref[0:BLOCK, :].astype(BF16)
            dkv_ref[BLOCK:, :] = jnp.zeros((tq - BLOCK, 2 * KV_WIDTH), BF16)

    def cur_map(t):
        return (jnp.minimum(t, nt - 1), 0)

    def prev_map(t):
        return (jnp.maximum(jnp.minimum(t, nt - 1) * per_tile - 1, 0), K_OFF // (2 * KV_WIDTH))

    blocks = [_nbytes((tq, PB_WIDTH), BF16), _nbytes((BLOCK, 2 * KV_WIDTH), BF16), _nbytes((tq, B_WIDTH), BF16),
              2 * _nbytes((B_HEADS, BLOCK, BLOCK), F32), _nbytes((tq, qz_width), BF16),
              _nbytes((tq, 2 * KV_WIDTH), BF16), _nbytes((B_HEADS, LANES), F32)]
    scratch = _nbytes((tq + BLOCK, 2 * KV_WIDTH), BF16) + _nbytes((tq + BLOCK, 2 * KV_WIDTH), F32)
    return pl.pallas_call(
        body, name=f"attn_bwd_{layer}", grid=(nt + 1,),
        in_specs=[pl.BlockSpec(memory_space=pltpu.SMEM),
                  pl.BlockSpec((tq, PB_WIDTH), cur_map),
                  pl.BlockSpec((BLOCK, 2 * KV_WIDTH), prev_map),
                  pl.BlockSpec((tq, B_WIDTH), cur_map),
                  pl.BlockSpec((B_HEADS, BLOCK, BLOCK), lambda t: (0, 0, 0))],
        out_specs=[pl.BlockSpec((tq, qz_width), cur_map),
                   pl.BlockSpec((tq, 2 * KV_WIDTH), lambda t: (t, 0)),
                   pl.BlockSpec((B_HEADS, BLOCK, BLOCK), lambda t: (0, 0, 0)),
                   pl.BlockSpec((B_HEADS, LANES), lambda t: (0, 0))],
        out_shape=[pltpu.HBM((tokens, qz_width), BF16),
                   pltpu.HBM((tokens + tq, 2 * KV_WIDTH), BF16),
                   pltpu.HBM((B_HEADS, BLOCK, BLOCK), F32),
                   pltpu.HBM((B_HEADS, LANES), F32)],
        scratch_shapes=[pltpu.VMEM((tq + BLOCK, 2 * KV_WIDTH), BF16),
                        pltpu.VMEM((tq + BLOCK, 2 * KV_WIDTH), F32)],
        compiler_params=pltpu.CompilerParams(dimension_semantics=("arbitrary",),
                                             vmem_limit_bytes=_vmem_limit(blocks, scratch)),
    )(sinks, _hbm(pb), _hbm(pb), _hbm(dyb), _hbm(bias))


def _inproj_bwd_dx(da, dqz, dkv, wa, wb, x2, g, pre_g3, layer):
    tokens = x2.shape[0]
    tm = _token_tile(tokens)

    def body(da_ref, dqz_ref, dkv_ref, wa_ref, wb_ref, x_ref, g_ref, pg_ref, gn_ref, dpg_ref):
        @pl.when(pl.program_id(0) == 0)
        def _():
            dpg_ref[...] = jnp.zeros_like(dpg_ref)

        dh = _mm_nt(da_ref[...], wa_ref[...])
        dh += _mm_nt(dqz_ref[:, 0:B_WIDTH], wb_ref[:, 0:B_WIDTH])
        dh += _mm_nt(dkv_ref[...], wb_ref[:, K_OFF:BZ_OFF])
        dh += _mm_nt(dqz_ref[:, B_WIDTH:2 * B_WIDTH], wb_ref[:, BZ_OFF:PB_WIDTH])
        x = x_ref[...]
        r = lax.rsqrt(jnp.mean(x * x, axis=-1, keepdims=True) + NORM_EPS)
        xhat = x * r
        dhg = dh * pg_ref[...]
        dpg_ref[...] += jnp.sum(dh * xhat, axis=0, keepdims=True)
        gn_ref[...] = g_ref[...] + r * (dhg - xhat * jnp.mean(dhg * xhat, axis=-1, keepdims=True))

    blocks = [_nbytes((tm, PA_WIDTH), BF16), _nbytes((tm, 2 * B_WIDTH), BF16), _nbytes((tm, 2 * KV_WIDTH), BF16),
              _nbytes((D_MODEL, IN_WIDTH), BF16), 3 * _nbytes((tm, D_MODEL), F32)]
    return pl.pallas_call(
        body, name=f"inproj_bwd_dx_{layer}", grid=(tokens // tm,),
        in_specs=[pl.BlockSpec((tm, PA_WIDTH), lambda i: (i, 0)),
                  pl.BlockSpec((tm, 2 * B_WIDTH), lambda i: (i, 0)),
                  pl.BlockSpec((tm, 2 * KV_WIDTH), lambda i: (i, 0)),
                  pl.BlockSpec((None, D_MODEL, PA_WIDTH), lambda i: (layer, 0, 0)),
                  pl.BlockSpec((None, D_MODEL, PB_WIDTH), lambda i: (layer, 0, 0)),
                  pl.BlockSpec((tm, D_MODEL), lambda i: (i, 0)),
                  pl.BlockSpec((tm, D_MODEL), lambda i: (i, 0)),
                  pl.BlockSpec((None, 1, D_MODEL), lambda i: (layer, 0, 0))],
        out_specs=[pl.BlockSpec((tm, D_MODEL), lambda i: (i, 0)),
                   pl.BlockSpec((1, D_MODEL), lambda i: (0, 0))],
        out_shape=[pltpu.HBM((tokens,D_MODEL), F32),
                   pltpu.HBM((1,D_MODEL), F32)],
        compiler_params=pltpu.CompilerParams(dimension_semantics=("arbitrary",),
                                             vmem_limit_bytes=_vmem_limit(blocks)),
    )(_hbm(da), _hbm(dqz), _hbm(dkv), _hbm(wa), _hbm(wb), _hbm(x2), _hbm(g), pre_g3)


def _inproj_bwd_dw(h, da, dqz, dkv, layer):
    tokens = h.shape[0]
    tm = _token_tile(tokens)

    def body(h_ref, da_ref, dqz_ref, dkv_ref, dwa_ref, dwb_ref):
        @pl.when(pl.program_id(0) == 0)
        def _():
            dwa_ref[...] = jnp.zeros_like(dwa_ref)
            dwb_ref[...] = jnp.zeros_like(dwb_ref)

        hv = h_ref[...]
        dwa_ref[...] += _mm_tn(hv, da_ref[...])
        dwb_ref[:, 0:B_WIDTH] += _mm_tn(hv, dqz_ref[:, 0:B_WIDTH])
        dwb_ref[:, K_OFF:BZ_OFF] += _mm_tn(hv, dkv_ref[...])
        dwb_ref[:, BZ_OFF:PB_WIDTH] += _mm_tn(hv, dqz_ref[:, B_WIDTH:2 * B_WIDTH])

    blocks = [_nbytes((tm, D_MODEL), BF16), _nbytes((tm, PA_WIDTH), BF16), _nbytes((tm, 2 * B_WIDTH), BF16),
              _nbytes((tm, 2 * KV_WIDTH), BF16), _nbytes((D_MODEL, IN_WIDTH), F32)]
    return pl.pallas_call(
        body, name=f"inproj_bwd_dw_{layer}", grid=(tokens // tm,),
        in_specs=[pl.BlockSpec((tm, D_MODEL), lambda i: (i, 0)),
                  pl.BlockSpec((tm, PA_WIDTH), lambda i: (i, 0)),
                  pl.BlockSpec((tm, 2 * B_WIDTH), lambda i: (i, 0)),
                  pl.BlockSpec((tm, 2 * KV_WIDTH), lambda i: (i, 0))],
        out_specs=[pl.BlockSpec((D_MODEL, PA_WIDTH), lambda i: (0, 0)),
                   pl.BlockSpec((D_MODEL, PB_WIDTH), lambda i: (0, 0))],
        out_shape=[pltpu.HBM((D_MODEL,PA_WIDTH), F32),
                   pltpu.HBM((D_MODEL,PB_WIDTH), F32)],
        compiler_params=pltpu.CompilerParams(dimension_semantics=("arbitrary",),
                                             vmem_limit_bytes=_vmem_limit(blocks)),
    )(_hbm(h), _hbm(da), _hbm(dqz), _hbm(dkv))


def _rel_bias_grad(dbias_layers, bucket):
    n = len(dbias_layers)

    def body(*refs):
        bucket_ref, out_ref = refs[n], refs[n + 1]
        bk = bucket_ref[...]
        row = lax.broadcasted_iota(jnp.int32, (REL_BUCKETS, LANES), 0)
        lane = lax.broadcasted_iota(jnp.int32, (REL_BUCKETS, LANES), 1)
        out = jnp.zeros((REL_BUCKETS, LANES), F32)
        for h in range(B_HEADS):
            tot = refs[0][h]
            for a in range(1, n):
                tot = tot + refs[a][h]

            def one(b, acc, tot=tot, h=h):
                s = jnp.sum(jnp.where(bk == b, tot, 0.0), axis=-1, keepdims=True)
                s = jnp.sum(s, axis=0, keepdims=True)
                return jnp.where(jnp.logical_and(row == b, lane == h), s, acc)

            out = lax.fori_loop(0, REL_BUCKETS, one, out)
        out_ref[...] = out

    vm = pl.BlockSpec(memory_space=pltpu.VMEM)
    return pl.pallas_call(
        body, name="rel_bias_grad",
        out_shape=jax.ShapeDtypeStruct((REL_BUCKETS, LANES), F32),
        in_specs=[vm] * (n + 1), out_specs=vm,
    )(*dbias_layers, bucket)


def _reduce_adamw(slots, w, m, v, name):
    rows, cols = w.shape
    tr = _row_tile(rows)
    c1 = 1.0 / (1.0 - ADAM_B1 ** ADAM_STEP)
    c2 = 1.0 / (1.0 - ADAM_B2 ** ADAM_STEP)

    def body(s_ref, w_ref, m_ref, v_ref, g_ref, d_ref, nm_ref, nv_ref):
        g = s_ref[0].astype(F32)
        for i in range(1, N_DEV):
            g = g + s_ref[i].astype(F32)
        nm = ADAM_B1 * m_ref[...] + (1.0 - ADAM_B1) * g
        nv = ADAM_B2 * v_ref[...] + (1.0 - ADAM_B2) * (g * g)
        g_ref[...] = g
        nm_ref[...] = nm
        nv_ref[...] = nv
        d_ref[...] = -ADAM_LR * ((nm * c1) / (jnp.sqrt(nv * c2) + ADAM_EPS) + ADAM_WD * w_ref[...])

    blocks = [_nbytes((N_DEV, tr, cols), slots.dtype), 7 * _nbytes((tr, cols), F32)]
    tile = pl.BlockSpec((tr, cols), lambda i: (i, 0))
    return pl.pallas_call(
        body, name=name, grid=(rows // tr,),
        in_specs=[pl.BlockSpec((N_DEV, tr, cols), lambda i: (0, i, 0)), tile, tile, tile],
        out_specs=[tile] * 4,
        out_shape=[pltpu.HBM((rows, cols), F32)] * 4,
        compiler_params=pltpu.CompilerParams(dimension_semantics=("parallel",),
                                             vmem_limit_bytes=_vmem_limit(blocks)),
    )(_hbm(slots), _hbm(w), _hbm(m), _hbm(v))


_SMALL = ("w_spatial", "pre_norm_g", "post_norm_g", "ln_v_g", "ln_v_b", "b_spatial", "sinks", "rel_bias")


def _pack_small(parts):
    slabs = []
    for name in _SMALL:
        flat = parts[name].astype(F32).reshape(-1)
        pad = (-flat.shape[0]) % (8 * LANES)
        slabs.append(jnp.pad(flat, (0, pad)).reshape(-1, LANES))
    return jnp.concatenate(slabs, axis=0)


def _unpack_small(slab, shapes):
    out, row = {}, 0
    for name in _SMALL:
        size = int(np.prod(shapes[name]))
        rows = -(-size // (8 * LANES)) * 8
        out[name] = slab[row:row + rows].reshape(-1)[:size].reshape(shapes[name])
        row += rows
    return out


def kernel(x, pre_norm_g, w_in, ln_v_g, ln_v_b, w_spatial, b_spatial, sinks, rel_bias, w_out, post_norm_g, loss_target, m_pre_norm_g, m_w_in, m_ln_v_g, m_ln_v_b, m_w_spatial, m_b_spatial, m_sinks, m_rel_bias, m_w_out, m_post_norm_g, v_pre_norm_g, v_w_in, v_ln_v_g, v_ln_v_b, v_w_spatial, v_b_spatial, v_sinks, v_rel_bias, v_w_out, v_post_norm_g):
    b_loc, seq, _ = x.shape
    tokens = b_loc * seq
    depth = w_in.shape[0]
    in_shard = w_in.shape[2]
    out_shard = w_out.shape[1]
    assert in_shard * N_DEV == IN_WIDTH and out_shard * N_DEV == D_MODEL and seq % BLOCK == 0

    gin, gout = _all_gather([w_in.astype(BF16).reshape(depth * D_MODEL, in_shard),
                             w_out.astype(BF16).reshape(depth * out_shard, D_MODEL)], "weights_all_gather")
    w_full = gin.reshape(N_DEV, depth, D_MODEL, in_shard).transpose(1, 2, 0, 3).reshape(depth, D_MODEL, IN_WIDTH)
    wa, wb = w_full[:, :, :PA_WIDTH], w_full[:, :, PA_WIDTH:]
    wo = gout.reshape(N_DEV, depth, out_shard, D_MODEL).transpose(1, 0, 2, 3).reshape(depth, D_MODEL, D_MODEL)

    causal = jnp.tril(jnp.ones((CHUNK, CHUNK), dtype=bool))
    wm = jnp.where(causal, w_spatial, 0.0).astype(BF16)
    wmt = _hbm(jnp.swapaxes(wm, -1, -2))
    wm = _hbm(wm)
    bsb = _hbm(jnp.repeat(jnp.swapaxes(b_spatial, -1, -2), HEAD_DIM, axis=-1))
    pre_g3 = _hbm(pre_norm_g.reshape(depth, 1, D_MODEL))
    post_g3 = _hbm(post_norm_g.reshape(depth, 1, D_MODEL))
    ln_g3 = _hbm(ln_v_g.reshape(depth, 1, A_WIDTH))
    ln_b3 = _hbm(ln_v_b.reshape(depth, 1, A_WIDTH))
    bucket = jnp.asarray(_bucket_table())
    bias = _bias_table(rel_bias, bucket)

    xs, saved = [x.reshape(tokens, D_MODEL)], []
    for layer in range(depth):
        h, pa, pb = _inproj_fwd(xs[-1], pre_g3, wa, wb, layer)
        ya = _gmlp_fwd(pa, ln_g3, ln_b3, wm, bsb, layer)
        yb = _attn_fwd(pb, bias, sinks, b_loc, layer)
        y, x_next = _outproj_fwd(ya, yb, wo, xs[-1], post_g3, layer)
        saved.append((h, pa, pb, ya, yb, y))
        xs.append(x_next)
    g, loss_part = _loss_head(xs[-1], loss_target.reshape(tokens, D_MODEL))
    loss = lax.psum(loss_part[0, 0], ("x", "y", "c"))

    grads = {name: [None] * depth for name in ("w_spatial", "pre_norm_g", "post_norm_g", "ln_v_g", "ln_v_b",
                                               "b_spatial", "sinks", "wa", "wb", "wo", "dbias")}
    for layer in reversed(range(depth)):
        h, pa, pb, ya, yb, y = saved[layer]
        dya, dyb, grads["wo"][layer], dpost = _outproj_bwd(g, y, ya, yb, wo, post_g3, layer)
        da, dws, dbs, dlg, dlb = _gmlp_bwd(pa, dya, ln_g3, ln_b3, wm, wmt, bsb, layer)
        dqz, dkv_shifted, grads["dbias"][layer], dsink = _attn_bwd(pb, dyb, bias, sinks, b_loc, layer)
        dkv = dkv_shifted[BLOCK:BLOCK + tokens]
        g, dpre = _inproj_bwd_dx(da, dqz, dkv, wa, wb, xs[layer], g, pre_g3, layer)
        grads["wa"][layer], grads["wb"][layer] = _inproj_bwd_dw(h, da, dqz, dkv, layer)
        grads["w_spatial"][layer] = dws
        grads["b_spatial"][layer] = dbs[:, :A_GROUPS].T
        grads["ln_v_g"][layer] = dlg[0]
        grads["ln_v_b"][layer] = dlb[0]
        grads["sinks"][layer] = dsink[:, 0]
        grads["pre_norm_g"][layer] = dpre[0]
        grads["post_norm_g"][layer] = dpost[0]
    grad_x = g.reshape(x.shape)
    drel = _rel_bias_grad(grads["dbias"], bucket)[:, :B_HEADS]

    dw_in = jnp.concatenate([jnp.stack(grads["wa"]), jnp.stack(grads["wb"])], axis=-1)
    send_in = dw_in.astype(BF16).reshape(depth, D_MODEL, N_DEV, in_shard).transpose(2, 0, 1, 3)
    send_out = jnp.stack(grads["wo"]).astype(BF16).reshape(depth, N_DEV, out_shard, D_MODEL).transpose(1, 0, 2, 3)
    recv_in, recv_out = _all_to_all([send_in.reshape(N_DEV, depth * D_MODEL, in_shard),
                                     send_out.reshape(N_DEV, depth * out_shard, D_MODEL)], "grads_all_to_all")
    res_in = _reduce_adamw(recv_in, w_in.reshape(-1, in_shard), m_w_in.reshape(-1, in_shard),
                           v_w_in.reshape(-1, in_shard), "adamw_w_in")
    res_out = _reduce_adamw(recv_out, w_out.reshape(-1, D_MODEL), m_w_out.reshape(-1, D_MODEL),
                            v_w_out.reshape(-1, D_MODEL), "adamw_w_out")
    res_in = [r.reshape(w_in.shape) for r in res_in]
    res_out = [r.reshape(w_out.shape) for r in res_out]

    small_w = dict(w_spatial=w_spatial, pre_norm_g=pre_norm_g, post_norm_g=post_norm_g, ln_v_g=ln_v_g,
                   ln_v_b=ln_v_b, b_spatial=b_spatial, sinks=sinks, rel_bias=rel_bias)
    small_m = dict(w_spatial=m_w_spatial, pre_norm_g=m_pre_norm_g, post_norm_g=m_post_norm_g, ln_v_g=m_ln_v_g,
                   ln_v_b=m_ln_v_b, b_spatial=m_b_spatial, sinks=m_sinks, rel_bias=m_rel_bias)
    small_v = dict(w_spatial=v_w_spatial, pre_norm_g=v_pre_norm_g, post_norm_g=v_post_norm_g, ln_v_g=v_ln_v_g,
                   ln_v_b=v_ln_v_b, b_spatial=v_b_spatial, sinks=v_sinks, rel_bias=v_rel_bias)
    small_g = {name: jnp.stack(grads[name]) for name in _SMALL if name != "rel_bias"}
    small_g["rel_bias"] = drel
    shapes = {name: small_w[name].shape for name in _SMALL}
    (slots,) = _all_gather([_pack_small(small_g)], "small_grads_all_gather")
    res_small = _reduce_adamw(slots, _pack_small(small_w), _pack_small(small_m), _pack_small(small_v), "adamw_small")
    res_small = [_unpack_small(r, shapes) for r in res_small]

    order = ("pre_norm_g", "w_in", "ln_v_g", "ln_v_b", "w_spatial", "b_spatial", "sinks", "rel_bias", "w_out",
             "post_norm_g")
    outs = [loss, grad_x]
    for kind in range(4):
        for name in order:
            if name == "w_in":
                outs.append(res_in[kind])
            elif name == "w_out":
                outs.append(res_out[kind])
            else:
                outs.append(res_small[kind][name])
    return tuple(outs)
```

```python
import math

import numpy as np
import jax
import jax.numpy as jnp
from jax import lax
from jax.experimental import pallas as pl
from jax.experimental.pallas import tpu as pltpu

F32 = jnp.float32
BF16 = jnp.bfloat16

D_MODEL = 1024
A_WIDTH = 512
A_GROUPS = 8
CHUNK = 128
B_HEADS = 8
HEAD_DIM = 64
B_WIDTH = 512
KV_WIDTH = 128
BLOCK = 128
REL_BUCKETS = 32
REL_MAX_DIST = 128
NORM_EPS = 1e-6
PA_WIDTH = 3 * A_WIDTH
PB_WIDTH = 2 * B_WIDTH + 2 * KV_WIDTH
IN_WIDTH = PA_WIDTH + PB_WIDTH
K_OFF, V_OFF, BZ_OFF = B_WIDTH, B_WIDTH + KV_WIDTH, B_WIDTH + 2 * KV_WIDTH
SCALE = HEAD_DIM ** -0.5
NEG = -1e30
N_DEV = 8
LANES = 128

ADAM_LR = 0.001
ADAM_B1 = 0.9
ADAM_B2 = 0.999
ADAM_EPS = 1e-08
ADAM_WD = 0.01
ADAM_STEP = 10

V7X_VMEM_BYTES = 64 * 1024 * 1024
VMEM_TEMP_BYTES = 12 * 1024 * 1024
MESH = pl.DeviceIdType.MESH


def _vmem_limit(block_bytes, scratch_bytes=0):
    need = 2 * sum(block_bytes) + scratch_bytes + VMEM_TEMP_BYTES
    return int(min(need, V7X_VMEM_BYTES - 8 * 1024 * 1024))


def _nbytes(shape, dtype):
    return int(np.prod(shape)) * jnp.dtype(dtype).itemsize


def _token_tile(tokens):
    tile = min(512, tokens // 2)
    assert tokens % tile == 0 and tile % CHUNK == 0, tokens
    return tile


def _row_tile(rows, cap=512):
    best = 8
    for t in range(8, cap + 1, 8):
        if rows % t == 0:
            best = t
    assert rows % best == 0, rows
    return best


def _mm(a, b):
    return lax.dot_general(a, b, (((1,), (0,)), ((), ())), preferred_element_type=F32)


def _mm_nt(a, b):
    return lax.dot_general(a, b, (((1,), (1,)), ((), ())), preferred_element_type=F32)


def _mm_tn(a, b):
    return lax.dot_general(a, b, (((0,), (0,)), ((), ())), preferred_element_type=F32)


_GELU_C = math.sqrt(2.0 / math.pi)


def _gelu(x):
    return 0.5 * x * (1.0 + jnp.tanh(_GELU_C * (x + 0.044715 * (x * x * x))))


def _gelu_grad(x):
    t = jnp.tanh(_GELU_C * (x + 0.044715 * (x * x * x)))
    return 0.5 * (1.0 + t) + 0.5 * x * (1.0 - t * t) * (_GELU_C * (1.0 + 3.0 * 0.044715 * (x * x)))


def _sigmoid(x):
    return 1.0 / (1.0 + jnp.exp(-x))


def _bucket_table():
    q = np.arange(BLOCK)[:, None]
    k = np.arange(BLOCK)[None, :]
    dist = np.where(k <= q, q - k, q + BLOCK - k)
    max_exact = REL_BUCKETS // 2
    safe = np.maximum(dist, 1).astype(np.float32)
    large = max_exact + (np.log(safe / np.float32(max_exact)) / np.float32(math.log(REL_MAX_DIST / max_exact))
                         * np.float32(REL_BUCKETS - max_exact)).astype(np.int32)
    large = np.minimum(large, REL_BUCKETS - 1)
    assert dist.min() >= 0 and dist.max() < BLOCK
    return np.where(dist < max_exact, dist, large).astype(np.int32)


def _hbm(x):
    return pltpu.with_memory_space_constraint(x, pltpu.HBM)


def _slot(px, py, pc):
    return 4 * px + 2 * py + pc


def _all_gather(arrs, name):
    n = len(arrs)

    def body(*refs):
        ins, outs = refs[:n], refs[n:2 * n]
        send_sems, recv_sems, local_sems = refs[2 * n:]
        x, y, c = lax.axis_index("x"), lax.axis_index("y"), lax.axis_index("c")
        me, sibling = (x, y, c), (x, y, 1 - c)
        chips = [(1 - x, y), (x, 1 - y), (1 - x, 1 - y)]

        def copy(a, k, block, to, src=None):
            dst = outs[a].at[_slot(*block)]
            return pltpu.make_async_remote_copy(
                src_ref=dst if src is None else src, dst_ref=dst,
                send_sem=send_sems.at[7 * a + k], recv_sem=recv_sems.at[7 * a + k],
                device_id=to, device_id_type=MESH)

        mine = [pltpu.make_async_copy(ins[a], outs[a].at[_slot(*me)], local_sems.at[a]) for a in range(n)]
        for cp in mine:
            cp.start()
        first = []
        for a in range(n):
            first.append(copy(a, 0, me, sibling, src=ins[a]))
            first += [copy(a, 1 + j, me, (*chip, c), src=ins[a]) for j, chip in enumerate(chips)]
        for cp in first:
            cp.start()
        passed = []
        for j, chip in enumerate(chips):
            for a in range(n):
                copy(a, 1 + j, (*chip, c), me).wait_recv()
                fwd = copy(a, 4 + j, (*chip, c), sibling)
                fwd.start()
                passed.append(fwd)
        for a in range(n):
            copy(a, 0, sibling, me).wait_recv()
            for j, chip in enumerate(chips):
                copy(a, 4 + j, (*chip, 1 - c), me).wait_recv()
        for cp in first + passed:
            cp.wait_send()
        for cp in mine:
            cp.wait()

    any_spec = pl.BlockSpec(memory_space=pl.ANY)
    return pl.pallas_call(
        body, name=name,
        out_shape=[jax.ShapeDtypeStruct((N_DEV,) + a.shape, a.dtype) for a in arrs],
        in_specs=[any_spec] * n, out_specs=[any_spec] * n,
        scratch_shapes=[pltpu.SemaphoreType.DMA((7 * n,)), pltpu.SemaphoreType.DMA((7 * n,)),
                        pltpu.SemaphoreType.DMA((n,))],
    )(*arrs)


_HBM_SPEC = pl.BlockSpec(memory_space=pltpu.HBM)
_SEM_SPEC = pl.BlockSpec(memory_space=pltpu.SEMAPHORE)
_DATAFLOW = pltpu.SideEffectType.DATAFLOW_SIDE_EFFECTING


def _peers():
    x, y, c = lax.axis_index("x"), lax.axis_index("y"), lax.axis_index("c")
    peers = []
    for k in range(1, N_DEV):
        fx, fy, fc = (k >> 2) & 1, (k >> 1) & 1, k & 1
        peers.append((1 - x if fx else x, 1 - y if fy else y, 1 - c if fc else c))
    return (x, y, c), peers


def _exchange_copy(send_ref, land_ref, row_off, src_slot, dst_slot, sems, idx, peer):
    rows = send_ref.shape[-2]
    src = send_ref.at[src_slot] if len(send_ref.shape) == 3 else send_ref
    return pltpu.make_async_remote_copy(
        src_ref=src, dst_ref=land_ref.at[dst_slot, pl.ds(row_off, rows), :],
        send_sem=sems[0].at[idx], recv_sem=sems[1].at[idx], device_id=peer, device_id_type=MESH)


def _exchange_start(sends, lands, row_offs, name):
    n = len(sends)

    def body(*refs):
        ins, zones = refs[:n], refs[n:2 * n]
        sems = refs[2 * n:2 * n + 2]
        token = refs[-1]
        me, peers = _peers()
        for a in range(n):
            for k, peer in enumerate(peers):
                _exchange_copy(ins[a], zones[a], row_offs[a], _slot(*peer), _slot(*me), sems, 7 * a + k, peer).start()
        token[...] = jnp.zeros_like(token)

    arrays = [_hbm(a) for a in list(sends) + list(lands)]
    out = pl.pallas_call(
        body, name=name,
        out_shape=(pltpu.SemaphoreType.DMA((7 * n,)), pltpu.SemaphoreType.DMA((7 * n,)),
                   *[pltpu.HBM(a.shape, a.dtype) for a in arrays], jax.ShapeDtypeStruct((8, LANES), F32)),
        in_specs=[_HBM_SPEC] * (2 * n),
        out_specs=(_SEM_SPEC, _SEM_SPEC, *[_HBM_SPEC] * (2 * n), pl.BlockSpec(memory_space=pltpu.VMEM)),
        input_output_aliases={i: 2 + i for i in range(2 * n)},
        compiler_params=pltpu.CompilerParams(has_side_effects=_DATAFLOW),
    )(*arrays)
    return (out[0], out[1]), list(out[2:2 + n]), list(out[2 + n:2 + 2 * n]), out[-1]


def _exchange_wait(started, lands, after, name):
    n = len(lands)
    flat_sends = [s for _, sends, _ in started for s in sends]
    flat_sems = [s for sems, _, _ in started for s in sems]
    ns = len(flat_sends)

    def body(*refs):
        ins, zones = refs[:ns], refs[ns:ns + n]
        sem_refs = refs[ns + n:ns + n + len(flat_sems)]
        me, peers = _peers()
        pos = 0
        for call, (_, sends, row_offs) in enumerate(started):
            sems = sem_refs[2 * call:2 * call + 2]
            for a in range(len(sends)):
                for k, peer in enumerate(peers):
                    cp = _exchange_copy(ins[pos + a], zones[a], row_offs[a], _slot(*peer), _slot(*peer), sems,
                                        7 * a + k, peer)
                    cp.wait_send()
                    cp.wait_recv()
            pos += len(sends)

    arrays = list(flat_sends) + list(lands)
    out = pl.pallas_call(
        body, name=name,
        out_shape=tuple(pltpu.HBM(a.shape, a.dtype) for a in arrays),
        in_specs=[_HBM_SPEC] * len(arrays) + [_SEM_SPEC] * len(flat_sems) + [pl.BlockSpec(memory_space=pl.ANY)],
        out_specs=tuple([_HBM_SPEC] * len(arrays)),
        input_output_aliases={i: i for i in range(len(arrays))},
        compiler_params=pltpu.CompilerParams(has_side_effects=_DATAFLOW),
    )(*arrays, *flat_sems, after)
    return list(out[ns:])


def _bias_table(rel_bias, bucket):
    def body(rel_ref, bucket_ref, out_ref):
        bk = bucket_ref[...]
        for h in range(B_HEADS):
            def pick(b, acc, h=h):
                return jnp.where(bk == b, rel_ref[b, h], acc)
            out_ref[h] = lax.fori_loop(0, REL_BUCKETS, pick, jnp.zeros((BLOCK, BLOCK), F32))

    return pl.pallas_call(
        body, name="bias_table",
        out_shape=jax.ShapeDtypeStruct((B_HEADS, BLOCK, BLOCK), F32),
        in_specs=[pl.BlockSpec(memory_space=pltpu.SMEM), pl.BlockSpec(memory_space=pltpu.VMEM)],
        out_specs=pl.BlockSpec(memory_space=pltpu.VMEM),
    )(rel_bias, bucket)


def _inproj_fwd(x2, pre_g3, wa, wb, layer):
    tokens = x2.shape[0]
    tm = _token_tile(tokens)

    def body(x_ref, g_ref, wa_ref, wb_ref, h_ref, pa_ref, pb_ref):
        x = x_ref[...]
        r = lax.rsqrt(jnp.mean(x * x, axis=-1, keepdims=True) + NORM_EPS)
        h = (x * r * g_ref[...]).astype(BF16)
        h_ref[...] = h
        pa_ref[...] = _mm(h, wa_ref[...]).astype(BF16)
        pb_ref[...] = _mm(h, wb_ref[...]).astype(BF16)

    blocks = [_nbytes((tm, D_MODEL), F32), _nbytes((D_MODEL, PA_WIDTH), BF16), _nbytes((D_MODEL, PB_WIDTH), BF16),
              _nbytes((tm, D_MODEL), BF16), _nbytes((tm, PA_WIDTH), BF16), _nbytes((tm, PB_WIDTH), BF16)]
    return pl.pallas_call(
        body, name=f"inproj_fwd_{layer}", grid=(tokens // tm,),
        in_specs=[pl.BlockSpec((tm, D_MODEL), lambda i: (i, 0)),
                  pl.BlockSpec((None, 1, D_MODEL), lambda i: (layer, 0, 0)),
                  pl.BlockSpec((None, D_MODEL, PA_WIDTH), lambda i: (layer, 0, 0)),
                  pl.BlockSpec((None, D_MODEL, PB_WIDTH), lambda i: (layer, 0, 0))],
        out_specs=[pl.BlockSpec((tm, D_MODEL), lambda i: (i, 0)),
                   pl.BlockSpec((tm, PA_WIDTH), lambda i: (i, 0)),
                   pl.BlockSpec((tm, PB_WIDTH), lambda i: (i, 0))],
        out_shape=[pltpu.HBM((tokens,D_MODEL), BF16),
                   pltpu.HBM((tokens,PA_WIDTH), BF16),
                   pltpu.HBM((tokens,PB_WIDTH), BF16)],
        compiler_params=pltpu.CompilerParams(dimension_semantics=("parallel",),
                                             vmem_limit_bytes=_vmem_limit(blocks)),
    )(_hbm(x2), pre_g3, _hbm(wa), _hbm(wb))


def _gmlp_forward_chunk(pa, ln_g, ln_b, wm_ref, bsb):
    pu = pa[:, 0:A_WIDTH].astype(F32)
    pv = pa[:, A_WIDTH:2 * A_WIDTH].astype(F32)
    pz = pa[:, 2 * A_WIDTH:3 * A_WIDTH].astype(F32)
    u = _gelu(pu)
    vv = _gelu(pv)
    mu = jnp.mean(vv, axis=-1, keepdims=True)
    xc = vv - mu
    rstd = lax.rsqrt(jnp.mean(xc * xc, axis=-1, keepdims=True) + NORM_EPS)
    vhat = xc * rstd
    vnb = (vhat * ln_g + ln_b).astype(BF16)
    low = lax.broadcasted_iota(jnp.int32, (CHUNK, LANES), 1) < HEAD_DIM
    parts = []
    for p in range(A_GROUPS // 2):
        vp = vnb[:, LANES * p:LANES * (p + 1)]
        parts.append(jnp.where(low, _mm(wm_ref[2 * p], vp), _mm(wm_ref[2 * p + 1], vp)))
    mixed = jnp.concatenate(parts, axis=1) + bsb
    sg = _sigmoid(pz)
    return pu, pv, pz, u, vhat, rstd, vnb, mixed, sg


def _gmlp_fwd(pa, ln_g3, ln_b3, wm, bsb, layer):
    tokens = pa.shape[0]
    tc = _token_tile(tokens)

    def body(pa_ref, lg_ref, lb_ref, wm_ref, bsb_ref, ya_ref):
        def chunk(ci, carry):
            rows = pl.ds(pl.multiple_of(ci * CHUNK, CHUNK), CHUNK)
            _, _, pz, u, _, _, _, mixed, sg = _gmlp_forward_chunk(
                pa_ref[rows, :], lg_ref[...], lb_ref[...], wm_ref, bsb_ref[...])
            ya_ref[rows, :] = (u * mixed * (pz * sg)).astype(BF16)
            return carry
        lax.fori_loop(0, tc // CHUNK, chunk, 0)

    blocks = [_nbytes((tc, PA_WIDTH), BF16), _nbytes((A_GROUPS, CHUNK, CHUNK), BF16),
              _nbytes((CHUNK, A_WIDTH), F32), _nbytes((tc, A_WIDTH), BF16)]
    return pl.pallas_call(
        body, name=f"gmlp_fwd_{layer}", grid=(tokens // tc,),
        in_specs=[pl.BlockSpec((tc, PA_WIDTH), lambda i: (i, 0)),
                  pl.BlockSpec((None, 1, A_WIDTH), lambda i: (layer, 0, 0)),
                  pl.BlockSpec((None, 1, A_WIDTH), lambda i: (layer, 0, 0)),
                  pl.BlockSpec((None, A_GROUPS, CHUNK, CHUNK), lambda i: (layer, 0, 0, 0)),
                  pl.BlockSpec((None, CHUNK, A_WIDTH), lambda i: (layer, 0, 0))],
        out_specs=pl.BlockSpec((tc, A_WIDTH), lambda i: (i, 0)),
        out_shape=pltpu.HBM((tokens,A_WIDTH), BF16),
        compiler_params=pltpu.CompilerParams(dimension_semantics=("parallel",),
                                             vmem_limit_bytes=_vmem_limit(blocks)),
    )(_hbm(pa), ln_g3, ln_b3, wm, bsb)


def _kv_variants(kv):
    t = kv.astype(F32)
    rolled = pltpu.roll(t, HEAD_DIM, 1)
    low = lax.broadcasted_iota(jnp.int32, t.shape, 1) < HEAD_DIM
    zero = jnp.zeros_like(t)
    head0 = (jnp.where(low, t, zero).astype(BF16), jnp.where(low, zero, rolled).astype(BF16))
    head1 = (jnp.where(low, rolled, zero).astype(BF16), jnp.where(low, zero, t).astype(BF16))
    return head0, head1


def _band_masks():
    row = lax.broadcasted_iota(jnp.int32, (BLOCK, BLOCK), 0)
    col = lax.broadcasted_iota(jnp.int32, (BLOCK, BLOCK), 1)
    return col <= row


def _wrap(full, tri):
    return jnp.where(tri, full[:, BLOCK:2 * BLOCK], full[:, 0:BLOCK])


def _attn_probs(sf, bias_h, sink, tri, kill):
    s = _wrap(sf, tri) * SCALE + bias_h
    s = jnp.where(kill, NEG, s)
    m = jnp.maximum(jnp.max(s, axis=-1, keepdims=True), sink)
    e = jnp.exp(s - m)
    es = jnp.exp(sink - m)
    inv = 1.0 / (jnp.sum(e, axis=-1, keepdims=True) + es)
    return e * inv, es * inv


def _unwrap16(p, tri):
    zero = jnp.zeros_like(p)
    return jnp.concatenate([jnp.where(tri, zero, p), jnp.where(tri, p, zero)], axis=1).astype(BF16)


def _fill_kv(kv_ref, prev_ref, cur_ref):
    kv_ref[0:BLOCK, :] = prev_ref[...]
    kv_ref[BLOCK:, :] = cur_ref[:, K_OFF:K_OFF + 2 * KV_WIDTH]


def _attn_fwd(pb, bias, sinks, b_loc, layer):
    tokens = pb.shape[0]
    nb = tokens // b_loc // BLOCK
    tq = _token_tile(tokens)
    per_tile = tq // BLOCK

    def body(sink_ref, cur_ref, prev_ref, bias_ref, yb_ref, kv_ref):
        t = pl.program_id(0)
        _fill_kv(kv_ref, prev_ref, cur_ref)
        tri = _band_masks()

        def block(i, carry):
            start = pl.multiple_of(i * BLOCK, BLOCK)
            rows = pl.ds(start, BLOCK)
            first = lax.rem(t * per_tile + i, nb) == 0
            kill = jnp.logical_and(first, jnp.logical_not(tri))
            kv = kv_ref[pl.ds(start, 2 * BLOCK), :]
            k_ops = _kv_variants(kv[:, 0:KV_WIDTH])
            v_ops = _kv_variants(kv[:, KV_WIDTH:2 * KV_WIDTH])
            p16 = {}
            for kvh in range(2):
                pairs = (2 * kvh, 2 * kvh + 1)
                qs = jnp.concatenate([cur_ref[rows, LANES * p:LANES * (p + 1)] for p in pairs], axis=0)
                for j in range(2):
                    sf = _mm_nt(qs, k_ops[kvh][j])
                    for r, p in enumerate(pairs):
                        hd = 2 * p + j
                        probs, _ = _attn_probs(sf[BLOCK * r:BLOCK * (r + 1)], bias_ref[hd], sink_ref[layer, hd],
                                               tri, kill)
                        p16[hd] = _unwrap16(probs, tri)
            for kvh in range(2):
                pairs = (2 * kvh, 2 * kvh + 1)
                out = jnp.zeros((2 * BLOCK, LANES), F32)
                for j in range(2):
                    out = out + _mm(jnp.concatenate([p16[2 * p + j] for p in pairs], axis=0), v_ops[kvh][j])
                for r, p in enumerate(pairs):
                    bz = cur_ref[rows, BZ_OFF + LANES * p:BZ_OFF + LANES * (p + 1)].astype(F32)
                    yb_ref[rows, LANES * p:LANES * (p + 1)] = (
                        out[BLOCK * r:BLOCK * (r + 1)] * (bz * _sigmoid(bz))).astype(BF16)
            return carry

        lax.fori_loop(0, per_tile, block, 0)

    blocks = [_nbytes((tq, PB_WIDTH), BF16), _nbytes((BLOCK, 2 * KV_WIDTH), BF16),
              _nbytes((B_HEADS, BLOCK, BLOCK), F32), _nbytes((tq, B_WIDTH), BF16)]
    scratch = _nbytes((tq + BLOCK, 2 * KV_WIDTH), BF16)
    return pl.pallas_call(
        body, name=f"attn_fwd_{layer}", grid=(tokens // tq,),
        in_specs=[pl.BlockSpec(memory_space=pltpu.SMEM),
                  pl.BlockSpec((tq, PB_WIDTH), lambda t: (t, 0)),
                  pl.BlockSpec((BLOCK, 2 * KV_WIDTH),
                               lambda t: (jnp.maximum(t * per_tile - 1, 0), K_OFF // (2 * KV_WIDTH))),
                  pl.BlockSpec((B_HEADS, BLOCK, BLOCK), lambda t: (0, 0, 0))],
        out_specs=pl.BlockSpec((tq, B_WIDTH), lambda t: (t, 0)),
        out_shape=pltpu.HBM((tokens, B_WIDTH), BF16),
        scratch_shapes=[pltpu.VMEM((tq + BLOCK, 2 * KV_WIDTH), BF16)],
        compiler_params=pltpu.CompilerParams(dimension_semantics=("parallel",),
                                             vmem_limit_bytes=_vmem_limit(blocks, scratch)),
    )(sinks, _hbm(pb), _hbm(pb), _hbm(bias))


def _outproj_fwd(ya, yb, wo, x2, post_g3, layer):
    tokens = x2.shape[0]
    tm = _token_tile(tokens)

    def body(ya_ref, yb_ref, woa_ref, wob_ref, x_ref, g_ref, y_ref, xn_ref):
        y = _mm(ya_ref[...], woa_ref[...]) + _mm(yb_ref[...], wob_ref[...])
        r = lax.rsqrt(jnp.mean(y * y, axis=-1, keepdims=True) + NORM_EPS)
        y_ref[...] = y.astype(BF16)
        xn_ref[...] = x_ref[...] + y * r * g_ref[...]

    half = D_MODEL // 2
    blocks = [2 * _nbytes((tm, half), BF16), 2 * _nbytes((half, D_MODEL), BF16), 2 * _nbytes((tm, D_MODEL), F32),
              _nbytes((tm, D_MODEL), BF16)]
    return pl.pallas_call(
        body, name=f"outproj_fwd_{layer}", grid=(tokens // tm,),
        in_specs=[pl.BlockSpec((tm, half), lambda i: (i, 0)),
                  pl.BlockSpec((tm, half), lambda i: (i, 0)),
                  pl.BlockSpec((None, half, D_MODEL), lambda i: (layer, 0, 0)),
                  pl.BlockSpec((None, half, D_MODEL), lambda i: (layer, 1, 0)),
                  pl.BlockSpec((tm, D_MODEL), lambda i: (i, 0)),
                  pl.BlockSpec((None, 1, D_MODEL), lambda i: (layer, 0, 0))],
        out_specs=[pl.BlockSpec((tm, D_MODEL), lambda i: (i, 0)),
                   pl.BlockSpec((tm, D_MODEL), lambda i: (i, 0))],
        out_shape=[pltpu.HBM((tokens,D_MODEL), BF16),
                   pltpu.HBM((tokens,D_MODEL), F32)],
        compiler_params=pltpu.CompilerParams(dimension_semantics=("parallel",),
                                             vmem_limit_bytes=_vmem_limit(blocks)),
    )(_hbm(ya), _hbm(yb), _hbm(wo), _hbm(wo), _hbm(x2), post_g3)


def _loss_head(xl, tgt):
    tokens = xl.shape[0]
    tm = _token_tile(tokens)
    steps = tokens // tm

    def body(x_ref, t_ref, g_ref, loss_ref, acc_ref):
        i = pl.program_id(0)

        @pl.when(i == 0)
        def _():
            acc_ref[...] = jnp.zeros_like(acc_ref)

        err = x_ref[...] - t_ref[...]
        g_ref[...] = err * (1.0 / D_MODEL)
        acc_ref[...] += jnp.sum(err * err, axis=0, keepdims=True)

        @pl.when(i == steps - 1)
        def _():
            total = jnp.sum(acc_ref[...], axis=-1, keepdims=True) * (0.5 / D_MODEL)
            loss_ref[...] = jnp.broadcast_to(total, loss_ref.shape)

    blocks = [3 * _nbytes((tm, D_MODEL), F32)]
    return pl.pallas_call(
        body, name="loss_head", grid=(steps,),
        in_specs=[pl.BlockSpec((tm, D_MODEL), lambda i: (i, 0)),
                  pl.BlockSpec((tm, D_MODEL), lambda i: (i, 0))],
        out_specs=[pl.BlockSpec((tm, D_MODEL), lambda i: (i, 0)),
                   pl.BlockSpec((1, LANES), lambda i: (0, 0))],
        out_shape=[pltpu.HBM((tokens,D_MODEL), F32),
                   pltpu.HBM((1,LANES), F32)],
        scratch_shapes=[pltpu.VMEM((1, D_MODEL), F32)],
        compiler_params=pltpu.CompilerParams(dimension_semantics=("arbitrary",),
                                             vmem_limit_bytes=_vmem_limit(blocks)),
    )(_hbm(xl), _hbm(tgt))


def _outproj_bwd(g, y, ya, yb, wo, post_g3, layer):
    tokens = g.shape[0]
    tm = _token_tile(tokens)
    half = D_MODEL // 2

    def body(g_ref, y_ref, ya_ref, yb_ref, woa_ref, wob_ref, pg_ref, dya_ref, dyb_ref, dwo_ref, dpg_ref):
        @pl.when(pl.program_id(0) == 0)
        def _():
            dwo_ref[...] = jnp.zeros_like(dwo_ref)
            dpg_ref[...] = jnp.zeros_like(dpg_ref)

        gv = g_ref[...]
        yf = y_ref[...].astype(F32)
        r = lax.rsqrt(jnp.mean(yf * yf, axis=-1, keepdims=True) + NORM_EPS)
        yhat = yf * r
        gg = gv * pg_ref[...]
        dy = r * (gg - yhat * jnp.mean(gg * yhat, axis=-1, keepdims=True))
        dpg_ref[...] += jnp.sum(gv * yhat, axis=0, keepdims=True)
        dyb16 = dy.astype(BF16)
        dya_ref[...] = _mm_nt(dyb16, woa_ref[...]).astype(BF16)
        dyb_ref[...] = _mm_nt(dyb16, wob_ref[...]).astype(BF16)
        dwo_ref[0:half, :] += _mm_tn(ya_ref[...], dyb16)
        dwo_ref[half:D_MODEL, :] += _mm_tn(yb_ref[...], dyb16)

    blocks = [_nbytes((tm, D_MODEL), F32), _nbytes((tm, D_MODEL), BF16), 4 * _nbytes((tm, half), BF16),
              2 * _nbytes((half, D_MODEL), BF16), _nbytes((D_MODEL, D_MODEL), F32)]
    return pl.pallas_call(
        body, name=f"outproj_bwd_{layer}", grid=(tokens // tm,),
        in_specs=[pl.BlockSpec((tm, D_MODEL), lambda i: (i, 0)),
                  pl.BlockSpec((tm, D_MODEL), lambda i: (i, 0)),
                  pl.BlockSpec((tm, half), lambda i: (i, 0)),
                  pl.BlockSpec((tm, half), lambda i: (i, 0)),
                  pl.BlockSpec((None, half, D_MODEL), lambda i: (layer, 0, 0)),
                  pl.BlockSpec((None, half, D_MODEL), lambda i: (layer, 1, 0)),
                  pl.BlockSpec((None, 1, D_MODEL), lambda i: (layer, 0, 0))],
        out_specs=[pl.BlockSpec((tm, half), lambda i: (i, 0)),
                   pl.BlockSpec((tm, half), lambda i: (i, 0)),
                   pl.BlockSpec((D_MODEL, D_MODEL), lambda i: (0, 0)),
                   pl.BlockSpec((1, D_MODEL), lambda i: (0, 0))],
        out_shape=[pltpu.HBM((tokens,half), BF16),
                   pltpu.HBM((tokens,half), BF16),
                   pltpu.HBM((D_MODEL,D_MODEL), F32),
                   pltpu.HBM((1,D_MODEL), F32)],
        compiler_params=pltpu.CompilerParams(dimension_semantics=("arbitrary",),
                                             vmem_limit_bytes=_vmem_limit(blocks)),
    )(_hbm(g), _hbm(y), _hbm(ya), _hbm(yb), _hbm(wo), _hbm(wo), post_g3)


def _gmlp_bwd(pa, dya, ln_g3, ln_b3, wm, wmt, bsb, layer):
    tokens = pa.shape[0]
    tc = _token_tile(tokens)
    steps = tokens // tc

    def body(pa_ref, dya_ref, lg_ref, lb_ref, wm_ref, wmt_ref, bsb_ref,
             da_ref, dws_ref, dbs_ref, dlg_ref, dlb_ref, dbsb_ref):
        i = pl.program_id(0)

        @pl.when(i == 0)
        def _():
            dws_ref[...] = jnp.zeros_like(dws_ref)
            dlg_ref[...] = jnp.zeros_like(dlg_ref)
            dlb_ref[...] = jnp.zeros_like(dlb_ref)
            dbsb_ref[...] = jnp.zeros_like(dbsb_ref)

        ln_g = lg_ref[...]
        low = lax.broadcasted_iota(jnp.int32, (CHUNK, LANES), 1) < HEAD_DIM

        def chunk(ci, carry):
            rows = pl.ds(pl.multiple_of(ci * CHUNK, CHUNK), CHUNK)
            pu, pv, pz, u, vhat, rstd, vnb, mixed, sg = _gmlp_forward_chunk(
                pa_ref[rows, :], ln_g, lb_ref[...], wm_ref, bsb_ref[...])
            dy = dya_ref[rows, :].astype(F32)
            sz = pz * sg
            du = dy * mixed * sz
            dmixed = dy * u * sz
            dz = dy * u * mixed * (sg * (1.0 + pz * (1.0 - sg)))
            dbsb_ref[...] += dmixed
            dmb = dmixed.astype(BF16)
            zero = jnp.zeros((CHUNK, LANES), BF16)
            parts = []
            for p in range(A_GROUPS // 2):
                dmp = dmb[:, LANES * p:LANES * (p + 1)]
                vp = vnb[:, LANES * p:LANES * (p + 1)]
                parts.append(jnp.where(low, _mm(wmt_ref[2 * p], dmp), _mm(wmt_ref[2 * p + 1], dmp)))
                dws_ref[2 * p] += _mm_nt(jnp.where(low, dmp, zero), vp)
                dws_ref[2 * p + 1] += _mm_nt(jnp.where(low, zero, dmp), vp)
            dvn = jnp.concatenate(parts, axis=1)
            dlg_ref[...] += jnp.sum(dvn * vhat, axis=0, keepdims=True)
            dlb_ref[...] += jnp.sum(dvn, axis=0, keepdims=True)
            dvh = dvn * ln_g
            dvv = rstd * (dvh - jnp.mean(dvh, axis=-1, keepdims=True)
                          - vhat * jnp.mean(dvh * vhat, axis=-1, keepdims=True))
            da_ref[rows, 0:A_WIDTH] = (du * _gelu_grad(pu)).astype(BF16)
            da_ref[rows, A_WIDTH:2 * A_WIDTH] = (dvv * _gelu_grad(pv)).astype(BF16)
            da_ref[rows, 2 * A_WIDTH:3 * A_WIDTH] = dz.astype(BF16)
            return carry

        lax.fori_loop(0, tc // CHUNK, chunk, 0)

        @pl.when(i == steps - 1)
        def _():
            causal = (lax.broadcasted_iota(jnp.int32, (CHUNK, CHUNK), 0)
                      >= lax.broadcasted_iota(jnp.int32, (CHUNK, CHUNK), 1))
            for h in range(A_GROUPS):
                dws_ref[h] = jnp.where(causal, dws_ref[h], 0.0)
            acc = dbsb_ref[...]
            lane_full = lax.broadcasted_iota(jnp.int32, (CHUNK, A_WIDTH), 1)
            lane_out = lax.broadcasted_iota(jnp.int32, (CHUNK, LANES), 1)
            out = jnp.zeros((CHUNK, LANES), F32)
            for h in range(A_GROUPS):
                in_group = jnp.logical_and(lane_full >= HEAD_DIM * h, lane_full < HEAD_DIM * (h + 1))
                s = jnp.sum(jnp.where(in_group, acc, 0.0), axis=-1, keepdims=True)
                out = jnp.where(lane_out == h, s, out)
            dbs_ref[...] = out

    blocks = [_nbytes((tc, PA_WIDTH), BF16), _nbytes((tc, A_WIDTH), BF16), 2 * _nbytes((A_GROUPS, CHUNK, CHUNK), BF16),
              _nbytes((CHUNK, A_WIDTH), F32), _nbytes((tc, PA_WIDTH), BF16), _nbytes((A_GROUPS, CHUNK, CHUNK), F32),
              _nbytes((CHUNK, LANES), F32)]
    return pl.pallas_call(
        body, name=f"gmlp_bwd_{layer}", grid=(steps,),
        in_specs=[pl.BlockSpec((tc, PA_WIDTH), lambda i: (i, 0)),
                  pl.BlockSpec((tc, A_WIDTH), lambda i: (i, 0)),
                  pl.BlockSpec((None, 1, A_WIDTH), lambda i: (layer, 0, 0)),
                  pl.BlockSpec((None, 1, A_WIDTH), lambda i: (layer, 0, 0)),
                  pl.BlockSpec((None, A_GROUPS, CHUNK, CHUNK), lambda i: (layer, 0, 0, 0)),
                  pl.BlockSpec((None, A_GROUPS, CHUNK, CHUNK), lambda i: (layer, 0, 0, 0)),
                  pl.BlockSpec((None, CHUNK, A_WIDTH), lambda i: (layer, 0, 0))],
        out_specs=[pl.BlockSpec((tc, PA_WIDTH), lambda i: (i, 0)),
                   pl.BlockSpec((A_GROUPS, CHUNK, CHUNK), lambda i: (0, 0, 0)),
                   pl.BlockSpec((CHUNK, LANES), lambda i: (0, 0)),
                   pl.BlockSpec((1, A_WIDTH), lambda i: (0, 0)),
                   pl.BlockSpec((1, A_WIDTH), lambda i: (0, 0))],
        out_shape=[pltpu.HBM((tokens,PA_WIDTH), BF16),
                   pltpu.HBM((A_GROUPS, CHUNK, CHUNK), F32),
                   pltpu.HBM((CHUNK, LANES), F32),
                   pltpu.HBM((1,A_WIDTH), F32),
                   pltpu.HBM((1,A_WIDTH), F32)],
        scratch_shapes=[pltpu.VMEM((CHUNK, A_WIDTH), F32)],
        compiler_params=pltpu.CompilerParams(dimension_semantics=("arbitrary",),
                                             vmem_limit_bytes=_vmem_limit(blocks, _nbytes((CHUNK, A_WIDTH), F32))),
    )(_hbm(pa), _hbm(dya), ln_g3, ln_b3, wm, wmt, bsb)


def _attn_bwd(pb, dyb, bias, sinks, b_loc, layer):
    tokens = pb.shape[0]
    nb = tokens // b_loc // BLOCK
    qz_width = 2 * B_WIDTH
    tq = _token_tile(tokens)
    per_tile = tq // BLOCK
    nt = tokens // tq

    def body(sink_ref, cur_ref, prev_ref, dyb_ref, bias_ref, dqz_ref, dkv_ref, dbias_ref, dsink_ref,
             kv_ref, acc_ref):
        t = pl.program_id(0)

        @pl.when(t == 0)
        def _():
            dbias_ref[...] = jnp.zeros_like(dbias_ref)
            dsink_ref[...] = jnp.zeros_like(dsink_ref)
            acc_ref[0:BLOCK, :] = jnp.zeros((BLOCK, 2 * KV_WIDTH), F32)

        @pl.when(t < nt)
        def _():
            acc_ref[BLOCK:, :] = jnp.zeros((tq, 2 * KV_WIDTH), F32)
            _fill_kv(kv_ref, prev_ref, cur_ref)
            tri = _band_masks()
            low = lax.broadcasted_iota(jnp.int32, (BLOCK, LANES), 1) < HEAD_DIM
            low_kv = lax.broadcasted_iota(jnp.int32, (2 * BLOCK, LANES), 1) < HEAD_DIM

            def block(i, carry):
                start = pl.multiple_of(i * BLOCK, BLOCK)
                rows = pl.ds(start, BLOCK)
                first = lax.rem(t * per_tile + i, nb) == 0
                kill = jnp.logical_and(first, jnp.logical_not(tri))
                kv = kv_ref[pl.ds(start, 2 * BLOCK), :]
                k_ops = _kv_variants(kv[:, 0:KV_WIDTH])
                v_ops = _kv_variants(kv[:, KV_WIDTH:2 * KV_WIDTH])
                probs, sink_p, p16, qs, dos, delta, ds16 = {}, {}, {}, {}, {}, {}, {}
                for kvh in range(2):
                    pairs = (2 * kvh, 2 * kvh + 1)
                    qs[kvh] = jnp.concatenate([cur_ref[rows, LANES * p:LANES * (p + 1)] for p in pairs], axis=0)
                    for j in range(2):
                        sf = _mm_nt(qs[kvh], k_ops[kvh][j])
                        for r, p in enumerate(pairs):
                            hd = 2 * p + j
                            probs[hd], sink_p[hd] = _attn_probs(sf[BLOCK * r:BLOCK * (r + 1)], bias_ref[hd],
                                                                sink_ref[layer, hd], tri, kill)
                            p16[hd] = _unwrap16(probs[hd], tri)
                for kvh in range(2):
                    pairs = (2 * kvh, 2 * kvh + 1)
                    out = jnp.zeros((2 * BLOCK, LANES), F32)
                    for j in range(2):
                        out = out + _mm(jnp.concatenate([p16[2 * p + j] for p in pairs], axis=0), v_ops[kvh][j])
                    d_outs = []
                    for r, p in enumerate(pairs):
                        bz = cur_ref[rows, BZ_OFF + LANES * p:BZ_OFF + LANES * (p + 1)].astype(F32)
                        sg = _sigmoid(bz)
                        dyp = dyb_ref[rows, LANES * p:LANES * (p + 1)].astype(F32)
                        out_p = out[BLOCK * r:BLOCK * (r + 1)]
                        d_out = dyp * (bz * sg)
                        dqz_ref[rows, B_WIDTH + LANES * p:B_WIDTH + LANES * (p + 1)] = (
                            dyp * out_p * (sg * (1.0 + bz * (1.0 - sg)))).astype(BF16)
                        dod = d_out * out_p
                        delta[2 * p] = jnp.sum(jnp.where(low, dod, 0.0), axis=-1, keepdims=True)
                        delta[2 * p + 1] = jnp.sum(jnp.where(low, 0.0, dod), axis=-1, keepdims=True)
                        d_outs.append(d_out.astype(BF16))
                    dos[kvh] = jnp.concatenate(d_outs, axis=0)
                for kvh in range(2):
                    pairs = (2 * kvh, 2 * kvh + 1)
                    for j in range(2):
                        dpf = _mm_nt(dos[kvh], v_ops[kvh][j])
                        for r, p in enumerate(pairs):
                            hd = 2 * p + j
                            ds = probs[hd] * (_wrap(dpf[BLOCK * r:BLOCK * (r + 1)], tri) - delta[hd])
                            dsink_ref[hd:hd + 1, :] += jnp.sum(-sink_p[hd] * delta[hd], axis=0, keepdims=True)
                            dbias_ref[hd] += ds
                            ds16[hd] = _unwrap16(ds, tri)
                dk_acc = [[None, None], [None, None]]
                dv_acc = [[None, None], [None, None]]
                for kvh in range(2):
                    pairs = (2 * kvh, 2 * kvh + 1)
                    dq = jnp.zeros((2 * BLOCK, LANES), F32)
                    for j in range(2):
                        dss = jnp.concatenate([ds16[2 * p + j] for p in pairs], axis=0)
                        pss = jnp.concatenate([p16[2 * p + j] for p in pairs], axis=0)
                        dq = dq + _mm(dss, k_ops[kvh][j])
                        dk_acc[kvh][j] = _mm_tn(dss, qs[kvh])
                        dv_acc[kvh][j] = _mm_tn(pss, dos[kvh])
                    for r, p in enumerate(pairs):
                        dqz_ref[rows, LANES * p:LANES * (p + 1)] = (dq[BLOCK * r:BLOCK * (r + 1)] * SCALE).astype(BF16)

                def fold(acc):
                    return jnp.where(low_kv,
                                     acc[0][0] + pltpu.roll(acc[0][1], HEAD_DIM, 1),
                                     pltpu.roll(acc[1][0], HEAD_DIM, 1) + acc[1][1])

                acc_ref[pl.ds(start, 2 * BLOCK), :] += jnp.concatenate(
                    [fold(dk_acc) * SCALE, fold(dv_acc)], axis=1)
                return carry

            lax.fori_loop(0, per_tile, block, 0)
            dkv_ref[...] = acc_ref[0:tq, :].astype(BF16)
            acc_ref[0:BLOCK, :] = acc_ref[tq:tq + BLOCK, :]

        @pl.when(t == nt)
        def _():
            dkv_ref[0:BLOCK, :] = acc_ref[0:BLOCK, :].astype(BF16)
            dkv_ref[BLOCK:, :] = jnp.zeros((tq - BLOCK, 2 * KV_WIDTH), BF16)

    def cur_map(t):
        return (jnp.minimum(t, nt - 1), 0)

    def prev_map(t):
        return (jnp.maximum(jnp.minimum(t, nt - 1) * per_tile - 1, 0), K_OFF // (2 * KV_WIDTH))

    blocks = [_nbytes((tq, PB_WIDTH), BF16), _nbytes((BLOCK, 2 * KV_WIDTH), BF16), _nbytes((tq, B_WIDTH), BF16),
              2 * _nbytes((B_HEADS, BLOCK, BLOCK), F32), _nbytes((tq, qz_width), BF16),
              _nbytes((tq, 2 * KV_WIDTH), BF16), _nbytes((B_HEADS, LANES), F32)]
    scratch = _nbytes((tq + BLOCK, 2 * KV_WIDTH), BF16) + _nbytes((tq + BLOCK, 2 * KV_WIDTH), F32)
    return pl.pallas_call(
        body, name=f"attn_bwd_{layer}", grid=(nt + 1,),
        in_specs=[pl.BlockSpec(memory_space=pltpu.SMEM),
                  pl.BlockSpec((tq, PB_WIDTH), cur_map),
                  pl.BlockSpec((BLOCK, 2 * KV_WIDTH), prev_map),
                  pl.BlockSpec((tq, B_WIDTH), cur_map),
                  pl.BlockSpec((B_HEADS, BLOCK, BLOCK), lambda t: (0, 0, 0))],
        out_specs=[pl.BlockSpec((tq, qz_width), cur_map),
                   pl.BlockSpec((tq, 2 * KV_WIDTH), lambda t: (t, 0)),
                   pl.BlockSpec((B_HEADS, BLOCK, BLOCK), lambda t: (0, 0, 0)),
                   pl.BlockSpec((B_HEADS, LANES), lambda t: (0, 0))],
        out_shape=[pltpu.HBM((tokens, qz_width), BF16),
                   pltpu.HBM((tokens + tq, 2 * KV_WIDTH), BF16),
                   pltpu.HBM((B_HEADS, BLOCK, BLOCK), F32),
                   pltpu.HBM((B_HEADS, LANES), F32)],
        scratch_shapes=[pltpu.VMEM((tq + BLOCK, 2 * KV_WIDTH), BF16),
                        pltpu.VMEM((tq + BLOCK, 2 * KV_WIDTH), F32)],
        compiler_params=pltpu.CompilerParams(dimension_semantics=("arbitrary",),
                                             vmem_limit_bytes=_vmem_limit(blocks, scratch)),
    )(sinks, _hbm(pb), _hbm(pb), _hbm(dyb), _hbm(bias))


def _inproj_bwd_dx(da, dqz, dkv, wa, wb, x2, g, pre_g3, layer):
    tokens = x2.shape[0]
    tm = _token_tile(tokens)

    def body(da_ref, dqz_ref, dkv_ref, wa_ref, wb_ref, x_ref, g_ref, pg_ref, gn_ref, dpg_ref):
        @pl.when(pl.program_id(0) == 0)
        def _():
            dpg_ref[...] = jnp.zeros_like(dpg_ref)

        dh = _mm_nt(da_ref[...], wa_ref[...])
        dh += _mm_nt(dqz_ref[:, 0:B_WIDTH], wb_ref[:, 0:B_WIDTH])
        dh += _mm_nt(dkv_ref[...], wb_ref[:, K_OFF:BZ_OFF])
        dh += _mm_nt(dqz_ref[:, B_WIDTH:2 * B_WIDTH], wb_ref[:, BZ_OFF:PB_WIDTH])
        x = x_ref[...]
        r = lax.rsqrt(jnp.mean(x * x, axis=-1, keepdims=True) + NORM_EPS)
        xhat = x * r
        dhg = dh * pg_ref[...]
        dpg_ref[...] += jnp.sum(dh * xhat, axis=0, keepdims=True)
        gn_ref[...] = g_ref[...] + r * (dhg - xhat * jnp.mean(dhg * xhat, axis=-1, keepdims=True))

    blocks = [_nbytes((tm, PA_WIDTH), BF16), _nbytes((tm, 2 * B_WIDTH), BF16), _nbytes((tm, 2 * KV_WIDTH), BF16),
              _nbytes((D_MODEL, IN_WIDTH), BF16), 3 * _nbytes((tm, D_MODEL), F32)]
    return pl.pallas_call(
        body, name=f"inproj_bwd_dx_{layer}", grid=(tokens // tm,),
        in_specs=[pl.BlockSpec((tm, PA_WIDTH), lambda i: (i, 0)),
                  pl.BlockSpec((tm, 2 * B_WIDTH), lambda i: (i, 0)),
                  pl.BlockSpec((tm, 2 * KV_WIDTH), lambda i: (i, 0)),
                  pl.BlockSpec((None, D_MODEL, PA_WIDTH), lambda i: (layer, 0, 0)),
                  pl.BlockSpec((None, D_MODEL, PB_WIDTH), lambda i: (layer, 0, 0)),
                  pl.BlockSpec((tm, D_MODEL), lambda i: (i, 0)),
                  pl.BlockSpec((tm, D_MODEL), lambda i: (i, 0)),
                  pl.BlockSpec((None, 1, D_MODEL), lambda i: (layer, 0, 0))],
        out_specs=[pl.BlockSpec((tm, D_MODEL), lambda i: (i, 0)),
                   pl.BlockSpec((1, D_MODEL), lambda i: (0, 0))],
        out_shape=[pltpu.HBM((tokens,D_MODEL), F32),
                   pltpu.HBM((1,D_MODEL), F32)],
        compiler_params=pltpu.CompilerParams(dimension_semantics=("arbitrary",),
                                             vmem_limit_bytes=_vmem_limit(blocks)),
    )(_hbm(da), _hbm(dqz), _hbm(dkv), _hbm(wa), _hbm(wb), _hbm(x2), _hbm(g), pre_g3)


def _inproj_bwd_dw(h, da, dqz, dkv, layer):
    tokens = h.shape[0]
    tm = _token_tile(tokens)

    def body(h_ref, da_ref, dqz_ref, dkv_ref, dwa_ref, dwb_ref):
        @pl.when(pl.program_id(0) == 0)
        def _():
            dwa_ref[...] = jnp.zeros_like(dwa_ref)
            dwb_ref[...] = jnp.zeros_like(dwb_ref)

        hv = h_ref[...]
        dwa_ref[...] += _mm_tn(hv, da_ref[...])
        dwb_ref[:, 0:B_WIDTH] += _mm_tn(hv, dqz_ref[:, 0:B_WIDTH])
        dwb_ref[:, K_OFF:BZ_OFF] += _mm_tn(hv, dkv_ref[...])
        dwb_ref[:, BZ_OFF:PB_WIDTH] += _mm_tn(hv, dqz_ref[:, B_WIDTH:2 * B_WIDTH])

    blocks = [_nbytes((tm, D_MODEL), BF16), _nbytes((tm, PA_WIDTH), BF16), _nbytes((tm, 2 * B_WIDTH), BF16),
              _nbytes((tm, 2 * KV_WIDTH), BF16), _nbytes((D_MODEL, IN_WIDTH), F32)]
    return pl.pallas_call(
        body, name=f"inproj_bwd_dw_{layer}", grid=(tokens // tm,),
        in_specs=[pl.BlockSpec((tm, D_MODEL), lambda i: (i, 0)),
                  pl.BlockSpec((tm, PA_WIDTH), lambda i: (i, 0)),
                  pl.BlockSpec((tm, 2 * B_WIDTH), lambda i: (i, 0)),
                  pl.BlockSpec((tm, 2 * KV_WIDTH), lambda i: (i, 0))],
        out_specs=[pl.BlockSpec((D_MODEL, PA_WIDTH), lambda i: (0, 0)),
                   pl.BlockSpec((D_MODEL, PB_WIDTH), lambda i: (0, 0))],
        out_shape=[pltpu.HBM((D_MODEL,PA_WIDTH), F32),
                   pltpu.HBM((D_MODEL,PB_WIDTH), F32)],
        compiler_params=pltpu.CompilerParams(dimension_semantics=("arbitrary",),
                                             vmem_limit_bytes=_vmem_limit(blocks)),
    )(_hbm(h), _hbm(da), _hbm(dqz), _hbm(dkv))


def _rel_bias_grad(dbias_layers, bucket):
    n = len(dbias_layers)

    def body(*refs):
        bucket_ref, out_ref = refs[n], refs[n + 1]
        bk = bucket_ref[...]
        row = lax.broadcasted_iota(jnp.int32, (REL_BUCKETS, LANES), 0)
        lane = lax.broadcasted_iota(jnp.int32, (REL_BUCKETS, LANES), 1)
        out = jnp.zeros((REL_BUCKETS, LANES), F32)
        for h in range(B_HEADS):
            tot = refs[0][h]
            for a in range(1, n):
                tot = tot + refs[a][h]

            def one(b, acc, tot=tot, h=h):
                s = jnp.sum(jnp.where(bk == b, tot, 0.0), axis=-1, keepdims=True)
                s = jnp.sum(s, axis=0, keepdims=True)
                return jnp.where(jnp.logical_and(row == b, lane == h), s, acc)

            out = lax.fori_loop(0, REL_BUCKETS, one, out)
        out_ref[...] = out

    vm = pl.BlockSpec(memory_space=pltpu.VMEM)
    return pl.pallas_call(
        body, name="rel_bias_grad",
        out_shape=jax.ShapeDtypeStruct((REL_BUCKETS, LANES), F32),
        in_specs=[vm] * (n + 1), out_specs=vm,
    )(*dbias_layers, bucket)


def _reduce_adamw(slots, w, m, v, name):
    rows, cols = w.shape
    tr = _row_tile(rows)
    c1 = 1.0 / (1.0 - ADAM_B1 ** ADAM_STEP)
    c2 = 1.0 / (1.0 - ADAM_B2 ** ADAM_STEP)

    def body(s_ref, w_ref, m_ref, v_ref, g_ref, d_ref, nm_ref, nv_ref):
        g = s_ref[0].astype(F32)
        for i in range(1, N_DEV):
            g = g + s_ref[i].astype(F32)
        nm = ADAM_B1 * m_ref[...] + (1.0 - ADAM_B1) * g
        nv = ADAM_B2 * v_ref[...] + (1.0 - ADAM_B2) * (g * g)
        g_ref[...] = g
        nm_ref[...] = nm
        nv_ref[...] = nv
        d_ref[...] = -ADAM_LR * ((nm * c1) / (jnp.sqrt(nv * c2) + ADAM_EPS) + ADAM_WD * w_ref[...])

    blocks = [_nbytes((N_DEV, tr, cols), slots.dtype), 7 * _nbytes((tr, cols), F32)]
    tile = pl.BlockSpec((tr, cols), lambda i: (i, 0))
    return pl.pallas_call(
        body, name=name, grid=(rows // tr,),
        in_specs=[pl.BlockSpec((N_DEV, tr, cols), lambda i: (0, i, 0)), tile, tile, tile],
        out_specs=[tile] * 4,
        out_shape=[pltpu.HBM((rows, cols), F32)] * 4,
        compiler_params=pltpu.CompilerParams(dimension_semantics=("parallel",),
                                             vmem_limit_bytes=_vmem_limit(blocks)),
    )(_hbm(slots), _hbm(w), _hbm(m), _hbm(v))


_SMALL = ("w_spatial", "pre_norm_g", "post_norm_g", "ln_v_g", "ln_v_b", "b_spatial", "sinks", "rel_bias")


def _pack_small(parts):
    slabs = []
    for name in _SMALL:
        flat = parts[name].astype(F32).reshape(-1)
        pad = (-flat.shape[0]) % (8 * LANES)
        slabs.append(jnp.pad(flat, (0, pad)).reshape(-1, LANES))
    return jnp.concatenate(slabs, axis=0)


def _unpack_small(slab, shapes):
    out, row = {}, 0
    for name in _SMALL:
        size = int(np.prod(shapes[name]))
        rows = -(-size // (8 * LANES)) * 8
        out[name] = slab[row:row + rows].reshape(-1)[:size].reshape(shapes[name])
        row += rows
    return out


def kernel(x, pre_norm_g, w_in, ln_v_g, ln_v_b, w_spatial, b_spatial, sinks, rel_bias, w_out, post_norm_g, loss_target, m_pre_norm_g, m_w_in, m_ln_v_g, m_ln_v_b, m_w_spatial, m_b_spatial, m_sinks, m_rel_bias, m_w_out, m_post_norm_g, v_pre_norm_g, v_w_in, v_ln_v_g, v_ln_v_b, v_w_spatial, v_b_spatial, v_sinks, v_rel_bias, v_w_out, v_post_norm_g):
    b_loc, seq, _ = x.shape
    tokens = b_loc * seq
    depth = w_in.shape[0]
    in_shard = w_in.shape[2]
    out_shard = w_out.shape[1]
    assert in_shard * N_DEV == IN_WIDTH and out_shard * N_DEV == D_MODEL and seq % BLOCK == 0

    gin, gout = _all_gather([w_in.astype(BF16).reshape(depth * D_MODEL, in_shard),
                             w_out.astype(BF16).reshape(depth * out_shard, D_MODEL)], "weights_all_gather")
    w_full = gin.reshape(N_DEV, depth, D_MODEL, in_shard).transpose(1, 2, 0, 3).reshape(depth, D_MODEL, IN_WIDTH)
    wa, wb = w_full[:, :, :PA_WIDTH], w_full[:, :, PA_WIDTH:]
    wo = gout.reshape(N_DEV, depth, out_shard, D_MODEL).transpose(1, 0, 2, 3).reshape(depth, D_MODEL, D_MODEL)

    causal = jnp.tril(jnp.ones((CHUNK, CHUNK), dtype=bool))
    wm = jnp.where(causal, w_spatial, 0.0).astype(BF16)
    wmt = _hbm(jnp.swapaxes(wm, -1, -2))
    wm = _hbm(wm)
    bsb = _hbm(jnp.repeat(jnp.swapaxes(b_spatial, -1, -2), HEAD_DIM, axis=-1))
    pre_g3 = _hbm(pre_norm_g.reshape(depth, 1, D_MODEL))
    post_g3 = _hbm(post_norm_g.reshape(depth, 1, D_MODEL))
    ln_g3 = _hbm(ln_v_g.reshape(depth, 1, A_WIDTH))
    ln_b3 = _hbm(ln_v_b.reshape(depth, 1, A_WIDTH))
    bucket = jnp.asarray(_bucket_table())
    bias = _bias_table(rel_bias, bucket)

    xs, saved = [x.reshape(tokens, D_MODEL)], []
    for layer in range(depth):
        h, pa, pb = _inproj_fwd(xs[-1], pre_g3, wa, wb, layer)
        ya = _gmlp_fwd(pa, ln_g3, ln_b3, wm, bsb, layer)
        yb = _attn_fwd(pb, bias, sinks, b_loc, layer)
        y, x_next = _outproj_fwd(ya, yb, wo, xs[-1], post_g3, layer)
        saved.append((h, pa, pb, ya, yb, y))
        xs.append(x_next)
    g, loss_part = _loss_head(xs[-1], loss_target.reshape(tokens, D_MODEL))
    loss = lax.psum(loss_part[0, 0], ("x", "y", "c"))

    grads = {name: [None] * depth for name in ("w_spatial", "pre_norm_g", "post_norm_g", "ln_v_g", "ln_v_b",
                                               "b_spatial", "sinks", "dbias")}
    me = _slot(lax.axis_index("x"), lax.axis_index("y"), lax.axis_index("c"))
    zones = [lax.empty((N_DEV, depth * D_MODEL, in_shard), BF16), lax.empty((N_DEV, depth * out_shard, D_MODEL), BF16)]
    started, own_in, own_out = [], [None] * depth, [None] * depth
    post_g_bwd = post_g3
    for layer in reversed(range(depth)):
        h, pa, pb, ya, yb, y = saved[layer]
        dya, dyb, dwo, dpost = _outproj_bwd(g, y, ya, yb, wo, post_g_bwd, layer)
        da, dws, dbs, dlg, dlb = _gmlp_bwd(pa, dya, ln_g3, ln_b3, wm, wmt, bsb, layer)
        dqz, dkv_shifted, grads["dbias"][layer], dsink = _attn_bwd(pb, dyb, bias, sinks, b_loc, layer)
        dkv = dkv_shifted[BLOCK:BLOCK + tokens]
        g, dpre = _inproj_bwd_dx(da, dqz, dkv, wa, wb, xs[layer], g, pre_g3, layer)
        dwa, dwb = _inproj_bwd_dw(h, da, dqz, dkv, layer)
        send_in = jnp.concatenate([dwa, dwb], axis=-1).astype(BF16).reshape(D_MODEL, N_DEV, in_shard).transpose(1, 0, 2)
        send_out = dwo.astype(BF16).reshape(N_DEV, out_shard, D_MODEL)
        own_in[layer] = lax.dynamic_index_in_dim(send_in, me, 0, keepdims=False)
        own_out[layer] = lax.dynamic_index_in_dim(send_out, me, 0, keepdims=False)
        row_offs = [layer * D_MODEL, layer * out_shard]
        sems, sends, zones, token = _exchange_start([send_in, send_out], zones, row_offs, f"grads_send_{layer}")
        started.append((sems, sends, row_offs))
        post_g_bwd = _hbm(post_norm_g.reshape(depth, 1, D_MODEL) + token[0, 0])
        grads["w_spatial"][layer] = dws
        grads["b_spatial"][layer] = dbs[:, :A_GROUPS].T
        grads["ln_v_g"][layer] = dlg[0]
        grads["ln_v_b"][layer] = dlb[0]
        grads["sinks"][layer] = dsink[:, 0]
        grads["pre_norm_g"][layer] = dpre[0]
        grads["post_norm_g"][layer] = dpost[0]
    grad_x = g.reshape(x.shape)
    drel = _rel_bias_grad(grads["dbias"], bucket)[:, :B_HEADS]

    recv_in, recv_out = _exchange_wait(started, zones, g, "grads_wait")
    recv_in = lax.dynamic_update_index_in_dim(recv_in, jnp.concatenate(own_in, axis=0), me, 0)
    recv_out = lax.dynamic_update_index_in_dim(recv_out, jnp.concatenate(own_out, axis=0), me, 0)
    res_in = _reduce_adamw(recv_in, w_in.reshape(-1, in_shard), m_w_in.reshape(-1, in_shard),
                           v_w_in.reshape(-1, in_shard), "adamw_w_in")
    res_out = _reduce_adamw(recv_out, w_out.reshape(-1, D_MODEL), m_w_out.reshape(-1, D_MODEL),
                            v_w_out.reshape(-1, D_MODEL), "adamw_w_out")
    res_in = [r.reshape(w_in.shape) for r in res_in]
    res_out = [r.reshape(w_out.shape) for r in res_out]

    small_w = dict(w_spatial=w_spatial, pre_norm_g=pre_norm_g, post_norm_g=post_norm_g, ln_v_g=ln_v_g,
                   ln_v_b=ln_v_b, b_spatial=b_spatial, sinks=sinks, rel_bias=rel_bias)
    small_m = dict(w_spatial=m_w_spatial, pre_norm_g=m_pre_norm_g, post_norm_g=m_post_norm_g, ln_v_g=m_ln_v_g,
                   ln_v_b=m_ln_v_b, b_spatial=m_b_spatial, sinks=m_sinks, rel_bias=m_rel_bias)
    small_v = dict(w_spatial=v_w_spatial, pre_norm_g=v_pre_norm_g, post_norm_g=v_post_norm_g, ln_v_g=v_ln_v_g,
                   ln_v_b=v_ln_v_b, b_spatial=v_b_spatial, sinks=v_sinks, rel_bias=v_rel_bias)
    small_g = {name: jnp.stack(grads[name]) for name in _SMALL if name != "rel_bias"}
    small_g["rel_bias"] = drel
    shapes = {name: small_w[name].shape for name in _SMALL}
    (slots,) = _all_gather([_pack_small(small_g)], "small_grads_all_gather")
    res_small = _reduce_adamw(slots, _pack_small(small_w), _pack_small(small_m), _pack_small(small_v), "adamw_small")
    res_small = [_unpack_small(r, shapes) for r in res_small]

    order = ("pre_norm_g", "w_in", "ln_v_g", "ln_v_b", "w_spatial", "b_spatial", "sinks", "rel_bias", "w_out",
             "post_norm_g")
    outs = [loss, grad_x]
    for kind in range(4):
        for name in order:
            if name == "w_in":
                outs.append(res_in[kind])
            elif name == "w_out":
                outs.append(res_out[kind])
            else:
                outs.append(res_small[kind][name])
    return tuple(outs)
```

```python
import math

import numpy as np
import jax
import jax.numpy as jnp
from jax import lax
from jax.experimental import pallas as pl
from jax.experimental.pallas import tpu as pltpu

F32 = jnp.float32
BF16 = jnp.bfloat16

D_MODEL = 1024
A_WIDTH = 512
A_GROUPS = 8
CHUNK = 128
B_HEADS = 8
HEAD_DIM = 64
B_WIDTH = 512
KV_WIDTH = 128
BLOCK = 128
REL_BUCKETS = 32
REL_MAX_DIST = 128
NORM_EPS = 1e-6
PA_WIDTH = 3 * A_WIDTH
PB_WIDTH = 2 * B_WIDTH + 2 * KV_WIDTH
IN_WIDTH = PA_WIDTH + PB_WIDTH
K_OFF, V_OFF, BZ_OFF = B_WIDTH, B_WIDTH + KV_WIDTH, B_WIDTH + 2 * KV_WIDTH
SCALE = HEAD_DIM ** -0.5
NEG = -1e30
N_DEV = 8
LANES = 128

ADAM_LR = 0.001
ADAM_B1 = 0.9
ADAM_B2 = 0.999
ADAM_EPS = 1e-08
ADAM_WD = 0.01
ADAM_STEP = 10

V7X_VMEM_BYTES = 64 * 1024 * 1024
VMEM_TEMP_BYTES = 12 * 1024 * 1024
MESH = pl.DeviceIdType.MESH


def _vmem_limit(block_bytes, scratch_bytes=0):
    need = 2 * sum(block_bytes) + scratch_bytes + VMEM_TEMP_BYTES
    return int(min(need, V7X_VMEM_BYTES - 8 * 1024 * 1024))


def _nbytes(shape, dtype):
    return int(np.prod(shape)) * jnp.dtype(dtype).itemsize


def _token_tile(tokens):
    tile = min(512, tokens // 2)
    assert tokens % tile == 0 and tile % CHUNK == 0, tokens
    return tile


def _row_tile(rows, cap=512):
    best = 8
    for t in range(8, cap + 1, 8):
        if rows % t == 0:
            best = t
    assert rows % best == 0, rows
    return best


def _mm(a, b):
    return lax.dot_general(a, b, (((1,), (0,)), ((), ())), preferred_element_type=F32)


def _mm_nt(a, b):
    return lax.dot_general(a, b, (((1,), (1,)), ((), ())), preferred_element_type=F32)


def _mm_tn(a, b):
    return lax.dot_general(a, b, (((0,), (0,)), ((), ())), preferred_element_type=F32)


_GELU_C = math.sqrt(2.0 / math.pi)


def _gelu(x):
    return 0.5 * x * (1.0 + jnp.tanh(_GELU_C * (x + 0.044715 * (x * x * x))))


def _gelu_grad(x):
    t = jnp.tanh(_GELU_C * (x + 0.044715 * (x * x * x)))
    return 0.5 * (1.0 + t) + 0.5 * x * (1.0 - t * t) * (_GELU_C * (1.0 + 3.0 * 0.044715 * (x * x)))


def _sigmoid(x):
    return 1.0 / (1.0 + jnp.exp(-x))


def _bucket_table():
    q = np.arange(BLOCK)[:, None]
    k = np.arange(BLOCK)[None, :]
    dist = np.where(k <= q, q - k, q + BLOCK - k)
    max_exact = REL_BUCKETS // 2
    safe = np.maximum(dist, 1).astype(np.float32)
    large = max_exact + (np.log(safe / np.float32(max_exact)) / np.float32(math.log(REL_MAX_DIST / max_exact))
                         * np.float32(REL_BUCKETS - max_exact)).astype(np.int32)
    large = np.minimum(large, REL_BUCKETS - 1)
    assert dist.min() >= 0 and dist.max() < BLOCK
    return np.where(dist < max_exact, dist, large).astype(np.int32)


def _hbm(x):
    return pltpu.with_memory_space_constraint(x, pltpu.HBM)


def _slot(px, py, pc):
    return 4 * px + 2 * py + pc


def _all_gather(arrs, name):
    n = len(arrs)

    def body(*refs):
        ins, outs = refs[:n], refs[n:2 * n]
        send_sems, recv_sems, local_sems = refs[2 * n:]
        x, y, c = lax.axis_index("x"), lax.axis_index("y"), lax.axis_index("c")
        me, sibling = (x, y, c), (x, y, 1 - c)
        chips = [(1 - x, y), (x, 1 - y), (1 - x, 1 - y)]

        def copy(a, k, block, to, src=None):
            dst = outs[a].at[_slot(*block)]
            return pltpu.make_async_remote_copy(
                src_ref=dst if src is None else src, dst_ref=dst,
                send_sem=send_sems.at[7 * a + k], recv_sem=recv_sems.at[7 * a + k],
                device_id=to, device_id_type=MESH)

        mine = [pltpu.make_async_copy(ins[a], outs[a].at[_slot(*me)], local_sems.at[a]) for a in range(n)]
        for cp in mine:
            cp.start()
        first = []
        for a in range(n):
            first.append(copy(a, 0, me, sibling, src=ins[a]))
            first += [copy(a, 1 + j, me, (*chip, c), src=ins[a]) for j, chip in enumerate(chips)]
        for cp in first:
            cp.start()
        passed = []
        for j, chip in enumerate(chips):
            for a in range(n):
                copy(a, 1 + j, (*chip, c), me).wait_recv()
                fwd = copy(a, 4 + j, (*chip, c), sibling)
                fwd.start()
                passed.append(fwd)
        for a in range(n):
            copy(a, 0, sibling, me).wait_recv()
            for j, chip in enumerate(chips):
                copy(a, 4 + j, (*chip, 1 - c), me).wait_recv()
        for cp in first + passed:
            cp.wait_send()
        for cp in mine:
            cp.wait()

    any_spec = pl.BlockSpec(memory_space=pl.ANY)
    return pl.pallas_call(
        body, name=name,
        out_shape=[jax.ShapeDtypeStruct((N_DEV,) + a.shape, a.dtype) for a in arrs],
        in_specs=[any_spec] * n, out_specs=[any_spec] * n,
        scratch_shapes=[pltpu.SemaphoreType.DMA((7 * n,)), pltpu.SemaphoreType.DMA((7 * n,)),
                        pltpu.SemaphoreType.DMA((n,))],
    )(*arrs)


_HBM_SPEC = pl.BlockSpec(memory_space=pltpu.HBM)
_SEM_SPEC = pl.BlockSpec(memory_space=pltpu.SEMAPHORE)
_DATAFLOW = pltpu.SideEffectType.DATAFLOW_SIDE_EFFECTING


def _peers():
    x, y, c = lax.axis_index("x"), lax.axis_index("y"), lax.axis_index("c")
    peers = []
    for k in range(1, N_DEV):
        fx, fy, fc = (k >> 2) & 1, (k >> 1) & 1, k & 1
        peers.append((1 - x if fx else x, 1 - y if fy else y, 1 - c if fc else c))
    return (x, y, c), peers


def _exchange_copy(send_ref, land_ref, row_off, src_slot, dst_slot, sems, idx, peer):
    rows = send_ref.shape[-2]
    src = send_ref.at[src_slot] if len(send_ref.shape) == 3 else send_ref
    return pltpu.make_async_remote_copy(
        src_ref=src, dst_ref=land_ref.at[dst_slot, pl.ds(row_off, rows), :],
        send_sem=sems[0].at[idx], recv_sem=sems[1].at[idx], device_id=peer, device_id_type=MESH)


def _exchange_start(sends, lands, row_offs, name):
    n = len(sends)

    def body(*refs):
        ins, zones = refs[:n], refs[n:2 * n]
        sems = refs[2 * n:2 * n + 2]
        token = refs[-1]
        me, peers = _peers()
        for a in range(n):
            for k, peer in enumerate(peers):
                _exchange_copy(ins[a], zones[a], row_offs[a], _slot(*peer), _slot(*me), sems, 7 * a + k, peer).start()
        token[...] = jnp.zeros_like(token)

    arrays = [_hbm(a) for a in list(sends) + list(lands)]
    out = pl.pallas_call(
        body, name=name,
        out_shape=(pltpu.SemaphoreType.DMA((7 * n,)), pltpu.SemaphoreType.DMA((7 * n,)),
                   *[pltpu.HBM(a.shape, a.dtype) for a in arrays], jax.ShapeDtypeStruct((8, LANES), F32)),
        in_specs=[_HBM_SPEC] * (2 * n),
        out_specs=(_SEM_SPEC, _SEM_SPEC, *[_HBM_SPEC] * (2 * n), pl.BlockSpec(memory_space=pltpu.VMEM)),
        input_output_aliases={i: 2 + i for i in range(2 * n)},
        compiler_params=pltpu.CompilerParams(has_side_effects=_DATAFLOW),
    )(*arrays)
    return (out[0], out[1]), list(out[2:2 + n]), list(out[2 + n:2 + 2 * n]), out[-1]


def _exchange_wait(started, lands, after, name):
    n = len(lands)
    flat_sends = [s for _, sends, _ in started for s in sends]
    flat_sems = [s for sems, _, _ in started for s in sems]
    ns = len(flat_sends)

    def body(*refs):
        ins, zones = refs[:ns], refs[ns:ns + n]
        sem_refs = refs[ns + n:ns + n + len(flat_sems)]
        me, peers = _peers()
        pos = 0
        for call, (_, sends, row_offs) in enumerate(started):
            sems = sem_refs[2 * call:2 * call + 2]
            for a in range(len(sends)):
                for k, peer in enumerate(peers):
                    cp = _exchange_copy(ins[pos + a], zones[a], row_offs[a], _slot(*peer), _slot(*peer), sems,
                                        7 * a + k, peer)
                    cp.wait_send()
                    cp.wait_recv()
            pos += len(sends)

    arrays = list(flat_sends) + list(lands)
    out = pl.pallas_call(
        body, name=name,
        out_shape=tuple(pltpu.HBM(a.shape, a.dtype) for a in arrays),
        in_specs=[_HBM_SPEC] * len(arrays) + [_SEM_SPEC] * len(flat_sems) + [pl.BlockSpec(memory_space=pl.ANY)],
        out_specs=tuple([_HBM_SPEC] * len(arrays)),
        input_output_aliases={i: i for i in range(len(arrays))},
        compiler_params=pltpu.CompilerParams(has_side_effects=_DATAFLOW),
    )(*arrays, *flat_sems, after)
    return list(out[ns:])


def _bias_table(rel_bias, bucket):
    def body(rel_ref, bucket_ref, out_ref):
        bk = bucket_ref[...]
        for h in range(B_HEADS):
            def pick(b, acc, h=h):
                return jnp.where(bk == b, rel_ref[b, h], acc)
            out_ref[h] = lax.fori_loop(0, REL_BUCKETS, pick, jnp.zeros((BLOCK, BLOCK), F32))

    return pl.pallas_call(
        body, name="bias_table",
        out_shape=jax.ShapeDtypeStruct((B_HEADS, BLOCK, BLOCK), F32),
        in_specs=[pl.BlockSpec(memory_space=pltpu.SMEM), pl.BlockSpec(memory_space=pltpu.VMEM)],
        out_specs=pl.BlockSpec(memory_space=pltpu.VMEM),
    )(rel_bias, bucket)


def _inproj_fwd(x2, pre_g3, wa, wb, layer):
    tokens = x2.shape[0]
    tm = _token_tile(tokens)

    def body(x_ref, g_ref, wa_ref, wb_ref, h_ref, pa_ref, pb_ref):
        x = x_ref[...]
        r = lax.rsqrt(jnp.mean(x * x, axis=-1, keepdims=True) + NORM_EPS)
        h = (x * r * g_ref[...]).astype(BF16)
        h_ref[...] = h
        pa_ref[...] = _mm(h, wa_ref[...]).astype(BF16)
        pb_ref[...] = _mm(h, wb_ref[...]).astype(BF16)

    blocks = [_nbytes((tm, D_MODEL), F32), _nbytes((D_MODEL, PA_WIDTH), BF16), _nbytes((D_MODEL, PB_WIDTH), BF16),
              _nbytes((tm, D_MODEL), BF16), _nbytes((tm, PA_WIDTH), BF16), _nbytes((tm, PB_WIDTH), BF16)]
    return pl.pallas_call(
        body, name=f"inproj_fwd_{layer}", grid=(tokens // tm,),
        in_specs=[pl.BlockSpec((tm, D_MODEL), lambda i: (i, 0)),
                  pl.BlockSpec((None, 1, D_MODEL), lambda i: (layer, 0, 0)),
                  pl.BlockSpec((D_MODEL, PA_WIDTH), lambda i: (0, 0)),
                  pl.BlockSpec((D_MODEL, PB_WIDTH), lambda i: (0, 0))],
        out_specs=[pl.BlockSpec((tm, D_MODEL), lambda i: (i, 0)),
                   pl.BlockSpec((tm, PA_WIDTH), lambda i: (i, 0)),
                   pl.BlockSpec((tm, PB_WIDTH), lambda i: (i, 0))],
        out_shape=[pltpu.HBM((tokens,D_MODEL), BF16),
                   pltpu.HBM((tokens,PA_WIDTH), BF16),
                   pltpu.HBM((tokens,PB_WIDTH), BF16)],
        compiler_params=pltpu.CompilerParams(dimension_semantics=("parallel",),
                                             vmem_limit_bytes=_vmem_limit(blocks)),
    )(_hbm(x2), pre_g3, _hbm(wa), _hbm(wb))


def _gmlp_forward_chunk(pa, ln_g, ln_b, wm_ref, bsb):
    pu = pa[:, 0:A_WIDTH].astype(F32)
    pv = pa[:, A_WIDTH:2 * A_WIDTH].astype(F32)
    pz = pa[:, 2 * A_WIDTH:3 * A_WIDTH].astype(F32)
    u = _gelu(pu)
    vv = _gelu(pv)
    mu = jnp.mean(vv, axis=-1, keepdims=True)
    xc = vv - mu
    rstd = lax.rsqrt(jnp.mean(xc * xc, axis=-1, keepdims=True) + NORM_EPS)
    vhat = xc * rstd
    vnb = (vhat * ln_g + ln_b).astype(BF16)
    low = lax.broadcasted_iota(jnp.int32, (CHUNK, LANES), 1) < HEAD_DIM
    parts = []
    for p in range(A_GROUPS // 2):
        vp = vnb[:, LANES * p:LANES * (p + 1)]
        parts.append(jnp.where(low, _mm(wm_ref[2 * p], vp), _mm(wm_ref[2 * p + 1], vp)))
    mixed = jnp.concatenate(parts, axis=1) + bsb
    sg = _sigmoid(pz)
    return pu, pv, pz, u, vhat, rstd, vnb, mixed, sg


def _gmlp_fwd(pa, ln_g3, ln_b3, wm, bsb, layer):
    tokens = pa.shape[0]
    tc = _token_tile(tokens)

    def body(pa_ref, lg_ref, lb_ref, wm_ref, bsb_ref, ya_ref):
        def chunk(ci, carry):
            rows = pl.ds(pl.multiple_of(ci * CHUNK, CHUNK), CHUNK)
            _, _, pz, u, _, _, _, mixed, sg = _gmlp_forward_chunk(
                pa_ref[rows, :], lg_ref[...], lb_ref[...], wm_ref, bsb_ref[...])
            ya_ref[rows, :] = (u * mixed * (pz * sg)).astype(BF16)
            return carry
        lax.fori_loop(0, tc // CHUNK, chunk, 0)

    blocks = [_nbytes((tc, PA_WIDTH), BF16), _nbytes((A_GROUPS, CHUNK, CHUNK), BF16),
              _nbytes((CHUNK, A_WIDTH), F32), _nbytes((tc, A_WIDTH), BF16)]
    return pl.pallas_call(
        body, name=f"gmlp_fwd_{layer}", grid=(tokens // tc,),
        in_specs=[pl.BlockSpec((tc, PA_WIDTH), lambda i: (i, 0)),
                  pl.BlockSpec((None, 1, A_WIDTH), lambda i: (layer, 0, 0)),
                  pl.BlockSpec((None, 1, A_WIDTH), lambda i: (layer, 0, 0)),
                  pl.BlockSpec((None, A_GROUPS, CHUNK, CHUNK), lambda i: (layer, 0, 0, 0)),
                  pl.BlockSpec((None, CHUNK, A_WIDTH), lambda i: (layer, 0, 0))],
        out_specs=pl.BlockSpec((tc, A_WIDTH), lambda i: (i, 0)),
        out_shape=pltpu.HBM((tokens,A_WIDTH), BF16),
        compiler_params=pltpu.CompilerParams(dimension_semantics=("parallel",),
                                             vmem_limit_bytes=_vmem_limit(blocks)),
    )(_hbm(pa), ln_g3, ln_b3, wm, bsb)


def _kv_variants(kv):
    t = kv.astype(F32)
    rolled = pltpu.roll(t, HEAD_DIM, 1)
    low = lax.broadcasted_iota(jnp.int32, t.shape, 1) < HEAD_DIM
    zero = jnp.zeros_like(t)
    head0 = (jnp.where(low, t, zero).astype(BF16), jnp.where(low, zero, rolled).astype(BF16))
    head1 = (jnp.where(low, rolled, zero).astype(BF16), jnp.where(low, zero, t).astype(BF16))
    return head0, head1


def _band_masks():
    row = lax.broadcasted_iota(jnp.int32, (BLOCK, BLOCK), 0)
    col = lax.broadcasted_iota(jnp.int32, (BLOCK, BLOCK), 1)
    return col <= row


def _wrap(full, tri):
    return jnp.where(tri, full[:, BLOCK:2 * BLOCK], full[:, 0:BLOCK])


def _attn_probs(sf, bias_h, sink, tri, kill):
    s = _wrap(sf, tri) * SCALE + bias_h
    s = jnp.where(kill, NEG, s)
    m = jnp.maximum(jnp.max(s, axis=-1, keepdims=True), sink)
    e = jnp.exp(s - m)
    es = jnp.exp(sink - m)
    inv = 1.0 / (jnp.sum(e, axis=-1, keepdims=True) + es)
    return e * inv, es * inv


def _unwrap16(p, tri):
    zero = jnp.zeros_like(p)
    return jnp.concatenate([jnp.where(tri, zero, p), jnp.where(tri, p, zero)], axis=1).astype(BF16)


def _fill_kv(kv_ref, prev_ref, cur_ref):
    kv_ref[0:BLOCK, :] = prev_ref[...]
    kv_ref[BLOCK:, :] = cur_ref[:, K_OFF:K_OFF + 2 * KV_WIDTH]


def _attn_fwd(pb, bias, sinks, b_loc, layer):
    tokens = pb.shape[0]
    nb = tokens // b_loc // BLOCK
    tq = _token_tile(tokens)
    per_tile = tq // BLOCK

    def body(sink_ref, cur_ref, prev_ref, bias_ref, yb_ref, kv_ref):
        t = pl.program_id(0)
        _fill_kv(kv_ref, prev_ref, cur_ref)
        tri = _band_masks()

        def block(i, carry):
            start = pl.multiple_of(i * BLOCK, BLOCK)
            rows = pl.ds(start, BLOCK)
            first = lax.rem(t * per_tile + i, nb) == 0
            kill = jnp.logical_and(first, jnp.logical_not(tri))
            kv = kv_ref[pl.ds(start, 2 * BLOCK), :]
            k_ops = _kv_variants(kv[:, 0:KV_WIDTH])
            v_ops = _kv_variants(kv[:, KV_WIDTH:2 * KV_WIDTH])
            p16 = {}
            for kvh in range(2):
                pairs = (2 * kvh, 2 * kvh + 1)
                qs = jnp.concatenate([cur_ref[rows, LANES * p:LANES * (p + 1)] for p in pairs], axis=0)
                for j in range(2):
                    sf = _mm_nt(qs, k_ops[kvh][j])
                    for r, p in enumerate(pairs):
                        hd = 2 * p + j
                        probs, _ = _attn_probs(sf[BLOCK * r:BLOCK * (r + 1)], bias_ref[hd], sink_ref[layer, hd],
                                               tri, kill)
                        p16[hd] = _unwrap16(probs, tri)
            for kvh in range(2):
                pairs = (2 * kvh, 2 * kvh + 1)
                out = jnp.zeros((2 * BLOCK, LANES), F32)
                for j in range(2):
                    out = out + _mm(jnp.concatenate([p16[2 * p + j] for p in pairs], axis=0), v_ops[kvh][j])
                for r, p in enumerate(pairs):
                    bz = cur_ref[rows, BZ_OFF + LANES * p:BZ_OFF + LANES * (p + 1)].astype(F32)
                    yb_ref[rows, LANES * p:LANES * (p + 1)] = (
                        out[BLOCK * r:BLOCK * (r + 1)] * (bz * _sigmoid(bz))).astype(BF16)
            return carry

        lax.fori_loop(0, per_tile, block, 0)

    blocks = [_nbytes((tq, PB_WIDTH), BF16), _nbytes((BLOCK, 2 * KV_WIDTH), BF16),
              _nbytes((B_HEADS, BLOCK, BLOCK), F32), _nbytes((tq, B_WIDTH), BF16)]
    scratch = _nbytes((tq + BLOCK, 2 * KV_WIDTH), BF16)
    return pl.pallas_call(
        body, name=f"attn_fwd_{layer}", grid=(tokens // tq,),
        in_specs=[pl.BlockSpec(memory_space=pltpu.SMEM),
                  pl.BlockSpec((tq, PB_WIDTH), lambda t: (t, 0)),
                  pl.BlockSpec((BLOCK, 2 * KV_WIDTH),
                               lambda t: (jnp.maximum(t * per_tile - 1, 0), K_OFF // (2 * KV_WIDTH))),
                  pl.BlockSpec((B_HEADS, BLOCK, BLOCK), lambda t: (0, 0, 0))],
        out_specs=pl.BlockSpec((tq, B_WIDTH), lambda t: (t, 0)),
        out_shape=pltpu.HBM((tokens, B_WIDTH), BF16),
        scratch_shapes=[pltpu.VMEM((tq + BLOCK, 2 * KV_WIDTH), BF16)],
        compiler_params=pltpu.CompilerParams(dimension_semantics=("parallel",),
                                             vmem_limit_bytes=_vmem_limit(blocks, scratch)),
    )(sinks, _hbm(pb), _hbm(pb), _hbm(bias))


def _outproj_fwd(ya, yb, wo, x2, post_g3, layer):
    tokens = x2.shape[0]
    tm = _token_tile(tokens)

    def body(ya_ref, yb_ref, woa_ref, wob_ref, x_ref, g_ref, y_ref, xn_ref):
        y = _mm(ya_ref[...], woa_ref[...]) + _mm(yb_ref[...], wob_ref[...])
        r = lax.rsqrt(jnp.mean(y * y, axis=-1, keepdims=True) + NORM_EPS)
        y_ref[...] = y.astype(BF16)
        xn_ref[...] = x_ref[...] + y * r * g_ref[...]

    half = D_MODEL // 2
    blocks = [2 * _nbytes((tm, half), BF16), 2 * _nbytes((half, D_MODEL), BF16), 2 * _nbytes((tm, D_MODEL), F32),
              _nbytes((tm, D_MODEL), BF16)]
    return pl.pallas_call(
        body, name=f"outproj_fwd_{layer}", grid=(tokens // tm,),
        in_specs=[pl.BlockSpec((tm, half), lambda i: (i, 0)),
                  pl.BlockSpec((tm, half), lambda i: (i, 0)),
                  pl.BlockSpec((half, D_MODEL), lambda i: (0, 0)),
                  pl.BlockSpec((half, D_MODEL), lambda i: (1, 0)),
                  pl.BlockSpec((tm, D_MODEL), lambda i: (i, 0)),
                  pl.BlockSpec((None, 1, D_MODEL), lambda i: (layer, 0, 0))],
        out_specs=[pl.BlockSpec((tm, D_MODEL), lambda i: (i, 0)),
                   pl.BlockSpec((tm, D_MODEL), lambda i: (i, 0))],
        out_shape=[pltpu.HBM((tokens,D_MODEL), BF16),
                   pltpu.HBM((tokens,D_MODEL), F32)],
        compiler_params=pltpu.CompilerParams(dimension_semantics=("parallel",),
                                             vmem_limit_bytes=_vmem_limit(blocks)),
    )(_hbm(ya), _hbm(yb), _hbm(wo), _hbm(wo), _hbm(x2), post_g3)


def _loss_head(xl, tgt):
    tokens = xl.shape[0]
    tm = _token_tile(tokens)
    steps = tokens // tm

    def body(x_ref, t_ref, g_ref, loss_ref, acc_ref):
        i = pl.program_id(0)

        @pl.when(i == 0)
        def _():
            acc_ref[...] = jnp.zeros_like(acc_ref)

        err = x_ref[...] - t_ref[...]
        g_ref[...] = err * (1.0 / D_MODEL)
        acc_ref[...] += jnp.sum(err * err, axis=0, keepdims=True)

        @pl.when(i == steps - 1)
        def _():
            total = jnp.sum(acc_ref[...], axis=-1, keepdims=True) * (0.5 / D_MODEL)
            loss_ref[...] = jnp.broadcast_to(total, loss_ref.shape)

    blocks = [3 * _nbytes((tm, D_MODEL), F32)]
    return pl.pallas_call(
        body, name="loss_head", grid=(steps,),
        in_specs=[pl.BlockSpec((tm, D_MODEL), lambda i: (i, 0)),
                  pl.BlockSpec((tm, D_MODEL), lambda i: (i, 0))],
        out_specs=[pl.BlockSpec((tm, D_MODEL), lambda i: (i, 0)),
                   pl.BlockSpec((1, LANES), lambda i: (0, 0))],
        out_shape=[pltpu.HBM((tokens,D_MODEL), F32),
                   pltpu.HBM((1,LANES), F32)],
        scratch_shapes=[pltpu.VMEM((1, D_MODEL), F32)],
        compiler_params=pltpu.CompilerParams(dimension_semantics=("arbitrary",),
                                             vmem_limit_bytes=_vmem_limit(blocks)),
    )(_hbm(xl), _hbm(tgt))


def _outproj_bwd(g, y, ya, yb, wo, post_g3, layer):
    tokens = g.shape[0]
    tm = _token_tile(tokens)
    half = D_MODEL // 2

    def body(g_ref, y_ref, ya_ref, yb_ref, woa_ref, wob_ref, pg_ref, dya_ref, dyb_ref, dwo_ref, dpg_ref):
        @pl.when(pl.program_id(0) == 0)
        def _():
            dwo_ref[...] = jnp.zeros_like(dwo_ref)
            dpg_ref[...] = jnp.zeros_like(dpg_ref)

        gv = g_ref[...]
        yf = y_ref[...].astype(F32)
        r = lax.rsqrt(jnp.mean(yf * yf, axis=-1, keepdims=True) + NORM_EPS)
        yhat = yf * r
        gg = gv * pg_ref[...]
        dy = r * (gg - yhat * jnp.mean(gg * yhat, axis=-1, keepdims=True))
        dpg_ref[...] += jnp.sum(gv * yhat, axis=0, keepdims=True)
        dyb16 = dy.astype(BF16)
        dya_ref[...] = _mm_nt(dyb16, woa_ref[...]).astype(BF16)
        dyb_ref[...] = _mm_nt(dyb16, wob_ref[...]).astype(BF16)
        dwo_ref[0:half, :] += _mm_tn(ya_ref[...], dyb16)
        dwo_ref[half:D_MODEL, :] += _mm_tn(yb_ref[...], dyb16)

    blocks = [_nbytes((tm, D_MODEL), F32), _nbytes((tm, D_MODEL), BF16), 4 * _nbytes((tm, half), BF16),
              2 * _nbytes((half, D_MODEL), BF16), _nbytes((D_MODEL, D_MODEL), F32)]
    return pl.pallas_call(
        body, name=f"outproj_bwd_{layer}", grid=(tokens // tm,),
        in_specs=[pl.BlockSpec((tm, D_MODEL), lambda i: (i, 0)),
                  pl.BlockSpec((tm, D_MODEL), lambda i: (i, 0)),
                  pl.BlockSpec((tm, half), lambda i: (i, 0)),
                  pl.BlockSpec((tm, half), lambda i: (i, 0)),
                  pl.BlockSpec((half, D_MODEL), lambda i: (0, 0)),
                  pl.BlockSpec((half, D_MODEL), lambda i: (1, 0)),
                  pl.BlockSpec((None, 1, D_MODEL), lambda i: (layer, 0, 0))],
        out_specs=[pl.BlockSpec((tm, half), lambda i: (i, 0)),
                   pl.BlockSpec((tm, half), lambda i: (i, 0)),
                   pl.BlockSpec((D_MODEL, D_MODEL), lambda i: (0, 0)),
                   pl.BlockSpec((1, D_MODEL), lambda i: (0, 0))],
        out_shape=[pltpu.HBM((tokens,half), BF16),
                   pltpu.HBM((tokens,half), BF16),
                   pltpu.HBM((D_MODEL,D_MODEL), F32),
                   pltpu.HBM((1,D_MODEL), F32)],
        compiler_params=pltpu.CompilerParams(dimension_semantics=("arbitrary",),
                                             vmem_limit_bytes=_vmem_limit(blocks)),
    )(_hbm(g), _hbm(y), _hbm(ya), _hbm(yb), _hbm(wo), _hbm(wo), post_g3)


def _gmlp_bwd(pa, dya, ln_g3, ln_b3, wm, wmt, bsb, layer):
    tokens = pa.shape[0]
    tc = _token_tile(tokens)
    steps = tokens // tc

    def body(pa_ref, dya_ref, lg_ref, lb_ref, wm_ref, wmt_ref, bsb_ref,
             da_ref, dws_ref, dbs_ref, dlg_ref, dlb_ref, dbsb_ref):
        i = pl.program_id(0)

        @pl.when(i == 0)
        def _():
            dws_ref[...] = jnp.zeros_like(dws_ref)
            dlg_ref[...] = jnp.zeros_like(dlg_ref)
            dlb_ref[...] = jnp.zeros_like(dlb_ref)
            dbsb_ref[...] = jnp.zeros_like(dbsb_ref)

        ln_g = lg_ref[...]
        low = lax.broadcasted_iota(jnp.int32, (CHUNK, LANES), 1) < HEAD_DIM

        def chunk(ci, carry):
            rows = pl.ds(pl.multiple_of(ci * CHUNK, CHUNK), CHUNK)
            pu, pv, pz, u, vhat, rstd, vnb, mixed, sg = _gmlp_forward_chunk(
                pa_ref[rows, :], ln_g, lb_ref[...], wm_ref, bsb_ref[...])
            dy = dya_ref[rows, :].astype(F32)
            sz = pz * sg
            du = dy * mixed * sz
            dmixed = dy * u * sz
            dz = dy * u * mixed * (sg * (1.0 + pz * (1.0 - sg)))
            dbsb_ref[...] += dmixed
            dmb = dmixed.astype(BF16)
            zero = jnp.zeros((CHUNK, LANES), BF16)
            parts = []
            for p in range(A_GROUPS // 2):
                dmp = dmb[:, LANES * p:LANES * (p + 1)]
                vp = vnb[:, LANES * p:LANES * (p + 1)]
                parts.append(jnp.where(low, _mm(wmt_ref[2 * p], dmp), _mm(wmt_ref[2 * p + 1], dmp)))
                dws_ref[2 * p] += _mm_nt(jnp.where(low, dmp, zero), vp)
                dws_ref[2 * p + 1] += _mm_nt(jnp.where(low, zero, dmp), vp)
            dvn = jnp.concatenate(parts, axis=1)
            dlg_ref[...] += jnp.sum(dvn * vhat, axis=0, keepdims=True)
            dlb_ref[...] += jnp.sum(dvn, axis=0, keepdims=True)
            dvh = dvn * ln_g
            dvv = rstd * (dvh - jnp.mean(dvh, axis=-1, keepdims=True)
                          - vhat * jnp.mean(dvh * vhat, axis=-1, keepdims=True))
            da_ref[rows, 0:A_WIDTH] = (du * _gelu_grad(pu)).astype(BF16)
            da_ref[rows, A_WIDTH:2 * A_WIDTH] = (dvv * _gelu_grad(pv)).astype(BF16)
            da_ref[rows, 2 * A_WIDTH:3 * A_WIDTH] = dz.astype(BF16)
            return carry

        lax.fori_loop(0, tc // CHUNK, chunk, 0)

        @pl.when(i == steps - 1)
        def _():
            causal = (lax.broadcasted_iota(jnp.int32, (CHUNK, CHUNK), 0)
                      >= lax.broadcasted_iota(jnp.int32, (CHUNK, CHUNK), 1))
            for h in range(A_GROUPS):
                dws_ref[h] = jnp.where(causal, dws_ref[h], 0.0)
            acc = dbsb_ref[...]
            lane_full = lax.broadcasted_iota(jnp.int32, (CHUNK, A_WIDTH), 1)
            lane_out = lax.broadcasted_iota(jnp.int32, (CHUNK, LANES), 1)
            out = jnp.zeros((CHUNK, LANES), F32)
            for h in range(A_GROUPS):
                in_group = jnp.logical_and(lane_full >= HEAD_DIM * h, lane_full < HEAD_DIM * (h + 1))
                s = jnp.sum(jnp.where(in_group, acc, 0.0), axis=-1, keepdims=True)
                out = jnp.where(lane_out == h, s, out)
            dbs_ref[...] = out

    blocks = [_nbytes((tc, PA_WIDTH), BF16), _nbytes((tc, A_WIDTH), BF16), 2 * _nbytes((A_GROUPS, CHUNK, CHUNK), BF16),
              _nbytes((CHUNK, A_WIDTH), F32), _nbytes((tc, PA_WIDTH), BF16), _nbytes((A_GROUPS, CHUNK, CHUNK), F32),
              _nbytes((CHUNK, LANES), F32)]
    return pl.pallas_call(
        body, name=f"gmlp_bwd_{layer}", grid=(steps,),
        in_specs=[pl.BlockSpec((tc, PA_WIDTH), lambda i: (i, 0)),
                  pl.BlockSpec((tc, A_WIDTH), lambda i: (i, 0)),
                  pl.BlockSpec((None, 1, A_WIDTH), lambda i: (layer, 0, 0)),
                  pl.BlockSpec((None, 1, A_WIDTH), lambda i: (layer, 0, 0)),
                  pl.BlockSpec((None, A_GROUPS, CHUNK, CHUNK), lambda i: (layer, 0, 0, 0)),
                  pl.BlockSpec((None, A_GROUPS, CHUNK, CHUNK), lambda i: (layer, 0, 0, 0)),
                  pl.BlockSpec((None, CHUNK, A_WIDTH), lambda i: (layer, 0, 0))],
        out_specs=[pl.BlockSpec((tc, PA_WIDTH), lambda i: (i, 0)),
                   pl.BlockSpec((A_GROUPS, CHUNK, CHUNK), lambda i: (0, 0, 0)),
                   pl.BlockSpec((CHUNK, LANES), lambda i: (0, 0)),
                   pl.BlockSpec((1, A_WIDTH), lambda i: (0, 0)),
                   pl.BlockSpec((1, A_WIDTH), lambda i: (0, 0))],
        out_shape=[pltpu.HBM((tokens,PA_WIDTH), BF16),
                   pltpu.HBM((A_GROUPS, CHUNK, CHUNK), F32),
                   pltpu.HBM((CHUNK, LANES), F32),
                   pltpu.HBM((1,A_WIDTH), F32),
                   pltpu.HBM((1,A_WIDTH), F32)],
        scratch_shapes=[pltpu.VMEM((CHUNK, A_WIDTH), F32)],
        compiler_params=pltpu.CompilerParams(dimension_semantics=("arbitrary",),
                                             vmem_limit_bytes=_vmem_limit(blocks, _nbytes((CHUNK, A_WIDTH), F32))),
    )(_hbm(pa), _hbm(dya), ln_g3, ln_b3, wm, wmt, bsb)


def _attn_bwd(pb, dyb, bias, sinks, b_loc, layer):
    tokens = pb.shape[0]
    nb = tokens // b_loc // BLOCK
    qz_width = 2 * B_WIDTH
    tq = _token_tile(tokens)
    per_tile = tq // BLOCK
    nt = tokens // tq

    def body(sink_ref, cur_ref, prev_ref, dyb_ref, bias_ref, dqz_ref, dkv_ref, dbias_ref, dsink_ref,
             kv_ref, acc_ref):
        t = pl.program_id(0)

        @pl.when(t == 0)
        def _():
            dbias_ref[...] = jnp.zeros_like(dbias_ref)
            dsink_ref[...] = jnp.zeros_like(dsink_ref)
            acc_ref[0:BLOCK, :] = jnp.zeros((BLOCK, 2 * KV_WIDTH), F32)

        @pl.when(t < nt)
        def _():
            acc_ref[BLOCK:, :] = jnp.zeros((tq, 2 * KV_WIDTH), F32)
            _fill_kv(kv_ref, prev_ref, cur_ref)
            tri = _band_masks()
            low = lax.broadcasted_iota(jnp.int32, (BLOCK, LANES), 1) < HEAD_DIM
            low_kv = lax.broadcasted_iota(jnp.int32, (2 * BLOCK, LANES), 1) < HEAD_DIM

            def block(i, carry):
                start = pl.multiple_of(i * BLOCK, BLOCK)
                rows = pl.ds(start, BLOCK)
                first = lax.rem(t * per_tile + i, nb) == 0
                kill = jnp.logical_and(first, jnp.logical_not(tri))
                kv = kv_ref[pl.ds(start, 2 * BLOCK), :]
                k_ops = _kv_variants(kv[:, 0:KV_WIDTH])
                v_ops = _kv_variants(kv[:, KV_WIDTH:2 * KV_WIDTH])
                probs, sink_p, p16, qs, dos, delta, ds16 = {}, {}, {}, {}, {}, {}, {}
                for kvh in range(2):
                    pairs = (2 * kvh, 2 * kvh + 1)
                    qs[kvh] = jnp.concatenate([cur_ref[rows, LANES * p:LANES * (p + 1)] for p in pairs], axis=0)
                    for j in range(2):
                        sf = _mm_nt(qs[kvh], k_ops[kvh][j])
                        for r, p in enumerate(pairs):
                            hd = 2 * p + j
                            probs[hd], sink_p[hd] = _attn_probs(sf[BLOCK * r:BLOCK * (r + 1)], bias_ref[hd],
                                                                sink_ref[layer, hd], tri, kill)
                            p16[hd] = _unwrap16(probs[hd], tri)
                for kvh in range(2):
                    pairs = (2 * kvh, 2 * kvh + 1)
                    out = jnp.zeros((2 * BLOCK, LANES), F32)
                    for j in range(2):
                        out = out + _mm(jnp.concatenate([p16[2 * p + j] for p in pairs], axis=0), v_ops[kvh][j])
                    d_outs = []
                    for r, p in enumerate(pairs):
                        bz = cur_ref[rows, BZ_OFF + LANES * p:BZ_OFF + LANES * (p + 1)].astype(F32)
                        sg = _sigmoid(bz)
                        dyp = dyb_ref[rows, LANES * p:LANES * (p + 1)].astype(F32)
                        out_p = out[BLOCK * r:BLOCK * (r + 1)]
                        d_out = dyp * (bz * sg)
                        dqz_ref[rows, B_WIDTH + LANES * p:B_WIDTH + LANES * (p + 1)] = (
                            dyp * out_p * (sg * (1.0 + bz * (1.0 - sg)))).astype(BF16)
                        dod = d_out * out_p
                        delta[2 * p] = jnp.sum(jnp.where(low, dod, 0.0), axis=-1, keepdims=True)
                        delta[2 * p + 1] = jnp.sum(jnp.where(low, 0.0, dod), axis=-1, keepdims=True)
                        d_outs.append(d_out.astype(BF16))
                    dos[kvh] = jnp.concatenate(d_outs, axis=0)
                for kvh in range(2):
                    pairs = (2 * kvh, 2 * kvh + 1)
                    for j in range(2):
                        dpf = _mm_nt(dos[kvh], v_ops[kvh][j])
                        for r, p in enumerate(pairs):
                            hd = 2 * p + j
                            ds = probs[hd] * (_wrap(dpf[BLOCK * r:BLOCK * (r + 1)], tri) - delta[hd])
                            dsink_ref[hd:hd + 1, :] += jnp.sum(-sink_p[hd] * delta[hd], axis=0, keepdims=True)
                            dbias_ref[hd] += ds
                            ds16[hd] = _unwrap16(ds, tri)
                dk_acc = [[None, None], [None, None]]
                dv_acc = [[None, None], [None, None]]
                for kvh in range(2):
                    pairs = (2 * kvh, 2 * kvh + 1)
                    dq = jnp.zeros((2 * BLOCK, LANES), F32)
                    for j in range(2):
                        dss = jnp.concatenate([ds16[2 * p + j] for p in pairs], axis=0)
                        pss = jnp.concatenate([p16[2 * p + j] for p in pairs], axis=0)
                        dq = dq + _mm(dss, k_ops[kvh][j])
                        dk_acc[kvh][j] = _mm_tn(dss, qs[kvh])
                        dv_acc[kvh][j] = _mm_tn(pss, dos[kvh])
                    for r, p in enumerate(pairs):
                        dqz_ref[rows, LANES * p:LANES * (p + 1)] = (dq[BLOCK * r:BLOCK * (r + 1)] * SCALE).astype(BF16)

                def fold(acc):
                    return jnp.where(low_kv,
                                     acc[0][0] + pltpu.roll(acc[0][1], HEAD_DIM, 1),
                                     pltpu.roll(acc[1][0], HEAD_DIM, 1) + acc[1][1])

                acc_ref[pl.ds(start, 2 * BLOCK), :] += jnp.concatenate(
                    [fold(dk_acc) * SCALE, fold(dv_acc)], axis=1)
                return carry

            lax.fori_loop(0, per_tile, block, 0)
            dkv_ref[...] = acc_ref[0:tq, :].astype(BF16)
            acc_ref[0:BLOCK, :] = acc_ref[tq:tq + BLOCK, :]

        @pl.when(t == nt)
        def _():
            dkv_ref[0:BLOCK, :] = acc_ref[0:BLOCK, :].astype(BF16)
            dkv_ref[BLOCK:, :] = jnp.zeros((tq - BLOCK, 2 * KV_WIDTH), BF16)

    def cur_map(t):
        return (jnp.minimum(t, nt - 1), 0)

    def prev_map(t):
        return (jnp.maximum(jnp.minimum(t, nt - 1) * per_tile - 1, 0), K_OFF // (2 * KV_WIDTH))

    blocks = [_nbytes((tq, PB_WIDTH), BF16), _nbytes((BLOCK, 2 * KV_WIDTH), BF16), _nbytes((tq, B_WIDTH), BF16),
              2 * _nbytes((B_HEADS, BLOCK, BLOCK), F32), _nbytes((tq, qz_width), BF16),
              _nbytes((tq, 2 * KV_WIDTH), BF16), _nbytes((B_HEADS, LANES), F32)]
    scratch = _nbytes((tq + BLOCK, 2 * KV_WIDTH), BF16) + _nbytes((tq + BLOCK, 2 * KV_WIDTH), F32)
    return pl.pallas_call(
        body, name=f"attn_bwd_{layer}", grid=(nt + 1,),
        in_specs=[pl.BlockSpec(memory_space=pltpu.SMEM),
                  pl.BlockSpec((tq, PB_WIDTH), cur_map),
                  pl.BlockSpec((BLOCK, 2 * KV_WIDTH), prev_map),
                  pl.BlockSpec((tq, B_WIDTH), cur_map),
                  pl.BlockSpec((B_HEADS, BLOCK, BLOCK), lambda t: (0, 0, 0))],
        out_specs=[pl.BlockSpec((tq, qz_width), cur_map),
                   pl.BlockSpec((tq, 2 * KV_WIDTH), lambda t: (t, 0)),
                   pl.BlockSpec((B_HEADS, BLOCK, BLOCK), lambda t: (0, 0, 0)),
                   pl.BlockSpec((B_HEADS, LANES), lambda t: (0, 0))],
        out_shape=[pltpu.HBM((tokens, qz_width), BF16),
                   pltpu.HBM((tokens + tq, 2 * KV_WIDTH), BF16),
                   pltpu.HBM((B_HEADS, BLOCK, BLOCK), F32),
                   pltpu.HBM((B_HEADS, LANES), F32)],
        scratch_shapes=[pltpu.VMEM((tq + BLOCK, 2 * KV_WIDTH), BF16),
                        pltpu.VMEM((tq + BLOCK, 2 * KV_WIDTH), F32)],
        compiler_params=pltpu.CompilerParams(dimension_semantics=("arbitrary",),
                                             vmem_limit_bytes=_vmem_limit(blocks, scratch)),
    )(sinks, _hbm(pb), _hbm(pb), _hbm(dyb), _hbm(bias))


def _inproj_bwd_dx(da, dqz, dkv, wa, wb, x2, g, pre_g3, layer):
    tokens = x2.shape[0]
    tm = _token_tile(tokens)

    def body(da_ref, dqz_ref, dkv_ref, wa_ref, wb_ref, x_ref, g_ref, pg_ref, gn_ref, dpg_ref):
        @pl.when(pl.program_id(0) == 0)
        def _():
            dpg_ref[...] = jnp.zeros_like(dpg_ref)

        dh = _mm_nt(da_ref[...], wa_ref[...])
        dh += _mm_nt(dqz_ref[:, 0:B_WIDTH], wb_ref[:, 0:B_WIDTH])
        dh += _mm_nt(dkv_ref[...], wb_ref[:, K_OFF:BZ_OFF])
        dh += _mm_nt(dqz_ref[:, B_WIDTH:2 * B_WIDTH], wb_ref[:, BZ_OFF:PB_WIDTH])
        x = x_ref[...]
        r = lax.rsqrt(jnp.mean(x * x, axis=-1, keepdims=True) + NORM_EPS)
        xhat = x * r
        dhg = dh * pg_ref[...]
        dpg_ref[...] += jnp.sum(dh * xhat, axis=0, keepdims=True)
        gn_ref[...] = g_ref[...] + r * (dhg - xhat * jnp.mean(dhg * xhat, axis=-1, keepdims=True))

    blocks = [_nbytes((tm, PA_WIDTH), BF16), _nbytes((tm, 2 * B_WIDTH), BF16), _nbytes((tm, 2 * KV_WIDTH), BF16),
              _nbytes((D_MODEL, IN_WIDTH), BF16), 3 * _nbytes((tm, D_MODEL), F32)]
    return pl.pallas_call(
        body, name=f"inproj_bwd_dx_{layer}", grid=(tokens // tm,),
        in_specs=[pl.BlockSpec((tm, PA_WIDTH), lambda i: (i, 0)),
                  pl.BlockSpec((tm, 2 * B_WIDTH), lambda i: (i, 0)),
                  pl.BlockSpec((tm, 2 * KV_WIDTH), lambda i: (i, 0)),
                  pl.BlockSpec((D_MODEL, PA_WIDTH), lambda i: (0, 0)),
                  pl.BlockSpec((D_MODEL, PB_WIDTH), lambda i: (0, 0)),
                  pl.BlockSpec((tm, D_MODEL), lambda i: (i, 0)),
                  pl.BlockSpec((tm, D_MODEL), lambda i: (i, 0)),
                  pl.BlockSpec((None, 1, D_MODEL), lambda i: (layer, 0, 0))],
        out_specs=[pl.BlockSpec((tm, D_MODEL), lambda i: (i, 0)),
                   pl.BlockSpec((1, D_MODEL), lambda i: (0, 0))],
        out_shape=[pltpu.HBM((tokens,D_MODEL), F32),
                   pltpu.HBM((1,D_MODEL), F32)],
        compiler_params=pltpu.CompilerParams(dimension_semantics=("arbitrary",),
                                             vmem_limit_bytes=_vmem_limit(blocks)),
    )(_hbm(da), _hbm(dqz), _hbm(dkv), _hbm(wa), _hbm(wb), _hbm(x2), _hbm(g), pre_g3)


def _inproj_bwd_dw(h, da, dqz, dkv, layer):
    tokens = h.shape[0]
    tm = _token_tile(tokens)

    def body(h_ref, da_ref, dqz_ref, dkv_ref, dwa_ref, dwb_ref):
        @pl.when(pl.program_id(0) == 0)
        def _():
            dwa_ref[...] = jnp.zeros_like(dwa_ref)
            dwb_ref[...] = jnp.zeros_like(dwb_ref)

        hv = h_ref[...]
        dwa_ref[...] += _mm_tn(hv, da_ref[...])
        dwb_ref[:, 0:B_WIDTH] += _mm_tn(hv, dqz_ref[:, 0:B_WIDTH])
        dwb_ref[:, K_OFF:BZ_OFF] += _mm_tn(hv, dkv_ref[...])
        dwb_ref[:, BZ_OFF:PB_WIDTH] += _mm_tn(hv, dqz_ref[:, B_WIDTH:2 * B_WIDTH])

    blocks = [_nbytes((tm, D_MODEL), BF16), _nbytes((tm, PA_WIDTH), BF16), _nbytes((tm, 2 * B_WIDTH), BF16),
              _nbytes((tm, 2 * KV_WIDTH), BF16), _nbytes((D_MODEL, IN_WIDTH), F32)]
    return pl.pallas_call(
        body, name=f"inproj_bwd_dw_{layer}", grid=(tokens // tm,),
        in_specs=[pl.BlockSpec((tm, D_MODEL), lambda i: (i, 0)),
                  pl.BlockSpec((tm, PA_WIDTH), lambda i: (i, 0)),
                  pl.BlockSpec((tm, 2 * B_WIDTH), lambda i: (i, 0)),
                  pl.BlockSpec((tm, 2 * KV_WIDTH), lambda i: (i, 0))],
        out_specs=[pl.BlockSpec((D_MODEL, PA_WIDTH), lambda i: (0, 0)),
                   pl.BlockSpec((D_MODEL, PB_WIDTH), lambda i: (0, 0))],
        out_shape=[pltpu.HBM((D_MODEL,PA_WIDTH), F32),
                   pltpu.HBM((D_MODEL,PB_WIDTH), F32)],
        compiler_params=pltpu.CompilerParams(dimension_semantics=("arbitrary",),
                                             vmem_limit_bytes=_vmem_limit(blocks)),
    )(_hbm(h), _hbm(da), _hbm(dqz), _hbm(dkv))


def _rel_bias_grad(dbias_layers, bucket):
    n = len(dbias_layers)

    def body(*refs):
        bucket_ref, out_ref = refs[n], refs[n + 1]
        bk = bucket_ref[...]
        row = lax.broadcasted_iota(jnp.int32, (REL_BUCKETS, LANES), 0)
        lane = lax.broadcasted_iota(jnp.int32, (REL_BUCKETS, LANES), 1)
        out = jnp.zeros((REL_BUCKETS, LANES), F32)
        for h in range(B_HEADS):
            tot = refs[0][h]
            for a in range(1, n):
                tot = tot + refs[a][h]

            def one(b, acc, tot=tot, h=h):
                s = jnp.sum(jnp.where(bk == b, tot, 0.0), axis=-1, keepdims=True)
                s = jnp.sum(s, axis=0, keepdims=True)
                return jnp.where(jnp.logical_and(row == b, lane == h), s, acc)

            out = lax.fori_loop(0, REL_BUCKETS, one, out)
        out_ref[...] = out

    vm = pl.BlockSpec(memory_space=pltpu.VMEM)
    return pl.pallas_call(
        body, name="rel_bias_grad",
        out_shape=jax.ShapeDtypeStruct((REL_BUCKETS, LANES), F32),
        in_specs=[vm] * (n + 1), out_specs=vm,
    )(*dbias_layers, bucket)


def _reduce_adamw(slots, w, m, v, name):
    rows, cols = w.shape
    tr = _row_tile(rows)
    c1 = 1.0 / (1.0 - ADAM_B1 ** ADAM_STEP)
    c2 = 1.0 / (1.0 - ADAM_B2 ** ADAM_STEP)

    def body(s_ref, w_ref, m_ref, v_ref, g_ref, d_ref, nm_ref, nv_ref):
        g = s_ref[0].astype(F32)
        for i in range(1, N_DEV):
            g = g + s_ref[i].astype(F32)
        nm = ADAM_B1 * m_ref[...] + (1.0 - ADAM_B1) * g
        nv = ADAM_B2 * v_ref[...] + (1.0 - ADAM_B2) * (g * g)
        g_ref[...] = g
        nm_ref[...] = nm
        nv_ref[...] = nv
        d_ref[...] = -ADAM_LR * ((nm * c1) / (jnp.sqrt(nv * c2) + ADAM_EPS) + ADAM_WD * w_ref[...])

    blocks = [_nbytes((N_DEV, tr, cols), slots.dtype), 7 * _nbytes((tr, cols), F32)]
    tile = pl.BlockSpec((tr, cols), lambda i: (i, 0))
    return pl.pallas_call(
        body, name=name, grid=(rows // tr,),
        in_specs=[pl.BlockSpec((N_DEV, tr, cols), lambda i: (0, i, 0)), tile, tile, tile],
        out_specs=[tile] * 4,
        out_shape=[pltpu.HBM((rows, cols), F32)] * 4,
        compiler_params=pltpu.CompilerParams(dimension_semantics=("parallel",),
                                             vmem_limit_bytes=_vmem_limit(blocks)),
    )(_hbm(slots), _hbm(w), _hbm(m), _hbm(v))


_SMALL = ("w_spatial", "pre_norm_g", "post_norm_g", "ln_v_g", "ln_v_b", "b_spatial", "sinks", "rel_bias")


def _pack_small(parts):
    slabs = []
    for name in _SMALL:
        flat = parts[name].astype(F32).reshape(-1)
        pad = (-flat.shape[0]) % (8 * LANES)
        slabs.append(jnp.pad(flat, (0, pad)).reshape(-1, LANES))
    return jnp.concatenate(slabs, axis=0)


def _unpack_small(slab, shapes):
    out, row = {}, 0
    for name in _SMALL:
        size = int(np.prod(shapes[name]))
        rows = -(-size // (8 * LANES)) * 8
        out[name] = slab[row:row + rows].reshape(-1)[:size].reshape(shapes[name])
        row += rows
    return out


def kernel(x, pre_norm_g, w_in, ln_v_g, ln_v_b, w_spatial, b_spatial, sinks, rel_bias, w_out, post_norm_g, loss_target, m_pre_norm_g, m_w_in, m_ln_v_g, m_ln_v_b, m_w_spatial, m_b_spatial, m_sinks, m_rel_bias, m_w_out, m_post_norm_g, v_pre_norm_g, v_w_in, v_ln_v_g, v_ln_v_b, v_w_spatial, v_b_spatial, v_sinks, v_rel_bias, v_w_out, v_post_norm_g):
    b_loc, seq, _ = x.shape
    tokens = b_loc * seq
    depth = w_in.shape[0]
    in_shard = w_in.shape[2]
    out_shard = w_out.shape[1]
    assert in_shard * N_DEV == IN_WIDTH and out_shard * N_DEV == D_MODEL and seq % BLOCK == 0

    me = _slot(lax.axis_index("x"), lax.axis_index("y"), lax.axis_index("c"))
    w_in16, w_out16 = w_in.astype(BF16), w_out.astype(BF16)

    def gather_start(layer, not_before):
        shards, _ = lax.optimization_barrier(((w_in16[layer], w_out16[layer]), not_before))
        zones = [lax.empty((N_DEV, D_MODEL, in_shard), BF16), lax.empty((N_DEV, out_shard, D_MODEL), BF16)]
        sems, sends, zones, token = _exchange_start(list(shards), zones, [0, 0], f"weights_send_{layer}")
        return (sems, sends, [0, 0]), zones, token

    def gather_wait(layer, started, zones, after):
        gin, gout = _exchange_wait([started], zones, after, f"weights_wait_{layer}")
        gin = lax.dynamic_update_index_in_dim(gin, w_in16[layer], me, 0)
        gout = lax.dynamic_update_index_in_dim(gout, w_out16[layer], me, 0)
        w_full = gin.transpose(1, 0, 2).reshape(D_MODEL, IN_WIDTH)
        return w_full[:, :PA_WIDTH], w_full[:, PA_WIDTH:], gout.reshape(D_MODEL, D_MODEL)

    causal = jnp.tril(jnp.ones((CHUNK, CHUNK), dtype=bool))
    wm = jnp.where(causal, w_spatial, 0.0).astype(BF16)
    wmt = _hbm(jnp.swapaxes(wm, -1, -2))
    wm = _hbm(wm)
    bsb = _hbm(jnp.repeat(jnp.swapaxes(b_spatial, -1, -2), HEAD_DIM, axis=-1))
    pre_g3 = _hbm(pre_norm_g.reshape(depth, 1, D_MODEL))
    post_g3 = _hbm(post_norm_g.reshape(depth, 1, D_MODEL))
    ln_g3 = _hbm(ln_v_g.reshape(depth, 1, A_WIDTH))
    ln_b3 = _hbm(ln_v_b.reshape(depth, 1, A_WIDTH))
    bucket = jnp.asarray(_bucket_table())
    bias = _bias_table(rel_bias, bucket)

    xs, saved, weights = [x.reshape(tokens, D_MODEL)], [], []
    pending = gather_start(0, xs[0])
    for layer in range(depth):
        wa, wb, wo = gather_wait(layer, pending[0], pending[1], xs[-1])
        pre_g_fwd = pre_g3
        if layer + 1 < depth:
            pending = gather_start(layer + 1, xs[-1])
            pre_g_fwd = _hbm(pre_norm_g.reshape(depth, 1, D_MODEL) + pending[2][0, 0])
        h, pa, pb = _inproj_fwd(xs[-1], pre_g_fwd, wa, wb, layer)
        ya = _gmlp_fwd(pa, ln_g3, ln_b3, wm, bsb, layer)
        yb = _attn_fwd(pb, bias, sinks, b_loc, layer)
        y, x_next = _outproj_fwd(ya, yb, wo, xs[-1], post_g3, layer)
        saved.append((h, pa, pb, ya, yb, y))
        weights.append((wa, wb, wo))
        xs.append(x_next)
    g, loss_part = _loss_head(xs[-1], loss_target.reshape(tokens, D_MODEL))
    loss = lax.psum(loss_part[0, 0], ("x", "y", "c"))

    grads = {name: [None] * depth for name in ("w_spatial", "pre_norm_g", "post_norm_g", "ln_v_g", "ln_v_b",
                                               "b_spatial", "sinks", "dbias")}
    zone_in = lax.empty((N_DEV, depth * D_MODEL, in_shard), BF16)
    zone_out = lax.empty((N_DEV, depth * out_shard, D_MODEL), BF16)
    started_in, started_out, own_in, own_out = [], [], [None] * depth, [None] * depth
    for layer in reversed(range(depth)):
        h, pa, pb, ya, yb, y = saved[layer]
        wa, wb, wo = weights[layer]
        dya, dyb, dwo, dpost = _outproj_bwd(g, y, ya, yb, wo, post_g3, layer)
        send_out = dwo.astype(BF16).reshape(N_DEV, out_shard, D_MODEL)
        own_out[layer] = lax.dynamic_index_in_dim(send_out, me, 0, keepdims=False)
        sems, sends, (zone_out,), token = _exchange_start([send_out], [zone_out], [layer * out_shard],
                                                          f"grads_send_out_{layer}")
        started_out.append((sems, sends, [layer * out_shard]))
        ln_g_bwd = _hbm(ln_v_g.reshape(depth, 1, A_WIDTH) + token[0, 0])
        da, dws, dbs, dlg, dlb = _gmlp_bwd(pa, dya, ln_g_bwd, ln_b3, wm, wmt, bsb, layer)
        dqz, dkv_shifted, grads["dbias"][layer], dsink = _attn_bwd(pb, dyb, bias, sinks, b_loc, layer)
        dkv = dkv_shifted[BLOCK:BLOCK + tokens]
        dwa, dwb = _inproj_bwd_dw(h, da, dqz, dkv, layer)
        send_in = jnp.concatenate([dwa, dwb], axis=-1).astype(BF16).reshape(D_MODEL, N_DEV, in_shard).transpose(1, 0, 2)
        own_in[layer] = lax.dynamic_index_in_dim(send_in, me, 0, keepdims=False)
        sems, sends, (zone_in,), token = _exchange_start([send_in], [zone_in], [layer * D_MODEL],
                                                         f"grads_send_in_{layer}")
        started_in.append((sems, sends, [layer * D_MODEL]))
        pre_g_bwd = _hbm(pre_norm_g.reshape(depth, 1, D_MODEL) + token[0, 0])
        g, dpre = _inproj_bwd_dx(da, dqz, dkv, wa, wb, xs[layer], g, pre_g_bwd, layer)
        grads["w_spatial"][layer] = dws
        grads["b_spatial"][layer] = dbs[:, :A_GROUPS].T
        grads["ln_v_g"][layer] = dlg[0]
        grads["ln_v_b"][layer] = dlb[0]
        grads["sinks"][layer] = dsink[:, 0]
        grads["pre_norm_g"][layer] = dpre[0]
        grads["post_norm_g"][layer] = dpost[0]
    grad_x = g.reshape(x.shape)
    drel = _rel_bias_grad(grads["dbias"], bucket)[:, :B_HEADS]

    (recv_out,) = _exchange_wait(started_out, [zone_out], g, "grads_wait_out")
    recv_out = lax.dynamic_update_index_in_dim(recv_out, jnp.concatenate(own_out, axis=0), me, 0)
    res_out = _reduce_adamw(recv_out, w_out.reshape(-1, D_MODEL), m_w_out.reshape(-1, D_MODEL),
                            v_w_out.reshape(-1, D_MODEL), "adamw_w_out")
    res_out = [r.reshape(w_out.shape) for r in res_out]

    small_w = dict(w_spatial=w_spatial, pre_norm_g=pre_norm_g, post_norm_g=post_norm_g, ln_v_g=ln_v_g,
                   ln_v_b=ln_v_b, b_spatial=b_spatial, sinks=sinks, rel_bias=rel_bias)
    small_m = dict(w_spatial=m_w_spatial, pre_norm_g=m_pre_norm_g, post_norm_g=m_post_norm_g, ln_v_g=m_ln_v_g,
                   ln_v_b=m_ln_v_b, b_spatial=m_b_spatial, sinks=m_sinks, rel_bias=m_rel_bias)
    small_v = dict(w_spatial=v_w_spatial, pre_norm_g=v_pre_norm_g, post_norm_g=v_post_norm_g, ln_v_g=v_ln_v_g,
                   ln_v_b=v_ln_v_b, b_spatial=v_b_spatial, sinks=v_sinks, rel_bias=v_rel_bias)
    small_g = {name: jnp.stack(grads[name]) for name in _SMALL if name != "rel_bias"}
    small_g["rel_bias"] = drel
    shapes = {name: small_w[name].shape for name in _SMALL}
    (slots,) = _all_gather([_pack_small(small_g)], "small_grads_all_gather")
    res_small = _reduce_adamw(slots, _pack_small(small_w), _pack_small(small_m), _pack_small(small_v), "adamw_small")

    (recv_in,) = _exchange_wait(started_in, [zone_in], res_small[0], "grads_wait_in")
    recv_in = lax.dynamic_update_index_in_dim(recv_in, jnp.concatenate(own_in, axis=0), me, 0)
    res_in = _reduce_adamw(recv_in, w_in.reshape(-1, in_shard), m_w_in.reshape(-1, in_shard),
                           v_w_in.reshape(-1, in_shard), "adamw_w_in")
    res_in = [r.reshape(w_in.shape) for r in res_in]
    res_small = [_unpack_small(r, shapes) for r in res_small]

    order = ("pre_norm_g", "w_in", "ln_v_g", "ln_v_b", "w_spatial", "b_spatial", "sinks", "rel_bias", "w_out",
             "post_norm_g")
    outs = [loss, grad_x]
    for kind in range(4):
        for name in order:
            if name == "w_in":
                outs.append(res_in[kind])
            elif name == "w_out":
                outs.append(res_out[kind])
            else:
                outs.append(res_small[kind][name])
    return tuple(outs)
```

```python
import math

import numpy as np
import jax
import jax.numpy as jnp
from jax import lax
from jax.experimental import pallas as pl
from jax.experimental.pallas import tpu as pltpu

F32 = jnp.float32
BF16 = jnp.bfloat16

D_MODEL = 1024
A_WIDTH = 512
A_GROUPS = 8
CHUNK = 128
B_HEADS = 8
HEAD_DIM = 64
B_WIDTH = 512
KV_WIDTH = 128
BLOCK = 128
REL_BUCKETS = 32
REL_MAX_DIST = 128
NORM_EPS = 1e-6
PA_WIDTH = 3 * A_WIDTH
PB_WIDTH = 2 * B_WIDTH + 2 * KV_WIDTH
IN_WIDTH = PA_WIDTH + PB_WIDTH
K_OFF, V_OFF, BZ_OFF = B_WIDTH, B_WIDTH + KV_WIDTH, B_WIDTH + 2 * KV_WIDTH
SCALE = HEAD_DIM ** -0.5
NEG = -1e30
N_DEV = 8
LANES = 128

ADAM_LR = 0.001
ADAM_B1 = 0.9
ADAM_B2 = 0.999
ADAM_EPS = 1e-08
ADAM_WD = 0.01
ADAM_STEP = 10

V7X_VMEM_BYTES = 64 * 1024 * 1024
VMEM_TEMP_BYTES = 12 * 1024 * 1024
MESH = pl.DeviceIdType.MESH


def _vmem_limit(block_bytes, scratch_bytes=0):
    need = 2 * sum(block_bytes) + scratch_bytes + VMEM_TEMP_BYTES
    return int(min(need, V7X_VMEM_BYTES - 8 * 1024 * 1024))


def _nbytes(shape, dtype):
    return int(np.prod(shape)) * jnp.dtype(dtype).itemsize


def _token_tile(tokens):
    tile = min(512, tokens // 2)
    assert tokens % tile == 0 and tile % CHUNK == 0, tokens
    return tile


def _row_tile(rows, cap=512):
    best = 8
    for t in range(8, cap + 1, 8):
        if rows % t == 0:
            best = t
    assert rows % best == 0, rows
    return best


def _mm(a, b):
    return lax.dot_general(a, b, (((1,), (0,)), ((), ())), preferred_element_type=F32)


def _mm_nt(a, b):
    return lax.dot_general(a, b, (((1,), (1,)), ((), ())), preferred_element_type=F32)


def _mm_tn(a, b):
    return lax.dot_general(a, b, (((0,), (0,)), ((), ())), preferred_element_type=F32)


_GELU_C = math.sqrt(2.0 / math.pi)


_GELU_A = _GELU_C * 0.044715


def _gelu_parts(x):
    x2 = x * x
    t = jnp.tanh(x * (_GELU_C + _GELU_A * x2))
    return x2, t, 0.5 + 0.5 * t


def _gelu(x):
    return x * _gelu_parts(x)[2]


def _gelu_and_grad(x):
    x2, t, half_plus = _gelu_parts(x)
    grad = half_plus + (0.5 * x) * (1.0 - t * t) * (_GELU_C + (3.0 * _GELU_A) * x2)
    return x * half_plus, grad


def _sigmoid(x):
    return 0.5 + 0.5 * jnp.tanh(0.5 * x)


def _bucket_table():
    q = np.arange(BLOCK)[:, None]
    k = np.arange(BLOCK)[None, :]
    dist = np.where(k <= q, q - k, q + BLOCK - k)
    max_exact = REL_BUCKETS // 2
    safe = np.maximum(dist, 1).astype(np.float32)
    large = max_exact + (np.log(safe / np.float32(max_exact)) / np.float32(math.log(REL_MAX_DIST / max_exact))
                         * np.float32(REL_BUCKETS - max_exact)).astype(np.int32)
    large = np.minimum(large, REL_BUCKETS - 1)
    assert dist.min() >= 0 and dist.max() < BLOCK
    return np.where(dist < max_exact, dist, large).astype(np.int32)


def _hbm(x):
    return pltpu.with_memory_space_constraint(x, pltpu.HBM)


def _slot(px, py, pc):
    return 4 * px + 2 * py + pc


def _all_gather(arrs, name):
    n = len(arrs)

    def body(*refs):
        ins, outs = refs[:n], refs[n:2 * n]
        send_sems, recv_sems, local_sems = refs[2 * n:]
        x, y, c = lax.axis_index("x"), lax.axis_index("y"), lax.axis_index("c")
        me, sibling = (x, y, c), (x, y, 1 - c)
        chips = [(1 - x, y), (x, 1 - y), (1 - x, 1 - y)]

        def copy(a, k, block, to, src=None):
            dst = outs[a].at[_slot(*block)]
            return pltpu.make_async_remote_copy(
                src_ref=dst if src is None else src, dst_ref=dst,
                send_sem=send_sems.at[7 * a + k], recv_sem=recv_sems.at[7 * a + k],
                device_id=to, device_id_type=MESH)

        mine = [pltpu.make_async_copy(ins[a], outs[a].at[_slot(*me)], local_sems.at[a]) for a in range(n)]
        for cp in mine:
            cp.start()
        first = []
        for a in range(n):
            first.append(copy(a, 0, me, sibling, src=ins[a]))
            first += [copy(a, 1 + j, me, (*chip, c), src=ins[a]) for j, chip in enumerate(chips)]
        for cp in first:
            cp.start()
        passed = []
        for j, chip in enumerate(chips):
            for a in range(n):
                copy(a, 1 + j, (*chip, c), me).wait_recv()
                fwd = copy(a, 4 + j, (*chip, c), sibling)
                fwd.start()
                passed.append(fwd)
        for a in range(n):
            copy(a, 0, sibling, me).wait_recv()
            for j, chip in enumerate(chips):
                copy(a, 4 + j, (*chip, 1 - c), me).wait_recv()
        for cp in first + passed:
            cp.wait_send()
        for cp in mine:
            cp.wait()

    any_spec = pl.BlockSpec(memory_space=pl.ANY)
    return pl.pallas_call(
        body, name=name,
        out_shape=[jax.ShapeDtypeStruct((N_DEV,) + a.shape, a.dtype) for a in arrs],
        in_specs=[any_spec] * n, out_specs=[any_spec] * n,
        scratch_shapes=[pltpu.SemaphoreType.DMA((7 * n,)), pltpu.SemaphoreType.DMA((7 * n,)),
                        pltpu.SemaphoreType.DMA((n,))],
    )(*arrs)


_HBM_SPEC = pl.BlockSpec(memory_space=pltpu.HBM)
_SEM_SPEC = pl.BlockSpec(memory_space=pltpu.SEMAPHORE)
_DATAFLOW = pltpu.SideEffectType.DATAFLOW_SIDE_EFFECTING


def _peers():
    x, y, c = lax.axis_index("x"), lax.axis_index("y"), lax.axis_index("c")
    peers = []
    for k in range(1, N_DEV):
        fx, fy, fc = (k >> 2) & 1, (k >> 1) & 1, k & 1
        peers.append((1 - x if fx else x, 1 - y if fy else y, 1 - c if fc else c))
    return (x, y, c), peers


def _exchange_copy(send_ref, land_ref, row_off, src_slot, dst_slot, sems, idx, peer):
    rows = send_ref.shape[-2]
    src = send_ref.at[src_slot] if len(send_ref.shape) == 3 else send_ref
    return pltpu.make_async_remote_copy(
        src_ref=src, dst_ref=land_ref.at[dst_slot, pl.ds(row_off, rows), :],
        send_sem=sems[0].at[idx], recv_sem=sems[1].at[idx], device_id=peer, device_id_type=MESH)


def _exchange_start(sends, lands, row_offs, name):
    n = len(sends)

    def body(*refs):
        ins, zones = refs[:n], refs[n:2 * n]
        sems = refs[2 * n:2 * n + 2]
        token = refs[-1]
        me, peers = _peers()
        for a in range(n):
            for k, peer in enumerate(peers):
                _exchange_copy(ins[a], zones[a], row_offs[a], _slot(*peer), _slot(*me), sems, 7 * a + k, peer).start()
        token[...] = jnp.zeros_like(token)

    arrays = [_hbm(a) for a in list(sends) + list(lands)]
    out = pl.pallas_call(
        body, name=name,
        out_shape=(pltpu.SemaphoreType.DMA((7 * n,)), pltpu.SemaphoreType.DMA((7 * n,)),
                   *[pltpu.HBM(a.shape, a.dtype) for a in arrays], jax.ShapeDtypeStruct((8, LANES), F32)),
        in_specs=[_HBM_SPEC] * (2 * n),
        out_specs=(_SEM_SPEC, _SEM_SPEC, *[_HBM_SPEC] * (2 * n), pl.BlockSpec(memory_space=pltpu.VMEM)),
        input_output_aliases={i: 2 + i for i in range(2 * n)},
        compiler_params=pltpu.CompilerParams(has_side_effects=_DATAFLOW),
    )(*arrays)
    return (out[0], out[1]), list(out[2:2 + n]), list(out[2 + n:2 + 2 * n]), out[-1]


def _exchange_wait(started, lands, after, name):
    n = len(lands)
    flat_sends = [s for _, sends, _ in started for s in sends]
    flat_sems = [s for sems, _, _ in started for s in sems]
    ns = len(flat_sends)

    def body(*refs):
        ins, zones = refs[:ns], refs[ns:ns + n]
        sem_refs = refs[ns + n:ns + n + len(flat_sems)]
        me, peers = _peers()
        pos = 0
        for call, (_, sends, row_offs) in enumerate(started):
            sems = sem_refs[2 * call:2 * call + 2]
            for a in range(len(sends)):
                for k, peer in enumerate(peers):
                    cp = _exchange_copy(ins[pos + a], zones[a], row_offs[a], _slot(*peer), _slot(*peer), sems,
                                        7 * a + k, peer)
                    cp.wait_send()
                    cp.wait_recv()
            pos += len(sends)

    arrays = list(flat_sends) + list(lands)
    out = pl.pallas_call(
        body, name=name,
        out_shape=tuple(pltpu.HBM(a.shape, a.dtype) for a in arrays),
        in_specs=[_HBM_SPEC] * len(arrays) + [_SEM_SPEC] * len(flat_sems) + [pl.BlockSpec(memory_space=pl.ANY)],
        out_specs=tuple([_HBM_SPEC] * len(arrays)),
        input_output_aliases={i: i for i in range(len(arrays))},
        compiler_params=pltpu.CompilerParams(has_side_effects=_DATAFLOW),
    )(*arrays, *flat_sems, after)
    return list(out[ns:])


def _bias_table(rel_bias, bucket):
    def body(rel_ref, bucket_ref, out_ref):
        bk = bucket_ref[...]
        for h in range(B_HEADS):
            def pick(b, acc, h=h):
                return jnp.where(bk == b, rel_ref[b, h], acc)
            out_ref[h] = lax.fori_loop(0, REL_BUCKETS, pick, jnp.zeros((BLOCK, BLOCK), F32))

    return pl.pallas_call(
        body, name="bias_table",
        out_shape=jax.ShapeDtypeStruct((B_HEADS, BLOCK, BLOCK), F32),
        in_specs=[pl.BlockSpec(memory_space=pltpu.SMEM), pl.BlockSpec(memory_space=pltpu.VMEM)],
        out_specs=pl.BlockSpec(memory_space=pltpu.VMEM),
    )(rel_bias, bucket)


def _inproj_fwd(x2, pre_g3, wa, wb, layer):
    tokens = x2.shape[0]
    tm = _token_tile(tokens)

    def body(x_ref, g_ref, wa_ref, wb_ref, h_ref, pa_ref, pb_ref):
        x = x_ref[...]
        r = lax.rsqrt(jnp.mean(x * x, axis=-1, keepdims=True) + NORM_EPS)
        h = (x * r * g_ref[...]).astype(BF16)
        h_ref[...] = h
        pa_ref[...] = _mm(h, wa_ref[...]).astype(BF16)
        pb_ref[...] = _mm(h, wb_ref[...]).astype(BF16)

    blocks = [_nbytes((tm, D_MODEL), F32), _nbytes((D_MODEL, PA_WIDTH), BF16), _nbytes((D_MODEL, PB_WIDTH), BF16),
              _nbytes((tm, D_MODEL), BF16), _nbytes((tm, PA_WIDTH), BF16), _nbytes((tm, PB_WIDTH), BF16)]
    return pl.pallas_call(
        body, name=f"inproj_fwd_{layer}", grid=(tokens // tm,),
        in_specs=[pl.BlockSpec((tm, D_MODEL), lambda i: (i, 0)),
                  pl.BlockSpec((None, 1, D_MODEL), lambda i: (layer, 0, 0)),
                  pl.BlockSpec((D_MODEL, PA_WIDTH), lambda i: (0, 0)),
                  pl.BlockSpec((D_MODEL, PB_WIDTH), lambda i: (0, 0))],
        out_specs=[pl.BlockSpec((tm, D_MODEL), lambda i: (i, 0)),
                   pl.BlockSpec((tm, PA_WIDTH), lambda i: (i, 0)),
                   pl.BlockSpec((tm, PB_WIDTH), lambda i: (i, 0))],
        out_shape=[pltpu.HBM((tokens,D_MODEL), BF16),
                   pltpu.HBM((tokens,PA_WIDTH), BF16),
                   pltpu.HBM((tokens,PB_WIDTH), BF16)],
        compiler_params=pltpu.CompilerParams(dimension_semantics=("parallel",),
                                             vmem_limit_bytes=_vmem_limit(blocks)),
    )(_hbm(x2), pre_g3, _hbm(wa), _hbm(wb))


def _gmlp_forward_chunk(pa, ln_g, ln_b, wm_ref, bsb):
    pu = pa[:, 0:A_WIDTH].astype(F32)
    pv = pa[:, A_WIDTH:2 * A_WIDTH].astype(F32)
    pz = pa[:, 2 * A_WIDTH:3 * A_WIDTH].astype(F32)
    u, gu = _gelu_and_grad(pu)
    vv, gv = _gelu_and_grad(pv)
    mu = jnp.mean(vv, axis=-1, keepdims=True)
    xc = vv - mu
    rstd = lax.rsqrt(jnp.mean(xc * xc, axis=-1, keepdims=True) + NORM_EPS)
    vhat = xc * rstd
    vnb = (vhat * ln_g + ln_b).astype(BF16)
    low = lax.broadcasted_iota(jnp.int32, (CHUNK, LANES), 1) < HEAD_DIM
    parts = []
    for p in range(A_GROUPS // 2):
        vp = vnb[:, LANES * p:LANES * (p + 1)]
        parts.append(jnp.where(low, _mm(wm_ref[2 * p], vp), _mm(wm_ref[2 * p + 1], vp)))
    mixed = jnp.concatenate(parts, axis=1) + bsb
    sg = _sigmoid(pz)
    return gu, gv, pz, u, vhat, rstd, vnb, mixed, sg


def _gmlp_fwd(pa, ln_g3, ln_b3, wm, bsb, layer):
    tokens = pa.shape[0]
    tc = _token_tile(tokens)

    def body(pa_ref, lg_ref, lb_ref, wm_ref, bsb_ref, ya_ref):
        def chunk(ci, carry):
            rows = pl.ds(pl.multiple_of(ci * CHUNK, CHUNK), CHUNK)
            _, _, pz, u, _, _, _, mixed, sg = _gmlp_forward_chunk(
                pa_ref[rows, :], lg_ref[...], lb_ref[...], wm_ref, bsb_ref[...])
            ya_ref[rows, :] = (u * mixed * (pz * sg)).astype(BF16)
            return carry
        lax.fori_loop(0, tc // CHUNK, chunk, 0)

    blocks = [_nbytes((tc, PA_WIDTH), BF16), _nbytes((A_GROUPS, CHUNK, CHUNK), BF16),
              _nbytes((CHUNK, A_WIDTH), F32), _nbytes((tc, A_WIDTH), BF16)]
    return pl.pallas_call(
        body, name=f"gmlp_fwd_{layer}", grid=(tokens // tc,),
        in_specs=[pl.BlockSpec((tc, PA_WIDTH), lambda i: (i, 0)),
                  pl.BlockSpec((None, 1, A_WIDTH), lambda i: (layer, 0, 0)),
                  pl.BlockSpec((None, 1, A_WIDTH), lambda i: (layer, 0, 0)),
                  pl.BlockSpec((None, A_GROUPS, CHUNK, CHUNK), lambda i: (layer, 0, 0, 0)),
                  pl.BlockSpec((None, CHUNK, A_WIDTH), lambda i: (layer, 0, 0))],
        out_specs=pl.BlockSpec((tc, A_WIDTH), lambda i: (i, 0)),
        out_shape=pltpu.HBM((tokens,A_WIDTH), BF16),
        compiler_params=pltpu.CompilerParams(dimension_semantics=("parallel",),
                                             vmem_limit_bytes=_vmem_limit(blocks)),
    )(_hbm(pa), ln_g3, ln_b3, wm, bsb)


def _kv_variants(kv):
    t = kv.astype(F32)
    rolled = pltpu.roll(t, HEAD_DIM, 1)
    low = lax.broadcasted_iota(jnp.int32, t.shape, 1) < HEAD_DIM
    zero = jnp.zeros_like(t)
    head0 = (jnp.where(low, t, zero).astype(BF16), jnp.where(low, zero, rolled).astype(BF16))
    head1 = (jnp.where(low, rolled, zero).astype(BF16), jnp.where(low, zero, t).astype(BF16))
    return head0, head1


def _band_masks():
    row = lax.broadcasted_iota(jnp.int32, (BLOCK, BLOCK), 0)
    col = lax.broadcasted_iota(jnp.int32, (BLOCK, BLOCK), 1)
    return col <= row


def _wrap(full, tri):
    return jnp.where(tri, full[:, BLOCK:2 * BLOCK], full[:, 0:BLOCK])


def _attn_probs(sf, bias_h, sink, tri, kill):
    s = _wrap(sf, tri) * SCALE + bias_h
    s = jnp.where(kill, NEG, s)
    m = jnp.maximum(jnp.max(s, axis=-1, keepdims=True), sink)
    e = jnp.exp(s - m)
    es = jnp.exp(sink - m)
    inv = 1.0 / (jnp.sum(e, axis=-1, keepdims=True) + es)
    return e * inv, es * inv


def _unwrap16(p, tri):
    zero = jnp.zeros_like(p)
    return jnp.concatenate([jnp.where(tri, zero, p), jnp.where(tri, p, zero)], axis=1).astype(BF16)


def _fill_kv(kv_ref, prev_ref, cur_ref):
    kv_ref[0:BLOCK, :] = prev_ref[...]
    kv_ref[BLOCK:, :] = cur_ref[:, K_OFF:K_OFF + 2 * KV_WIDTH]


def _attn_fwd(pb, bias, sinks, b_loc, layer):
    tokens = pb.shape[0]
    nb = tokens // b_loc // BLOCK
    tq = _token_tile(tokens)
    per_tile = tq // BLOCK

    def body(sink_ref, cur_ref, prev_ref, bias_ref, yb_ref, kv_ref):
        t = pl.program_id(0)
        _fill_kv(kv_ref, prev_ref, cur_ref)
        tri = _band_masks()

        def block(i, carry):
            start = pl.multiple_of(i * BLOCK, BLOCK)
            rows = pl.ds(start, BLOCK)
            first = lax.rem(t * per_tile + i, nb) == 0
            kill = jnp.logical_and(first, jnp.logical_not(tri))
            kv = kv_ref[pl.ds(start, 2 * BLOCK), :]
            k_ops = _kv_variants(kv[:, 0:KV_WIDTH])
            v_ops = _kv_variants(kv[:, KV_WIDTH:2 * KV_WIDTH])
            p16 = {}
            for kvh in range(2):
                pairs = (2 * kvh, 2 * kvh + 1)
                qs = jnp.concatenate([cur_ref[rows, LANES * p:LANES * (p + 1)] for p in pairs], axis=0)
                for j in range(2):
                    sf = _mm_nt(qs, k_ops[kvh][j])
                    for r, p in enumerate(pairs):
                        hd = 2 * p + j
                        probs, _ = _attn_probs(sf[BLOCK * r:BLOCK * (r + 1)], bias_ref[hd], sink_ref[layer, hd],
                                               tri, kill)
                        p16[hd] = _unwrap16(probs, tri)
            for kvh in range(2):
                pairs = (2 * kvh, 2 * kvh + 1)
                out = jnp.zeros((2 * BLOCK, LANES), F32)
                for j in range(2):
                    out = out + _mm(jnp.concatenate([p16[2 * p + j] for p in pairs], axis=0), v_ops[kvh][j])
                for r, p in enumerate(pairs):
                    bz = cur_ref[rows, BZ_OFF + LANES * p:BZ_OFF + LANES * (p + 1)].astype(F32)
                    yb_ref[rows, LANES * p:LANES * (p + 1)] = (
                        out[BLOCK * r:BLOCK * (r + 1)] * (bz * _sigmoid(bz))).astype(BF16)
            return carry

        lax.fori_loop(0, per_tile, block, 0)

    blocks = [_nbytes((tq, PB_WIDTH), BF16), _nbytes((BLOCK, 2 * KV_WIDTH), BF16),
              _nbytes((B_HEADS, BLOCK, BLOCK), F32), _nbytes((tq, B_WIDTH), BF16)]
    scratch = _nbytes((tq + BLOCK, 2 * KV_WIDTH), BF16)
    return pl.pallas_call(
        body, name=f"attn_fwd_{layer}", grid=(tokens // tq,),
        in_specs=[pl.BlockSpec(memory_space=pltpu.SMEM),
                  pl.BlockSpec((tq, PB_WIDTH), lambda t: (t, 0)),
                  pl.BlockSpec((BLOCK, 2 * KV_WIDTH),
                               lambda t: (jnp.maximum(t * per_tile - 1, 0), K_OFF // (2 * KV_WIDTH))),
                  pl.BlockSpec((B_HEADS, BLOCK, BLOCK), lambda t: (0, 0, 0))],
        out_specs=pl.BlockSpec((tq, B_WIDTH), lambda t: (t, 0)),
        out_shape=pltpu.HBM((tokens, B_WIDTH), BF16),
        scratch_shapes=[pltpu.VMEM((tq + BLOCK, 2 * KV_WIDTH), BF16)],
        compiler_params=pltpu.CompilerParams(dimension_semantics=("parallel",),
                                             vmem_limit_bytes=_vmem_limit(blocks, scratch)),
    )(sinks, _hbm(pb), _hbm(pb), _hbm(bias))


def _outproj_fwd(ya, yb, wo, x2, post_g3, layer):
    tokens = x2.shape[0]
    tm = _token_tile(tokens)

    def body(ya_ref, yb_ref, woa_ref, wob_ref, x_ref, g_ref, y_ref, xn_ref):
        y = _mm(ya_ref[...], woa_ref[...]) + _mm(yb_ref[...], wob_ref[...])
        r = lax.rsqrt(jnp.mean(y * y, axis=-1, keepdims=True) + NORM_EPS)
        y_ref[...] = y.astype(BF16)
        xn_ref[...] = x_ref[...] + y * r * g_ref[...]

    half = D_MODEL // 2
    blocks = [2 * _nbytes((tm, half), BF16), 2 * _nbytes((half, D_MODEL), BF16), 2 * _nbytes((tm, D_MODEL), F32),
              _nbytes((tm, D_MODEL), BF16)]
    return pl.pallas_call(
        body, name=f"outproj_fwd_{layer}", grid=(tokens // tm,),
        in_specs=[pl.BlockSpec((tm, half), lambda i: (i, 0)),
                  pl.BlockSpec((tm, half), lambda i: (i, 0)),
                  pl.BlockSpec((half, D_MODEL), lambda i: (0, 0)),
                  pl.BlockSpec((half, D_MODEL), lambda i: (1, 0)),
                  pl.BlockSpec((tm, D_MODEL), lambda i: (i, 0)),
                  pl.BlockSpec((None, 1, D_MODEL), lambda i: (layer, 0, 0))],
        out_specs=[pl.BlockSpec((tm, D_MODEL), lambda i: (i, 0)),
                   pl.BlockSpec((tm, D_MODEL), lambda i: (i, 0))],
        out_shape=[pltpu.HBM((tokens,D_MODEL), BF16),
                   pltpu.HBM((tokens,D_MODEL), F32)],
        compiler_params=pltpu.CompilerParams(dimension_semantics=("parallel",),
                                             vmem_limit_bytes=_vmem_limit(blocks)),
    )(_hbm(ya), _hbm(yb), _hbm(wo), _hbm(wo), _hbm(x2), post_g3)


def _loss_head(xl, tgt):
    tokens = xl.shape[0]
    tm = _token_tile(tokens)
    steps = tokens // tm

    def body(x_ref, t_ref, g_ref, loss_ref, acc_ref):
        i = pl.program_id(0)

        @pl.when(i == 0)
        def _():
            acc_ref[...] = jnp.zeros_like(acc_ref)

        err = x_ref[...] - t_ref[...]
        g_ref[...] = err * (1.0 / D_MODEL)
        acc_ref[...] += jnp.sum(err * err, axis=0, keepdims=True)

        @pl.when(i == steps - 1)
        def _():
            total = jnp.sum(acc_ref[...], axis=-1, keepdims=True) * (0.5 / D_MODEL)
            loss_ref[...] = jnp.broadcast_to(total, loss_ref.shape)

    blocks = [3 * _nbytes((tm, D_MODEL), F32)]
    return pl.pallas_call(
        body, name="loss_head", grid=(steps,),
        in_specs=[pl.BlockSpec((tm, D_MODEL), lambda i: (i, 0)),
                  pl.BlockSpec((tm, D_MODEL), lambda i: (i, 0))],
        out_specs=[pl.BlockSpec((tm, D_MODEL), lambda i: (i, 0)),
                   pl.BlockSpec((1, LANES), lambda i: (0, 0))],
        out_shape=[pltpu.HBM((tokens,D_MODEL), F32),
                   pltpu.HBM((1,LANES), F32)],
        scratch_shapes=[pltpu.VMEM((1, D_MODEL), F32)],
        compiler_params=pltpu.CompilerParams(dimension_semantics=("arbitrary",),
                                             vmem_limit_bytes=_vmem_limit(blocks)),
    )(_hbm(xl), _hbm(tgt))


def _outproj_bwd(g, y, ya, yb, wo, post_g3, layer):
    tokens = g.shape[0]
    tm = _token_tile(tokens)
    half = D_MODEL // 2

    def body(g_ref, y_ref, ya_ref, yb_ref, woa_ref, wob_ref, pg_ref, dya_ref, dyb_ref, dwo_ref, dpg_ref):
        @pl.when(pl.program_id(0) == 0)
        def _():
            dwo_ref[...] = jnp.zeros_like(dwo_ref)
            dpg_ref[...] = jnp.zeros_like(dpg_ref)

        gv = g_ref[...]
        yf = y_ref[...].astype(F32)
        r = lax.rsqrt(jnp.mean(yf * yf, axis=-1, keepdims=True) + NORM_EPS)
        yhat = yf * r
        gg = gv * pg_ref[...]
        dy = r * (gg - yhat * jnp.mean(gg * yhat, axis=-1, keepdims=True))
        dpg_ref[...] += jnp.sum(gv * yhat, axis=0, keepdims=True)
        dyb16 = dy.astype(BF16)
        dya_ref[...] = _mm_nt(dyb16, woa_ref[...]).astype(BF16)
        dyb_ref[...] = _mm_nt(dyb16, wob_ref[...]).astype(BF16)
        dwo_ref[0:half, :] += _mm_tn(ya_ref[...], dyb16)
        dwo_ref[half:D_MODEL, :] += _mm_tn(yb_ref[...], dyb16)

    blocks = [_nbytes((tm, D_MODEL), F32), _nbytes((tm, D_MODEL), BF16), 4 * _nbytes((tm, half), BF16),
              2 * _nbytes((half, D_MODEL), BF16), _nbytes((D_MODEL, D_MODEL), F32)]
    return pl.pallas_call(
        body, name=f"outproj_bwd_{layer}", grid=(tokens // tm,),
        in_specs=[pl.BlockSpec((tm, D_MODEL), lambda i: (i, 0)),
                  pl.BlockSpec((tm, D_MODEL), lambda i: (i, 0)),
                  pl.BlockSpec((tm, half), lambda i: (i, 0)),
                  pl.BlockSpec((tm, half), lambda i: (i, 0)),
                  pl.BlockSpec((half, D_MODEL), lambda i: (0, 0)),
                  pl.BlockSpec((half, D_MODEL), lambda i: (1, 0)),
                  pl.BlockSpec((None, 1, D_MODEL), lambda i: (layer, 0, 0))],
        out_specs=[pl.BlockSpec((tm, half), lambda i: (i, 0)),
                   pl.BlockSpec((tm, half), lambda i: (i, 0)),
                   pl.BlockSpec((D_MODEL, D_MODEL), lambda i: (0, 0)),
                   pl.BlockSpec((1, D_MODEL), lambda i: (0, 0))],
        out_shape=[pltpu.HBM((tokens,half), BF16),
                   pltpu.HBM((tokens,half), BF16),
                   pltpu.HBM((D_MODEL,D_MODEL), F32),
                   pltpu.HBM((1,D_MODEL), F32)],
        compiler_params=pltpu.CompilerParams(dimension_semantics=("arbitrary",),
                                             vmem_limit_bytes=_vmem_limit(blocks)),
    )(_hbm(g), _hbm(y), _hbm(ya), _hbm(yb), _hbm(wo), _hbm(wo), post_g3)


def _gmlp_bwd(pa, dya, ln_g3, ln_b3, wm, wmt, bsb, layer):
    tokens = pa.shape[0]
    tc = _token_tile(tokens)
    steps = tokens // tc

    def body(pa_ref, dya_ref, lg_ref, lb_ref, wm_ref, wmt_ref, bsb_ref,
             da_ref, dws_ref, dbs_ref, dlg_ref, dlb_ref, dbsb_ref):
        i = pl.program_id(0)

        @pl.when(i == 0)
        def _():
            dws_ref[...] = jnp.zeros_like(dws_ref)
            dlg_ref[...] = jnp.zeros_like(dlg_ref)
            dlb_ref[...] = jnp.zeros_like(dlb_ref)
            dbsb_ref[...] = jnp.zeros_like(dbsb_ref)

        ln_g = lg_ref[...]
        low = lax.broadcasted_iota(jnp.int32, (CHUNK, LANES), 1) < HEAD_DIM

        def chunk(ci, carry):
            rows = pl.ds(pl.multiple_of(ci * CHUNK, CHUNK), CHUNK)
            gu, gv, pz, u, vhat, rstd, vnb, mixed, sg = _gmlp_forward_chunk(
                pa_ref[rows, :], ln_g, lb_ref[...], wm_ref, bsb_ref[...])
            dy = dya_ref[rows, :].astype(F32)
            sz = pz * sg
            dy_sz = dy * sz
            du = dy_sz * mixed
            dmixed = dy_sz * u
            dz = dy * (u * mixed) * (sg + sz * (1.0 - sg))
            dbsb_ref[...] += dmixed
            dmb = dmixed.astype(BF16)
            zero = jnp.zeros((CHUNK, LANES), BF16)
            parts = []
            for p in range(A_GROUPS // 2):
                dmp = dmb[:, LANES * p:LANES * (p + 1)]
                vp = vnb[:, LANES * p:LANES * (p + 1)]
                parts.append(jnp.where(low, _mm(wmt_ref[2 * p], dmp), _mm(wmt_ref[2 * p + 1], dmp)))
                dws_ref[2 * p] += _mm_nt(jnp.where(low, dmp, zero), vp)
                dws_ref[2 * p + 1] += _mm_nt(jnp.where(low, zero, dmp), vp)
            dvn = jnp.concatenate(parts, axis=1)
            dlg_ref[...] += jnp.sum(dvn * vhat, axis=0, keepdims=True)
            dlb_ref[...] += jnp.sum(dvn, axis=0, keepdims=True)
            dvh = dvn * ln_g
            dvv = rstd * (dvh - jnp.mean(dvh, axis=-1, keepdims=True)
                          - vhat * jnp.mean(dvh * vhat, axis=-1, keepdims=True))
            da_ref[rows, 0:A_WIDTH] = (du * gu).astype(BF16)
            da_ref[rows, A_WIDTH:2 * A_WIDTH] = (dvv * gv).astype(BF16)
            da_ref[rows, 2 * A_WIDTH:3 * A_WIDTH] = dz.astype(BF16)
            return carry

        lax.fori_loop(0, tc // CHUNK, chunk, 0)

        @pl.when(i == steps - 1)
        def _():
            causal = (lax.broadcasted_iota(jnp.int32, (CHUNK, CHUNK), 0)
                      >= lax.broadcasted_iota(jnp.int32, (CHUNK, CHUNK), 1))
            for h in range(A_GROUPS):
                dws_ref[h] = jnp.where(causal, dws_ref[h], 0.0)
            acc = dbsb_ref[...]
            lane_full = lax.broadcasted_iota(jnp.int32, (CHUNK, A_WIDTH), 1)
            lane_out = lax.broadcasted_iota(jnp.int32, (CHUNK, LANES), 1)
            out = jnp.zeros((CHUNK, LANES), F32)
            for h in range(A_GROUPS):
                in_group = jnp.logical_and(lane_full >= HEAD_DIM * h, lane_full < HEAD_DIM * (h + 1))
                s = jnp.sum(jnp.where(in_group, acc, 0.0), axis=-1, keepdims=True)
                out = jnp.where(lane_out == h, s, out)
            dbs_ref[...] = out

    blocks = [_nbytes((tc, PA_WIDTH), BF16), _nbytes((tc, A_WIDTH), BF16), 2 * _nbytes((A_GROUPS, CHUNK, CHUNK), BF16),
              _nbytes((CHUNK, A_WIDTH), F32), _nbytes((tc, PA_WIDTH), BF16), _nbytes((A_GROUPS, CHUNK, CHUNK), F32),
              _nbytes((CHUNK, LANES), F32)]
    return pl.pallas_call(
        body, name=f"gmlp_bwd_{layer}", grid=(steps,),
        in_specs=[pl.BlockSpec((tc, PA_WIDTH), lambda i: (i, 0)),
                  pl.BlockSpec((tc, A_WIDTH), lambda i: (i, 0)),
                  pl.BlockSpec((None, 1, A_WIDTH), lambda i: (layer, 0, 0)),
                  pl.BlockSpec((None, 1, A_WIDTH), lambda i: (layer, 0, 0)),
                  pl.BlockSpec((None, A_GROUPS, CHUNK, CHUNK), lambda i: (layer, 0, 0, 0)),
                  pl.BlockSpec((None, A_GROUPS, CHUNK, CHUNK), lambda i: (layer, 0, 0, 0)),
                  pl.BlockSpec((None, CHUNK, A_WIDTH), lambda i: (layer, 0, 0))],
        out_specs=[pl.BlockSpec((tc, PA_WIDTH), lambda i: (i, 0)),
                   pl.BlockSpec((A_GROUPS, CHUNK, CHUNK), lambda i: (0, 0, 0)),
                   pl.BlockSpec((CHUNK, LANES), lambda i: (0, 0)),
                   pl.BlockSpec((1, A_WIDTH), lambda i: (0, 0)),
                   pl.BlockSpec((1, A_WIDTH), lambda i: (0, 0))],
        out_shape=[pltpu.HBM((tokens,PA_WIDTH), BF16),
                   pltpu.HBM((A_GROUPS, CHUNK, CHUNK), F32),
                   pltpu.HBM((CHUNK, LANES), F32),
                   pltpu.HBM((1,A_WIDTH), F32),
                   pltpu.HBM((1,A_WIDTH), F32)],
        scratch_shapes=[pltpu.VMEM((CHUNK, A_WIDTH), F32)],
        compiler_params=pltpu.CompilerParams(dimension_semantics=("arbitrary",),
                                             vmem_limit_bytes=_vmem_limit(blocks, _nbytes((CHUNK, A_WIDTH), F32))),
    )(_hbm(pa), _hbm(dya), ln_g3, ln_b3, wm, wmt, bsb)


def _attn_bwd(pb, dyb, bias, sinks, b_loc, layer):
    tokens = pb.shape[0]
    nb = tokens // b_loc // BLOCK
    qz_width = 2 * B_WIDTH
    tq = _token_tile(tokens)
    per_tile = tq // BLOCK
    nt = tokens // tq

    def body(sink_ref, cur_ref, prev_ref, dyb_ref, bias_ref, dqz_ref, dkv_ref, dbias_ref, dsink_ref,
             kv_ref, acc_ref):
        t = pl.program_id(0)

        @pl.when(t == 0)
        def _():
            dbias_ref[...] = jnp.zeros_like(dbias_ref)
            dsink_ref[...] = jnp.zeros_like(dsink_ref)
            acc_ref[0:BLOCK, :] = jnp.zeros((BLOCK, 2 * KV_WIDTH), F32)

        @pl.when(t < nt)
        def _():
            acc_ref[BLOCK:, :] = jnp.zeros((tq, 2 * KV_WIDTH), F32)
            _fill_kv(kv_ref, prev_ref, cur_ref)
            tri = _band_masks()
            low = lax.broadcasted_iota(jnp.int32, (BLOCK, LANES), 1) < HEAD_DIM
            low_kv = lax.broadcasted_iota(jnp.int32, (2 * BLOCK, LANES), 1) < HEAD_DIM

            def block(i, carry):
                start = pl.multiple_of(i * BLOCK, BLOCK)
                rows = pl.ds(start, BLOCK)
                first = lax.rem(t * per_tile + i, nb) == 0
                kill = jnp.logical_and(first, jnp.logical_not(tri))
                kv = kv_ref[pl.ds(start, 2 * BLOCK), :]
                k_ops = _kv_variants(kv[:, 0:KV_WIDTH])
                v_ops = _kv_variants(kv[:, KV_WIDTH:2 * KV_WIDTH])
                probs, sink_p, p16, qs, dos, delta, ds16 = {}, {}, {}, {}, {}, {}, {}
                for kvh in range(2):
                    pairs = (2 * kvh, 2 * kvh + 1)
                    qs[kvh] = jnp.concatenate([cur_ref[rows, LANES * p:LANES * (p + 1)] for p in pairs], axis=0)
                    for j in range(2):
                        sf = _mm_nt(qs[kvh], k_ops[kvh][j])
                        for r, p in enumerate(pairs):
                            hd = 2 * p + j
                            probs[hd], sink_p[hd] = _attn_probs(sf[BLOCK * r:BLOCK * (r + 1)], bias_ref[hd],
                                                                sink_ref[layer, hd], tri, kill)
                            p16[hd] = _unwrap16(probs[hd], tri)
                for kvh in range(2):
                    pairs = (2 * kvh, 2 * kvh + 1)
                    out = jnp.zeros((2 * BLOCK, LANES), F32)
                    for j in range(2):
                        out = out + _mm(jnp.concatenate([p16[2 * p + j] for p in pairs], axis=0), v_ops[kvh][j])
                    d_outs = []
                    for r, p in enumerate(pairs):
                        bz = cur_ref[rows, BZ_OFF + LANES * p:BZ_OFF + LANES * (p + 1)].astype(F32)
                        sg = _sigmoid(bz)
                        dyp = dyb_ref[rows, LANES * p:LANES * (p + 1)].astype(F32)
                        out_p = out[BLOCK * r:BLOCK * (r + 1)]
                        d_out = dyp * (bz * sg)
                        dqz_ref[rows, B_WIDTH + LANES * p:B_WIDTH + LANES * (p + 1)] = (
                            dyp * out_p * (sg * (1.0 + bz * (1.0 - sg)))).astype(BF16)
                        dod = d_out * out_p
                        delta[2 * p] = jnp.sum(jnp.where(low, dod, 0.0), axis=-1, keepdims=True)
                        delta[2 * p + 1] = jnp.sum(jnp.where(low, 0.0, dod), axis=-1, keepdims=True)
                        d_outs.append(d_out.astype(BF16))
                    dos[kvh] = jnp.concatenate(d_outs, axis=0)
                for kvh in range(2):
                    pairs = (2 * kvh, 2 * kvh + 1)
                    for j in range(2):
                        dpf = _mm_nt(dos[kvh], v_ops[kvh][j])
                        for r, p in enumerate(pairs):
                            hd = 2 * p + j
                            ds = probs[hd] * (_wrap(dpf[BLOCK * r:BLOCK * (r + 1)], tri) - delta[hd])
                            dsink_ref[hd:hd + 1, :] += jnp.sum(-sink_p[hd] * delta[hd], axis=0, keepdims=True)
                            dbias_ref[hd] += ds
                            ds16[hd] = _unwrap16(ds, tri)
                dk_acc = [[None, None], [None, None]]
                dv_acc = [[None, None], [None, None]]
                for kvh in range(2):
                    pairs = (2 * kvh, 2 * kvh + 1)
                    dq = jnp.zeros((2 * BLOCK, LANES), F32)
                    for j in range(2):
                        dss = jnp.concatenate([ds16[2 * p + j] for p in pairs], axis=0)
                        pss = jnp.concatenate([p16[2 * p + j] for p in pairs], axis=0)
                        dq = dq + _mm(dss, k_ops[kvh][j])
                        dk_acc[kvh][j] = _mm_tn(dss, qs[kvh])
                        dv_acc[kvh][j] = _mm_tn(pss, dos[kvh])
                    for r, p in enumerate(pairs):
                        dqz_ref[rows, LANES * p:LANES * (p + 1)] = (dq[BLOCK * r:BLOCK * (r + 1)] * SCALE).astype(BF16)

                def fold(acc):
                    return jnp.where(low_kv,
                                     acc[0][0] + pltpu.roll(acc[0][1], HEAD_DIM, 1),
                                     pltpu.roll(acc[1][0], HEAD_DIM, 1) + acc[1][1])

                acc_ref[pl.ds(start, 2 * BLOCK), :] += jnp.concatenate(
                    [fold(dk_acc) * SCALE, fold(dv_acc)], axis=1)
                return carry

            lax.fori_loop(0, per_tile, block, 0)
            dkv_ref[...] = acc_ref[0:tq, :].astype(BF16)
            acc_ref[0:BLOCK, :] = acc_ref[tq:tq + BLOCK, :]

        @pl.when(t == nt)
        def _():
            dkv_ref[0:BLOCK, :] = acc_ref[0:BLOCK, :].astype(BF16)
            dkv_ref[BLOCK:, :] = jnp.zeros((tq - BLOCK, 2 * KV_WIDTH), BF16)

    def cur_map(t):
        return (jnp.minimum(t, nt - 1), 0)

    def prev_map(t):
        return (jnp.maximum(jnp.minimum(t, nt - 1) * per_tile - 1, 0), K_OFF // (2 * KV_WIDTH))

    blocks = [_nbytes((tq, PB_WIDTH), BF16), _nbytes((BLOCK, 2 * KV_WIDTH), BF16), _nbytes((tq, B_WIDTH), BF16),
              2 * _nbytes((B_HEADS, BLOCK, BLOCK), F32), _nbytes((tq, qz_width), BF16),
              _nbytes((tq, 2 * KV_WIDTH), BF16), _nbytes((B_HEADS, LANES), F32)]
    scratch = _nbytes((tq + BLOCK, 2 * KV_WIDTH), BF16) + _nbytes((tq + BLOCK, 2 * KV_WIDTH), F32)
    return pl.pallas_call(
        body, name=f"attn_bwd_{layer}", grid=(nt + 1,),
        in_specs=[pl.BlockSpec(memory_space=pltpu.SMEM),
                  pl.BlockSpec((tq, PB_WIDTH), cur_map),
                  pl.BlockSpec((BLOCK, 2 * KV_WIDTH), prev_map),
                  pl.BlockSpec((tq, B_WIDTH), cur_map),
                  pl.BlockSpec((B_HEADS, BLOCK, BLOCK), lambda t: (0, 0, 0))],
        out_specs=[pl.BlockSpec((tq, qz_width), cur_map),
                   pl.BlockSpec((tq, 2 * KV_WIDTH), lambda t: (t, 0)),
                   pl.BlockSpec((B_HEADS, BLOCK, BLOCK), lambda t: (0, 0, 0)),
                   pl.BlockSpec((B_HEADS, LANES), lambda t: (0, 0))],
        out_shape=[pltpu.HBM((tokens, qz_width), BF16),
                   pltpu.HBM((tokens + tq, 2 * KV_WIDTH), BF16),
                   pltpu.HBM((B_HEADS, BLOCK, BLOCK), F32),
                   pltpu.HBM((B_HEADS, LANES), F32)],
        scratch_shapes=[pltpu.VMEM((tq + BLOCK, 2 * KV_WIDTH), BF16),
                        pltpu.VMEM((tq + BLOCK, 2 * KV_WIDTH), F32)],
        compiler_params=pltpu.CompilerParams(dimension_semantics=("arbitrary",),
                                             vmem_limit_bytes=_vmem_limit(blocks, scratch)),
    )(sinks, _hbm(pb), _hbm(pb), _hbm(dyb), _hbm(bias))


def _inproj_bwd_dx(da, dqz, dkv, wa, wb, x2, g, pre_g3, layer):
    tokens = x2.shape[0]
    tm = _token_tile(tokens)

    def body(da_ref, dqz_ref, dkv_ref, wa_ref, wb_ref, x_ref, g_ref, pg_ref, gn_ref, dpg_ref):
        @pl.when(pl.program_id(0) == 0)
        def _():
            dpg_ref[...] = jnp.zeros_like(dpg_ref)

        dh = _mm_nt(da_ref[...], wa_ref[...])
        dh += _mm_nt(dqz_ref[:, 0:B_WIDTH], wb_ref[:, 0:B_WIDTH])
        dh += _mm_nt(dkv_ref[...], wb_ref[:, K_OFF:BZ_OFF])
        dh += _mm_nt(dqz_ref[:, B_WIDTH:2 * B_WIDTH], wb_ref[:, BZ_OFF:PB_WIDTH])
        x = x_ref[...]
        r = lax.rsqrt(jnp.mean(x * x, axis=-1, keepdims=True) + NORM_EPS)
        xhat = x * r
        dhg = dh * pg_ref[...]
        dpg_ref[...] += jnp.sum(dh * xhat, axis=0, keepdims=True)
        gn_ref[...] = g_ref[...] + r * (dhg - xhat * jnp.mean(dhg * xhat, axis=-1, keepdims=True))

    blocks = [_nbytes((tm, PA_WIDTH), BF16), _nbytes((tm, 2 * B_WIDTH), BF16), _nbytes((tm, 2 * KV_WIDTH), BF16),
              _nbytes((D_MODEL, IN_WIDTH), BF16), 3 * _nbytes((tm, D_MODEL), F32)]
    return pl.pallas_call(
        body, name=f"inproj_bwd_dx_{layer}", grid=(tokens // tm,),
        in_specs=[pl.BlockSpec((tm, PA_WIDTH), lambda i: (i, 0)),
                  pl.BlockSpec((tm, 2 * B_WIDTH), lambda i: (i, 0)),
                  pl.BlockSpec((tm, 2 * KV_WIDTH), lambda i: (i, 0)),
                  pl.BlockSpec((D_MODEL, PA_WIDTH), lambda i: (0, 0)),
                  pl.BlockSpec((D_MODEL, PB_WIDTH), lambda i: (0, 0)),
                  pl.BlockSpec((tm, D_MODEL), lambda i: (i, 0)),
                  pl.BlockSpec((tm, D_MODEL), lambda i: (i, 0)),
                  pl.BlockSpec((None, 1, D_MODEL), lambda i: (layer, 0, 0))],
        out_specs=[pl.BlockSpec((tm, D_MODEL), lambda i: (i, 0)),
                   pl.BlockSpec((1, D_MODEL), lambda i: (0, 0))],
        out_shape=[pltpu.HBM((tokens,D_MODEL), F32),
                   pltpu.HBM((1,D_MODEL), F32)],
        compiler_params=pltpu.CompilerParams(dimension_semantics=("arbitrary",),
                                             vmem_limit_bytes=_vmem_limit(blocks)),
    )(_hbm(da), _hbm(dqz), _hbm(dkv), _hbm(wa), _hbm(wb), _hbm(x2), _hbm(g), pre_g3)


def _inproj_bwd_dw(h, da, dqz, dkv, layer):
    tokens = h.shape[0]
    tm = _token_tile(tokens)

    def body(h_ref, da_ref, dqz_ref, dkv_ref, dwa_ref, dwb_ref):
        @pl.when(pl.program_id(0) == 0)
        def _():
            dwa_ref[...] = jnp.zeros_like(dwa_ref)
            dwb_ref[...] = jnp.zeros_like(dwb_ref)

        hv = h_ref[...]
        dwa_ref[...] += _mm_tn(hv, da_ref[...])
        dwb_ref[:, 0:B_WIDTH] += _mm_tn(hv, dqz_ref[:, 0:B_WIDTH])
        dwb_ref[:, K_OFF:BZ_OFF] += _mm_tn(hv, dkv_ref[...])
        dwb_ref[:, BZ_OFF:PB_WIDTH] += _mm_tn(hv, dqz_ref[:, B_WIDTH:2 * B_WIDTH])

    blocks = [_nbytes((tm, D_MODEL), BF16), _nbytes((tm, PA_WIDTH), BF16), _nbytes((tm, 2 * B_WIDTH), BF16),
              _nbytes((tm, 2 * KV_WIDTH), BF16), _nbytes((D_MODEL, IN_WIDTH), F32)]
    return pl.pallas_call(
        body, name=f"inproj_bwd_dw_{layer}", grid=(tokens // tm,),
        in_specs=[pl.BlockSpec((tm, D_MODEL), lambda i: (i, 0)),
                  pl.BlockSpec((tm, PA_WIDTH), lambda i: (i, 0)),
                  pl.BlockSpec((tm, 2 * B_WIDTH), lambda i: (i, 0)),
                  pl.BlockSpec((tm, 2 * KV_WIDTH), lambda i: (i, 0))],
        out_specs=[pl.BlockSpec((D_MODEL, PA_WIDTH), lambda i: (0, 0)),
                   pl.BlockSpec((D_MODEL, PB_WIDTH), lambda i: (0, 0))],
        out_shape=[pltpu.HBM((D_MODEL,PA_WIDTH), F32),
                   pltpu.HBM((D_MODEL,PB_WIDTH), F32)],
        compiler_params=pltpu.CompilerParams(dimension_semantics=("arbitrary",),
                                             vmem_limit_bytes=_vmem_limit(blocks)),
    )(_hbm(h), _hbm(da), _hbm(dqz), _hbm(dkv))


def _rel_bias_grad(dbias_layers, bucket):
    n = len(dbias_layers)

    def body(*refs):
        bucket_ref, out_ref = refs[n], refs[n + 1]
        bk = bucket_ref[...]
        row = lax.broadcasted_iota(jnp.int32, (REL_BUCKETS, LANES), 0)
        lane = lax.broadcasted_iota(jnp.int32, (REL_BUCKETS, LANES), 1)
        out = jnp.zeros((REL_BUCKETS, LANES), F32)
        for h in range(B_HEADS):
            tot = refs[0][h]
            for a in range(1, n):
                tot = tot + refs[a][h]

            def one(b, acc, tot=tot, h=h):
                s = jnp.sum(jnp.where(bk == b, tot, 0.0), axis=-1, keepdims=True)
                s = jnp.sum(s, axis=0, keepdims=True)
                return jnp.where(jnp.logical_and(row == b, lane == h), s, acc)

            out = lax.fori_loop(0, REL_BUCKETS, one, out)
        out_ref[...] = out

    vm = pl.BlockSpec(memory_space=pltpu.VMEM)
    return pl.pallas_call(
        body, name="rel_bias_grad",
        out_shape=jax.ShapeDtypeStruct((REL_BUCKETS, LANES), F32),
        in_specs=[vm] * (n + 1), out_specs=vm,
    )(*dbias_layers, bucket)


def _reduce_adamw(slots, w, m, v, name):
    rows, cols = w.shape
    tr = _row_tile(rows)
    c1 = 1.0 / (1.0 - ADAM_B1 ** ADAM_STEP)
    c2 = 1.0 / (1.0 - ADAM_B2 ** ADAM_STEP)

    def body(s_ref, w_ref, m_ref, v_ref, g_ref, d_ref, nm_ref, nv_ref):
        g = s_ref[0].astype(F32)
        for i in range(1, N_DEV):
            g = g + s_ref[i].astype(F32)
        nm = ADAM_B1 * m_ref[...] + (1.0 - ADAM_B1) * g
        nv = ADAM_B2 * v_ref[...] + (1.0 - ADAM_B2) * (g * g)
        g_ref[...] = g
        nm_ref[...] = nm
        nv_ref[...] = nv
        d_ref[...] = -ADAM_LR * ((nm * c1) / (jnp.sqrt(nv * c2) + ADAM_EPS) + ADAM_WD * w_ref[...])

    blocks = [_nbytes((N_DEV, tr, cols), slots.dtype), 7 * _nbytes((tr, cols), F32)]
    tile = pl.BlockSpec((tr, cols), lambda i: (i, 0))
    return pl.pallas_call(
        body, name=name, grid=(rows // tr,),
        in_specs=[pl.BlockSpec((N_DEV, tr, cols), lambda i: (0, i, 0)), tile, tile, tile],
        out_specs=[tile] * 4,
        out_shape=[pltpu.HBM((rows, cols), F32)] * 4,
        compiler_params=pltpu.CompilerParams(dimension_semantics=("parallel",),
                                             vmem_limit_bytes=_vmem_limit(blocks)),
    )(_hbm(slots), _hbm(w), _hbm(m), _hbm(v))


_SMALL = ("pre_norm_g","post_norm_g", "ln_v_g", "ln_v_b", "b_spatial", "sinks", "rel_bias")


def _pack_small(parts):
    slabs = []
    for name in _SMALL:
        flat = parts[name].astype(F32).reshape(-1)
        pad = (-flat.shape[0]) % (8 * LANES)
        slabs.append(jnp.pad(flat, (0, pad)).reshape(-1, LANES))
    return jnp.concatenate(slabs, axis=0)


def _unpack_small(slab, shapes):
    out, row = {}, 0
    for name in _SMALL:
        size = int(np.prod(shapes[name]))
        rows = -(-size // (8 * LANES)) * 8
        out[name] = slab[row:row + rows].reshape(-1)[:size].reshape(shapes[name])
        row += rows
    return out


def kernel(x, pre_norm_g, w_in, ln_v_g, ln_v_b, w_spatial, b_spatial, sinks, rel_bias, w_out, post_norm_g, loss_target, m_pre_norm_g, m_w_in, m_ln_v_g, m_ln_v_b, m_w_spatial, m_b_spatial, m_sinks, m_rel_bias, m_w_out, m_post_norm_g, v_pre_norm_g, v_w_in, v_ln_v_g, v_ln_v_b, v_w_spatial, v_b_spatial, v_sinks, v_rel_bias, v_w_out, v_post_norm_g):
    b_loc, seq, _ = x.shape
    tokens = b_loc * seq
    depth = w_in.shape[0]
    in_shard = w_in.shape[2]
    out_shard = w_out.shape[1]
    assert in_shard * N_DEV == IN_WIDTH and out_shard * N_DEV == D_MODEL and seq % BLOCK == 0

    me = _slot(lax.axis_index("x"), lax.axis_index("y"), lax.axis_index("c"))
    w_in16, w_out16 = w_in.astype(BF16), w_out.astype(BF16)

    def gather_start(layer, not_before):
        shards, _ = lax.optimization_barrier(((w_in16[layer], w_out16[layer]), not_before))
        zones = [lax.empty((N_DEV, D_MODEL, in_shard), BF16), lax.empty((N_DEV, out_shard, D_MODEL), BF16)]
        sems, sends, zones, token = _exchange_start(list(shards), zones, [0, 0], f"weights_send_{layer}")
        return (sems, sends, [0, 0]), zones, token

    def full_weights(gin, gout):
        w_full = gin.transpose(1, 0, 2).reshape(D_MODEL, IN_WIDTH)
        return w_full[:, :PA_WIDTH], w_full[:, PA_WIDTH:], gout.reshape(D_MODEL, D_MODEL)

    def gather_wait(layer, started, zones, after):
        gin, gout = _exchange_wait([started], zones, after, f"weights_wait_{layer}")
        gin = lax.dynamic_update_index_in_dim(gin, w_in16[layer], me, 0)
        gout = lax.dynamic_update_index_in_dim(gout, w_out16[layer], me, 0)
        return full_weights(gin, gout)

    causal = jnp.tril(jnp.ones((CHUNK, CHUNK), dtype=bool))
    wm = jnp.where(causal, w_spatial, 0.0).astype(BF16)
    wmt = _hbm(jnp.swapaxes(wm, -1, -2))
    wm = _hbm(wm)
    bsb = _hbm(jnp.repeat(jnp.swapaxes(b_spatial, -1, -2), HEAD_DIM, axis=-1))
    pre_g3 = _hbm(pre_norm_g.reshape(depth, 1, D_MODEL))
    post_g3 = _hbm(post_norm_g.reshape(depth, 1, D_MODEL))
    ln_g3 = _hbm(ln_v_g.reshape(depth, 1, A_WIDTH))
    ln_b3 = _hbm(ln_v_b.reshape(depth, 1, A_WIDTH))
    bucket = jnp.asarray(_bucket_table())
    bias = _bias_table(rel_bias, bucket)

    xs, saved, weights = [x.reshape(tokens, D_MODEL)], [], []
    pending = None
    for layer in range(depth):
        if layer == 0:
            wa, wb, wo = full_weights(*_all_gather([w_in16[0], w_out16[0]], "weights_gather_0"))
        else:
            wa, wb, wo = gather_wait(layer, pending[0], pending[1], xs[-1])
        pre_g_fwd = pre_g3
        if layer + 1 < depth:
            pending = gather_start(layer + 1, xs[-1])
            pre_g_fwd = _hbm(pre_norm_g.reshape(depth, 1, D_MODEL) + pending[2][0, 0])
        h, pa, pb = _inproj_fwd(xs[-1], pre_g_fwd, wa, wb, layer)
        ya = _gmlp_fwd(pa, ln_g3, ln_b3, wm, bsb, layer)
        yb = _attn_fwd(pb, bias, sinks, b_loc, layer)
        y, x_next = _outproj_fwd(ya, yb, wo, xs[-1], post_g3, layer)
        saved.append((h, pa, pb, ya, yb, y))
        weights.append((wa, wb, wo))
        xs.append(x_next)
    g, loss_part = _loss_head(xs[-1], loss_target.reshape(tokens, D_MODEL))
    loss = lax.psum(loss_part[0, 0], ("x", "y", "c"))

    grads = {name: [None] * depth for name in ("pre_norm_g","post_norm_g", "ln_v_g", "ln_v_b",
                                               "b_spatial", "sinks", "dbias")}
    zone_in = lax.empty((N_DEV, depth * D_MODEL, in_shard), BF16)
    zone_out = lax.empty((N_DEV, depth * out_shard, D_MODEL), BF16)
    zone_ws = lax.empty((N_DEV, depth * A_GROUPS * CHUNK, CHUNK), F32)
    started_in, started_out, own_in, own_out = [], [], [None] * depth, [None] * depth
    started_ws, own_ws = [], [None] * depth
    for layer in reversed(range(depth)):
        h, pa, pb, ya, yb, y = saved[layer]
        wa, wb, wo = weights[layer]
        dya, dyb, dwo, dpost = _outproj_bwd(g, y, ya, yb, wo, post_g3, layer)
        send_out = dwo.astype(BF16).reshape(N_DEV, out_shard, D_MODEL)
        own_out[layer] = lax.dynamic_index_in_dim(send_out, me, 0, keepdims=False)
        sems, sends, (zone_out,), token = _exchange_start([send_out], [zone_out], [layer * out_shard],
                                                          f"grads_send_out_{layer}")
        started_out.append((sems, sends, [layer * out_shard]))
        ln_g_bwd = _hbm(ln_v_g.reshape(depth, 1, A_WIDTH) + token[0, 0])
        da, dws, dbs, dlg, dlb = _gmlp_bwd(pa, dya, ln_g_bwd, ln_b3, wm, wmt, bsb, layer)
        own_ws[layer] = dws.reshape(A_GROUPS * CHUNK, CHUNK)
        sems, sends, (zone_ws,), token_ws = _exchange_start([own_ws[layer]], [zone_ws], [layer * A_GROUPS * CHUNK],
                                                            f"grads_send_ws_{layer}")
        started_ws.append((sems, sends, [layer * A_GROUPS * CHUNK]))
        dqz, dkv_shifted, grads["dbias"][layer], dsink = _attn_bwd(pb, dyb, bias, sinks, b_loc, layer)
        dkv = dkv_shifted[BLOCK:BLOCK + tokens]
        dwa, dwb = _inproj_bwd_dw(h, da, dqz, dkv, layer)
        send_in = jnp.concatenate([dwa, dwb], axis=-1).astype(BF16).reshape(D_MODEL, N_DEV, in_shard).transpose(1, 0, 2)
        own_in[layer] = lax.dynamic_index_in_dim(send_in, me, 0, keepdims=False)
        sems, sends, (zone_in,), token = _exchange_start([send_in], [zone_in], [layer * D_MODEL],
                                                         f"grads_send_in_{layer}")
        started_in.append((sems, sends, [layer * D_MODEL]))
        pre_g_bwd = _hbm(pre_norm_g.reshape(depth, 1, D_MODEL) + (token[0, 0] + token_ws[0, 0]))
        g, dpre = _inproj_bwd_dx(da, dqz, dkv, wa, wb, xs[layer], g, pre_g_bwd, layer)
        grads["b_spatial"][layer] = dbs[:, :A_GROUPS].T
        grads["ln_v_g"][layer] = dlg[0]
        grads["ln_v_b"][layer] = dlb[0]
        grads["sinks"][layer] = dsink[:, 0]
        grads["pre_norm_g"][layer] = dpre[0]
        grads["post_norm_g"][layer] = dpost[0]
    grad_x = g.reshape(x.shape)
    drel = _rel_bias_grad(grads["dbias"], bucket)[:, :B_HEADS]

    (recv_out,) = _exchange_wait(started_out, [zone_out], g, "grads_wait_out")
    recv_out = lax.dynamic_update_index_in_dim(recv_out, jnp.concatenate(own_out, axis=0), me, 0)
    res_out = _reduce_adamw(recv_out, w_out.reshape(-1, D_MODEL), m_w_out.reshape(-1, D_MODEL),
                            v_w_out.reshape(-1, D_MODEL), "adamw_w_out")
    res_out = [r.reshape(w_out.shape) for r in res_out]

    (recv_ws,) = _exchange_wait(started_ws, [zone_ws], g, "grads_wait_ws")
    recv_ws = lax.dynamic_update_index_in_dim(recv_ws, jnp.concatenate(own_ws, axis=0), me, 0)
    res_ws = _reduce_adamw(recv_ws, w_spatial.reshape(-1, CHUNK), m_w_spatial.reshape(-1, CHUNK),
                           v_w_spatial.reshape(-1, CHUNK), "adamw_w_spatial")
    res_ws = [r.reshape(w_spatial.shape) for r in res_ws]
    small_w = dict(pre_norm_g=pre_norm_g, post_norm_g=post_norm_g, ln_v_g=ln_v_g,
                   ln_v_b=ln_v_b, b_spatial=b_spatial, sinks=sinks, rel_bias=rel_bias)
    small_m = dict(pre_norm_g=m_pre_norm_g, post_norm_g=m_post_norm_g, ln_v_g=m_ln_v_g,
                   ln_v_b=m_ln_v_b, b_spatial=m_b_spatial, sinks=m_sinks, rel_bias=m_rel_bias)
    small_v = dict(pre_norm_g=v_pre_norm_g, post_norm_g=v_post_norm_g, ln_v_g=v_ln_v_g,
                   ln_v_b=v_ln_v_b, b_spatial=v_b_spatial, sinks=v_sinks, rel_bias=v_rel_bias)
    small_g = {name: jnp.stack(grads[name]) for name in _SMALL if name != "rel_bias"}
    small_g["rel_bias"] = drel
    shapes = {name: small_w[name].shape for name in _SMALL}
    (slots,) = _all_gather([_pack_small(small_g)], "small_grads_all_gather")
    res_small = _reduce_adamw(slots, _pack_small(small_w), _pack_small(small_m), _pack_small(small_v), "adamw_small")

    (recv_in,) = _exchange_wait(started_in, [zone_in], res_small[0], "grads_wait_in")
    recv_in = lax.dynamic_update_index_in_dim(recv_in, jnp.concatenate(own_in, axis=0), me, 0)
    res_in = _reduce_adamw(recv_in, w_in.reshape(-1, in_shard), m_w_in.reshape(-1, in_shard),
                           v_w_in.reshape(-1, in_shard), "adamw_w_in")
    res_in = [r.reshape(w_in.shape) for r in res_in]
    res_small = [_unpack_small(r, shapes) for r in res_small]

    order = ("pre_norm_g", "w_in", "ln_v_g", "ln_v_b", "w_spatial", "b_spatial", "sinks", "rel_bias", "w_out",
             "post_norm_g")
    outs = [loss, grad_x]
    for kind in range(4):
        for name in order:
            if name == "w_in":
                outs.append(res_in[kind])
            elif name == "w_out":
                outs.append(res_out[kind])
            elif name == "w_spatial":
                outs.append(res_ws[kind])
            else:
                outs.append(res_small[kind][name])
    return tuple(outs)
```

```python
import math

import numpy as np
import jax
import jax.numpy as jnp
from jax import lax
from jax.experimental import pallas as pl
from jax.experimental.pallas import tpu as pltpu

F32 = jnp.float32
BF16 = jnp.bfloat16

D_MODEL = 1024
A_WIDTH = 512
A_GROUPS = 8
CHUNK = 128
B_HEADS = 8
HEAD_DIM = 64
B_WIDTH = 512
KV_WIDTH = 128
BLOCK = 128
REL_BUCKETS = 32
REL_MAX_DIST = 128
NORM_EPS = 1e-6
PA_WIDTH = 3 * A_WIDTH
PB_WIDTH = 2 * B_WIDTH + 2 * KV_WIDTH
IN_WIDTH = PA_WIDTH + PB_WIDTH
K_OFF, V_OFF, BZ_OFF = B_WIDTH, B_WIDTH + KV_WIDTH, B_WIDTH + 2 * KV_WIDTH
SCALE = HEAD_DIM ** -0.5
NEG = -1e30
N_DEV = 8
LANES = 128

ADAM_LR = 0.001
ADAM_B1 = 0.9
ADAM_B2 = 0.999
ADAM_EPS = 1e-08
ADAM_WD = 0.01
ADAM_STEP = 10

V7X_VMEM_BYTES = 64 * 1024 * 1024
VMEM_TEMP_BYTES = 12 * 1024 * 1024
MESH = pl.DeviceIdType.MESH


def _vmem_limit(block_bytes, scratch_bytes=0):
    need = 2 * sum(block_bytes) + scratch_bytes + VMEM_TEMP_BYTES
    return int(min(need, V7X_VMEM_BYTES - 8 * 1024 * 1024))


def _nbytes(shape, dtype):
    return int(np.prod(shape)) * jnp.dtype(dtype).itemsize


def _token_tile(tokens):
    tile = min(512, tokens // 2)
    assert tokens % tile == 0 and tile % CHUNK == 0, tokens
    return tile


def _row_tile(rows, cap=512):
    best = 8
    for t in range(8, cap + 1, 8):
        if rows % t == 0:
            best = t
    assert rows % best == 0, rows
    return best


def _mm(a, b):
    return lax.dot_general(a, b, (((1,), (0,)), ((), ())), preferred_element_type=F32)


def _mm_nt(a, b):
    return lax.dot_general(a, b, (((1,), (1,)), ((), ())), preferred_element_type=F32)


def _mm_tn(a, b):
    return lax.dot_general(a, b, (((0,), (0,)), ((), ())), preferred_element_type=F32)


_GELU_C = math.sqrt(2.0 / math.pi)


_GELU_A = _GELU_C * 0.044715


def _gelu_parts(x):
    x2 = x * x
    t = jnp.tanh(x * (_GELU_C + _GELU_A * x2))
    return x2, t, 0.5 + 0.5 * t


def _gelu(x):
    return x * _gelu_parts(x)[2]


def _gelu_and_grad(x):
    x2, t, half_plus = _gelu_parts(x)
    grad = half_plus + (0.5 * x) * (1.0 - t * t) * (_GELU_C + (3.0 * _GELU_A) * x2)
    return x * half_plus, grad


def _sigmoid(x):
    return 0.5 + 0.5 * jnp.tanh(0.5 * x)


def _bucket_table():
    q = np.arange(BLOCK)[:, None]
    k = np.arange(BLOCK)[None, :]
    dist = np.where(k <= q, q - k, q + BLOCK - k)
    max_exact = REL_BUCKETS // 2
    safe = np.maximum(dist, 1).astype(np.float32)
    large = max_exact + (np.log(safe / np.float32(max_exact)) / np.float32(math.log(REL_MAX_DIST / max_exact))
                         * np.float32(REL_BUCKETS - max_exact)).astype(np.int32)
    large = np.minimum(large, REL_BUCKETS - 1)
    assert dist.min() >= 0 and dist.max() < BLOCK
    return np.where(dist < max_exact, dist, large).astype(np.int32)


def _hbm(x):
    return pltpu.with_memory_space_constraint(x, pltpu.HBM)


def _slot(px, py, pc):
    return 4 * px + 2 * py + pc


def _all_gather(arrs, name):
    n = len(arrs)

    def body(*refs):
        ins, outs = refs[:n], refs[n:2 * n]
        send_sems, recv_sems, local_sems = refs[2 * n:]
        x, y, c = lax.axis_index("x"), lax.axis_index("y"), lax.axis_index("c")
        me, sibling = (x, y, c), (x, y, 1 - c)
        chips = [(1 - x, y), (x, 1 - y), (1 - x, 1 - y)]

        def copy(a, k, block, to, src=None):
            dst = outs[a].at[_slot(*block)]
            return pltpu.make_async_remote_copy(
                src_ref=dst if src is None else src, dst_ref=dst,
                send_sem=send_sems.at[7 * a + k], recv_sem=recv_sems.at[7 * a + k],
                device_id=to, device_id_type=MESH)

        mine = [pltpu.make_async_copy(ins[a], outs[a].at[_slot(*me)], local_sems.at[a]) for a in range(n)]
        for cp in mine:
            cp.start()
        first = []
        for a in range(n):
            first.append(copy(a, 0, me, sibling, src=ins[a]))
            first += [copy(a, 1 + j, me, (*chip, c), src=ins[a]) for j, chip in enumerate(chips)]
        for cp in first:
            cp.start()
        passed = []
        for j, chip in enumerate(chips):
            for a in range(n):
                copy(a, 1 + j, (*chip, c), me).wait_recv()
                fwd = copy(a, 4 + j, (*chip, c), sibling)
                fwd.start()
                passed.append(fwd)
        for a in range(n):
            copy(a, 0, sibling, me).wait_recv()
            for j, chip in enumerate(chips):
                copy(a, 4 + j, (*chip, 1 - c), me).wait_recv()
        for cp in first + passed:
            cp.wait_send()
        for cp in mine:
            cp.wait()

    any_spec = pl.BlockSpec(memory_space=pl.ANY)
    return pl.pallas_call(
        body, name=name,
        out_shape=[jax.ShapeDtypeStruct((N_DEV,) + a.shape, a.dtype) for a in arrs],
        in_specs=[any_spec] * n, out_specs=[any_spec] * n,
        scratch_shapes=[pltpu.SemaphoreType.DMA((7 * n,)), pltpu.SemaphoreType.DMA((7 * n,)),
                        pltpu.SemaphoreType.DMA((n,))],
    )(*arrs)


_HBM_SPEC = pl.BlockSpec(memory_space=pltpu.HBM)
_SEM_SPEC = pl.BlockSpec(memory_space=pltpu.SEMAPHORE)
_DATAFLOW = pltpu.SideEffectType.DATAFLOW_SIDE_EFFECTING


def _peers():
    x, y, c = lax.axis_index("x"), lax.axis_index("y"), lax.axis_index("c")
    peers = []
    for k in range(1, N_DEV):
        fx, fy, fc = (k >> 2) & 1, (k >> 1) & 1, k & 1
        peers.append((1 - x if fx else x, 1 - y if fy else y, 1 - c if fc else c))
    return (x, y, c), peers


def _exchange_copy(send_ref, land_ref, row_off, src_slot, dst_slot, sems, idx, peer):
    rows = send_ref.shape[-2]
    src = send_ref.at[src_slot] if len(send_ref.shape) == 3 else send_ref
    return pltpu.make_async_remote_copy(
        src_ref=src, dst_ref=land_ref.at[dst_slot, pl.ds(row_off, rows), :],
        send_sem=sems[0].at[idx], recv_sem=sems[1].at[idx], device_id=peer, device_id_type=MESH)


def _exchange_start(sends, lands, row_offs, name):
    n = len(sends)

    def body(*refs):
        ins, zones = refs[:n], refs[n:2 * n]
        sems = refs[2 * n:2 * n + 2]
        token = refs[-1]
        me, peers = _peers()
        for a in range(n):
            for k, peer in enumerate(peers):
                _exchange_copy(ins[a], zones[a], row_offs[a], _slot(*peer), _slot(*me), sems, 7 * a + k, peer).start()
        token[...] = jnp.zeros_like(token)

    arrays = [_hbm(a) for a in list(sends) + list(lands)]
    out = pl.pallas_call(
        body, name=name,
        out_shape=(pltpu.SemaphoreType.DMA((7 * n,)), pltpu.SemaphoreType.DMA((7 * n,)),
                   *[pltpu.HBM(a.shape, a.dtype) for a in arrays], jax.ShapeDtypeStruct((8, LANES), F32)),
        in_specs=[_HBM_SPEC] * (2 * n),
        out_specs=(_SEM_SPEC, _SEM_SPEC, *[_HBM_SPEC] * (2 * n), pl.BlockSpec(memory_space=pltpu.VMEM)),
        input_output_aliases={i: 2 + i for i in range(2 * n)},
        compiler_params=pltpu.CompilerParams(has_side_effects=_DATAFLOW),
    )(*arrays)
    return (out[0], out[1]), list(out[2:2 + n]), list(out[2 + n:2 + 2 * n]), out[-1]


def _exchange_wait(started, lands, after, name):
    n = len(lands)
    flat_sends = [s for _, sends, _ in started for s in sends]
    flat_sems = [s for sems, _, _ in started for s in sems]
    ns = len(flat_sends)

    def body(*refs):
        ins, zones = refs[:ns], refs[ns:ns + n]
        sem_refs = refs[ns + n:ns + n + len(flat_sems)]
        me, peers = _peers()
        pos = 0
        for call, (_, sends, row_offs) in enumerate(started):
            sems = sem_refs[2 * call:2 * call + 2]
            for a in range(len(sends)):
                for k, peer in enumerate(peers):
                    cp = _exchange_copy(ins[pos + a], zones[a], row_offs[a], _slot(*peer), _slot(*peer), sems,
                                        7 * a + k, peer)
                    cp.wait_send()
                    cp.wait_recv()
            pos += len(sends)

    arrays = list(flat_sends) + list(lands)
    out = pl.pallas_call(
        body, name=name,
        out_shape=tuple(pltpu.HBM(a.shape, a.dtype) for a in arrays),
        in_specs=[_HBM_SPEC] * len(arrays) + [_SEM_SPEC] * len(flat_sems) + [pl.BlockSpec(memory_space=pl.ANY)],
        out_specs=tuple([_HBM_SPEC] * len(arrays)),
        input_output_aliases={i: i for i in range(len(arrays))},
        compiler_params=pltpu.CompilerParams(has_side_effects=_DATAFLOW),
    )(*arrays, *flat_sems, after)
    return list(out[ns:])


def _bias_table(rel_bias, bucket):
    def body(rel_ref, bucket_ref, out_ref):
        bk = bucket_ref[...]
        for h in range(B_HEADS):
            def pick(b, acc, h=h):
                return jnp.where(bk == b, rel_ref[b, h], acc)
            out_ref[h] = lax.fori_loop(0, REL_BUCKETS, pick, jnp.zeros((BLOCK, BLOCK), F32))

    return pl.pallas_call(
        body, name="bias_table",
        out_shape=jax.ShapeDtypeStruct((B_HEADS, BLOCK, BLOCK), F32),
        in_specs=[pl.BlockSpec(memory_space=pltpu.SMEM), pl.BlockSpec(memory_space=pltpu.VMEM)],
        out_specs=pl.BlockSpec(memory_space=pltpu.VMEM),
    )(rel_bias, bucket)


def _inproj_fwd(x2, pre_g3, w, layer):
    tokens = x2.shape[0]
    tm = _token_tile(tokens)

    def body(x_ref, g_ref, w_ref, h_ref, pa_ref, pb_ref):
        x = x_ref[...]
        r = lax.rsqrt(jnp.mean(x * x, axis=-1, keepdims=True) + NORM_EPS)
        h = (x * r * g_ref[...]).astype(BF16)
        h_ref[...] = h
        pa_ref[...] = _mm(h, w_ref[:, 0:PA_WIDTH]).astype(BF16)
        pb_ref[...] = _mm(h, w_ref[:, PA_WIDTH:IN_WIDTH]).astype(BF16)

    blocks = [_nbytes((tm, D_MODEL), F32), _nbytes((D_MODEL, IN_WIDTH), BF16),
              _nbytes((tm, D_MODEL), BF16), _nbytes((tm, PA_WIDTH), BF16), _nbytes((tm, PB_WIDTH), BF16)]
    return pl.pallas_call(
        body, name=f"inproj_fwd_{layer}", grid=(tokens // tm,),
        in_specs=[pl.BlockSpec((tm, D_MODEL), lambda i: (i, 0)),
                  pl.BlockSpec((None, 1, D_MODEL), lambda i: (layer, 0, 0)),
                  pl.BlockSpec((D_MODEL, IN_WIDTH), lambda i: (0, 0))],
        out_specs=[pl.BlockSpec((tm, D_MODEL), lambda i: (i, 0)),
                   pl.BlockSpec((tm, PA_WIDTH), lambda i: (i, 0)),
                   pl.BlockSpec((tm, PB_WIDTH), lambda i: (i, 0))],
        out_shape=[pltpu.HBM((tokens,D_MODEL), BF16),
                   pltpu.HBM((tokens,PA_WIDTH), BF16),
                   pltpu.HBM((tokens,PB_WIDTH), BF16)],
        compiler_params=pltpu.CompilerParams(dimension_semantics=("parallel",),
                                             vmem_limit_bytes=_vmem_limit(blocks)),
    )(_hbm(x2), pre_g3, _hbm(w))


def _gmlp_forward_chunk(pa, ln_g, ln_b, wm_ref, bsb):
    pu = pa[:, 0:A_WIDTH].astype(F32)
    pv = pa[:, A_WIDTH:2 * A_WIDTH].astype(F32)
    pz = pa[:, 2 * A_WIDTH:3 * A_WIDTH].astype(F32)
    u, gu = _gelu_and_grad(pu)
    vv, gv = _gelu_and_grad(pv)
    mu = jnp.mean(vv, axis=-1, keepdims=True)
    xc = vv - mu
    rstd = lax.rsqrt(jnp.mean(xc * xc, axis=-1, keepdims=True) + NORM_EPS)
    vhat = xc * rstd
    vnb = (vhat * ln_g + ln_b).astype(BF16)
    low = lax.broadcasted_iota(jnp.int32, (CHUNK, LANES), 1) < HEAD_DIM
    parts = []
    for p in range(A_GROUPS // 2):
        vp = vnb[:, LANES * p:LANES * (p + 1)]
        parts.append(jnp.where(low, _mm(wm_ref[2 * p], vp), _mm(wm_ref[2 * p + 1], vp)))
    mixed = jnp.concatenate(parts, axis=1) + bsb
    sg = _sigmoid(pz)
    return gu, gv, pz, u, vhat, rstd, vnb, mixed, sg


def _gmlp_fwd(pa, ln_g3, ln_b3, wm, bsb, layer):
    tokens = pa.shape[0]
    tc = _token_tile(tokens)

    def body(pa_ref, lg_ref, lb_ref, wm_ref, bsb_ref, ya_ref):
        def chunk(ci, carry):
            rows = pl.ds(pl.multiple_of(ci * CHUNK, CHUNK), CHUNK)
            _, _, pz, u, _, _, _, mixed, sg = _gmlp_forward_chunk(
                pa_ref[rows, :], lg_ref[...], lb_ref[...], wm_ref, bsb_ref[...])
            ya_ref[rows, :] = (u * mixed * (pz * sg)).astype(BF16)
            return carry
        lax.fori_loop(0, tc // CHUNK, chunk, 0)

    blocks = [_nbytes((tc, PA_WIDTH), BF16), _nbytes((A_GROUPS, CHUNK, CHUNK), BF16),
              _nbytes((CHUNK, A_WIDTH), F32), _nbytes((tc, A_WIDTH), BF16)]
    return pl.pallas_call(
        body, name=f"gmlp_fwd_{layer}", grid=(tokens // tc,),
        in_specs=[pl.BlockSpec((tc, PA_WIDTH), lambda i: (i, 0)),
                  pl.BlockSpec((None, 1, A_WIDTH), lambda i: (layer, 0, 0)),
                  pl.BlockSpec((None, 1, A_WIDTH), lambda i: (layer, 0, 0)),
                  pl.BlockSpec((None, A_GROUPS, CHUNK, CHUNK), lambda i: (layer, 0, 0, 0)),
                  pl.BlockSpec((None, CHUNK, A_WIDTH), lambda i: (layer, 0, 0))],
        out_specs=pl.BlockSpec((tc, A_WIDTH), lambda i: (i, 0)),
        out_shape=pltpu.HBM((tokens,A_WIDTH), BF16),
        compiler_params=pltpu.CompilerParams(dimension_semantics=("parallel",),
                                             vmem_limit_bytes=_vmem_limit(blocks)),
    )(_hbm(pa), ln_g3, ln_b3, wm, bsb)


def _kv_variants(kv):
    t = kv.astype(F32)
    rolled = pltpu.roll(t, HEAD_DIM, 1)
    low = lax.broadcasted_iota(jnp.int32, t.shape, 1) < HEAD_DIM
    zero = jnp.zeros_like(t)
    head0 = (jnp.where(low, t, zero).astype(BF16), jnp.where(low, zero, rolled).astype(BF16))
    head1 = (jnp.where(low, rolled, zero).astype(BF16), jnp.where(low, zero, t).astype(BF16))
    return head0, head1


def _band_masks():
    row = lax.broadcasted_iota(jnp.int32, (BLOCK, BLOCK), 0)
    col = lax.broadcasted_iota(jnp.int32, (BLOCK, BLOCK), 1)
    return col <= row


def _wrap(full, tri):
    return jnp.where(tri, full[:, BLOCK:2 * BLOCK], full[:, 0:BLOCK])


def _attn_probs(sf, bias_h, sink, tri, kill):
    s = _wrap(sf, tri) * SCALE + bias_h
    s = jnp.where(kill, NEG, s)
    m = jnp.maximum(jnp.max(s, axis=-1, keepdims=True), sink)
    e = jnp.exp(s - m)
    es = jnp.exp(sink - m)
    inv = 1.0 / (jnp.sum(e, axis=-1, keepdims=True) + es)
    return e * inv, es * inv


def _unwrap16(p, tri):
    zero = jnp.zeros_like(p)
    return jnp.concatenate([jnp.where(tri, zero, p), jnp.where(tri, p, zero)], axis=1).astype(BF16)


def _fill_kv(kv_ref, prev_ref, cur_ref):
    kv_ref[0:BLOCK, :] = prev_ref[...]
    kv_ref[BLOCK:, :] = cur_ref[:, K_OFF:K_OFF + 2 * KV_WIDTH]


def _attn_fwd(pb, bias, sinks, b_loc, layer):
    tokens = pb.shape[0]
    nb = tokens // b_loc // BLOCK
    tq = _token_tile(tokens)
    per_tile = tq // BLOCK

    def body(sink_ref, cur_ref, prev_ref, bias_ref, yb_ref, kv_ref):
        t = pl.program_id(0)
        _fill_kv(kv_ref, prev_ref, cur_ref)
        tri = _band_masks()

        def block(i, carry):
            start = pl.multiple_of(i * BLOCK, BLOCK)
            rows = pl.ds(start, BLOCK)
            first = lax.rem(t * per_tile + i, nb) == 0
            kill = jnp.logical_and(first, jnp.logical_not(tri))
            kv = kv_ref[pl.ds(start, 2 * BLOCK), :]
            k_ops = _kv_variants(kv[:, 0:KV_WIDTH])
            v_ops = _kv_variants(kv[:, KV_WIDTH:2 * KV_WIDTH])
            p16 = {}
            for kvh in range(2):
                pairs = (2 * kvh, 2 * kvh + 1)
                qs = jnp.concatenate([cur_ref[rows, LANES * p:LANES * (p + 1)] for p in pairs], axis=0)
                for j in range(2):
                    sf = _mm_nt(qs, k_ops[kvh][j])
                    for r, p in enumerate(pairs):
                        hd = 2 * p + j
                        probs, _ = _attn_probs(sf[BLOCK * r:BLOCK * (r + 1)], bias_ref[hd], sink_ref[layer, hd],
                                               tri, kill)
                        p16[hd] = _unwrap16(probs, tri)
            for kvh in range(2):
                pairs = (2 * kvh, 2 * kvh + 1)
                out = jnp.zeros((2 * BLOCK, LANES), F32)
                for j in range(2):
                    out = out + _mm(jnp.concatenate([p16[2 * p + j] for p in pairs], axis=0), v_ops[kvh][j])
                for r, p in enumerate(pairs):
                    bz = cur_ref[rows, BZ_OFF + LANES * p:BZ_OFF + LANES * (p + 1)].astype(F32)
                    yb_ref[rows, LANES * p:LANES * (p + 1)] = (
                        out[BLOCK * r:BLOCK * (r + 1)] * (bz * _sigmoid(bz))).astype(BF16)
            return carry

        lax.fori_loop(0, per_tile, block, 0)

    blocks = [_nbytes((tq, PB_WIDTH), BF16), _nbytes((BLOCK, 2 * KV_WIDTH), BF16),
              _nbytes((B_HEADS, BLOCK, BLOCK), F32), _nbytes((tq, B_WIDTH), BF16)]
    scratch = _nbytes((tq + BLOCK, 2 * KV_WIDTH), BF16)
    return pl.pallas_call(
        body, name=f"attn_fwd_{layer}", grid=(tokens // tq,),
        in_specs=[pl.BlockSpec(memory_space=pltpu.SMEM),
                  pl.BlockSpec((tq, PB_WIDTH), lambda t: (t, 0)),
                  pl.BlockSpec((BLOCK, 2 * KV_WIDTH),
                               lambda t: (jnp.maximum(t * per_tile - 1, 0), K_OFF // (2 * KV_WIDTH))),
                  pl.BlockSpec((B_HEADS, BLOCK, BLOCK), lambda t: (0, 0, 0))],
        out_specs=pl.BlockSpec((tq, B_WIDTH), lambda t: (t, 0)),
        out_shape=pltpu.HBM((tokens, B_WIDTH), BF16),
        scratch_shapes=[pltpu.VMEM((tq + BLOCK, 2 * KV_WIDTH), BF16)],
        compiler_params=pltpu.CompilerParams(dimension_semantics=("parallel",),
                                             vmem_limit_bytes=_vmem_limit(blocks, scratch)),
    )(sinks, _hbm(pb), _hbm(pb), _hbm(bias))


def _outproj_fwd(ya, yb, wo, x2, post_g3, layer, target=None):
    tokens = x2.shape[0]
    tm = _token_tile(tokens)
    steps = tokens // tm
    with_loss = target is not None

    def body(*refs):
        ya_ref, yb_ref, woa_ref, wob_ref, x_ref, g_ref = refs[:6]
        y = _mm(ya_ref[...], woa_ref[...]) + _mm(yb_ref[...], wob_ref[...])
        r = lax.rsqrt(jnp.mean(y * y, axis=-1, keepdims=True) + NORM_EPS)
        x_next = x_ref[...] + y * r * g_ref[...]
        if not with_loss:
            y_ref, xn_ref = refs[6:]
            y_ref[...] = y.astype(BF16)
            xn_ref[...] = x_next
            return
        t_ref, y_ref, grad_ref, loss_ref, acc_ref = refs[6:]
        i = pl.program_id(0)

        @pl.when(i == 0)
        def _():
            acc_ref[...] = jnp.zeros_like(acc_ref)

        y_ref[...] = y.astype(BF16)
        err = x_next - t_ref[...]
        grad_ref[...] = err * (1.0 / D_MODEL)
        acc_ref[...] += jnp.sum(err * err, axis=0, keepdims=True)

        @pl.when(i == steps - 1)
        def _():
            total = jnp.sum(acc_ref[...], axis=-1, keepdims=True) * (0.5 / D_MODEL)
            loss_ref[...] = jnp.broadcast_to(total, loss_ref.shape)

    half = D_MODEL // 2
    tile = pl.BlockSpec((tm, D_MODEL), lambda i: (i, 0))
    blocks = [2 * _nbytes((tm, half), BF16), 2 * _nbytes((half, D_MODEL), BF16), 3 * _nbytes((tm, D_MODEL), F32),
              _nbytes((tm, D_MODEL), BF16)]
    in_specs = [pl.BlockSpec((tm, half), lambda i: (i, 0)),
                pl.BlockSpec((tm, half), lambda i: (i, 0)),
                pl.BlockSpec((half, D_MODEL), lambda i: (0, 0)),
                pl.BlockSpec((half, D_MODEL), lambda i: (1, 0)),
                tile,
                pl.BlockSpec((None, 1, D_MODEL), lambda i: (layer, 0, 0))]
    out_specs = [tile, tile]
    out_shape = [pltpu.HBM((tokens, D_MODEL), BF16), pltpu.HBM((tokens, D_MODEL), F32)]
    operands = [_hbm(ya), _hbm(yb), _hbm(wo), _hbm(wo), _hbm(x2), post_g3]
    if with_loss:
        in_specs.append(tile)
        out_specs.append(pl.BlockSpec((1, LANES), lambda i: (0, 0)))
        out_shape.append(pltpu.HBM((1, LANES), F32))
        operands.append(_hbm(target))
    return pl.pallas_call(
        body, name=f"outproj_fwd_{layer}", grid=(steps,),
        in_specs=in_specs, out_specs=out_specs, out_shape=out_shape,
        scratch_shapes=[pltpu.VMEM((1, D_MODEL), F32)] if with_loss else [],
        compiler_params=pltpu.CompilerParams(dimension_semantics=("arbitrary" if with_loss else "parallel",),
                                             vmem_limit_bytes=_vmem_limit(blocks)),
    )(*operands)


def _outproj_bwd(g, y, ya, yb, wo, post_g3, layer):
    tokens = g.shape[0]
    tm = _token_tile(tokens)
    half = D_MODEL // 2
    steps = tokens // tm

    def body(g_ref, y_ref, ya_ref, yb_ref, woa_ref, wob_ref, pg_ref, dya_ref, dyb_ref, dwo16_ref, dpg_ref, dwo_ref):
        @pl.when(pl.program_id(0) == 0)
        def _():
            dwo_ref[...] = jnp.zeros_like(dwo_ref)
            dpg_ref[...] = jnp.zeros_like(dpg_ref)

        gv = g_ref[...]
        yf = y_ref[...].astype(F32)
        r = lax.rsqrt(jnp.mean(yf * yf, axis=-1, keepdims=True) + NORM_EPS)
        yhat = yf * r
        gg = gv * pg_ref[...]
        dy = r * (gg - yhat * jnp.mean(gg * yhat, axis=-1, keepdims=True))
        dpg_ref[...] += jnp.sum(gv * yhat, axis=0, keepdims=True)
        dyb16 = dy.astype(BF16)
        dya_ref[...] = _mm_nt(dyb16, woa_ref[...]).astype(BF16)
        dyb_ref[...] = _mm_nt(dyb16, wob_ref[...]).astype(BF16)
        dwo_ref[0:half, :] += _mm_tn(ya_ref[...], dyb16)
        dwo_ref[half:D_MODEL, :] += _mm_tn(yb_ref[...], dyb16)

        @pl.when(pl.program_id(0) == steps - 1)
        def _():
            dwo16_ref[...] = dwo_ref[...].astype(BF16)

    blocks = [_nbytes((tm, D_MODEL), F32), _nbytes((tm, D_MODEL), BF16), 4 * _nbytes((tm, half), BF16),
              2 * _nbytes((half, D_MODEL), BF16), _nbytes((D_MODEL, D_MODEL), BF16)]
    return pl.pallas_call(
        body, name=f"outproj_bwd_{layer}", grid=(tokens // tm,),
        in_specs=[pl.BlockSpec((tm, D_MODEL), lambda i: (i, 0)),
                  pl.BlockSpec((tm, D_MODEL), lambda i: (i, 0)),
                  pl.BlockSpec((tm, half), lambda i: (i, 0)),
                  pl.BlockSpec((tm, half), lambda i: (i, 0)),
                  pl.BlockSpec((half, D_MODEL), lambda i: (0, 0)),
                  pl.BlockSpec((half, D_MODEL), lambda i: (1, 0)),
                  pl.BlockSpec((None, 1, D_MODEL), lambda i: (layer, 0, 0))],
        out_specs=[pl.BlockSpec((tm, half), lambda i: (i, 0)),
                   pl.BlockSpec((tm, half), lambda i: (i, 0)),
                   pl.BlockSpec((D_MODEL, D_MODEL), lambda i: (0, 0)),
                   pl.BlockSpec((1, D_MODEL), lambda i: (0, 0))],
        out_shape=[pltpu.HBM((tokens,half), BF16),
                   pltpu.HBM((tokens,half), BF16),
                   pltpu.HBM((D_MODEL,D_MODEL), BF16),
                   pltpu.HBM((1,D_MODEL), F32)],
        scratch_shapes=[pltpu.VMEM((D_MODEL, D_MODEL), F32)],
        compiler_params=pltpu.CompilerParams(dimension_semantics=("arbitrary",),
                                             vmem_limit_bytes=_vmem_limit(blocks, _nbytes((D_MODEL, D_MODEL), F32))),
    )(_hbm(g), _hbm(y), _hbm(ya), _hbm(yb), _hbm(wo), _hbm(wo), post_g3)


def _gmlp_bwd(pa, dya, ln_g3, ln_b3, wm, wmt, bsb, layer):
    tokens = pa.shape[0]
    tc = _token_tile(tokens)
    steps = tokens // tc

    def body(pa_ref, dya_ref, lg_ref, lb_ref, wm_ref, wmt_ref, bsb_ref,
             da_ref, dws_ref, dbs_ref, dlg_ref, dlb_ref, dbsb_ref):
        i = pl.program_id(0)

        @pl.when(i == 0)
        def _():
            dws_ref[...] = jnp.zeros_like(dws_ref)
            dlg_ref[...] = jnp.zeros_like(dlg_ref)
            dlb_ref[...] = jnp.zeros_like(dlb_ref)
            dbsb_ref[...] = jnp.zeros_like(dbsb_ref)

        ln_g = lg_ref[...]
        low = lax.broadcasted_iota(jnp.int32, (CHUNK, LANES), 1) < HEAD_DIM

        def chunk(ci, carry):
            rows = pl.ds(pl.multiple_of(ci * CHUNK, CHUNK), CHUNK)
            gu, gv, pz, u, vhat, rstd, vnb, mixed, sg = _gmlp_forward_chunk(
                pa_ref[rows, :], ln_g, lb_ref[...], wm_ref, bsb_ref[...])
            dy = dya_ref[rows, :].astype(F32)
            sz = pz * sg
            dy_sz = dy * sz
            du = dy_sz * mixed
            dmixed = dy_sz * u
            dz = dy * (u * mixed) * (sg + sz * (1.0 - sg))
            dbsb_ref[...] += dmixed
            dmb = dmixed.astype(BF16)
            zero = jnp.zeros((CHUNK, LANES), BF16)
            parts = []
            for p in range(A_GROUPS // 2):
                dmp = dmb[:, LANES * p:LANES * (p + 1)]
                vp = vnb[:, LANES * p:LANES * (p + 1)]
                parts.append(jnp.where(low, _mm(wmt_ref[2 * p], dmp), _mm(wmt_ref[2 * p + 1], dmp)))
                dws_ref[2 * p] += _mm_nt(jnp.where(low, dmp, zero), vp)
                dws_ref[2 * p + 1] += _mm_nt(jnp.where(low, zero, dmp), vp)
            dvn = jnp.concatenate(parts, axis=1)
            dlg_ref[...] += jnp.sum(dvn * vhat, axis=0, keepdims=True)
            dlb_ref[...] += jnp.sum(dvn, axis=0, keepdims=True)
            dvh = dvn * ln_g
            dvv = rstd * (dvh - jnp.mean(dvh, axis=-1, keepdims=True)
                          - vhat * jnp.mean(dvh * vhat, axis=-1, keepdims=True))
            da_ref[rows, 0:A_WIDTH] = (du * gu).astype(BF16)
            da_ref[rows, A_WIDTH:2 * A_WIDTH] = (dvv * gv).astype(BF16)
            da_ref[rows, 2 * A_WIDTH:3 * A_WIDTH] = dz.astype(BF16)
            return carry

        lax.fori_loop(0, tc // CHUNK, chunk, 0)

        @pl.when(i == steps - 1)
        def _():
            causal = (lax.broadcasted_iota(jnp.int32, (CHUNK, CHUNK), 0)
                      >= lax.broadcasted_iota(jnp.int32, (CHUNK, CHUNK), 1))
            for h in range(A_GROUPS):
                dws_ref[h] = jnp.where(causal, dws_ref[h], 0.0)
            acc = dbsb_ref[...]
            lane_full = lax.broadcasted_iota(jnp.int32, (CHUNK, A_WIDTH), 1)
            lane_out = lax.broadcasted_iota(jnp.int32, (CHUNK, LANES), 1)
            out = jnp.zeros((CHUNK, LANES), F32)
            for h in range(A_GROUPS):
                in_group = jnp.logical_and(lane_full >= HEAD_DIM * h, lane_full < HEAD_DIM * (h + 1))
                s = jnp.sum(jnp.where(in_group, acc, 0.0), axis=-1, keepdims=True)
                out = jnp.where(lane_out == h, s, out)
            dbs_ref[...] = out

    blocks = [_nbytes((tc, PA_WIDTH), BF16), _nbytes((tc, A_WIDTH), BF16), 2 * _nbytes((A_GROUPS, CHUNK, CHUNK), BF16),
              _nbytes((CHUNK, A_WIDTH), F32), _nbytes((tc, PA_WIDTH), BF16), _nbytes((A_GROUPS, CHUNK, CHUNK), F32),
              _nbytes((CHUNK, LANES), F32)]
    return pl.pallas_call(
        body, name=f"gmlp_bwd_{layer}", grid=(steps,),
        in_specs=[pl.BlockSpec((tc, PA_WIDTH), lambda i: (i, 0)),
                  pl.BlockSpec((tc, A_WIDTH), lambda i: (i, 0)),
                  pl.BlockSpec((None, 1, A_WIDTH), lambda i: (layer, 0, 0)),
                  pl.BlockSpec((None, 1, A_WIDTH), lambda i: (layer, 0, 0)),
                  pl.BlockSpec((None, A_GROUPS, CHUNK, CHUNK), lambda i: (layer, 0, 0, 0)),
                  pl.BlockSpec((None, A_GROUPS, CHUNK, CHUNK), lambda i: (layer, 0, 0, 0)),
                  pl.BlockSpec((None, CHUNK, A_WIDTH), lambda i: (layer, 0, 0))],
        out_specs=[pl.BlockSpec((tc, PA_WIDTH), lambda i: (i, 0)),
                   pl.BlockSpec((A_GROUPS, CHUNK, CHUNK), lambda i: (0, 0, 0)),
                   pl.BlockSpec((CHUNK, LANES), lambda i: (0, 0)),
                   pl.BlockSpec((1, A_WIDTH), lambda i: (0, 0)),
                   pl.BlockSpec((1, A_WIDTH), lambda i: (0, 0))],
        out_shape=[pltpu.HBM((tokens,PA_WIDTH), BF16),
                   pltpu.HBM((A_GROUPS, CHUNK, CHUNK), F32),
                   pltpu.HBM((CHUNK, LANES), F32),
                   pltpu.HBM((1,A_WIDTH), F32),
                   pltpu.HBM((1,A_WIDTH), F32)],
        scratch_shapes=[pltpu.VMEM((CHUNK, A_WIDTH), F32)],
        compiler_params=pltpu.CompilerParams(dimension_semantics=("arbitrary",),
                                             vmem_limit_bytes=_vmem_limit(blocks, _nbytes((CHUNK, A_WIDTH), F32))),
    )(_hbm(pa), _hbm(dya), ln_g3, ln_b3, wm, wmt, bsb)


def _attn_bwd(pb, dyb, bias, dbias_in, sinks, b_loc, layer):
    tokens = pb.shape[0]
    nb = tokens // b_loc // BLOCK
    qz_width = 2 * B_WIDTH
    tq = _token_tile(tokens)
    per_tile = tq // BLOCK
    nt = tokens // tq

    def body(sink_ref, cur_ref, prev_ref, dyb_ref, bias_ref, dbias_in_ref, dqz_ref, dkv_ref, dbias_ref, dsink_ref,
             kv_ref, acc_ref):
        t = pl.program_id(0)

        @pl.when(t == 0)
        def _():
            dbias_ref[...] = dbias_in_ref[...]
            dsink_ref[...] = jnp.zeros_like(dsink_ref)
            acc_ref[0:BLOCK, :] = jnp.zeros((BLOCK, 2 * KV_WIDTH), F32)

        @pl.when(t < nt)
        def _():
            acc_ref[BLOCK:, :] = jnp.zeros((tq, 2 * KV_WIDTH), F32)
            _fill_kv(kv_ref, prev_ref, cur_ref)
            tri = _band_masks()
            low = lax.broadcasted_iota(jnp.int32, (BLOCK, LANES), 1) < HEAD_DIM
            low_kv = lax.broadcasted_iota(jnp.int32, (2 * BLOCK, LANES), 1) < HEAD_DIM

            def block(i, carry):
                start = pl.multiple_of(i * BLOCK, BLOCK)
                rows = pl.ds(start, BLOCK)
                first = lax.rem(t * per_tile + i, nb) == 0
                kill = jnp.logical_and(first, jnp.logical_not(tri))
                kv = kv_ref[pl.ds(start, 2 * BLOCK), :]
                k_ops = _kv_variants(kv[:, 0:KV_WIDTH])
                v_ops = _kv_variants(kv[:, KV_WIDTH:2 * KV_WIDTH])
                probs, sink_p, p16, qs, dos, delta, ds16 = {}, {}, {}, {}, {}, {}, {}
                for kvh in range(2):
                    pairs = (2 * kvh, 2 * kvh + 1)
                    qs[kvh] = jnp.concatenate([cur_ref[rows, LANES * p:LANES * (p + 1)] for p in pairs], axis=0)
                    for j in range(2):
                        sf = _mm_nt(qs[kvh], k_ops[kvh][j])
                        for r, p in enumerate(pairs):
                            hd = 2 * p + j
                            probs[hd], sink_p[hd] = _attn_probs(sf[BLOCK * r:BLOCK * (r + 1)], bias_ref[hd],
                                                                sink_ref[layer, hd], tri, kill)
                            p16[hd] = _unwrap16(probs[hd], tri)
                for kvh in range(2):
                    pairs = (2 * kvh, 2 * kvh + 1)
                    out = jnp.zeros((2 * BLOCK, LANES), F32)
                    for j in range(2):
                        out = out + _mm(jnp.concatenate([p16[2 * p + j] for p in pairs], axis=0), v_ops[kvh][j])
                    d_outs = []
                    for r, p in enumerate(pairs):
                        bz = cur_ref[rows, BZ_OFF + LANES * p:BZ_OFF + LANES * (p + 1)].astype(F32)
                        sg = _sigmoid(bz)
                        dyp = dyb_ref[rows, LANES * p:LANES * (p + 1)].astype(F32)
                        out_p = out[BLOCK * r:BLOCK * (r + 1)]
                        d_out = dyp * (bz * sg)
                        dqz_ref[rows, B_WIDTH + LANES * p:B_WIDTH + LANES * (p + 1)] = (
                            dyp * out_p * (sg * (1.0 + bz * (1.0 - sg)))).astype(BF16)
                        dod = d_out * out_p
                        delta[2 * p] = jnp.sum(jnp.where(low, dod, 0.0), axis=-1, keepdims=True)
                        delta[2 * p + 1] = jnp.sum(jnp.where(low, 0.0, dod), axis=-1, keepdims=True)
                        d_outs.append(d_out.astype(BF16))
                    dos[kvh] = jnp.concatenate(d_outs, axis=0)
                for kvh in range(2):
                    pairs = (2 * kvh, 2 * kvh + 1)
                    for j in range(2):
                        dpf = _mm_nt(dos[kvh], v_ops[kvh][j])
                        for r, p in enumerate(pairs):
                            hd = 2 * p + j
                            ds = probs[hd] * (_wrap(dpf[BLOCK * r:BLOCK * (r + 1)], tri) - delta[hd])
                            dsink_ref[hd:hd + 1, :] += jnp.sum(-sink_p[hd] * delta[hd], axis=0, keepdims=True)
                            dbias_ref[hd] += ds
                            ds16[hd] = _unwrap16(ds, tri)
                dk_acc = [[None, None], [None, None]]
                dv_acc = [[None, None], [None, None]]
                for kvh in range(2):
                    pairs = (2 * kvh, 2 * kvh + 1)
                    dq = jnp.zeros((2 * BLOCK, LANES), F32)
                    for j in range(2):
                        dss = jnp.concatenate([ds16[2 * p + j] for p in pairs], axis=0)
                        pss = jnp.concatenate([p16[2 * p + j] for p in pairs], axis=0)
                        dq = dq + _mm(dss, k_ops[kvh][j])
                        dk_acc[kvh][j] = _mm_tn(dss, qs[kvh])
                        dv_acc[kvh][j] = _mm_tn(pss, dos[kvh])
                    for r, p in enumerate(pairs):
                        dqz_ref[rows, LANES * p:LANES * (p + 1)] = (dq[BLOCK * r:BLOCK * (r + 1)] * SCALE).astype(BF16)

                def fold(acc):
                    return jnp.where(low_kv,
                                     acc[0][0] + pltpu.roll(acc[0][1], HEAD_DIM, 1),
                                     pltpu.roll(acc[1][0], HEAD_DIM, 1) + acc[1][1])

                acc_ref[pl.ds(start, 2 * BLOCK), :] += jnp.concatenate(
                    [fold(dk_acc) * SCALE, fold(dv_acc)], axis=1)
                return carry

            lax.fori_loop(0, per_tile, block, 0)
            dkv_ref[...] = acc_ref[0:tq, :].astype(BF16)
            acc_ref[0:BLOCK, :] = acc_ref[tq:tq + BLOCK, :]

        @pl.when(t == nt)
        def _():
            dkv_ref[0:BLOCK, :] = acc_ref[0:BLOCK, :].astype(BF16)
            dkv_ref[BLOCK:, :] = jnp.zeros((tq - BLOCK, 2 * KV_WIDTH), BF16)

    def cur_map(t):
        return (jnp.minimum(t, nt - 1), 0)

    def prev_map(t):
        return (jnp.maximum(jnp.minimum(t, nt - 1) * per_tile - 1, 0), K_OFF // (2 * KV_WIDTH))

    blocks = [_nbytes((tq, PB_WIDTH), BF16), _nbytes((BLOCK, 2 * KV_WIDTH), BF16), _nbytes((tq, B_WIDTH), BF16),
              3 * _nbytes((B_HEADS, BLOCK, BLOCK), F32), _nbytes((tq, qz_width), BF16),
              _nbytes((tq, 2 * KV_WIDTH), BF16), _nbytes((B_HEADS, LANES), F32)]
    scratch = _nbytes((tq + BLOCK, 2 * KV_WIDTH), BF16) + _nbytes((tq + BLOCK, 2 * KV_WIDTH), F32)
    return pl.pallas_call(
        body, name=f"attn_bwd_{layer}", grid=(nt + 1,),
        in_specs=[pl.BlockSpec(memory_space=pltpu.SMEM),
                  pl.BlockSpec((tq, PB_WIDTH), cur_map),
                  pl.BlockSpec((BLOCK, 2 * KV_WIDTH), prev_map),
                  pl.BlockSpec((tq, B_WIDTH), cur_map),
                  pl.BlockSpec((B_HEADS, BLOCK, BLOCK), lambda t: (0, 0, 0)),
                  pl.BlockSpec((B_HEADS, BLOCK, BLOCK), lambda t: (0, 0, 0))],
        out_specs=[pl.BlockSpec((tq, qz_width), cur_map),
                   pl.BlockSpec((tq, 2 * KV_WIDTH), lambda t: (t, 0)),
                   pl.BlockSpec((B_HEADS, BLOCK, BLOCK), lambda t: (0, 0, 0)),
                   pl.BlockSpec((B_HEADS, LANES), lambda t: (0, 0))],
        out_shape=[pltpu.HBM((tokens, qz_width), BF16),
                   pltpu.HBM((tokens + tq, 2 * KV_WIDTH), BF16),
                   pltpu.HBM((B_HEADS, BLOCK, BLOCK), F32),
                   pltpu.HBM((B_HEADS, LANES), F32)],
        scratch_shapes=[pltpu.VMEM((tq + BLOCK, 2 * KV_WIDTH), BF16),
                        pltpu.VMEM((tq + BLOCK, 2 * KV_WIDTH), F32)],
        compiler_params=pltpu.CompilerParams(dimension_semantics=("arbitrary",),
                                             vmem_limit_bytes=_vmem_limit(blocks, scratch)),
    )(sinks, _hbm(pb), _hbm(pb), _hbm(dyb), _hbm(bias), _hbm(dbias_in))


def _inproj_bwd_dx(da, dqz, dkv, w, x2, g, pre_g3, layer):
    tokens = x2.shape[0]
    tm = _token_tile(tokens)

    def body(da_ref, dqz_ref, dkv_ref, w_ref, x_ref, g_ref, pg_ref, gn_ref, dpg_ref):
        @pl.when(pl.program_id(0) == 0)
        def _():
            dpg_ref[...] = jnp.zeros_like(dpg_ref)

        dh = _mm_nt(da_ref[...], w_ref[:, 0:PA_WIDTH])
        dh += _mm_nt(dqz_ref[:, 0:B_WIDTH], w_ref[:, PA_WIDTH:PA_WIDTH + B_WIDTH])
        dh += _mm_nt(dkv_ref[...], w_ref[:, PA_WIDTH + K_OFF:PA_WIDTH + BZ_OFF])
        dh += _mm_nt(dqz_ref[:, B_WIDTH:2 * B_WIDTH], w_ref[:, PA_WIDTH + BZ_OFF:IN_WIDTH])
        x = x_ref[...]
        r = lax.rsqrt(jnp.mean(x * x, axis=-1, keepdims=True) + NORM_EPS)
        xhat = x * r
        dhg = dh * pg_ref[...]
        dpg_ref[...] += jnp.sum(dh * xhat, axis=0, keepdims=True)
        gn_ref[...] = g_ref[...] + r * (dhg - xhat * jnp.mean(dhg * xhat, axis=-1, keepdims=True))

    blocks = [_nbytes((tm, PA_WIDTH), BF16), _nbytes((tm, 2 * B_WIDTH), BF16), _nbytes((tm, 2 * KV_WIDTH), BF16),
              _nbytes((D_MODEL, IN_WIDTH), BF16), 3 * _nbytes((tm, D_MODEL), F32)]
    return pl.pallas_call(
        body, name=f"inproj_bwd_dx_{layer}", grid=(tokens // tm,),
        in_specs=[pl.BlockSpec((tm, PA_WIDTH), lambda i: (i, 0)),
                  pl.BlockSpec((tm, 2 * B_WIDTH), lambda i: (i, 0)),
                  pl.BlockSpec((tm, 2 * KV_WIDTH), lambda i: (i, 0)),
                  pl.BlockSpec((D_MODEL, IN_WIDTH), lambda i: (0, 0)),
                  pl.BlockSpec((tm, D_MODEL), lambda i: (i, 0)),
                  pl.BlockSpec((tm, D_MODEL), lambda i: (i, 0)),
                  pl.BlockSpec((None, 1, D_MODEL), lambda i: (layer, 0, 0))],
        out_specs=[pl.BlockSpec((tm, D_MODEL), lambda i: (i, 0)),
                   pl.BlockSpec((1, D_MODEL), lambda i: (0, 0))],
        out_shape=[pltpu.HBM((tokens,D_MODEL), F32),
                   pltpu.HBM((1,D_MODEL), F32)],
        compiler_params=pltpu.CompilerParams(dimension_semantics=("arbitrary",),
                                             vmem_limit_bytes=_vmem_limit(blocks)),
    )(_hbm(da), _hbm(dqz), _hbm(dkv), _hbm(w), _hbm(x2), _hbm(g), pre_g3)


def _inproj_bwd_dw(h, da, dqz, dkv, layer):
    tokens = h.shape[0]
    tm = _token_tile(tokens)
    steps = tokens // tm

    def body(h_ref, da_ref, dqz_ref, dkv_ref, dw_ref, acc_ref):
        i = pl.program_id(0)

        @pl.when(i == 0)
        def _():
            acc_ref[...] = jnp.zeros_like(acc_ref)

        hv = h_ref[...]
        acc_ref[:, 0:PA_WIDTH] += _mm_tn(hv, da_ref[...])
        acc_ref[:, PA_WIDTH:PA_WIDTH + B_WIDTH] += _mm_tn(hv, dqz_ref[:, 0:B_WIDTH])
        acc_ref[:, PA_WIDTH + K_OFF:PA_WIDTH + BZ_OFF] += _mm_tn(hv, dkv_ref[...])
        acc_ref[:, PA_WIDTH + BZ_OFF:IN_WIDTH] += _mm_tn(hv, dqz_ref[:, B_WIDTH:2 * B_WIDTH])

        @pl.when(i == steps - 1)
        def _():
            dw_ref[...] = acc_ref[...].astype(BF16)

    blocks = [_nbytes((tm, D_MODEL), BF16), _nbytes((tm, PA_WIDTH), BF16), _nbytes((tm, 2 * B_WIDTH), BF16),
              _nbytes((tm, 2 * KV_WIDTH), BF16), _nbytes((D_MODEL, IN_WIDTH), BF16)]
    return pl.pallas_call(
        body, name=f"inproj_bwd_dw_{layer}", grid=(steps,),
        in_specs=[pl.BlockSpec((tm, D_MODEL), lambda i: (i, 0)),
                  pl.BlockSpec((tm, PA_WIDTH), lambda i: (i, 0)),
                  pl.BlockSpec((tm, 2 * B_WIDTH), lambda i: (i, 0)),
                  pl.BlockSpec((tm, 2 * KV_WIDTH), lambda i: (i, 0))],
        out_specs=pl.BlockSpec((D_MODEL, IN_WIDTH), lambda i: (0, 0)),
        out_shape=pltpu.HBM((D_MODEL, IN_WIDTH), BF16),
        scratch_shapes=[pltpu.VMEM((D_MODEL, IN_WIDTH), F32)],
        compiler_params=pltpu.CompilerParams(dimension_semantics=("arbitrary",),
                                             vmem_limit_bytes=_vmem_limit(blocks, _nbytes((D_MODEL, IN_WIDTH), F32))),
    )(_hbm(h), _hbm(da), _hbm(dqz), _hbm(dkv))


def _rel_bias_grad(dbias, col_bucket, flip):
    def body(db_ref, cb_ref, flip_ref, out_ref):
        cb = cb_ref[...]
        anti = flip_ref[...]
        sums = []
        for h in range(B_HEADS):
            x = db_ref[h]
            hi = x.astype(BF16)
            rest = x - hi.astype(F32)
            mid = rest.astype(BF16)
            low = (rest - mid.astype(F32)).astype(BF16)
            reversed_x = _mm(hi, anti) + _mm(mid, anti) + _mm(low, anti)
            rolled = pltpu.roll(reversed_x, 0, 1, stride=1, stride_axis=0)
            sums.append(jnp.sum(rolled, axis=0, keepdims=True))
        per_dist = jnp.concatenate(sums, axis=0)
        lane = lax.broadcasted_iota(jnp.int32, (B_HEADS, LANES), 1)
        out = jnp.zeros((B_HEADS, LANES), F32)
        for b in range(REL_BUCKETS):
            s = jnp.sum(jnp.where(cb == b, per_dist, 0.0), axis=-1, keepdims=True)
            out = jnp.where(lane == b, s, out)
        out_ref[...] = out

    vm = pl.BlockSpec(memory_space=pltpu.VMEM)
    return pl.pallas_call(
        body, name="rel_bias_grad",
        out_shape=jax.ShapeDtypeStruct((B_HEADS, LANES), F32),
        in_specs=[vm, vm, vm], out_specs=vm,
    )(dbias, col_bucket, flip)


def _reduce_adamw(slots, w, m, v, name):
    rows, cols = w.shape
    tr = _row_tile(rows)
    c1 = 1.0 / (1.0 - ADAM_B1 ** ADAM_STEP)
    c2 = 1.0 / (1.0 - ADAM_B2 ** ADAM_STEP)

    def body(s_ref, w_ref, m_ref, v_ref, g_ref, d_ref, nm_ref, nv_ref):
        g = s_ref[0].astype(F32)
        for i in range(1, N_DEV):
            g = g + s_ref[i].astype(F32)
        nm = ADAM_B1 * m_ref[...] + (1.0 - ADAM_B1) * g
        nv = ADAM_B2 * v_ref[...] + (1.0 - ADAM_B2) * (g * g)
        g_ref[...] = g
        nm_ref[...] = nm
        nv_ref[...] = nv
        d_ref[...] = -ADAM_LR * ((nm * c1) / (jnp.sqrt(nv * c2) + ADAM_EPS) + ADAM_WD * w_ref[...])

    blocks = [_nbytes((N_DEV, tr, cols), slots.dtype), 7 * _nbytes((tr, cols), F32)]
    tile = pl.BlockSpec((tr, cols), lambda i: (i, 0))
    return pl.pallas_call(
        body, name=name, grid=(rows // tr,),
        in_specs=[pl.BlockSpec((N_DEV, tr, cols), lambda i: (0, i, 0)), tile, tile, tile],
        out_specs=[tile] * 4,
        out_shape=[pltpu.HBM((rows, cols), F32)] * 4,
        compiler_params=pltpu.CompilerParams(dimension_semantics=("parallel",),
                                             vmem_limit_bytes=_vmem_limit(blocks)),
    )(_hbm(slots), _hbm(w), _hbm(m), _hbm(v))


_SMALL = ("pre_norm_g","post_norm_g", "ln_v_g", "ln_v_b", "b_spatial", "sinks", "rel_bias")


def _pack_small(parts):
    slabs = []
    for name in _SMALL:
        flat = parts[name].astype(F32).reshape(-1)
        pad = (-flat.shape[0]) % (8 * LANES)
        slabs.append(jnp.pad(flat, (0, pad)).reshape(-1, LANES))
    return jnp.concatenate(slabs, axis=0)


def _unpack_small(slab, shapes):
    out, row = {}, 0
    for name in _SMALL:
        size = int(np.prod(shapes[name]))
        rows = -(-size // (8 * LANES)) * 8
        out[name] = slab[row:row + rows].reshape(-1)[:size].reshape(shapes[name])
        row += rows
    return out


def kernel(x, pre_norm_g, w_in, ln_v_g, ln_v_b, w_spatial, b_spatial, sinks, rel_bias, w_out, post_norm_g, loss_target, m_pre_norm_g, m_w_in, m_ln_v_g, m_ln_v_b, m_w_spatial, m_b_spatial, m_sinks, m_rel_bias, m_w_out, m_post_norm_g, v_pre_norm_g, v_w_in, v_ln_v_g, v_ln_v_b, v_w_spatial, v_b_spatial, v_sinks, v_rel_bias, v_w_out, v_post_norm_g):
    b_loc, seq, _ = x.shape
    tokens = b_loc * seq
    depth = w_in.shape[0]
    in_shard = w_in.shape[2]
    out_shard = w_out.shape[1]
    assert in_shard * N_DEV == IN_WIDTH and out_shard * N_DEV == D_MODEL and seq % BLOCK == 0

    me = _slot(lax.axis_index("x"), lax.axis_index("y"), lax.axis_index("c"))
    w_in16, w_out16 = w_in.astype(BF16), w_out.astype(BF16)

    def gather_start(layer, not_before):
        shards, _ = lax.optimization_barrier(((w_in16[layer], w_out16[layer]), not_before))
        zones = [lax.empty((N_DEV, D_MODEL, in_shard), BF16), lax.empty((N_DEV, out_shard, D_MODEL), BF16)]
        sems, sends, zones, token = _exchange_start(list(shards), zones, [0, 0], f"weights_send_{layer}")
        return (sems, sends, [0, 0]), zones, token

    def full_weights(gin, gout):
        return gin.transpose(1, 0, 2).reshape(D_MODEL, IN_WIDTH), gout.reshape(D_MODEL, D_MODEL)

    def gather_wait(layer, started, zones, after):
        gin, gout = _exchange_wait([started], zones, after, f"weights_wait_{layer}")
        gin = lax.dynamic_update_index_in_dim(gin, w_in16[layer], me, 0)
        gout = lax.dynamic_update_index_in_dim(gout, w_out16[layer], me, 0)
        return full_weights(gin, gout)

    causal = jnp.tril(jnp.ones((CHUNK, CHUNK), dtype=bool))
    wm = jnp.where(causal, w_spatial, 0.0).astype(BF16)
    wmt = _hbm(jnp.swapaxes(wm, -1, -2))
    wm = _hbm(wm)
    bsb = _hbm(jnp.repeat(jnp.swapaxes(b_spatial, -1, -2), HEAD_DIM, axis=-1))
    pre_g3 = _hbm(pre_norm_g.reshape(depth, 1, D_MODEL))
    post_g3 = _hbm(post_norm_g.reshape(depth, 1, D_MODEL))
    ln_g3 = _hbm(ln_v_g.reshape(depth, 1, A_WIDTH))
    ln_b3 = _hbm(ln_v_b.reshape(depth, 1, A_WIDTH))
    bucket = jnp.asarray(_bucket_table())
    bias = _bias_table(rel_bias, bucket)

    xs, saved, weights = [x.reshape(tokens, D_MODEL)], [], []
    pending = None
    for layer in range(depth):
        if layer == 0:
            w, wo = full_weights(*_all_gather([w_in16[0], w_out16[0]], "weights_gather_0"))
        else:
            w, wo = gather_wait(layer, pending[0], pending[1], xs[-1])
        pre_g_fwd = pre_g3
        if layer + 1 < depth:
            pending = gather_start(layer + 1, xs[-1])
            pre_g_fwd = _hbm(pre_norm_g.reshape(depth, 1, D_MODEL) + pending[2][0, 0])
        h, pa, pb = _inproj_fwd(xs[-1], pre_g_fwd, w, layer)
        ya = _gmlp_fwd(pa, ln_g3, ln_b3, wm, bsb, layer)
        yb = _attn_fwd(pb, bias, sinks, b_loc, layer)
        if layer + 1 < depth:
            y, x_next = _outproj_fwd(ya, yb, wo, xs[-1], post_g3, layer)
            xs.append(x_next)
        else:
            y, g, loss_part = _outproj_fwd(ya, yb, wo, xs[-1], post_g3, layer,
                                           target=loss_target.reshape(tokens, D_MODEL))
        saved.append((h, pa, pb, ya, yb, y))
        weights.append((w, wo))
    loss = lax.psum(loss_part[0, 0], ("x", "y", "c"))

    grads = {name: [None] * depth for name in ("pre_norm_g","post_norm_g", "ln_v_g", "ln_v_b",
                                               "b_spatial", "sinks", "dbias")}
    zone_in = lax.empty((N_DEV, depth * D_MODEL, in_shard), BF16)
    zone_out = lax.empty((N_DEV, depth * out_shard, D_MODEL), BF16)
    zone_ws = lax.empty((N_DEV, depth * A_GROUPS * CHUNK, CHUNK), F32)
    started_in, started_out, own_in, own_out = [], [], [None] * depth, [None] * depth
    started_ws, own_ws = [], [None] * depth
    dbias = jnp.zeros((B_HEADS, BLOCK, BLOCK), F32)
    for layer in reversed(range(depth)):
        h, pa, pb, ya, yb, y = saved[layer]
        w, wo = weights[layer]
        dya, dyb, dwo, dpost = _outproj_bwd(g, y, ya, yb, wo, post_g3, layer)
        send_out = dwo.reshape(N_DEV, out_shard, D_MODEL)
        own_out[layer] = lax.dynamic_index_in_dim(send_out, me, 0, keepdims=False)
        sems, sends, (zone_out,), token = _exchange_start([send_out], [zone_out], [layer * out_shard],
                                                          f"grads_send_out_{layer}")
        started_out.append((sems, sends, [layer * out_shard]))
        ln_g_bwd = _hbm(ln_v_g.reshape(depth, 1, A_WIDTH) + token[0, 0])
        da, dws, dbs, dlg, dlb = _gmlp_bwd(pa, dya, ln_g_bwd, ln_b3, wm, wmt, bsb, layer)
        own_ws[layer] = dws.reshape(A_GROUPS * CHUNK, CHUNK)
        sems, sends, (zone_ws,), token_ws = _exchange_start([own_ws[layer]], [zone_ws], [layer * A_GROUPS * CHUNK],
                                                            f"grads_send_ws_{layer}")
        started_ws.append((sems, sends, [layer * A_GROUPS * CHUNK]))
        dqz, dkv_shifted, dbias, dsink = _attn_bwd(pb, dyb, bias, dbias, sinks, b_loc, layer)
        dkv = dkv_shifted[BLOCK:BLOCK + tokens]
        dw = _inproj_bwd_dw(h, da, dqz, dkv, layer)
        send_in = dw.reshape(D_MODEL, N_DEV, in_shard).transpose(1, 0, 2)
        own_in[layer] = lax.dynamic_index_in_dim(send_in, me, 0, keepdims=False)
        sems, sends, (zone_in,), token = _exchange_start([send_in], [zone_in], [layer * D_MODEL],
                                                         f"grads_send_in_{layer}")
        started_in.append((sems, sends, [layer * D_MODEL]))
        pre_g_bwd = _hbm(pre_norm_g.reshape(depth, 1, D_MODEL) + (token[0, 0] + token_ws[0, 0]))
        g, dpre = _inproj_bwd_dx(da, dqz, dkv, w, xs[layer], g, pre_g_bwd, layer)
        grads["b_spatial"][layer] = dbs[:, :A_GROUPS].T
        grads["ln_v_g"][layer] = dlg[0]
        grads["ln_v_b"][layer] = dlb[0]
        grads["sinks"][layer] = dsink[:, 0]
        grads["pre_norm_g"][layer] = dpre[0]
        grads["post_norm_g"][layer] = dpost[0]
    grad_x = g.reshape(x.shape)
    col_bucket = jnp.asarray(np.broadcast_to(_bucket_table()[0:1, ::-1], (B_HEADS, BLOCK)))
    flip = jnp.asarray(np.eye(BLOCK, dtype=np.float32)[::-1], dtype=BF16)
    drel = _rel_bias_grad(dbias, col_bucket, flip)[:, :REL_BUCKETS].T

    (recv_out,) = _exchange_wait(started_out, [zone_out], g, "grads_wait_out")
    recv_out = lax.dynamic_update_index_in_dim(recv_out, jnp.concatenate(own_out, axis=0), me, 0)
    res_out = _reduce_adamw(recv_out, w_out.reshape(-1, D_MODEL), m_w_out.reshape(-1, D_MODEL),
                            v_w_out.reshape(-1, D_MODEL), "adamw_w_out")
    res_out = [r.reshape(w_out.shape) for r in res_out]

    (recv_ws,) = _exchange_wait(started_ws, [zone_ws], g, "grads_wait_ws")
    recv_ws = lax.dynamic_update_index_in_dim(recv_ws, jnp.concatenate(own_ws, axis=0), me, 0)
    res_ws = _reduce_adamw(recv_ws, w_spatial.reshape(-1, CHUNK), m_w_spatial.reshape(-1, CHUNK),
                           v_w_spatial.reshape(-1, CHUNK), "adamw_w_spatial")
    res_ws = [r.reshape(w_spatial.shape) for r in res_ws]
    small_w = dict(pre_norm_g=pre_norm_g, post_norm_g=post_norm_g, ln_v_g=ln_v_g,
                   ln_v_b=ln_v_b, b_spatial=b_spatial, sinks=sinks, rel_bias=rel_bias)
    small_m = dict(pre_norm_g=m_pre_norm_g, post_norm_g=m_post_norm_g, ln_v_g=m_ln_v_g,
                   ln_v_b=m_ln_v_b, b_spatial=m_b_spatial, sinks=m_sinks, rel_bias=m_rel_bias)
    small_v = dict(pre_norm_g=v_pre_norm_g, post_norm_g=v_post_norm_g, ln_v_g=v_ln_v_g,
                   ln_v_b=v_ln_v_b, b_spatial=v_b_spatial, sinks=v_sinks, rel_bias=v_rel_bias)
    small_g = {name: jnp.stack(grads[name]) for name in _SMALL if name != "rel_bias"}
    small_g["rel_bias"] = drel
    shapes = {name: small_w[name].shape for name in _SMALL}
    (slots,) = _all_gather([_pack_small(small_g)], "small_grads_all_gather")
    res_small = _reduce_adamw(slots, _pack_small(small_w), _pack_small(small_m), _pack_small(small_v), "adamw_small")

    (recv_in,) = _exchange_wait(started_in, [zone_in], res_small[0], "grads_wait_in")
    recv_in = lax.dynamic_update_index_in_dim(recv_in, jnp.concatenate(own_in, axis=0), me, 0)
    res_in = _reduce_adamw(recv_in, w_in.reshape(-1, in_shard), m_w_in.reshape(-1, in_shard),
                           v_w_in.reshape(-1, in_shard), "adamw_w_in")
    res_in = [r.reshape(w_in.shape) for r in res_in]
    res_small = [_unpack_small(r, shapes) for r in res_small]

    order = ("pre_norm_g", "w_in", "ln_v_g", "ln_v_b", "w_spatial", "b_spatial", "sinks", "rel_bias", "w_out",
             "post_norm_g")
    outs = [loss, grad_x]
    for kind in range(4):
        for name in order:
            if name == "w_in":
                outs.append(res_in[kind])
            elif name == "w_out":
                outs.append(res_out[kind])
            elif name == "w_spatial":
                outs.append(res_ws[kind])
            else:
                outs.append(res_small[kind][name])
    return tuple(outs)
```

```python
import math

import numpy as np
import jax
import jax.numpy as jnp
from jax import lax
from jax.experimental import pallas as pl
from jax.experimental.pallas import tpu as pltpu

F32 = jnp.float32
BF16 = jnp.bfloat16

D_MODEL = 1024
A_WIDTH = 512
A_GROUPS = 8
CHUNK = 128
B_HEADS = 8
HEAD_DIM = 64
B_WIDTH = 512
KV_WIDTH = 128
BLOCK = 128
REL_BUCKETS = 32
REL_MAX_DIST = 128
NORM_EPS = 1e-6
PA_WIDTH = 3 * A_WIDTH
PB_WIDTH = 2 * B_WIDTH + 2 * KV_WIDTH
IN_WIDTH = PA_WIDTH + PB_WIDTH
K_OFF, V_OFF, BZ_OFF = B_WIDTH, B_WIDTH + KV_WIDTH, B_WIDTH + 2 * KV_WIDTH
SCALE = HEAD_DIM ** -0.5
NEG = -1e30
N_DEV = 8
LANES = 128

ADAM_LR = 0.001
ADAM_B1 = 0.9
ADAM_B2 = 0.999
ADAM_EPS = 1e-08
ADAM_WD = 0.01
ADAM_STEP = 10

V7X_VMEM_BYTES = 64 * 1024 * 1024
VMEM_TEMP_BYTES = 12 * 1024 * 1024
MESH = pl.DeviceIdType.MESH


def _vmem_limit(block_bytes, scratch_bytes=0):
    need = 2 * sum(block_bytes) + scratch_bytes + VMEM_TEMP_BYTES
    return int(min(need, V7X_VMEM_BYTES - 8 * 1024 * 1024))


def _nbytes(shape, dtype):
    return int(np.prod(shape)) * jnp.dtype(dtype).itemsize


def _token_tile(tokens):
    tile = min(512, tokens // 2)
    assert tokens % tile == 0 and tile % CHUNK == 0, tokens
    return tile


def _row_tile(rows, cap=512):
    best = 8
    for t in range(8, cap + 1, 8):
        if rows % t == 0:
            best = t
    assert rows % best == 0, rows
    return best


def _mm(a, b):
    return lax.dot_general(a, b, (((1,), (0,)), ((), ())), preferred_element_type=F32)


def _mm_nt(a, b):
    return lax.dot_general(a, b, (((1,), (1,)), ((), ())), preferred_element_type=F32)


def _mm_tn(a, b):
    return lax.dot_general(a, b, (((0,), (0,)), ((), ())), preferred_element_type=F32)


_GELU_C = math.sqrt(2.0 / math.pi)


_GELU_A = _GELU_C * 0.044715


def _gelu_parts(x):
    x2 = x * x
    t = jnp.tanh(x * (_GELU_C + _GELU_A * x2))
    return x2, t, 0.5 + 0.5 * t


def _gelu(x):
    return x * _gelu_parts(x)[2]


def _gelu_and_grad(x):
    x2, t, half_plus = _gelu_parts(x)
    grad = half_plus + (0.5 * x) * (1.0 - t * t) * (_GELU_C + (3.0 * _GELU_A) * x2)
    return x * half_plus, grad


def _sigmoid(x):
    return 0.5 + 0.5 * jnp.tanh(0.5 * x)


def _bucket_table():
    q = np.arange(BLOCK)[:, None]
    k = np.arange(BLOCK)[None, :]
    dist = np.where(k <= q, q - k, q + BLOCK - k)
    max_exact = REL_BUCKETS // 2
    safe = np.maximum(dist, 1).astype(np.float32)
    large = max_exact + (np.log(safe / np.float32(max_exact)) / np.float32(math.log(REL_MAX_DIST / max_exact))
                         * np.float32(REL_BUCKETS - max_exact)).astype(np.int32)
    large = np.minimum(large, REL_BUCKETS - 1)
    assert dist.min() >= 0 and dist.max() < BLOCK
    return np.where(dist < max_exact, dist, large).astype(np.int32)


def _hbm(x):
    return pltpu.with_memory_space_constraint(x, pltpu.HBM)


def _slot(px, py, pc):
    return 4 * px + 2 * py + pc


def _all_gather(arrs, name):
    n = len(arrs)

    def body(*refs):
        ins, outs = refs[:n], refs[n:2 * n]
        send_sems, recv_sems, local_sems = refs[2 * n:]
        x, y, c = lax.axis_index("x"), lax.axis_index("y"), lax.axis_index("c")
        me, sibling = (x, y, c), (x, y, 1 - c)
        chips = [(1 - x, y), (x, 1 - y), (1 - x, 1 - y)]

        def copy(a, k, block, to, src=None):
            dst = outs[a].at[_slot(*block)]
            return pltpu.make_async_remote_copy(
                src_ref=dst if src is None else src, dst_ref=dst,
                send_sem=send_sems.at[7 * a + k], recv_sem=recv_sems.at[7 * a + k],
                device_id=to, device_id_type=MESH)

        mine = [pltpu.make_async_copy(ins[a], outs[a].at[_slot(*me)], local_sems.at[a]) for a in range(n)]
        for cp in mine:
            cp.start()
        first = []
        for a in range(n):
            first.append(copy(a, 0, me, sibling, src=ins[a]))
            first += [copy(a, 1 + j, me, (*chip, c), src=ins[a]) for j, chip in enumerate(chips)]
        for cp in first:
            cp.start()
        passed = []
        for j, chip in enumerate(chips):
            for a in range(n):
                copy(a, 1 + j, (*chip, c), me).wait_recv()
                fwd = copy(a, 4 + j, (*chip, c), sibling)
                fwd.start()
                passed.append(fwd)
        for a in range(n):
            copy(a, 0, sibling, me).wait_recv()
            for j, chip in enumerate(chips):
                copy(a, 4 + j, (*chip, 1 - c), me).wait_recv()
        for cp in first + passed:
            cp.wait_send()
        for cp in mine:
            cp.wait()

    any_spec = pl.BlockSpec(memory_space=pl.ANY)
    return pl.pallas_call(
        body, name=name,
        out_shape=[jax.ShapeDtypeStruct((N_DEV,) + a.shape, a.dtype) for a in arrs],
        in_specs=[any_spec] * n, out_specs=[any_spec] * n,
        scratch_shapes=[pltpu.SemaphoreType.DMA((7 * n,)), pltpu.SemaphoreType.DMA((7 * n,)),
                        pltpu.SemaphoreType.DMA((n,))],
    )(*arrs)


_HBM_SPEC = pl.BlockSpec(memory_space=pltpu.HBM)
_SEM_SPEC = pl.BlockSpec(memory_space=pltpu.SEMAPHORE)
_DATAFLOW = pltpu.SideEffectType.DATAFLOW_SIDE_EFFECTING


def _peers():
    x, y, c = lax.axis_index("x"), lax.axis_index("y"), lax.axis_index("c")
    peers = []
    for k in range(1, N_DEV):
        fx, fy, fc = (k >> 2) & 1, (k >> 1) & 1, k & 1
        peers.append((1 - x if fx else x, 1 - y if fy else y, 1 - c if fc else c))
    return (x, y, c), peers


def _exchange_copy(send_ref, land_ref, row_off, src_slot, dst_slot, sems, idx, peer):
    rows = send_ref.shape[-2]
    src = send_ref.at[src_slot] if len(send_ref.shape) == 3 else send_ref
    return pltpu.make_async_remote_copy(
        src_ref=src, dst_ref=land_ref.at[dst_slot, pl.ds(row_off, rows), :],
        send_sem=sems[0].at[idx], recv_sem=sems[1].at[idx], device_id=peer, device_id_type=MESH)


def _exchange_start(sends, lands, row_offs, name):
    n = len(sends)

    def body(*refs):
        ins, zones = refs[:n], refs[n:2 * n]
        sems = refs[2 * n:2 * n + 2]
        token = refs[-1]
        me, peers = _peers()
        for a in range(n):
            for k, peer in enumerate(peers):
                _exchange_copy(ins[a], zones[a], row_offs[a], _slot(*peer), _slot(*me), sems, 7 * a + k, peer).start()
        token[...] = jnp.zeros_like(token)

    arrays = [_hbm(a) for a in list(sends) + list(lands)]
    out = pl.pallas_call(
        body, name=name,
        out_shape=(pltpu.SemaphoreType.DMA((7 * n,)), pltpu.SemaphoreType.DMA((7 * n,)),
                   *[pltpu.HBM(a.shape, a.dtype) for a in arrays], jax.ShapeDtypeStruct((8, LANES), F32)),
        in_specs=[_HBM_SPEC] * (2 * n),
        out_specs=(_SEM_SPEC, _SEM_SPEC, *[_HBM_SPEC] * (2 * n), pl.BlockSpec(memory_space=pltpu.VMEM)),
        input_output_aliases={i: 2 + i for i in range(2 * n)},
        compiler_params=pltpu.CompilerParams(has_side_effects=_DATAFLOW),
    )(*arrays)
    return (out[0], out[1]), list(out[2:2 + n]), list(out[2 + n:2 + 2 * n]), out[-1]


def _exchange_wait(started, lands, after, name):
    n = len(lands)
    flat_sends = [s for _, sends, _ in started for s in sends]
    flat_sems = [s for sems, _, _ in started for s in sems]
    ns = len(flat_sends)

    def body(*refs):
        ins, zones = refs[:ns], refs[ns:ns + n]
        sem_refs = refs[ns + n:ns + n + len(flat_sems)]
        me, peers = _peers()
        pos = 0
        for call, (_, sends, row_offs) in enumerate(started):
            sems = sem_refs[2 * call:2 * call + 2]
            for a in range(len(sends)):
                for k, peer in enumerate(peers):
                    cp = _exchange_copy(ins[pos + a], zones[a], row_offs[a], _slot(*peer), _slot(*peer), sems,
                                        7 * a + k, peer)
                    cp.wait_send()
                    cp.wait_recv()
            pos += len(sends)

    arrays = list(flat_sends) + list(lands)
    out = pl.pallas_call(
        body, name=name,
        out_shape=tuple(pltpu.HBM(a.shape, a.dtype) for a in arrays),
        in_specs=[_HBM_SPEC] * len(arrays) + [_SEM_SPEC] * len(flat_sems) + [pl.BlockSpec(memory_space=pl.ANY)],
        out_specs=tuple([_HBM_SPEC] * len(arrays)),
        input_output_aliases={i: i for i in range(len(arrays))},
        compiler_params=pltpu.CompilerParams(has_side_effects=_DATAFLOW),
    )(*arrays, *flat_sems, after)
    return list(out[ns:])


def _bias_table(rel_bias, bucket):
    def body(rel_ref, bucket_ref, out_ref):
        bk = bucket_ref[...]
        for h in range(B_HEADS):
            def pick(b, acc, h=h):
                return jnp.where(bk == b, rel_ref[b, h], acc)
            out_ref[h] = lax.fori_loop(0, REL_BUCKETS, pick, jnp.zeros((BLOCK, BLOCK), F32))

    return pl.pallas_call(
        body, name="bias_table",
        out_shape=jax.ShapeDtypeStruct((B_HEADS, BLOCK, BLOCK), F32),
        in_specs=[pl.BlockSpec(memory_space=pltpu.SMEM), pl.BlockSpec(memory_space=pltpu.VMEM)],
        out_specs=pl.BlockSpec(memory_space=pltpu.VMEM),
    )(rel_bias, bucket)


def _inproj_fwd(x2, pre_g3, w, layer):
    tokens = x2.shape[0]
    tm = _token_tile(tokens)

    def body(x_ref, g_ref, w_ref, h_ref, pa_ref, pb_ref):
        x = x_ref[...]
        r = lax.rsqrt(jnp.mean(x * x, axis=-1, keepdims=True) + NORM_EPS)
        h = (x * r * g_ref[...]).astype(BF16)
        h_ref[...] = h
        pa_ref[...] = _mm_nt(h, w_ref[0:PA_WIDTH, :]).astype(BF16)
        pb_ref[...] = _mm_nt(h, w_ref[PA_WIDTH:IN_WIDTH, :]).astype(BF16)

    blocks = [_nbytes((tm, D_MODEL), F32), _nbytes((D_MODEL, IN_WIDTH), BF16),
              _nbytes((tm, D_MODEL), BF16), _nbytes((tm, PA_WIDTH), BF16), _nbytes((tm, PB_WIDTH), BF16)]
    return pl.pallas_call(
        body, name=f"inproj_fwd_{layer}", grid=(tokens // tm,),
        in_specs=[pl.BlockSpec((tm, D_MODEL), lambda i: (i, 0)),
                  pl.BlockSpec((None, 1, D_MODEL), lambda i: (layer, 0, 0)),
                  pl.BlockSpec((IN_WIDTH, D_MODEL), lambda i: (0, 0))],
        out_specs=[pl.BlockSpec((tm, D_MODEL), lambda i: (i, 0)),
                   pl.BlockSpec((tm, PA_WIDTH), lambda i: (i, 0)),
                   pl.BlockSpec((tm, PB_WIDTH), lambda i: (i, 0))],
        out_shape=[pltpu.HBM((tokens,D_MODEL), BF16),
                   pltpu.HBM((tokens,PA_WIDTH), BF16),
                   pltpu.HBM((tokens,PB_WIDTH), BF16)],
        compiler_params=pltpu.CompilerParams(dimension_semantics=("parallel",),
                                             vmem_limit_bytes=_vmem_limit(blocks)),
    )(_hbm(x2), pre_g3, _hbm(w))


def _gmlp_forward_chunk(pa, ln_g, ln_b, wm_ref, bsb):
    pu = pa[:, 0:A_WIDTH].astype(F32)
    pv = pa[:, A_WIDTH:2 * A_WIDTH].astype(F32)
    pz = pa[:, 2 * A_WIDTH:3 * A_WIDTH].astype(F32)
    u, gu = _gelu_and_grad(pu)
    vv, gv = _gelu_and_grad(pv)
    mu = jnp.mean(vv, axis=-1, keepdims=True)
    xc = vv - mu
    rstd = lax.rsqrt(jnp.mean(xc * xc, axis=-1, keepdims=True) + NORM_EPS)
    vhat = xc * rstd
    vnb = (vhat * ln_g + ln_b).astype(BF16)
    low = lax.broadcasted_iota(jnp.int32, (CHUNK, LANES), 1) < HEAD_DIM
    parts = []
    for p in range(A_GROUPS // 2):
        vp = vnb[:, LANES * p:LANES * (p + 1)]
        parts.append(jnp.where(low, _mm(wm_ref[2 * p], vp), _mm(wm_ref[2 * p + 1], vp)))
    mixed = jnp.concatenate(parts, axis=1) + bsb
    sg = _sigmoid(pz)
    return gu, gv, pz, u, vhat, rstd, vnb, mixed, sg


def _gmlp_fwd(pa, ln_g3, ln_b3, wm, bsb, layer):
    tokens = pa.shape[0]
    tc = _token_tile(tokens)

    def body(pa_ref, lg_ref, lb_ref, wm_ref, bsb_ref, ya_ref):
        def chunk(ci, carry):
            rows = pl.ds(pl.multiple_of(ci * CHUNK, CHUNK), CHUNK)
            _, _, pz, u, _, _, _, mixed, sg = _gmlp_forward_chunk(
                pa_ref[rows, :], lg_ref[...], lb_ref[...], wm_ref, bsb_ref[...])
            ya_ref[rows, :] = (u * mixed * (pz * sg)).astype(BF16)
            return carry
        lax.fori_loop(0, tc // CHUNK, chunk, 0)

    blocks = [_nbytes((tc, PA_WIDTH), BF16), _nbytes((A_GROUPS, CHUNK, CHUNK), BF16),
              _nbytes((CHUNK, A_WIDTH), F32), _nbytes((tc, A_WIDTH), BF16)]
    return pl.pallas_call(
        body, name=f"gmlp_fwd_{layer}", grid=(tokens // tc,),
        in_specs=[pl.BlockSpec((tc, PA_WIDTH), lambda i: (i, 0)),
                  pl.BlockSpec((None, 1, A_WIDTH), lambda i: (layer, 0, 0)),
                  pl.BlockSpec((None, 1, A_WIDTH), lambda i: (layer, 0, 0)),
                  pl.BlockSpec((None, A_GROUPS, CHUNK, CHUNK), lambda i: (layer, 0, 0, 0)),
                  pl.BlockSpec((None, CHUNK, A_WIDTH), lambda i: (layer, 0, 0))],
        out_specs=pl.BlockSpec((tc, A_WIDTH), lambda i: (i, 0)),
        out_shape=pltpu.HBM((tokens,A_WIDTH), BF16),
        compiler_params=pltpu.CompilerParams(dimension_semantics=("parallel",),
                                             vmem_limit_bytes=_vmem_limit(blocks)),
    )(_hbm(pa), ln_g3, ln_b3, wm, bsb)


def _kv_variants(kv):
    t = kv.astype(F32)
    rolled = pltpu.roll(t, HEAD_DIM, 1)
    low = lax.broadcasted_iota(jnp.int32, t.shape, 1) < HEAD_DIM
    zero = jnp.zeros_like(t)
    head0 = (jnp.where(low, t, zero).astype(BF16), jnp.where(low, zero, rolled).astype(BF16))
    head1 = (jnp.where(low, rolled, zero).astype(BF16), jnp.where(low, zero, t).astype(BF16))
    return head0, head1


def _band_masks():
    row = lax.broadcasted_iota(jnp.int32, (BLOCK, BLOCK), 0)
    col = lax.broadcasted_iota(jnp.int32, (BLOCK, BLOCK), 1)
    return col <= row


def _wrap(full, tri):
    return jnp.where(tri, full[:, BLOCK:2 * BLOCK], full[:, 0:BLOCK])


def _attn_probs(sf, bias_h, sink, tri, kill):
    s = _wrap(sf, tri) * SCALE + bias_h
    s = jnp.where(kill, NEG, s)
    m = jnp.maximum(jnp.max(s, axis=-1, keepdims=True), sink)
    e = jnp.exp(s - m)
    es = jnp.exp(sink - m)
    inv = 1.0 / (jnp.sum(e, axis=-1, keepdims=True) + es)
    return e * inv, es * inv


def _unwrap16(p, tri):
    zero = jnp.zeros_like(p)
    return jnp.concatenate([jnp.where(tri, zero, p), jnp.where(tri, p, zero)], axis=1).astype(BF16)


def _fill_kv(kv_ref, prev_ref, cur_ref):
    kv_ref[0:BLOCK, :] = prev_ref[...]
    kv_ref[BLOCK:, :] = cur_ref[:, K_OFF:K_OFF + 2 * KV_WIDTH]


def _attn_fwd(pb, bias, sinks, b_loc, layer):
    tokens = pb.shape[0]
    nb = tokens // b_loc // BLOCK
    tq = _token_tile(tokens)
    per_tile = tq // BLOCK

    def body(sink_ref, cur_ref, prev_ref, bias_ref, yb_ref, kv_ref):
        t = pl.program_id(0)
        _fill_kv(kv_ref, prev_ref, cur_ref)
        tri = _band_masks()

        def block(i, carry):
            start = pl.multiple_of(i * BLOCK, BLOCK)
            rows = pl.ds(start, BLOCK)
            first = lax.rem(t * per_tile + i, nb) == 0
            kill = jnp.logical_and(first, jnp.logical_not(tri))
            kv = kv_ref[pl.ds(start, 2 * BLOCK), :]
            k_ops = _kv_variants(kv[:, 0:KV_WIDTH])
            v_ops = _kv_variants(kv[:, KV_WIDTH:2 * KV_WIDTH])
            p16 = {}
            for kvh in range(2):
                pairs = (2 * kvh, 2 * kvh + 1)
                qs = jnp.concatenate([cur_ref[rows, LANES * p:LANES * (p + 1)] for p in pairs], axis=0)
                for j in range(2):
                    sf = _mm_nt(qs, k_ops[kvh][j])
                    for r, p in enumerate(pairs):
                        hd = 2 * p + j
                        probs, _ = _attn_probs(sf[BLOCK * r:BLOCK * (r + 1)], bias_ref[hd], sink_ref[layer, hd],
                                               tri, kill)
                        p16[hd] = _unwrap16(probs, tri)
            for kvh in range(2):
                pairs = (2 * kvh, 2 * kvh + 1)
                out = jnp.zeros((2 * BLOCK, LANES), F32)
                for j in range(2):
                    out = out + _mm(jnp.concatenate([p16[2 * p + j] for p in pairs], axis=0), v_ops[kvh][j])
                for r, p in enumerate(pairs):
                    bz = cur_ref[rows, BZ_OFF + LANES * p:BZ_OFF + LANES * (p + 1)].astype(F32)
                    yb_ref[rows, LANES * p:LANES * (p + 1)] = (
                        out[BLOCK * r:BLOCK * (r + 1)] * (bz * _sigmoid(bz))).astype(BF16)
            return carry

        lax.fori_loop(0, per_tile, block, 0)

    blocks = [_nbytes((tq, PB_WIDTH), BF16), _nbytes((BLOCK, 2 * KV_WIDTH), BF16),
              _nbytes((B_HEADS, BLOCK, BLOCK), F32), _nbytes((tq, B_WIDTH), BF16)]
    scratch = _nbytes((tq + BLOCK, 2 * KV_WIDTH), BF16)
    return pl.pallas_call(
        body, name=f"attn_fwd_{layer}", grid=(tokens // tq,),
        in_specs=[pl.BlockSpec(memory_space=pltpu.SMEM),
                  pl.BlockSpec((tq, PB_WIDTH), lambda t: (t, 0)),
                  pl.BlockSpec((BLOCK, 2 * KV_WIDTH),
                               lambda t: (jnp.maximum(t * per_tile - 1, 0), K_OFF // (2 * KV_WIDTH))),
                  pl.BlockSpec((B_HEADS, BLOCK, BLOCK), lambda t: (0, 0, 0))],
        out_specs=pl.BlockSpec((tq, B_WIDTH), lambda t: (t, 0)),
        out_shape=pltpu.HBM((tokens, B_WIDTH), BF16),
        scratch_shapes=[pltpu.VMEM((tq + BLOCK, 2 * KV_WIDTH), BF16)],
        compiler_params=pltpu.CompilerParams(dimension_semantics=("parallel",),
                                             vmem_limit_bytes=_vmem_limit(blocks, scratch)),
    )(sinks, _hbm(pb), _hbm(pb), _hbm(bias))


def _outproj_fwd(ya, yb, wo, x2, post_g3, layer, target=None):
    tokens = x2.shape[0]
    tm = _token_tile(tokens)
    steps = tokens // tm
    with_loss = target is not None

    def body(*refs):
        ya_ref, yb_ref, woa_ref, wob_ref, x_ref, g_ref = refs[:6]
        y = _mm(ya_ref[...], woa_ref[...]) + _mm(yb_ref[...], wob_ref[...])
        r = lax.rsqrt(jnp.mean(y * y, axis=-1, keepdims=True) + NORM_EPS)
        x_next = x_ref[...] + y * r * g_ref[...]
        if not with_loss:
            y_ref, xn_ref = refs[6:]
            y_ref[...] = y.astype(BF16)
            xn_ref[...] = x_next
            return
        t_ref, y_ref, grad_ref, loss_ref, acc_ref = refs[6:]
        i = pl.program_id(0)

        @pl.when(i == 0)
        def _():
            acc_ref[...] = jnp.zeros_like(acc_ref)

        y_ref[...] = y.astype(BF16)
        err = x_next - t_ref[...]
        grad_ref[...] = err * (1.0 / D_MODEL)
        acc_ref[...] += jnp.sum(err * err, axis=0, keepdims=True)

        @pl.when(i == steps - 1)
        def _():
            total = jnp.sum(acc_ref[...], axis=-1, keepdims=True) * (0.5 / D_MODEL)
            loss_ref[...] = jnp.broadcast_to(total, loss_ref.shape)

    half = D_MODEL // 2
    tile = pl.BlockSpec((tm, D_MODEL), lambda i: (i, 0))
    blocks = [2 * _nbytes((tm, half), BF16), 2 * _nbytes((half, D_MODEL), BF16), 3 * _nbytes((tm, D_MODEL), F32),
              _nbytes((tm, D_MODEL), BF16)]
    in_specs = [pl.BlockSpec((tm, half), lambda i: (i, 0)),
                pl.BlockSpec((tm, half), lambda i: (i, 0)),
                pl.BlockSpec((half, D_MODEL), lambda i: (0, 0)),
                pl.BlockSpec((half, D_MODEL), lambda i: (1, 0)),
                tile,
                pl.BlockSpec((None, 1, D_MODEL), lambda i: (layer, 0, 0))]
    out_specs = [tile, tile]
    out_shape = [pltpu.HBM((tokens, D_MODEL), BF16), pltpu.HBM((tokens, D_MODEL), F32)]
    operands = [_hbm(ya), _hbm(yb), _hbm(wo), _hbm(wo), _hbm(x2), post_g3]
    if with_loss:
        in_specs.append(tile)
        out_specs.append(pl.BlockSpec((1, LANES), lambda i: (0, 0)))
        out_shape.append(pltpu.HBM((1, LANES), F32))
        operands.append(_hbm(target))
    return pl.pallas_call(
        body, name=f"outproj_fwd_{layer}", grid=(steps,),
        in_specs=in_specs, out_specs=out_specs, out_shape=out_shape,
        scratch_shapes=[pltpu.VMEM((1, D_MODEL), F32)] if with_loss else [],
        compiler_params=pltpu.CompilerParams(dimension_semantics=("arbitrary" if with_loss else "parallel",),
                                             vmem_limit_bytes=_vmem_limit(blocks)),
    )(*operands)


def _outproj_bwd(g, y, ya, yb, wo, post_g3, layer):
    tokens = g.shape[0]
    tm = _token_tile(tokens)
    half = D_MODEL // 2
    steps = tokens // tm

    def body(g_ref, y_ref, ya_ref, yb_ref, woa_ref, wob_ref, pg_ref, dya_ref, dyb_ref, dwo16_ref, dpg_ref, dwo_ref):
        @pl.when(pl.program_id(0) == 0)
        def _():
            dwo_ref[...] = jnp.zeros_like(dwo_ref)
            dpg_ref[...] = jnp.zeros_like(dpg_ref)

        gv = g_ref[...]
        yf = y_ref[...].astype(F32)
        r = lax.rsqrt(jnp.mean(yf * yf, axis=-1, keepdims=True) + NORM_EPS)
        yhat = yf * r
        gg = gv * pg_ref[...]
        dy = r * (gg - yhat * jnp.mean(gg * yhat, axis=-1, keepdims=True))
        dpg_ref[...] += jnp.sum(gv * yhat, axis=0, keepdims=True)
        dyb16 = dy.astype(BF16)
        dya_ref[...] = _mm_nt(dyb16, woa_ref[...]).astype(BF16)
        dyb_ref[...] = _mm_nt(dyb16, wob_ref[...]).astype(BF16)
        dwo_ref[0:half, :] += _mm_tn(ya_ref[...], dyb16)
        dwo_ref[half:D_MODEL, :] += _mm_tn(yb_ref[...], dyb16)

        @pl.when(pl.program_id(0) == steps - 1)
        def _():
            dwo16_ref[...] = dwo_ref[...].astype(BF16)

    blocks = [_nbytes((tm, D_MODEL), F32), _nbytes((tm, D_MODEL), BF16), 4 * _nbytes((tm, half), BF16),
              2 * _nbytes((half, D_MODEL), BF16), _nbytes((D_MODEL, D_MODEL), BF16)]
    return pl.pallas_call(
        body, name=f"outproj_bwd_{layer}", grid=(tokens // tm,),
        in_specs=[pl.BlockSpec((tm, D_MODEL), lambda i: (i, 0)),
                  pl.BlockSpec((tm, D_MODEL), lambda i: (i, 0)),
                  pl.BlockSpec((tm, half), lambda i: (i, 0)),
                  pl.BlockSpec((tm, half), lambda i: (i, 0)),
                  pl.BlockSpec((half, D_MODEL), lambda i: (0, 0)),
                  pl.BlockSpec((half, D_MODEL), lambda i: (1, 0)),
                  pl.BlockSpec((None, 1, D_MODEL), lambda i: (layer, 0, 0))],
        out_specs=[pl.BlockSpec((tm, half), lambda i: (i, 0)),
                   pl.BlockSpec((tm, half), lambda i: (i, 0)),
                   pl.BlockSpec((D_MODEL, D_MODEL), lambda i: (0, 0)),
                   pl.BlockSpec((1, D_MODEL), lambda i: (0, 0))],
        out_shape=[pltpu.HBM((tokens,half), BF16),
                   pltpu.HBM((tokens,half), BF16),
                   pltpu.HBM((D_MODEL,D_MODEL), BF16),
                   pltpu.HBM((1,D_MODEL), F32)],
        scratch_shapes=[pltpu.VMEM((D_MODEL, D_MODEL), F32)],
        compiler_params=pltpu.CompilerParams(dimension_semantics=("arbitrary",),
                                             vmem_limit_bytes=_vmem_limit(blocks, _nbytes((D_MODEL, D_MODEL), F32))),
    )(_hbm(g), _hbm(y), _hbm(ya), _hbm(yb), _hbm(wo), _hbm(wo), post_g3)


def _gmlp_bwd(pa, dya, ln_g3, ln_b3, wm, wmt, bsb, layer):
    tokens = pa.shape[0]
    tc = _token_tile(tokens)
    steps = tokens // tc

    def body(pa_ref, dya_ref, lg_ref, lb_ref, wm_ref, wmt_ref, bsb_ref,
             da_ref, dws_ref, dbs_ref, dlg_ref, dlb_ref, dbsb_ref):
        i = pl.program_id(0)

        @pl.when(i == 0)
        def _():
            dws_ref[...] = jnp.zeros_like(dws_ref)
            dlg_ref[...] = jnp.zeros_like(dlg_ref)
            dlb_ref[...] = jnp.zeros_like(dlb_ref)
            dbsb_ref[...] = jnp.zeros_like(dbsb_ref)

        ln_g = lg_ref[...]
        low = lax.broadcasted_iota(jnp.int32, (CHUNK, LANES), 1) < HEAD_DIM

        def chunk(ci, carry):
            rows = pl.ds(pl.multiple_of(ci * CHUNK, CHUNK), CHUNK)
            gu, gv, pz, u, vhat, rstd, vnb, mixed, sg = _gmlp_forward_chunk(
                pa_ref[rows, :], ln_g, lb_ref[...], wm_ref, bsb_ref[...])
            dy = dya_ref[rows, :].astype(F32)
            sz = pz * sg
            dy_sz = dy * sz
            du = dy_sz * mixed
            dmixed = dy_sz * u
            dz = dy * (u * mixed) * (sg + sz * (1.0 - sg))
            dbsb_ref[...] += dmixed
            dmb = dmixed.astype(BF16)
            zero = jnp.zeros((CHUNK, LANES), BF16)
            parts = []
            for p in range(A_GROUPS // 2):
                dmp = dmb[:, LANES * p:LANES * (p + 1)]
                vp = vnb[:, LANES * p:LANES * (p + 1)]
                parts.append(jnp.where(low, _mm(wmt_ref[2 * p], dmp), _mm(wmt_ref[2 * p + 1], dmp)))
                dws_ref[2 * p] += _mm_nt(jnp.where(low, dmp, zero), vp)
                dws_ref[2 * p + 1] += _mm_nt(jnp.where(low, zero, dmp), vp)
            dvn = jnp.concatenate(parts, axis=1)
            dlg_ref[...] += jnp.sum(dvn * vhat, axis=0, keepdims=True)
            dlb_ref[...] += jnp.sum(dvn, axis=0, keepdims=True)
            dvh = dvn * ln_g
            dvv = rstd * (dvh - jnp.mean(dvh, axis=-1, keepdims=True)
                          - vhat * jnp.mean(dvh * vhat, axis=-1, keepdims=True))
            da_ref[rows, 0:A_WIDTH] = (du * gu).astype(BF16)
            da_ref[rows, A_WIDTH:2 * A_WIDTH] = (dvv * gv).astype(BF16)
            da_ref[rows, 2 * A_WIDTH:3 * A_WIDTH] = dz.astype(BF16)
            return carry

        lax.fori_loop(0, tc // CHUNK, chunk, 0)

        @pl.when(i == steps - 1)
        def _():
            causal = (lax.broadcasted_iota(jnp.int32, (CHUNK, CHUNK), 0)
                      >= lax.broadcasted_iota(jnp.int32, (CHUNK, CHUNK), 1))
            for h in range(A_GROUPS):
                dws_ref[h] = jnp.where(causal, dws_ref[h], 0.0)
            acc = dbsb_ref[...]
            lane_full = lax.broadcasted_iota(jnp.int32, (CHUNK, A_WIDTH), 1)
            lane_out = lax.broadcasted_iota(jnp.int32, (CHUNK, LANES), 1)
            out = jnp.zeros((CHUNK, LANES), F32)
            for h in range(A_GROUPS):
                in_group = jnp.logical_and(lane_full >= HEAD_DIM * h, lane_full < HEAD_DIM * (h + 1))
                s = jnp.sum(jnp.where(in_group, acc, 0.0), axis=-1, keepdims=True)
                out = jnp.where(lane_out == h, s, out)
            dbs_ref[...] = out

    blocks = [_nbytes((tc, PA_WIDTH), BF16), _nbytes((tc, A_WIDTH), BF16), 2 * _nbytes((A_GROUPS, CHUNK, CHUNK), BF16),
              _nbytes((CHUNK, A_WIDTH), F32), _nbytes((tc, PA_WIDTH), BF16), _nbytes((A_GROUPS, CHUNK, CHUNK), F32),
              _nbytes((CHUNK, LANES), F32)]
    return pl.pallas_call(
        body, name=f"gmlp_bwd_{layer}", grid=(steps,),
        in_specs=[pl.BlockSpec((tc, PA_WIDTH), lambda i: (i, 0)),
                  pl.BlockSpec((tc, A_WIDTH), lambda i: (i, 0)),
                  pl.BlockSpec((None, 1, A_WIDTH), lambda i: (layer, 0, 0)),
                  pl.BlockSpec((None, 1, A_WIDTH), lambda i: (layer, 0, 0)),
                  pl.BlockSpec((None, A_GROUPS, CHUNK, CHUNK), lambda i: (layer, 0, 0, 0)),
                  pl.BlockSpec((None, A_GROUPS, CHUNK, CHUNK), lambda i: (layer, 0, 0, 0)),
                  pl.BlockSpec((None, CHUNK, A_WIDTH), lambda i: (layer, 0, 0))],
        out_specs=[pl.BlockSpec((tc, PA_WIDTH), lambda i: (i, 0)),
                   pl.BlockSpec((A_GROUPS, CHUNK, CHUNK), lambda i: (0, 0, 0)),
                   pl.BlockSpec((CHUNK, LANES), lambda i: (0, 0)),
                   pl.BlockSpec((1, A_WIDTH), lambda i: (0, 0)),
                   pl.BlockSpec((1, A_WIDTH), lambda i: (0, 0))],
        out_shape=[pltpu.HBM((tokens,PA_WIDTH), BF16),
                   pltpu.HBM((A_GROUPS, CHUNK, CHUNK), F32),
                   pltpu.HBM((CHUNK, LANES), F32),
                   pltpu.HBM((1,A_WIDTH), F32),
                   pltpu.HBM((1,A_WIDTH), F32)],
        scratch_shapes=[pltpu.VMEM((CHUNK, A_WIDTH), F32)],
        compiler_params=pltpu.CompilerParams(dimension_semantics=("arbitrary",),
                                             vmem_limit_bytes=_vmem_limit(blocks, _nbytes((CHUNK, A_WIDTH), F32))),
    )(_hbm(pa), _hbm(dya), ln_g3, ln_b3, wm, wmt, bsb)


def _attn_bwd(pb, dyb, bias, dbias_in, sinks, b_loc, layer):
    tokens = pb.shape[0]
    nb = tokens // b_loc // BLOCK
    qz_width = 2 * B_WIDTH
    tq = _token_tile(tokens)
    per_tile = tq // BLOCK
    nt = tokens // tq

    def body(sink_ref, cur_ref, prev_ref, dyb_ref, bias_ref, dbias_in_ref, dqz_ref, dkv_ref, dbias_ref, dsink_ref,
             kv_ref, acc_ref):
        t = pl.program_id(0)

        @pl.when(t == 0)
        def _():
            dbias_ref[...] = dbias_in_ref[...]
            dsink_ref[...] = jnp.zeros_like(dsink_ref)
            acc_ref[0:BLOCK, :] = jnp.zeros((BLOCK, 2 * KV_WIDTH), F32)

        @pl.when(t < nt)
        def _():
            acc_ref[BLOCK:, :] = jnp.zeros((tq, 2 * KV_WIDTH), F32)
            _fill_kv(kv_ref, prev_ref, cur_ref)
            tri = _band_masks()
            low = lax.broadcasted_iota(jnp.int32, (BLOCK, LANES), 1) < HEAD_DIM
            low_kv = lax.broadcasted_iota(jnp.int32, (2 * BLOCK, LANES), 1) < HEAD_DIM

            def block(i, carry):
                start = pl.multiple_of(i * BLOCK, BLOCK)
                rows = pl.ds(start, BLOCK)
                first = lax.rem(t * per_tile + i, nb) == 0
                kill = jnp.logical_and(first, jnp.logical_not(tri))
                kv = kv_ref[pl.ds(start, 2 * BLOCK), :]
                k_ops = _kv_variants(kv[:, 0:KV_WIDTH])
                v_ops = _kv_variants(kv[:, KV_WIDTH:2 * KV_WIDTH])
                probs, sink_p, p16, qs, dos, delta, ds16 = {}, {}, {}, {}, {}, {}, {}
                for kvh in range(2):
                    pairs = (2 * kvh, 2 * kvh + 1)
                    qs[kvh] = jnp.concatenate([cur_ref[rows, LANES * p:LANES * (p + 1)] for p in pairs], axis=0)
                    for j in range(2):
                        sf = _mm_nt(qs[kvh], k_ops[kvh][j])
                        for r, p in enumerate(pairs):
                            hd = 2 * p + j
                            probs[hd], sink_p[hd] = _attn_probs(sf[BLOCK * r:BLOCK * (r + 1)], bias_ref[hd],
                                                                sink_ref[layer, hd], tri, kill)
                            p16[hd] = _unwrap16(probs[hd], tri)
                for kvh in range(2):
                    pairs = (2 * kvh, 2 * kvh + 1)
                    out = jnp.zeros((2 * BLOCK, LANES), F32)
                    for j in range(2):
                        out = out + _mm(jnp.concatenate([p16[2 * p + j] for p in pairs], axis=0), v_ops[kvh][j])
                    d_outs = []
                    for r, p in enumerate(pairs):
                        bz = cur_ref[rows, BZ_OFF + LANES * p:BZ_OFF + LANES * (p + 1)].astype(F32)
                        sg = _sigmoid(bz)
                        dyp = dyb_ref[rows, LANES * p:LANES * (p + 1)].astype(F32)
                        out_p = out[BLOCK * r:BLOCK * (r + 1)]
                        d_out = dyp * (bz * sg)
                        dqz_ref[rows, B_WIDTH + LANES * p:B_WIDTH + LANES * (p + 1)] = (
                            dyp * out_p * (sg * (1.0 + bz * (1.0 - sg)))).astype(BF16)
                        dod = d_out * out_p
                        delta[2 * p] = jnp.sum(jnp.where(low, dod, 0.0), axis=-1, keepdims=True)
                        delta[2 * p + 1] = jnp.sum(jnp.where(low, 0.0, dod), axis=-1, keepdims=True)
                        d_outs.append(d_out.astype(BF16))
                    dos[kvh] = jnp.concatenate(d_outs, axis=0)
                for kvh in range(2):
                    pairs = (2 * kvh, 2 * kvh + 1)
                    for j in range(2):
                        dpf = _mm_nt(dos[kvh], v_ops[kvh][j])
                        for r, p in enumerate(pairs):
                            hd = 2 * p + j
                            ds = probs[hd] * (_wrap(dpf[BLOCK * r:BLOCK * (r + 1)], tri) - delta[hd])
                            dsink_ref[hd:hd + 1, :] += jnp.sum(-sink_p[hd] * delta[hd], axis=0, keepdims=True)
                            dbias_ref[hd] += ds
                            ds16[hd] = _unwrap16(ds, tri)
                dk_acc = [[None, None], [None, None]]
                dv_acc = [[None, None], [None, None]]
                for kvh in range(2):
                    pairs = (2 * kvh, 2 * kvh + 1)
                    dq = jnp.zeros((2 * BLOCK, LANES), F32)
                    for j in range(2):
                        dss = jnp.concatenate([ds16[2 * p + j] for p in pairs], axis=0)
                        pss = jnp.concatenate([p16[2 * p + j] for p in pairs], axis=0)
                        dq = dq + _mm(dss, k_ops[kvh][j])
                        dk_acc[kvh][j] = _mm_tn(dss, qs[kvh])
                        dv_acc[kvh][j] = _mm_tn(pss, dos[kvh])
                    for r, p in enumerate(pairs):
                        dqz_ref[rows, LANES * p:LANES * (p + 1)] = (dq[BLOCK * r:BLOCK * (r + 1)] * SCALE).astype(BF16)

                def fold(acc):
                    return jnp.where(low_kv,
                                     acc[0][0] + pltpu.roll(acc[0][1], HEAD_DIM, 1),
                                     pltpu.roll(acc[1][0], HEAD_DIM, 1) + acc[1][1])

                acc_ref[pl.ds(start, 2 * BLOCK), :] += jnp.concatenate(
                    [fold(dk_acc) * SCALE, fold(dv_acc)], axis=1)
                return carry

            lax.fori_loop(0, per_tile, block, 0)
            dkv_ref[...] = acc_ref[0:tq, :].astype(BF16)
            acc_ref[0:BLOCK, :] = acc_ref[tq:tq + BLOCK, :]

        @pl.when(t == nt)
        def _():
            dkv_ref[0:BLOCK, :] = acc_ref[0:BLOCK, :].astype(BF16)
            dkv_ref[BLOCK:, :] = jnp.zeros((tq - BLOCK, 2 * KV_WIDTH), BF16)

    def cur_map(t):
        return (jnp.minimum(t, nt - 1), 0)

    def prev_map(t):
        return (jnp.maximum(jnp.minimum(t, nt - 1) * per_tile - 1, 0), K_OFF // (2 * KV_WIDTH))

    blocks = [_nbytes((tq, PB_WIDTH), BF16), _nbytes((BLOCK, 2 * KV_WIDTH), BF16), _nbytes((tq, B_WIDTH), BF16),
              3 * _nbytes((B_HEADS, BLOCK, BLOCK), F32), _nbytes((tq, qz_width), BF16),
              _nbytes((tq, 2 * KV_WIDTH), BF16), _nbytes((B_HEADS, LANES), F32)]
    scratch = _nbytes((tq + BLOCK, 2 * KV_WIDTH), BF16) + _nbytes((tq + BLOCK, 2 * KV_WIDTH), F32)
    return pl.pallas_call(
        body, name=f"attn_bwd_{layer}", grid=(nt + 1,),
        in_specs=[pl.BlockSpec(memory_space=pltpu.SMEM),
                  pl.BlockSpec((tq, PB_WIDTH), cur_map),
                  pl.BlockSpec((BLOCK, 2 * KV_WIDTH), prev_map),
                  pl.BlockSpec((tq, B_WIDTH), cur_map),
                  pl.BlockSpec((B_HEADS, BLOCK, BLOCK), lambda t: (0, 0, 0)),
                  pl.BlockSpec((B_HEADS, BLOCK, BLOCK), lambda t: (0, 0, 0))],
        out_specs=[pl.BlockSpec((tq, qz_width), cur_map),
                   pl.BlockSpec((tq, 2 * KV_WIDTH), lambda t: (t, 0)),
                   pl.BlockSpec((B_HEADS, BLOCK, BLOCK), lambda t: (0, 0, 0)),
                   pl.BlockSpec((B_HEADS, LANES), lambda t: (0, 0))],
        out_shape=[pltpu.HBM((tokens, qz_width), BF16),
                   pltpu.HBM((tokens + tq, 2 * KV_WIDTH), BF16),
                   pltpu.HBM((B_HEADS, BLOCK, BLOCK), F32),
                   pltpu.HBM((B_HEADS, LANES), F32)],
        scratch_shapes=[pltpu.VMEM((tq + BLOCK, 2 * KV_WIDTH), BF16),
                        pltpu.VMEM((tq + BLOCK, 2 * KV_WIDTH), F32)],
        compiler_params=pltpu.CompilerParams(dimension_semantics=("arbitrary",),
                                             vmem_limit_bytes=_vmem_limit(blocks, scratch)),
    )(sinks, _hbm(pb), _hbm(pb), _hbm(dyb), _hbm(bias), _hbm(dbias_in))


def _inproj_bwd_dx(da, dqz, dkv, w, x2, g, pre_g3, layer):
    tokens = x2.shape[0]
    tm = _token_tile(tokens)

    def body(da_ref, dqz_ref, dkv_ref, w_ref, x_ref, g_ref, pg_ref, gn_ref, dpg_ref):
        @pl.when(pl.program_id(0) == 0)
        def _():
            dpg_ref[...] = jnp.zeros_like(dpg_ref)

        dh = _mm(da_ref[...], w_ref[0:PA_WIDTH, :])
        dh += _mm(dqz_ref[:, 0:B_WIDTH], w_ref[PA_WIDTH:PA_WIDTH + B_WIDTH, :])
        dh += _mm(dkv_ref[...], w_ref[PA_WIDTH + K_OFF:PA_WIDTH + BZ_OFF, :])
        dh += _mm(dqz_ref[:, B_WIDTH:2 * B_WIDTH], w_ref[PA_WIDTH + BZ_OFF:IN_WIDTH, :])
        x = x_ref[...]
        r = lax.rsqrt(jnp.mean(x * x, axis=-1, keepdims=True) + NORM_EPS)
        xhat = x * r
        dhg = dh * pg_ref[...]
        dpg_ref[...] += jnp.sum(dh * xhat, axis=0, keepdims=True)
        gn_ref[...] = g_ref[...] + r * (dhg - xhat * jnp.mean(dhg * xhat, axis=-1, keepdims=True))

    blocks = [_nbytes((tm, PA_WIDTH), BF16), _nbytes((tm, 2 * B_WIDTH), BF16), _nbytes((tm, 2 * KV_WIDTH), BF16),
              _nbytes((D_MODEL, IN_WIDTH), BF16), 3 * _nbytes((tm, D_MODEL), F32)]
    return pl.pallas_call(
        body, name=f"inproj_bwd_dx_{layer}", grid=(tokens // tm,),
        in_specs=[pl.BlockSpec((tm, PA_WIDTH), lambda i: (i, 0)),
                  pl.BlockSpec((tm, 2 * B_WIDTH), lambda i: (i, 0)),
                  pl.BlockSpec((tm, 2 * KV_WIDTH), lambda i: (i, 0)),
                  pl.BlockSpec((IN_WIDTH, D_MODEL), lambda i: (0, 0)),
                  pl.BlockSpec((tm, D_MODEL), lambda i: (i, 0)),
                  pl.BlockSpec((tm, D_MODEL), lambda i: (i, 0)),
                  pl.BlockSpec((None, 1, D_MODEL), lambda i: (layer, 0, 0))],
        out_specs=[pl.BlockSpec((tm, D_MODEL), lambda i: (i, 0)),
                   pl.BlockSpec((1, D_MODEL), lambda i: (0, 0))],
        out_shape=[pltpu.HBM((tokens,D_MODEL), F32),
                   pltpu.HBM((1,D_MODEL), F32)],
        compiler_params=pltpu.CompilerParams(dimension_semantics=("arbitrary",),
                                             vmem_limit_bytes=_vmem_limit(blocks)),
    )(_hbm(da), _hbm(dqz), _hbm(dkv), _hbm(w), _hbm(x2), _hbm(g), pre_g3)


def _inproj_bwd_dw(h, da, dqz, dkv, layer):
    tokens = h.shape[0]
    tm = _token_tile(tokens)
    steps = tokens // tm

    def body(h_ref, da_ref, dqz_ref, dkv_ref, dw_ref, acc_ref):
        i = pl.program_id(0)

        @pl.when(i == 0)
        def _():
            acc_ref[...] = jnp.zeros_like(acc_ref)

        hv = h_ref[...]
        acc_ref[:, 0:PA_WIDTH] += _mm_tn(hv, da_ref[...])
        acc_ref[:, PA_WIDTH:PA_WIDTH + B_WIDTH] += _mm_tn(hv, dqz_ref[:, 0:B_WIDTH])
        acc_ref[:, PA_WIDTH + K_OFF:PA_WIDTH + BZ_OFF] += _mm_tn(hv, dkv_ref[...])
        acc_ref[:, PA_WIDTH + BZ_OFF:IN_WIDTH] += _mm_tn(hv, dqz_ref[:, B_WIDTH:2 * B_WIDTH])

        @pl.when(i == steps - 1)
        def _():
            for c in range(IN_WIDTH // LANES):
                cols = slice(LANES * c, LANES * (c + 1))
                dw_ref[cols, :] = acc_ref[:, cols].T.astype(BF16)

    blocks = [_nbytes((tm, D_MODEL), BF16), _nbytes((tm, PA_WIDTH), BF16), _nbytes((tm, 2 * B_WIDTH), BF16),
              _nbytes((tm, 2 * KV_WIDTH), BF16), _nbytes((D_MODEL, IN_WIDTH), BF16)]
    return pl.pallas_call(
        body, name=f"inproj_bwd_dw_{layer}", grid=(steps,),
        in_specs=[pl.BlockSpec((tm, D_MODEL), lambda i: (i, 0)),
                  pl.BlockSpec((tm, PA_WIDTH), lambda i: (i, 0)),
                  pl.BlockSpec((tm, 2 * B_WIDTH), lambda i: (i, 0)),
                  pl.BlockSpec((tm, 2 * KV_WIDTH), lambda i: (i, 0))],
        out_specs=pl.BlockSpec((IN_WIDTH, D_MODEL), lambda i: (0, 0)),
        out_shape=pltpu.HBM((IN_WIDTH, D_MODEL), BF16),
        scratch_shapes=[pltpu.VMEM((D_MODEL, IN_WIDTH), F32)],
        compiler_params=pltpu.CompilerParams(dimension_semantics=("arbitrary",),
                                             vmem_limit_bytes=_vmem_limit(blocks, _nbytes((D_MODEL, IN_WIDTH), F32))),
    )(_hbm(h), _hbm(da), _hbm(dqz), _hbm(dkv))


def _rel_bias_grad(dbias, col_bucket, flip):
    def body(db_ref, cb_ref, flip_ref, out_ref):
        cb = cb_ref[...]
        anti = flip_ref[...]
        sums = []
        for h in range(B_HEADS):
            x = db_ref[h]
            hi = x.astype(BF16)
            rest = x - hi.astype(F32)
            mid = rest.astype(BF16)
            low = (rest - mid.astype(F32)).astype(BF16)
            reversed_x = _mm(hi, anti) + _mm(mid, anti) + _mm(low, anti)
            rolled = pltpu.roll(reversed_x, 0, 1, stride=1, stride_axis=0)
            sums.append(jnp.sum(rolled, axis=0, keepdims=True))
        per_dist = jnp.concatenate(sums, axis=0)
        lane = lax.broadcasted_iota(jnp.int32, (B_HEADS, LANES), 1)
        out = jnp.zeros((B_HEADS, LANES), F32)
        for b in range(REL_BUCKETS):
            s = jnp.sum(jnp.where(cb == b, per_dist, 0.0), axis=-1, keepdims=True)
            out = jnp.where(lane == b, s, out)
        out_ref[...] = out

    vm = pl.BlockSpec(memory_space=pltpu.VMEM)
    return pl.pallas_call(
        body, name="rel_bias_grad",
        out_shape=jax.ShapeDtypeStruct((B_HEADS, LANES), F32),
        in_specs=[vm, vm, vm], out_specs=vm,
    )(dbias, col_bucket, flip)


def _reduce_adamw(slots, w, m, v, name):
    rows, cols = w.shape
    tr = _row_tile(rows)
    c1 = 1.0 / (1.0 - ADAM_B1 ** ADAM_STEP)
    c2 = 1.0 / (1.0 - ADAM_B2 ** ADAM_STEP)

    def body(s_ref, w_ref, m_ref, v_ref, g_ref, d_ref, nm_ref, nv_ref):
        g = s_ref[0].astype(F32)
        for i in range(1, N_DEV):
            g = g + s_ref[i].astype(F32)
        nm = ADAM_B1 * m_ref[...] + (1.0 - ADAM_B1) * g
        nv = ADAM_B2 * v_ref[...] + (1.0 - ADAM_B2) * (g * g)
        g_ref[...] = g
        nm_ref[...] = nm
        nv_ref[...] = nv
        d_ref[...] = -ADAM_LR * ((nm * c1) / (jnp.sqrt(nv * c2) + ADAM_EPS) + ADAM_WD * w_ref[...])

    blocks = [_nbytes((N_DEV, tr, cols), slots.dtype), 7 * _nbytes((tr, cols), F32)]
    tile = pl.BlockSpec((tr, cols), lambda i: (i, 0))
    return pl.pallas_call(
        body, name=name, grid=(rows // tr,),
        in_specs=[pl.BlockSpec((N_DEV, tr, cols), lambda i: (0, i, 0)), tile, tile, tile],
        out_specs=[tile] * 4,
        out_shape=[pltpu.HBM((rows, cols), F32)] * 4,
        compiler_params=pltpu.CompilerParams(dimension_semantics=("parallel",),
                                             vmem_limit_bytes=_vmem_limit(blocks)),
    )(_hbm(slots), _hbm(w), _hbm(m), _hbm(v))


_SMALL = ("pre_norm_g","post_norm_g", "ln_v_g", "ln_v_b", "b_spatial", "sinks", "rel_bias")


def _pack_small(parts):
    slabs = []
    for name in _SMALL:
        flat = parts[name].astype(F32).reshape(-1)
        pad = (-flat.shape[0]) % (8 * LANES)
        slabs.append(jnp.pad(flat, (0, pad)).reshape(-1, LANES))
    return jnp.concatenate(slabs, axis=0)


def _unpack_small(slab, shapes):
    out, row = {}, 0
    for name in _SMALL:
        size = int(np.prod(shapes[name]))
        rows = -(-size // (8 * LANES)) * 8
        out[name] = slab[row:row + rows].reshape(-1)[:size].reshape(shapes[name])
        row += rows
    return out


def kernel(x, pre_norm_g, w_in, ln_v_g, ln_v_b, w_spatial, b_spatial, sinks, rel_bias, w_out, post_norm_g, loss_target, m_pre_norm_g, m_w_in, m_ln_v_g, m_ln_v_b, m_w_spatial, m_b_spatial, m_sinks, m_rel_bias, m_w_out, m_post_norm_g, v_pre_norm_g, v_w_in, v_ln_v_g, v_ln_v_b, v_w_spatial, v_b_spatial, v_sinks, v_rel_bias, v_w_out, v_post_norm_g):
    b_loc, seq, _ = x.shape
    tokens = b_loc * seq
    depth = w_in.shape[0]
    in_shard = w_in.shape[2]
    out_shard = w_out.shape[1]
    assert in_shard * N_DEV == IN_WIDTH and out_shard * N_DEV == D_MODEL and seq % BLOCK == 0

    me = _slot(lax.axis_index("x"), lax.axis_index("y"), lax.axis_index("c"))
    w_in_t, m_w_in_t, v_w_in_t = (jnp.swapaxes(a, 1, 2) for a in (w_in, m_w_in, v_w_in))
    w_in16, w_out16 = w_in_t.astype(BF16), w_out.astype(BF16)

    def gather_start(layer, not_before):
        shards, _ = lax.optimization_barrier(((w_in16[layer], w_out16[layer]), not_before))
        zones = [lax.empty((N_DEV, in_shard, D_MODEL), BF16), lax.empty((N_DEV, out_shard, D_MODEL), BF16)]
        sems, sends, zones, token = _exchange_start(list(shards), zones, [0, 0], f"weights_send_{layer}")
        return (sems, sends, [0, 0]), zones, token

    def full_weights(gin, gout):
        return gin.reshape(IN_WIDTH, D_MODEL), gout.reshape(D_MODEL, D_MODEL)

    def gather_wait(layer, started, zones, after):
        gin, gout = _exchange_wait([started], zones, after, f"weights_wait_{layer}")
        gin = lax.dynamic_update_index_in_dim(gin, w_in16[layer], me, 0)
        gout = lax.dynamic_update_index_in_dim(gout, w_out16[layer], me, 0)
        return full_weights(gin, gout)

    causal = jnp.tril(jnp.ones((CHUNK, CHUNK), dtype=bool))
    wm = jnp.where(causal, w_spatial, 0.0).astype(BF16)
    wmt = _hbm(jnp.swapaxes(wm, -1, -2))
    wm = _hbm(wm)
    bsb = _hbm(jnp.repeat(jnp.swapaxes(b_spatial, -1, -2), HEAD_DIM, axis=-1))
    pre_g3 = _hbm(pre_norm_g.reshape(depth, 1, D_MODEL))
    post_g3 = _hbm(post_norm_g.reshape(depth, 1, D_MODEL))
    ln_g3 = _hbm(ln_v_g.reshape(depth, 1, A_WIDTH))
    ln_b3 = _hbm(ln_v_b.reshape(depth, 1, A_WIDTH))
    bucket = jnp.asarray(_bucket_table())
    bias = _bias_table(rel_bias, bucket)

    xs, saved, weights = [x.reshape(tokens, D_MODEL)], [], []
    pending = None
    for layer in range(depth):
        if layer == 0:
            w, wo = full_weights(*_all_gather([w_in16[0], w_out16[0]], "weights_gather_0"))
        else:
            w, wo = gather_wait(layer, pending[0], pending[1], xs[-1])
        pre_g_fwd = pre_g3
        if layer + 1 < depth:
            pending = gather_start(layer + 1, xs[-1])
            pre_g_fwd = _hbm(pre_norm_g.reshape(depth, 1, D_MODEL) + pending[2][0, 0])
        h, pa, pb = _inproj_fwd(xs[-1], pre_g_fwd, w, layer)
        ya = _gmlp_fwd(pa, ln_g3, ln_b3, wm, bsb, layer)
        yb = _attn_fwd(pb, bias, sinks, b_loc, layer)
        if layer + 1 < depth:
            y, x_next = _outproj_fwd(ya, yb, wo, xs[-1], post_g3, layer)
            xs.append(x_next)
        else:
            y, g, loss_part = _outproj_fwd(ya, yb, wo, xs[-1], post_g3, layer,
                                           target=loss_target.reshape(tokens, D_MODEL))
        saved.append((h, pa, pb, ya, yb, y))
        weights.append((w, wo))
    loss = lax.psum(loss_part[0, 0], ("x", "y", "c"))

    grads = {name: [None] * depth for name in ("pre_norm_g","post_norm_g", "ln_v_g", "ln_v_b",
                                               "b_spatial", "sinks", "dbias")}
    zone_in = lax.empty((N_DEV, depth * in_shard, D_MODEL), BF16)
    zone_out = lax.empty((N_DEV, depth * out_shard, D_MODEL), BF16)
    zone_ws = lax.empty((N_DEV, depth * A_GROUPS * CHUNK, CHUNK), F32)
    started_in, started_out, own_in, own_out = [], [], [None] * depth, [None] * depth
    started_ws, own_ws = [], [None] * depth
    dbias = jnp.zeros((B_HEADS, BLOCK, BLOCK), F32)
    for layer in reversed(range(depth)):
        h, pa, pb, ya, yb, y = saved[layer]
        w, wo = weights[layer]
        dya, dyb, dwo, dpost = _outproj_bwd(g, y, ya, yb, wo, post_g3, layer)
        send_out = dwo.reshape(N_DEV, out_shard, D_MODEL)
        own_out[layer] = lax.dynamic_index_in_dim(send_out, me, 0, keepdims=False)
        sems, sends, (zone_out,), token = _exchange_start([send_out], [zone_out], [layer * out_shard],
                                                          f"grads_send_out_{layer}")
        started_out.append((sems, sends, [layer * out_shard]))
        ln_g_bwd = _hbm(ln_v_g.reshape(depth, 1, A_WIDTH) + token[0, 0])
        da, dws, dbs, dlg, dlb = _gmlp_bwd(pa, dya, ln_g_bwd, ln_b3, wm, wmt, bsb, layer)
        own_ws[layer] = dws.reshape(A_GROUPS * CHUNK, CHUNK)
        sems, sends, (zone_ws,), token_ws = _exchange_start([own_ws[layer]], [zone_ws], [layer * A_GROUPS * CHUNK],
                                                            f"grads_send_ws_{layer}")
        started_ws.append((sems, sends, [layer * A_GROUPS * CHUNK]))
        dqz, dkv_shifted, dbias, dsink = _attn_bwd(pb, dyb, bias, dbias, sinks, b_loc, layer)
        dkv = dkv_shifted[BLOCK:BLOCK + tokens]
        dw = _inproj_bwd_dw(h, da, dqz, dkv, layer)
        send_in = dw.reshape(N_DEV, in_shard, D_MODEL)
        own_in[layer] = lax.dynamic_index_in_dim(send_in, me, 0, keepdims=False)
        sems, sends, (zone_in,), token = _exchange_start([send_in], [zone_in], [layer * in_shard],
                                                         f"grads_send_in_{layer}")
        started_in.append((sems, sends, [layer * in_shard]))
        pre_g_bwd = _hbm(pre_norm_g.reshape(depth, 1, D_MODEL) + (token[0, 0] + token_ws[0, 0]))
        g, dpre = _inproj_bwd_dx(da, dqz, dkv, w, xs[layer], g, pre_g_bwd, layer)
        grads["b_spatial"][layer] = dbs[:, :A_GROUPS].T
        grads["ln_v_g"][layer] = dlg[0]
        grads["ln_v_b"][layer] = dlb[0]
        grads["sinks"][layer] = dsink[:, 0]
        grads["pre_norm_g"][layer] = dpre[0]
        grads["post_norm_g"][layer] = dpost[0]
    grad_x = g.reshape(x.shape)
    col_bucket = jnp.asarray(np.broadcast_to(_bucket_table()[0:1, ::-1], (B_HEADS, BLOCK)))
    flip = jnp.asarray(np.eye(BLOCK, dtype=np.float32)[::-1], dtype=BF16)
    drel = _rel_bias_grad(dbias, col_bucket, flip)[:, :REL_BUCKETS].T

    (recv_out,) = _exchange_wait(started_out, [zone_out], g, "grads_wait_out")
    recv_out = lax.dynamic_update_index_in_dim(recv_out, jnp.concatenate(own_out, axis=0), me, 0)
    res_out = _reduce_adamw(recv_out, w_out.reshape(-1, D_MODEL), m_w_out.reshape(-1, D_MODEL),
                            v_w_out.reshape(-1, D_MODEL), "adamw_w_out")
    res_out = [r.reshape(w_out.shape) for r in res_out]

    (recv_ws,) = _exchange_wait(started_ws, [zone_ws], g, "grads_wait_ws")
    recv_ws = lax.dynamic_update_index_in_dim(recv_ws, jnp.concatenate(own_ws, axis=0), me, 0)
    res_ws = _reduce_adamw(recv_ws, w_spatial.reshape(-1, CHUNK), m_w_spatial.reshape(-1, CHUNK),
                           v_w_spatial.reshape(-1, CHUNK), "adamw_w_spatial")
    res_ws = [r.reshape(w_spatial.shape) for r in res_ws]
    small_w = dict(pre_norm_g=pre_norm_g, post_norm_g=post_norm_g, ln_v_g=ln_v_g,
                   ln_v_b=ln_v_b, b_spatial=b_spatial, sinks=sinks, rel_bias=rel_bias)
    small_m = dict(pre_norm_g=m_pre_norm_g, post_norm_g=m_post_norm_g, ln_v_g=m_ln_v_g,
                   ln_v_b=m_ln_v_b, b_spatial=m_b_spatial, sinks=m_sinks, rel_bias=m_rel_bias)
    small_v = dict(pre_norm_g=v_pre_norm_g, post_norm_g=v_post_norm_g, ln_v_g=v_ln_v_g,
                   ln_v_b=v_ln_v_b, b_spatial=v_b_spatial, sinks=v_sinks, rel_bias=v_rel_bias)
    small_g = {name: jnp.stack(grads[name]) for name in _SMALL if name != "rel_bias"}
    small_g["rel_bias"] = drel
    shapes = {name: small_w[name].shape for name in _SMALL}
    (slots,) = _all_gather([_pack_small(small_g)], "small_grads_all_gather")
    res_small = _reduce_adamw(slots, _pack_small(small_w), _pack_small(small_m), _pack_small(small_v), "adamw_small")

    (recv_in,) = _exchange_wait(started_in, [zone_in], res_small[0], "grads_wait_in")
    recv_in = lax.dynamic_update_index_in_dim(recv_in, jnp.concatenate(own_in, axis=0), me, 0)
    res_in = _reduce_adamw(recv_in, w_in_t.reshape(-1, D_MODEL), m_w_in_t.reshape(-1, D_MODEL),
                           v_w_in_t.reshape(-1, D_MODEL), "adamw_w_in")
    res_in = [jnp.swapaxes(r.reshape(w_in_t.shape), 1, 2) for r in res_in]
    res_small = [_unpack_small(r, shapes) for r in res_small]

    order = ("pre_norm_g", "w_in", "ln_v_g", "ln_v_b", "w_spatial", "b_spatial", "sinks", "rel_bias", "w_out",
             "post_norm_g")
    outs = [loss, grad_x]
    for kind in range(4):
        for name in order:
            if name == "w_in":
                outs.append(res_in[kind])
            elif name == "w_out":
                outs.append(res_out[kind])
            elif name == "w_spatial":
                outs.append(res_ws[kind])
            else:
                outs.append(res_small[kind][name])
    return tuple(outs)
```

```python
import math

import numpy as np
import jax
import jax.numpy as jnp
from jax import lax
from jax.experimental import pallas as pl
from jax.experimental.pallas import tpu as pltpu

F32 = jnp.float32
BF16 = jnp.bfloat16

D_MODEL = 1024
A_WIDTH = 512
A_GROUPS = 8
CHUNK = 128
B_HEADS = 8
HEAD_DIM = 64
B_WIDTH = 512
KV_WIDTH = 128
BLOCK = 128
REL_BUCKETS = 32
REL_MAX_DIST = 128
NORM_EPS = 1e-6
PA_WIDTH = 3 * A_WIDTH
PB_WIDTH = 2 * B_WIDTH + 2 * KV_WIDTH
IN_WIDTH = PA_WIDTH + PB_WIDTH
K_OFF, V_OFF, BZ_OFF = B_WIDTH, B_WIDTH + KV_WIDTH, B_WIDTH + 2 * KV_WIDTH
SCALE = HEAD_DIM ** -0.5
NEG = -1e30
N_DEV = 8
LANES = 128

ADAM_LR = 0.001
ADAM_B1 = 0.9
ADAM_B2 = 0.999
ADAM_EPS = 1e-08
ADAM_WD = 0.01
ADAM_STEP = 10

V7X_VMEM_BYTES = 64 * 1024 * 1024
VMEM_TEMP_BYTES = 12 * 1024 * 1024
MESH = pl.DeviceIdType.MESH


def _vmem_limit(block_bytes, scratch_bytes=0):
    need = 2 * sum(block_bytes) + scratch_bytes + VMEM_TEMP_BYTES
    return int(min(need, V7X_VMEM_BYTES - 8 * 1024 * 1024))


def _nbytes(shape, dtype):
    return int(np.prod(shape)) * jnp.dtype(dtype).itemsize


def _token_tile(tokens):
    tile = min(512, tokens // 2)
    assert tokens % tile == 0 and tile % CHUNK == 0, tokens
    return tile


def _row_tile(rows, cap=512):
    best = 8
    for t in range(8, cap + 1, 8):
        if rows % t == 0:
            best = t
    assert rows % best == 0, rows
    return best


def _mm(a, b):
    return lax.dot_general(a, b, (((1,), (0,)), ((), ())), preferred_element_type=F32)


def _mm_nt(a, b):
    return lax.dot_general(a, b, (((1,), (1,)), ((), ())), preferred_element_type=F32)


def _mm_tn(a, b):
    return lax.dot_general(a, b, (((0,), (0,)), ((), ())), preferred_element_type=F32)


_GELU_C = math.sqrt(2.0 / math.pi)


_GELU_A = _GELU_C * 0.044715


def _gelu_parts(x):
    x2 = x * x
    t = jnp.tanh(x * (_GELU_C + _GELU_A * x2))
    return x2, t, 0.5 + 0.5 * t


def _gelu(x):
    return x * _gelu_parts(x)[2]


def _gelu_and_grad(x):
    x2, t, half_plus = _gelu_parts(x)
    grad = half_plus + (0.5 * x) * (1.0 - t * t) * (_GELU_C + (3.0 * _GELU_A) * x2)
    return x * half_plus, grad


def _sigmoid(x):
    return 0.5 + 0.5 * jnp.tanh(0.5 * x)


def _bucket_table():
    q = np.arange(BLOCK)[:, None]
    k = np.arange(BLOCK)[None, :]
    dist = np.where(k <= q, q - k, q + BLOCK - k)
    max_exact = REL_BUCKETS // 2
    safe = np.maximum(dist, 1).astype(np.float32)
    large = max_exact + (np.log(safe / np.float32(max_exact)) / np.float32(math.log(REL_MAX_DIST / max_exact))
                         * np.float32(REL_BUCKETS - max_exact)).astype(np.int32)
    large = np.minimum(large, REL_BUCKETS - 1)
    assert dist.min() >= 0 and dist.max() < BLOCK
    return np.where(dist < max_exact, dist, large).astype(np.int32)


def _hbm(x):
    return pltpu.with_memory_space_constraint(x, pltpu.HBM)


def _after(value, *tokens):
    value, _ = lax.optimization_barrier((value, tokens))
    return value


def _slot(px, py, pc):
    return 4 * px + 2 * py + pc


def _all_gather(arrs, name):
    n = len(arrs)

    def body(*refs):
        ins, outs = refs[:n], refs[n:2 * n]
        send_sems, recv_sems, local_sems = refs[2 * n:]
        x, y, c = lax.axis_index("x"), lax.axis_index("y"), lax.axis_index("c")
        me, sibling = (x, y, c), (x, y, 1 - c)
        chips = [(1 - x, y), (x, 1 - y), (1 - x, 1 - y)]

        def copy(a, k, block, to, src=None):
            dst = outs[a].at[_slot(*block)]
            return pltpu.make_async_remote_copy(
                src_ref=dst if src is None else src, dst_ref=dst,
                send_sem=send_sems.at[7 * a + k], recv_sem=recv_sems.at[7 * a + k],
                device_id=to, device_id_type=MESH)

        mine = [pltpu.make_async_copy(ins[a], outs[a].at[_slot(*me)], local_sems.at[a]) for a in range(n)]
        for cp in mine:
            cp.start()
        first = []
        for a in range(n):
            first.append(copy(a, 0, me, sibling, src=ins[a]))
            first += [copy(a, 1 + j, me, (*chip, c), src=ins[a]) for j, chip in enumerate(chips)]
        for cp in first:
            cp.start()
        passed = []
        for j, chip in enumerate(chips):
            for a in range(n):
                copy(a, 1 + j, (*chip, c), me).wait_recv()
                fwd = copy(a, 4 + j, (*chip, c), sibling)
                fwd.start()
                passed.append(fwd)
        for a in range(n):
            copy(a, 0, sibling, me).wait_recv()
            for j, chip in enumerate(chips):
                copy(a, 4 + j, (*chip, 1 - c), me).wait_recv()
        for cp in first + passed:
            cp.wait_send()
        for cp in mine:
            cp.wait()

    any_spec = pl.BlockSpec(memory_space=pl.ANY)
    return pl.pallas_call(
        body, name=name,
        out_shape=[jax.ShapeDtypeStruct((N_DEV,) + a.shape, a.dtype) for a in arrs],
        in_specs=[any_spec] * n, out_specs=[any_spec] * n,
        scratch_shapes=[pltpu.SemaphoreType.DMA((7 * n,)), pltpu.SemaphoreType.DMA((7 * n,)),
                        pltpu.SemaphoreType.DMA((n,))],
    )(*arrs)


_HBM_SPEC = pl.BlockSpec(memory_space=pltpu.HBM)
_SEM_SPEC = pl.BlockSpec(memory_space=pltpu.SEMAPHORE)
_DATAFLOW = pltpu.SideEffectType.DATAFLOW_SIDE_EFFECTING


def _peers():
    x, y, c = lax.axis_index("x"), lax.axis_index("y"), lax.axis_index("c")
    peers = []
    for k in range(1, N_DEV):
        fx, fy, fc = (k >> 2) & 1, (k >> 1) & 1, k & 1
        peers.append((1 - x if fx else x, 1 - y if fy else y, 1 - c if fc else c))
    return (x, y, c), peers


def _exchange_copy(send_ref, land_ref, row_off, src_slot, dst_slot, sems, idx, peer):
    rows = send_ref.shape[-2]
    src = send_ref.at[src_slot] if len(send_ref.shape) == 3 else send_ref
    return pltpu.make_async_remote_copy(
        src_ref=src, dst_ref=land_ref.at[dst_slot, pl.ds(row_off, rows), :],
        send_sem=sems[0].at[idx], recv_sem=sems[1].at[idx], device_id=peer, device_id_type=MESH)


def _exchange_start(sends, lands, row_offs, name):
    n = len(sends)

    def body(*refs):
        ins, zones = refs[:n], refs[n:2 * n]
        sems = refs[2 * n:2 * n + 2]
        token = refs[-1]
        me, peers = _peers()
        for a in range(n):
            for k, peer in enumerate(peers):
                _exchange_copy(ins[a], zones[a], row_offs[a], _slot(*peer), _slot(*me), sems, 7 * a + k, peer).start()
        token[...] = jnp.zeros_like(token)

    arrays = [_hbm(a) for a in list(sends) + list(lands)]
    out = pl.pallas_call(
        body, name=name,
        out_shape=(pltpu.SemaphoreType.DMA((7 * n,)), pltpu.SemaphoreType.DMA((7 * n,)),
                   *[pltpu.HBM(a.shape, a.dtype) for a in arrays], jax.ShapeDtypeStruct((8, LANES), F32)),
        in_specs=[_HBM_SPEC] * (2 * n),
        out_specs=(_SEM_SPEC, _SEM_SPEC, *[_HBM_SPEC] * (2 * n), pl.BlockSpec(memory_space=pltpu.VMEM)),
        input_output_aliases={i: 2 + i for i in range(2 * n)},
        compiler_params=pltpu.CompilerParams(has_side_effects=_DATAFLOW),
    )(*arrays)
    return (out[0], out[1]), list(out[2:2 + n]), list(out[2 + n:2 + 2 * n]), out[-1]


def _exchange_wait(started, lands, after, name):
    n = len(lands)
    flat_sends = [s for _, sends, _ in started for s in sends]
    flat_sems = [s for sems, _, _ in started for s in sems]
    ns = len(flat_sends)

    def body(*refs):
        ins, zones = refs[:ns], refs[ns:ns + n]
        sem_refs = refs[ns + n:ns + n + len(flat_sems)]
        me, peers = _peers()
        pos = 0
        for call, (_, sends, row_offs) in enumerate(started):
            sems = sem_refs[2 * call:2 * call + 2]
            for a in range(len(sends)):
                for k, peer in enumerate(peers):
                    cp = _exchange_copy(ins[pos + a], zones[a], row_offs[a], _slot(*peer), _slot(*peer), sems,
                                        7 * a + k, peer)
                    cp.wait_send()
                    cp.wait_recv()
            pos += len(sends)

    arrays = list(flat_sends) + list(lands)
    out = pl.pallas_call(
        body, name=name,
        out_shape=tuple(pltpu.HBM(a.shape, a.dtype) for a in arrays),
        in_specs=[_HBM_SPEC] * len(arrays) + [_SEM_SPEC] * len(flat_sems) + [pl.BlockSpec(memory_space=pl.ANY)],
        out_specs=tuple([_HBM_SPEC] * len(arrays)),
        input_output_aliases={i: i for i in range(len(arrays))},
        compiler_params=pltpu.CompilerParams(has_side_effects=_DATAFLOW),
    )(*arrays, *flat_sems, after)
    return list(out[ns:])


def _bias_table(rel_bias, bucket):
    def body(rel_ref, bucket_ref, out_ref):
        bk = bucket_ref[...]
        for h in range(B_HEADS):
            def pick(b, acc, h=h):
                return jnp.where(bk == b, rel_ref[b, h], acc)
            out_ref[h] = lax.fori_loop(0, REL_BUCKETS, pick, jnp.zeros((BLOCK, BLOCK), F32))

    return pl.pallas_call(
        body, name="bias_table",
        out_shape=jax.ShapeDtypeStruct((B_HEADS, BLOCK, BLOCK), F32),
        in_specs=[pl.BlockSpec(memory_space=pltpu.SMEM), pl.BlockSpec(memory_space=pltpu.VMEM)],
        out_specs=pl.BlockSpec(memory_space=pltpu.VMEM),
    )(rel_bias, bucket)


def _inproj_fwd(x2, pre_g3, w, layer):
    tokens = x2.shape[0]
    tm = _token_tile(tokens)

    def body(x_ref, g_ref, w_ref, h_ref, pa_ref, pb_ref):
        x = x_ref[...]
        r = lax.rsqrt(jnp.mean(x * x, axis=-1, keepdims=True) + NORM_EPS)
        h = (x * r * g_ref[...]).astype(BF16)
        h_ref[...] = h
        pa_ref[...] = _mm_nt(h, w_ref[0:PA_WIDTH, :]).astype(BF16)
        pb_ref[...] = _mm_nt(h, w_ref[PA_WIDTH:IN_WIDTH, :]).astype(BF16)

    blocks = [_nbytes((tm, D_MODEL), F32), _nbytes((D_MODEL, IN_WIDTH), BF16),
              _nbytes((tm, D_MODEL), BF16), _nbytes((tm, PA_WIDTH), BF16), _nbytes((tm, PB_WIDTH), BF16)]
    return pl.pallas_call(
        body, name=f"inproj_fwd_{layer}", grid=(tokens // tm,),
        in_specs=[pl.BlockSpec((tm, D_MODEL), lambda i: (i, 0)),
                  pl.BlockSpec((None, 1, D_MODEL), lambda i: (layer, 0, 0)),
                  pl.BlockSpec((IN_WIDTH, D_MODEL), lambda i: (0, 0))],
        out_specs=[pl.BlockSpec((tm, D_MODEL), lambda i: (i, 0)),
                   pl.BlockSpec((tm, PA_WIDTH), lambda i: (i, 0)),
                   pl.BlockSpec((tm, PB_WIDTH), lambda i: (i, 0))],
        out_shape=[pltpu.HBM((tokens,D_MODEL), BF16),
                   pltpu.HBM((tokens,PA_WIDTH), BF16),
                   pltpu.HBM((tokens,PB_WIDTH), BF16)],
        compiler_params=pltpu.CompilerParams(dimension_semantics=("parallel",),
                                             vmem_limit_bytes=_vmem_limit(blocks)),
    )(_hbm(x2), pre_g3, _hbm(w))


def _gmlp_forward_chunk(pa, ln_g, ln_b, wm_ref, bsb):
    pu = pa[:, 0:A_WIDTH].astype(F32)
    pv = pa[:, A_WIDTH:2 * A_WIDTH].astype(F32)
    pz = pa[:, 2 * A_WIDTH:3 * A_WIDTH].astype(F32)
    u, gu = _gelu_and_grad(pu)
    vv, gv = _gelu_and_grad(pv)
    mu = jnp.mean(vv, axis=-1, keepdims=True)
    xc = vv - mu
    rstd = lax.rsqrt(jnp.mean(xc * xc, axis=-1, keepdims=True) + NORM_EPS)
    vhat = xc * rstd
    vnb = (vhat * ln_g + ln_b).astype(BF16)
    low = lax.broadcasted_iota(jnp.int32, (CHUNK, LANES), 1) < HEAD_DIM
    parts = []
    for p in range(A_GROUPS // 2):
        vp = vnb[:, LANES * p:LANES * (p + 1)]
        parts.append(jnp.where(low, _mm(wm_ref[2 * p], vp), _mm(wm_ref[2 * p + 1], vp)))
    mixed = jnp.concatenate(parts, axis=1) + bsb
    sg = _sigmoid(pz)
    return gu, gv, pz, u, vhat, rstd, vnb, mixed, sg


def _gmlp_fwd(pa, ln_g3, ln_b3, wm, bsb, layer):
    tokens = pa.shape[0]
    tc = _token_tile(tokens)

    def body(pa_ref, lg_ref, lb_ref, wm_ref, bsb_ref, ya_ref):
        def chunk(ci, carry):
            rows = pl.ds(pl.multiple_of(ci * CHUNK, CHUNK), CHUNK)
            _, _, pz, u, _, _, _, mixed, sg = _gmlp_forward_chunk(
                pa_ref[rows, :], lg_ref[...], lb_ref[...], wm_ref, bsb_ref[...])
            ya_ref[rows, :] = (u * mixed * (pz * sg)).astype(BF16)
            return carry
        lax.fori_loop(0, tc // CHUNK, chunk, 0)

    blocks = [_nbytes((tc, PA_WIDTH), BF16), _nbytes((A_GROUPS, CHUNK, CHUNK), BF16),
              _nbytes((CHUNK, A_WIDTH), F32), _nbytes((tc, A_WIDTH), BF16)]
    return pl.pallas_call(
        body, name=f"gmlp_fwd_{layer}", grid=(tokens // tc,),
        in_specs=[pl.BlockSpec((tc, PA_WIDTH), lambda i: (i, 0)),
                  pl.BlockSpec((None, 1, A_WIDTH), lambda i: (layer, 0, 0)),
                  pl.BlockSpec((None, 1, A_WIDTH), lambda i: (layer, 0, 0)),
                  pl.BlockSpec((None, A_GROUPS, CHUNK, CHUNK), lambda i: (layer, 0, 0, 0)),
                  pl.BlockSpec((None, CHUNK, A_WIDTH), lambda i: (layer, 0, 0))],
        out_specs=pl.BlockSpec((tc, A_WIDTH), lambda i: (i, 0)),
        out_shape=pltpu.HBM((tokens,A_WIDTH), BF16),
        compiler_params=pltpu.CompilerParams(dimension_semantics=("parallel",),
                                             vmem_limit_bytes=_vmem_limit(blocks)),
    )(_hbm(pa), ln_g3, ln_b3, wm, bsb)


def _kv_variants(kv):
    t = kv.astype(F32)
    rolled = pltpu.roll(t, HEAD_DIM, 1)
    low = lax.broadcasted_iota(jnp.int32, t.shape, 1) < HEAD_DIM
    zero = jnp.zeros_like(t)
    head0 = (jnp.where(low, t, zero).astype(BF16), jnp.where(low, zero, rolled).astype(BF16))
    head1 = (jnp.where(low, rolled, zero).astype(BF16), jnp.where(low, zero, t).astype(BF16))
    return head0, head1


def _band_masks():
    row = lax.broadcasted_iota(jnp.int32, (BLOCK, BLOCK), 0)
    col = lax.broadcasted_iota(jnp.int32, (BLOCK, BLOCK), 1)
    return col <= row


def _wrap(full, tri):
    return jnp.where(tri, full[:, BLOCK:2 * BLOCK], full[:, 0:BLOCK])


def _attn_probs(sf, bias_h, sink, tri, kill):
    s = _wrap(sf, tri) * SCALE + bias_h
    s = jnp.where(kill, NEG, s)
    m = jnp.maximum(jnp.max(s, axis=-1, keepdims=True), sink)
    e = jnp.exp(s - m)
    es = jnp.exp(sink - m)
    inv = 1.0 / (jnp.sum(e, axis=-1, keepdims=True) + es)
    return e * inv, es * inv


def _unwrap16(p, tri):
    zero = jnp.zeros_like(p)
    return jnp.concatenate([jnp.where(tri, zero, p), jnp.where(tri, p, zero)], axis=1).astype(BF16)


def _fill_kv(kv_ref, prev_ref, cur_ref):
    kv_ref[0:BLOCK, :] = prev_ref[...]
    kv_ref[BLOCK:, :] = cur_ref[:, K_OFF:K_OFF + 2 * KV_WIDTH]


def _attn_fwd(pb, bias, sinks, b_loc, layer):
    tokens = pb.shape[0]
    nb = tokens // b_loc // BLOCK
    tq = _token_tile(tokens)
    per_tile = tq // BLOCK

    def body(sink_ref, cur_ref, prev_ref, bias_ref, yb_ref, kv_ref):
        t = pl.program_id(0)
        _fill_kv(kv_ref, prev_ref, cur_ref)
        tri = _band_masks()

        def block(i, carry):
            start = pl.multiple_of(i * BLOCK, BLOCK)
            rows = pl.ds(start, BLOCK)
            first = lax.rem(t * per_tile + i, nb) == 0
            kill = jnp.logical_and(first, jnp.logical_not(tri))
            kv = kv_ref[pl.ds(start, 2 * BLOCK), :]
            k_ops = _kv_variants(kv[:, 0:KV_WIDTH])
            v_ops = _kv_variants(kv[:, KV_WIDTH:2 * KV_WIDTH])
            p16 = {}
            for kvh in range(2):
                pairs = (2 * kvh, 2 * kvh + 1)
                qs = jnp.concatenate([cur_ref[rows, LANES * p:LANES * (p + 1)] for p in pairs], axis=0)
                for j in range(2):
                    sf = _mm_nt(qs, k_ops[kvh][j])
                    for r, p in enumerate(pairs):
                        hd = 2 * p + j
                        probs, _ = _attn_probs(sf[BLOCK * r:BLOCK * (r + 1)], bias_ref[hd], sink_ref[layer, hd],
                                               tri, kill)
                        p16[hd] = _unwrap16(probs, tri)
            for kvh in range(2):
                pairs = (2 * kvh, 2 * kvh + 1)
                out = jnp.zeros((2 * BLOCK, LANES), F32)
                for j in range(2):
                    out = out + _mm(jnp.concatenate([p16[2 * p + j] for p in pairs], axis=0), v_ops[kvh][j])
                for r, p in enumerate(pairs):
                    bz = cur_ref[rows, BZ_OFF + LANES * p:BZ_OFF + LANES * (p + 1)].astype(F32)
                    yb_ref[rows, LANES * p:LANES * (p + 1)] = (
                        out[BLOCK * r:BLOCK * (r + 1)] * (bz * _sigmoid(bz))).astype(BF16)
            return carry

        lax.fori_loop(0, per_tile, block, 0)

    blocks = [_nbytes((tq, PB_WIDTH), BF16), _nbytes((BLOCK, 2 * KV_WIDTH), BF16),
              _nbytes((B_HEADS, BLOCK, BLOCK), F32), _nbytes((tq, B_WIDTH), BF16)]
    scratch = _nbytes((tq + BLOCK, 2 * KV_WIDTH), BF16)
    return pl.pallas_call(
        body, name=f"attn_fwd_{layer}", grid=(tokens // tq,),
        in_specs=[pl.BlockSpec(memory_space=pltpu.SMEM),
                  pl.BlockSpec((tq, PB_WIDTH), lambda t: (t, 0)),
                  pl.BlockSpec((BLOCK, 2 * KV_WIDTH),
                               lambda t: (jnp.maximum(t * per_tile - 1, 0), K_OFF // (2 * KV_WIDTH))),
                  pl.BlockSpec((B_HEADS, BLOCK, BLOCK), lambda t: (0, 0, 0))],
        out_specs=pl.BlockSpec((tq, B_WIDTH), lambda t: (t, 0)),
        out_shape=pltpu.HBM((tokens, B_WIDTH), BF16),
        scratch_shapes=[pltpu.VMEM((tq + BLOCK, 2 * KV_WIDTH), BF16)],
        compiler_params=pltpu.CompilerParams(dimension_semantics=("parallel",),
                                             vmem_limit_bytes=_vmem_limit(blocks, scratch)),
    )(sinks, _hbm(pb), _hbm(pb), _hbm(bias))


def _outproj_fwd(ya, yb, wo, x2, post_g3, layer, target=None):
    tokens = x2.shape[0]
    tm = _token_tile(tokens)
    steps = tokens // tm
    with_loss = target is not None

    def body(*refs):
        ya_ref, yb_ref, woa_ref, wob_ref, x_ref, g_ref = refs[:6]
        y = _mm(ya_ref[...], woa_ref[...]) + _mm(yb_ref[...], wob_ref[...])
        r = lax.rsqrt(jnp.mean(y * y, axis=-1, keepdims=True) + NORM_EPS)
        x_next = x_ref[...] + y * r * g_ref[...]
        if not with_loss:
            y_ref, xn_ref = refs[6:]
            y_ref[...] = y.astype(BF16)
            xn_ref[...] = x_next
            return
        t_ref, y_ref, grad_ref, loss_ref, acc_ref = refs[6:]
        i = pl.program_id(0)

        @pl.when(i == 0)
        def _():
            acc_ref[...] = jnp.zeros_like(acc_ref)

        y_ref[...] = y.astype(BF16)
        err = x_next - t_ref[...]
        grad_ref[...] = err * (1.0 / D_MODEL)
        acc_ref[...] += jnp.sum(err * err, axis=0, keepdims=True)

        @pl.when(i == steps - 1)
        def _():
            total = jnp.sum(acc_ref[...], axis=-1, keepdims=True) * (0.5 / D_MODEL)
            loss_ref[...] = jnp.broadcast_to(total, loss_ref.shape)

    half = D_MODEL // 2
    tile = pl.BlockSpec((tm, D_MODEL), lambda i: (i, 0))
    blocks = [2 * _nbytes((tm, half), BF16), 2 * _nbytes((half, D_MODEL), BF16), 3 * _nbytes((tm, D_MODEL), F32),
              _nbytes((tm, D_MODEL), BF16)]
    in_specs = [pl.BlockSpec((tm, half), lambda i: (i, 0)),
                pl.BlockSpec((tm, half), lambda i: (i, 0)),
                pl.BlockSpec((half, D_MODEL), lambda i: (0, 0)),
                pl.BlockSpec((half, D_MODEL), lambda i: (1, 0)),
                tile,
                pl.BlockSpec((None, 1, D_MODEL), lambda i: (layer, 0, 0))]
    out_specs = [tile, tile]
    out_shape = [pltpu.HBM((tokens, D_MODEL), BF16), pltpu.HBM((tokens, D_MODEL), F32)]
    operands = [_hbm(ya), _hbm(yb), _hbm(wo), _hbm(wo), _hbm(x2), post_g3]
    if with_loss:
        in_specs.append(tile)
        out_specs.append(pl.BlockSpec((1, LANES), lambda i: (0, 0)))
        out_shape.append(pltpu.HBM((1, LANES), F32))
        operands.append(_hbm(target))
    return pl.pallas_call(
        body, name=f"outproj_fwd_{layer}", grid=(steps,),
        in_specs=in_specs, out_specs=out_specs, out_shape=out_shape,
        scratch_shapes=[pltpu.VMEM((1, D_MODEL), F32)] if with_loss else [],
        compiler_params=pltpu.CompilerParams(dimension_semantics=("arbitrary" if with_loss else "parallel",),
                                             vmem_limit_bytes=_vmem_limit(blocks)),
    )(*operands)


def _outproj_bwd(g, y, ya, yb, wo, post_g3, layer):
    tokens = g.shape[0]
    tm = _token_tile(tokens)
    half = D_MODEL // 2
    steps = tokens // tm

    def body(g_ref, y_ref, ya_ref, yb_ref, woa_ref, wob_ref, pg_ref, dya_ref, dyb_ref, dwo16_ref, dpg_ref, dwo_ref):
        @pl.when(pl.program_id(0) == 0)
        def _():
            dwo_ref[...] = jnp.zeros_like(dwo_ref)
            dpg_ref[...] = jnp.zeros_like(dpg_ref)

        gv = g_ref[...]
        yf = y_ref[...].astype(F32)
        r = lax.rsqrt(jnp.mean(yf * yf, axis=-1, keepdims=True) + NORM_EPS)
        yhat = yf * r
        gg = gv * pg_ref[...]
        dy = r * (gg - yhat * jnp.mean(gg * yhat, axis=-1, keepdims=True))
        dpg_ref[...] += jnp.sum(gv * yhat, axis=0, keepdims=True)
        dyb16 = dy.astype(BF16)
        dya_ref[...] = _mm_nt(dyb16, woa_ref[...]).astype(BF16)
        dyb_ref[...] = _mm_nt(dyb16, wob_ref[...]).astype(BF16)
        dwo_ref[0:half, :] += _mm_tn(ya_ref[...], dyb16)
        dwo_ref[half:D_MODEL, :] += _mm_tn(yb_ref[...], dyb16)

        @pl.when(pl.program_id(0) == steps - 1)
        def _():
            dwo16_ref[...] = dwo_ref[...].astype(BF16)

    blocks = [_nbytes((tm, D_MODEL), F32), _nbytes((tm, D_MODEL), BF16), 4 * _nbytes((tm, half), BF16),
              2 * _nbytes((half, D_MODEL), BF16), _nbytes((D_MODEL, D_MODEL), BF16)]
    return pl.pallas_call(
        body, name=f"outproj_bwd_{layer}", grid=(tokens // tm,),
        in_specs=[pl.BlockSpec((tm, D_MODEL), lambda i: (i, 0)),
                  pl.BlockSpec((tm, D_MODEL), lambda i: (i, 0)),
                  pl.BlockSpec((tm, half), lambda i: (i, 0)),
                  pl.BlockSpec((tm, half), lambda i: (i, 0)),
                  pl.BlockSpec((half, D_MODEL), lambda i: (0, 0)),
                  pl.BlockSpec((half, D_MODEL), lambda i: (1, 0)),
                  pl.BlockSpec((None, 1, D_MODEL), lambda i: (layer, 0, 0))],
        out_specs=[pl.BlockSpec((tm, half), lambda i: (i, 0)),
                   pl.BlockSpec((tm, half), lambda i: (i, 0)),
                   pl.BlockSpec((D_MODEL, D_MODEL), lambda i: (0, 0)),
                   pl.BlockSpec((1, D_MODEL), lambda i: (0, 0))],
        out_shape=[pltpu.HBM((tokens,half), BF16),
                   pltpu.HBM((tokens,half), BF16),
                   pltpu.HBM((D_MODEL,D_MODEL), BF16),
                   pltpu.HBM((1,D_MODEL), F32)],
        scratch_shapes=[pltpu.VMEM((D_MODEL, D_MODEL), F32)],
        compiler_params=pltpu.CompilerParams(dimension_semantics=("arbitrary",),
                                             vmem_limit_bytes=_vmem_limit(blocks, _nbytes((D_MODEL, D_MODEL), F32))),
    )(_hbm(g), _hbm(y), _hbm(ya), _hbm(yb), _hbm(wo), _hbm(wo), post_g3)


def _gmlp_bwd(pa, dya, ln_g3, ln_b3, wm, wmt, bsb, layer):
    tokens = pa.shape[0]
    tc = _token_tile(tokens)
    steps = tokens // tc

    def body(pa_ref, dya_ref, lg_ref, lb_ref, wm_ref, wmt_ref, bsb_ref,
             da_ref, dws_ref, dbs_ref, dlg_ref, dlb_ref, dbsb_ref):
        i = pl.program_id(0)

        @pl.when(i == 0)
        def _():
            dws_ref[...] = jnp.zeros_like(dws_ref)
            dlg_ref[...] = jnp.zeros_like(dlg_ref)
            dlb_ref[...] = jnp.zeros_like(dlb_ref)
            dbsb_ref[...] = jnp.zeros_like(dbsb_ref)

        ln_g = lg_ref[...]
        low = lax.broadcasted_iota(jnp.int32, (CHUNK, LANES), 1) < HEAD_DIM

        def chunk(ci, carry):
            rows = pl.ds(pl.multiple_of(ci * CHUNK, CHUNK), CHUNK)
            gu, gv, pz, u, vhat, rstd, vnb, mixed, sg = _gmlp_forward_chunk(
                pa_ref[rows, :], ln_g, lb_ref[...], wm_ref, bsb_ref[...])
            dy = dya_ref[rows, :].astype(F32)
            sz = pz * sg
            dy_sz = dy * sz
            du = dy_sz * mixed
            dmixed = dy_sz * u
            dz = dy * (u * mixed) * (sg + sz * (1.0 - sg))
            dbsb_ref[...] += dmixed
            dmb = dmixed.astype(BF16)
            zero = jnp.zeros((CHUNK, LANES), BF16)
            parts = []
            for p in range(A_GROUPS // 2):
                dmp = dmb[:, LANES * p:LANES * (p + 1)]
                vp = vnb[:, LANES * p:LANES * (p + 1)]
                parts.append(jnp.where(low, _mm(wmt_ref[2 * p], dmp), _mm(wmt_ref[2 * p + 1], dmp)))
                dws_ref[2 * p] += _mm_nt(jnp.where(low, dmp, zero), vp)
                dws_ref[2 * p + 1] += _mm_nt(jnp.where(low, zero, dmp), vp)
            dvn = jnp.concatenate(parts, axis=1)
            dlg_ref[...] += jnp.sum(dvn * vhat, axis=0, keepdims=True)
            dlb_ref[...] += jnp.sum(dvn, axis=0, keepdims=True)
            dvh = dvn * ln_g
            dvv = rstd * (dvh - jnp.mean(dvh, axis=-1, keepdims=True)
                          - vhat * jnp.mean(dvh * vhat, axis=-1, keepdims=True))
            da_ref[rows, 0:A_WIDTH] = (du * gu).astype(BF16)
            da_ref[rows, A_WIDTH:2 * A_WIDTH] = (dvv * gv).astype(BF16)
            da_ref[rows, 2 * A_WIDTH:3 * A_WIDTH] = dz.astype(BF16)
            return carry

        lax.fori_loop(0, tc // CHUNK, chunk, 0)

        @pl.when(i == steps - 1)
        def _():
            causal = (lax.broadcasted_iota(jnp.int32, (CHUNK, CHUNK), 0)
                      >= lax.broadcasted_iota(jnp.int32, (CHUNK, CHUNK), 1))
            for h in range(A_GROUPS):
                dws_ref[h] = jnp.where(causal, dws_ref[h], 0.0)
            acc = dbsb_ref[...]
            lane_full = lax.broadcasted_iota(jnp.int32, (CHUNK, A_WIDTH), 1)
            lane_out = lax.broadcasted_iota(jnp.int32, (CHUNK, LANES), 1)
            out = jnp.zeros((CHUNK, LANES), F32)
            for h in range(A_GROUPS):
                in_group = jnp.logical_and(lane_full >= HEAD_DIM * h, lane_full < HEAD_DIM * (h + 1))
                s = jnp.sum(jnp.where(in_group, acc, 0.0), axis=-1, keepdims=True)
                out = jnp.where(lane_out == h, s, out)
            dbs_ref[...] = out

    blocks = [_nbytes((tc, PA_WIDTH), BF16), _nbytes((tc, A_WIDTH), BF16), 2 * _nbytes((A_GROUPS, CHUNK, CHUNK), BF16),
              _nbytes((CHUNK, A_WIDTH), F32), _nbytes((tc, PA_WIDTH), BF16), _nbytes((A_GROUPS, CHUNK, CHUNK), F32),
              _nbytes((CHUNK, LANES), F32)]
    return pl.pallas_call(
        body, name=f"gmlp_bwd_{layer}", grid=(steps,),
        in_specs=[pl.BlockSpec((tc, PA_WIDTH), lambda i: (i, 0)),
                  pl.BlockSpec((tc, A_WIDTH), lambda i: (i, 0)),
                  pl.BlockSpec((None, 1, A_WIDTH), lambda i: (layer, 0, 0)),
                  pl.BlockSpec((None, 1, A_WIDTH), lambda i: (layer, 0, 0)),
                  pl.BlockSpec((None, A_GROUPS, CHUNK, CHUNK), lambda i: (layer, 0, 0, 0)),
                  pl.BlockSpec((None, A_GROUPS, CHUNK, CHUNK), lambda i: (layer, 0, 0, 0)),
                  pl.BlockSpec((None, CHUNK, A_WIDTH), lambda i: (layer, 0, 0))],
        out_specs=[pl.BlockSpec((tc, PA_WIDTH), lambda i: (i, 0)),
                   pl.BlockSpec((A_GROUPS, CHUNK, CHUNK), lambda i: (0, 0, 0)),
                   pl.BlockSpec((CHUNK, LANES), lambda i: (0, 0)),
                   pl.BlockSpec((1, A_WIDTH), lambda i: (0, 0)),
                   pl.BlockSpec((1, A_WIDTH), lambda i: (0, 0))],
        out_shape=[pltpu.HBM((tokens,PA_WIDTH), BF16),
                   pltpu.HBM((A_GROUPS, CHUNK, CHUNK), F32),
                   pltpu.HBM((CHUNK, LANES), F32),
                   pltpu.HBM((1,A_WIDTH), F32),
                   pltpu.HBM((1,A_WIDTH), F32)],
        scratch_shapes=[pltpu.VMEM((CHUNK, A_WIDTH), F32)],
        compiler_params=pltpu.CompilerParams(dimension_semantics=("arbitrary",),
                                             vmem_limit_bytes=_vmem_limit(blocks, _nbytes((CHUNK, A_WIDTH), F32))),
    )(_hbm(pa), _hbm(dya), ln_g3, ln_b3, wm, wmt, bsb)


def _attn_bwd(pb, dyb, bias, dbias_in, sinks, b_loc, layer):
    tokens = pb.shape[0]
    nb = tokens // b_loc // BLOCK
    qz_width = 2 * B_WIDTH
    tq = _token_tile(tokens)
    per_tile = tq // BLOCK
    nt = tokens // tq

    def body(sink_ref, cur_ref, prev_ref, dyb_ref, bias_ref, dbias_in_ref, dqz_ref, dkv_ref, dbias_ref, dsink_ref,
             kv_ref, acc_ref):
        t = pl.program_id(0)

        @pl.when(t == 0)
        def _():
            dbias_ref[...] = dbias_in_ref[...]
            dsink_ref[...] = jnp.zeros_like(dsink_ref)
            acc_ref[0:BLOCK, :] = jnp.zeros((BLOCK, 2 * KV_WIDTH), F32)

        @pl.when(t < nt)
        def _():
            acc_ref[BLOCK:, :] = jnp.zeros((tq, 2 * KV_WIDTH), F32)
            _fill_kv(kv_ref, prev_ref, cur_ref)
            tri = _band_masks()
            low = lax.broadcasted_iota(jnp.int32, (BLOCK, LANES), 1) < HEAD_DIM
            low_kv = lax.broadcasted_iota(jnp.int32, (2 * BLOCK, LANES), 1) < HEAD_DIM

            def block(i, carry):
                start = pl.multiple_of(i * BLOCK, BLOCK)
                rows = pl.ds(start, BLOCK)
                first = lax.rem(t * per_tile + i, nb) == 0
                kill = jnp.logical_and(first, jnp.logical_not(tri))
                kv = kv_ref[pl.ds(start, 2 * BLOCK), :]
                k_ops = _kv_variants(kv[:, 0:KV_WIDTH])
                v_ops = _kv_variants(kv[:, KV_WIDTH:2 * KV_WIDTH])
                probs, sink_p, p16, qs, dos, delta, ds16 = {}, {}, {}, {}, {}, {}, {}
                for kvh in range(2):
                    pairs = (2 * kvh, 2 * kvh + 1)
                    qs[kvh] = jnp.concatenate([cur_ref[rows, LANES * p:LANES * (p + 1)] for p in pairs], axis=0)
                    for j in range(2):
                        sf = _mm_nt(qs[kvh], k_ops[kvh][j])
                        for r, p in enumerate(pairs):
                            hd = 2 * p + j
                            probs[hd], sink_p[hd] = _attn_probs(sf[BLOCK * r:BLOCK * (r + 1)], bias_ref[hd],
                                                                sink_ref[layer, hd], tri, kill)
                            p16[hd] = _unwrap16(probs[hd], tri)
                for kvh in range(2):
                    pairs = (2 * kvh, 2 * kvh + 1)
                    out = jnp.zeros((2 * BLOCK, LANES), F32)
                    for j in range(2):
                        out = out + _mm(jnp.concatenate([p16[2 * p + j] for p in pairs], axis=0), v_ops[kvh][j])
                    d_outs = []
                    for r, p in enumerate(pairs):
                        bz = cur_ref[rows, BZ_OFF + LANES * p:BZ_OFF + LANES * (p + 1)].astype(F32)
                        sg = _sigmoid(bz)
                        dyp = dyb_ref[rows, LANES * p:LANES * (p + 1)].astype(F32)
                        out_p = out[BLOCK * r:BLOCK * (r + 1)]
                        d_out = dyp * (bz * sg)
                        dqz_ref[rows, B_WIDTH + LANES * p:B_WIDTH + LANES * (p + 1)] = (
                            dyp * out_p * (sg * (1.0 + bz * (1.0 - sg)))).astype(BF16)
                        dod = d_out * out_p
                        delta[2 * p] = jnp.sum(jnp.where(low, dod, 0.0), axis=-1, keepdims=True)
                        delta[2 * p + 1] = jnp.sum(jnp.where(low, 0.0, dod), axis=-1, keepdims=True)
                        d_outs.append(d_out.astype(BF16))
                    dos[kvh] = jnp.concatenate(d_outs, axis=0)
                for kvh in range(2):
                    pairs = (2 * kvh, 2 * kvh + 1)
                    for j in range(2):
                        dpf = _mm_nt(dos[kvh], v_ops[kvh][j])
                        for r, p in enumerate(pairs):
                            hd = 2 * p + j
                            ds = probs[hd] * (_wrap(dpf[BLOCK * r:BLOCK * (r + 1)], tri) - delta[hd])
                            dsink_ref[hd:hd + 1, :] += jnp.sum(-sink_p[hd] * delta[hd], axis=0, keepdims=True)
                            dbias_ref[hd] += ds
                            ds16[hd] = _unwrap16(ds, tri)
                dk_acc = [[None, None], [None, None]]
                dv_acc = [[None, None], [None, None]]
                for kvh in range(2):
                    pairs = (2 * kvh, 2 * kvh + 1)
                    dq = jnp.zeros((2 * BLOCK, LANES), F32)
                    for j in range(2):
                        dss = jnp.concatenate([ds16[2 * p + j] for p in pairs], axis=0)
                        pss = jnp.concatenate([p16[2 * p + j] for p in pairs], axis=0)
                        dq = dq + _mm(dss, k_ops[kvh][j])
                        dk_acc[kvh][j] = _mm_tn(dss, qs[kvh])
                        dv_acc[kvh][j] = _mm_tn(pss, dos[kvh])
                    for r, p in enumerate(pairs):
                        dqz_ref[rows, LANES * p:LANES * (p + 1)] = (dq[BLOCK * r:BLOCK * (r + 1)] * SCALE).astype(BF16)

                def fold(acc):
                    return jnp.where(low_kv,
                                     acc[0][0] + pltpu.roll(acc[0][1], HEAD_DIM, 1),
                                     pltpu.roll(acc[1][0], HEAD_DIM, 1) + acc[1][1])

                acc_ref[pl.ds(start, 2 * BLOCK), :] += jnp.concatenate(
                    [fold(dk_acc) * SCALE, fold(dv_acc)], axis=1)
                return carry

            lax.fori_loop(0, per_tile, block, 0)
            dkv_ref[...] = acc_ref[0:tq, :].astype(BF16)
            acc_ref[0:BLOCK, :] = acc_ref[tq:tq + BLOCK, :]

        @pl.when(t == nt)
        def _():
            dkv_ref[0:BLOCK, :] = acc_ref[0:BLOCK, :].astype(BF16)
            dkv_ref[BLOCK:, :] = jnp.zeros((tq - BLOCK, 2 * KV_WIDTH), BF16)

    def cur_map(t):
        return (jnp.minimum(t, nt - 1), 0)

    def prev_map(t):
        return (jnp.maximum(jnp.minimum(t, nt - 1) * per_tile - 1, 0), K_OFF // (2 * KV_WIDTH))

    blocks = [_nbytes((tq, PB_WIDTH), BF16), _nbytes((BLOCK, 2 * KV_WIDTH), BF16), _nbytes((tq, B_WIDTH), BF16),
              3 * _nbytes((B_HEADS, BLOCK, BLOCK), F32), _nbytes((tq, qz_width), BF16),
              _nbytes((tq, 2 * KV_WIDTH), BF16), _nbytes((B_HEADS, LANES), F32)]
    scratch = _nbytes((tq + BLOCK, 2 * KV_WIDTH), BF16) + _nbytes((tq + BLOCK, 2 * KV_WIDTH), F32)
    return pl.pallas_call(
        body, name=f"attn_bwd_{layer}", grid=(nt + 1,),
        in_specs=[pl.BlockSpec(memory_space=pltpu.SMEM),
                  pl.BlockSpec((tq, PB_WIDTH), cur_map),
                  pl.BlockSpec((BLOCK, 2 * KV_WIDTH), prev_map),
                  pl.BlockSpec((tq, B_WIDTH), cur_map),
                  pl.BlockSpec((B_HEADS, BLOCK, BLOCK), lambda t: (0, 0, 0)),
                  pl.BlockSpec((B_HEADS, BLOCK, BLOCK), lambda t: (0, 0, 0))],
        out_specs=[pl.BlockSpec((tq, qz_width), cur_map),
                   pl.BlockSpec((tq, 2 * KV_WIDTH), lambda t: (t, 0)),
                   pl.BlockSpec((B_HEADS, BLOCK, BLOCK), lambda t: (0, 0, 0)),
                   pl.BlockSpec((B_HEADS, LANES), lambda t: (0, 0))],
        out_shape=[pltpu.HBM((tokens, qz_width), BF16),
                   pltpu.HBM((tokens + tq, 2 * KV_WIDTH), BF16),
                   pltpu.HBM((B_HEADS, BLOCK, BLOCK), F32),
                   pltpu.HBM((B_HEADS, LANES), F32)],
        scratch_shapes=[pltpu.VMEM((tq + BLOCK, 2 * KV_WIDTH), BF16),
                        pltpu.VMEM((tq + BLOCK, 2 * KV_WIDTH), F32)],
        compiler_params=pltpu.CompilerParams(dimension_semantics=("arbitrary",),
                                             vmem_limit_bytes=_vmem_limit(blocks, scratch)),
    )(sinks, _hbm(pb), _hbm(pb), _hbm(dyb), _hbm(bias), _hbm(dbias_in))


def _inproj_bwd_dx(da, dqz, dkv, w, x2, g, pre_g3, layer):
    tokens = x2.shape[0]
    tm = _token_tile(tokens)

    def body(da_ref, dqz_ref, dkv_ref, w_ref, x_ref, g_ref, pg_ref, gn_ref, dpg_ref):
        @pl.when(pl.program_id(0) == 0)
        def _():
            dpg_ref[...] = jnp.zeros_like(dpg_ref)

        dh = _mm(da_ref[...], w_ref[0:PA_WIDTH, :])
        dh += _mm(dqz_ref[:, 0:B_WIDTH], w_ref[PA_WIDTH:PA_WIDTH + B_WIDTH, :])
        dh += _mm(dkv_ref[...], w_ref[PA_WIDTH + K_OFF:PA_WIDTH + BZ_OFF, :])
        dh += _mm(dqz_ref[:, B_WIDTH:2 * B_WIDTH], w_ref[PA_WIDTH + BZ_OFF:IN_WIDTH, :])
        x = x_ref[...]
        r = lax.rsqrt(jnp.mean(x * x, axis=-1, keepdims=True) + NORM_EPS)
        xhat = x * r
        dhg = dh * pg_ref[...]
        dpg_ref[...] += jnp.sum(dh * xhat, axis=0, keepdims=True)
        gn_ref[...] = g_ref[...] + r * (dhg - xhat * jnp.mean(dhg * xhat, axis=-1, keepdims=True))

    blocks = [_nbytes((tm, PA_WIDTH), BF16), _nbytes((tm, 2 * B_WIDTH), BF16), _nbytes((tm, 2 * KV_WIDTH), BF16),
              _nbytes((D_MODEL, IN_WIDTH), BF16), 3 * _nbytes((tm, D_MODEL), F32)]
    return pl.pallas_call(
        body, name=f"inproj_bwd_dx_{layer}", grid=(tokens // tm,),
        in_specs=[pl.BlockSpec((tm, PA_WIDTH), lambda i: (i, 0)),
                  pl.BlockSpec((tm, 2 * B_WIDTH), lambda i: (i, 0)),
                  pl.BlockSpec((pl.Element(tm), pl.Element(2 * KV_WIDTH)), lambda i: (pl.multiple_of(i * tm + BLOCK, BLOCK), 0)),
                  pl.BlockSpec((IN_WIDTH, D_MODEL), lambda i: (0, 0)),
                  pl.BlockSpec((tm, D_MODEL), lambda i: (i, 0)),
                  pl.BlockSpec((tm, D_MODEL), lambda i: (i, 0)),
                  pl.BlockSpec((None, 1, D_MODEL), lambda i: (layer, 0, 0))],
        out_specs=[pl.BlockSpec((tm, D_MODEL), lambda i: (i, 0)),
                   pl.BlockSpec((1, D_MODEL), lambda i: (0, 0))],
        out_shape=[pltpu.HBM((tokens,D_MODEL), F32),
                   pltpu.HBM((1,D_MODEL), F32)],
        compiler_params=pltpu.CompilerParams(dimension_semantics=("arbitrary",),
                                             vmem_limit_bytes=_vmem_limit(blocks)),
    )(_hbm(da), _hbm(dqz), _hbm(dkv), _hbm(w), _hbm(x2), _hbm(g), pre_g3)


def _inproj_bwd_dw(h, da, dqz, dkv, layer):
    tokens = h.shape[0]
    tm = _token_tile(tokens)
    steps = tokens // tm

    def body(h_ref, da_ref, dqz_ref, dkv_ref, dw_ref, acc_ref):
        i = pl.program_id(0)

        @pl.when(i == 0)
        def _():
            acc_ref[...] = jnp.zeros_like(acc_ref)

        hv = h_ref[...]
        acc_ref[:, 0:PA_WIDTH] += _mm_tn(hv, da_ref[...])
        acc_ref[:, PA_WIDTH:PA_WIDTH + B_WIDTH] += _mm_tn(hv, dqz_ref[:, 0:B_WIDTH])
        acc_ref[:, PA_WIDTH + K_OFF:PA_WIDTH + BZ_OFF] += _mm_tn(hv, dkv_ref[...])
        acc_ref[:, PA_WIDTH + BZ_OFF:IN_WIDTH] += _mm_tn(hv, dqz_ref[:, B_WIDTH:2 * B_WIDTH])

        @pl.when(i == steps - 1)
        def _():
            for c in range(IN_WIDTH // LANES):
                cols = slice(LANES * c, LANES * (c + 1))
                dw_ref[cols, :] = acc_ref[:, cols].T.astype(BF16)

    blocks = [_nbytes((tm, D_MODEL), BF16), _nbytes((tm, PA_WIDTH), BF16), _nbytes((tm, 2 * B_WIDTH), BF16),
              _nbytes((tm, 2 * KV_WIDTH), BF16), _nbytes((D_MODEL, IN_WIDTH), BF16)]
    return pl.pallas_call(
        body, name=f"inproj_bwd_dw_{layer}", grid=(steps,),
        in_specs=[pl.BlockSpec((tm, D_MODEL), lambda i: (i, 0)),
                  pl.BlockSpec((tm, PA_WIDTH), lambda i: (i, 0)),
                  pl.BlockSpec((tm, 2 * B_WIDTH), lambda i: (i, 0)),
                  pl.BlockSpec((pl.Element(tm), pl.Element(2 * KV_WIDTH)), lambda i: (pl.multiple_of(i * tm + BLOCK, BLOCK), 0))],
        out_specs=pl.BlockSpec((IN_WIDTH, D_MODEL), lambda i: (0, 0)),
        out_shape=pltpu.HBM((IN_WIDTH, D_MODEL), BF16),
        scratch_shapes=[pltpu.VMEM((D_MODEL, IN_WIDTH), F32)],
        compiler_params=pltpu.CompilerParams(dimension_semantics=("arbitrary",),
                                             vmem_limit_bytes=_vmem_limit(blocks, _nbytes((D_MODEL, IN_WIDTH), F32))),
    )(_hbm(h), _hbm(da), _hbm(dqz), _hbm(dkv))


def _rel_bias_grad(dbias, col_bucket, flip):
    def body(db_ref, cb_ref, flip_ref, out_ref):
        cb = cb_ref[...]
        anti = flip_ref[...]
        sums = []
        for h in range(B_HEADS):
            x = db_ref[h]
            hi = x.astype(BF16)
            rest = x - hi.astype(F32)
            mid = rest.astype(BF16)
            low = (rest - mid.astype(F32)).astype(BF16)
            reversed_x = _mm(hi, anti) + _mm(mid, anti) + _mm(low, anti)
            rolled = pltpu.roll(reversed_x, 0, 1, stride=1, stride_axis=0)
            sums.append(jnp.sum(rolled, axis=0, keepdims=True))
        per_dist = jnp.concatenate(sums, axis=0)
        lane = lax.broadcasted_iota(jnp.int32, (B_HEADS, LANES), 1)
        out = jnp.zeros((B_HEADS, LANES), F32)
        for b in range(REL_BUCKETS):
            s = jnp.sum(jnp.where(cb == b, per_dist, 0.0), axis=-1, keepdims=True)
            out = jnp.where(lane == b, s, out)
        out_ref[...] = out

    vm = pl.BlockSpec(memory_space=pltpu.VMEM)
    return pl.pallas_call(
        body, name="rel_bias_grad",
        out_shape=jax.ShapeDtypeStruct((B_HEADS, LANES), F32),
        in_specs=[vm, vm, vm], out_specs=vm,
    )(dbias, col_bucket, flip)


def _reduce_adamw(slots, w, m, v, name):
    rows, cols = w.shape
    tr = _row_tile(rows)
    c1 = 1.0 / (1.0 - ADAM_B1 ** ADAM_STEP)
    c2 = 1.0 / (1.0 - ADAM_B2 ** ADAM_STEP)

    def body(s_ref, w_ref, m_ref, v_ref, g_ref, d_ref, nm_ref, nv_ref):
        g = s_ref[0].astype(F32)
        for i in range(1, N_DEV):
            g = g + s_ref[i].astype(F32)
        nm = ADAM_B1 * m_ref[...] + (1.0 - ADAM_B1) * g
        nv = ADAM_B2 * v_ref[...] + (1.0 - ADAM_B2) * (g * g)
        g_ref[...] = g
        nm_ref[...] = nm
        nv_ref[...] = nv
        d_ref[...] = -ADAM_LR * ((nm * c1) / (jnp.sqrt(nv * c2) + ADAM_EPS) + ADAM_WD * w_ref[...])

    blocks = [_nbytes((N_DEV, tr, cols), slots.dtype), 7 * _nbytes((tr, cols), F32)]
    tile = pl.BlockSpec((tr, cols), lambda i: (i, 0))
    return pl.pallas_call(
        body, name=name, grid=(rows // tr,),
        in_specs=[pl.BlockSpec((N_DEV, tr, cols), lambda i: (0, i, 0)), tile, tile, tile],
        out_specs=[tile] * 4,
        out_shape=[pltpu.HBM((rows, cols), F32)] * 4,
        compiler_params=pltpu.CompilerParams(dimension_semantics=("parallel",),
                                             vmem_limit_bytes=_vmem_limit(blocks)),
    )(_hbm(slots), _hbm(w), _hbm(m), _hbm(v))


_SMALL = ("pre_norm_g","post_norm_g", "ln_v_g", "ln_v_b", "b_spatial", "sinks", "rel_bias")


def _pack_small(parts):
    slabs = []
    for name in _SMALL:
        flat = parts[name].astype(F32).reshape(-1)
        pad = (-flat.shape[0]) % (8 * LANES)
        slabs.append(jnp.pad(flat, (0, pad)).reshape(-1, LANES))
    return jnp.concatenate(slabs, axis=0)


def _unpack_small(slab, shapes):
    out, row = {}, 0
    for name in _SMALL:
        size = int(np.prod(shapes[name]))
        rows = -(-size // (8 * LANES)) * 8
        out[name] = slab[row:row + rows].reshape(-1)[:size].reshape(shapes[name])
        row += rows
    return out


def kernel(x, pre_norm_g, w_in, ln_v_g, ln_v_b, w_spatial, b_spatial, sinks, rel_bias, w_out, post_norm_g, loss_target, m_pre_norm_g, m_w_in, m_ln_v_g, m_ln_v_b, m_w_spatial, m_b_spatial, m_sinks, m_rel_bias, m_w_out, m_post_norm_g, v_pre_norm_g, v_w_in, v_ln_v_g, v_ln_v_b, v_w_spatial, v_b_spatial, v_sinks, v_rel_bias, v_w_out, v_post_norm_g):
    b_loc, seq, _ = x.shape
    tokens = b_loc * seq
    depth = w_in.shape[0]
    in_shard = w_in.shape[2]
    out_shard = w_out.shape[1]
    assert in_shard * N_DEV == IN_WIDTH and out_shard * N_DEV == D_MODEL and seq % BLOCK == 0

    me = _slot(lax.axis_index("x"), lax.axis_index("y"), lax.axis_index("c"))
    w_in_t, m_w_in_t, v_w_in_t = (jnp.swapaxes(a, 1, 2) for a in (w_in, m_w_in, v_w_in))
    w_in16, w_out16 = w_in_t.astype(BF16), w_out.astype(BF16)

    def gather_start(layer, not_before):
        shards, _ = lax.optimization_barrier(((w_in16[layer], w_out16[layer]), not_before))
        zones = [lax.empty((N_DEV, in_shard, D_MODEL), BF16), lax.empty((N_DEV, out_shard, D_MODEL), BF16)]
        sems, sends, zones, token = _exchange_start(list(shards), zones, [0, 0], f"weights_send_{layer}")
        return (sems, sends, [0, 0]), zones, token

    def full_weights(gin, gout):
        return gin.reshape(IN_WIDTH, D_MODEL), gout.reshape(D_MODEL, D_MODEL)

    def gather_wait(layer, started, zones, after):
        gin, gout = _exchange_wait([started], zones, after, f"weights_wait_{layer}")
        gin = lax.dynamic_update_index_in_dim(gin, w_in16[layer], me, 0)
        gout = lax.dynamic_update_index_in_dim(gout, w_out16[layer], me, 0)
        return full_weights(gin, gout)

    causal = jnp.tril(jnp.ones((CHUNK, CHUNK), dtype=bool))
    wm = jnp.where(causal, w_spatial, 0.0).astype(BF16)
    wmt = _hbm(jnp.swapaxes(wm, -1, -2))
    wm = _hbm(wm)
    bsb = _hbm(jnp.repeat(jnp.swapaxes(b_spatial, -1, -2), HEAD_DIM, axis=-1))
    pre_g3 = _hbm(pre_norm_g.reshape(depth, 1, D_MODEL))
    post_g3 = _hbm(post_norm_g.reshape(depth, 1, D_MODEL))
    ln_g3 = _hbm(ln_v_g.reshape(depth, 1, A_WIDTH))
    ln_b3 = _hbm(ln_v_b.reshape(depth, 1, A_WIDTH))
    bucket = jnp.asarray(_bucket_table())
    bias = _bias_table(rel_bias, bucket)

    xs, saved, weights = [x.reshape(tokens, D_MODEL)], [], []
    pending = None
    for layer in range(depth):
        if layer == 0:
            w, wo = full_weights(*_all_gather([w_in16[0], w_out16[0]], "weights_gather_0"))
        else:
            w, wo = gather_wait(layer, pending[0], pending[1], xs[-1])
        pre_g_fwd = pre_g3
        if layer + 1 < depth:
            pending = gather_start(layer + 1, w)
            pre_g_fwd = _hbm(_after(pre_norm_g.reshape(depth, 1, D_MODEL), pending[2]))
        h, pa, pb = _inproj_fwd(xs[-1], pre_g_fwd, w, layer)
        ya = _gmlp_fwd(pa, ln_g3, ln_b3, wm, bsb, layer)
        yb = _attn_fwd(pb, bias, sinks, b_loc, layer)
        if layer + 1 < depth:
            y, x_next = _outproj_fwd(ya, yb, wo, xs[-1], post_g3, layer)
            xs.append(x_next)
        else:
            y, g, loss_part = _outproj_fwd(ya, yb, wo, xs[-1], post_g3, layer,
                                           target=loss_target.reshape(tokens, D_MODEL))
        saved.append((h, pa, pb, ya, yb, y))
        weights.append((w, wo))
    loss = lax.psum(loss_part[0, 0], ("x", "y", "c"))

    grads = {name: [None] * depth for name in ("pre_norm_g","post_norm_g", "ln_v_g", "ln_v_b",
                                               "b_spatial", "sinks", "dbias")}
    zone_in = lax.empty((N_DEV, depth * in_shard, D_MODEL), BF16)
    zone_out = lax.empty((N_DEV, depth * out_shard, D_MODEL), BF16)
    zone_ws = lax.empty((N_DEV, depth * A_GROUPS * CHUNK, CHUNK), F32)
    started_in, started_out, own_in, own_out = [], [], [None] * depth, [None] * depth
    started_ws, own_ws = [], [None] * depth
    dbias = jnp.zeros((B_HEADS, BLOCK, BLOCK), F32)
    for layer in reversed(range(depth)):
        h, pa, pb, ya, yb, y = saved[layer]
        w, wo = weights[layer]
        dya, dyb, dwo, dpost = _outproj_bwd(g, y, ya, yb, wo, post_g3, layer)
        send_out = dwo.reshape(N_DEV, out_shard, D_MODEL)
        own_out[layer] = lax.dynamic_index_in_dim(send_out, me, 0, keepdims=False)
        sems, sends, (zone_out,), token = _exchange_start([send_out], [zone_out], [layer * out_shard],
                                                          f"grads_send_out_{layer}")
        started_out.append((sems, sends, [layer * out_shard]))
        ln_g_bwd = _hbm(_after(ln_v_g.reshape(depth, 1, A_WIDTH), token))
        da, dws, dbs, dlg, dlb = _gmlp_bwd(pa, dya, ln_g_bwd, ln_b3, wm, wmt, bsb, layer)
        own_ws[layer] = dws.reshape(A_GROUPS * CHUNK, CHUNK)
        sems, sends, (zone_ws,), token_ws = _exchange_start([own_ws[layer]], [zone_ws], [layer * A_GROUPS * CHUNK],
                                                            f"grads_send_ws_{layer}")
        started_ws.append((sems, sends, [layer * A_GROUPS * CHUNK]))
        dqz, dkv_shifted, dbias, dsink = _attn_bwd(pb, dyb, bias, dbias, sinks, b_loc, layer)
        dkv = dkv_shifted
        dw = _inproj_bwd_dw(h, da, dqz, dkv, layer)
        send_in = dw.reshape(N_DEV, in_shard, D_MODEL)
        own_in[layer] = lax.dynamic_index_in_dim(send_in, me, 0, keepdims=False)
        sems, sends, (zone_in,), token = _exchange_start([send_in], [zone_in], [layer * in_shard],
                                                         f"grads_send_in_{layer}")
        started_in.append((sems, sends, [layer * in_shard]))
        pre_g_bwd = _hbm(_after(pre_norm_g.reshape(depth, 1, D_MODEL), token, token_ws))
        g, dpre = _inproj_bwd_dx(da, dqz, dkv, w, xs[layer], g, pre_g_bwd, layer)
        grads["b_spatial"][layer] = dbs[:, :A_GROUPS].T
        grads["ln_v_g"][layer] = dlg[0]
        grads["ln_v_b"][layer] = dlb[0]
        grads["sinks"][layer] = dsink[:, 0]
        grads["pre_norm_g"][layer] = dpre[0]
        grads["post_norm_g"][layer] = dpost[0]
    grad_x = g.reshape(x.shape)
    col_bucket = jnp.asarray(np.broadcast_to(_bucket_table()[0:1, ::-1], (B_HEADS, BLOCK)))
    flip = jnp.asarray(np.eye(BLOCK, dtype=np.float32)[::-1], dtype=BF16)
    drel = _rel_bias_grad(dbias, col_bucket, flip)[:, :REL_BUCKETS].T

    (recv_out,) = _exchange_wait(started_out, [zone_out], g, "grads_wait_out")
    recv_out = lax.dynamic_update_index_in_dim(recv_out, jnp.concatenate(own_out, axis=0), me, 0)
    res_out = _reduce_adamw(recv_out, w_out.reshape(-1, D_MODEL), m_w_out.reshape(-1, D_MODEL),
                            v_w_out.reshape(-1, D_MODEL), "adamw_w_out")
    res_out = [r.reshape(w_out.shape) for r in res_out]

    (recv_ws,) = _exchange_wait(started_ws, [zone_ws], g, "grads_wait_ws")
    recv_ws = lax.dynamic_update_index_in_dim(recv_ws, jnp.concatenate(own_ws, axis=0), me, 0)
    res_ws = _reduce_adamw(recv_ws, w_spatial.reshape(-1, CHUNK), m_w_spatial.reshape(-1, CHUNK),
                           v_w_spatial.reshape(-1, CHUNK), "adamw_w_spatial")
    res_ws = [r.reshape(w_spatial.shape) for r in res_ws]
    small_w = dict(pre_norm_g=pre_norm_g, post_norm_g=post_norm_g, ln_v_g=ln_v_g,
                   ln_v_b=ln_v_b, b_spatial=b_spatial, sinks=sinks, rel_bias=rel_bias)
    small_m = dict(pre_norm_g=m_pre_norm_g, post_norm_g=m_post_norm_g, ln_v_g=m_ln_v_g,
                   ln_v_b=m_ln_v_b, b_spatial=m_b_spatial, sinks=m_sinks, rel_bias=m_rel_bias)
    small_v = dict(pre_norm_g=v_pre_norm_g, post_norm_g=v_post_norm_g, ln_v_g=v_ln_v_g,
                   ln_v_b=v_ln_v_b, b_spatial=v_b_spatial, sinks=v_sinks, rel_bias=v_rel_bias)
    small_g = {name: jnp.stack(grads[name]) for name in _SMALL if name != "rel_bias"}
    small_g["rel_bias"] = drel
    shapes = {name: small_w[name].shape for name in _SMALL}
    (slots,) = _all_gather([_pack_small(small_g)], "small_grads_all_gather")
    res_small = _reduce_adamw(slots, _pack_small(small_w), _pack_small(small_m), _pack_small(small_v), "adamw_small")

    (recv_in,) = _exchange_wait(started_in, [zone_in], res_small[0], "grads_wait_in")
    recv_in = lax.dynamic_update_index_in_dim(recv_in, jnp.concatenate(own_in, axis=0), me, 0)
    res_in = _reduce_adamw(recv_in, w_in_t.reshape(-1, D_MODEL), m_w_in_t.reshape(-1, D_MODEL),
                           v_w_in_t.reshape(-1, D_MODEL), "adamw_w_in")
    res_in = [jnp.swapaxes(r.reshape(w_in_t.shape), 1, 2) for r in res_in]
    res_small = [_unpack_small(r, shapes) for r in res_small]

    order = ("pre_norm_g", "w_in", "ln_v_g", "ln_v_b", "w_spatial", "b_spatial", "sinks", "rel_bias", "w_out",
             "post_norm_g")
    outs = [loss, grad_x]
    for kind in range(4):
        for name in order:
            if name == "w_in":
                outs.append(res_in[kind])
            elif name == "w_out":
                outs.append(res_out[kind])
            elif name == "w_spatial":
                outs.append(res_ws[kind])
            else:
                outs.append(res_small[kind][name])
    return tuple(outs)
```

```python
import math

import numpy as np
import jax
import jax.numpy as jnp
from jax import lax
from jax.experimental import pallas as pl
from jax.experimental.pallas import tpu as pltpu

F32 = jnp.float32
BF16 = jnp.bfloat16

D_MODEL = 1024
A_WIDTH = 512
A_GROUPS = 8
CHUNK = 128
B_HEADS = 8
HEAD_DIM = 64
B_WIDTH = 512
KV_WIDTH = 128
BLOCK = 128
REL_BUCKETS = 32
REL_MAX_DIST = 128
NORM_EPS = 1e-6
PA_WIDTH = 3 * A_WIDTH
PB_WIDTH = 2 * B_WIDTH + 2 * KV_WIDTH
IN_WIDTH = PA_WIDTH + PB_WIDTH
K_OFF, V_OFF, BZ_OFF = B_WIDTH, B_WIDTH + KV_WIDTH, B_WIDTH + 2 * KV_WIDTH
SCALE = HEAD_DIM ** -0.5
NEG = -1e30
N_DEV = 8
LANES = 128

ADAM_LR = 0.001
ADAM_B1 = 0.9
ADAM_B2 = 0.999
ADAM_EPS = 1e-08
ADAM_WD = 0.01
ADAM_STEP = 10

V7X_VMEM_BYTES = 64 * 1024 * 1024
VMEM_TEMP_BYTES = 12 * 1024 * 1024
MESH = pl.DeviceIdType.MESH


def _vmem_limit(block_bytes, scratch_bytes=0):
    need = 2 * sum(block_bytes) + scratch_bytes + VMEM_TEMP_BYTES
    return int(min(need, V7X_VMEM_BYTES - 8 * 1024 * 1024))


def _nbytes(shape, dtype):
    return int(np.prod(shape)) * jnp.dtype(dtype).itemsize


def _token_tile(tokens):
    tile = min(512, tokens // 2)
    assert tokens % tile == 0 and tile % CHUNK == 0, tokens
    return tile


def _row_tile(rows, cap=512):
    best = 8
    for t in range(8, cap + 1, 8):
        if rows % t == 0:
            best = t
    assert rows % best == 0, rows
    return best


def _mm(a, b):
    return lax.dot_general(a, b, (((1,), (0,)), ((), ())), preferred_element_type=F32)


def _mm_nt(a, b):
    return lax.dot_general(a, b, (((1,), (1,)), ((), ())), preferred_element_type=F32)


def _mm_tn(a, b):
    return lax.dot_general(a, b, (((0,), (0,)), ((), ())), preferred_element_type=F32)


_GELU_C = math.sqrt(2.0 / math.pi)


_GELU_A = _GELU_C * 0.044715


def _gelu_parts(x):
    x2 = x * x
    t = jnp.tanh(x * (_GELU_C + _GELU_A * x2))
    return x2, t, 0.5 + 0.5 * t


def _gelu(x):
    return x * _gelu_parts(x)[2]


def _gelu_and_grad(x):
    x2, t, half_plus = _gelu_parts(x)
    grad = half_plus + (0.5 * x) * (1.0 - t * t) * (_GELU_C + (3.0 * _GELU_A) * x2)
    return x * half_plus, grad


def _sigmoid(x):
    return 0.5 + 0.5 * jnp.tanh(0.5 * x)


def _bucket_table():
    q = np.arange(BLOCK)[:, None]
    k = np.arange(BLOCK)[None, :]
    dist = np.where(k <= q, q - k, q + BLOCK - k)
    max_exact = REL_BUCKETS // 2
    safe = np.maximum(dist, 1).astype(np.float32)
    large = max_exact + (np.log(safe / np.float32(max_exact)) / np.float32(math.log(REL_MAX_DIST / max_exact))
                         * np.float32(REL_BUCKETS - max_exact)).astype(np.int32)
    large = np.minimum(large, REL_BUCKETS - 1)
    assert dist.min() >= 0 and dist.max() < BLOCK
    return np.where(dist < max_exact, dist, large).astype(np.int32)


def _hbm(x):
    return pltpu.with_memory_space_constraint(x, pltpu.HBM)


def _slot(px, py, pc):
    return 4 * px + 2 * py + pc


def _all_gather(arrs, name):
    n = len(arrs)

    def body(*refs):
        ins, outs = refs[:n], refs[n:2 * n]
        send_sems, recv_sems, local_sems = refs[2 * n:]
        x, y, c = lax.axis_index("x"), lax.axis_index("y"), lax.axis_index("c")
        me, sibling = (x, y, c), (x, y, 1 - c)
        chips = [(1 - x, y), (x, 1 - y), (1 - x, 1 - y)]

        def copy(a, k, block, to, src=None):
            dst = outs[a].at[_slot(*block)]
            return pltpu.make_async_remote_copy(
                src_ref=dst if src is None else src, dst_ref=dst,
                send_sem=send_sems.at[7 * a + k], recv_sem=recv_sems.at[7 * a + k],
                device_id=to, device_id_type=MESH)

        mine = [pltpu.make_async_copy(ins[a], outs[a].at[_slot(*me)], local_sems.at[a]) for a in range(n)]
        for cp in mine:
            cp.start()
        first = []
        for a in range(n):
            first.append(copy(a, 0, me, sibling, src=ins[a]))
            first += [copy(a, 1 + j, me, (*chip, c), src=ins[a]) for j, chip in enumerate(chips)]
        for cp in first:
            cp.start()
        passed = []
        for j, chip in enumerate(chips):
            for a in range(n):
                copy(a, 1 + j, (*chip, c), me).wait_recv()
                fwd = copy(a, 4 + j, (*chip, c), sibling)
                fwd.start()
                passed.append(fwd)
        for a in range(n):
            copy(a, 0, sibling, me).wait_recv()
            for j, chip in enumerate(chips):
                copy(a, 4 + j, (*chip, 1 - c), me).wait_recv()
        for cp in first + passed:
            cp.wait_send()
        for cp in mine:
            cp.wait()

    any_spec = pl.BlockSpec(memory_space=pl.ANY)
    return pl.pallas_call(
        body, name=name,
        out_shape=[jax.ShapeDtypeStruct((N_DEV,) + a.shape, a.dtype) for a in arrs],
        in_specs=[any_spec] * n, out_specs=[any_spec] * n,
        scratch_shapes=[pltpu.SemaphoreType.DMA((7 * n,)), pltpu.SemaphoreType.DMA((7 * n,)),
                        pltpu.SemaphoreType.DMA((n,))],
    )(*arrs)


_HBM_SPEC = pl.BlockSpec(memory_space=pltpu.HBM)
_SEM_SPEC = pl.BlockSpec(memory_space=pltpu.SEMAPHORE)
_DATAFLOW = pltpu.SideEffectType.DATAFLOW_SIDE_EFFECTING


def _peers():
    x, y, c = lax.axis_index("x"), lax.axis_index("y"), lax.axis_index("c")
    peers = []
    for k in range(1, N_DEV):
        fx, fy, fc = (k >> 2) & 1, (k >> 1) & 1, k & 1
        peers.append((1 - x if fx else x, 1 - y if fy else y, 1 - c if fc else c))
    return (x, y, c), peers


def _exchange_copy(send_ref, land_ref, row_off, src_slot, dst_slot, sems, idx, peer):
    rows = send_ref.shape[-2]
    src = send_ref.at[src_slot] if len(send_ref.shape) == 3 else send_ref
    return pltpu.make_async_remote_copy(
        src_ref=src, dst_ref=land_ref.at[dst_slot, pl.ds(row_off, rows), :],
        send_sem=sems[0].at[idx], recv_sem=sems[1].at[idx], device_id=peer, device_id_type=MESH)


def _exchange_start(sends, lands, row_offs, name, carry, after=None):
    n = len(sends)

    def body(*refs):
        ins, zones, carry_ref = refs[:n], refs[n:2 * n], refs[2 * n]
        first_out = 2 * n + 1 + (after is not None)
        sems = refs[first_out:first_out + 2]
        carry_out = refs[-1]
        me, peers = _peers()
        for a in range(n):
            for k, peer in enumerate(peers):
                _exchange_copy(ins[a], zones[a], row_offs[a], _slot(*peer), _slot(*me), sems, 7 * a + k, peer).start()
        carry_out[...] = carry_ref[...]

    arrays = [_hbm(a) for a in list(sends) + list(lands)]
    vmem = pl.BlockSpec(memory_space=pltpu.VMEM)
    out = pl.pallas_call(
        body, name=name,
        out_shape=(pltpu.SemaphoreType.DMA((7 * n,)), pltpu.SemaphoreType.DMA((7 * n,)),
                   *[pltpu.HBM(a.shape, a.dtype) for a in arrays], jax.ShapeDtypeStruct(carry.shape, carry.dtype)),
        in_specs=[_HBM_SPEC] * (2 * n) + [vmem] + ([pl.BlockSpec(memory_space=pl.ANY)] if after is not None else []),
        out_specs=(_SEM_SPEC, _SEM_SPEC, *[_HBM_SPEC] * (2 * n), vmem),
        input_output_aliases={i: 2 + i for i in range(2 * n)},
        compiler_params=pltpu.CompilerParams(has_side_effects=_DATAFLOW),
    )(*arrays, carry, *([after] if after is not None else []))
    return (out[0], out[1]), list(out[2:2 + n]), list(out[2 + n:2 + 2 * n]), out[-1]


def _exchange_wait(started, lands, after, name):
    n = len(lands)
    flat_sends = [s for _, sends, _ in started for s in sends]
    flat_sems = [s for sems, _, _ in started for s in sems]
    ns = len(flat_sends)

    def body(*refs):
        ins, zones = refs[:ns], refs[ns:ns + n]
        sem_refs = refs[ns + n:ns + n + len(flat_sems)]
        me, peers = _peers()
        pos = 0
        for call, (_, sends, row_offs) in enumerate(started):
            sems = sem_refs[2 * call:2 * call + 2]
            for a in range(len(sends)):
                for k, peer in enumerate(peers):
                    cp = _exchange_copy(ins[pos + a], zones[a], row_offs[a], _slot(*peer), _slot(*peer), sems,
                                        7 * a + k, peer)
                    cp.wait_send()
                    cp.wait_recv()
            pos += len(sends)

    arrays = list(flat_sends) + list(lands)
    out = pl.pallas_call(
        body, name=name,
        out_shape=tuple(pltpu.HBM(a.shape, a.dtype) for a in arrays),
        in_specs=[_HBM_SPEC] * len(arrays) + [_SEM_SPEC] * len(flat_sems) + [pl.BlockSpec(memory_space=pl.ANY)],
        out_specs=tuple([_HBM_SPEC] * len(arrays)),
        input_output_aliases={i: i for i in range(len(arrays))},
        compiler_params=pltpu.CompilerParams(has_side_effects=_DATAFLOW),
    )(*arrays, *flat_sems, after)
    return list(out[ns:])


def _bias_table(rel_bias, bucket):
    def body(rel_ref, bucket_ref, out_ref):
        bk = bucket_ref[...]
        for h in range(B_HEADS):
            def pick(b, acc, h=h):
                return jnp.where(bk == b, rel_ref[b, h], acc)
            out_ref[h] = lax.fori_loop(0, REL_BUCKETS, pick, jnp.zeros((BLOCK, BLOCK), F32))

    return pl.pallas_call(
        body, name="bias_table",
        out_shape=jax.ShapeDtypeStruct((B_HEADS, BLOCK, BLOCK), F32),
        in_specs=[pl.BlockSpec(memory_space=pltpu.SMEM), pl.BlockSpec(memory_space=pltpu.VMEM)],
        out_specs=pl.BlockSpec(memory_space=pltpu.VMEM),
    )(rel_bias, bucket)


def _inproj_fwd(x2, pre_g3, w, layer):
    tokens = x2.shape[0]
    tm = _token_tile(tokens)

    def body(x_ref, g_ref, w_ref, h_ref, pa_ref, pb_ref):
        x = x_ref[...]
        r = lax.rsqrt(jnp.mean(x * x, axis=-1, keepdims=True) + NORM_EPS)
        h = (x * r * g_ref[...]).astype(BF16)
        h_ref[...] = h
        pa_ref[...] = _mm_nt(h, w_ref[0:PA_WIDTH, :]).astype(BF16)
        pb_ref[...] = _mm_nt(h, w_ref[PA_WIDTH:IN_WIDTH, :]).astype(BF16)

    blocks = [_nbytes((tm, D_MODEL), F32), _nbytes((D_MODEL, IN_WIDTH), BF16),
              _nbytes((tm, D_MODEL), BF16), _nbytes((tm, PA_WIDTH), BF16), _nbytes((tm, PB_WIDTH), BF16)]
    return pl.pallas_call(
        body, name=f"inproj_fwd_{layer}", grid=(tokens // tm,),
        in_specs=[pl.BlockSpec((tm, D_MODEL), lambda i: (i, 0)),
                  pl.BlockSpec((None, 1, D_MODEL), lambda i: (layer, 0, 0)),
                  pl.BlockSpec((IN_WIDTH, D_MODEL), lambda i: (0, 0))],
        out_specs=[pl.BlockSpec((tm, D_MODEL), lambda i: (i, 0)),
                   pl.BlockSpec((tm, PA_WIDTH), lambda i: (i, 0)),
                   pl.BlockSpec((tm, PB_WIDTH), lambda i: (i, 0))],
        out_shape=[pltpu.HBM((tokens,D_MODEL), BF16),
                   pltpu.HBM((tokens,PA_WIDTH), BF16),
                   pltpu.HBM((tokens,PB_WIDTH), BF16)],
        compiler_params=pltpu.CompilerParams(dimension_semantics=("parallel",),
                                             vmem_limit_bytes=_vmem_limit(blocks)),
    )(_hbm(x2), pre_g3, _hbm(w))


def _gmlp_forward_chunk(pa, ln_g, ln_b, wm_ref, bsb):
    pu = pa[:, 0:A_WIDTH].astype(F32)
    pv = pa[:, A_WIDTH:2 * A_WIDTH].astype(F32)
    pz = pa[:, 2 * A_WIDTH:3 * A_WIDTH].astype(F32)
    u, gu = _gelu_and_grad(pu)
    vv, gv = _gelu_and_grad(pv)
    mu = jnp.mean(vv, axis=-1, keepdims=True)
    xc = vv - mu
    rstd = lax.rsqrt(jnp.mean(xc * xc, axis=-1, keepdims=True) + NORM_EPS)
    vhat = xc * rstd
    vnb = (vhat * ln_g + ln_b).astype(BF16)
    low = lax.broadcasted_iota(jnp.int32, (CHUNK, LANES), 1) < HEAD_DIM
    parts = []
    for p in range(A_GROUPS // 2):
        vp = vnb[:, LANES * p:LANES * (p + 1)]
        parts.append(jnp.where(low, _mm(wm_ref[2 * p], vp), _mm(wm_ref[2 * p + 1], vp)))
    mixed = jnp.concatenate(parts, axis=1) + bsb
    sg = _sigmoid(pz)
    return gu, gv, pz, u, vhat, rstd, vnb, mixed, sg


def _gmlp_fwd(pa, ln_g3, ln_b3, wm, bsb, layer):
    tokens = pa.shape[0]
    tc = _token_tile(tokens)

    def body(pa_ref, lg_ref, lb_ref, wm_ref, bsb_ref, ya_ref):
        def chunk(ci, carry):
            rows = pl.ds(pl.multiple_of(ci * CHUNK, CHUNK), CHUNK)
            _, _, pz, u, _, _, _, mixed, sg = _gmlp_forward_chunk(
                pa_ref[rows, :], lg_ref[...], lb_ref[...], wm_ref, bsb_ref[...])
            ya_ref[rows, :] = (u * mixed * (pz * sg)).astype(BF16)
            return carry
        lax.fori_loop(0, tc // CHUNK, chunk, 0)

    blocks = [_nbytes((tc, PA_WIDTH), BF16), _nbytes((A_GROUPS, CHUNK, CHUNK), BF16),
              _nbytes((CHUNK, A_WIDTH), F32), _nbytes((tc, A_WIDTH), BF16)]
    return pl.pallas_call(
        body, name=f"gmlp_fwd_{layer}", grid=(tokens // tc,),
        in_specs=[pl.BlockSpec((tc, PA_WIDTH), lambda i: (i, 0)),
                  pl.BlockSpec((None, 1, A_WIDTH), lambda i: (layer, 0, 0)),
                  pl.BlockSpec((None, 1, A_WIDTH), lambda i: (layer, 0, 0)),
                  pl.BlockSpec((None, A_GROUPS, CHUNK, CHUNK), lambda i: (layer, 0, 0, 0)),
                  pl.BlockSpec((None, CHUNK, A_WIDTH), lambda i: (layer, 0, 0))],
        out_specs=pl.BlockSpec((tc, A_WIDTH), lambda i: (i, 0)),
        out_shape=pltpu.HBM((tokens,A_WIDTH), BF16),
        compiler_params=pltpu.CompilerParams(dimension_semantics=("parallel",),
                                             vmem_limit_bytes=_vmem_limit(blocks)),
    )(_hbm(pa), ln_g3, ln_b3, wm, bsb)


def _kv_variants(kv):
    t = kv.astype(F32)
    rolled = pltpu.roll(t, HEAD_DIM, 1)
    low = lax.broadcasted_iota(jnp.int32, t.shape, 1) < HEAD_DIM
    zero = jnp.zeros_like(t)
    head0 = (jnp.where(low, t, zero).astype(BF16), jnp.where(low, zero, rolled).astype(BF16))
    head1 = (jnp.where(low, rolled, zero).astype(BF16), jnp.where(low, zero, t).astype(BF16))
    return head0, head1


def _band_masks():
    row = lax.broadcasted_iota(jnp.int32, (BLOCK, BLOCK), 0)
    col = lax.broadcasted_iota(jnp.int32, (BLOCK, BLOCK), 1)
    return col <= row


def _wrap(full, tri):
    return jnp.where(tri, full[:, BLOCK:2 * BLOCK], full[:, 0:BLOCK])


def _attn_probs(sf, bias_h, sink, tri, kill):
    s = _wrap(sf, tri) * SCALE + bias_h
    s = jnp.where(kill, NEG, s)
    m = jnp.maximum(jnp.max(s, axis=-1, keepdims=True), sink)
    e = jnp.exp(s - m)
    es = jnp.exp(sink - m)
    inv = 1.0 / (jnp.sum(e, axis=-1, keepdims=True) + es)
    return e * inv, es * inv


def _unwrap16(p, tri):
    zero = jnp.zeros_like(p)
    return jnp.concatenate([jnp.where(tri, zero, p), jnp.where(tri, p, zero)], axis=1).astype(BF16)


def _fill_kv(kv_ref, prev_ref, cur_ref):
    kv_ref[0:BLOCK, :] = prev_ref[...]
    kv_ref[BLOCK:, :] = cur_ref[:, K_OFF:K_OFF + 2 * KV_WIDTH]


def _attn_fwd(pb, bias, sinks, b_loc, layer):
    tokens = pb.shape[0]
    nb = tokens // b_loc // BLOCK
    tq = _token_tile(tokens)
    per_tile = tq // BLOCK

    def body(sink_ref, cur_ref, prev_ref, bias_ref, yb_ref, kv_ref):
        t = pl.program_id(0)
        _fill_kv(kv_ref, prev_ref, cur_ref)
        tri = _band_masks()

        def block(i, carry):
            start = pl.multiple_of(i * BLOCK, BLOCK)
            rows = pl.ds(start, BLOCK)
            first = lax.rem(t * per_tile + i, nb) == 0
            kill = jnp.logical_and(first, jnp.logical_not(tri))
            kv = kv_ref[pl.ds(start, 2 * BLOCK), :]
            k_ops = _kv_variants(kv[:, 0:KV_WIDTH])
            v_ops = _kv_variants(kv[:, KV_WIDTH:2 * KV_WIDTH])
            p16 = {}
            for kvh in range(2):
                pairs = (2 * kvh, 2 * kvh + 1)
                qs = jnp.concatenate([cur_ref[rows, LANES * p:LANES * (p + 1)] for p in pairs], axis=0)
                for j in range(2):
                    sf = _mm_nt(qs, k_ops[kvh][j])
                    for r, p in enumerate(pairs):
                        hd = 2 * p + j
                        probs, _ = _attn_probs(sf[BLOCK * r:BLOCK * (r + 1)], bias_ref[hd], sink_ref[layer, hd],
                                               tri, kill)
                        p16[hd] = _unwrap16(probs, tri)
            for kvh in range(2):
                pairs = (2 * kvh, 2 * kvh + 1)
                out = jnp.zeros((2 * BLOCK, LANES), F32)
                for j in range(2):
                    out = out + _mm(jnp.concatenate([p16[2 * p + j] for p in pairs], axis=0), v_ops[kvh][j])
                for r, p in enumerate(pairs):
                    bz = cur_ref[rows, BZ_OFF + LANES * p:BZ_OFF + LANES * (p + 1)].astype(F32)
                    yb_ref[rows, LANES * p:LANES * (p + 1)] = (
                        out[BLOCK * r:BLOCK * (r + 1)] * (bz * _sigmoid(bz))).astype(BF16)
            return carry

        lax.fori_loop(0, per_tile, block, 0)

    blocks = [_nbytes((tq, PB_WIDTH), BF16), _nbytes((BLOCK, 2 * KV_WIDTH), BF16),
              _nbytes((B_HEADS, BLOCK, BLOCK), F32), _nbytes((tq, B_WIDTH), BF16)]
    scratch = _nbytes((tq + BLOCK, 2 * KV_WIDTH), BF16)
    return pl.pallas_call(
        body, name=f"attn_fwd_{layer}", grid=(tokens // tq,),
        in_specs=[pl.BlockSpec(memory_space=pltpu.SMEM),
                  pl.BlockSpec((tq, PB_WIDTH), lambda t: (t, 0)),
                  pl.BlockSpec((BLOCK, 2 * KV_WIDTH),
                               lambda t: (jnp.maximum(t * per_tile - 1, 0), K_OFF // (2 * KV_WIDTH))),
                  pl.BlockSpec((B_HEADS, BLOCK, BLOCK), lambda t: (0, 0, 0))],
        out_specs=pl.BlockSpec((tq, B_WIDTH), lambda t: (t, 0)),
        out_shape=pltpu.HBM((tokens, B_WIDTH), BF16),
        scratch_shapes=[pltpu.VMEM((tq + BLOCK, 2 * KV_WIDTH), BF16)],
        compiler_params=pltpu.CompilerParams(dimension_semantics=("parallel",),
                                             vmem_limit_bytes=_vmem_limit(blocks, scratch)),
    )(sinks, _hbm(pb), _hbm(pb), _hbm(bias))


def _outproj_fwd(ya, yb, wo, x2, post_g3, layer, target=None):
    tokens = x2.shape[0]
    tm = _token_tile(tokens)
    steps = tokens // tm
    with_loss = target is not None

    def body(*refs):
        ya_ref, yb_ref, woa_ref, wob_ref, x_ref, g_ref = refs[:6]
        y = _mm(ya_ref[...], woa_ref[...]) + _mm(yb_ref[...], wob_ref[...])
        r = lax.rsqrt(jnp.mean(y * y, axis=-1, keepdims=True) + NORM_EPS)
        x_next = x_ref[...] + y * r * g_ref[...]
        if not with_loss:
            y_ref, xn_ref = refs[6:]
            y_ref[...] = y.astype(BF16)
            xn_ref[...] = x_next
            return
        t_ref, y_ref, grad_ref, loss_ref, acc_ref = refs[6:]
        i = pl.program_id(0)

        @pl.when(i == 0)
        def _():
            acc_ref[...] = jnp.zeros_like(acc_ref)

        y_ref[...] = y.astype(BF16)
        err = x_next - t_ref[...]
        grad_ref[...] = err * (1.0 / D_MODEL)
        acc_ref[...] += jnp.sum(err * err, axis=0, keepdims=True)

        @pl.when(i == steps - 1)
        def _():
            total = jnp.sum(acc_ref[...], axis=-1, keepdims=True) * (0.5 / D_MODEL)
            loss_ref[...] = jnp.broadcast_to(total, loss_ref.shape)

    half = D_MODEL // 2
    tile = pl.BlockSpec((tm, D_MODEL), lambda i: (i, 0))
    blocks = [2 * _nbytes((tm, half), BF16), 2 * _nbytes((half, D_MODEL), BF16), 3 * _nbytes((tm, D_MODEL), F32),
              _nbytes((tm, D_MODEL), BF16)]
    in_specs = [pl.BlockSpec((tm, half), lambda i: (i, 0)),
                pl.BlockSpec((tm, half), lambda i: (i, 0)),
                pl.BlockSpec((half, D_MODEL), lambda i: (0, 0)),
                pl.BlockSpec((half, D_MODEL), lambda i: (1, 0)),
                tile,
                pl.BlockSpec((None, 1, D_MODEL), lambda i: (layer, 0, 0))]
    out_specs = [tile, tile]
    out_shape = [pltpu.HBM((tokens, D_MODEL), BF16), pltpu.HBM((tokens, D_MODEL), F32)]
    operands = [_hbm(ya), _hbm(yb), _hbm(wo), _hbm(wo), _hbm(x2), post_g3]
    if with_loss:
        in_specs.append(tile)
        out_specs.append(pl.BlockSpec((1, LANES), lambda i: (0, 0)))
        out_shape.append(pltpu.HBM((1, LANES), F32))
        operands.append(_hbm(target))
    return pl.pallas_call(
        body, name=f"outproj_fwd_{layer}", grid=(steps,),
        in_specs=in_specs, out_specs=out_specs, out_shape=out_shape,
        scratch_shapes=[pltpu.VMEM((1, D_MODEL), F32)] if with_loss else [],
        compiler_params=pltpu.CompilerParams(dimension_semantics=("arbitrary" if with_loss else "parallel",),
                                             vmem_limit_bytes=_vmem_limit(blocks)),
    )(*operands)


def _outproj_bwd(g, y, ya, yb, wo, post_g3, layer):
    tokens = g.shape[0]
    tm = _token_tile(tokens)
    half = D_MODEL // 2
    steps = tokens // tm

    def body(g_ref, y_ref, ya_ref, yb_ref, woa_ref, wob_ref, pg_ref, dya_ref, dyb_ref, dwo16_ref, dpg_ref, dwo_ref):
        @pl.when(pl.program_id(0) == 0)
        def _():
            dwo_ref[...] = jnp.zeros_like(dwo_ref)
            dpg_ref[...] = jnp.zeros_like(dpg_ref)

        gv = g_ref[...]
        yf = y_ref[...].astype(F32)
        r = lax.rsqrt(jnp.mean(yf * yf, axis=-1, keepdims=True) + NORM_EPS)
        yhat = yf * r
        gg = gv * pg_ref[...]
        dy = r * (gg - yhat * jnp.mean(gg * yhat, axis=-1, keepdims=True))
        dpg_ref[...] += jnp.sum(gv * yhat, axis=0, keepdims=True)
        dyb16 = dy.astype(BF16)
        dya_ref[...] = _mm_nt(dyb16, woa_ref[...]).astype(BF16)
        dyb_ref[...] = _mm_nt(dyb16, wob_ref[...]).astype(BF16)
        dwo_ref[0:half, :] += _mm_tn(ya_ref[...], dyb16)
        dwo_ref[half:D_MODEL, :] += _mm_tn(yb_ref[...], dyb16)

        @pl.when(pl.program_id(0) == steps - 1)
        def _():
            dwo16_ref[...] = dwo_ref[...].astype(BF16)

    blocks = [_nbytes((tm, D_MODEL), F32), _nbytes((tm, D_MODEL), BF16), 4 * _nbytes((tm, half), BF16),
              2 * _nbytes((half, D_MODEL), BF16), _nbytes((D_MODEL, D_MODEL), BF16)]
    return pl.pallas_call(
        body, name=f"outproj_bwd_{layer}", grid=(tokens // tm,),
        in_specs=[pl.BlockSpec((tm, D_MODEL), lambda i: (i, 0)),
                  pl.BlockSpec((tm, D_MODEL), lambda i: (i, 0)),
                  pl.BlockSpec((tm, half), lambda i: (i, 0)),
                  pl.BlockSpec((tm, half), lambda i: (i, 0)),
                  pl.BlockSpec((half, D_MODEL), lambda i: (0, 0)),
                  pl.BlockSpec((half, D_MODEL), lambda i: (1, 0)),
                  pl.BlockSpec((None, 1, D_MODEL), lambda i: (layer, 0, 0))],
        out_specs=[pl.BlockSpec((tm, half), lambda i: (i, 0)),
                   pl.BlockSpec((tm, half), lambda i: (i, 0)),
                   pl.BlockSpec((D_MODEL, D_MODEL), lambda i: (0, 0)),
                   pl.BlockSpec((1, D_MODEL), lambda i: (0, 0))],
        out_shape=[pltpu.HBM((tokens,half), BF16),
                   pltpu.HBM((tokens,half), BF16),
                   pltpu.HBM((D_MODEL,D_MODEL), BF16),
                   pltpu.HBM((1,D_MODEL), F32)],
        scratch_shapes=[pltpu.VMEM((D_MODEL, D_MODEL), F32)],
        compiler_params=pltpu.CompilerParams(dimension_semantics=("arbitrary",),
                                             vmem_limit_bytes=_vmem_limit(blocks, _nbytes((D_MODEL, D_MODEL), F32))),
    )(_hbm(g), _hbm(y), _hbm(ya), _hbm(yb), _hbm(wo), _hbm(wo), post_g3)


def _gmlp_bwd(pa, dya, ln_g3, ln_b3, wm, wmt, bsb, layer):
    tokens = pa.shape[0]
    tc = _token_tile(tokens)
    steps = tokens // tc

    def body(pa_ref, dya_ref, lg_ref, lb_ref, wm_ref, wmt_ref, bsb_ref,
             da_ref, dws_ref, dbs_ref, dlg_ref, dlb_ref, dbsb_ref):
        i = pl.program_id(0)

        @pl.when(i == 0)
        def _():
            dws_ref[...] = jnp.zeros_like(dws_ref)
            dlg_ref[...] = jnp.zeros_like(dlg_ref)
            dlb_ref[...] = jnp.zeros_like(dlb_ref)
            dbsb_ref[...] = jnp.zeros_like(dbsb_ref)

        ln_g = lg_ref[...]
        low = lax.broadcasted_iota(jnp.int32, (CHUNK, LANES), 1) < HEAD_DIM

        def chunk(ci, carry):
            rows = pl.ds(pl.multiple_of(ci * CHUNK, CHUNK), CHUNK)
            gu, gv, pz, u, vhat, rstd, vnb, mixed, sg = _gmlp_forward_chunk(
                pa_ref[rows, :], ln_g, lb_ref[...], wm_ref, bsb_ref[...])
            dy = dya_ref[rows, :].astype(F32)
            sz = pz * sg
            dy_sz = dy * sz
            du = dy_sz * mixed
            dmixed = dy_sz * u
            dz = dy * (u * mixed) * (sg + sz * (1.0 - sg))
            dbsb_ref[...] += dmixed
            dmb = dmixed.astype(BF16)
            zero = jnp.zeros((CHUNK, LANES), BF16)
            parts = []
            for p in range(A_GROUPS // 2):
                dmp = dmb[:, LANES * p:LANES * (p + 1)]
                vp = vnb[:, LANES * p:LANES * (p + 1)]
                parts.append(jnp.where(low, _mm(wmt_ref[2 * p], dmp), _mm(wmt_ref[2 * p + 1], dmp)))
                dws_ref[2 * p] += _mm_nt(jnp.where(low, dmp, zero), vp)
                dws_ref[2 * p + 1] += _mm_nt(jnp.where(low, zero, dmp), vp)
            dvn = jnp.concatenate(parts, axis=1)
            dlg_ref[...] += jnp.sum(dvn * vhat, axis=0, keepdims=True)
            dlb_ref[...] += jnp.sum(dvn, axis=0, keepdims=True)
            dvh = dvn * ln_g
            dvv = rstd * (dvh - jnp.mean(dvh, axis=-1, keepdims=True)
                          - vhat * jnp.mean(dvh * vhat, axis=-1, keepdims=True))
            da_ref[rows, 0:A_WIDTH] = (du * gu).astype(BF16)
            da_ref[rows, A_WIDTH:2 * A_WIDTH] = (dvv * gv).astype(BF16)
            da_ref[rows, 2 * A_WIDTH:3 * A_WIDTH] = dz.astype(BF16)
            return carry

        lax.fori_loop(0, tc // CHUNK, chunk, 0)

        @pl.when(i == steps - 1)
        def _():
            causal = (lax.broadcasted_iota(jnp.int32, (CHUNK, CHUNK), 0)
                      >= lax.broadcasted_iota(jnp.int32, (CHUNK, CHUNK), 1))
            for h in range(A_GROUPS):
                dws_ref[h] = jnp.where(causal, dws_ref[h], 0.0)
            acc = dbsb_ref[...]
            lane_full = lax.broadcasted_iota(jnp.int32, (CHUNK, A_WIDTH), 1)
            lane_out = lax.broadcasted_iota(jnp.int32, (CHUNK, LANES), 1)
            out = jnp.zeros((CHUNK, LANES), F32)
            for h in range(A_GROUPS):
                in_group = jnp.logical_and(lane_full >= HEAD_DIM * h, lane_full < HEAD_DIM * (h + 1))
                s = jnp.sum(jnp.where(in_group, acc, 0.0), axis=-1, keepdims=True)
                out = jnp.where(lane_out == h, s, out)
            dbs_ref[...] = out

    blocks = [_nbytes((tc, PA_WIDTH), BF16), _nbytes((tc, A_WIDTH), BF16), 2 * _nbytes((A_GROUPS, CHUNK, CHUNK), BF16),
              _nbytes((CHUNK, A_WIDTH), F32), _nbytes((tc, PA_WIDTH), BF16), _nbytes((A_GROUPS, CHUNK, CHUNK), F32),
              _nbytes((CHUNK, LANES), F32)]
    return pl.pallas_call(
        body, name=f"gmlp_bwd_{layer}", grid=(steps,),
        in_specs=[pl.BlockSpec((tc, PA_WIDTH), lambda i: (i, 0)),
                  pl.BlockSpec((tc, A_WIDTH), lambda i: (i, 0)),
                  pl.BlockSpec((None, 1, A_WIDTH), lambda i: (layer, 0, 0)),
                  pl.BlockSpec((None, 1, A_WIDTH), lambda i: (layer, 0, 0)),
                  pl.BlockSpec((None, A_GROUPS, CHUNK, CHUNK), lambda i: (layer, 0, 0, 0)),
                  pl.BlockSpec((None, A_GROUPS, CHUNK, CHUNK), lambda i: (layer, 0, 0, 0)),
                  pl.BlockSpec((None, CHUNK, A_WIDTH), lambda i: (layer, 0, 0))],
        out_specs=[pl.BlockSpec((tc, PA_WIDTH), lambda i: (i, 0)),
                   pl.BlockSpec((A_GROUPS, CHUNK, CHUNK), lambda i: (0, 0, 0)),
                   pl.BlockSpec((CHUNK, LANES), lambda i: (0, 0)),
                   pl.BlockSpec((1, A_WIDTH), lambda i: (0, 0)),
                   pl.BlockSpec((1, A_WIDTH), lambda i: (0, 0))],
        out_shape=[pltpu.HBM((tokens,PA_WIDTH), BF16),
                   pltpu.HBM((A_GROUPS, CHUNK, CHUNK), F32),
                   pltpu.HBM((CHUNK, LANES), F32),
                   pltpu.HBM((1,A_WIDTH), F32),
                   pltpu.HBM((1,A_WIDTH), F32)],
        scratch_shapes=[pltpu.VMEM((CHUNK, A_WIDTH), F32)],
        compiler_params=pltpu.CompilerParams(dimension_semantics=("arbitrary",),
                                             vmem_limit_bytes=_vmem_limit(blocks, _nbytes((CHUNK, A_WIDTH), F32))),
    )(_hbm(pa), _hbm(dya), ln_g3, ln_b3, wm, wmt, bsb)


def _attn_bwd(pb, dyb, bias, dbias_in, sinks, b_loc, layer):
    tokens = pb.shape[0]
    nb = tokens // b_loc // BLOCK
    qz_width = 2 * B_WIDTH
    tq = _token_tile(tokens)
    per_tile = tq // BLOCK
    nt = tokens // tq

    def body(sink_ref, cur_ref, prev_ref, dyb_ref, bias_ref, dbias_in_ref, dqz_ref, dkv_ref, dbias_ref, dsink_ref,
             kv_ref, acc_ref):
        t = pl.program_id(0)

        @pl.when(t == 0)
        def _():
            dbias_ref[...] = dbias_in_ref[...]
            dsink_ref[...] = jnp.zeros_like(dsink_ref)
            acc_ref[0:BLOCK, :] = jnp.zeros((BLOCK, 2 * KV_WIDTH), F32)

        @pl.when(t < nt)
        def _():
            acc_ref[BLOCK:, :] = jnp.zeros((tq, 2 * KV_WIDTH), F32)
            _fill_kv(kv_ref, prev_ref, cur_ref)
            tri = _band_masks()
            low = lax.broadcasted_iota(jnp.int32, (BLOCK, LANES), 1) < HEAD_DIM
            low_kv = lax.broadcasted_iota(jnp.int32, (2 * BLOCK, LANES), 1) < HEAD_DIM

            def block(i, carry):
                start = pl.multiple_of(i * BLOCK, BLOCK)
                rows = pl.ds(start, BLOCK)
                first = lax.rem(t * per_tile + i, nb) == 0
                kill = jnp.logical_and(first, jnp.logical_not(tri))
                kv = kv_ref[pl.ds(start, 2 * BLOCK), :]
                k_ops = _kv_variants(kv[:, 0:KV_WIDTH])
                v_ops = _kv_variants(kv[:, KV_WIDTH:2 * KV_WIDTH])
                probs, sink_p, p16, qs, dos, delta, ds16 = {}, {}, {}, {}, {}, {}, {}
                for kvh in range(2):
                    pairs = (2 * kvh, 2 * kvh + 1)
                    qs[kvh] = jnp.concatenate([cur_ref[rows, LANES * p:LANES * (p + 1)] for p in pairs], axis=0)
                    for j in range(2):
                        sf = _mm_nt(qs[kvh], k_ops[kvh][j])
                        for r, p in enumerate(pairs):
                            hd = 2 * p + j
                            probs[hd], sink_p[hd] = _attn_probs(sf[BLOCK * r:BLOCK * (r + 1)], bias_ref[hd],
                                                                sink_ref[layer, hd], tri, kill)
                            p16[hd] = _unwrap16(probs[hd], tri)
                for kvh in range(2):
                    pairs = (2 * kvh, 2 * kvh + 1)
                    out = jnp.zeros((2 * BLOCK, LANES), F32)
                    for j in range(2):
                        out = out + _mm(jnp.concatenate([p16[2 * p + j] for p in pairs], axis=0), v_ops[kvh][j])
                    d_outs = []
                    for r, p in enumerate(pairs):
                        bz = cur_ref[rows, BZ_OFF + LANES * p:BZ_OFF + LANES * (p + 1)].astype(F32)
                        sg = _sigmoid(bz)
                        dyp = dyb_ref[rows, LANES * p:LANES * (p + 1)].astype(F32)
                        out_p = out[BLOCK * r:BLOCK * (r + 1)]
                        d_out = dyp * (bz * sg)
                        dqz_ref[rows, B_WIDTH + LANES * p:B_WIDTH + LANES * (p + 1)] = (
                            dyp * out_p * (sg * (1.0 + bz * (1.0 - sg)))).astype(BF16)
                        dod = d_out * out_p
                        delta[2 * p] = jnp.sum(jnp.where(low, dod, 0.0), axis=-1, keepdims=True)
                        delta[2 * p + 1] = jnp.sum(jnp.where(low, 0.0, dod), axis=-1, keepdims=True)
                        d_outs.append(d_out.astype(BF16))
                    dos[kvh] = jnp.concatenate(d_outs, axis=0)
                for kvh in range(2):
                    pairs = (2 * kvh, 2 * kvh + 1)
                    for j in range(2):
                        dpf = _mm_nt(dos[kvh], v_ops[kvh][j])
                        for r, p in enumerate(pairs):
                            hd = 2 * p + j
                            ds = probs[hd] * (_wrap(dpf[BLOCK * r:BLOCK * (r + 1)], tri) - delta[hd])
                            dsink_ref[hd:hd + 1, :] += jnp.sum(-sink_p[hd] * delta[hd], axis=0, keepdims=True)
                            dbias_ref[hd] += ds
                            ds16[hd] = _unwrap16(ds, tri)
                dk_acc = [[None, None], [None, None]]
                dv_acc = [[None, None], [None, None]]
                for kvh in range(2):
                    pairs = (2 * kvh, 2 * kvh + 1)
                    dq = jnp.zeros((2 * BLOCK, LANES), F32)
                    for j in range(2):
                        dss = jnp.concatenate([ds16[2 * p + j] for p in pairs], axis=0)
                        pss = jnp.concatenate([p16[2 * p + j] for p in pairs], axis=0)
                        dq = dq + _mm(dss, k_ops[kvh][j])
                        dk_acc[kvh][j] = _mm_tn(dss, qs[kvh])
                        dv_acc[kvh][j] = _mm_tn(pss, dos[kvh])
                    for r, p in enumerate(pairs):
                        dqz_ref[rows, LANES * p:LANES * (p + 1)] = (dq[BLOCK * r:BLOCK * (r + 1)] * SCALE).astype(BF16)

                def fold(acc):
                    return jnp.where(low_kv,
                                     acc[0][0] + pltpu.roll(acc[0][1], HEAD_DIM, 1),
                                     pltpu.roll(acc[1][0], HEAD_DIM, 1) + acc[1][1])

                acc_ref[pl.ds(start, 2 * BLOCK), :] += jnp.concatenate(
                    [fold(dk_acc) * SCALE, fold(dv_acc)], axis=1)
                return carry

            lax.fori_loop(0, per_tile, block, 0)
            dkv_ref[...] = acc_ref[0:tq, :].astype(BF16)
            acc_ref[0:BLOCK, :] = acc_ref[tq:tq + BLOCK, :]

        @pl.when(t == nt)
        def _():
            dkv_ref[0:BLOCK, :] = acc_ref[0:BLOCK, :].astype(BF16)
            dkv_ref[BLOCK:, :] = jnp.zeros((tq - BLOCK, 2 * KV_WIDTH), BF16)

    def cur_map(t):
        return (jnp.minimum(t, nt - 1), 0)

    def prev_map(t):
        return (jnp.maximum(jnp.minimum(t, nt - 1) * per_tile - 1, 0), K_OFF // (2 * KV_WIDTH))

    blocks = [_nbytes((tq, PB_WIDTH), BF16), _nbytes((BLOCK, 2 * KV_WIDTH), BF16), _nbytes((tq, B_WIDTH), BF16),
              3 * _nbytes((B_HEADS, BLOCK, BLOCK), F32), _nbytes((tq, qz_width), BF16),
              _nbytes((tq, 2 * KV_WIDTH), BF16), _nbytes((B_HEADS, LANES), F32)]
    scratch = _nbytes((tq + BLOCK, 2 * KV_WIDTH), BF16) + _nbytes((tq + BLOCK, 2 * KV_WIDTH), F32)
    return pl.pallas_call(
        body, name=f"attn_bwd_{layer}", grid=(nt + 1,),
        in_specs=[pl.BlockSpec(memory_space=pltpu.SMEM),
                  pl.BlockSpec((tq, PB_WIDTH), cur_map),
                  pl.BlockSpec((BLOCK, 2 * KV_WIDTH), prev_map),
                  pl.BlockSpec((tq, B_WIDTH), cur_map),
                  pl.BlockSpec((B_HEADS, BLOCK, BLOCK), lambda t: (0, 0, 0)),
                  pl.BlockSpec((B_HEADS, BLOCK, BLOCK), lambda t: (0, 0, 0))],
        out_specs=[pl.BlockSpec((tq, qz_width), cur_map),
                   pl.BlockSpec((tq, 2 * KV_WIDTH), lambda t: (t, 0)),
                   pl.BlockSpec((B_HEADS, BLOCK, BLOCK), lambda t: (0, 0, 0)),
                   pl.BlockSpec((B_HEADS, LANES), lambda t: (0, 0))],
        out_shape=[pltpu.HBM((tokens, qz_width), BF16),
                   pltpu.HBM((tokens + tq, 2 * KV_WIDTH), BF16),
                   pltpu.HBM((B_HEADS, BLOCK, BLOCK), F32),
                   pltpu.HBM((B_HEADS, LANES), F32)],
        scratch_shapes=[pltpu.VMEM((tq + BLOCK, 2 * KV_WIDTH), BF16),
                        pltpu.VMEM((tq + BLOCK, 2 * KV_WIDTH), F32)],
        compiler_params=pltpu.CompilerParams(dimension_semantics=("arbitrary",),
                                             vmem_limit_bytes=_vmem_limit(blocks, scratch)),
    )(sinks, _hbm(pb), _hbm(pb), _hbm(dyb), _hbm(bias), _hbm(dbias_in))


def _inproj_bwd_dx(da, dqz, dkv, w, x2, g, pre_g3, layer):
    tokens = x2.shape[0]
    tm = _token_tile(tokens)

    def body(da_ref, dqz_ref, dkv_ref, w_ref, x_ref, g_ref, pg_ref, gn_ref, dpg_ref):
        @pl.when(pl.program_id(0) == 0)
        def _():
            dpg_ref[...] = jnp.zeros_like(dpg_ref)

        dh = _mm(da_ref[...], w_ref[0:PA_WIDTH, :])
        dh += _mm(dqz_ref[:, 0:B_WIDTH], w_ref[PA_WIDTH:PA_WIDTH + B_WIDTH, :])
        dh += _mm(dkv_ref[...], w_ref[PA_WIDTH + K_OFF:PA_WIDTH + BZ_OFF, :])
        dh += _mm(dqz_ref[:, B_WIDTH:2 * B_WIDTH], w_ref[PA_WIDTH + BZ_OFF:IN_WIDTH, :])
        x = x_ref[...]
        r = lax.rsqrt(jnp.mean(x * x, axis=-1, keepdims=True) + NORM_EPS)
        xhat = x * r
        dhg = dh * pg_ref[...]
        dpg_ref[...] += jnp.sum(dh * xhat, axis=0, keepdims=True)
        gn_ref[...] = g_ref[...] + r * (dhg - xhat * jnp.mean(dhg * xhat, axis=-1, keepdims=True))

    blocks = [_nbytes((tm, PA_WIDTH), BF16), _nbytes((tm, 2 * B_WIDTH), BF16), _nbytes((tm, 2 * KV_WIDTH), BF16),
              _nbytes((D_MODEL, IN_WIDTH), BF16), 3 * _nbytes((tm, D_MODEL), F32)]
    return pl.pallas_call(
        body, name=f"inproj_bwd_dx_{layer}", grid=(tokens // tm,),
        in_specs=[pl.BlockSpec((tm, PA_WIDTH), lambda i: (i, 0)),
                  pl.BlockSpec((tm, 2 * B_WIDTH), lambda i: (i, 0)),
                  pl.BlockSpec((pl.Element(tm), pl.Element(2 * KV_WIDTH)), lambda i: (pl.multiple_of(i * tm + BLOCK, BLOCK), 0)),
                  pl.BlockSpec((IN_WIDTH, D_MODEL), lambda i: (0, 0)),
                  pl.BlockSpec((tm, D_MODEL), lambda i: (i, 0)),
                  pl.BlockSpec((tm, D_MODEL), lambda i: (i, 0)),
                  pl.BlockSpec((None, 1, D_MODEL), lambda i: (layer, 0, 0))],
        out_specs=[pl.BlockSpec((tm, D_MODEL), lambda i: (i, 0)),
                   pl.BlockSpec((1, D_MODEL), lambda i: (0, 0))],
        out_shape=[pltpu.HBM((tokens,D_MODEL), F32),
                   pltpu.HBM((1,D_MODEL), F32)],
        compiler_params=pltpu.CompilerParams(dimension_semantics=("arbitrary",),
                                             vmem_limit_bytes=_vmem_limit(blocks)),
    )(_hbm(da), _hbm(dqz), _hbm(dkv), _hbm(w), _hbm(x2), _hbm(g), pre_g3)


def _inproj_bwd_dw(h, da, dqz, dkv, layer):
    tokens = h.shape[0]
    tm = _token_tile(tokens)
    steps = tokens // tm

    def body(h_ref, da_ref, dqz_ref, dkv_ref, dw_ref, acc_ref):
        i = pl.program_id(0)

        @pl.when(i == 0)
        def _():
            acc_ref[...] = jnp.zeros_like(acc_ref)

        hv = h_ref[...]
        acc_ref[:, 0:PA_WIDTH] += _mm_tn(hv, da_ref[...])
        acc_ref[:, PA_WIDTH:PA_WIDTH + B_WIDTH] += _mm_tn(hv, dqz_ref[:, 0:B_WIDTH])
        acc_ref[:, PA_WIDTH + K_OFF:PA_WIDTH + BZ_OFF] += _mm_tn(hv, dkv_ref[...])
        acc_ref[:, PA_WIDTH + BZ_OFF:IN_WIDTH] += _mm_tn(hv, dqz_ref[:, B_WIDTH:2 * B_WIDTH])

        @pl.when(i == steps - 1)
        def _():
            for c in range(IN_WIDTH // LANES):
                cols = slice(LANES * c, LANES * (c + 1))
                dw_ref[cols, :] = acc_ref[:, cols].T.astype(BF16)

    blocks = [_nbytes((tm, D_MODEL), BF16), _nbytes((tm, PA_WIDTH), BF16), _nbytes((tm, 2 * B_WIDTH), BF16),
              _nbytes((tm, 2 * KV_WIDTH), BF16), _nbytes((D_MODEL, IN_WIDTH), BF16)]
    return pl.pallas_call(
        body, name=f"inproj_bwd_dw_{layer}", grid=(steps,),
        in_specs=[pl.BlockSpec((tm, D_MODEL), lambda i: (i, 0)),
                  pl.BlockSpec((tm, PA_WIDTH), lambda i: (i, 0)),
                  pl.BlockSpec((tm, 2 * B_WIDTH), lambda i: (i, 0)),
                  pl.BlockSpec((pl.Element(tm), pl.Element(2 * KV_WIDTH)), lambda i: (pl.multiple_of(i * tm + BLOCK, BLOCK), 0))],
        out_specs=pl.BlockSpec((IN_WIDTH, D_MODEL), lambda i: (0, 0)),
        out_shape=pltpu.HBM((IN_WIDTH, D_MODEL), BF16),
        scratch_shapes=[pltpu.VMEM((D_MODEL, IN_WIDTH), F32)],
        compiler_params=pltpu.CompilerParams(dimension_semantics=("arbitrary",),
                                             vmem_limit_bytes=_vmem_limit(blocks, _nbytes((D_MODEL, IN_WIDTH), F32))),
    )(_hbm(h), _hbm(da), _hbm(dqz), _hbm(dkv))


def _rel_bias_grad(dbias, col_bucket, flip):
    def body(db_ref, cb_ref, flip_ref, out_ref):
        cb = cb_ref[...]
        anti = flip_ref[...]
        sums = []
        for h in range(B_HEADS):
            x = db_ref[h]
            hi = x.astype(BF16)
            rest = x - hi.astype(F32)
            mid = rest.astype(BF16)
            low = (rest - mid.astype(F32)).astype(BF16)
            reversed_x = _mm(hi, anti) + _mm(mid, anti) + _mm(low, anti)
            rolled = pltpu.roll(reversed_x, 0, 1, stride=1, stride_axis=0)
            sums.append(jnp.sum(rolled, axis=0, keepdims=True))
        per_dist = jnp.concatenate(sums, axis=0)
        lane = lax.broadcasted_iota(jnp.int32, (B_HEADS, LANES), 1)
        out = jnp.zeros((B_HEADS, LANES), F32)
        for b in range(REL_BUCKETS):
            s = jnp.sum(jnp.where(cb == b, per_dist, 0.0), axis=-1, keepdims=True)
            out = jnp.where(lane == b, s, out)
        out_ref[...] = out

    vm = pl.BlockSpec(memory_space=pltpu.VMEM)
    return pl.pallas_call(
        body, name="rel_bias_grad",
        out_shape=jax.ShapeDtypeStruct((B_HEADS, LANES), F32),
        in_specs=[vm, vm, vm], out_specs=vm,
    )(dbias, col_bucket, flip)


def _reduce_adamw(slots, w, m, v, name):
    rows, cols = w.shape
    tr = _row_tile(rows)
    c1 = 1.0 / (1.0 - ADAM_B1 ** ADAM_STEP)
    c2 = 1.0 / (1.0 - ADAM_B2 ** ADAM_STEP)

    def body(s_ref, w_ref, m_ref, v_ref, g_ref, d_ref, nm_ref, nv_ref):
        g = s_ref[0].astype(F32)
        for i in range(1, N_DEV):
            g = g + s_ref[i].astype(F32)
        nm = ADAM_B1 * m_ref[...] + (1.0 - ADAM_B1) * g
        nv = ADAM_B2 * v_ref[...] + (1.0 - ADAM_B2) * (g * g)
        g_ref[...] = g
        nm_ref[...] = nm
        nv_ref[...] = nv
        d_ref[...] = -ADAM_LR * ((nm * c1) / (jnp.sqrt(nv * c2) + ADAM_EPS) + ADAM_WD * w_ref[...])

    blocks = [_nbytes((N_DEV, tr, cols), slots.dtype), 7 * _nbytes((tr, cols), F32)]
    tile = pl.BlockSpec((tr, cols), lambda i: (i, 0))
    return pl.pallas_call(
        body, name=name, grid=(rows // tr,),
        in_specs=[pl.BlockSpec((N_DEV, tr, cols), lambda i: (0, i, 0)), tile, tile, tile],
        out_specs=[tile] * 4,
        out_shape=[pltpu.HBM((rows, cols), F32)] * 4,
        compiler_params=pltpu.CompilerParams(dimension_semantics=("parallel",),
                                             vmem_limit_bytes=_vmem_limit(blocks)),
    )(_hbm(slots), _hbm(w), _hbm(m), _hbm(v))


_SMALL = ("pre_norm_g","post_norm_g", "ln_v_g", "ln_v_b", "b_spatial", "sinks", "rel_bias")


def _pack_small(parts):
    slabs = []
    for name in _SMALL:
        flat = parts[name].astype(F32).reshape(-1)
        pad = (-flat.shape[0]) % (8 * LANES)
        slabs.append(jnp.pad(flat, (0, pad)).reshape(-1, LANES))
    return jnp.concatenate(slabs, axis=0)


def _unpack_small(slab, shapes):
    out, row = {}, 0
    for name in _SMALL:
        size = int(np.prod(shapes[name]))
        rows = -(-size // (8 * LANES)) * 8
        out[name] = slab[row:row + rows].reshape(-1)[:size].reshape(shapes[name])
        row += rows
    return out


def kernel(x, pre_norm_g, w_in, ln_v_g, ln_v_b, w_spatial, b_spatial, sinks, rel_bias, w_out, post_norm_g, loss_target, m_pre_norm_g, m_w_in, m_ln_v_g, m_ln_v_b, m_w_spatial, m_b_spatial, m_sinks, m_rel_bias, m_w_out, m_post_norm_g, v_pre_norm_g, v_w_in, v_ln_v_g, v_ln_v_b, v_w_spatial, v_b_spatial, v_sinks, v_rel_bias, v_w_out, v_post_norm_g):
    b_loc, seq, _ = x.shape
    tokens = b_loc * seq
    depth = w_in.shape[0]
    in_shard = w_in.shape[2]
    out_shard = w_out.shape[1]
    assert in_shard * N_DEV == IN_WIDTH and out_shard * N_DEV == D_MODEL and seq % BLOCK == 0

    me = _slot(lax.axis_index("x"), lax.axis_index("y"), lax.axis_index("c"))
    w_in_t, m_w_in_t, v_w_in_t = (jnp.swapaxes(a, 1, 2) for a in (w_in, m_w_in, v_w_in))
    w_in16, w_out16 = w_in_t.astype(BF16), w_out.astype(BF16)

    def gather_start(layer, not_before):
        zones = [lax.empty((N_DEV, in_shard, D_MODEL), BF16), lax.empty((N_DEV, out_shard, D_MODEL), BF16)]
        sems, sends, zones, pre_g = _exchange_start(
            [w_in16[layer], w_out16[layer]], zones, [0, 0], f"weights_send_{layer}",
            pre_norm_g.reshape(depth, 1, D_MODEL), after=not_before)
        return (sems, sends, [0, 0]), zones, pre_g

    def full_weights(gin, gout):
        return gin.reshape(IN_WIDTH, D_MODEL), gout.reshape(D_MODEL, D_MODEL)

    def gather_wait(layer, started, zones, after):
        gin, gout = _exchange_wait([started], zones, after, f"weights_wait_{layer}")
        gin = lax.dynamic_update_index_in_dim(gin, w_in16[layer], me, 0)
        gout = lax.dynamic_update_index_in_dim(gout, w_out16[layer], me, 0)
        return full_weights(gin, gout)

    causal = jnp.tril(jnp.ones((CHUNK, CHUNK), dtype=bool))
    wm = jnp.where(causal, w_spatial, 0.0).astype(BF16)
    wmt = _hbm(jnp.swapaxes(wm, -1, -2))
    wm = _hbm(wm)
    bsb = _hbm(jnp.repeat(jnp.swapaxes(b_spatial, -1, -2), HEAD_DIM, axis=-1))
    pre_g3 = _hbm(pre_norm_g.reshape(depth, 1, D_MODEL))
    post_g3 = _hbm(post_norm_g.reshape(depth, 1, D_MODEL))
    ln_g3 = _hbm(ln_v_g.reshape(depth, 1, A_WIDTH))
    ln_b3 = _hbm(ln_v_b.reshape(depth, 1, A_WIDTH))
    bucket = jnp.asarray(_bucket_table())
    bias = _bias_table(rel_bias, bucket)

    xs, saved, weights = [x.reshape(tokens, D_MODEL)], [], []
    pending = None
    for layer in range(depth):
        if layer == 0:
            w, wo = full_weights(*_all_gather([w_in16[0], w_out16[0]], "weights_gather_0"))
        else:
            w, wo = gather_wait(layer, pending[0], pending[1], xs[-1])
        pre_g_fwd = pre_g3
        if layer + 1 < depth:
            pending = gather_start(layer + 1, w)
            pre_g_fwd = _hbm(pending[2])
        h, pa, pb = _inproj_fwd(xs[-1], pre_g_fwd, w, layer)
        ya = _gmlp_fwd(pa, ln_g3, ln_b3, wm, bsb, layer)
        yb = _attn_fwd(pb, bias, sinks, b_loc, layer)
        if layer + 1 < depth:
            y, x_next = _outproj_fwd(ya, yb, wo, xs[-1], post_g3, layer)
            xs.append(x_next)
        else:
            y, g, loss_part = _outproj_fwd(ya, yb, wo, xs[-1], post_g3, layer,
                                           target=loss_target.reshape(tokens, D_MODEL))
        saved.append((h, pa, pb, ya, yb, y))
        weights.append((w, wo))
    loss = lax.psum(loss_part[0, 0], ("x", "y", "c"))

    grads = {name: [None] * depth for name in ("pre_norm_g","post_norm_g", "ln_v_g", "ln_v_b",
                                               "b_spatial", "sinks", "dbias")}
    zone_in = lax.empty((N_DEV, depth * in_shard, D_MODEL), BF16)
    zone_out = lax.empty((N_DEV, depth * out_shard, D_MODEL), BF16)
    zone_ws = lax.empty((N_DEV, depth * A_GROUPS * CHUNK, CHUNK), F32)
    started_in, started_out, own_in, own_out = [], [], [None] * depth, [None] * depth
    started_ws, own_ws = [], [None] * depth
    dbias = jnp.zeros((B_HEADS, BLOCK, BLOCK), F32)
    for layer in reversed(range(depth)):
        h, pa, pb, ya, yb, y = saved[layer]
        w, wo = weights[layer]
        dya, dyb, dwo, dpost = _outproj_bwd(g, y, ya, yb, wo, post_g3, layer)
        send_out = dwo.reshape(N_DEV, out_shard, D_MODEL)
        own_out[layer] = lax.dynamic_index_in_dim(send_out, me, 0, keepdims=False)
        sems, sends, (zone_out,), ln_g_bwd = _exchange_start(
            [send_out], [zone_out], [layer * out_shard], f"grads_send_out_{layer}", ln_v_g.reshape(depth, 1, A_WIDTH))
        started_out.append((sems, sends, [layer * out_shard]))
        da, dws, dbs, dlg, dlb = _gmlp_bwd(pa, dya, _hbm(ln_g_bwd), ln_b3, wm, wmt, bsb, layer)
        own_ws[layer] = dws.reshape(A_GROUPS * CHUNK, CHUNK)
        sems, sends, (zone_ws,), pre_g_bwd = _exchange_start(
            [own_ws[layer]], [zone_ws], [layer * A_GROUPS * CHUNK], f"grads_send_ws_{layer}",
            pre_norm_g.reshape(depth, 1, D_MODEL))
        started_ws.append((sems, sends, [layer * A_GROUPS * CHUNK]))
        dqz, dkv_shifted, dbias, dsink = _attn_bwd(pb, dyb, bias, dbias, sinks, b_loc, layer)
        dkv = dkv_shifted
        dw = _inproj_bwd_dw(h, da, dqz, dkv, layer)
        send_in = dw.reshape(N_DEV, in_shard, D_MODEL)
        own_in[layer] = lax.dynamic_index_in_dim(send_in, me, 0, keepdims=False)
        sems, sends, (zone_in,), pre_g_bwd = _exchange_start(
            [send_in], [zone_in], [layer * in_shard], f"grads_send_in_{layer}", pre_g_bwd)
        started_in.append((sems, sends, [layer * in_shard]))
        g, dpre = _inproj_bwd_dx(da, dqz, dkv, w, xs[layer], g, _hbm(pre_g_bwd), layer)
        grads["b_spatial"][layer] = dbs[:, :A_GROUPS].T
        grads["ln_v_g"][layer] = dlg[0]
        grads["ln_v_b"][layer] = dlb[0]
        grads["sinks"][layer] = dsink[:, 0]
        grads["pre_norm_g"][layer] = dpre[0]
        grads["post_norm_g"][layer] = dpost[0]
    grad_x = g.reshape(x.shape)
    col_bucket = jnp.asarray(np.broadcast_to(_bucket_table()[0:1, ::-1], (B_HEADS, BLOCK)))
    flip = jnp.asarray(np.eye(BLOCK, dtype=np.float32)[::-1], dtype=BF16)
    drel = _rel_bias_grad(dbias, col_bucket, flip)[:, :REL_BUCKETS].T

    (recv_out,) = _exchange_wait(started_out, [zone_out], g, "grads_wait_out")
    recv_out = lax.dynamic_update_index_in_dim(recv_out, jnp.concatenate(own_out, axis=0), me, 0)
    res_out = _reduce_adamw(recv_out, w_out.reshape(-1, D_MODEL), m_w_out.reshape(-1, D_MODEL),
                            v_w_out.reshape(-1, D_MODEL), "adamw_w_out")
    res_out = [r.reshape(w_out.shape) for r in res_out]

    (recv_ws,) = _exchange_wait(started_ws, [zone_ws], g, "grads_wait_ws")
    recv_ws = lax.dynamic_update_index_in_dim(recv_ws, jnp.concatenate(own_ws, axis=0), me, 0)
    res_ws = _reduce_adamw(recv_ws, w_spatial.reshape(-1, CHUNK), m_w_spatial.reshape(-1, CHUNK),
                           v_w_spatial.reshape(-1, CHUNK), "adamw_w_spatial")
    res_ws = [r.reshape(w_spatial.shape) for r in res_ws]
    small_w = dict(pre_norm_g=pre_norm_g, post_norm_g=post_norm_g, ln_v_g=ln_v_g,
                   ln_v_b=ln_v_b, b_spatial=b_spatial, sinks=sinks, rel_bias=rel_bias)
    small_m = dict(pre_norm_g=m_pre_norm_g, post_norm_g=m_post_norm_g, ln_v_g=m_ln_v_g,
                   ln_v_b=m_ln_v_b, b_spatial=m_b_spatial, sinks=m_sinks, rel_bias=m_rel_bias)
    small_v = dict(pre_norm_g=v_pre_norm_g, post_norm_g=v_post_norm_g, ln_v_g=v_ln_v_g,
                   ln_v_b=v_ln_v_b, b_spatial=v_b_spatial, sinks=v_sinks, rel_bias=v_rel_bias)
    small_g = {name: jnp.stack(grads[name]) for name in _SMALL if name != "rel_bias"}
    small_g["rel_bias"] = drel
    shapes = {name: small_w[name].shape for name in _SMALL}
    (slots,) = _all_gather([_pack_small(small_g)], "small_grads_all_gather")
    res_small = _reduce_adamw(slots, _pack_small(small_w), _pack_small(small_m), _pack_small(small_v), "adamw_small")

    (recv_in,) = _exchange_wait(started_in, [zone_in], res_small[0], "grads_wait_in")
    recv_in = lax.dynamic_update_index_in_dim(recv_in, jnp.concatenate(own_in, axis=0), me, 0)
    res_in = _reduce_adamw(recv_in, w_in_t.reshape(-1, D_MODEL), m_w_in_t.reshape(-1, D_MODEL),
                           v_w_in_t.reshape(-1, D_MODEL), "adamw_w_in")
    res_in = [jnp.swapaxes(r.reshape(w_in_t.shape), 1, 2) for r in res_in]
    res_small = [_unpack_small(r, shapes) for r in res_small]

    order = ("pre_norm_g", "w_in", "ln_v_g", "ln_v_b", "w_spatial", "b_spatial", "sinks", "rel_bias", "w_out",
             "post_norm_g")
    outs = [loss, grad_x]
    for kind in range(4):
        for name in order:
            if name == "w_in":
                outs.append(res_in[kind])
            elif name == "w_out":
                outs.append(res_out[kind])
            elif name == "w_spatial":
                outs.append(res_ws[kind])
            else:
                outs.append(res_small[kind][name])
    return tuple(outs)
```

```python
import math

import numpy as np
import jax
import jax.numpy as jnp
from jax import lax
from jax.experimental import pallas as pl
from jax.experimental.pallas import tpu as pltpu

F32 = jnp.float32
BF16 = jnp.bfloat16

D_MODEL = 1024
A_WIDTH = 512
A_GROUPS = 8
CHUNK = 128
B_HEADS = 8
HEAD_DIM = 64
B_WIDTH = 512
KV_WIDTH = 128
BLOCK = 128
REL_BUCKETS = 32
REL_MAX_DIST = 128
NORM_EPS = 1e-6
PA_WIDTH = 3 * A_WIDTH
PB_WIDTH = 2 * B_WIDTH + 2 * KV_WIDTH
IN_WIDTH = PA_WIDTH + PB_WIDTH
K_OFF, V_OFF, BZ_OFF = B_WIDTH, B_WIDTH + KV_WIDTH, B_WIDTH + 2 * KV_WIDTH
SCALE = HEAD_DIM ** -0.5
NEG = -1e30
N_DEV = 8
LANES = 128

ADAM_LR = 0.001
ADAM_B1 = 0.9
ADAM_B2 = 0.999
ADAM_EPS = 1e-08
ADAM_WD = 0.01
ADAM_STEP = 10

V7X_VMEM_BYTES = 64 * 1024 * 1024
VMEM_TEMP_BYTES = 12 * 1024 * 1024
MESH = pl.DeviceIdType.MESH


def _vmem_limit(block_bytes, scratch_bytes=0):
    need = 2 * sum(block_bytes) + scratch_bytes + VMEM_TEMP_BYTES
    return int(min(need, V7X_VMEM_BYTES - 8 * 1024 * 1024))


def _nbytes(shape, dtype):
    return int(np.prod(shape)) * jnp.dtype(dtype).itemsize


def _token_tile(tokens):
    tile = min(512, tokens // 2)
    assert tokens % tile == 0 and tile % CHUNK == 0, tokens
    return tile


def _row_tile(rows, cap=512):
    best = 8
    for t in range(8, cap + 1, 8):
        if rows % t == 0:
            best = t
    assert rows % best == 0, rows
    return best


def _mm(a, b):
    return lax.dot_general(a, b, (((1,), (0,)), ((), ())), preferred_element_type=F32)


def _mm_nt(a, b):
    return lax.dot_general(a, b, (((1,), (1,)), ((), ())), preferred_element_type=F32)


def _mm_tn(a, b):
    return lax.dot_general(a, b, (((0,), (0,)), ((), ())), preferred_element_type=F32)


_GELU_C = math.sqrt(2.0 / math.pi)


_GELU_A = _GELU_C * 0.044715


def _gelu_parts(x):
    x2 = x * x
    t = jnp.tanh(x * (_GELU_C + _GELU_A * x2))
    return x2, t, 0.5 + 0.5 * t


def _gelu(x):
    return x * _gelu_parts(x)[2]


def _gelu_and_grad(x):
    x2, t, half_plus = _gelu_parts(x)
    grad = half_plus + (0.5 * x) * (1.0 - t * t) * (_GELU_C + (3.0 * _GELU_A) * x2)
    return x * half_plus, grad


def _sigmoid(x):
    return 0.5 + 0.5 * jnp.tanh(0.5 * x)


def _bucket_table():
    q = np.arange(BLOCK)[:, None]
    k = np.arange(BLOCK)[None, :]
    dist = np.where(k <= q, q - k, q + BLOCK - k)
    max_exact = REL_BUCKETS // 2
    safe = np.maximum(dist, 1).astype(np.float32)
    large = max_exact + (np.log(safe / np.float32(max_exact)) / np.float32(math.log(REL_MAX_DIST / max_exact))
                         * np.float32(REL_BUCKETS - max_exact)).astype(np.int32)
    large = np.minimum(large, REL_BUCKETS - 1)
    assert dist.min() >= 0 and dist.max() < BLOCK
    return np.where(dist < max_exact, dist, large).astype(np.int32)


def _hbm(x):
    return pltpu.with_memory_space_constraint(x, pltpu.HBM)


def _slot(px, py, pc):
    return 4 * px + 2 * py + pc


def _all_gather(arrs, name):
    n = len(arrs)

    def body(*refs):
        ins, outs = refs[:n], refs[n:2 * n]
        send_sems, recv_sems, local_sems = refs[2 * n:]
        x, y, c = lax.axis_index("x"), lax.axis_index("y"), lax.axis_index("c")
        me, sibling = (x, y, c), (x, y, 1 - c)
        chips = [(1 - x, y), (x, 1 - y), (1 - x, 1 - y)]

        def copy(a, k, block, to, src=None):
            dst = outs[a].at[_slot(*block)]
            return pltpu.make_async_remote_copy(
                src_ref=dst if src is None else src, dst_ref=dst,
                send_sem=send_sems.at[7 * a + k], recv_sem=recv_sems.at[7 * a + k],
                device_id=to, device_id_type=MESH)

        mine = [pltpu.make_async_copy(ins[a], outs[a].at[_slot(*me)], local_sems.at[a]) for a in range(n)]
        for cp in mine:
            cp.start()
        first = []
        for a in range(n):
            first.append(copy(a, 0, me, sibling, src=ins[a]))
            first += [copy(a, 1 + j, me, (*chip, c), src=ins[a]) for j, chip in enumerate(chips)]
        for cp in first:
            cp.start()
        passed = []
        for j, chip in enumerate(chips):
            for a in range(n):
                copy(a, 1 + j, (*chip, c), me).wait_recv()
                fwd = copy(a, 4 + j, (*chip, c), sibling)
                fwd.start()
                passed.append(fwd)
        for a in range(n):
            copy(a, 0, sibling, me).wait_recv()
            for j, chip in enumerate(chips):
                copy(a, 4 + j, (*chip, 1 - c), me).wait_recv()
        for cp in first + passed:
            cp.wait_send()
        for cp in mine:
            cp.wait()

    any_spec = pl.BlockSpec(memory_space=pl.ANY)
    return pl.pallas_call(
        body, name=name,
        out_shape=[jax.ShapeDtypeStruct((N_DEV,) + a.shape, a.dtype) for a in arrs],
        in_specs=[any_spec] * n, out_specs=[any_spec] * n,
        scratch_shapes=[pltpu.SemaphoreType.DMA((7 * n,)), pltpu.SemaphoreType.DMA((7 * n,)),
                        pltpu.SemaphoreType.DMA((n,))],
    )(*arrs)


_HBM_SPEC = pl.BlockSpec(memory_space=pltpu.HBM)
_SEM_SPEC = pl.BlockSpec(memory_space=pltpu.SEMAPHORE)
_DATAFLOW = pltpu.SideEffectType.DATAFLOW_SIDE_EFFECTING


def _peers():
    x, y, c = lax.axis_index("x"), lax.axis_index("y"), lax.axis_index("c")
    peers = []
    for k in range(1, N_DEV):
        fx, fy, fc = (k >> 2) & 1, (k >> 1) & 1, k & 1
        peers.append((1 - x if fx else x, 1 - y if fy else y, 1 - c if fc else c))
    return (x, y, c), peers


def _exchange_copy(send_ref, land_ref, row_off, src_slot, dst_slot, sems, idx, peer):
    rows = send_ref.shape[-2]
    src = send_ref.at[src_slot] if len(send_ref.shape) == 3 else send_ref
    return pltpu.make_async_remote_copy(
        src_ref=src, dst_ref=land_ref.at[dst_slot, pl.ds(row_off, rows), :],
        send_sem=sems[0].at[idx], recv_sem=sems[1].at[idx], device_id=peer, device_id_type=MESH)


def _exchange_start(sends, lands, row_offs, name, carry, after=None):
    n = len(sends)

    def body(*refs):
        ins, zones, carry_ref = refs[:n], refs[n:2 * n], refs[2 * n]
        first_out = 2 * n + 1 + (after is not None)
        sems = refs[first_out:first_out + 2]
        carry_out = refs[-1]
        me, peers = _peers()
        for a in range(n):
            for k, peer in enumerate(peers):
                _exchange_copy(ins[a], zones[a], row_offs[a], _slot(*peer), _slot(*me), sems, 7 * a + k, peer).start()
        carry_out[...] = carry_ref[...]

    arrays = [_hbm(a) for a in list(sends) + list(lands)]
    vmem = pl.BlockSpec(memory_space=pltpu.VMEM)
    out = pl.pallas_call(
        body, name=name,
        out_shape=(pltpu.SemaphoreType.DMA((7 * n,)), pltpu.SemaphoreType.DMA((7 * n,)),
                   *[pltpu.HBM(a.shape, a.dtype) for a in arrays], jax.ShapeDtypeStruct(carry.shape, carry.dtype)),
        in_specs=[_HBM_SPEC] * (2 * n) + [vmem] + ([pl.BlockSpec(memory_space=pl.ANY)] if after is not None else []),
        out_specs=(_SEM_SPEC, _SEM_SPEC, *[_HBM_SPEC] * (2 * n), vmem),
        input_output_aliases={i: 2 + i for i in range(2 * n)},
        compiler_params=pltpu.CompilerParams(has_side_effects=_DATAFLOW),
    )(*arrays, carry, *([after] if after is not None else []))
    return (out[0], out[1]), list(out[2:2 + n]), list(out[2 + n:2 + 2 * n]), out[-1]


def _exchange_wait(started, lands, after, name):
    n = len(lands)
    flat_sends = [s for _, sends, _ in started for s in sends]
    flat_sems = [s for sems, _, _ in started for s in sems]
    ns = len(flat_sends)

    def body(*refs):
        ins, zones = refs[:ns], refs[ns:ns + n]
        sem_refs = refs[ns + n:ns + n + len(flat_sems)]
        me, peers = _peers()
        pos = 0
        for call, (_, sends, row_offs) in enumerate(started):
            sems = sem_refs[2 * call:2 * call + 2]
            for a in range(len(sends)):
                for k, peer in enumerate(peers):
                    cp = _exchange_copy(ins[pos + a], zones[a], row_offs[a], _slot(*peer), _slot(*peer), sems,
                                        7 * a + k, peer)
                    cp.wait_send()
                    cp.wait_recv()
            pos += len(sends)

    arrays = list(flat_sends) + list(lands)
    out = pl.pallas_call(
        body, name=name,
        out_shape=tuple(pltpu.HBM(a.shape, a.dtype) for a in arrays),
        in_specs=[_HBM_SPEC] * len(arrays) + [_SEM_SPEC] * len(flat_sems) + [pl.BlockSpec(memory_space=pl.ANY)],
        out_specs=tuple([_HBM_SPEC] * len(arrays)),
        input_output_aliases={i: i for i in range(len(arrays))},
        compiler_params=pltpu.CompilerParams(has_side_effects=_DATAFLOW),
    )(*arrays, *flat_sems, after)
    return list(out[ns:])


def _bias_table(rel_bias, bucket):
    def body(rel_ref, bucket_ref, out_ref):
        bk = bucket_ref[...]
        for h in range(B_HEADS):
            def pick(b, acc, h=h):
                return jnp.where(bk == b, rel_ref[b, h], acc)
            out_ref[h] = lax.fori_loop(0, REL_BUCKETS, pick, jnp.zeros((BLOCK, BLOCK), F32))

    return pl.pallas_call(
        body, name="bias_table",
        out_shape=jax.ShapeDtypeStruct((B_HEADS, BLOCK, BLOCK), F32),
        in_specs=[pl.BlockSpec(memory_space=pltpu.SMEM), pl.BlockSpec(memory_space=pltpu.VMEM)],
        out_specs=pl.BlockSpec(memory_space=pltpu.VMEM),
    )(rel_bias, bucket)


def _inproj_gmlp_fwd(x2, pre_g3, w, ln_g3, ln_b3, wm, bsb, layer):
    tokens = x2.shape[0]
    tm = _token_tile(tokens)

    def body(x_ref, g_ref, w_ref, lg_ref, lb_ref, wm_ref, bsb_ref, h_ref, pa_ref, pb_ref, ya_ref):
        x = x_ref[...]
        r = lax.rsqrt(jnp.mean(x * x, axis=-1, keepdims=True) + NORM_EPS)
        h = (x * r * g_ref[...]).astype(BF16)
        h_ref[...] = h
        pa_ref[...] = _mm_nt(h, w_ref[0:PA_WIDTH, :]).astype(BF16)
        pb_ref[...] = _mm_nt(h, w_ref[PA_WIDTH:IN_WIDTH, :]).astype(BF16)
        for ci in range(tm // CHUNK):
            rows = slice(ci * CHUNK, (ci + 1) * CHUNK)
            _, _, pz, u, _, _, _, mixed, sg = _gmlp_forward_chunk(
                pa_ref[rows, :], lg_ref[...], lb_ref[...], wm_ref, bsb_ref[...])
            ya_ref[rows, :] = (u * mixed * (pz * sg)).astype(BF16)

    blocks = [_nbytes((tm, D_MODEL), F32), _nbytes((D_MODEL, IN_WIDTH), BF16),
              _nbytes((tm, D_MODEL), BF16), _nbytes((tm, PA_WIDTH), BF16), _nbytes((tm, PB_WIDTH), BF16),
              _nbytes((A_GROUPS, CHUNK, CHUNK), BF16), _nbytes((CHUNK, A_WIDTH), F32), _nbytes((tm, A_WIDTH), BF16)]
    return pl.pallas_call(
        body, name=f"inproj_gmlp_fwd_{layer}", grid=(tokens // tm,),
        in_specs=[pl.BlockSpec((tm, D_MODEL), lambda i: (i, 0)),
                  pl.BlockSpec((None, 1, D_MODEL), lambda i: (layer, 0, 0)),
                  pl.BlockSpec((IN_WIDTH, D_MODEL), lambda i: (0, 0)),
                  pl.BlockSpec((None, 1, A_WIDTH), lambda i: (layer, 0, 0)),
                  pl.BlockSpec((None, 1, A_WIDTH), lambda i: (layer, 0, 0)),
                  pl.BlockSpec((None, A_GROUPS, CHUNK, CHUNK), lambda i: (layer, 0, 0, 0)),
                  pl.BlockSpec((None, CHUNK, A_WIDTH), lambda i: (layer, 0, 0))],
        out_specs=[pl.BlockSpec((tm, D_MODEL), lambda i: (i, 0)),
                   pl.BlockSpec((tm, PA_WIDTH), lambda i: (i, 0)),
                   pl.BlockSpec((tm, PB_WIDTH), lambda i: (i, 0)),
                   pl.BlockSpec((tm, A_WIDTH), lambda i: (i, 0))],
        out_shape=[pltpu.HBM((tokens, D_MODEL), BF16),
                   pltpu.HBM((tokens, PA_WIDTH), BF16),
                   pltpu.HBM((tokens, PB_WIDTH), BF16),
                   pltpu.HBM((tokens, A_WIDTH), BF16)],
        compiler_params=pltpu.CompilerParams(dimension_semantics=("parallel",),
                                             vmem_limit_bytes=_vmem_limit(blocks)),
    )(_hbm(x2), pre_g3, _hbm(w), ln_g3, ln_b3, wm, bsb)


def _gmlp_forward_chunk(pa, ln_g, ln_b, wm_ref, bsb):
    pu = pa[:, 0:A_WIDTH].astype(F32)
    pv = pa[:, A_WIDTH:2 * A_WIDTH].astype(F32)
    pz = pa[:, 2 * A_WIDTH:3 * A_WIDTH].astype(F32)
    u, gu = _gelu_and_grad(pu)
    vv, gv = _gelu_and_grad(pv)
    mu = jnp.mean(vv, axis=-1, keepdims=True)
    xc = vv - mu
    rstd = lax.rsqrt(jnp.mean(xc * xc, axis=-1, keepdims=True) + NORM_EPS)
    vhat = xc * rstd
    vnb = (vhat * ln_g + ln_b).astype(BF16)
    low = lax.broadcasted_iota(jnp.int32, (CHUNK, LANES), 1) < HEAD_DIM
    parts = []
    for p in range(A_GROUPS // 2):
        vp = vnb[:, LANES * p:LANES * (p + 1)]
        parts.append(jnp.where(low, _mm(wm_ref[2 * p], vp), _mm(wm_ref[2 * p + 1], vp)))
    mixed = jnp.concatenate(parts, axis=1) + bsb
    sg = _sigmoid(pz)
    return gu, gv, pz, u, vhat, rstd, vnb, mixed, sg


def _kv_variants(kv):
    t = kv.astype(F32)
    rolled = pltpu.roll(t, HEAD_DIM, 1)
    low = lax.broadcasted_iota(jnp.int32, t.shape, 1) < HEAD_DIM
    zero = jnp.zeros_like(t)
    head0 = (jnp.where(low, t, zero).astype(BF16), jnp.where(low, zero, rolled).astype(BF16))
    head1 = (jnp.where(low, rolled, zero).astype(BF16), jnp.where(low, zero, t).astype(BF16))
    return head0, head1


def _band_masks():
    row = lax.broadcasted_iota(jnp.int32, (BLOCK, BLOCK), 0)
    col = lax.broadcasted_iota(jnp.int32, (BLOCK, BLOCK), 1)
    return col <= row


def _wrap(full, tri):
    return jnp.where(tri, full[:, BLOCK:2 * BLOCK], full[:, 0:BLOCK])


def _attn_probs(sf, bias_h, sink, tri, kill):
    s = _wrap(sf, tri) * SCALE + bias_h
    s = jnp.where(kill, NEG, s)
    m = jnp.maximum(jnp.max(s, axis=-1, keepdims=True), sink)
    e = jnp.exp(s - m)
    es = jnp.exp(sink - m)
    inv = 1.0 / (jnp.sum(e, axis=-1, keepdims=True) + es)
    return e * inv, es * inv


def _unwrap16(p, tri):
    zero = jnp.zeros_like(p)
    return jnp.concatenate([jnp.where(tri, zero, p), jnp.where(tri, p, zero)], axis=1).astype(BF16)


def _fill_kv(kv_ref, prev_ref, cur_ref):
    kv_ref[0:BLOCK, :] = prev_ref[...]
    kv_ref[BLOCK:, :] = cur_ref[:, K_OFF:K_OFF + 2 * KV_WIDTH]


def _attn_fwd(pb, bias, sinks, b_loc, layer):
    tokens = pb.shape[0]
    nb = tokens // b_loc // BLOCK
    tq = _token_tile(tokens)
    per_tile = tq // BLOCK

    def body(sink_ref, cur_ref, prev_ref, bias_ref, yb_ref, kv_ref):
        t = pl.program_id(0)
        _fill_kv(kv_ref, prev_ref, cur_ref)
        tri = _band_masks()

        def block(i, carry):
            start = pl.multiple_of(i * BLOCK, BLOCK)
            rows = pl.ds(start, BLOCK)
            first = lax.rem(t * per_tile + i, nb) == 0
            kill = jnp.logical_and(first, jnp.logical_not(tri))
            kv = kv_ref[pl.ds(start, 2 * BLOCK), :]
            k_ops = _kv_variants(kv[:, 0:KV_WIDTH])
            v_ops = _kv_variants(kv[:, KV_WIDTH:2 * KV_WIDTH])
            p16 = {}
            for kvh in range(2):
                pairs = (2 * kvh, 2 * kvh + 1)
                qs = jnp.concatenate([cur_ref[rows, LANES * p:LANES * (p + 1)] for p in pairs], axis=0)
                for j in range(2):
                    sf = _mm_nt(qs, k_ops[kvh][j])
                    for r, p in enumerate(pairs):
                        hd = 2 * p + j
                        probs, _ = _attn_probs(sf[BLOCK * r:BLOCK * (r + 1)], bias_ref[hd], sink_ref[layer, hd],
                                               tri, kill)
                        p16[hd] = _unwrap16(probs, tri)
            for kvh in range(2):
                pairs = (2 * kvh, 2 * kvh + 1)
                out = jnp.zeros((2 * BLOCK, LANES), F32)
                for j in range(2):
                    out = out + _mm(jnp.concatenate([p16[2 * p + j] for p in pairs], axis=0), v_ops[kvh][j])
                for r, p in enumerate(pairs):
                    bz = cur_ref[rows, BZ_OFF + LANES * p:BZ_OFF + LANES * (p + 1)].astype(F32)
                    yb_ref[rows, LANES * p:LANES * (p + 1)] = (
                        out[BLOCK * r:BLOCK * (r + 1)] * (bz * _sigmoid(bz))).astype(BF16)
            return carry

        lax.fori_loop(0, per_tile, block, 0)

    blocks = [_nbytes((tq, PB_WIDTH), BF16), _nbytes((BLOCK, 2 * KV_WIDTH), BF16),
              _nbytes((B_HEADS, BLOCK, BLOCK), F32), _nbytes((tq, B_WIDTH), BF16)]
    scratch = _nbytes((tq + BLOCK, 2 * KV_WIDTH), BF16)
    return pl.pallas_call(
        body, name=f"attn_fwd_{layer}", grid=(tokens // tq,),
        in_specs=[pl.BlockSpec(memory_space=pltpu.SMEM),
                  pl.BlockSpec((tq, PB_WIDTH), lambda t: (t, 0)),
                  pl.BlockSpec((BLOCK, 2 * KV_WIDTH),
                               lambda t: (jnp.maximum(t * per_tile - 1, 0), K_OFF // (2 * KV_WIDTH))),
                  pl.BlockSpec((B_HEADS, BLOCK, BLOCK), lambda t: (0, 0, 0))],
        out_specs=pl.BlockSpec((tq, B_WIDTH), lambda t: (t, 0)),
        out_shape=pltpu.HBM((tokens, B_WIDTH), BF16),
        scratch_shapes=[pltpu.VMEM((tq + BLOCK, 2 * KV_WIDTH), BF16)],
        compiler_params=pltpu.CompilerParams(dimension_semantics=("parallel",),
                                             vmem_limit_bytes=_vmem_limit(blocks, scratch)),
    )(sinks, _hbm(pb), _hbm(pb), _hbm(bias))


def _outproj_fwd(ya, yb, wo, x2, post_g3, layer, target=None):
    tokens = x2.shape[0]
    tm = _token_tile(tokens)
    steps = tokens // tm
    with_loss = target is not None

    def body(*refs):
        ya_ref, yb_ref, woa_ref, wob_ref, x_ref, g_ref = refs[:6]
        y = _mm(ya_ref[...], woa_ref[...]) + _mm(yb_ref[...], wob_ref[...])
        r = lax.rsqrt(jnp.mean(y * y, axis=-1, keepdims=True) + NORM_EPS)
        x_next = x_ref[...] + y * r * g_ref[...]
        if not with_loss:
            y_ref, xn_ref = refs[6:]
            y_ref[...] = y.astype(BF16)
            xn_ref[...] = x_next
            return
        t_ref, y_ref, grad_ref, loss_ref, acc_ref = refs[6:]
        i = pl.program_id(0)

        @pl.when(i == 0)
        def _():
            acc_ref[...] = jnp.zeros_like(acc_ref)

        y_ref[...] = y.astype(BF16)
        err = x_next - t_ref[...]
        grad_ref[...] = err * (1.0 / D_MODEL)
        acc_ref[...] += jnp.sum(err * err, axis=0, keepdims=True)

        @pl.when(i == steps - 1)
        def _():
            total = jnp.sum(acc_ref[...], axis=-1, keepdims=True) * (0.5 / D_MODEL)
            loss_ref[...] = jnp.broadcast_to(total, loss_ref.shape)

    half = D_MODEL // 2
    tile = pl.BlockSpec((tm, D_MODEL), lambda i: (i, 0))
    blocks = [2 * _nbytes((tm, half), BF16), 2 * _nbytes((half, D_MODEL), BF16), 3 * _nbytes((tm, D_MODEL), F32),
              _nbytes((tm, D_MODEL), BF16)]
    in_specs = [pl.BlockSpec((tm, half), lambda i: (i, 0)),
                pl.BlockSpec((tm, half), lambda i: (i, 0)),
                pl.BlockSpec((half, D_MODEL), lambda i: (0, 0)),
                pl.BlockSpec((half, D_MODEL), lambda i: (1, 0)),
                tile,
                pl.BlockSpec((None, 1, D_MODEL), lambda i: (layer, 0, 0))]
    out_specs = [tile, tile]
    out_shape = [pltpu.HBM((tokens, D_MODEL), BF16), pltpu.HBM((tokens, D_MODEL), F32)]
    operands = [_hbm(ya), _hbm(yb), _hbm(wo), _hbm(wo), _hbm(x2), post_g3]
    if with_loss:
        in_specs.append(tile)
        out_specs.append(pl.BlockSpec((1, LANES), lambda i: (0, 0)))
        out_shape.append(pltpu.HBM((1, LANES), F32))
        operands.append(_hbm(target))
    return pl.pallas_call(
        body, name=f"outproj_fwd_{layer}", grid=(steps,),
        in_specs=in_specs, out_specs=out_specs, out_shape=out_shape,
        scratch_shapes=[pltpu.VMEM((1, D_MODEL), F32)] if with_loss else [],
        compiler_params=pltpu.CompilerParams(dimension_semantics=("arbitrary" if with_loss else "parallel",),
                                             vmem_limit_bytes=_vmem_limit(blocks)),
    )(*operands)


def _outproj_bwd(g, y, ya, yb, wo, post_g3, layer):
    tokens = g.shape[0]
    tm = _token_tile(tokens)
    half = D_MODEL // 2
    steps = tokens // tm

    def body(g_ref, y_ref, ya_ref, yb_ref, woa_ref, wob_ref, pg_ref, dya_ref, dyb_ref, dwo16_ref, dpg_ref, dwo_ref):
        @pl.when(pl.program_id(0) == 0)
        def _():
            dwo_ref[...] = jnp.zeros_like(dwo_ref)
            dpg_ref[...] = jnp.zeros_like(dpg_ref)

        gv = g_ref[...]
        yf = y_ref[...].astype(F32)
        r = lax.rsqrt(jnp.mean(yf * yf, axis=-1, keepdims=True) + NORM_EPS)
        yhat = yf * r
        gg = gv * pg_ref[...]
        dy = r * (gg - yhat * jnp.mean(gg * yhat, axis=-1, keepdims=True))
        dpg_ref[...] += jnp.sum(gv * yhat, axis=0, keepdims=True)
        dyb16 = dy.astype(BF16)
        dya_ref[...] = _mm_nt(dyb16, woa_ref[...]).astype(BF16)
        dyb_ref[...] = _mm_nt(dyb16, wob_ref[...]).astype(BF16)
        dwo_ref[0:half, :] += _mm_tn(ya_ref[...], dyb16)
        dwo_ref[half:D_MODEL, :] += _mm_tn(yb_ref[...], dyb16)

        @pl.when(pl.program_id(0) == steps - 1)
        def _():
            dwo16_ref[...] = dwo_ref[...].astype(BF16)

    blocks = [_nbytes((tm, D_MODEL), F32), _nbytes((tm, D_MODEL), BF16), 4 * _nbytes((tm, half), BF16),
              2 * _nbytes((half, D_MODEL), BF16), _nbytes((D_MODEL, D_MODEL), BF16)]
    return pl.pallas_call(
        body, name=f"outproj_bwd_{layer}", grid=(tokens // tm,),
        in_specs=[pl.BlockSpec((tm, D_MODEL), lambda i: (i, 0)),
                  pl.BlockSpec((tm, D_MODEL), lambda i: (i, 0)),
                  pl.BlockSpec((tm, half), lambda i: (i, 0)),
                  pl.BlockSpec((tm, half), lambda i: (i, 0)),
                  pl.BlockSpec((half, D_MODEL), lambda i: (0, 0)),
                  pl.BlockSpec((half, D_MODEL), lambda i: (1, 0)),
                  pl.BlockSpec((None, 1, D_MODEL), lambda i: (layer, 0, 0))],
        out_specs=[pl.BlockSpec((tm, half), lambda i: (i, 0)),
                   pl.BlockSpec((tm, half), lambda i: (i, 0)),
                   pl.BlockSpec((D_MODEL, D_MODEL), lambda i: (0, 0)),
                   pl.BlockSpec((1, D_MODEL), lambda i: (0, 0))],
        out_shape=[pltpu.HBM((tokens,half), BF16),
                   pltpu.HBM((tokens,half), BF16),
                   pltpu.HBM((D_MODEL,D_MODEL), BF16),
                   pltpu.HBM((1,D_MODEL), F32)],
        scratch_shapes=[pltpu.VMEM((D_MODEL, D_MODEL), F32)],
        compiler_params=pltpu.CompilerParams(dimension_semantics=("arbitrary",),
                                             vmem_limit_bytes=_vmem_limit(blocks, _nbytes((D_MODEL, D_MODEL), F32))),
    )(_hbm(g), _hbm(y), _hbm(ya), _hbm(yb), _hbm(wo), _hbm(wo), post_g3)


def _gmlp_bwd_dw(pa, dya, h, dqz, dkv, ln_g3, ln_b3, wm, wmt, bsb, layer):
    tokens = pa.shape[0]
    tc = _token_tile(tokens)
    steps = tokens // tc

    def body(pa_ref, dya_ref, h_ref, dqz_ref, dkv_ref, lg_ref, lb_ref, wm_ref, wmt_ref, bsb_ref,
             da_ref, dw_ref, dws_ref, dbs_ref, dlg_ref, dlb_ref, acc_ref, dbsb_ref):
        i = pl.program_id(0)

        @pl.when(i == 0)
        def _():
            acc_ref[...] = jnp.zeros_like(acc_ref)
            dws_ref[...] = jnp.zeros_like(dws_ref)
            dlg_ref[...] = jnp.zeros_like(dlg_ref)
            dlb_ref[...] = jnp.zeros_like(dlb_ref)
            dbsb_ref[...] = jnp.zeros_like(dbsb_ref)

        ln_g = lg_ref[...]
        low = lax.broadcasted_iota(jnp.int32, (CHUNK, LANES), 1) < HEAD_DIM
        hv = h_ref[...]
        acc_ref[:, PA_WIDTH:PA_WIDTH + B_WIDTH] += _mm_tn(hv, dqz_ref[:, 0:B_WIDTH])
        acc_ref[:, PA_WIDTH + K_OFF:PA_WIDTH + BZ_OFF] += _mm_tn(hv, dkv_ref[...])
        acc_ref[:, PA_WIDTH + BZ_OFF:IN_WIDTH] += _mm_tn(hv, dqz_ref[:, B_WIDTH:2 * B_WIDTH])

        for ci in range(tc // CHUNK):
            rows = slice(ci * CHUNK, (ci + 1) * CHUNK)
            gu, gv, pz, u, vhat, rstd, vnb, mixed, sg = _gmlp_forward_chunk(
                pa_ref[rows, :], ln_g, lb_ref[...], wm_ref, bsb_ref[...])
            dy = dya_ref[rows, :].astype(F32)
            sz = pz * sg
            dy_sz = dy * sz
            du = dy_sz * mixed
            dmixed = dy_sz * u
            dz = dy * (u * mixed) * (sg + sz * (1.0 - sg))
            dbsb_ref[...] += dmixed
            dmb = dmixed.astype(BF16)
            zero = jnp.zeros((CHUNK, LANES), BF16)
            parts = []
            for p in range(A_GROUPS // 2):
                dmp = dmb[:, LANES * p:LANES * (p + 1)]
                vp = vnb[:, LANES * p:LANES * (p + 1)]
                parts.append(jnp.where(low, _mm(wmt_ref[2 * p], dmp), _mm(wmt_ref[2 * p + 1], dmp)))
                dws_ref[2 * p] += _mm_nt(jnp.where(low, dmp, zero), vp)
                dws_ref[2 * p + 1] += _mm_nt(jnp.where(low, zero, dmp), vp)
            dvn = jnp.concatenate(parts, axis=1)
            dlg_ref[...] += jnp.sum(dvn * vhat, axis=0, keepdims=True)
            dlb_ref[...] += jnp.sum(dvn, axis=0, keepdims=True)
            dvh = dvn * ln_g
            dvv = rstd * (dvh - jnp.mean(dvh, axis=-1, keepdims=True)
                          - vhat * jnp.mean(dvh * vhat, axis=-1, keepdims=True))
            da_ref[rows, 0:A_WIDTH] = (du * gu).astype(BF16)
            da_ref[rows, A_WIDTH:2 * A_WIDTH] = (dvv * gv).astype(BF16)
            da_ref[rows, 2 * A_WIDTH:3 * A_WIDTH] = dz.astype(BF16)

        acc_ref[:, 0:PA_WIDTH] += _mm_tn(hv, da_ref[...])

        @pl.when(i == steps - 1)
        def _():
            for c in range(IN_WIDTH // LANES):
                cols = slice(LANES * c, LANES * (c + 1))
                dw_ref[cols, :] = acc_ref[:, cols].T.astype(BF16)
            causal =(lax.broadcasted_iota(jnp.int32, (CHUNK, CHUNK), 0)
                      >= lax.broadcasted_iota(jnp.int32, (CHUNK, CHUNK), 1))
            for h in range(A_GROUPS):
                dws_ref[h] = jnp.where(causal, dws_ref[h], 0.0)
            acc = dbsb_ref[...]
            lane_full = lax.broadcasted_iota(jnp.int32, (CHUNK, A_WIDTH), 1)
            lane_out = lax.broadcasted_iota(jnp.int32, (CHUNK, LANES), 1)
            out = jnp.zeros((CHUNK, LANES), F32)
            for h in range(A_GROUPS):
                in_group = jnp.logical_and(lane_full >= HEAD_DIM * h, lane_full < HEAD_DIM * (h + 1))
                s = jnp.sum(jnp.where(in_group, acc, 0.0), axis=-1, keepdims=True)
                out = jnp.where(lane_out == h, s, out)
            dbs_ref[...] = out

    blocks = [_nbytes((tc, PA_WIDTH), BF16), _nbytes((tc, A_WIDTH), BF16), 2 * _nbytes((A_GROUPS, CHUNK, CHUNK), BF16),
              _nbytes((CHUNK, A_WIDTH), F32), _nbytes((tc, PA_WIDTH), BF16), _nbytes((A_GROUPS, CHUNK, CHUNK), F32),
              _nbytes((CHUNK, LANES), F32), _nbytes((tc, D_MODEL), BF16), _nbytes((tc, 2 * B_WIDTH), BF16),
              _nbytes((tc, 2 * KV_WIDTH), BF16), _nbytes((D_MODEL, IN_WIDTH), BF16)]
    scratch = _nbytes((D_MODEL, IN_WIDTH), F32) + _nbytes((CHUNK, A_WIDTH), F32)
    return pl.pallas_call(
        body, name=f"gmlp_bwd_dw_{layer}", grid=(steps,),
        in_specs=[pl.BlockSpec((tc, PA_WIDTH), lambda i: (i, 0)),
                  pl.BlockSpec((tc, A_WIDTH), lambda i: (i, 0)),
                  pl.BlockSpec((tc, D_MODEL), lambda i: (i, 0)),
                  pl.BlockSpec((tc, 2 * B_WIDTH), lambda i: (i, 0)),
                  pl.BlockSpec((pl.Element(tc), pl.Element(2 * KV_WIDTH)),
                               lambda i: (pl.multiple_of(i * tc + BLOCK, BLOCK), 0)),
                  pl.BlockSpec((None, 1, A_WIDTH), lambda i: (layer, 0, 0)),
                  pl.BlockSpec((None, 1, A_WIDTH), lambda i: (layer, 0, 0)),
                  pl.BlockSpec((None, A_GROUPS, CHUNK, CHUNK), lambda i: (layer, 0, 0, 0)),
                  pl.BlockSpec((None, A_GROUPS, CHUNK, CHUNK), lambda i: (layer, 0, 0, 0)),
                  pl.BlockSpec((None, CHUNK, A_WIDTH), lambda i: (layer, 0, 0))],
        out_specs=[pl.BlockSpec((tc, PA_WIDTH), lambda i: (i, 0)),
                   pl.BlockSpec((IN_WIDTH, D_MODEL), lambda i: (0, 0)),
                   pl.BlockSpec((A_GROUPS, CHUNK, CHUNK), lambda i: (0, 0, 0)),
                   pl.BlockSpec((CHUNK, LANES), lambda i: (0, 0)),
                   pl.BlockSpec((1, A_WIDTH), lambda i: (0, 0)),
                   pl.BlockSpec((1, A_WIDTH), lambda i: (0, 0))],
        out_shape=[pltpu.HBM((tokens,PA_WIDTH), BF16),
                   pltpu.HBM((IN_WIDTH, D_MODEL), BF16),
                   pltpu.HBM((A_GROUPS, CHUNK, CHUNK), F32),
                   pltpu.HBM((CHUNK, LANES), F32),
                   pltpu.HBM((1,A_WIDTH), F32),
                   pltpu.HBM((1,A_WIDTH), F32)],
        scratch_shapes=[pltpu.VMEM((D_MODEL, IN_WIDTH), F32), pltpu.VMEM((CHUNK, A_WIDTH), F32)],
        compiler_params=pltpu.CompilerParams(dimension_semantics=("arbitrary",),
                                             vmem_limit_bytes=_vmem_limit(blocks, scratch)),
    )(_hbm(pa), _hbm(dya), _hbm(h), _hbm(dqz), _hbm(dkv), ln_g3, ln_b3, wm, wmt, bsb)


def _attn_bwd(pb, dyb, bias, dbias_in, sinks, b_loc, layer):
    tokens = pb.shape[0]
    nb = tokens // b_loc // BLOCK
    qz_width = 2 * B_WIDTH
    tq = _token_tile(tokens)
    per_tile = tq // BLOCK
    nt = tokens // tq

    def body(sink_ref, cur_ref, prev_ref, dyb_ref, bias_ref, dbias_in_ref, dqz_ref, dkv_ref, dbias_ref, dsink_ref,
             kv_ref, acc_ref):
        t = pl.program_id(0)

        @pl.when(t == 0)
        def _():
            dbias_ref[...] = dbias_in_ref[...]
            dsink_ref[...] = jnp.zeros_like(dsink_ref)
            acc_ref[0:BLOCK, :] = jnp.zeros((BLOCK, 2 * KV_WIDTH), F32)

        @pl.when(t < nt)
        def _():
            acc_ref[BLOCK:, :] = jnp.zeros((tq, 2 * KV_WIDTH), F32)
            _fill_kv(kv_ref, prev_ref, cur_ref)
            tri = _band_masks()
            low = lax.broadcasted_iota(jnp.int32, (BLOCK, LANES), 1) < HEAD_DIM
            low_kv = lax.broadcasted_iota(jnp.int32, (2 * BLOCK, LANES), 1) < HEAD_DIM

            def block(i, carry):
                start = pl.multiple_of(i * BLOCK, BLOCK)
                rows = pl.ds(start, BLOCK)
                first = lax.rem(t * per_tile + i, nb) == 0
                kill = jnp.logical_and(first, jnp.logical_not(tri))
                kv = kv_ref[pl.ds(start, 2 * BLOCK), :]
                k_ops = _kv_variants(kv[:, 0:KV_WIDTH])
                v_ops = _kv_variants(kv[:, KV_WIDTH:2 * KV_WIDTH])
                probs, sink_p, p16, qs, dos, delta, ds16 = {}, {}, {}, {}, {}, {}, {}
                for kvh in range(2):
                    pairs = (2 * kvh, 2 * kvh + 1)
                    qs[kvh] = jnp.concatenate([cur_ref[rows, LANES * p:LANES * (p + 1)] for p in pairs], axis=0)
                    for j in range(2):
                        sf = _mm_nt(qs[kvh], k_ops[kvh][j])
                        for r, p in enumerate(pairs):
                            hd = 2 * p + j
                            probs[hd], sink_p[hd] = _attn_probs(sf[BLOCK * r:BLOCK * (r + 1)], bias_ref[hd],
                                                                sink_ref[layer, hd], tri, kill)
                            p16[hd] = _unwrap16(probs[hd], tri)
                for kvh in range(2):
                    pairs = (2 * kvh, 2 * kvh + 1)
                    out = jnp.zeros((2 * BLOCK, LANES), F32)
                    for j in range(2):
                        out = out + _mm(jnp.concatenate([p16[2 * p + j] for p in pairs], axis=0), v_ops[kvh][j])
                    d_outs = []
                    for r, p in enumerate(pairs):
                        bz = cur_ref[rows, BZ_OFF + LANES * p:BZ_OFF + LANES * (p + 1)].astype(F32)
                        sg = _sigmoid(bz)
                        dyp = dyb_ref[rows, LANES * p:LANES * (p + 1)].astype(F32)
                        out_p = out[BLOCK * r:BLOCK * (r + 1)]
                        d_out = dyp * (bz * sg)
                        dqz_ref[rows, B_WIDTH + LANES * p:B_WIDTH + LANES * (p + 1)] = (
                            dyp * out_p * (sg * (1.0 + bz * (1.0 - sg)))).astype(BF16)
                        dod = d_out * out_p
                        delta[2 * p] = jnp.sum(jnp.where(low, dod, 0.0), axis=-1, keepdims=True)
                        delta[2 * p + 1] = jnp.sum(jnp.where(low, 0.0, dod), axis=-1, keepdims=True)
                        d_outs.append(d_out.astype(BF16))
                    dos[kvh] = jnp.concatenate(d_outs, axis=0)
                for kvh in range(2):
                    pairs = (2 * kvh, 2 * kvh + 1)
                    for j in range(2):
                        dpf = _mm_nt(dos[kvh], v_ops[kvh][j])
                        for r, p in enumerate(pairs):
                            hd = 2 * p + j
                            ds = probs[hd] * (_wrap(dpf[BLOCK * r:BLOCK * (r + 1)], tri) - delta[hd])
                            dsink_ref[hd:hd + 1, :] += jnp.sum(-sink_p[hd] * delta[hd], axis=0, keepdims=True)
                            dbias_ref[hd] += ds
                            ds16[hd] = _unwrap16(ds, tri)
                dk_acc = [[None, None], [None, None]]
                dv_acc = [[None, None], [None, None]]
                for kvh in range(2):
                    pairs = (2 * kvh, 2 * kvh + 1)
                    dq = jnp.zeros((2 * BLOCK, LANES), F32)
                    for j in range(2):
                        dss = jnp.concatenate([ds16[2 * p + j] for p in pairs], axis=0)
                        pss = jnp.concatenate([p16[2 * p + j] for p in pairs], axis=0)
                        dq = dq + _mm(dss, k_ops[kvh][j])
                        dk_acc[kvh][j] = _mm_tn(dss, qs[kvh])
                        dv_acc[kvh][j] = _mm_tn(pss, dos[kvh])
                    for r, p in enumerate(pairs):
                        dqz_ref[rows, LANES * p:LANES * (p + 1)] = (dq[BLOCK * r:BLOCK * (r + 1)] * SCALE).astype(BF16)

                def fold(acc):
                    return jnp.where(low_kv,
                                     acc[0][0] + pltpu.roll(acc[0][1], HEAD_DIM, 1),
                                     pltpu.roll(acc[1][0], HEAD_DIM, 1) + acc[1][1])

                acc_ref[pl.ds(start, 2 * BLOCK), :] += jnp.concatenate(
                    [fold(dk_acc) * SCALE, fold(dv_acc)], axis=1)
                return carry

            lax.fori_loop(0, per_tile, block, 0)
            dkv_ref[...] = acc_ref[0:tq, :].astype(BF16)
            acc_ref[0:BLOCK, :] = acc_ref[tq:tq + BLOCK, :]

        @pl.when(t == nt)
        def _():
            dkv_ref[0:BLOCK, :] = acc_ref[0:BLOCK, :].astype(BF16)
            dkv_ref[BLOCK:, :] = jnp.zeros((tq - BLOCK, 2 * KV_WIDTH), BF16)

    def cur_map(t):
        return (jnp.minimum(t, nt - 1), 0)

    def prev_map(t):
        return (jnp.maximum(jnp.minimum(t, nt - 1) * per_tile - 1, 0), K_OFF // (2 * KV_WIDTH))

    blocks = [_nbytes((tq, PB_WIDTH), BF16), _nbytes((BLOCK, 2 * KV_WIDTH), BF16), _nbytes((tq, B_WIDTH), BF16),
              3 * _nbytes((B_HEADS, BLOCK, BLOCK), F32), _nbytes((tq, qz_width), BF16),
              _nbytes((tq, 2 * KV_WIDTH), BF16), _nbytes((B_HEADS, LANES), F32)]
    scratch = _nbytes((tq + BLOCK, 2 * KV_WIDTH), BF16) + _nbytes((tq + BLOCK, 2 * KV_WIDTH), F32)
    return pl.pallas_call(
        body, name=f"attn_bwd_{layer}", grid=(nt + 1,),
        in_specs=[pl.BlockSpec(memory_space=pltpu.SMEM),
                  pl.BlockSpec((tq, PB_WIDTH), cur_map),
                  pl.BlockSpec((BLOCK, 2 * KV_WIDTH), prev_map),
                  pl.BlockSpec((tq, B_WIDTH), cur_map),
                  pl.BlockSpec((B_HEADS, BLOCK, BLOCK), lambda t: (0, 0, 0)),
                  pl.BlockSpec((B_HEADS, BLOCK, BLOCK), lambda t: (0, 0, 0))],
        out_specs=[pl.BlockSpec((tq, qz_width), cur_map),
                   pl.BlockSpec((tq, 2 * KV_WIDTH), lambda t: (t, 0)),
                   pl.BlockSpec((B_HEADS, BLOCK, BLOCK), lambda t: (0, 0, 0)),
                   pl.BlockSpec((B_HEADS, LANES), lambda t: (0, 0))],
        out_shape=[pltpu.HBM((tokens, qz_width), BF16),
                   pltpu.HBM((tokens + tq, 2 * KV_WIDTH), BF16),
                   pltpu.HBM((B_HEADS, BLOCK, BLOCK), F32),
                   pltpu.HBM((B_HEADS, LANES), F32)],
        scratch_shapes=[pltpu.VMEM((tq + BLOCK, 2 * KV_WIDTH), BF16),
                        pltpu.VMEM((tq + BLOCK, 2 * KV_WIDTH), F32)],
        compiler_params=pltpu.CompilerParams(dimension_semantics=("arbitrary",),
                                             vmem_limit_bytes=_vmem_limit(blocks, scratch)),
    )(sinks, _hbm(pb), _hbm(pb), _hbm(dyb), _hbm(bias), _hbm(dbias_in))


def _inproj_bwd_dx(da, dqz, dkv, w, x2, g, pre_g3, layer):
    tokens = x2.shape[0]
    tm = _token_tile(tokens)

    def body(da_ref, dqz_ref, dkv_ref, w_ref, x_ref, g_ref, pg_ref, gn_ref, dpg_ref):
        @pl.when(pl.program_id(0) == 0)
        def _():
            dpg_ref[...] = jnp.zeros_like(dpg_ref)

        dh = _mm(da_ref[...], w_ref[0:PA_WIDTH, :])
        dh += _mm(dqz_ref[:, 0:B_WIDTH], w_ref[PA_WIDTH:PA_WIDTH + B_WIDTH, :])
        dh += _mm(dkv_ref[...], w_ref[PA_WIDTH + K_OFF:PA_WIDTH + BZ_OFF, :])
        dh += _mm(dqz_ref[:, B_WIDTH:2 * B_WIDTH], w_ref[PA_WIDTH + BZ_OFF:IN_WIDTH, :])
        x = x_ref[...]
        r = lax.rsqrt(jnp.mean(x * x, axis=-1, keepdims=True) + NORM_EPS)
        xhat = x * r
        dhg = dh * pg_ref[...]
        dpg_ref[...] += jnp.sum(dh * xhat, axis=0, keepdims=True)
        gn_ref[...] = g_ref[...] + r * (dhg - xhat * jnp.mean(dhg * xhat, axis=-1, keepdims=True))

    blocks = [_nbytes((tm, PA_WIDTH), BF16), _nbytes((tm, 2 * B_WIDTH), BF16), _nbytes((tm, 2 * KV_WIDTH), BF16),
              _nbytes((D_MODEL, IN_WIDTH), BF16), 3 * _nbytes((tm, D_MODEL), F32)]
    return pl.pallas_call(
        body, name=f"inproj_bwd_dx_{layer}", grid=(tokens // tm,),
        in_specs=[pl.BlockSpec((tm, PA_WIDTH), lambda i: (i, 0)),
                  pl.BlockSpec((tm, 2 * B_WIDTH), lambda i: (i, 0)),
                  pl.BlockSpec((pl.Element(tm), pl.Element(2 * KV_WIDTH)), lambda i: (pl.multiple_of(i * tm + BLOCK, BLOCK), 0)),
                  pl.BlockSpec((IN_WIDTH, D_MODEL), lambda i: (0, 0)),
                  pl.BlockSpec((tm, D_MODEL), lambda i: (i, 0)),
                  pl.BlockSpec((tm, D_MODEL), lambda i: (i, 0)),
                  pl.BlockSpec((None, 1, D_MODEL), lambda i: (layer, 0, 0))],
        out_specs=[pl.BlockSpec((tm, D_MODEL), lambda i: (i, 0)),
                   pl.BlockSpec((1, D_MODEL), lambda i: (0, 0))],
        out_shape=[pltpu.HBM((tokens,D_MODEL), F32),
                   pltpu.HBM((1,D_MODEL), F32)],
        compiler_params=pltpu.CompilerParams(dimension_semantics=("arbitrary",),
                                             vmem_limit_bytes=_vmem_limit(blocks)),
    )(_hbm(da), _hbm(dqz), _hbm(dkv), _hbm(w), _hbm(x2), _hbm(g), pre_g3)


def _rel_bias_grad(dbias, col_bucket, flip):
    def body(db_ref, cb_ref, flip_ref, out_ref):
        cb = cb_ref[...]
        anti = flip_ref[...]
        sums = []
        for h in range(B_HEADS):
            x = db_ref[h]
            hi = x.astype(BF16)
            rest = x - hi.astype(F32)
            mid = rest.astype(BF16)
            low = (rest - mid.astype(F32)).astype(BF16)
            reversed_x = _mm(hi, anti) + _mm(mid, anti) + _mm(low, anti)
            rolled = pltpu.roll(reversed_x, 0, 1, stride=1, stride_axis=0)
            sums.append(jnp.sum(rolled, axis=0, keepdims=True))
        per_dist = jnp.concatenate(sums, axis=0)
        lane = lax.broadcasted_iota(jnp.int32, (B_HEADS, LANES), 1)
        out = jnp.zeros((B_HEADS, LANES), F32)
        for b in range(REL_BUCKETS):
            s = jnp.sum(jnp.where(cb == b, per_dist, 0.0), axis=-1, keepdims=True)
            out = jnp.where(lane == b, s, out)
        out_ref[...] = out

    vm = pl.BlockSpec(memory_space=pltpu.VMEM)
    return pl.pallas_call(
        body, name="rel_bias_grad",
        out_shape=jax.ShapeDtypeStruct((B_HEADS, LANES), F32),
        in_specs=[vm, vm, vm], out_specs=vm,
    )(dbias, col_bucket, flip)


def _reduce_adamw(slots, w, m, v, name):
    rows, cols = w.shape
    tr = _row_tile(rows)
    c1 = 1.0 / (1.0 - ADAM_B1 ** ADAM_STEP)
    c2 = 1.0 / (1.0 - ADAM_B2 ** ADAM_STEP)

    def body(s_ref, w_ref, m_ref, v_ref, g_ref, d_ref, nm_ref, nv_ref):
        g = s_ref[0].astype(F32)
        for i in range(1, N_DEV):
            g = g + s_ref[i].astype(F32)
        nm = ADAM_B1 * m_ref[...] + (1.0 - ADAM_B1) * g
        nv = ADAM_B2 * v_ref[...] + (1.0 - ADAM_B2) * (g * g)
        g_ref[...] = g
        nm_ref[...] = nm
        nv_ref[...] = nv
        d_ref[...] = -ADAM_LR * ((nm * c1) / (jnp.sqrt(nv * c2) + ADAM_EPS) + ADAM_WD * w_ref[...])

    blocks = [_nbytes((N_DEV, tr, cols), slots.dtype), 7 * _nbytes((tr, cols), F32)]
    tile = pl.BlockSpec((tr, cols), lambda i: (i, 0))
    return pl.pallas_call(
        body, name=name, grid=(rows // tr,),
        in_specs=[pl.BlockSpec((N_DEV, tr, cols), lambda i: (0, i, 0)), tile, tile, tile],
        out_specs=[tile] * 4,
        out_shape=[pltpu.HBM((rows, cols), F32)] * 4,
        compiler_params=pltpu.CompilerParams(dimension_semantics=("parallel",),
                                             vmem_limit_bytes=_vmem_limit(blocks)),
    )(_hbm(slots), _hbm(w), _hbm(m), _hbm(v))


_SMALL = ("pre_norm_g","post_norm_g", "ln_v_g", "ln_v_b", "b_spatial", "sinks", "rel_bias")


def _pack_small(parts):
    slabs = []
    for name in _SMALL:
        flat = parts[name].astype(F32).reshape(-1)
        pad = (-flat.shape[0]) % (8 * LANES)
        slabs.append(jnp.pad(flat, (0, pad)).reshape(-1, LANES))
    return jnp.concatenate(slabs, axis=0)


def _unpack_small(slab, shapes):
    out, row = {}, 0
    for name in _SMALL:
        size = int(np.prod(shapes[name]))
        rows = -(-size // (8 * LANES)) * 8
        out[name] = slab[row:row + rows].reshape(-1)[:size].reshape(shapes[name])
        row += rows
    return out


def kernel(x, pre_norm_g, w_in, ln_v_g, ln_v_b, w_spatial, b_spatial, sinks, rel_bias, w_out, post_norm_g, loss_target, m_pre_norm_g, m_w_in, m_ln_v_g, m_ln_v_b, m_w_spatial, m_b_spatial, m_sinks, m_rel_bias, m_w_out, m_post_norm_g, v_pre_norm_g, v_w_in, v_ln_v_g, v_ln_v_b, v_w_spatial, v_b_spatial, v_sinks, v_rel_bias, v_w_out, v_post_norm_g):
    b_loc, seq, _ = x.shape
    tokens = b_loc * seq
    depth = w_in.shape[0]
    in_shard = w_in.shape[2]
    out_shard = w_out.shape[1]
    assert in_shard * N_DEV == IN_WIDTH and out_shard * N_DEV == D_MODEL and seq % BLOCK == 0

    me = _slot(lax.axis_index("x"), lax.axis_index("y"), lax.axis_index("c"))
    w_in_t, m_w_in_t, v_w_in_t = (jnp.swapaxes(a, 1, 2) for a in (w_in, m_w_in, v_w_in))
    w_in16, w_out16 = w_in_t.astype(BF16), w_out.astype(BF16)

    def gather_start(layer, not_before):
        zones = [lax.empty((N_DEV, in_shard, D_MODEL), BF16), lax.empty((N_DEV, out_shard, D_MODEL), BF16)]
        sems, sends, zones, pre_g = _exchange_start(
            [w_in16[layer], w_out16[layer]], zones, [0, 0], f"weights_send_{layer}",
            pre_norm_g.reshape(depth, 1, D_MODEL), after=not_before)
        return (sems, sends, [0, 0]), zones, pre_g

    def full_weights(gin, gout):
        return gin.reshape(IN_WIDTH, D_MODEL), gout.reshape(D_MODEL, D_MODEL)

    def gather_wait(layer, started, zones, after):
        gin, gout = _exchange_wait([started], zones, after, f"weights_wait_{layer}")
        gin = lax.dynamic_update_index_in_dim(gin, w_in16[layer], me, 0)
        gout = lax.dynamic_update_index_in_dim(gout, w_out16[layer], me, 0)
        return full_weights(gin, gout)

    causal = jnp.tril(jnp.ones((CHUNK, CHUNK), dtype=bool))
    wm = jnp.where(causal, w_spatial, 0.0).astype(BF16)
    wmt = _hbm(jnp.swapaxes(wm, -1, -2))
    wm = _hbm(wm)
    bsb = _hbm(jnp.repeat(jnp.swapaxes(b_spatial, -1, -2), HEAD_DIM, axis=-1))
    pre_g3 = _hbm(pre_norm_g.reshape(depth, 1, D_MODEL))
    post_g3 = _hbm(post_norm_g.reshape(depth, 1, D_MODEL))
    ln_g3 = _hbm(ln_v_g.reshape(depth, 1, A_WIDTH))
    ln_b3 = _hbm(ln_v_b.reshape(depth, 1, A_WIDTH))
    bucket = jnp.asarray(_bucket_table())
    bias = _bias_table(rel_bias, bucket)

    xs, saved, weights = [x.reshape(tokens, D_MODEL)], [], []
    pending = None
    for layer in range(depth):
        if layer == 0:
            w, wo = full_weights(*_all_gather([w_in16[0], w_out16[0]], "weights_gather_0"))
        else:
            w, wo = gather_wait(layer, pending[0], pending[1], xs[-1])
        pre_g_fwd = pre_g3
        if layer + 1 < depth:
            pending = gather_start(layer + 1, w)
            pre_g_fwd = _hbm(pending[2])
        h, pa, pb, ya = _inproj_gmlp_fwd(xs[-1], pre_g_fwd, w, ln_g3, ln_b3, wm, bsb, layer)
        yb = _attn_fwd(pb, bias, sinks, b_loc, layer)
        if layer + 1 < depth:
            y, x_next = _outproj_fwd(ya, yb, wo, xs[-1], post_g3, layer)
            xs.append(x_next)
        else:
            y, g, loss_part = _outproj_fwd(ya, yb, wo, xs[-1], post_g3, layer,
                                           target=loss_target.reshape(tokens, D_MODEL))
        saved.append((h, pa, pb, ya, yb, y))
        weights.append((w, wo))
    loss = lax.psum(loss_part[0, 0], ("x", "y", "c"))

    grads = {name: [None] * depth for name in ("pre_norm_g","post_norm_g", "ln_v_g", "ln_v_b",
                                               "b_spatial", "sinks", "dbias")}
    zone_in = lax.empty((N_DEV, depth * in_shard, D_MODEL), BF16)
    zone_out = lax.empty((N_DEV, depth * out_shard, D_MODEL), BF16)
    zone_ws = lax.empty((N_DEV, depth * A_GROUPS * CHUNK, CHUNK), F32)
    started_in, started_out, own_in, own_out = [], [], [None] * depth, [None] * depth
    own_ws = [None] * depth
    dbias = jnp.zeros((B_HEADS, BLOCK, BLOCK), F32)
    for layer in reversed(range(depth)):
        h, pa, pb, ya, yb, y = saved[layer]
        w, wo = weights[layer]
        dya, dyb, dwo, dpost = _outproj_bwd(g, y, ya, yb, wo, post_g3, layer)
        send_out = dwo.reshape(N_DEV, out_shard, D_MODEL)
        own_out[layer] = lax.dynamic_index_in_dim(send_out, me, 0, keepdims=False)
        sems, sends, (zone_out,), ln_g_bwd = _exchange_start(
            [send_out], [zone_out], [layer * out_shard], f"grads_send_out_{layer}", ln_v_g.reshape(depth, 1, A_WIDTH))
        started_out.append((sems, sends, [layer * out_shard]))
        dqz, dkv, dbias, dsink = _attn_bwd(pb, dyb, bias, dbias, sinks, b_loc, layer)
        da, dw, dws, dbs, dlg, dlb = _gmlp_bwd_dw(pa, dya, h, dqz, dkv, _hbm(ln_g_bwd), ln_b3, wm, wmt, bsb, layer)
        own_ws[layer] = dws.reshape(A_GROUPS * CHUNK, CHUNK)
        send_in = dw.reshape(N_DEV, in_shard, D_MODEL)
        own_in[layer] = lax.dynamic_index_in_dim(send_in, me, 0, keepdims=False)
        row_offs = [layer * A_GROUPS * CHUNK, layer * in_shard]
        sems, sends, (zone_ws, zone_in), pre_g_bwd = _exchange_start(
            [own_ws[layer], send_in], [zone_ws, zone_in], row_offs, f"grads_send_in_{layer}",
            pre_norm_g.reshape(depth, 1, D_MODEL))
        started_in.append((sems, sends, row_offs))
        g, dpre = _inproj_bwd_dx(da, dqz, dkv, w, xs[layer], g, _hbm(pre_g_bwd), layer)
        grads["b_spatial"][layer] = dbs[:, :A_GROUPS].T
        grads["ln_v_g"][layer] = dlg[0]
        grads["ln_v_b"][layer] = dlb[0]
        grads["sinks"][layer] = dsink[:, 0]
        grads["pre_norm_g"][layer] = dpre[0]
        grads["post_norm_g"][layer] = dpost[0]
    grad_x = g.reshape(x.shape)
    col_bucket = jnp.asarray(np.broadcast_to(_bucket_table()[0:1, ::-1], (B_HEADS, BLOCK)))
    flip = jnp.asarray(np.eye(BLOCK, dtype=np.float32)[::-1], dtype=BF16)
    drel = _rel_bias_grad(dbias, col_bucket, flip)[:, :REL_BUCKETS].T

    (recv_out,) = _exchange_wait(started_out, [zone_out], g, "grads_wait_out")
    recv_out = lax.dynamic_update_index_in_dim(recv_out, jnp.concatenate(own_out, axis=0), me, 0)
    res_out = _reduce_adamw(recv_out, w_out.reshape(-1, D_MODEL), m_w_out.reshape(-1, D_MODEL),
                            v_w_out.reshape(-1, D_MODEL), "adamw_w_out")
    res_out = [r.reshape(w_out.shape) for r in res_out]

    small_w = dict(pre_norm_g=pre_norm_g, post_norm_g=post_norm_g, ln_v_g=ln_v_g,
                   ln_v_b=ln_v_b, b_spatial=b_spatial, sinks=sinks, rel_bias=rel_bias)
    small_m = dict(pre_norm_g=m_pre_norm_g, post_norm_g=m_post_norm_g, ln_v_g=m_ln_v_g,
                   ln_v_b=m_ln_v_b, b_spatial=m_b_spatial, sinks=m_sinks, rel_bias=m_rel_bias)
    small_v = dict(pre_norm_g=v_pre_norm_g, post_norm_g=v_post_norm_g, ln_v_g=v_ln_v_g,
                   ln_v_b=v_ln_v_b, b_spatial=v_b_spatial, sinks=v_sinks, rel_bias=v_rel_bias)
    small_g = {name: jnp.stack(grads[name]) for name in _SMALL if name != "rel_bias"}
    small_g["rel_bias"] = drel
    shapes = {name: small_w[name].shape for name in _SMALL}
    (slots,) = _all_gather([_pack_small(small_g)], "small_grads_all_gather")
    res_small = _reduce_adamw(slots, _pack_small(small_w), _pack_small(small_m), _pack_small(small_v), "adamw_small")

    recv_ws, recv_in = _exchange_wait(started_in, [zone_ws, zone_in], res_small[0], "grads_wait_in")
    recv_ws = lax.dynamic_update_index_in_dim(recv_ws, jnp.concatenate(own_ws, axis=0), me, 0)
    res_ws = _reduce_adamw(recv_ws, w_spatial.reshape(-1, CHUNK), m_w_spatial.reshape(-1, CHUNK),
                           v_w_spatial.reshape(-1, CHUNK), "adamw_w_spatial")
    res_ws = [r.reshape(w_spatial.shape) for r in res_ws]
    recv_in = lax.dynamic_update_index_in_dim(recv_in, jnp.concatenate(own_in, axis=0), me, 0)
    res_in = _reduce_adamw(recv_in, w_in_t.reshape(-1, D_MODEL), m_w_in_t.reshape(-1, D_MODEL),
                           v_w_in_t.reshape(-1, D_MODEL), "adamw_w_in")
    res_in = [jnp.swapaxes(r.reshape(w_in_t.shape), 1, 2) for r in res_in]
    res_small = [_unpack_small(r, shapes) for r in res_small]

    order = ("pre_norm_g", "w_in", "ln_v_g", "ln_v_b", "w_spatial", "b_spatial", "sinks", "rel_bias", "w_out",
             "post_norm_g")
    outs = [loss, grad_x]
    for kind in range(4):
        for name in order:
            if name == "w_in":
                outs.append(res_in[kind])
            elif name == "w_out":
                outs.append(res_out[kind])
            elif name == "w_spatial":
                outs.append(res_ws[kind])
            else:
                outs.append(res_small[kind][name])
    return tuple(outs)
```

```python
import math

import numpy as np
import jax
import jax.numpy as jnp
from jax import lax
from jax.experimental import pallas as pl
from jax.experimental.pallas import tpu as pltpu

F32 = jnp.float32
BF16 = jnp.bfloat16

D_MODEL = 1024
A_WIDTH = 512
A_GROUPS = 8
CHUNK = 128
B_HEADS = 8
HEAD_DIM = 64
B_WIDTH = 512
KV_WIDTH = 128
BLOCK = 128
REL_BUCKETS = 32
REL_MAX_DIST = 128
NORM_EPS = 1e-6
PA_WIDTH = 3 * A_WIDTH
PB_WIDTH = 2 * B_WIDTH + 2 * KV_WIDTH
IN_WIDTH = PA_WIDTH + PB_WIDTH
K_OFF, V_OFF, BZ_OFF = B_WIDTH, B_WIDTH + KV_WIDTH, B_WIDTH + 2 * KV_WIDTH
SCALE = HEAD_DIM ** -0.5
NEG = -1e30
N_DEV = 8
LANES = 128

ADAM_LR = 0.001
ADAM_B1 = 0.9
ADAM_B2 = 0.999
ADAM_EPS = 1e-08
ADAM_WD = 0.01
ADAM_STEP = 10

V7X_VMEM_BYTES = 64 * 1024 * 1024
VMEM_TEMP_BYTES = 12 * 1024 * 1024
MESH = pl.DeviceIdType.MESH


def _vmem_limit(block_bytes, scratch_bytes=0):
    need = 2 * sum(block_bytes) + scratch_bytes + VMEM_TEMP_BYTES
    return int(min(need, V7X_VMEM_BYTES - 8 * 1024 * 1024))


def _nbytes(shape, dtype):
    return int(np.prod(shape)) * jnp.dtype(dtype).itemsize


def _token_tile(tokens):
    tile = min(512, tokens // 2)
    assert tokens % tile == 0 and tile % CHUNK == 0, tokens
    return tile


def _row_tile(rows, cap=512):
    best = 8
    for t in range(8, cap + 1, 8):
        if rows % t == 0:
            best = t
    assert rows % best == 0, rows
    return best


def _mm(a, b):
    return lax.dot_general(a, b, (((1,), (0,)), ((), ())), preferred_element_type=F32)


def _mm_nt(a, b):
    return lax.dot_general(a, b, (((1,), (1,)), ((), ())), preferred_element_type=F32)


def _mm_tn(a, b):
    return lax.dot_general(a, b, (((0,), (0,)), ((), ())), preferred_element_type=F32)


_GELU_C = math.sqrt(2.0 / math.pi)


_GELU_A = _GELU_C * 0.044715


def _gelu_parts(x):
    x2 = x * x
    t = jnp.tanh(x * (_GELU_C + _GELU_A * x2))
    return x2, t, 0.5 + 0.5 * t


def _gelu(x):
    return x * _gelu_parts(x)[2]


def _gelu_and_grad(x):
    x2, t, half_plus = _gelu_parts(x)
    grad = half_plus + (0.5 * x) * (1.0 - t * t) * (_GELU_C + (3.0 * _GELU_A) * x2)
    return x * half_plus, grad


def _sigmoid(x):
    return 0.5 + 0.5 * jnp.tanh(0.5 * x)


def _bucket_table():
    q = np.arange(BLOCK)[:, None]
    k = np.arange(BLOCK)[None, :]
    dist = np.where(k <= q, q - k, q + BLOCK - k)
    max_exact = REL_BUCKETS // 2
    safe = np.maximum(dist, 1).astype(np.float32)
    large = max_exact + (np.log(safe / np.float32(max_exact)) / np.float32(math.log(REL_MAX_DIST / max_exact))
                         * np.float32(REL_BUCKETS - max_exact)).astype(np.int32)
    large = np.minimum(large, REL_BUCKETS - 1)
    assert dist.min() >= 0 and dist.max() < BLOCK
    return np.where(dist < max_exact, dist, large).astype(np.int32)


def _hbm(x):
    return pltpu.with_memory_space_constraint(x, pltpu.HBM)


def _slot(px, py, pc):
    return 4 * px + 2 * py + pc


def _all_gather(arrs, name):
    n = len(arrs)

    def body(*refs):
        ins, outs = refs[:n], refs[n:2 * n]
        send_sems, recv_sems, local_sems = refs[2 * n:]
        x, y, c = lax.axis_index("x"), lax.axis_index("y"), lax.axis_index("c")
        me, sibling = (x, y, c), (x, y, 1 - c)
        chips = [(1 - x, y), (x, 1 - y), (1 - x, 1 - y)]

        def copy(a, k, block, to, src=None):
            dst = outs[a].at[_slot(*block)]
            return pltpu.make_async_remote_copy(
                src_ref=dst if src is None else src, dst_ref=dst,
                send_sem=send_sems.at[7 * a + k], recv_sem=recv_sems.at[7 * a + k],
                device_id=to, device_id_type=MESH)

        mine = [pltpu.make_async_copy(ins[a], outs[a].at[_slot(*me)], local_sems.at[a]) for a in range(n)]
        for cp in mine:
            cp.start()
        first = []
        for a in range(n):
            first.append(copy(a, 0, me, sibling, src=ins[a]))
            first += [copy(a, 1 + j, me, (*chip, c), src=ins[a]) for j, chip in enumerate(chips)]
        for cp in first:
            cp.start()
        passed = []
        for j, chip in enumerate(chips):
            for a in range(n):
                copy(a, 1 + j, (*chip, c), me).wait_recv()
                fwd = copy(a, 4 + j, (*chip, c), sibling)
                fwd.start()
                passed.append(fwd)
        for a in range(n):
            copy(a, 0, sibling, me).wait_recv()
            for j, chip in enumerate(chips):
                copy(a, 4 + j, (*chip, 1 - c), me).wait_recv()
        for cp in first + passed:
            cp.wait_send()
        for cp in mine:
            cp.wait()

    any_spec = pl.BlockSpec(memory_space=pl.ANY)
    return pl.pallas_call(
        body, name=name,
        out_shape=[jax.ShapeDtypeStruct((N_DEV,) + a.shape, a.dtype) for a in arrs],
        in_specs=[any_spec] * n, out_specs=[any_spec] * n,
        scratch_shapes=[pltpu.SemaphoreType.DMA((7 * n,)), pltpu.SemaphoreType.DMA((7 * n,)),
                        pltpu.SemaphoreType.DMA((n,))],
    )(*arrs)


_HBM_SPEC = pl.BlockSpec(memory_space=pltpu.HBM)
_SEM_SPEC = pl.BlockSpec(memory_space=pltpu.SEMAPHORE)
_DATAFLOW = pltpu.SideEffectType.DATAFLOW_SIDE_EFFECTING


def _peers():
    x, y, c = lax.axis_index("x"), lax.axis_index("y"), lax.axis_index("c")
    peers = []
    for k in range(1, N_DEV):
        fx, fy, fc = (k >> 2) & 1, (k >> 1) & 1, k & 1
        peers.append((1 - x if fx else x, 1 - y if fy else y, 1 - c if fc else c))
    return (x, y, c), peers


def _exchange_copy(send_ref, land_ref, row_off, src_slot, dst_slot, sems, idx, peer):
    rows = send_ref.shape[-2]
    src = send_ref.at[src_slot] if len(send_ref.shape) == 3 else send_ref
    return pltpu.make_async_remote_copy(
        src_ref=src, dst_ref=land_ref.at[dst_slot, pl.ds(row_off, rows), :],
        send_sem=sems[0].at[idx], recv_sem=sems[1].at[idx], device_id=peer, device_id_type=MESH)


def _exchange_start(sends, lands, row_offs, name, carry, after=None):
    n = len(sends)

    def body(*refs):
        ins, zones, carry_ref = refs[:n], refs[n:2 * n], refs[2 * n]
        first_out = 2 * n + 1 + (after is not None)
        sems = refs[first_out:first_out + 2]
        carry_out = refs[-1]
        me, peers = _peers()
        for a in range(n):
            for k, peer in enumerate(peers):
                _exchange_copy(ins[a], zones[a], row_offs[a], _slot(*peer), _slot(*me), sems, 7 * a + k, peer).start()
        carry_out[...] = carry_ref[...]

    arrays = [_hbm(a) for a in list(sends) + list(lands)]
    vmem = pl.BlockSpec(memory_space=pltpu.VMEM)
    out = pl.pallas_call(
        body, name=name,
        out_shape=(pltpu.SemaphoreType.DMA((7 * n,)), pltpu.SemaphoreType.DMA((7 * n,)),
                   *[pltpu.HBM(a.shape, a.dtype) for a in arrays], jax.ShapeDtypeStruct(carry.shape, carry.dtype)),
        in_specs=[_HBM_SPEC] * (2 * n) + [vmem] + ([pl.BlockSpec(memory_space=pl.ANY)] if after is not None else []),
        out_specs=(_SEM_SPEC, _SEM_SPEC, *[_HBM_SPEC] * (2 * n), vmem),
        input_output_aliases={i: 2 + i for i in range(2 * n)},
        compiler_params=pltpu.CompilerParams(has_side_effects=_DATAFLOW),
    )(*arrays, carry, *([after] if after is not None else []))
    return (out[0], out[1]), list(out[2:2 + n]), list(out[2 + n:2 + 2 * n]), out[-1]


def _exchange_wait(started, lands, after, name):
    n = len(lands)
    flat_sends = [s for _, sends, _ in started for s in sends]
    flat_sems = [s for sems, _, _ in started for s in sems]
    ns = len(flat_sends)

    def body(*refs):
        ins, zones = refs[:ns], refs[ns:ns + n]
        sem_refs = refs[ns + n:ns + n + len(flat_sems)]
        me, peers = _peers()
        pos = 0
        for call, (_, sends, row_offs) in enumerate(started):
            sems = sem_refs[2 * call:2 * call + 2]
            for a in range(len(sends)):
                for k, peer in enumerate(peers):
                    cp = _exchange_copy(ins[pos + a], zones[a], row_offs[a], _slot(*peer), _slot(*peer), sems,
                                        7 * a + k, peer)
                    cp.wait_send()
                    cp.wait_recv()
            pos += len(sends)

    arrays = list(flat_sends) + list(lands)
    out = pl.pallas_call(
        body, name=name,
        out_shape=tuple(pltpu.HBM(a.shape, a.dtype) for a in arrays),
        in_specs=[_HBM_SPEC] * len(arrays) + [_SEM_SPEC] * len(flat_sems) + [pl.BlockSpec(memory_space=pl.ANY)],
        out_specs=tuple([_HBM_SPEC] * len(arrays)),
        input_output_aliases={i: i for i in range(len(arrays))},
        compiler_params=pltpu.CompilerParams(has_side_effects=_DATAFLOW),
    )(*arrays, *flat_sems, after)
    return list(out[ns:])


def _bias_table(rel_bias, bucket):
    def body(rel_ref, bucket_ref, out_ref):
        bk = bucket_ref[...]
        for h in range(B_HEADS):
            def pick(b, acc, h=h):
                return jnp.where(bk == b, rel_ref[b, h], acc)
            out_ref[h] = lax.fori_loop(0, REL_BUCKETS, pick, jnp.zeros((BLOCK, BLOCK), F32))

    return pl.pallas_call(
        body, name="bias_table",
        out_shape=jax.ShapeDtypeStruct((B_HEADS, BLOCK, BLOCK), F32),
        in_specs=[pl.BlockSpec(memory_space=pltpu.SMEM), pl.BlockSpec(memory_space=pltpu.VMEM)],
        out_specs=pl.BlockSpec(memory_space=pltpu.VMEM),
    )(rel_bias, bucket)


def _inproj_gmlp_fwd(x2, pre_g3, w, ln_g3, ln_b3, wm, bsb, layer):
    tokens = x2.shape[0]
    tm = _token_tile(tokens)

    def body(x_ref, g_ref, w_ref, lg_ref, lb_ref, wm_ref, bsb_ref, h_ref, pa_ref, pb_ref, ya_ref):
        x = x_ref[...]
        r = lax.rsqrt(jnp.mean(x * x, axis=-1, keepdims=True) + NORM_EPS)
        h = (x * r * g_ref[...]).astype(BF16)
        h_ref[...] = h
        pa_ref[...] = _mm_nt(h, w_ref[0:PA_WIDTH, :]).astype(BF16)
        pb_ref[...] = _mm_nt(h, w_ref[PA_WIDTH:IN_WIDTH, :]).astype(BF16)
        for ci in range(tm // CHUNK):
            rows = slice(ci * CHUNK, (ci + 1) * CHUNK)
            _, _, pz, u, _, _, _, mixed, sg = _gmlp_forward_chunk(
                pa_ref[rows, :], lg_ref[...], lb_ref[...], wm_ref, bsb_ref[...])
            ya_ref[rows, :] = (u * mixed * (pz * sg)).astype(BF16)

    blocks = [_nbytes((tm, D_MODEL), F32), _nbytes((D_MODEL, IN_WIDTH), BF16),
              _nbytes((tm, D_MODEL), BF16), _nbytes((tm, PA_WIDTH), BF16), _nbytes((tm, PB_WIDTH), BF16),
              _nbytes((A_GROUPS, CHUNK, CHUNK), BF16), _nbytes((CHUNK, A_WIDTH), F32), _nbytes((tm, A_WIDTH), BF16)]
    return pl.pallas_call(
        body, name=f"inproj_gmlp_fwd_{layer}", grid=(tokens // tm,),
        in_specs=[pl.BlockSpec((tm, D_MODEL), lambda i: (i, 0)),
                  pl.BlockSpec((None, 1, D_MODEL), lambda i: (layer, 0, 0)),
                  pl.BlockSpec((IN_WIDTH, D_MODEL), lambda i: (0, 0)),
                  pl.BlockSpec((None, 1, A_WIDTH), lambda i: (layer, 0, 0)),
                  pl.BlockSpec((None, 1, A_WIDTH), lambda i: (layer, 0, 0)),
                  pl.BlockSpec((None, A_GROUPS, CHUNK, CHUNK), lambda i: (layer, 0, 0, 0)),
                  pl.BlockSpec((None, CHUNK, A_WIDTH), lambda i: (layer, 0, 0))],
        out_specs=[pl.BlockSpec((tm, D_MODEL), lambda i: (i, 0)),
                   pl.BlockSpec((tm, PA_WIDTH), lambda i: (i, 0)),
                   pl.BlockSpec((tm, PB_WIDTH), lambda i: (i, 0)),
                   pl.BlockSpec((tm, A_WIDTH), lambda i: (i, 0))],
        out_shape=[pltpu.HBM((tokens, D_MODEL), BF16),
                   pltpu.HBM((tokens, PA_WIDTH), BF16),
                   pltpu.HBM((tokens, PB_WIDTH), BF16),
                   pltpu.HBM((tokens, A_WIDTH), BF16)],
        compiler_params=pltpu.CompilerParams(dimension_semantics=("parallel",),
                                             vmem_limit_bytes=_vmem_limit(blocks)),
    )(_hbm(x2), pre_g3, _hbm(w), ln_g3, ln_b3, wm, bsb)


def _gmlp_forward_chunk(pa, ln_g, ln_b, wm_ref, bsb):
    pu = pa[:, 0:A_WIDTH].astype(F32)
    pv = pa[:, A_WIDTH:2 * A_WIDTH].astype(F32)
    pz = pa[:, 2 * A_WIDTH:3 * A_WIDTH].astype(F32)
    u, gu = _gelu_and_grad(pu)
    vv, gv = _gelu_and_grad(pv)
    mu = jnp.mean(vv, axis=-1, keepdims=True)
    xc = vv - mu
    rstd = lax.rsqrt(jnp.mean(xc * xc, axis=-1, keepdims=True) + NORM_EPS)
    vhat = xc * rstd
    vnb = (vhat * ln_g + ln_b).astype(BF16)
    low = lax.broadcasted_iota(jnp.int32, (CHUNK, LANES), 1) < HEAD_DIM
    parts = []
    for p in range(A_GROUPS // 2):
        vp = vnb[:, LANES * p:LANES * (p + 1)]
        parts.append(jnp.where(low, _mm(wm_ref[2 * p], vp), _mm(wm_ref[2 * p + 1], vp)))
    mixed = jnp.concatenate(parts, axis=1) + bsb
    sg = _sigmoid(pz)
    return gu, gv, pz, u, vhat, rstd, vnb, mixed, sg


def _kv_variants(kv):
    t = kv.astype(F32)
    rolled = pltpu.roll(t, HEAD_DIM, 1)
    low = lax.broadcasted_iota(jnp.int32, t.shape, 1) < HEAD_DIM
    zero = jnp.zeros_like(t)
    head0 = (jnp.where(low, t, zero).astype(BF16), jnp.where(low, zero, rolled).astype(BF16))
    head1 = (jnp.where(low, rolled, zero).astype(BF16), jnp.where(low, zero, t).astype(BF16))
    return head0, head1


def _band_masks():
    row = lax.broadcasted_iota(jnp.int32, (BLOCK, BLOCK), 0)
    col = lax.broadcasted_iota(jnp.int32, (BLOCK, BLOCK), 1)
    return col <= row


def _wrap(full, tri):
    return jnp.where(tri, full[:, BLOCK:2 * BLOCK], full[:, 0:BLOCK])


def _attn_probs(sf, bias_h, sink, tri, kill):
    s = _wrap(sf, tri) * SCALE + bias_h
    s = jnp.where(kill, NEG, s)
    m = jnp.maximum(jnp.max(s, axis=-1, keepdims=True), sink)
    e = jnp.exp(s - m)
    es = jnp.exp(sink - m)
    inv = 1.0 / (jnp.sum(e, axis=-1, keepdims=True) + es)
    return e * inv, es * inv


def _unwrap16(p, tri):
    zero = jnp.zeros_like(p)
    return jnp.concatenate([jnp.where(tri, zero, p), jnp.where(tri, p, zero)], axis=1).astype(BF16)


def _fill_kv(kv_ref, prev_ref, cur_ref):
    kv_ref[0:BLOCK, :] = prev_ref[...]
    kv_ref[BLOCK:, :] = cur_ref[:, K_OFF:K_OFF + 2 * KV_WIDTH]


def _attn_outproj_fwd(pb, bias, sinks, ya, wo, x2, post_g3, b_loc, layer, target=None):
    tokens = pb.shape[0]
    nb = tokens // b_loc // BLOCK
    tq = _token_tile(tokens)
    per_tile = tq // BLOCK
    steps = tokens // tq
    with_loss = target is not None
    half = D_MODEL // 2

    def body(*refs):
        sink_ref, cur_ref, prev_ref, bias_ref, ya_ref, woa_ref, wob_ref, x_ref, g_ref = refs[:9]
        if with_loss:
            t_ref, yb_ref, y_ref, grad_ref, loss_ref, kv_ref, acc_ref = refs[9:]
        else:
            yb_ref, y_ref, xn_ref, kv_ref = refs[9:]
        t = pl.program_id(0)
        _fill_kv(kv_ref, prev_ref, cur_ref)
        tri = _band_masks()
        y_from_a = _mm(ya_ref[...], woa_ref[...])

        def block(i, carry):
            start = i * BLOCK
            rows = pl.ds(start, BLOCK)
            first = lax.rem(t * per_tile + i, nb) == 0
            kill = jnp.logical_and(first, jnp.logical_not(tri))
            kv = kv_ref[pl.ds(start, 2 * BLOCK), :]
            k_ops = _kv_variants(kv[:, 0:KV_WIDTH])
            v_ops = _kv_variants(kv[:, KV_WIDTH:2 * KV_WIDTH])
            p16 = {}
            for kvh in range(2):
                pairs = (2 * kvh, 2 * kvh + 1)
                qs = jnp.concatenate([cur_ref[rows, LANES * p:LANES * (p + 1)] for p in pairs], axis=0)
                for j in range(2):
                    sf = _mm_nt(qs, k_ops[kvh][j])
                    for r, p in enumerate(pairs):
                        hd = 2 * p + j
                        probs, _ = _attn_probs(sf[BLOCK * r:BLOCK * (r + 1)], bias_ref[hd], sink_ref[layer, hd],
                                               tri, kill)
                        p16[hd] = _unwrap16(probs, tri)
            for kvh in range(2):
                pairs = (2 * kvh, 2 * kvh + 1)
                out = jnp.zeros((2 * BLOCK, LANES), F32)
                for j in range(2):
                    out = out + _mm(jnp.concatenate([p16[2 * p + j] for p in pairs], axis=0), v_ops[kvh][j])
                for r, p in enumerate(pairs):
                    bz = cur_ref[rows, BZ_OFF + LANES * p:BZ_OFF + LANES * (p + 1)].astype(F32)
                    yb_ref[rows, LANES * p:LANES * (p + 1)] = (
                        out[BLOCK * r:BLOCK * (r + 1)] * (bz * _sigmoid(bz))).astype(BF16)
            return carry

        for i in range(per_tile):
            block(i, 0)

        y = y_from_a + _mm(yb_ref[...], wob_ref[...])
        r = lax.rsqrt(jnp.mean(y * y, axis=-1, keepdims=True) + NORM_EPS)
        x_next = x_ref[...] + y * r * g_ref[...]
        y_ref[...] = y.astype(BF16)
        if not with_loss:
            xn_ref[...] = x_next
            return

        @pl.when(t == 0)
        def _():
            acc_ref[...] = jnp.zeros_like(acc_ref)

        err = x_next - t_ref[...]
        grad_ref[...] = err * (1.0 / D_MODEL)
        acc_ref[...] += jnp.sum(err * err, axis=0, keepdims=True)

        @pl.when(t == steps - 1)
        def _():
            total = jnp.sum(acc_ref[...], axis=-1, keepdims=True) * (0.5 / D_MODEL)
            loss_ref[...] = jnp.broadcast_to(total, loss_ref.shape)

    tile = pl.BlockSpec((tq, D_MODEL), lambda t: (t, 0))
    blocks = [_nbytes((tq, PB_WIDTH), BF16), _nbytes((BLOCK, 2 * KV_WIDTH), BF16),
              _nbytes((B_HEADS, BLOCK, BLOCK), F32), 2 * _nbytes((tq, B_WIDTH), BF16),
              2 * _nbytes((half, D_MODEL), BF16), 3 * _nbytes((tq, D_MODEL), F32), _nbytes((tq, D_MODEL), BF16)]
    scratch = _nbytes((tq + BLOCK, 2 * KV_WIDTH), BF16)
    in_specs = [pl.BlockSpec(memory_space=pltpu.SMEM),
                pl.BlockSpec((tq, PB_WIDTH), lambda t: (t, 0)),
                pl.BlockSpec((BLOCK, 2 * KV_WIDTH),
                             lambda t: (jnp.maximum(t * per_tile - 1, 0), K_OFF // (2 * KV_WIDTH))),
                pl.BlockSpec((B_HEADS, BLOCK, BLOCK), lambda t: (0, 0, 0)),
                pl.BlockSpec((tq, half), lambda t: (t, 0)),
                pl.BlockSpec((half, D_MODEL), lambda t: (0, 0)),
                pl.BlockSpec((half, D_MODEL), lambda t: (1, 0)),
                tile,
                pl.BlockSpec((None, 1, D_MODEL), lambda t: (layer, 0, 0))]
    out_specs = [pl.BlockSpec((tq, B_WIDTH), lambda t: (t, 0)), tile, tile]
    out_shape = [pltpu.HBM((tokens, B_WIDTH), BF16), pltpu.HBM((tokens, D_MODEL), BF16),
                 pltpu.HBM((tokens, D_MODEL), F32)]
    scratch_shapes = [pltpu.VMEM((tq + BLOCK, 2 * KV_WIDTH), BF16)]
    operands = [sinks, _hbm(pb), _hbm(pb), _hbm(bias), _hbm(ya), _hbm(wo), _hbm(wo), _hbm(x2), post_g3]
    if with_loss:
        in_specs.append(tile)
        out_specs.append(pl.BlockSpec((1, LANES), lambda t: (0, 0)))
        out_shape.append(pltpu.HBM((1, LANES), F32))
        scratch_shapes.append(pltpu.VMEM((1, D_MODEL), F32))
        operands.append(_hbm(target))
    return pl.pallas_call(
        body, name=f"attn_outproj_fwd_{layer}", grid=(steps,),
        in_specs=in_specs, out_specs=out_specs, out_shape=out_shape, scratch_shapes=scratch_shapes,
        compiler_params=pltpu.CompilerParams(dimension_semantics=("arbitrary" if with_loss else "parallel",),
                                             vmem_limit_bytes=_vmem_limit(blocks, scratch)),
    )(*operands)


def _outproj_bwd(g, y, ya, yb, wo, post_g3, layer):
    tokens = g.shape[0]
    tm = _token_tile(tokens)
    half = D_MODEL // 2
    steps = tokens // tm

    def body(g_ref, y_ref, ya_ref, yb_ref, woa_ref, wob_ref, pg_ref, dya_ref, dyb_ref, dwo16_ref, dpg_ref, dwo_ref):
        @pl.when(pl.program_id(0) == 0)
        def _():
            dwo_ref[...] = jnp.zeros_like(dwo_ref)
            dpg_ref[...] = jnp.zeros_like(dpg_ref)

        gv = g_ref[...]
        yf = y_ref[...].astype(F32)
        r = lax.rsqrt(jnp.mean(yf * yf, axis=-1, keepdims=True) + NORM_EPS)
        yhat = yf * r
        gg = gv * pg_ref[...]
        dy = r * (gg - yhat * jnp.mean(gg * yhat, axis=-1, keepdims=True))
        dpg_ref[...] += jnp.sum(gv * yhat, axis=0, keepdims=True)
        dyb16 = dy.astype(BF16)
        dya_ref[...] = _mm_nt(dyb16, woa_ref[...]).astype(BF16)
        dyb_ref[...] = _mm_nt(dyb16, wob_ref[...]).astype(BF16)
        dwo_ref[0:half, :] += _mm_tn(ya_ref[...], dyb16)
        dwo_ref[half:D_MODEL, :] += _mm_tn(yb_ref[...], dyb16)

        @pl.when(pl.program_id(0) == steps - 1)
        def _():
            dwo16_ref[...] = dwo_ref[...].astype(BF16)

    blocks = [_nbytes((tm, D_MODEL), F32), _nbytes((tm, D_MODEL), BF16), 4 * _nbytes((tm, half), BF16),
              2 * _nbytes((half, D_MODEL), BF16), _nbytes((D_MODEL, D_MODEL), BF16)]
    return pl.pallas_call(
        body, name=f"outproj_bwd_{layer}", grid=(tokens // tm,),
        in_specs=[pl.BlockSpec((tm, D_MODEL), lambda i: (i, 0)),
                  pl.BlockSpec((tm, D_MODEL), lambda i: (i, 0)),
                  pl.BlockSpec((tm, half), lambda i: (i, 0)),
                  pl.BlockSpec((tm, half), lambda i: (i, 0)),
                  pl.BlockSpec((half, D_MODEL), lambda i: (0, 0)),
                  pl.BlockSpec((half, D_MODEL), lambda i: (1, 0)),
                  pl.BlockSpec((None, 1, D_MODEL), lambda i: (layer, 0, 0))],
        out_specs=[pl.BlockSpec((tm, half), lambda i: (i, 0)),
                   pl.BlockSpec((tm, half), lambda i: (i, 0)),
                   pl.BlockSpec((D_MODEL, D_MODEL), lambda i: (0, 0)),
                   pl.BlockSpec((1, D_MODEL), lambda i: (0, 0))],
        out_shape=[pltpu.HBM((tokens,half), BF16),
                   pltpu.HBM((tokens,half), BF16),
                   pltpu.HBM((D_MODEL,D_MODEL), BF16),
                   pltpu.HBM((1,D_MODEL), F32)],
        scratch_shapes=[pltpu.VMEM((D_MODEL, D_MODEL), F32)],
        compiler_params=pltpu.CompilerParams(dimension_semantics=("arbitrary",),
                                             vmem_limit_bytes=_vmem_limit(blocks, _nbytes((D_MODEL, D_MODEL), F32))),
    )(_hbm(g), _hbm(y), _hbm(ya), _hbm(yb), _hbm(wo), _hbm(wo), post_g3)


def _gmlp_bwd_dw(pa, dya, h, dqz, dkv, ln_g3, ln_b3, wm, wmt, bsb, layer):
    tokens = pa.shape[0]
    tc = _token_tile(tokens)
    steps = tokens // tc

    def body(pa_ref, dya_ref, h_ref, dqz_ref, dkv_ref, lg_ref, lb_ref, wm_ref, wmt_ref, bsb_ref,
             da_ref, dw_ref, dws_ref, dbs_ref, dlg_ref, dlb_ref, acc_ref, dbsb_ref):
        i = pl.program_id(0)

        @pl.when(i == 0)
        def _():
            acc_ref[...] = jnp.zeros_like(acc_ref)
            dws_ref[...] = jnp.zeros_like(dws_ref)
            dlg_ref[...] = jnp.zeros_like(dlg_ref)
            dlb_ref[...] = jnp.zeros_like(dlb_ref)
            dbsb_ref[...] = jnp.zeros_like(dbsb_ref)

        ln_g = lg_ref[...]
        low = lax.broadcasted_iota(jnp.int32, (CHUNK, LANES), 1) < HEAD_DIM
        hv = h_ref[...]
        acc_ref[:, PA_WIDTH:PA_WIDTH + B_WIDTH] += _mm_tn(hv, dqz_ref[:, 0:B_WIDTH])
        acc_ref[:, PA_WIDTH + K_OFF:PA_WIDTH + BZ_OFF] += _mm_tn(hv, dkv_ref[...])
        acc_ref[:, PA_WIDTH + BZ_OFF:IN_WIDTH] += _mm_tn(hv, dqz_ref[:, B_WIDTH:2 * B_WIDTH])

        for ci in range(tc // CHUNK):
            rows = slice(ci * CHUNK, (ci + 1) * CHUNK)
            gu, gv, pz, u, vhat, rstd, vnb, mixed, sg = _gmlp_forward_chunk(
                pa_ref[rows, :], ln_g, lb_ref[...], wm_ref, bsb_ref[...])
            dy = dya_ref[rows, :].astype(F32)
            sz = pz * sg
            dy_sz = dy * sz
            du = dy_sz * mixed
            dmixed = dy_sz * u
            dz = dy * (u * mixed) * (sg + sz * (1.0 - sg))
            dbsb_ref[...] += dmixed
            dmb = dmixed.astype(BF16)
            zero = jnp.zeros((CHUNK, LANES), BF16)
            parts = []
            for p in range(A_GROUPS // 2):
                dmp = dmb[:, LANES * p:LANES * (p + 1)]
                vp = vnb[:, LANES * p:LANES * (p + 1)]
                parts.append(jnp.where(low, _mm(wmt_ref[2 * p], dmp), _mm(wmt_ref[2 * p + 1], dmp)))
                dws_ref[2 * p] += _mm_nt(jnp.where(low, dmp, zero), vp)
                dws_ref[2 * p + 1] += _mm_nt(jnp.where(low, zero, dmp), vp)
            dvn = jnp.concatenate(parts, axis=1)
            dlg_ref[...] += jnp.sum(dvn * vhat, axis=0, keepdims=True)
            dlb_ref[...] += jnp.sum(dvn, axis=0, keepdims=True)
            dvh = dvn * ln_g
            dvv = rstd * (dvh - jnp.mean(dvh, axis=-1, keepdims=True)
                          - vhat * jnp.mean(dvh * vhat, axis=-1, keepdims=True))
            da_ref[rows, 0:A_WIDTH] = (du * gu).astype(BF16)
            da_ref[rows, A_WIDTH:2 * A_WIDTH] = (dvv * gv).astype(BF16)
            da_ref[rows, 2 * A_WIDTH:3 * A_WIDTH] = dz.astype(BF16)

        acc_ref[:, 0:PA_WIDTH] += _mm_tn(hv, da_ref[...])

        @pl.when(i == steps - 1)
        def _():
            for c in range(IN_WIDTH // LANES):
                cols = slice(LANES * c, LANES * (c + 1))
                dw_ref[cols, :] = acc_ref[:, cols].T.astype(BF16)
            causal =(lax.broadcasted_iota(jnp.int32, (CHUNK, CHUNK), 0)
                      >= lax.broadcasted_iota(jnp.int32, (CHUNK, CHUNK), 1))
            for h in range(A_GROUPS):
                dws_ref[h] = jnp.where(causal, dws_ref[h], 0.0)
            acc = dbsb_ref[...]
            lane_full = lax.broadcasted_iota(jnp.int32, (CHUNK, A_WIDTH), 1)
            lane_out = lax.broadcasted_iota(jnp.int32, (CHUNK, LANES), 1)
            out = jnp.zeros((CHUNK, LANES), F32)
            for h in range(A_GROUPS):
                in_group = jnp.logical_and(lane_full >= HEAD_DIM * h, lane_full < HEAD_DIM * (h + 1))
                s = jnp.sum(jnp.where(in_group, acc, 0.0), axis=-1, keepdims=True)
                out = jnp.where(lane_out == h, s, out)
            dbs_ref[...] = out

    blocks = [_nbytes((tc, PA_WIDTH), BF16), _nbytes((tc, A_WIDTH), BF16), 2 * _nbytes((A_GROUPS, CHUNK, CHUNK), BF16),
              _nbytes((CHUNK, A_WIDTH), F32), _nbytes((tc, PA_WIDTH), BF16), _nbytes((A_GROUPS, CHUNK, CHUNK), F32),
              _nbytes((CHUNK, LANES), F32), _nbytes((tc, D_MODEL), BF16), _nbytes((tc, 2 * B_WIDTH), BF16),
              _nbytes((tc, 2 * KV_WIDTH), BF16), _nbytes((D_MODEL, IN_WIDTH), BF16)]
    scratch = _nbytes((D_MODEL, IN_WIDTH), F32) + _nbytes((CHUNK, A_WIDTH), F32)
    return pl.pallas_call(
        body, name=f"gmlp_bwd_dw_{layer}", grid=(steps,),
        in_specs=[pl.BlockSpec((tc, PA_WIDTH), lambda i: (i, 0)),
                  pl.BlockSpec((tc, A_WIDTH), lambda i: (i, 0)),
                  pl.BlockSpec((tc, D_MODEL), lambda i: (i, 0)),
                  pl.BlockSpec((tc, 2 * B_WIDTH), lambda i: (i, 0)),
                  pl.BlockSpec((pl.Element(tc), pl.Element(2 * KV_WIDTH)),
                               lambda i: (pl.multiple_of(i * tc + BLOCK, BLOCK), 0)),
                  pl.BlockSpec((None, 1, A_WIDTH), lambda i: (layer, 0, 0)),
                  pl.BlockSpec((None, 1, A_WIDTH), lambda i: (layer, 0, 0)),
                  pl.BlockSpec((None, A_GROUPS, CHUNK, CHUNK), lambda i: (layer, 0, 0, 0)),
                  pl.BlockSpec((None, A_GROUPS, CHUNK, CHUNK), lambda i: (layer, 0, 0, 0)),
                  pl.BlockSpec((None, CHUNK, A_WIDTH), lambda i: (layer, 0, 0))],
        out_specs=[pl.BlockSpec((tc, PA_WIDTH), lambda i: (i, 0)),
                   pl.BlockSpec((IN_WIDTH, D_MODEL), lambda i: (0, 0)),
                   pl.BlockSpec((A_GROUPS, CHUNK, CHUNK), lambda i: (0, 0, 0)),
                   pl.BlockSpec((CHUNK, LANES), lambda i: (0, 0)),
                   pl.BlockSpec((1, A_WIDTH), lambda i: (0, 0)),
                   pl.BlockSpec((1, A_WIDTH), lambda i: (0, 0))],
        out_shape=[pltpu.HBM((tokens,PA_WIDTH), BF16),
                   pltpu.HBM((IN_WIDTH, D_MODEL), BF16),
                   pltpu.HBM((A_GROUPS, CHUNK, CHUNK), F32),
                   pltpu.HBM((CHUNK, LANES), F32),
                   pltpu.HBM((1,A_WIDTH), F32),
                   pltpu.HBM((1,A_WIDTH), F32)],
        scratch_shapes=[pltpu.VMEM((D_MODEL, IN_WIDTH), F32), pltpu.VMEM((CHUNK, A_WIDTH), F32)],
        compiler_params=pltpu.CompilerParams(dimension_semantics=("arbitrary",),
                                             vmem_limit_bytes=_vmem_limit(blocks, scratch)),
    )(_hbm(pa), _hbm(dya), _hbm(h), _hbm(dqz), _hbm(dkv), ln_g3, ln_b3, wm, wmt, bsb)


def _attn_bwd(pb, dyb, bias, dbias_in, sinks, b_loc, layer):
    tokens = pb.shape[0]
    nb = tokens // b_loc // BLOCK
    qz_width = 2 * B_WIDTH
    tq = _token_tile(tokens)
    per_tile = tq // BLOCK
    nt = tokens // tq

    def body(sink_ref, cur_ref, prev_ref, dyb_ref, bias_ref, dbias_in_ref, dqz_ref, dkv_ref, dbias_ref, dsink_ref,
             kv_ref, acc_ref):
        t = pl.program_id(0)

        @pl.when(t == 0)
        def _():
            dbias_ref[...] = dbias_in_ref[...]
            dsink_ref[...] = jnp.zeros_like(dsink_ref)
            acc_ref[0:BLOCK, :] = jnp.zeros((BLOCK, 2 * KV_WIDTH), F32)

        @pl.when(t < nt)
        def _():
            acc_ref[BLOCK:, :] = jnp.zeros((tq, 2 * KV_WIDTH), F32)
            _fill_kv(kv_ref, prev_ref, cur_ref)
            tri = _band_masks()
            low = lax.broadcasted_iota(jnp.int32, (BLOCK, LANES), 1) < HEAD_DIM
            low_kv = lax.broadcasted_iota(jnp.int32, (2 * BLOCK, LANES), 1) < HEAD_DIM

            def block(i, carry):
                start = i * BLOCK
                rows = pl.ds(start, BLOCK)
                first = lax.rem(t * per_tile + i, nb) == 0
                kill = jnp.logical_and(first, jnp.logical_not(tri))
                kv = kv_ref[pl.ds(start, 2 * BLOCK), :]
                k_ops = _kv_variants(kv[:, 0:KV_WIDTH])
                v_ops = _kv_variants(kv[:, KV_WIDTH:2 * KV_WIDTH])
                probs, sink_p, p16, qs, dos, delta, ds16 = {}, {}, {}, {}, {}, {}, {}
                for kvh in range(2):
                    pairs = (2 * kvh, 2 * kvh + 1)
                    qs[kvh] = jnp.concatenate([cur_ref[rows, LANES * p:LANES * (p + 1)] for p in pairs], axis=0)
                    for j in range(2):
                        sf = _mm_nt(qs[kvh], k_ops[kvh][j])
                        for r, p in enumerate(pairs):
                            hd = 2 * p + j
                            probs[hd], sink_p[hd] = _attn_probs(sf[BLOCK * r:BLOCK * (r + 1)], bias_ref[hd],
                                                                sink_ref[layer, hd], tri, kill)
                            p16[hd] = _unwrap16(probs[hd], tri)
                for kvh in range(2):
                    pairs = (2 * kvh, 2 * kvh + 1)
                    out = jnp.zeros((2 * BLOCK, LANES), F32)
                    for j in range(2):
                        out = out + _mm(jnp.concatenate([p16[2 * p + j] for p in pairs], axis=0), v_ops[kvh][j])
                    d_outs = []
                    for r, p in enumerate(pairs):
                        bz = cur_ref[rows, BZ_OFF + LANES * p:BZ_OFF + LANES * (p + 1)].astype(F32)
                        sg = _sigmoid(bz)
                        dyp = dyb_ref[rows, LANES * p:LANES * (p + 1)].astype(F32)
                        out_p = out[BLOCK * r:BLOCK * (r + 1)]
                        d_out = dyp * (bz * sg)
                        dqz_ref[rows, B_WIDTH + LANES * p:B_WIDTH + LANES * (p + 1)] = (
                            dyp * out_p * (sg * (1.0 + bz * (1.0 - sg)))).astype(BF16)
                        dod = d_out * out_p
                        delta[2 * p] = jnp.sum(jnp.where(low, dod, 0.0), axis=-1, keepdims=True)
                        delta[2 * p + 1] = jnp.sum(jnp.where(low, 0.0, dod), axis=-1, keepdims=True)
                        d_outs.append(d_out.astype(BF16))
                    dos[kvh] = jnp.concatenate(d_outs, axis=0)
                for kvh in range(2):
                    pairs = (2 * kvh, 2 * kvh + 1)
                    for j in range(2):
                        dpf = _mm_nt(dos[kvh], v_ops[kvh][j])
                        for r, p in enumerate(pairs):
                            hd = 2 * p + j
                            ds = probs[hd] * (_wrap(dpf[BLOCK * r:BLOCK * (r + 1)], tri) - delta[hd])
                            dsink_ref[hd:hd + 1, :] += jnp.sum(-sink_p[hd] * delta[hd], axis=0, keepdims=True)
                            dbias_ref[hd] += ds
                            ds16[hd] = _unwrap16(ds, tri)
                dk_acc = [[None, None], [None, None]]
                dv_acc = [[None, None], [None, None]]
                for kvh in range(2):
                    pairs = (2 * kvh, 2 * kvh + 1)
                    dq = jnp.zeros((2 * BLOCK, LANES), F32)
                    for j in range(2):
                        dss = jnp.concatenate([ds16[2 * p + j] for p in pairs], axis=0)
                        pss = jnp.concatenate([p16[2 * p + j] for p in pairs], axis=0)
                        dq = dq + _mm(dss, k_ops[kvh][j])
                        dk_acc[kvh][j] = _mm_tn(dss, qs[kvh])
                        dv_acc[kvh][j] = _mm_tn(pss, dos[kvh])
                    for r, p in enumerate(pairs):
                        dqz_ref[rows, LANES * p:LANES * (p + 1)] = (dq[BLOCK * r:BLOCK * (r + 1)] * SCALE).astype(BF16)

                def fold(acc):
                    return jnp.where(low_kv,
                                     acc[0][0] + pltpu.roll(acc[0][1], HEAD_DIM, 1),
                                     pltpu.roll(acc[1][0], HEAD_DIM, 1) + acc[1][1])

                acc_ref[pl.ds(start, 2 * BLOCK), :] += jnp.concatenate(
                    [fold(dk_acc) * SCALE, fold(dv_acc)], axis=1)
                return carry

            for i in range(per_tile):
                block(i, 0)
            dkv_ref[...] = acc_ref[0:tq, :].astype(BF16)
            acc_ref[0:BLOCK, :] = acc_ref[tq:tq + BLOCK, :]

        @pl.when(t == nt)
        def _():
            dkv_ref[0:BLOCK, :] = acc_ref[0:BLOCK, :].astype(BF16)
            dkv_ref[BLOCK:, :] = jnp.zeros((tq - BLOCK, 2 * KV_WIDTH), BF16)

    def cur_map(t):
        return (jnp.minimum(t, nt - 1), 0)

    def prev_map(t):
        return (jnp.maximum(jnp.minimum(t, nt - 1) * per_tile - 1, 0), K_OFF // (2 * KV_WIDTH))

    blocks = [_nbytes((tq, PB_WIDTH), BF16), _nbytes((BLOCK, 2 * KV_WIDTH), BF16), _nbytes((tq, B_WIDTH), BF16),
              3 * _nbytes((B_HEADS, BLOCK, BLOCK), F32), _nbytes((tq, qz_width), BF16),
              _nbytes((tq, 2 * KV_WIDTH), BF16), _nbytes((B_HEADS, LANES), F32)]
    scratch = _nbytes((tq + BLOCK, 2 * KV_WIDTH), BF16) + _nbytes((tq + BLOCK, 2 * KV_WIDTH), F32)
    return pl.pallas_call(
        body, name=f"attn_bwd_{layer}", grid=(nt + 1,),
        in_specs=[pl.BlockSpec(memory_space=pltpu.SMEM),
                  pl.BlockSpec((tq, PB_WIDTH), cur_map),
                  pl.BlockSpec((BLOCK, 2 * KV_WIDTH), prev_map),
                  pl.BlockSpec((tq, B_WIDTH), cur_map),
                  pl.BlockSpec((B_HEADS, BLOCK, BLOCK), lambda t: (0, 0, 0)),
                  pl.BlockSpec((B_HEADS, BLOCK, BLOCK), lambda t: (0, 0, 0))],
        out_specs=[pl.BlockSpec((tq, qz_width), cur_map),
                   pl.BlockSpec((tq, 2 * KV_WIDTH), lambda t: (t, 0)),
                   pl.BlockSpec((B_HEADS, BLOCK, BLOCK), lambda t: (0, 0, 0)),
                   pl.BlockSpec((B_HEADS, LANES), lambda t: (0, 0))],
        out_shape=[pltpu.HBM((tokens, qz_width), BF16),
                   pltpu.HBM((tokens + tq, 2 * KV_WIDTH), BF16),
                   pltpu.HBM((B_HEADS, BLOCK, BLOCK), F32),
                   pltpu.HBM((B_HEADS, LANES), F32)],
        scratch_shapes=[pltpu.VMEM((tq + BLOCK, 2 * KV_WIDTH), BF16),
                        pltpu.VMEM((tq + BLOCK, 2 * KV_WIDTH), F32)],
        compiler_params=pltpu.CompilerParams(dimension_semantics=("arbitrary",),
                                             vmem_limit_bytes=_vmem_limit(blocks, scratch)),
    )(sinks, _hbm(pb), _hbm(pb), _hbm(dyb), _hbm(bias), _hbm(dbias_in))


def _inproj_bwd_dx(da, dqz, dkv, w, x2, g, pre_g3, layer):
    tokens = x2.shape[0]
    tm = _token_tile(tokens)

    def body(da_ref, dqz_ref, dkv_ref, w_ref, x_ref, g_ref, pg_ref, gn_ref, dpg_ref):
        @pl.when(pl.program_id(0) == 0)
        def _():
            dpg_ref[...] = jnp.zeros_like(dpg_ref)

        dh = _mm(da_ref[...], w_ref[0:PA_WIDTH, :])
        dh += _mm(dqz_ref[:, 0:B_WIDTH], w_ref[PA_WIDTH:PA_WIDTH + B_WIDTH, :])
        dh += _mm(dkv_ref[...], w_ref[PA_WIDTH + K_OFF:PA_WIDTH + BZ_OFF, :])
        dh += _mm(dqz_ref[:, B_WIDTH:2 * B_WIDTH], w_ref[PA_WIDTH + BZ_OFF:IN_WIDTH, :])
        x = x_ref[...]
        r = lax.rsqrt(jnp.mean(x * x, axis=-1, keepdims=True) + NORM_EPS)
        xhat = x * r
        dhg = dh * pg_ref[...]
        dpg_ref[...] += jnp.sum(dh * xhat, axis=0, keepdims=True)
        gn_ref[...] = g_ref[...] + r * (dhg - xhat * jnp.mean(dhg * xhat, axis=-1, keepdims=True))

    blocks = [_nbytes((tm, PA_WIDTH), BF16), _nbytes((tm, 2 * B_WIDTH), BF16), _nbytes((tm, 2 * KV_WIDTH), BF16),
              _nbytes((D_MODEL, IN_WIDTH), BF16), 3 * _nbytes((tm, D_MODEL), F32)]
    return pl.pallas_call(
        body, name=f"inproj_bwd_dx_{layer}", grid=(tokens // tm,),
        in_specs=[pl.BlockSpec((tm, PA_WIDTH), lambda i: (i, 0)),
                  pl.BlockSpec((tm, 2 * B_WIDTH), lambda i: (i, 0)),
                  pl.BlockSpec((pl.Element(tm), pl.Element(2 * KV_WIDTH)), lambda i: (pl.multiple_of(i * tm + BLOCK, BLOCK), 0)),
                  pl.BlockSpec((IN_WIDTH, D_MODEL), lambda i: (0, 0)),
                  pl.BlockSpec((tm, D_MODEL), lambda i: (i, 0)),
                  pl.BlockSpec((tm, D_MODEL), lambda i: (i, 0)),
                  pl.BlockSpec((None, 1, D_MODEL), lambda i: (layer, 0, 0))],
        out_specs=[pl.BlockSpec((tm, D_MODEL), lambda i: (i, 0)),
                   pl.BlockSpec((1, D_MODEL), lambda i: (0, 0))],
        out_shape=[pltpu.HBM((tokens,D_MODEL), F32),
                   pltpu.HBM((1,D_MODEL), F32)],
        compiler_params=pltpu.CompilerParams(dimension_semantics=("arbitrary",),
                                             vmem_limit_bytes=_vmem_limit(blocks)),
    )(_hbm(da), _hbm(dqz), _hbm(dkv), _hbm(w), _hbm(x2), _hbm(g), pre_g3)


def _rel_bias_grad(dbias, col_bucket, flip):
    def body(db_ref, cb_ref, flip_ref, out_ref):
        cb = cb_ref[...]
        anti = flip_ref[...]
        sums = []
        for h in range(B_HEADS):
            x = db_ref[h]
            hi = x.astype(BF16)
            rest = x - hi.astype(F32)
            mid = rest.astype(BF16)
            low = (rest - mid.astype(F32)).astype(BF16)
            reversed_x = _mm(hi, anti) + _mm(mid, anti) + _mm(low, anti)
            rolled = pltpu.roll(reversed_x, 0, 1, stride=1, stride_axis=0)
            sums.append(jnp.sum(rolled, axis=0, keepdims=True))
        per_dist = jnp.concatenate(sums, axis=0)
        lane = lax.broadcasted_iota(jnp.int32, (B_HEADS, LANES), 1)
        out = jnp.zeros((B_HEADS, LANES), F32)
        for b in range(REL_BUCKETS):
            s = jnp.sum(jnp.where(cb == b, per_dist, 0.0), axis=-1, keepdims=True)
            out = jnp.where(lane == b, s, out)
        out_ref[...] = out

    vm = pl.BlockSpec(memory_space=pltpu.VMEM)
    return pl.pallas_call(
        body, name="rel_bias_grad",
        out_shape=jax.ShapeDtypeStruct((B_HEADS, LANES), F32),
        in_specs=[vm, vm, vm], out_specs=vm,
    )(dbias, col_bucket, flip)


def _reduce_adamw(slots, w, m, v, name):
    rows, cols = w.shape
    tr = _row_tile(rows)
    c1 = 1.0 / (1.0 - ADAM_B1 ** ADAM_STEP)
    c2 = 1.0 / (1.0 - ADAM_B2 ** ADAM_STEP)

    def body(s_ref, w_ref, m_ref, v_ref, g_ref, d_ref, nm_ref, nv_ref):
        g = s_ref[0].astype(F32)
        for i in range(1, N_DEV):
            g = g + s_ref[i].astype(F32)
        nm = ADAM_B1 * m_ref[...] + (1.0 - ADAM_B1) * g
        nv = ADAM_B2 * v_ref[...] + (1.0 - ADAM_B2) * (g * g)
        g_ref[...] = g
        nm_ref[...] = nm
        nv_ref[...] = nv
        d_ref[...] = -ADAM_LR * ((nm * c1) / (jnp.sqrt(nv * c2) + ADAM_EPS) + ADAM_WD * w_ref[...])

    blocks = [_nbytes((N_DEV, tr, cols), slots.dtype), 7 * _nbytes((tr, cols), F32)]
    tile = pl.BlockSpec((tr, cols), lambda i: (i, 0))
    return pl.pallas_call(
        body, name=name, grid=(rows // tr,),
        in_specs=[pl.BlockSpec((N_DEV, tr, cols), lambda i: (0, i, 0)), tile, tile, tile],
        out_specs=[tile] * 4,
        out_shape=[pltpu.HBM((rows, cols), F32)] * 4,
        compiler_params=pltpu.CompilerParams(dimension_semantics=("parallel",),
                                             vmem_limit_bytes=_vmem_limit(blocks)),
    )(_hbm(slots), _hbm(w), _hbm(m), _hbm(v))


_SMALL = ("pre_norm_g","post_norm_g", "ln_v_g", "ln_v_b", "b_spatial", "sinks", "rel_bias")


def _pack_small(parts):
    slabs = []
    for name in _SMALL:
        flat = parts[name].astype(F32).reshape(-1)
        pad = (-flat.shape[0]) % (8 * LANES)
        slabs.append(jnp.pad(flat, (0, pad)).reshape(-1, LANES))
    return jnp.concatenate(slabs, axis=0)


def _unpack_small(slab, shapes):
    out, row = {}, 0
    for name in _SMALL:
        size = int(np.prod(shapes[name]))
        rows = -(-size // (8 * LANES)) * 8
        out[name] = slab[row:row + rows].reshape(-1)[:size].reshape(shapes[name])
        row += rows
    return out


def kernel(x, pre_norm_g, w_in, ln_v_g, ln_v_b, w_spatial, b_spatial, sinks, rel_bias, w_out, post_norm_g, loss_target, m_pre_norm_g, m_w_in, m_ln_v_g, m_ln_v_b, m_w_spatial, m_b_spatial, m_sinks, m_rel_bias, m_w_out, m_post_norm_g, v_pre_norm_g, v_w_in, v_ln_v_g, v_ln_v_b, v_w_spatial, v_b_spatial, v_sinks, v_rel_bias, v_w_out, v_post_norm_g):
    b_loc, seq, _ = x.shape
    tokens = b_loc * seq
    depth = w_in.shape[0]
    in_shard = w_in.shape[2]
    out_shard = w_out.shape[1]
    assert in_shard * N_DEV == IN_WIDTH and out_shard * N_DEV == D_MODEL and seq % BLOCK == 0

    me = _slot(lax.axis_index("x"), lax.axis_index("y"), lax.axis_index("c"))
    w_in_t, m_w_in_t, v_w_in_t = (jnp.swapaxes(a, 1, 2) for a in (w_in, m_w_in, v_w_in))
    w_in16, w_out16 = w_in_t.astype(BF16), w_out.astype(BF16)

    def gather_start(layer, not_before):
        zones = [lax.empty((N_DEV, in_shard, D_MODEL), BF16), lax.empty((N_DEV, out_shard, D_MODEL), BF16)]
        sems, sends, zones, pre_g = _exchange_start(
            [w_in16[layer], w_out16[layer]], zones, [0, 0], f"weights_send_{layer}",
            pre_norm_g.reshape(depth, 1, D_MODEL), after=not_before)
        return (sems, sends, [0, 0]), zones, pre_g

    def full_weights(gin, gout):
        return gin.reshape(IN_WIDTH, D_MODEL), gout.reshape(D_MODEL, D_MODEL)

    def gather_wait(layer, started, zones, after):
        gin, gout = _exchange_wait([started], zones, after, f"weights_wait_{layer}")
        gin = lax.dynamic_update_index_in_dim(gin, w_in16[layer], me, 0)
        gout = lax.dynamic_update_index_in_dim(gout, w_out16[layer], me, 0)
        return full_weights(gin, gout)

    causal = jnp.tril(jnp.ones((CHUNK, CHUNK), dtype=bool))
    wm = jnp.where(causal, w_spatial, 0.0).astype(BF16)
    wmt = _hbm(jnp.swapaxes(wm, -1, -2))
    wm = _hbm(wm)
    bsb = _hbm(jnp.repeat(jnp.swapaxes(b_spatial, -1, -2), HEAD_DIM, axis=-1))
    pre_g3 = _hbm(pre_norm_g.reshape(depth, 1, D_MODEL))
    post_g3 = _hbm(post_norm_g.reshape(depth, 1, D_MODEL))
    ln_g3 = _hbm(ln_v_g.reshape(depth, 1, A_WIDTH))
    ln_b3 = _hbm(ln_v_b.reshape(depth, 1, A_WIDTH))
    bucket = jnp.asarray(_bucket_table())
    bias = _bias_table(rel_bias, bucket)

    xs, saved, weights = [x.reshape(tokens, D_MODEL)], [], []
    pending = None
    for layer in range(depth):
        if layer == 0:
            w, wo = full_weights(*_all_gather([w_in16[0], w_out16[0]], "weights_gather_0"))
        else:
            w, wo = gather_wait(layer, pending[0], pending[1], xs[-1])
        pre_g_fwd = pre_g3
        if layer + 1 < depth:
            pending = gather_start(layer + 1, w)
            pre_g_fwd = _hbm(pending[2])
        h, pa, pb, ya = _inproj_gmlp_fwd(xs[-1], pre_g_fwd, w, ln_g3, ln_b3, wm, bsb, layer)
        if layer + 1 < depth:
            yb, y, x_next = _attn_outproj_fwd(pb, bias, sinks, ya, wo, xs[-1], post_g3, b_loc, layer)
            xs.append(x_next)
        else:
            yb, y, g, loss_part = _attn_outproj_fwd(pb, bias, sinks, ya, wo, xs[-1], post_g3, b_loc, layer,
                                                    target=loss_target.reshape(tokens, D_MODEL))
        saved.append((h, pa, pb, ya, yb, y))
        weights.append((w, wo))
    loss = lax.psum(loss_part[0, 0], ("x", "y", "c"))

    grads = {name: [None] * depth for name in ("pre_norm_g","post_norm_g", "ln_v_g", "ln_v_b",
                                               "b_spatial", "sinks", "dbias")}
    zone_in = lax.empty((N_DEV, depth * in_shard, D_MODEL), BF16)
    zone_out = lax.empty((N_DEV, depth * out_shard, D_MODEL), BF16)
    zone_ws = lax.empty((N_DEV, depth * A_GROUPS * CHUNK, CHUNK), F32)
    started_in, started_out, own_in, own_out = [], [], [None] * depth, [None] * depth
    own_ws = [None] * depth
    dbias = jnp.zeros((B_HEADS, BLOCK, BLOCK), F32)
    for layer in reversed(range(depth)):
        h, pa, pb, ya, yb, y = saved[layer]
        w, wo = weights[layer]
        dya, dyb, dwo, dpost = _outproj_bwd(g, y, ya, yb, wo, post_g3, layer)
        send_out = dwo.reshape(N_DEV, out_shard, D_MODEL)
        own_out[layer] = lax.dynamic_index_in_dim(send_out, me, 0, keepdims=False)
        sems, sends, (zone_out,), ln_g_bwd = _exchange_start(
            [send_out], [zone_out], [layer * out_shard], f"grads_send_out_{layer}", ln_v_g.reshape(depth, 1, A_WIDTH))
        started_out.append((sems, sends, [layer * out_shard]))
        dqz, dkv, dbias, dsink = _attn_bwd(pb, dyb, bias, dbias, sinks, b_loc, layer)
        da, dw, dws, dbs, dlg, dlb = _gmlp_bwd_dw(pa, dya, h, dqz, dkv, _hbm(ln_g_bwd), ln_b3, wm, wmt, bsb, layer)
        own_ws[layer] = dws.reshape(A_GROUPS * CHUNK, CHUNK)
        send_in = dw.reshape(N_DEV, in_shard, D_MODEL)
        own_in[layer] = lax.dynamic_index_in_dim(send_in, me, 0, keepdims=False)
        row_offs = [layer * A_GROUPS * CHUNK, layer * in_shard]
        sems, sends, (zone_ws, zone_in), pre_g_bwd = _exchange_start(
            [own_ws[layer], send_in], [zone_ws, zone_in], row_offs, f"grads_send_in_{layer}",
            pre_norm_g.reshape(depth, 1, D_MODEL))
        started_in.append((sems, sends, row_offs))
        g, dpre = _inproj_bwd_dx(da, dqz, dkv, w, xs[layer], g, _hbm(pre_g_bwd), layer)
        grads["b_spatial"][layer] = dbs[:, :A_GROUPS].T
        grads["ln_v_g"][layer] = dlg[0]
        grads["ln_v_b"][layer] = dlb[0]
        grads["sinks"][layer] = dsink[:, 0]
        grads["pre_norm_g"][layer] = dpre[0]
        grads["post_norm_g"][layer] = dpost[0]
    grad_x = g.reshape(x.shape)
    col_bucket = jnp.asarray(np.broadcast_to(_bucket_table()[0:1, ::-1], (B_HEADS, BLOCK)))
    flip = jnp.asarray(np.eye(BLOCK, dtype=np.float32)[::-1], dtype=BF16)
    drel = _rel_bias_grad(dbias, col_bucket, flip)[:, :REL_BUCKETS].T

    (recv_out,) = _exchange_wait(started_out, [zone_out], g, "grads_wait_out")
    recv_out = lax.dynamic_update_index_in_dim(recv_out, jnp.concatenate(own_out, axis=0), me, 0)
    res_out = _reduce_adamw(recv_out, w_out.reshape(-1, D_MODEL), m_w_out.reshape(-1, D_MODEL),
                            v_w_out.reshape(-1, D_MODEL), "adamw_w_out")
    res_out = [r.reshape(w_out.shape) for r in res_out]

    small_w = dict(pre_norm_g=pre_norm_g, post_norm_g=post_norm_g, ln_v_g=ln_v_g,
                   ln_v_b=ln_v_b, b_spatial=b_spatial, sinks=sinks, rel_bias=rel_bias)
    small_m = dict(pre_norm_g=m_pre_norm_g, post_norm_g=m_post_norm_g, ln_v_g=m_ln_v_g,
                   ln_v_b=m_ln_v_b, b_spatial=m_b_spatial, sinks=m_sinks, rel_bias=m_rel_bias)
    small_v = dict(pre_norm_g=v_pre_norm_g, post_norm_g=v_post_norm_g, ln_v_g=v_ln_v_g,
                   ln_v_b=v_ln_v_b, b_spatial=v_b_spatial, sinks=v_sinks, rel_bias=v_rel_bias)
    small_g = {name: jnp.stack(grads[name]) for name in _SMALL if name != "rel_bias"}
    small_g["rel_bias"] = drel
    shapes = {name: small_w[name].shape for name in _SMALL}
    (slots,) = _all_gather([_pack_small(small_g)], "small_grads_all_gather")
    res_small = _reduce_adamw(slots, _pack_small(small_w), _pack_small(small_m), _pack_small(small_v), "adamw_small")

    recv_ws, recv_in = _exchange_wait(started_in, [zone_ws, zone_in], res_small[0], "grads_wait_in")
    recv_ws = lax.dynamic_update_index_in_dim(recv_ws, jnp.concatenate(own_ws, axis=0), me, 0)
    res_ws = _reduce_adamw(recv_ws, w_spatial.reshape(-1, CHUNK), m_w_spatial.reshape(-1, CHUNK),
                           v_w_spatial.reshape(-1, CHUNK), "adamw_w_spatial")
    res_ws = [r.reshape(w_spatial.shape) for r in res_ws]
    recv_in = lax.dynamic_update_index_in_dim(recv_in, jnp.concatenate(own_in, axis=0), me, 0)
    res_in = _reduce_adamw(recv_in, w_in_t.reshape(-1, D_MODEL), m_w_in_t.reshape(-1, D_MODEL),
                           v_w_in_t.reshape(-1, D_MODEL), "adamw_w_in")
    res_in = [jnp.swapaxes(r.reshape(w_in_t.shape), 1, 2) for r in res_in]
    res_small = [_unpack_small(r, shapes) for r in res_small]

    order = ("pre_norm_g", "w_in", "ln_v_g", "ln_v_b", "w_spatial", "b_spatial", "sinks", "rel_bias", "w_out",
             "post_norm_g")
    outs = [loss, grad_x]
    for kind in range(4):
        for name in order:
            if name == "w_in":
                outs.append(res_in[kind])
            elif name == "w_out":
                outs.append(res_out[kind])
            elif name == "w_spatial":
                outs.append(res_ws[kind])
            else:
                outs.append(res_small[kind][name])
    return tuple(outs)
```

```python
import math

import numpy as np
import jax
import jax.numpy as jnp
from jax import lax
from jax.experimental import pallas as pl
from jax.experimental.pallas import tpu as pltpu

F32 = jnp.float32
BF16 = jnp.bfloat16

D_MODEL = 1024
A_WIDTH = 512
A_GROUPS = 8
CHUNK = 128
B_HEADS = 8
HEAD_DIM = 64
B_WIDTH = 512
KV_WIDTH = 128
BLOCK = 128
REL_BUCKETS = 32
REL_MAX_DIST = 128
NORM_EPS = 1e-6
PA_WIDTH = 3 * A_WIDTH
PB_WIDTH = 2 * B_WIDTH + 2 * KV_WIDTH
IN_WIDTH = PA_WIDTH + PB_WIDTH
K_OFF, V_OFF, BZ_OFF = B_WIDTH, B_WIDTH + KV_WIDTH, B_WIDTH + 2 * KV_WIDTH
SCALE = HEAD_DIM ** -0.5
NEG = -1e30
N_DEV = 8
LANES = 128

ADAM_LR = 0.001
ADAM_B1 = 0.9
ADAM_B2 = 0.999
ADAM_EPS = 1e-08
ADAM_WD = 0.01
ADAM_STEP = 10

V7X_VMEM_BYTES = 64 * 1024 * 1024
VMEM_TEMP_BYTES = 12 * 1024 * 1024
MESH = pl.DeviceIdType.MESH


def _vmem_limit(block_bytes, scratch_bytes=0):
    need = 2 * sum(block_bytes) + scratch_bytes + VMEM_TEMP_BYTES
    return int(min(need, V7X_VMEM_BYTES - 8 * 1024 * 1024))


def _nbytes(shape, dtype):
    return int(np.prod(shape)) * jnp.dtype(dtype).itemsize


def _token_tile(tokens):
    tile = min(512, tokens // 2)
    assert tokens % tile == 0 and tile % CHUNK == 0, tokens
    return tile


def _row_tile(rows, cap=512):
    best = 8
    for t in range(8, cap + 1, 8):
        if rows % t == 0:
            best = t
    assert rows % best == 0, rows
    return best


def _mm(a, b):
    return lax.dot_general(a, b, (((1,), (0,)), ((), ())), preferred_element_type=F32)


def _mm_nt(a, b):
    return lax.dot_general(a, b, (((1,), (1,)), ((), ())), preferred_element_type=F32)


def _mm_tn(a, b):
    return lax.dot_general(a, b, (((0,), (0,)), ((), ())), preferred_element_type=F32)


_GELU_C = math.sqrt(2.0 / math.pi)


_GELU_A = _GELU_C * 0.044715


def _gelu_parts(x):
    x2 = x * x
    t = jnp.tanh(x * (_GELU_C + _GELU_A * x2))
    return x2, t, 0.5 + 0.5 * t


def _gelu(x):
    return x * _gelu_parts(x)[2]


def _gelu_and_grad(x):
    x2, t, half_plus = _gelu_parts(x)
    grad = half_plus + (0.5 * x) * (1.0 - t * t) * (_GELU_C + (3.0 * _GELU_A) * x2)
    return x * half_plus, grad


def _sigmoid(x):
    return 0.5 + 0.5 * jnp.tanh(0.5 * x)


def _bucket_table():
    q = np.arange(BLOCK)[:, None]
    k = np.arange(BLOCK)[None, :]
    dist = np.where(k <= q, q - k, q + BLOCK - k)
    max_exact = REL_BUCKETS // 2
    safe = np.maximum(dist, 1).astype(np.float32)
    large = max_exact + (np.log(safe / np.float32(max_exact)) / np.float32(math.log(REL_MAX_DIST / max_exact))
                         * np.float32(REL_BUCKETS - max_exact)).astype(np.int32)
    large = np.minimum(large, REL_BUCKETS - 1)
    assert dist.min() >= 0 and dist.max() < BLOCK
    return np.where(dist < max_exact, dist, large).astype(np.int32)


def _hbm(x):
    return pltpu.with_memory_space_constraint(x, pltpu.HBM)


def _slot(px, py, pc):
    return 4 * px + 2 * py + pc


def _all_gather(arrs, name):
    n = len(arrs)

    def body(*refs):
        ins, outs = refs[:n], refs[n:2 * n]
        send_sems, recv_sems, local_sems = refs[2 * n:]
        x, y, c = lax.axis_index("x"), lax.axis_index("y"), lax.axis_index("c")
        me, sibling = (x, y, c), (x, y, 1 - c)
        chips = [(1 - x, y), (x, 1 - y), (1 - x, 1 - y)]

        def copy(a, k, block, to, src=None):
            dst = outs[a].at[_slot(*block)]
            return pltpu.make_async_remote_copy(
                src_ref=dst if src is None else src, dst_ref=dst,
                send_sem=send_sems.at[7 * a + k], recv_sem=recv_sems.at[7 * a + k],
                device_id=to, device_id_type=MESH)

        mine = [pltpu.make_async_copy(ins[a], outs[a].at[_slot(*me)], local_sems.at[a]) for a in range(n)]
        for cp in mine:
            cp.start()
        first = []
        for a in range(n):
            first.append(copy(a, 0, me, sibling, src=ins[a]))
            first += [copy(a, 1 + j, me, (*chip, c), src=ins[a]) for j, chip in enumerate(chips)]
        for cp in first:
            cp.start()
        passed = []
        for j, chip in enumerate(chips):
            for a in range(n):
                copy(a, 1 + j, (*chip, c), me).wait_recv()
                fwd = copy(a, 4 + j, (*chip, c), sibling)
                fwd.start()
                passed.append(fwd)
        for a in range(n):
            copy(a, 0, sibling, me).wait_recv()
            for j, chip in enumerate(chips):
                copy(a, 4 + j, (*chip, 1 - c), me).wait_recv()
        for cp in first + passed:
            cp.wait_send()
        for cp in mine:
            cp.wait()

    any_spec = pl.BlockSpec(memory_space=pl.ANY)
    return pl.pallas_call(
        body, name=name,
        out_shape=[jax.ShapeDtypeStruct((N_DEV,) + a.shape, a.dtype) for a in arrs],
        in_specs=[any_spec] * n, out_specs=[any_spec] * n,
        scratch_shapes=[pltpu.SemaphoreType.DMA((7 * n,)), pltpu.SemaphoreType.DMA((7 * n,)),
                        pltpu.SemaphoreType.DMA((n,))],
    )(*arrs)


_HBM_SPEC = pl.BlockSpec(memory_space=pltpu.HBM)
_SEM_SPEC = pl.BlockSpec(memory_space=pltpu.SEMAPHORE)
_DATAFLOW = pltpu.SideEffectType.DATAFLOW_SIDE_EFFECTING


def _peers():
    x, y, c = lax.axis_index("x"), lax.axis_index("y"), lax.axis_index("c")
    peers = []
    for k in range(1, N_DEV):
        fx, fy, fc = (k >> 2) & 1, (k >> 1) & 1, k & 1
        peers.append((1 - x if fx else x, 1 - y if fy else y, 1 - c if fc else c))
    return (x, y, c), peers


def _exchange_copy(send_ref, land_ref, row_off, src_slot, dst_slot, sems, idx, peer):
    rows = send_ref.shape[-2]
    src = send_ref.at[src_slot] if len(send_ref.shape) == 3 else send_ref
    return pltpu.make_async_remote_copy(
        src_ref=src, dst_ref=land_ref.at[dst_slot, pl.ds(row_off, rows), :],
        send_sem=sems[0].at[idx], recv_sem=sems[1].at[idx], device_id=peer, device_id_type=MESH)


def _exchange_start(sends, lands, row_offs, name, carry, after=None):
    n = len(sends)

    def body(*refs):
        ins, zones, carry_ref = refs[:n], refs[n:2 * n], refs[2 * n]
        first_out = 2 * n + 1 + (after is not None)
        sems = refs[first_out:first_out + 2]
        carry_out = refs[-1]
        me, peers = _peers()
        for a in range(n):
            for k, peer in enumerate(peers):
                _exchange_copy(ins[a], zones[a], row_offs[a], _slot(*peer), _slot(*me), sems, 7 * a + k, peer).start()
        carry_out[...] = carry_ref[...]

    arrays = [_hbm(a) for a in list(sends) + list(lands)]
    vmem = pl.BlockSpec(memory_space=pltpu.VMEM)
    out = pl.pallas_call(
        body, name=name,
        out_shape=(pltpu.SemaphoreType.DMA((7 * n,)), pltpu.SemaphoreType.DMA((7 * n,)),
                   *[pltpu.HBM(a.shape, a.dtype) for a in arrays], jax.ShapeDtypeStruct(carry.shape, carry.dtype)),
        in_specs=[_HBM_SPEC] * (2 * n) + [vmem] + ([pl.BlockSpec(memory_space=pl.ANY)] if after is not None else []),
        out_specs=(_SEM_SPEC, _SEM_SPEC, *[_HBM_SPEC] * (2 * n), vmem),
        input_output_aliases={i: 2 + i for i in range(2 * n)},
        compiler_params=pltpu.CompilerParams(has_side_effects=_DATAFLOW),
    )(*arrays, carry, *([after] if after is not None else []))
    return (out[0], out[1]), list(out[2:2 + n]), list(out[2 + n:2 + 2 * n]), out[-1]


def _exchange_wait(started, lands, after, name):
    n = len(lands)
    flat_sends = [s for _, sends, _ in started for s in sends]
    flat_sems = [s for sems, _, _ in started for s in sems]
    ns = len(flat_sends)

    def body(*refs):
        ins, zones = refs[:ns], refs[ns:ns + n]
        sem_refs = refs[ns + n:ns + n + len(flat_sems)]
        me, peers = _peers()
        pos = 0
        for call, (_, sends, row_offs) in enumerate(started):
            sems = sem_refs[2 * call:2 * call + 2]
            for a in range(len(sends)):
                for k, peer in enumerate(peers):
                    cp = _exchange_copy(ins[pos + a], zones[a], row_offs[a], _slot(*peer), _slot(*peer), sems,
                                        7 * a + k, peer)
                    cp.wait_send()
                    cp.wait_recv()
            pos += len(sends)

    arrays = list(flat_sends) + list(lands)
    out = pl.pallas_call(
        body, name=name,
        out_shape=tuple(pltpu.HBM(a.shape, a.dtype) for a in arrays),
        in_specs=[_HBM_SPEC] * len(arrays) + [_SEM_SPEC] * len(flat_sems) + [pl.BlockSpec(memory_space=pl.ANY)],
        out_specs=tuple([_HBM_SPEC] * len(arrays)),
        input_output_aliases={i: i for i in range(len(arrays))},
        compiler_params=pltpu.CompilerParams(has_side_effects=_DATAFLOW),
    )(*arrays, *flat_sems, after)
    return list(out[ns:])


def _bias_table(rel_bias, bucket):
    def body(rel_ref, bucket_ref, out_ref):
        bk = bucket_ref[...]
        for h in range(B_HEADS):
            def pick(b, acc, h=h):
                return jnp.where(bk == b, rel_ref[b, h], acc)
            out_ref[h] = lax.fori_loop(0, REL_BUCKETS, pick, jnp.zeros((BLOCK, BLOCK), F32))

    return pl.pallas_call(
        body, name="bias_table",
        out_shape=jax.ShapeDtypeStruct((B_HEADS, BLOCK, BLOCK), F32),
        in_specs=[pl.BlockSpec(memory_space=pltpu.SMEM), pl.BlockSpec(memory_space=pltpu.VMEM)],
        out_specs=pl.BlockSpec(memory_space=pltpu.VMEM),
    )(rel_bias, bucket)


def _inproj_gmlp_fwd(x2, pre_g3, w, ln_g3, ln_b3, wm, bsb, layer):
    tokens = x2.shape[0]
    tm = _token_tile(tokens)

    def body(x_ref, g_ref, w_ref, lg_ref, lb_ref, wm_ref, bsb_ref, h_ref, pa_ref, pb_ref, ya_ref):
        x = x_ref[...]
        r = lax.rsqrt(jnp.mean(x * x, axis=-1, keepdims=True) + NORM_EPS)
        h = (x * r * g_ref[...]).astype(BF16)
        h_ref[...] = h
        pa_ref[...] = _mm_nt(h, w_ref[0:PA_WIDTH, :]).astype(BF16)
        pb_ref[...] = _mm_nt(h, w_ref[PA_WIDTH:IN_WIDTH, :]).astype(BF16)
        for ci in range(tm // CHUNK):
            rows = slice(ci * CHUNK, (ci + 1) * CHUNK)
            _, _, pz, u, _, _, _, mixed, sg = _gmlp_forward_chunk(
                pa_ref[rows, :], lg_ref[...], lb_ref[...], wm_ref, bsb_ref[...])
            ya_ref[rows, :] = (u * mixed * (pz * sg)).astype(BF16)

    blocks = [_nbytes((tm, D_MODEL), F32), _nbytes((D_MODEL, IN_WIDTH), BF16),
              _nbytes((tm, D_MODEL), BF16), _nbytes((tm, PA_WIDTH), BF16), _nbytes((tm, PB_WIDTH), BF16),
              _nbytes((A_GROUPS, CHUNK, CHUNK), BF16), _nbytes((CHUNK, A_WIDTH), F32), _nbytes((tm, A_WIDTH), BF16)]
    return pl.pallas_call(
        body, name=f"inproj_gmlp_fwd_{layer}", grid=(tokens // tm,),
        in_specs=[pl.BlockSpec((tm, D_MODEL), lambda i: (i, 0)),
                  pl.BlockSpec((None, 1, D_MODEL), lambda i: (layer, 0, 0)),
                  pl.BlockSpec((IN_WIDTH, D_MODEL), lambda i: (0, 0)),
                  pl.BlockSpec((None, 1, A_WIDTH), lambda i: (layer, 0, 0)),
                  pl.BlockSpec((None, 1, A_WIDTH), lambda i: (layer, 0, 0)),
                  pl.BlockSpec((None, A_GROUPS, CHUNK, CHUNK), lambda i: (layer, 0, 0, 0)),
                  pl.BlockSpec((None, CHUNK, A_WIDTH), lambda i: (layer, 0, 0))],
        out_specs=[pl.BlockSpec((tm, D_MODEL), lambda i: (i, 0)),
                   pl.BlockSpec((tm, PA_WIDTH), lambda i: (i, 0)),
                   pl.BlockSpec((tm, PB_WIDTH), lambda i: (i, 0)),
                   pl.BlockSpec((tm, A_WIDTH), lambda i: (i, 0))],
        out_shape=[pltpu.HBM((tokens, D_MODEL), BF16),
                   pltpu.HBM((tokens, PA_WIDTH), BF16),
                   pltpu.HBM((tokens, PB_WIDTH), BF16),
                   pltpu.HBM((tokens, A_WIDTH), BF16)],
        compiler_params=pltpu.CompilerParams(dimension_semantics=("parallel",),
                                             vmem_limit_bytes=_vmem_limit(blocks)),
    )(_hbm(x2), pre_g3, _hbm(w), ln_g3, ln_b3, wm, bsb)


def _gmlp_forward_chunk(pa, ln_g, ln_b, wm_ref, bsb):
    pu = pa[:, 0:A_WIDTH].astype(F32)
    pv = pa[:, A_WIDTH:2 * A_WIDTH].astype(F32)
    pz = pa[:, 2 * A_WIDTH:3 * A_WIDTH].astype(F32)
    u, gu = _gelu_and_grad(pu)
    vv, gv = _gelu_and_grad(pv)
    mu = jnp.mean(vv, axis=-1, keepdims=True)
    xc = vv - mu
    rstd = lax.rsqrt(jnp.mean(xc * xc, axis=-1, keepdims=True) + NORM_EPS)
    vhat = xc * rstd
    vnb = (vhat * ln_g + ln_b).astype(BF16)
    low = lax.broadcasted_iota(jnp.int32, (CHUNK, LANES), 1) < HEAD_DIM
    parts = []
    for p in range(A_GROUPS // 2):
        vp = vnb[:, LANES * p:LANES * (p + 1)]
        parts.append(jnp.where(low, _mm(wm_ref[2 * p], vp), _mm(wm_ref[2 * p + 1], vp)))
    mixed = jnp.concatenate(parts, axis=1) + bsb
    sg = _sigmoid(pz)
    return gu, gv, pz, u, vhat, rstd, vnb, mixed, sg


def _kv_variants(kv):
    t = kv.astype(F32)
    rolled = pltpu.roll(t, HEAD_DIM, 1)
    low = lax.broadcasted_iota(jnp.int32, t.shape, 1) < HEAD_DIM
    zero = jnp.zeros_like(t)
    head0 = (jnp.where(low, t, zero).astype(BF16), jnp.where(low, zero, rolled).astype(BF16))
    head1 = (jnp.where(low, rolled, zero).astype(BF16), jnp.where(low, zero, t).astype(BF16))
    return head0, head1


def _band_masks():
    row = lax.broadcasted_iota(jnp.int32, (BLOCK, BLOCK), 0)
    col = lax.broadcasted_iota(jnp.int32, (BLOCK, BLOCK), 1)
    return col <= row


def _wrap(full, tri):
    return jnp.where(tri, full[:, BLOCK:2 * BLOCK], full[:, 0:BLOCK])


def _attn_probs(sf, bias_h, sink, tri, kill):
    s = _wrap(sf, tri) * SCALE + bias_h
    s = jnp.where(kill, NEG, s)
    m = jnp.maximum(jnp.max(s, axis=-1, keepdims=True), sink)
    e = jnp.exp(s - m)
    es = jnp.exp(sink - m)
    inv = 1.0 / (jnp.sum(e, axis=-1, keepdims=True) + es)
    return e * inv, es * inv


def _unwrap16(p, tri):
    zero = jnp.zeros_like(p)
    return jnp.concatenate([jnp.where(tri, zero, p), jnp.where(tri, p, zero)], axis=1).astype(BF16)


def _fill_kv(kv_ref, prev_ref, cur_ref):
    kv_ref[0:BLOCK, :] = prev_ref[...]
    kv_ref[BLOCK:, :] = cur_ref[:, K_OFF:K_OFF + 2 * KV_WIDTH]


def _attn_outproj_fwd(pb, bias, sinks, ya, wo, x2, post_g3, b_loc, layer, target=None):
    tokens = pb.shape[0]
    nb = tokens // b_loc // BLOCK
    tq = _token_tile(tokens)
    per_tile = tq // BLOCK
    steps = tokens // tq
    with_loss = target is not None
    half = D_MODEL // 2

    def body(*refs):
        sink_ref, cur_ref, prev_ref, bias_ref, ya_ref, woa_ref, wob_ref, x_ref, g_ref = refs[:9]
        if with_loss:
            t_ref, yb_ref, pw_ref, spw_ref, y_ref, grad_ref, loss_ref, kv_ref, acc_ref = refs[9:]
        else:
            yb_ref, pw_ref, spw_ref, y_ref, xn_ref, kv_ref = refs[9:]
        t = pl.program_id(0)
        _fill_kv(kv_ref, prev_ref, cur_ref)
        tri = _band_masks()
        head_lane = lax.broadcasted_iota(jnp.int32, (BLOCK, LANES), 1)
        y_from_a = _mm(ya_ref[...], woa_ref[...])

        def block(i, carry):
            start = i * BLOCK
            rows = pl.ds(start, BLOCK)
            first = lax.rem(t * per_tile + i, nb) == 0
            kill = jnp.logical_and(first, jnp.logical_not(tri))
            kv = kv_ref[pl.ds(start, 2 * BLOCK), :]
            k_ops = _kv_variants(kv[:, 0:KV_WIDTH])
            v_ops = _kv_variants(kv[:, KV_WIDTH:2 * KV_WIDTH])
            p16 = {}
            sink_cols = jnp.zeros((BLOCK, LANES), F32)
            for kvh in range(2):
                pairs = (2 * kvh, 2 * kvh + 1)
                qs = jnp.concatenate([cur_ref[rows, LANES * p:LANES * (p + 1)] for p in pairs], axis=0)
                for j in range(2):
                    sf = _mm_nt(qs, k_ops[kvh][j])
                    for r, p in enumerate(pairs):
                        hd = 2 * p + j
                        probs, sink_p = _attn_probs(sf[BLOCK * r:BLOCK * (r + 1)], bias_ref[hd],
                                                    sink_ref[layer, hd], tri, kill)
                        p16[hd] = _unwrap16(probs, tri)
                        pw_ref[rows, LANES * hd:LANES * (hd + 1)] = probs.astype(BF16)
                        sink_cols = jnp.where(head_lane == hd, sink_p, sink_cols)
            spw_ref[rows, :] = sink_cols
            for kvh in range(2):
                pairs = (2 * kvh, 2 * kvh + 1)
                out = jnp.zeros((2 * BLOCK, LANES), F32)
                for j in range(2):
                    out = out + _mm(jnp.concatenate([p16[2 * p + j] for p in pairs], axis=0), v_ops[kvh][j])
                for r, p in enumerate(pairs):
                    bz = cur_ref[rows, BZ_OFF + LANES * p:BZ_OFF + LANES * (p + 1)].astype(F32)
                    yb_ref[rows, LANES * p:LANES * (p + 1)] = (
                        out[BLOCK * r:BLOCK * (r + 1)] * (bz * _sigmoid(bz))).astype(BF16)
            return carry

        for i in range(per_tile):
            block(i, 0)

        y = y_from_a + _mm(yb_ref[...], wob_ref[...])
        r = lax.rsqrt(jnp.mean(y * y, axis=-1, keepdims=True) + NORM_EPS)
        x_next = x_ref[...] + y * r * g_ref[...]
        y_ref[...] = y.astype(BF16)
        if not with_loss:
            xn_ref[...] = x_next
            return

        @pl.when(t == 0)
        def _():
            acc_ref[...] = jnp.zeros_like(acc_ref)

        err = x_next - t_ref[...]
        grad_ref[...] = err * (1.0 / D_MODEL)
        acc_ref[...] += jnp.sum(err * err, axis=0, keepdims=True)

        @pl.when(t == steps - 1)
        def _():
            total = jnp.sum(acc_ref[...], axis=-1, keepdims=True) * (0.5 / D_MODEL)
            loss_ref[...] = jnp.broadcast_to(total, loss_ref.shape)

    tile = pl.BlockSpec((tq, D_MODEL), lambda t: (t, 0))
    blocks = [_nbytes((tq, PB_WIDTH), BF16), _nbytes((BLOCK, 2 * KV_WIDTH), BF16),
              _nbytes((B_HEADS, BLOCK, BLOCK), F32), 2 * _nbytes((tq, B_WIDTH), BF16),
              2 * _nbytes((half, D_MODEL), BF16), 3 * _nbytes((tq, D_MODEL), F32), _nbytes((tq, D_MODEL), BF16),
              _nbytes((tq, B_HEADS * BLOCK), BF16), _nbytes((tq, LANES), F32)]
    scratch = _nbytes((tq + BLOCK, 2 * KV_WIDTH), BF16)
    in_specs = [pl.BlockSpec(memory_space=pltpu.SMEM),
                pl.BlockSpec((tq, PB_WIDTH), lambda t: (t, 0)),
                pl.BlockSpec((BLOCK, 2 * KV_WIDTH),
                             lambda t: (jnp.maximum(t * per_tile - 1, 0), K_OFF // (2 * KV_WIDTH))),
                pl.BlockSpec((B_HEADS, BLOCK, BLOCK), lambda t: (0, 0, 0)),
                pl.BlockSpec((tq, half), lambda t: (t, 0)),
                pl.BlockSpec((half, D_MODEL), lambda t: (0, 0)),
                pl.BlockSpec((half, D_MODEL), lambda t: (1, 0)),
                tile,
                pl.BlockSpec((None, 1, D_MODEL), lambda t: (layer, 0, 0))]
    out_specs = [pl.BlockSpec((tq, B_WIDTH), lambda t: (t, 0)), pl.BlockSpec((tq, B_HEADS * BLOCK), lambda t: (t, 0)),
                 pl.BlockSpec((tq, LANES), lambda t: (t, 0)), tile, tile]
    out_shape = [pltpu.HBM((tokens, B_WIDTH), BF16), pltpu.HBM((tokens, B_HEADS * BLOCK), BF16),
                 pltpu.HBM((tokens, LANES), F32), pltpu.HBM((tokens, D_MODEL), BF16),
                 pltpu.HBM((tokens, D_MODEL), F32)]
    scratch_shapes = [pltpu.VMEM((tq + BLOCK, 2 * KV_WIDTH), BF16)]
    operands = [sinks, _hbm(pb), _hbm(pb), _hbm(bias), _hbm(ya), _hbm(wo), _hbm(wo), _hbm(x2), post_g3]
    if with_loss:
        in_specs.append(tile)
        out_specs.append(pl.BlockSpec((1, LANES), lambda t: (0, 0)))
        out_shape.append(pltpu.HBM((1, LANES), F32))
        scratch_shapes.append(pltpu.VMEM((1, D_MODEL), F32))
        operands.append(_hbm(target))
    return pl.pallas_call(
        body, name=f"attn_outproj_fwd_{layer}", grid=(steps,),
        in_specs=in_specs, out_specs=out_specs, out_shape=out_shape, scratch_shapes=scratch_shapes,
        compiler_params=pltpu.CompilerParams(dimension_semantics=("arbitrary" if with_loss else "parallel",),
                                             vmem_limit_bytes=_vmem_limit(blocks, scratch)),
    )(*operands)


def _outproj_bwd(g, y, ya, yb, wo, post_g3, layer):
    tokens = g.shape[0]
    tm = _token_tile(tokens)
    half = D_MODEL // 2
    steps = tokens // tm

    def body(g_ref, y_ref, ya_ref, yb_ref, woa_ref, wob_ref, pg_ref, dya_ref, dyb_ref, dwo16_ref, dpg_ref, dwo_ref):
        @pl.when(pl.program_id(0) == 0)
        def _():
            dwo_ref[...] = jnp.zeros_like(dwo_ref)
            dpg_ref[...] = jnp.zeros_like(dpg_ref)

        gv = g_ref[...]
        yf = y_ref[...].astype(F32)
        r = lax.rsqrt(jnp.mean(yf * yf, axis=-1, keepdims=True) + NORM_EPS)
        yhat = yf * r
        gg = gv * pg_ref[...]
        dy = r * (gg - yhat * jnp.mean(gg * yhat, axis=-1, keepdims=True))
        dpg_ref[...] += jnp.sum(gv * yhat, axis=0, keepdims=True)
        dyb16 = dy.astype(BF16)
        dya_ref[...] = _mm_nt(dyb16, woa_ref[...]).astype(BF16)
        dyb_ref[...] = _mm_nt(dyb16, wob_ref[...]).astype(BF16)
        dwo_ref[0:half, :] += _mm_tn(ya_ref[...], dyb16)
        dwo_ref[half:D_MODEL, :] += _mm_tn(yb_ref[...], dyb16)

        @pl.when(pl.program_id(0) == steps - 1)
        def _():
            dwo16_ref[...] = dwo_ref[...].astype(BF16)

    blocks = [_nbytes((tm, D_MODEL), F32), _nbytes((tm, D_MODEL), BF16), 4 * _nbytes((tm, half), BF16),
              2 * _nbytes((half, D_MODEL), BF16), _nbytes((D_MODEL, D_MODEL), BF16)]
    return pl.pallas_call(
        body, name=f"outproj_bwd_{layer}", grid=(tokens // tm,),
        in_specs=[pl.BlockSpec((tm, D_MODEL), lambda i: (i, 0)),
                  pl.BlockSpec((tm, D_MODEL), lambda i: (i, 0)),
                  pl.BlockSpec((tm, half), lambda i: (i, 0)),
                  pl.BlockSpec((tm, half), lambda i: (i, 0)),
                  pl.BlockSpec((half, D_MODEL), lambda i: (0, 0)),
                  pl.BlockSpec((half, D_MODEL), lambda i: (1, 0)),
                  pl.BlockSpec((None, 1, D_MODEL), lambda i: (layer, 0, 0))],
        out_specs=[pl.BlockSpec((tm, half), lambda i: (i, 0)),
                   pl.BlockSpec((tm, half), lambda i: (i, 0)),
                   pl.BlockSpec((D_MODEL, D_MODEL), lambda i: (0, 0)),
                   pl.BlockSpec((1, D_MODEL), lambda i: (0, 0))],
        out_shape=[pltpu.HBM((tokens,half), BF16),
                   pltpu.HBM((tokens,half), BF16),
                   pltpu.HBM((D_MODEL,D_MODEL), BF16),
                   pltpu.HBM((1,D_MODEL), F32)],
        scratch_shapes=[pltpu.VMEM((D_MODEL, D_MODEL), F32)],
        compiler_params=pltpu.CompilerParams(dimension_semantics=("arbitrary",),
                                             vmem_limit_bytes=_vmem_limit(blocks, _nbytes((D_MODEL, D_MODEL), F32))),
    )(_hbm(g), _hbm(y), _hbm(ya), _hbm(yb), _hbm(wo), _hbm(wo), post_g3)


def _gmlp_bwd_dw(pa, dya, h, dqz, dkv, ln_g3, ln_b3, wm, wmt, bsb, layer):
    tokens = pa.shape[0]
    tc = _token_tile(tokens)
    steps = tokens // tc

    def body(pa_ref, dya_ref, h_ref, dqz_ref, dkv_ref, lg_ref, lb_ref, wm_ref, wmt_ref, bsb_ref,
             da_ref, dw_ref, dws_ref, dbs_ref, dlg_ref, dlb_ref, acc_ref, dbsb_ref):
        i = pl.program_id(0)

        @pl.when(i == 0)
        def _():
            acc_ref[...] = jnp.zeros_like(acc_ref)
            dws_ref[...] = jnp.zeros_like(dws_ref)
            dlg_ref[...] = jnp.zeros_like(dlg_ref)
            dlb_ref[...] = jnp.zeros_like(dlb_ref)
            dbsb_ref[...] = jnp.zeros_like(dbsb_ref)

        ln_g = lg_ref[...]
        low = lax.broadcasted_iota(jnp.int32, (CHUNK, LANES), 1) < HEAD_DIM
        hv = h_ref[...]
        acc_ref[:, PA_WIDTH:PA_WIDTH + B_WIDTH] += _mm_tn(hv, dqz_ref[:, 0:B_WIDTH])
        acc_ref[:, PA_WIDTH + K_OFF:PA_WIDTH + BZ_OFF] += _mm_tn(hv, dkv_ref[...])
        acc_ref[:, PA_WIDTH + BZ_OFF:IN_WIDTH] += _mm_tn(hv, dqz_ref[:, B_WIDTH:2 * B_WIDTH])

        for ci in range(tc // CHUNK):
            rows = slice(ci * CHUNK, (ci + 1) * CHUNK)
            gu, gv, pz, u, vhat, rstd, vnb, mixed, sg = _gmlp_forward_chunk(
                pa_ref[rows, :], ln_g, lb_ref[...], wm_ref, bsb_ref[...])
            dy = dya_ref[rows, :].astype(F32)
            sz = pz * sg
            dy_sz = dy * sz
            du = dy_sz * mixed
            dmixed = dy_sz * u
            dz = dy * (u * mixed) * (sg + sz * (1.0 - sg))
            dbsb_ref[...] += dmixed
            dmb = dmixed.astype(BF16)
            zero = jnp.zeros((CHUNK, LANES), BF16)
            parts = []
            for p in range(A_GROUPS // 2):
                dmp = dmb[:, LANES * p:LANES * (p + 1)]
                vp = vnb[:, LANES * p:LANES * (p + 1)]
                parts.append(jnp.where(low, _mm(wmt_ref[2 * p], dmp), _mm(wmt_ref[2 * p + 1], dmp)))
                dws_ref[2 * p] += _mm_nt(jnp.where(low, dmp, zero), vp)
                dws_ref[2 * p + 1] += _mm_nt(jnp.where(low, zero, dmp), vp)
            dvn = jnp.concatenate(parts, axis=1)
            dlg_ref[...] += jnp.sum(dvn * vhat, axis=0, keepdims=True)
            dlb_ref[...] += jnp.sum(dvn, axis=0, keepdims=True)
            dvh = dvn * ln_g
            dvv = rstd * (dvh - jnp.mean(dvh, axis=-1, keepdims=True)
                          - vhat * jnp.mean(dvh * vhat, axis=-1, keepdims=True))
            da_ref[rows, 0:A_WIDTH] = (du * gu).astype(BF16)
            da_ref[rows, A_WIDTH:2 * A_WIDTH] = (dvv * gv).astype(BF16)
            da_ref[rows, 2 * A_WIDTH:3 * A_WIDTH] = dz.astype(BF16)

        acc_ref[:, 0:PA_WIDTH] += _mm_tn(hv, da_ref[...])

        @pl.when(i == steps - 1)
        def _():
            for c in range(IN_WIDTH // LANES):
                cols = slice(LANES * c, LANES * (c + 1))
                dw_ref[cols, :] = acc_ref[:, cols].T.astype(BF16)
            causal =(lax.broadcasted_iota(jnp.int32, (CHUNK, CHUNK), 0)
                      >= lax.broadcasted_iota(jnp.int32, (CHUNK, CHUNK), 1))
            for h in range(A_GROUPS):
                dws_ref[h] = jnp.where(causal, dws_ref[h], 0.0)
            acc = dbsb_ref[...]
            lane_full = lax.broadcasted_iota(jnp.int32, (CHUNK, A_WIDTH), 1)
            lane_out = lax.broadcasted_iota(jnp.int32, (CHUNK, LANES), 1)
            out = jnp.zeros((CHUNK, LANES), F32)
            for h in range(A_GROUPS):
                in_group = jnp.logical_and(lane_full >= HEAD_DIM * h, lane_full < HEAD_DIM * (h + 1))
                s = jnp.sum(jnp.where(in_group, acc, 0.0), axis=-1, keepdims=True)
                out = jnp.where(lane_out == h, s, out)
            dbs_ref[...] = out

    blocks = [_nbytes((tc, PA_WIDTH), BF16), _nbytes((tc, A_WIDTH), BF16), 2 * _nbytes((A_GROUPS, CHUNK, CHUNK), BF16),
              _nbytes((CHUNK, A_WIDTH), F32), _nbytes((tc, PA_WIDTH), BF16), _nbytes((A_GROUPS, CHUNK, CHUNK), F32),
              _nbytes((CHUNK, LANES), F32), _nbytes((tc, D_MODEL), BF16), _nbytes((tc, 2 * B_WIDTH), BF16),
              _nbytes((tc, 2 * KV_WIDTH), BF16), _nbytes((D_MODEL, IN_WIDTH), BF16)]
    scratch = _nbytes((D_MODEL, IN_WIDTH), F32) + _nbytes((CHUNK, A_WIDTH), F32)
    return pl.pallas_call(
        body, name=f"gmlp_bwd_dw_{layer}", grid=(steps,),
        in_specs=[pl.BlockSpec((tc, PA_WIDTH), lambda i: (i, 0)),
                  pl.BlockSpec((tc, A_WIDTH), lambda i: (i, 0)),
                  pl.BlockSpec((tc, D_MODEL), lambda i: (i, 0)),
                  pl.BlockSpec((tc, 2 * B_WIDTH), lambda i: (i, 0)),
                  pl.BlockSpec((pl.Element(tc), pl.Element(2 * KV_WIDTH)),
                               lambda i: (pl.multiple_of(i * tc + BLOCK, BLOCK), 0)),
                  pl.BlockSpec((None, 1, A_WIDTH), lambda i: (layer, 0, 0)),
                  pl.BlockSpec((None, 1, A_WIDTH), lambda i: (layer, 0, 0)),
                  pl.BlockSpec((None, A_GROUPS, CHUNK, CHUNK), lambda i: (layer, 0, 0, 0)),
                  pl.BlockSpec((None, A_GROUPS, CHUNK, CHUNK), lambda i: (layer, 0, 0, 0)),
                  pl.BlockSpec((None, CHUNK, A_WIDTH), lambda i: (layer, 0, 0))],
        out_specs=[pl.BlockSpec((tc, PA_WIDTH), lambda i: (i, 0)),
                   pl.BlockSpec((IN_WIDTH, D_MODEL), lambda i: (0, 0)),
                   pl.BlockSpec((A_GROUPS, CHUNK, CHUNK), lambda i: (0, 0, 0)),
                   pl.BlockSpec((CHUNK, LANES), lambda i: (0, 0)),
                   pl.BlockSpec((1, A_WIDTH), lambda i: (0, 0)),
                   pl.BlockSpec((1, A_WIDTH), lambda i: (0, 0))],
        out_shape=[pltpu.HBM((tokens,PA_WIDTH), BF16),
                   pltpu.HBM((IN_WIDTH, D_MODEL), BF16),
                   pltpu.HBM((A_GROUPS, CHUNK, CHUNK), F32),
                   pltpu.HBM((CHUNK, LANES), F32),
                   pltpu.HBM((1,A_WIDTH), F32),
                   pltpu.HBM((1,A_WIDTH), F32)],
        scratch_shapes=[pltpu.VMEM((D_MODEL, IN_WIDTH), F32), pltpu.VMEM((CHUNK, A_WIDTH), F32)],
        compiler_params=pltpu.CompilerParams(dimension_semantics=("arbitrary",),
                                             vmem_limit_bytes=_vmem_limit(blocks, scratch)),
    )(_hbm(pa), _hbm(dya), _hbm(h), _hbm(dqz), _hbm(dkv), ln_g3, ln_b3, wm, wmt, bsb)


def _attn_bwd(pb, dyb, probs_w, sink_w, dbias_in, layer):
    tokens = pb.shape[0]
    qz_width = 2 * B_WIDTH
    tq = _token_tile(tokens)
    per_tile = tq // BLOCK
    nt = tokens // tq

    def body(cur_ref, prev_ref, dyb_ref, pw_ref, spw_ref, dbias_in_ref, dqz_ref, dkv_ref, dbias_ref, dsink_ref,
             kv_ref, acc_ref):
        t = pl.program_id(0)

        @pl.when(t == 0)
        def _():
            dbias_ref[...] = dbias_in_ref[...]
            dsink_ref[...] = jnp.zeros_like(dsink_ref)
            acc_ref[0:BLOCK, :] = jnp.zeros((BLOCK, 2 * KV_WIDTH), F32)

        @pl.when(t < nt)
        def _():
            acc_ref[BLOCK:, :] = jnp.zeros((tq, 2 * KV_WIDTH), F32)
            _fill_kv(kv_ref, prev_ref, cur_ref)
            tri = _band_masks()
            low = lax.broadcasted_iota(jnp.int32, (BLOCK, LANES), 1) < HEAD_DIM
            low_kv = lax.broadcasted_iota(jnp.int32, (2 * BLOCK, LANES), 1) < HEAD_DIM

            head_lane = lax.broadcasted_iota(jnp.int32, (BLOCK, LANES), 1)

            def block(i, carry):
                start = i * BLOCK
                rows = pl.ds(start, BLOCK)
                kv = kv_ref[pl.ds(start, 2 * BLOCK), :]
                k_ops = _kv_variants(kv[:, 0:KV_WIDTH])
                v_ops = _kv_variants(kv[:, KV_WIDTH:2 * KV_WIDTH])
                probs, p16, qs, dos, delta, ds16 = {}, {}, {}, {}, {}, {}
                for hd in range(B_HEADS):
                    probs[hd] = pw_ref[rows, LANES * hd:LANES * (hd + 1)].astype(F32)
                    p16[hd] = _unwrap16(probs[hd], tri)
                for kvh in range(2):
                    qs[kvh] = jnp.concatenate(
                        [cur_ref[rows, LANES * p:LANES * (p + 1)] for p in (2 * kvh, 2 * kvh + 1)], axis=0)
                for kvh in range(2):
                    pairs = (2 * kvh, 2 * kvh + 1)
                    out = jnp.zeros((2 * BLOCK, LANES), F32)
                    for j in range(2):
                        out = out + _mm(jnp.concatenate([p16[2 * p + j] for p in pairs], axis=0), v_ops[kvh][j])
                    d_outs = []
                    for r, p in enumerate(pairs):
                        bz = cur_ref[rows, BZ_OFF + LANES * p:BZ_OFF + LANES * (p + 1)].astype(F32)
                        sg = _sigmoid(bz)
                        dyp = dyb_ref[rows, LANES * p:LANES * (p + 1)].astype(F32)
                        out_p = out[BLOCK * r:BLOCK * (r + 1)]
                        d_out = dyp * (bz * sg)
                        dqz_ref[rows, B_WIDTH + LANES * p:B_WIDTH + LANES * (p + 1)] = (
                            dyp * out_p * (sg * (1.0 + bz * (1.0 - sg)))).astype(BF16)
                        dod = d_out * out_p
                        delta[2 * p] = jnp.sum(jnp.where(low, dod, 0.0), axis=-1, keepdims=True)
                        delta[2 * p + 1] = jnp.sum(jnp.where(low, 0.0, dod), axis=-1, keepdims=True)
                        d_outs.append(d_out.astype(BF16))
                    dos[kvh] = jnp.concatenate(d_outs, axis=0)
                delta_cols = jnp.zeros((BLOCK, LANES), F32)
                for hd in range(B_HEADS):
                    delta_cols = jnp.where(head_lane == hd, delta[hd], delta_cols)
                dsink_ref[0:1, :] += jnp.sum(-(spw_ref[rows, :] * delta_cols), axis=0, keepdims=True)
                for kvh in range(2):
                    pairs = (2 * kvh, 2 * kvh + 1)
                    for j in range(2):
                        dpf = _mm_nt(dos[kvh], v_ops[kvh][j])
                        for r, p in enumerate(pairs):
                            hd = 2 * p + j
                            ds = probs[hd] * (_wrap(dpf[BLOCK * r:BLOCK * (r + 1)], tri) - delta[hd])
                            dbias_ref[hd] += ds
                            ds16[hd] = _unwrap16(ds, tri)
                dk_acc = [[None, None], [None, None]]
                dv_acc = [[None, None], [None, None]]
                for kvh in range(2):
                    pairs = (2 * kvh, 2 * kvh + 1)
                    dq = jnp.zeros((2 * BLOCK, LANES), F32)
                    for j in range(2):
                        dss = jnp.concatenate([ds16[2 * p + j] for p in pairs], axis=0)
                        pss = jnp.concatenate([p16[2 * p + j] for p in pairs], axis=0)
                        dq = dq + _mm(dss, k_ops[kvh][j])
                        dk_acc[kvh][j] = _mm_tn(dss, qs[kvh])
                        dv_acc[kvh][j] = _mm_tn(pss, dos[kvh])
                    for r, p in enumerate(pairs):
                        dqz_ref[rows, LANES * p:LANES * (p + 1)] = (dq[BLOCK * r:BLOCK * (r + 1)] * SCALE).astype(BF16)

                def fold(acc):
                    return jnp.where(low_kv,
                                     acc[0][0] + pltpu.roll(acc[0][1], HEAD_DIM, 1),
                                     pltpu.roll(acc[1][0], HEAD_DIM, 1) + acc[1][1])

                acc_ref[pl.ds(start, 2 * BLOCK), :] += jnp.concatenate(
                    [fold(dk_acc) * SCALE, fold(dv_acc)], axis=1)
                return carry

            for i in range(per_tile):
                block(i, 0)
            dkv_ref[...] = acc_ref[0:tq, :].astype(BF16)
            acc_ref[0:BLOCK, :] = acc_ref[tq:tq + BLOCK, :]

        @pl.when(t == nt)
        def _():
            dkv_ref[0:BLOCK, :] = acc_ref[0:BLOCK, :].astype(BF16)
            dkv_ref[BLOCK:, :] = jnp.zeros((tq - BLOCK, 2 * KV_WIDTH), BF16)

    def cur_map(t):
        return (jnp.minimum(t, nt - 1), 0)

    def prev_map(t):
        return (jnp.maximum(jnp.minimum(t, nt - 1) * per_tile - 1, 0), K_OFF // (2 * KV_WIDTH))

    blocks = [_nbytes((tq, PB_WIDTH), BF16), _nbytes((BLOCK, 2 * KV_WIDTH), BF16), _nbytes((tq, B_WIDTH), BF16),
              2 * _nbytes((B_HEADS, BLOCK, BLOCK), F32), _nbytes((tq, qz_width), BF16),
              _nbytes((tq, 2 * KV_WIDTH), BF16), _nbytes((B_HEADS, LANES), F32),
              _nbytes((tq, B_HEADS * BLOCK), BF16), _nbytes((tq, LANES), F32)]
    scratch = _nbytes((tq + BLOCK, 2 * KV_WIDTH), BF16) + _nbytes((tq + BLOCK, 2 * KV_WIDTH), F32)
    return pl.pallas_call(
        body, name=f"attn_bwd_{layer}", grid=(nt + 1,),
        in_specs=[pl.BlockSpec((tq, PB_WIDTH), cur_map),
                  pl.BlockSpec((BLOCK, 2 * KV_WIDTH), prev_map),
                  pl.BlockSpec((tq, B_WIDTH), cur_map),
                  pl.BlockSpec((tq, B_HEADS * BLOCK), cur_map),
                  pl.BlockSpec((tq, LANES), cur_map),
                  pl.BlockSpec((B_HEADS, BLOCK, BLOCK), lambda t: (0, 0, 0))],
        out_specs=[pl.BlockSpec((tq, qz_width), cur_map),
                   pl.BlockSpec((tq, 2 * KV_WIDTH), lambda t: (t, 0)),
                   pl.BlockSpec((B_HEADS, BLOCK, BLOCK), lambda t: (0, 0, 0)),
                   pl.BlockSpec((B_HEADS, LANES), lambda t: (0, 0))],
        out_shape=[pltpu.HBM((tokens, qz_width), BF16),
                   pltpu.HBM((tokens + tq, 2 * KV_WIDTH), BF16),
                   pltpu.HBM((B_HEADS, BLOCK, BLOCK), F32),
                   pltpu.HBM((B_HEADS, LANES), F32)],
        scratch_shapes=[pltpu.VMEM((tq + BLOCK, 2 * KV_WIDTH), BF16),
                        pltpu.VMEM((tq + BLOCK, 2 * KV_WIDTH), F32)],
        compiler_params=pltpu.CompilerParams(dimension_semantics=("arbitrary",),
                                             vmem_limit_bytes=_vmem_limit(blocks, scratch)),
    )(_hbm(pb), _hbm(pb), _hbm(dyb), _hbm(probs_w), _hbm(sink_w), _hbm(dbias_in))


def _inproj_bwd_dx(da, dqz, dkv, w, x2, g, pre_g3, layer):
    tokens = x2.shape[0]
    tm = _token_tile(tokens)

    def body(da_ref, dqz_ref, dkv_ref, w_ref, x_ref, g_ref, pg_ref, gn_ref, dpg_ref):
        @pl.when(pl.program_id(0) == 0)
        def _():
            dpg_ref[...] = jnp.zeros_like(dpg_ref)

        dh = _mm(da_ref[...], w_ref[0:PA_WIDTH, :])
        dh += _mm(dqz_ref[:, 0:B_WIDTH], w_ref[PA_WIDTH:PA_WIDTH + B_WIDTH, :])
        dh += _mm(dkv_ref[...], w_ref[PA_WIDTH + K_OFF:PA_WIDTH + BZ_OFF, :])
        dh += _mm(dqz_ref[:, B_WIDTH:2 * B_WIDTH], w_ref[PA_WIDTH + BZ_OFF:IN_WIDTH, :])
        x = x_ref[...]
        r = lax.rsqrt(jnp.mean(x * x, axis=-1, keepdims=True) + NORM_EPS)
        xhat = x * r
        dhg = dh * pg_ref[...]
        dpg_ref[...] += jnp.sum(dh * xhat, axis=0, keepdims=True)
        gn_ref[...] = g_ref[...] + r * (dhg - xhat * jnp.mean(dhg * xhat, axis=-1, keepdims=True))

    blocks = [_nbytes((tm, PA_WIDTH), BF16), _nbytes((tm, 2 * B_WIDTH), BF16), _nbytes((tm, 2 * KV_WIDTH), BF16),
              _nbytes((D_MODEL, IN_WIDTH), BF16), 3 * _nbytes((tm, D_MODEL), F32)]
    return pl.pallas_call(
        body, name=f"inproj_bwd_dx_{layer}", grid=(tokens // tm,),
        in_specs=[pl.BlockSpec((tm, PA_WIDTH), lambda i: (i, 0)),
                  pl.BlockSpec((tm, 2 * B_WIDTH), lambda i: (i, 0)),
                  pl.BlockSpec((pl.Element(tm), pl.Element(2 * KV_WIDTH)), lambda i: (pl.multiple_of(i * tm + BLOCK, BLOCK), 0)),
                  pl.BlockSpec((IN_WIDTH, D_MODEL), lambda i: (0, 0)),
                  pl.BlockSpec((tm, D_MODEL), lambda i: (i, 0)),
                  pl.BlockSpec((tm, D_MODEL), lambda i: (i, 0)),
                  pl.BlockSpec((None, 1, D_MODEL), lambda i: (layer, 0, 0))],
        out_specs=[pl.BlockSpec((tm, D_MODEL), lambda i: (i, 0)),
                   pl.BlockSpec((1, D_MODEL), lambda i: (0, 0))],
        out_shape=[pltpu.HBM((tokens,D_MODEL), F32),
                   pltpu.HBM((1,D_MODEL), F32)],
        compiler_params=pltpu.CompilerParams(dimension_semantics=("arbitrary",),
                                             vmem_limit_bytes=_vmem_limit(blocks)),
    )(_hbm(da), _hbm(dqz), _hbm(dkv), _hbm(w), _hbm(x2), _hbm(g), pre_g3)


def _rel_bias_grad(dbias, col_bucket, flip):
    def body(db_ref, cb_ref, flip_ref, out_ref):
        cb = cb_ref[...]
        anti = flip_ref[...]
        sums = []
        for h in range(B_HEADS):
            x = db_ref[h]
            hi = x.astype(BF16)
            rest = x - hi.astype(F32)
            mid = rest.astype(BF16)
            low = (rest - mid.astype(F32)).astype(BF16)
            reversed_x = _mm(hi, anti) + _mm(mid, anti) + _mm(low, anti)
            rolled = pltpu.roll(reversed_x, 0, 1, stride=1, stride_axis=0)
            sums.append(jnp.sum(rolled, axis=0, keepdims=True))
        per_dist = jnp.concatenate(sums, axis=0)
        lane = lax.broadcasted_iota(jnp.int32, (B_HEADS, LANES), 1)
        out = jnp.zeros((B_HEADS, LANES), F32)
        for b in range(REL_BUCKETS):
            s = jnp.sum(jnp.where(cb == b, per_dist, 0.0), axis=-1, keepdims=True)
            out = jnp.where(lane == b, s, out)
        out_ref[...] = out

    vm = pl.BlockSpec(memory_space=pltpu.VMEM)
    return pl.pallas_call(
        body, name="rel_bias_grad",
        out_shape=jax.ShapeDtypeStruct((B_HEADS, LANES), F32),
        in_specs=[vm, vm, vm], out_specs=vm,
    )(dbias, col_bucket, flip)


def _reduce_adamw(slots, w, m, v, name):
    rows, cols = w.shape
    tr = _row_tile(rows)
    c1 = 1.0 / (1.0 - ADAM_B1 ** ADAM_STEP)
    c2 = 1.0 / (1.0 - ADAM_B2 ** ADAM_STEP)

    def body(s_ref, w_ref, m_ref, v_ref, g_ref, d_ref, nm_ref, nv_ref):
        g = s_ref[0].astype(F32)
        for i in range(1, N_DEV):
            g = g + s_ref[i].astype(F32)
        nm = ADAM_B1 * m_ref[...] + (1.0 - ADAM_B1) * g
        nv = ADAM_B2 * v_ref[...] + (1.0 - ADAM_B2) * (g * g)
        g_ref[...] = g
        nm_ref[...] = nm
        nv_ref[...] = nv
        d_ref[...] = -ADAM_LR * ((nm * c1) / (jnp.sqrt(nv * c2) + ADAM_EPS) + ADAM_WD * w_ref[...])

    blocks = [_nbytes((N_DEV, tr, cols), slots.dtype), 7 * _nbytes((tr, cols), F32)]
    tile = pl.BlockSpec((tr, cols), lambda i: (i, 0))
    return pl.pallas_call(
        body, name=name, grid=(rows // tr,),
        in_specs=[pl.BlockSpec((N_DEV, tr, cols), lambda i: (0, i, 0)), tile, tile, tile],
        out_specs=[tile] * 4,
        out_shape=[pltpu.HBM((rows, cols), F32)] * 4,
        compiler_params=pltpu.CompilerParams(dimension_semantics=("parallel",),
                                             vmem_limit_bytes=_vmem_limit(blocks)),
    )(_hbm(slots), _hbm(w), _hbm(m), _hbm(v))


_SMALL = ("pre_norm_g","post_norm_g", "ln_v_g", "ln_v_b", "b_spatial", "sinks", "rel_bias")


def _pack_small(parts):
    slabs = []
    for name in _SMALL:
        flat = parts[name].astype(F32).reshape(-1)
        pad = (-flat.shape[0]) % (8 * LANES)
        slabs.append(jnp.pad(flat, (0, pad)).reshape(-1, LANES))
    return jnp.concatenate(slabs, axis=0)


def _unpack_small(slab, shapes):
    out, row = {}, 0
    for name in _SMALL:
        size = int(np.prod(shapes[name]))
        rows = -(-size // (8 * LANES)) * 8
        out[name] = slab[row:row + rows].reshape(-1)[:size].reshape(shapes[name])
        row += rows
    return out


def kernel(x, pre_norm_g, w_in, ln_v_g, ln_v_b, w_spatial, b_spatial, sinks, rel_bias, w_out, post_norm_g, loss_target, m_pre_norm_g, m_w_in, m_ln_v_g, m_ln_v_b, m_w_spatial, m_b_spatial, m_sinks, m_rel_bias, m_w_out, m_post_norm_g, v_pre_norm_g, v_w_in, v_ln_v_g, v_ln_v_b, v_w_spatial, v_b_spatial, v_sinks, v_rel_bias, v_w_out, v_post_norm_g):
    b_loc, seq, _ = x.shape
    tokens = b_loc * seq
    depth = w_in.shape[0]
    in_shard = w_in.shape[2]
    out_shard = w_out.shape[1]
    assert in_shard * N_DEV == IN_WIDTH and out_shard * N_DEV == D_MODEL and seq % BLOCK == 0

    me = _slot(lax.axis_index("x"), lax.axis_index("y"), lax.axis_index("c"))
    w_in_t, m_w_in_t, v_w_in_t = (jnp.swapaxes(a, 1, 2) for a in (w_in, m_w_in, v_w_in))
    w_in16, w_out16 = w_in_t.astype(BF16), w_out.astype(BF16)

    def gather_start(layer, not_before):
        zones = [lax.empty((N_DEV, in_shard, D_MODEL), BF16), lax.empty((N_DEV, out_shard, D_MODEL), BF16)]
        sems, sends, zones, pre_g = _exchange_start(
            [w_in16[layer], w_out16[layer]], zones, [0, 0], f"weights_send_{layer}",
            pre_norm_g.reshape(depth, 1, D_MODEL), after=not_before)
        return (sems, sends, [0, 0]), zones, pre_g

    def full_weights(gin, gout):
        return gin.reshape(IN_WIDTH, D_MODEL), gout.reshape(D_MODEL, D_MODEL)

    def gather_wait(layer, started, zones, after):
        gin, gout = _exchange_wait([started], zones, after, f"weights_wait_{layer}")
        gin = lax.dynamic_update_index_in_dim(gin, w_in16[layer], me, 0)
        gout = lax.dynamic_update_index_in_dim(gout, w_out16[layer], me, 0)
        return full_weights(gin, gout)

    causal = jnp.tril(jnp.ones((CHUNK, CHUNK), dtype=bool))
    wm = jnp.where(causal, w_spatial, 0.0).astype(BF16)
    wmt = _hbm(jnp.swapaxes(wm, -1, -2))
    wm = _hbm(wm)
    bsb = _hbm(jnp.repeat(jnp.swapaxes(b_spatial, -1, -2), HEAD_DIM, axis=-1))
    pre_g3 = _hbm(pre_norm_g.reshape(depth, 1, D_MODEL))
    post_g3 = _hbm(post_norm_g.reshape(depth, 1, D_MODEL))
    ln_g3 = _hbm(ln_v_g.reshape(depth, 1, A_WIDTH))
    ln_b3 = _hbm(ln_v_b.reshape(depth, 1, A_WIDTH))
    bucket = jnp.asarray(_bucket_table())
    bias = _bias_table(rel_bias, bucket)

    xs, saved, weights = [x.reshape(tokens, D_MODEL)], [], []
    pending = None
    for layer in range(depth):
        if layer == 0:
            w, wo = full_weights(*_all_gather([w_in16[0], w_out16[0]], "weights_gather_0"))
        else:
            w, wo = gather_wait(layer, pending[0], pending[1], xs[-1])
        pre_g_fwd = pre_g3
        if layer + 1 < depth:
            pending = gather_start(layer + 1, w)
            pre_g_fwd = _hbm(pending[2])
        h, pa, pb, ya = _inproj_gmlp_fwd(xs[-1], pre_g_fwd, w, ln_g3, ln_b3, wm, bsb, layer)
        if layer + 1 < depth:
            yb, pw, spw, y, x_next = _attn_outproj_fwd(pb, bias, sinks, ya, wo, xs[-1], post_g3, b_loc, layer)
            xs.append(x_next)
        else:
            yb, pw, spw, y, g, loss_part = _attn_outproj_fwd(pb, bias, sinks, ya, wo, xs[-1], post_g3, b_loc, layer,
                                                    target=loss_target.reshape(tokens, D_MODEL))
        saved.append((h, pa, pb, ya, yb, y, pw, spw))
        weights.append((w, wo))
    loss = lax.psum(loss_part[0, 0], ("x", "y", "c"))

    grads = {name: [None] * depth for name in ("pre_norm_g","post_norm_g", "ln_v_g", "ln_v_b",
                                               "b_spatial", "sinks", "dbias")}
    zone_in = lax.empty((N_DEV, depth * in_shard, D_MODEL), BF16)
    zone_out = lax.empty((N_DEV, depth * out_shard, D_MODEL), BF16)
    zone_ws = lax.empty((N_DEV, depth * A_GROUPS * CHUNK, CHUNK), F32)
    started_in, started_out, own_in, own_out = [], [], [None] * depth, [None] * depth
    own_ws = [None] * depth
    dbias = jnp.zeros((B_HEADS, BLOCK, BLOCK), F32)
    for layer in reversed(range(depth)):
        h, pa, pb, ya, yb, y, pw, spw = saved[layer]
        w, wo = weights[layer]
        dya, dyb, dwo, dpost = _outproj_bwd(g, y, ya, yb, wo, post_g3, layer)
        send_out = dwo.reshape(N_DEV, out_shard, D_MODEL)
        own_out[layer] = lax.dynamic_index_in_dim(send_out, me, 0, keepdims=False)
        sems, sends, (zone_out,), ln_g_bwd = _exchange_start(
            [send_out], [zone_out], [layer * out_shard], f"grads_send_out_{layer}", ln_v_g.reshape(depth, 1, A_WIDTH))
        started_out.append((sems, sends, [layer * out_shard]))
        dqz, dkv, dbias, dsink = _attn_bwd(pb, dyb, pw, spw, dbias, layer)
        da, dw, dws, dbs, dlg, dlb = _gmlp_bwd_dw(pa, dya, h, dqz, dkv, _hbm(ln_g_bwd), ln_b3, wm, wmt, bsb, layer)
        own_ws[layer] = dws.reshape(A_GROUPS * CHUNK, CHUNK)
        send_in = dw.reshape(N_DEV, in_shard, D_MODEL)
        own_in[layer] = lax.dynamic_index_in_dim(send_in, me, 0, keepdims=False)
        row_offs = [layer * A_GROUPS * CHUNK, layer * in_shard]
        sems, sends, (zone_ws, zone_in), pre_g_bwd = _exchange_start(
            [own_ws[layer], send_in], [zone_ws, zone_in], row_offs, f"grads_send_in_{layer}",
            pre_norm_g.reshape(depth, 1, D_MODEL))
        started_in.append((sems, sends, row_offs))
        g, dpre = _inproj_bwd_dx(da, dqz, dkv, w, xs[layer], g, _hbm(pre_g_bwd), layer)
        grads["b_spatial"][layer] = dbs[:, :A_GROUPS].T
        grads["ln_v_g"][layer] = dlg[0]
        grads["ln_v_b"][layer] = dlb[0]
        grads["sinks"][layer] = dsink[0, :B_HEADS]
        grads["pre_norm_g"][layer] = dpre[0]
        grads["post_norm_g"][layer] = dpost[0]
    grad_x = g.reshape(x.shape)
    col_bucket = jnp.asarray(np.broadcast_to(_bucket_table()[0:1, ::-1], (B_HEADS, BLOCK)))
    flip = jnp.asarray(np.eye(BLOCK, dtype=np.float32)[::-1], dtype=BF16)
    drel = _rel_bias_grad(dbias, col_bucket, flip)[:, :REL_BUCKETS].T

    (recv_out,) = _exchange_wait(started_out, [zone_out], g, "grads_wait_out")
    recv_out = lax.dynamic_update_index_in_dim(recv_out, jnp.concatenate(own_out, axis=0), me, 0)
    res_out = _reduce_adamw(recv_out, w_out.reshape(-1, D_MODEL), m_w_out.reshape(-1, D_MODEL),
                            v_w_out.reshape(-1, D_MODEL), "adamw_w_out")
    res_out = [r.reshape(w_out.shape) for r in res_out]

    small_w = dict(pre_norm_g=pre_norm_g, post_norm_g=post_norm_g, ln_v_g=ln_v_g,
                   ln_v_b=ln_v_b, b_spatial=b_spatial, sinks=sinks, rel_bias=rel_bias)
    small_m = dict(pre_norm_g=m_pre_norm_g, post_norm_g=m_post_norm_g, ln_v_g=m_ln_v_g,
                   ln_v_b=m_ln_v_b, b_spatial=m_b_spatial, sinks=m_sinks, rel_bias=m_rel_bias)
    small_v = dict(pre_norm_g=v_pre_norm_g, post_norm_g=v_post_norm_g, ln_v_g=v_ln_v_g,
                   ln_v_b=v_ln_v_b, b_spatial=v_b_spatial, sinks=v_sinks, rel_bias=v_rel_bias)
    small_g = {name: jnp.stack(grads[name]) for name in _SMALL if name != "rel_bias"}
    small_g["rel_bias"] = drel
    shapes = {name: small_w[name].shape for name in _SMALL}
    (slots,) = _all_gather([_pack_small(small_g)], "small_grads_all_gather")
    res_small = _reduce_adamw(slots, _pack_small(small_w), _pack_small(small_m), _pack_small(small_v), "adamw_small")

    recv_ws, recv_in = _exchange_wait(started_in, [zone_ws, zone_in], res_small[0], "grads_wait_in")
    recv_ws = lax.dynamic_update_index_in_dim(recv_ws, jnp.concatenate(own_ws, axis=0), me, 0)
    res_ws = _reduce_adamw(recv_ws, w_spatial.reshape(-1, CHUNK), m_w_spatial.reshape(-1, CHUNK),
                           v_w_spatial.reshape(-1, CHUNK), "adamw_w_spatial")
    res_ws = [r.reshape(w_spatial.shape) for r in res_ws]
    recv_in = lax.dynamic_update_index_in_dim(recv_in, jnp.concatenate(own_in, axis=0), me, 0)
    res_in = _reduce_adamw(recv_in, w_in_t.reshape(-1, D_MODEL), m_w_in_t.reshape(-1, D_MODEL),
                           v_w_in_t.reshape(-1, D_MODEL), "adamw_w_in")
    res_in = [jnp.swapaxes(r.reshape(w_in_t.shape), 1, 2) for r in res_in]
    res_small = [_unpack_small(r, shapes) for r in res_small]

    order = ("pre_norm_g", "w_in", "ln_v_g", "ln_v_b", "w_spatial", "b_spatial", "sinks", "rel_bias", "w_out",
             "post_norm_g")
    outs = [loss, grad_x]
    for kind in range(4):
        for name in order:
            if name == "w_in":
                outs.append(res_in[kind])
            elif name == "w_out":
                outs.append(res_out[kind])
            elif name == "w_spatial":
                outs.append(res_ws[kind])
            else:
                outs.append(res_small[kind][name])
    return tuple(outs)
```

```python
import math

import numpy as np
import jax
import jax.numpy as jnp
from jax import lax
from jax.experimental import pallas as pl
from jax.experimental.pallas import tpu as pltpu

F32 = jnp.float32
BF16 = jnp.bfloat16

D_MODEL = 1024
A_WIDTH = 512
A_GROUPS = 8
CHUNK = 128
B_HEADS = 8
HEAD_DIM = 64
B_WIDTH = 512
KV_WIDTH = 128
BLOCK = 128
REL_BUCKETS = 32
REL_MAX_DIST = 128
NORM_EPS = 1e-6
PA_WIDTH = 3 * A_WIDTH
PB_WIDTH = 2 * B_WIDTH + 2 * KV_WIDTH
IN_WIDTH = PA_WIDTH + PB_WIDTH
K_OFF, V_OFF, BZ_OFF = B_WIDTH, B_WIDTH + KV_WIDTH, B_WIDTH + 2 * KV_WIDTH
KEPT_WIDTH = 5 * A_WIDTH
SCALE = HEAD_DIM ** -0.5
NEG = -1e30
N_DEV = 8
LANES = 128

ADAM_LR = 0.001
ADAM_B1 = 0.9
ADAM_B2 = 0.999
ADAM_EPS = 1e-08
ADAM_WD = 0.01
ADAM_STEP = 10

V7X_VMEM_BYTES = 64 * 1024 * 1024
VMEM_TEMP_BYTES = 12 * 1024 * 1024
MESH = pl.DeviceIdType.MESH


def _vmem_limit(block_bytes, scratch_bytes=0):
    need = 2 * sum(block_bytes) + scratch_bytes + VMEM_TEMP_BYTES
    return int(min(need, V7X_VMEM_BYTES - 8 * 1024 * 1024))


def _nbytes(shape, dtype):
    return int(np.prod(shape)) * jnp.dtype(dtype).itemsize


def _token_tile(tokens):
    tile = min(512, tokens // 2)
    assert tokens % tile == 0 and tile % CHUNK == 0, tokens
    return tile


def _row_tile(rows, cap=512):
    best = 8
    for t in range(8, cap + 1, 8):
        if rows % t == 0:
            best = t
    assert rows % best == 0, rows
    return best


def _mm(a, b):
    return lax.dot_general(a, b, (((1,), (0,)), ((), ())), preferred_element_type=F32)


def _mm_nt(a, b):
    return lax.dot_general(a, b, (((1,), (1,)), ((), ())), preferred_element_type=F32)


def _mm_tn(a, b):
    return lax.dot_general(a, b, (((0,), (0,)), ((), ())), preferred_element_type=F32)


_GELU_C = math.sqrt(2.0 / math.pi)


_GELU_A = _GELU_C * 0.044715


def _gelu_parts(x):
    x2 = x * x
    t = jnp.tanh(x * (_GELU_C + _GELU_A * x2))
    return x2, t, 0.5 + 0.5 * t


def _gelu(x):
    return x * _gelu_parts(x)[2]


def _gelu_and_grad(x):
    x2, t, half_plus = _gelu_parts(x)
    grad = half_plus + (0.5 * x) * (1.0 - t * t) * (_GELU_C + (3.0 * _GELU_A) * x2)
    return x * half_plus, grad


def _sigmoid(x):
    return 0.5 + 0.5 * jnp.tanh(0.5 * x)


def _bucket_table():
    q = np.arange(BLOCK)[:, None]
    k = np.arange(BLOCK)[None, :]
    dist = np.where(k <= q, q - k, q + BLOCK - k)
    max_exact = REL_BUCKETS // 2
    safe = np.maximum(dist, 1).astype(np.float32)
    large = max_exact + (np.log(safe / np.float32(max_exact)) / np.float32(math.log(REL_MAX_DIST / max_exact))
                         * np.float32(REL_BUCKETS - max_exact)).astype(np.int32)
    large = np.minimum(large, REL_BUCKETS - 1)
    assert dist.min() >= 0 and dist.max() < BLOCK
    return np.where(dist < max_exact, dist, large).astype(np.int32)


def _hbm(x):
    return pltpu.with_memory_space_constraint(x, pltpu.HBM)


def _slot(px, py, pc):
    return 4 * px + 2 * py + pc


def _all_gather(arrs, name):
    n = len(arrs)

    def body(*refs):
        ins, outs = refs[:n], refs[n:2 * n]
        send_sems, recv_sems, local_sems = refs[2 * n:]
        x, y, c = lax.axis_index("x"), lax.axis_index("y"), lax.axis_index("c")
        me, sibling = (x, y, c), (x, y, 1 - c)
        chips = [(1 - x, y), (x, 1 - y), (1 - x, 1 - y)]

        def copy(a, k, block, to, src=None):
            dst = outs[a].at[_slot(*block)]
            return pltpu.make_async_remote_copy(
                src_ref=dst if src is None else src, dst_ref=dst,
                send_sem=send_sems.at[7 * a + k], recv_sem=recv_sems.at[7 * a + k],
                device_id=to, device_id_type=MESH)

        mine = [pltpu.make_async_copy(ins[a], outs[a].at[_slot(*me)], local_sems.at[a]) for a in range(n)]
        for cp in mine:
            cp.start()
        first = []
        for a in range(n):
            first.append(copy(a, 0, me, sibling, src=ins[a]))
            first += [copy(a, 1 + j, me, (*chip, c), src=ins[a]) for j, chip in enumerate(chips)]
        for cp in first:
            cp.start()
        passed = []
        for j, chip in enumerate(chips):
            for a in range(n):
                copy(a, 1 + j, (*chip, c), me).wait_recv()
                fwd = copy(a, 4 + j, (*chip, c), sibling)
                fwd.start()
                passed.append(fwd)
        for a in range(n):
            copy(a, 0, sibling, me).wait_recv()
            for j, chip in enumerate(chips):
                copy(a, 4 + j, (*chip, 1 - c), me).wait_recv()
        for cp in first + passed:
            cp.wait_send()
        for cp in mine:
            cp.wait()

    any_spec = pl.BlockSpec(memory_space=pl.ANY)
    return pl.pallas_call(
        body, name=name,
        out_shape=[jax.ShapeDtypeStruct((N_DEV,) + a.shape, a.dtype) for a in arrs],
        in_specs=[any_spec] * n, out_specs=[any_spec] * n,
        scratch_shapes=[pltpu.SemaphoreType.DMA((7 * n,)), pltpu.SemaphoreType.DMA((7 * n,)),
                        pltpu.SemaphoreType.DMA((n,))],
    )(*arrs)


_HBM_SPEC = pl.BlockSpec(memory_space=pltpu.HBM)
_SEM_SPEC = pl.BlockSpec(memory_space=pltpu.SEMAPHORE)
_DATAFLOW = pltpu.SideEffectType.DATAFLOW_SIDE_EFFECTING


def _peers():
    x, y, c = lax.axis_index("x"), lax.axis_index("y"), lax.axis_index("c")
    peers = []
    for k in range(1, N_DEV):
        fx, fy, fc = (k >> 2) & 1, (k >> 1) & 1, k & 1
        peers.append((1 - x if fx else x, 1 - y if fy else y, 1 - c if fc else c))
    return (x, y, c), peers


def _exchange_copy(send_ref, land_ref, row_off, src_slot, dst_slot, sems, idx, peer):
    rows = send_ref.shape[-2]
    src = send_ref.at[src_slot] if len(send_ref.shape) == 3 else send_ref
    return pltpu.make_async_remote_copy(
        src_ref=src, dst_ref=land_ref.at[dst_slot, pl.ds(row_off, rows), :],
        send_sem=sems[0].at[idx], recv_sem=sems[1].at[idx], device_id=peer, device_id_type=MESH)


def _exchange_start(sends, lands, row_offs, name, carry, after=None):
    n = len(sends)

    def body(*refs):
        ins, zones, carry_ref = refs[:n], refs[n:2 * n], refs[2 * n]
        first_out = 2 * n + 1 + (after is not None)
        sems = refs[first_out:first_out + 2]
        carry_out = refs[-1]
        me, peers = _peers()
        for a in range(n):
            for k, peer in enumerate(peers):
                _exchange_copy(ins[a], zones[a], row_offs[a], _slot(*peer), _slot(*me), sems, 7 * a + k, peer).start()
        carry_out[...] = carry_ref[...]

    arrays = [_hbm(a) for a in list(sends) + list(lands)]
    vmem = pl.BlockSpec(memory_space=pltpu.VMEM)
    out = pl.pallas_call(
        body, name=name,
        out_shape=(pltpu.SemaphoreType.DMA((7 * n,)), pltpu.SemaphoreType.DMA((7 * n,)),
                   *[pltpu.HBM(a.shape, a.dtype) for a in arrays], jax.ShapeDtypeStruct(carry.shape, carry.dtype)),
        in_specs=[_HBM_SPEC] * (2 * n) + [vmem] + ([pl.BlockSpec(memory_space=pl.ANY)] if after is not None else []),
        out_specs=(_SEM_SPEC, _SEM_SPEC, *[_HBM_SPEC] * (2 * n), vmem),
        input_output_aliases={i: 2 + i for i in range(2 * n)},
        compiler_params=pltpu.CompilerParams(has_side_effects=_DATAFLOW),
    )(*arrays, carry, *([after] if after is not None else []))
    return (out[0], out[1]), list(out[2:2 + n]), list(out[2 + n:2 + 2 * n]), out[-1]


def _exchange_wait(started, lands, after, name):
    n = len(lands)
    flat_sends = [s for _, sends, _ in started for s in sends]
    flat_sems = [s for sems, _, _ in started for s in sems]
    ns = len(flat_sends)

    def body(*refs):
        ins, zones = refs[:ns], refs[ns:ns + n]
        sem_refs = refs[ns + n:ns + n + len(flat_sems)]
        me, peers = _peers()
        pos = 0
        for call, (_, sends, row_offs) in enumerate(started):
            sems = sem_refs[2 * call:2 * call + 2]
            for a in range(len(sends)):
                for k, peer in enumerate(peers):
                    cp = _exchange_copy(ins[pos + a], zones[a], row_offs[a], _slot(*peer), _slot(*peer), sems,
                                        7 * a + k, peer)
                    cp.wait_send()
                    cp.wait_recv()
            pos += len(sends)

    arrays = list(flat_sends) + list(lands)
    out = pl.pallas_call(
        body, name=name,
        out_shape=tuple(pltpu.HBM(a.shape, a.dtype) for a in arrays),
        in_specs=[_HBM_SPEC] * len(arrays) + [_SEM_SPEC] * len(flat_sems) + [pl.BlockSpec(memory_space=pl.ANY)],
        out_specs=tuple([_HBM_SPEC] * len(arrays)),
        input_output_aliases={i: i for i in range(len(arrays))},
        compiler_params=pltpu.CompilerParams(has_side_effects=_DATAFLOW),
    )(*arrays, *flat_sems, after)
    return list(out[ns:])


def _bias_table(rel_bias, bucket):
    def body(rel_ref, bucket_ref, out_ref):
        bk = bucket_ref[...]
        for h in range(B_HEADS):
            def pick(b, acc, h=h):
                return jnp.where(bk == b, rel_ref[b, h], acc)
            out_ref[h] = lax.fori_loop(0, REL_BUCKETS, pick, jnp.zeros((BLOCK, BLOCK), F32))

    return pl.pallas_call(
        body, name="bias_table",
        out_shape=jax.ShapeDtypeStruct((B_HEADS, BLOCK, BLOCK), F32),
        in_specs=[pl.BlockSpec(memory_space=pltpu.SMEM), pl.BlockSpec(memory_space=pltpu.VMEM)],
        out_specs=pl.BlockSpec(memory_space=pltpu.VMEM),
    )(rel_bias, bucket)


def _inproj_gmlp_fwd(x2, pre_g3, w, ln_g3, ln_b3, wm, bsb, layer):
    tokens = x2.shape[0]
    tm = _token_tile(tokens)

    def body(x_ref, g_ref, w_ref, lg_ref, lb_ref, wm_ref, bsb_ref, h_ref, pa_ref, pb_ref, ya_ref, kept_ref, rstd_ref):
        x = x_ref[...]
        r = lax.rsqrt(jnp.mean(x * x, axis=-1, keepdims=True) + NORM_EPS)
        h = (x * r * g_ref[...]).astype(BF16)
        h_ref[...] = h
        pa_ref[...] = _mm_nt(h, w_ref[0:PA_WIDTH, :]).astype(BF16)
        pb_ref[...] = _mm_nt(h, w_ref[PA_WIDTH:IN_WIDTH, :]).astype(BF16)
        for ci in range(tm // CHUNK):
            rows = slice(ci * CHUNK, (ci + 1) * CHUNK)
            gu, gv, pz, u, vhat, rstd, _, mixed, sg = _gmlp_forward_chunk(
                pa_ref[rows, :], lg_ref[...], lb_ref[...], wm_ref, bsb_ref[...])
            ya_ref[rows, :] = (u * mixed * (pz * sg)).astype(BF16)
            for k, val in enumerate((u, gu, gv, vhat, mixed)):
                kept_ref[rows, A_WIDTH * k:A_WIDTH * (k + 1)] = val.astype(BF16)
            rstd_ref[rows, :] = jnp.broadcast_to(rstd, (CHUNK, LANES))

    blocks = [_nbytes((tm, D_MODEL), F32), _nbytes((D_MODEL, IN_WIDTH), BF16),
              _nbytes((tm, D_MODEL), BF16), _nbytes((tm, PA_WIDTH), BF16), _nbytes((tm, PB_WIDTH), BF16),
              _nbytes((A_GROUPS, CHUNK, CHUNK), BF16), _nbytes((CHUNK, A_WIDTH), F32), _nbytes((tm, A_WIDTH), BF16)]
    return pl.pallas_call(
        body, name=f"inproj_gmlp_fwd_{layer}", grid=(tokens // tm,),
        in_specs=[pl.BlockSpec((tm, D_MODEL), lambda i: (i, 0)),
                  pl.BlockSpec((None, 1, D_MODEL), lambda i: (layer, 0, 0)),
                  pl.BlockSpec((IN_WIDTH, D_MODEL), lambda i: (0, 0)),
                  pl.BlockSpec((None, 1, A_WIDTH), lambda i: (layer, 0, 0)),
                  pl.BlockSpec((None, 1, A_WIDTH), lambda i: (layer, 0, 0)),
                  pl.BlockSpec((None, A_GROUPS, CHUNK, CHUNK), lambda i: (layer, 0, 0, 0)),
                  pl.BlockSpec((None, CHUNK, A_WIDTH), lambda i: (layer, 0, 0))],
        out_specs=[pl.BlockSpec((tm, D_MODEL), lambda i: (i, 0)),
                   pl.BlockSpec((tm, PA_WIDTH), lambda i: (i, 0)),
                   pl.BlockSpec((tm, PB_WIDTH), lambda i: (i, 0)),
                   pl.BlockSpec((tm, A_WIDTH), lambda i: (i, 0)),
                   pl.BlockSpec((tm, KEPT_WIDTH), lambda i: (i, 0)),
                   pl.BlockSpec((tm, LANES), lambda i: (i, 0))],
        out_shape=[pltpu.HBM((tokens, D_MODEL), BF16),
                   pltpu.HBM((tokens, PA_WIDTH), BF16),
                   pltpu.HBM((tokens, PB_WIDTH), BF16),
                   pltpu.HBM((tokens, A_WIDTH), BF16),
                   pltpu.HBM((tokens, KEPT_WIDTH), BF16),
                   pltpu.HBM((tokens, LANES), F32)],
        compiler_params=pltpu.CompilerParams(dimension_semantics=("parallel",),
                                             vmem_limit_bytes=_vmem_limit(blocks + [_nbytes((tm, KEPT_WIDTH), BF16),
                                                                                    _nbytes((tm, LANES), F32)])),
    )(_hbm(x2), pre_g3, _hbm(w), ln_g3, ln_b3, wm, bsb)


def _gmlp_forward_chunk(pa, ln_g, ln_b, wm_ref, bsb):
    pu = pa[:, 0:A_WIDTH].astype(F32)
    pv = pa[:, A_WIDTH:2 * A_WIDTH].astype(F32)
    pz = pa[:, 2 * A_WIDTH:3 * A_WIDTH].astype(F32)
    u, gu = _gelu_and_grad(pu)
    vv, gv = _gelu_and_grad(pv)
    mu = jnp.mean(vv, axis=-1, keepdims=True)
    xc = vv - mu
    rstd = lax.rsqrt(jnp.mean(xc * xc, axis=-1, keepdims=True) + NORM_EPS)
    vhat = xc * rstd
    vnb = (vhat * ln_g + ln_b).astype(BF16)
    low = lax.broadcasted_iota(jnp.int32, (CHUNK, LANES), 1) < HEAD_DIM
    parts = []
    for p in range(A_GROUPS // 2):
        vp = vnb[:, LANES * p:LANES * (p + 1)]
        parts.append(jnp.where(low, _mm(wm_ref[2 * p], vp), _mm(wm_ref[2 * p + 1], vp)))
    mixed = jnp.concatenate(parts, axis=1) + bsb
    sg = _sigmoid(pz)
    return gu, gv, pz, u, vhat, rstd, vnb, mixed, sg


def _kv_variants(kv):
    t = kv.astype(F32)
    rolled = pltpu.roll(t, HEAD_DIM, 1)
    low = lax.broadcasted_iota(jnp.int32, t.shape, 1) < HEAD_DIM
    zero = jnp.zeros_like(t)
    head0 = (jnp.where(low, t, zero).astype(BF16), jnp.where(low, zero, rolled).astype(BF16))
    head1 = (jnp.where(low, rolled, zero).astype(BF16), jnp.where(low, zero, t).astype(BF16))
    return head0, head1


def _band_masks():
    row = lax.broadcasted_iota(jnp.int32, (BLOCK, BLOCK), 0)
    col = lax.broadcasted_iota(jnp.int32, (BLOCK, BLOCK), 1)
    return col <= row


def _wrap(full, tri):
    return jnp.where(tri, full[:, BLOCK:2 * BLOCK], full[:, 0:BLOCK])


def _attn_probs(sf, bias_h, sink, tri, kill):
    s = _wrap(sf, tri) * SCALE + bias_h
    s = jnp.where(kill, NEG, s)
    m = jnp.maximum(jnp.max(s, axis=-1, keepdims=True), sink)
    e = jnp.exp(s - m)
    es = jnp.exp(sink - m)
    inv = 1.0 / (jnp.sum(e, axis=-1, keepdims=True) + es)
    return e * inv, es * inv


def _unwrap16(p, tri):
    zero = jnp.zeros_like(p)
    return jnp.concatenate([jnp.where(tri, zero, p), jnp.where(tri, p, zero)], axis=1).astype(BF16)


def _fill_kv(kv_ref, prev_ref, cur_ref):
    kv_ref[0:BLOCK, :] = prev_ref[...]
    kv_ref[BLOCK:, :] = cur_ref[:, K_OFF:K_OFF + 2 * KV_WIDTH]


def _attn_outproj_fwd(pb, bias, sinks, ya, wo, x2, post_g3, b_loc, layer, target=None):
    tokens = pb.shape[0]
    nb = tokens // b_loc // BLOCK
    tq = _token_tile(tokens)
    per_tile = tq // BLOCK
    steps = tokens // tq
    with_loss = target is not None
    half = D_MODEL // 2

    def body(*refs):
        sink_ref, cur_ref, prev_ref, bias_ref, ya_ref, woa_ref, wob_ref, x_ref, g_ref = refs[:9]
        if with_loss:
            t_ref, yb_ref, pw_ref, spw_ref, y_ref, grad_ref, loss_ref, kv_ref, acc_ref = refs[9:]
        else:
            yb_ref, pw_ref, spw_ref, y_ref, xn_ref, kv_ref = refs[9:]
        t = pl.program_id(0)
        _fill_kv(kv_ref, prev_ref, cur_ref)
        tri = _band_masks()
        head_lane = lax.broadcasted_iota(jnp.int32, (BLOCK, LANES), 1)
        y_from_a = _mm(ya_ref[...], woa_ref[...])

        def block(i, carry):
            start = i * BLOCK
            rows = pl.ds(start, BLOCK)
            first = lax.rem(t * per_tile + i, nb) == 0
            kill = jnp.logical_and(first, jnp.logical_not(tri))
            kv = kv_ref[pl.ds(start, 2 * BLOCK), :]
            k_ops = _kv_variants(kv[:, 0:KV_WIDTH])
            v_ops = _kv_variants(kv[:, KV_WIDTH:2 * KV_WIDTH])
            p16 = {}
            sink_cols = jnp.zeros((BLOCK, LANES), F32)
            for kvh in range(2):
                pairs = (2 * kvh, 2 * kvh + 1)
                qs = jnp.concatenate([cur_ref[rows, LANES * p:LANES * (p + 1)] for p in pairs], axis=0)
                for j in range(2):
                    sf = _mm_nt(qs, k_ops[kvh][j])
                    for r, p in enumerate(pairs):
                        hd = 2 * p + j
                        probs, sink_p = _attn_probs(sf[BLOCK * r:BLOCK * (r + 1)], bias_ref[hd],
                                                    sink_ref[layer, hd], tri, kill)
                        p16[hd] = _unwrap16(probs, tri)
                        pw_ref[rows, LANES * hd:LANES * (hd + 1)] = probs.astype(BF16)
                        sink_cols = jnp.where(head_lane == hd, sink_p, sink_cols)
            spw_ref[rows, :] = sink_cols
            for kvh in range(2):
                pairs = (2 * kvh, 2 * kvh + 1)
                out = jnp.zeros((2 * BLOCK, LANES), F32)
                for j in range(2):
                    out = out + _mm(jnp.concatenate([p16[2 * p + j] for p in pairs], axis=0), v_ops[kvh][j])
                for r, p in enumerate(pairs):
                    bz = cur_ref[rows, BZ_OFF + LANES * p:BZ_OFF + LANES * (p + 1)].astype(F32)
                    yb_ref[rows, LANES * p:LANES * (p + 1)] = (
                        out[BLOCK * r:BLOCK * (r + 1)] * (bz * _sigmoid(bz))).astype(BF16)
            return carry

        for i in range(per_tile):
            block(i, 0)

        y = y_from_a + _mm(yb_ref[...], wob_ref[...])
        r = lax.rsqrt(jnp.mean(y * y, axis=-1, keepdims=True) + NORM_EPS)
        x_next = x_ref[...] + y * r * g_ref[...]
        y_ref[...] = y.astype(BF16)
        if not with_loss:
            xn_ref[...] = x_next
            return

        @pl.when(t == 0)
        def _():
            acc_ref[...] = jnp.zeros_like(acc_ref)

        err = x_next - t_ref[...]
        grad_ref[...] = err * (1.0 / D_MODEL)
        acc_ref[...] += jnp.sum(err * err, axis=0, keepdims=True)

        @pl.when(t == steps - 1)
        def _():
            total = jnp.sum(acc_ref[...], axis=-1, keepdims=True) * (0.5 / D_MODEL)
            loss_ref[...] = jnp.broadcast_to(total, loss_ref.shape)

    tile = pl.BlockSpec((tq, D_MODEL), lambda t: (t, 0))
    blocks = [_nbytes((tq, PB_WIDTH), BF16), _nbytes((BLOCK, 2 * KV_WIDTH), BF16),
              _nbytes((B_HEADS, BLOCK, BLOCK), F32), 2 * _nbytes((tq, B_WIDTH), BF16),
              2 * _nbytes((half, D_MODEL), BF16), 3 * _nbytes((tq, D_MODEL), F32), _nbytes((tq, D_MODEL), BF16),
              _nbytes((tq, B_HEADS * BLOCK), BF16), _nbytes((tq, LANES), F32)]
    scratch = _nbytes((tq + BLOCK, 2 * KV_WIDTH), BF16)
    in_specs = [pl.BlockSpec(memory_space=pltpu.SMEM),
                pl.BlockSpec((tq, PB_WIDTH), lambda t: (t, 0)),
                pl.BlockSpec((BLOCK, 2 * KV_WIDTH),
                             lambda t: (jnp.maximum(t * per_tile - 1, 0), K_OFF // (2 * KV_WIDTH))),
                pl.BlockSpec((B_HEADS, BLOCK, BLOCK), lambda t: (0, 0, 0)),
                pl.BlockSpec((tq, half), lambda t: (t, 0)),
                pl.BlockSpec((half, D_MODEL), lambda t: (0, 0)),
                pl.BlockSpec((half, D_MODEL), lambda t: (1, 0)),
                tile,
                pl.BlockSpec((None, 1, D_MODEL), lambda t: (layer, 0, 0))]
    out_specs = [pl.BlockSpec((tq, B_WIDTH), lambda t: (t, 0)), pl.BlockSpec((tq, B_HEADS * BLOCK), lambda t: (t, 0)),
                 pl.BlockSpec((tq, LANES), lambda t: (t, 0)), tile, tile]
    out_shape = [pltpu.HBM((tokens, B_WIDTH), BF16), pltpu.HBM((tokens, B_HEADS * BLOCK), BF16),
                 pltpu.HBM((tokens, LANES), F32), pltpu.HBM((tokens, D_MODEL), BF16),
                 pltpu.HBM((tokens, D_MODEL), F32)]
    scratch_shapes = [pltpu.VMEM((tq + BLOCK, 2 * KV_WIDTH), BF16)]
    operands = [sinks, _hbm(pb), _hbm(pb), _hbm(bias), _hbm(ya), _hbm(wo), _hbm(wo), _hbm(x2), post_g3]
    if with_loss:
        in_specs.append(tile)
        out_specs.append(pl.BlockSpec((1, LANES), lambda t: (0, 0)))
        out_shape.append(pltpu.HBM((1, LANES), F32))
        scratch_shapes.append(pltpu.VMEM((1, D_MODEL), F32))
        operands.append(_hbm(target))
    return pl.pallas_call(
        body, name=f"attn_outproj_fwd_{layer}", grid=(steps,),
        in_specs=in_specs, out_specs=out_specs, out_shape=out_shape, scratch_shapes=scratch_shapes,
        compiler_params=pltpu.CompilerParams(dimension_semantics=("arbitrary" if with_loss else "parallel",),
                                             vmem_limit_bytes=_vmem_limit(blocks, scratch)),
    )(*operands)


def _outproj_bwd(g, y, ya, yb, wo, post_g3, layer):
    tokens = g.shape[0]
    tm = _token_tile(tokens)
    half = D_MODEL // 2
    steps = tokens // tm

    def body(g_ref, y_ref, ya_ref, yb_ref, woa_ref, wob_ref, pg_ref, dya_ref, dyb_ref, dwo16_ref, dpg_ref, dwo_ref):
        @pl.when(pl.program_id(0) == 0)
        def _():
            dwo_ref[...] = jnp.zeros_like(dwo_ref)
            dpg_ref[...] = jnp.zeros_like(dpg_ref)

        gv = g_ref[...]
        yf = y_ref[...].astype(F32)
        r = lax.rsqrt(jnp.mean(yf * yf, axis=-1, keepdims=True) + NORM_EPS)
        yhat = yf * r
        gg = gv * pg_ref[...]
        dy = r * (gg - yhat * jnp.mean(gg * yhat, axis=-1, keepdims=True))
        dpg_ref[...] += jnp.sum(gv * yhat, axis=0, keepdims=True)
        dyb16 = dy.astype(BF16)
        dya_ref[...] = _mm_nt(dyb16, woa_ref[...]).astype(BF16)
        dyb_ref[...] = _mm_nt(dyb16, wob_ref[...]).astype(BF16)
        dwo_ref[0:half, :] += _mm_tn(ya_ref[...], dyb16)
        dwo_ref[half:D_MODEL, :] += _mm_tn(yb_ref[...], dyb16)

        @pl.when(pl.program_id(0) == steps - 1)
        def _():
            dwo16_ref[...] = dwo_ref[...].astype(BF16)

    blocks = [_nbytes((tm, D_MODEL), F32), _nbytes((tm, D_MODEL), BF16), 4 * _nbytes((tm, half), BF16),
              2 * _nbytes((half, D_MODEL), BF16), _nbytes((D_MODEL, D_MODEL), BF16)]
    return pl.pallas_call(
        body, name=f"outproj_bwd_{layer}", grid=(tokens // tm,),
        in_specs=[pl.BlockSpec((tm, D_MODEL), lambda i: (i, 0)),
                  pl.BlockSpec((tm, D_MODEL), lambda i: (i, 0)),
                  pl.BlockSpec((tm, half), lambda i: (i, 0)),
                  pl.BlockSpec((tm, half), lambda i: (i, 0)),
                  pl.BlockSpec((half, D_MODEL), lambda i: (0, 0)),
                  pl.BlockSpec((half, D_MODEL), lambda i: (1, 0)),
                  pl.BlockSpec((None, 1, D_MODEL), lambda i: (layer, 0, 0))],
        out_specs=[pl.BlockSpec((tm, half), lambda i: (i, 0)),
                   pl.BlockSpec((tm, half), lambda i: (i, 0)),
                   pl.BlockSpec((D_MODEL, D_MODEL), lambda i: (0, 0)),
                   pl.BlockSpec((1, D_MODEL), lambda i: (0, 0))],
        out_shape=[pltpu.HBM((tokens,half), BF16),
                   pltpu.HBM((tokens,half), BF16),
                   pltpu.HBM((D_MODEL,D_MODEL), BF16),
                   pltpu.HBM((1,D_MODEL), F32)],
        scratch_shapes=[pltpu.VMEM((D_MODEL, D_MODEL), F32)],
        compiler_params=pltpu.CompilerParams(dimension_semantics=("arbitrary",),
                                             vmem_limit_bytes=_vmem_limit(blocks, _nbytes((D_MODEL, D_MODEL), F32))),
    )(_hbm(g), _hbm(y), _hbm(ya), _hbm(yb), _hbm(wo), _hbm(wo), post_g3)


def _gmlp_bwd_dw(pa, kept, rstd_w, dya, h, dqz, dkv, ln_g3, ln_b3, wmt, layer):
    tokens = pa.shape[0]
    tc = _token_tile(tokens)
    steps = tokens // tc

    def body(pz_ref, kept_ref, rstd_ref, dya_ref, h_ref, dqz_ref, dkv_ref, lg_ref, lb_ref, wmt_ref,
             da_ref, dw_ref, dws_ref, dbs_ref, dlg_ref, dlb_ref, acc_ref, dbsb_ref):
        i = pl.program_id(0)

        @pl.when(i == 0)
        def _():
            acc_ref[...] = jnp.zeros_like(acc_ref)
            dws_ref[...] = jnp.zeros_like(dws_ref)
            dlg_ref[...] = jnp.zeros_like(dlg_ref)
            dlb_ref[...] = jnp.zeros_like(dlb_ref)
            dbsb_ref[...] = jnp.zeros_like(dbsb_ref)

        ln_g = lg_ref[...]
        low = lax.broadcasted_iota(jnp.int32, (CHUNK, LANES), 1) < HEAD_DIM
        hv = h_ref[...]
        acc_ref[:, PA_WIDTH:PA_WIDTH + B_WIDTH] += _mm_tn(hv, dqz_ref[:, 0:B_WIDTH])
        acc_ref[:, PA_WIDTH + K_OFF:PA_WIDTH + BZ_OFF] += _mm_tn(hv, dkv_ref[...])
        acc_ref[:, PA_WIDTH + BZ_OFF:IN_WIDTH] += _mm_tn(hv, dqz_ref[:, B_WIDTH:2 * B_WIDTH])

        for ci in range(tc // CHUNK):
            rows = slice(ci * CHUNK, (ci + 1) * CHUNK)
            u, gu, gv, vhat, mixed = (kept_ref[rows, A_WIDTH * k:A_WIDTH * (k + 1)].astype(F32) for k in range(5))
            rstd = jnp.concatenate([rstd_ref[rows, :]] * (A_WIDTH // LANES), axis=1)
            vnb = (vhat * ln_g + lb_ref[...]).astype(BF16)
            pz = pz_ref[rows, :].astype(F32)
            sg = _sigmoid(pz)
            dy = dya_ref[rows, :].astype(F32)
            sz = pz * sg
            dy_sz = dy * sz
            du = dy_sz * mixed
            dmixed = dy_sz * u
            dz = dy * (u * mixed) * (sg + sz * (1.0 - sg))
            dbsb_ref[...] += dmixed
            dmb = dmixed.astype(BF16)
            zero = jnp.zeros((CHUNK, LANES), BF16)
            parts = []
            for p in range(A_GROUPS // 2):
                dmp = dmb[:, LANES * p:LANES * (p + 1)]
                vp = vnb[:, LANES * p:LANES * (p + 1)]
                parts.append(jnp.where(low, _mm(wmt_ref[2 * p], dmp), _mm(wmt_ref[2 * p + 1], dmp)))
                dws_ref[2 * p] += _mm_nt(jnp.where(low, dmp, zero), vp)
                dws_ref[2 * p + 1] += _mm_nt(jnp.where(low, zero, dmp), vp)
            dvn = jnp.concatenate(parts, axis=1)
            dlg_ref[...] += jnp.sum(dvn * vhat, axis=0, keepdims=True)
            dlb_ref[...] += jnp.sum(dvn, axis=0, keepdims=True)
            dvh = dvn * ln_g
            dvv = rstd * (dvh - jnp.mean(dvh, axis=-1, keepdims=True)
                          - vhat * jnp.mean(dvh * vhat, axis=-1, keepdims=True))
            da_ref[rows, 0:A_WIDTH] = (du * gu).astype(BF16)
            da_ref[rows, A_WIDTH:2 * A_WIDTH] = (dvv * gv).astype(BF16)
            da_ref[rows, 2 * A_WIDTH:3 * A_WIDTH] = dz.astype(BF16)

        acc_ref[:, 0:PA_WIDTH] += _mm_tn(hv, da_ref[...])

        @pl.when(i == steps - 1)
        def _():
            for c in range(IN_WIDTH // LANES):
                cols = slice(LANES * c, LANES * (c + 1))
                dw_ref[cols, :] = acc_ref[:, cols].T.astype(BF16)
            causal =(lax.broadcasted_iota(jnp.int32, (CHUNK, CHUNK), 0)
                      >= lax.broadcasted_iota(jnp.int32, (CHUNK, CHUNK), 1))
            for h in range(A_GROUPS):
                dws_ref[h] = jnp.where(causal, dws_ref[h], 0.0)
            acc = dbsb_ref[...]
            lane_full = lax.broadcasted_iota(jnp.int32, (CHUNK, A_WIDTH), 1)
            lane_out = lax.broadcasted_iota(jnp.int32, (CHUNK, LANES), 1)
            out = jnp.zeros((CHUNK, LANES), F32)
            for h in range(A_GROUPS):
                in_group = jnp.logical_and(lane_full >= HEAD_DIM * h, lane_full < HEAD_DIM * (h + 1))
                s = jnp.sum(jnp.where(in_group, acc, 0.0), axis=-1, keepdims=True)
                out = jnp.where(lane_out == h, s, out)
            dbs_ref[...] = out

    blocks = [_nbytes((tc, KEPT_WIDTH), BF16), 2 * _nbytes((tc, A_WIDTH), BF16), _nbytes((A_GROUPS, CHUNK, CHUNK), BF16),
              _nbytes((tc, LANES), F32), _nbytes((tc, PA_WIDTH), BF16), _nbytes((A_GROUPS, CHUNK, CHUNK), F32),
              _nbytes((CHUNK, LANES), F32), _nbytes((tc, D_MODEL), BF16), _nbytes((tc, 2 * B_WIDTH), BF16),
              _nbytes((tc, 2 * KV_WIDTH), BF16), _nbytes((D_MODEL, IN_WIDTH), BF16)]
    scratch = _nbytes((D_MODEL, IN_WIDTH), F32) + _nbytes((CHUNK, A_WIDTH), F32)
    return pl.pallas_call(
        body, name=f"gmlp_bwd_dw_{layer}", grid=(steps,),
        in_specs=[pl.BlockSpec((tc, A_WIDTH), lambda i: (i, 2)),
                  pl.BlockSpec((tc, KEPT_WIDTH), lambda i: (i, 0)),
                  pl.BlockSpec((tc, LANES), lambda i: (i, 0)),
                  pl.BlockSpec((tc, A_WIDTH), lambda i: (i, 0)),
                  pl.BlockSpec((tc, D_MODEL), lambda i: (i, 0)),
                  pl.BlockSpec((tc, 2 * B_WIDTH), lambda i: (i, 0)),
                  pl.BlockSpec((pl.Element(tc), pl.Element(2 * KV_WIDTH)),
                               lambda i: (pl.multiple_of(i * tc + BLOCK, BLOCK), 0)),
                  pl.BlockSpec((None, 1, A_WIDTH), lambda i: (layer, 0, 0)),
                  pl.BlockSpec((None, 1, A_WIDTH), lambda i: (layer, 0, 0)),
                  pl.BlockSpec((None, A_GROUPS, CHUNK, CHUNK), lambda i: (layer, 0, 0, 0))],
        out_specs=[pl.BlockSpec((tc, PA_WIDTH), lambda i: (i, 0)),
                   pl.BlockSpec((IN_WIDTH, D_MODEL), lambda i: (0, 0)),
                   pl.BlockSpec((A_GROUPS, CHUNK, CHUNK), lambda i: (0, 0, 0)),
                   pl.BlockSpec((CHUNK, LANES), lambda i: (0, 0)),
                   pl.BlockSpec((1, A_WIDTH), lambda i: (0, 0)),
                   pl.BlockSpec((1, A_WIDTH), lambda i: (0, 0))],
        out_shape=[pltpu.HBM((tokens,PA_WIDTH), BF16),
                   pltpu.HBM((IN_WIDTH, D_MODEL), BF16),
                   pltpu.HBM((A_GROUPS, CHUNK, CHUNK), F32),
                   pltpu.HBM((CHUNK, LANES), F32),
                   pltpu.HBM((1,A_WIDTH), F32),
                   pltpu.HBM((1,A_WIDTH), F32)],
        scratch_shapes=[pltpu.VMEM((D_MODEL, IN_WIDTH), F32), pltpu.VMEM((CHUNK, A_WIDTH), F32)],
        compiler_params=pltpu.CompilerParams(dimension_semantics=("arbitrary",),
                                             vmem_limit_bytes=_vmem_limit(blocks, scratch)),
    )(_hbm(pa), _hbm(kept), _hbm(rstd_w), _hbm(dya), _hbm(h), _hbm(dqz), _hbm(dkv), ln_g3, ln_b3, wmt)


def _attn_bwd(pb, dyb, probs_w, sink_w, dbias_in, layer):
    tokens = pb.shape[0]
    qz_width = 2 * B_WIDTH
    tq = _token_tile(tokens)
    per_tile = tq // BLOCK
    nt = tokens // tq

    def body(cur_ref, prev_ref, dyb_ref, pw_ref, spw_ref, dbias_in_ref, dqz_ref, dkv_ref, dbias_ref, dsink_ref,
             kv_ref, acc_ref):
        t = pl.program_id(0)

        @pl.when(t == 0)
        def _():
            dbias_ref[...] = dbias_in_ref[...]
            dsink_ref[...] = jnp.zeros_like(dsink_ref)
            acc_ref[0:BLOCK, :] = jnp.zeros((BLOCK, 2 * KV_WIDTH), F32)

        @pl.when(t < nt)
        def _():
            acc_ref[BLOCK:, :] = jnp.zeros((tq, 2 * KV_WIDTH), F32)
            _fill_kv(kv_ref, prev_ref, cur_ref)
            tri = _band_masks()
            low = lax.broadcasted_iota(jnp.int32, (BLOCK, LANES), 1) < HEAD_DIM
            low_kv = lax.broadcasted_iota(jnp.int32, (2 * BLOCK, LANES), 1) < HEAD_DIM

            head_lane = lax.broadcasted_iota(jnp.int32, (BLOCK, LANES), 1)

            def block(i, carry):
                start = i * BLOCK
                rows = pl.ds(start, BLOCK)
                kv = kv_ref[pl.ds(start, 2 * BLOCK), :]
                k_ops = _kv_variants(kv[:, 0:KV_WIDTH])
                v_ops = _kv_variants(kv[:, KV_WIDTH:2 * KV_WIDTH])
                probs, p16, qs, dos, delta, ds16 = {}, {}, {}, {}, {}, {}
                for hd in range(B_HEADS):
                    probs[hd] = pw_ref[rows, LANES * hd:LANES * (hd + 1)].astype(F32)
                    p16[hd] = _unwrap16(probs[hd], tri)
                for kvh in range(2):
                    qs[kvh] = jnp.concatenate(
                        [cur_ref[rows, LANES * p:LANES * (p + 1)] for p in (2 * kvh, 2 * kvh + 1)], axis=0)
                for kvh in range(2):
                    pairs = (2 * kvh, 2 * kvh + 1)
                    out = jnp.zeros((2 * BLOCK, LANES), F32)
                    for j in range(2):
                        out = out + _mm(jnp.concatenate([p16[2 * p + j] for p in pairs], axis=0), v_ops[kvh][j])
                    d_outs = []
                    for r, p in enumerate(pairs):
                        bz = cur_ref[rows, BZ_OFF + LANES * p:BZ_OFF + LANES * (p + 1)].astype(F32)
                        sg = _sigmoid(bz)
                        dyp = dyb_ref[rows, LANES * p:LANES * (p + 1)].astype(F32)
                        out_p = out[BLOCK * r:BLOCK * (r + 1)]
                        d_out = dyp * (bz * sg)
                        dqz_ref[rows, B_WIDTH + LANES * p:B_WIDTH + LANES * (p + 1)] = (
                            dyp * out_p * (sg * (1.0 + bz * (1.0 - sg)))).astype(BF16)
                        dod = d_out * out_p
                        delta[2 * p] = jnp.sum(jnp.where(low, dod, 0.0), axis=-1, keepdims=True)
                        delta[2 * p + 1] = jnp.sum(jnp.where(low, 0.0, dod), axis=-1, keepdims=True)
                        d_outs.append(d_out.astype(BF16))
                    dos[kvh] = jnp.concatenate(d_outs, axis=0)
                delta_cols = jnp.zeros((BLOCK, LANES), F32)
                for hd in range(B_HEADS):
                    delta_cols = jnp.where(head_lane == hd, delta[hd], delta_cols)
                dsink_ref[0:1, :] += jnp.sum(-(spw_ref[rows, :] * delta_cols), axis=0, keepdims=True)
                for kvh in range(2):
                    pairs = (2 * kvh, 2 * kvh + 1)
                    for j in range(2):
                        dpf = _mm_nt(dos[kvh], v_ops[kvh][j])
                        for r, p in enumerate(pairs):
                            hd = 2 * p + j
                            ds = probs[hd] * (_wrap(dpf[BLOCK * r:BLOCK * (r + 1)], tri) - delta[hd])
                            dbias_ref[hd] += ds
                            ds16[hd] = _unwrap16(ds, tri)
                dk_acc = [[None, None], [None, None]]
                dv_acc = [[None, None], [None, None]]
                for kvh in range(2):
                    pairs = (2 * kvh, 2 * kvh + 1)
                    dq = jnp.zeros((2 * BLOCK, LANES), F32)
                    for j in range(2):
                        dss = jnp.concatenate([ds16[2 * p + j] for p in pairs], axis=0)
                        pss = jnp.concatenate([p16[2 * p + j] for p in pairs], axis=0)
                        dq = dq + _mm(dss, k_ops[kvh][j])
                        dk_acc[kvh][j] = _mm_tn(dss, qs[kvh])
                        dv_acc[kvh][j] = _mm_tn(pss, dos[kvh])
                    for r, p in enumerate(pairs):
                        dqz_ref[rows, LANES * p:LANES * (p + 1)] = (dq[BLOCK * r:BLOCK * (r + 1)] * SCALE).astype(BF16)

                def fold(acc):
                    return jnp.where(low_kv,
                                     acc[0][0] + pltpu.roll(acc[0][1], HEAD_DIM, 1),
                                     pltpu.roll(acc[1][0], HEAD_DIM, 1) + acc[1][1])

                acc_ref[pl.ds(start, 2 * BLOCK), :] += jnp.concatenate(
                    [fold(dk_acc) * SCALE, fold(dv_acc)], axis=1)
                return carry

            for i in range(per_tile):
                block(i, 0)
            dkv_ref[...] = acc_ref[0:tq, :].astype(BF16)
            acc_ref[0:BLOCK, :] = acc_ref[tq:tq + BLOCK, :]

        @pl.when(t == nt)
        def _():
            dkv_ref[0:BLOCK, :] = acc_ref[0:BLOCK, :].astype(BF16)
            dkv_ref[BLOCK:, :] = jnp.zeros((tq - BLOCK, 2 * KV_WIDTH), BF16)

    def cur_map(t):
        return (jnp.minimum(t, nt - 1), 0)

    def prev_map(t):
        return (jnp.maximum(jnp.minimum(t, nt - 1) * per_tile - 1, 0), K_OFF // (2 * KV_WIDTH))

    blocks = [_nbytes((tq, PB_WIDTH), BF16), _nbytes((BLOCK, 2 * KV_WIDTH), BF16), _nbytes((tq, B_WIDTH), BF16),
              2 * _nbytes((B_HEADS, BLOCK, BLOCK), F32), _nbytes((tq, qz_width), BF16),
              _nbytes((tq, 2 * KV_WIDTH), BF16), _nbytes((B_HEADS, LANES), F32),
              _nbytes((tq, B_HEADS * BLOCK), BF16), _nbytes((tq, LANES), F32)]
    scratch = _nbytes((tq + BLOCK, 2 * KV_WIDTH), BF16) + _nbytes((tq + BLOCK, 2 * KV_WIDTH), F32)
    return pl.pallas_call(
        body, name=f"attn_bwd_{layer}", grid=(nt + 1,),
        in_specs=[pl.BlockSpec((tq, PB_WIDTH), cur_map),
                  pl.BlockSpec((BLOCK, 2 * KV_WIDTH), prev_map),
                  pl.BlockSpec((tq, B_WIDTH), cur_map),
                  pl.BlockSpec((tq, B_HEADS * BLOCK), cur_map),
                  pl.BlockSpec((tq, LANES), cur_map),
                  pl.BlockSpec((B_HEADS, BLOCK, BLOCK), lambda t: (0, 0, 0))],
        out_specs=[pl.BlockSpec((tq, qz_width), cur_map),
                   pl.BlockSpec((tq, 2 * KV_WIDTH), lambda t: (t, 0)),
                   pl.BlockSpec((B_HEADS, BLOCK, BLOCK), lambda t: (0, 0, 0)),
                   pl.BlockSpec((B_HEADS, LANES), lambda t: (0, 0))],
        out_shape=[pltpu.HBM((tokens, qz_width), BF16),
                   pltpu.HBM((tokens + tq, 2 * KV_WIDTH), BF16),
                   pltpu.HBM((B_HEADS, BLOCK, BLOCK), F32),
                   pltpu.HBM((B_HEADS, LANES), F32)],
        scratch_shapes=[pltpu.VMEM((tq + BLOCK, 2 * KV_WIDTH), BF16),
                        pltpu.VMEM((tq + BLOCK, 2 * KV_WIDTH), F32)],
        compiler_params=pltpu.CompilerParams(dimension_semantics=("arbitrary",),
                                             vmem_limit_bytes=_vmem_limit(blocks, scratch)),
    )(_hbm(pb), _hbm(pb), _hbm(dyb), _hbm(probs_w), _hbm(sink_w), _hbm(dbias_in))


def _inproj_bwd_dx(da, dqz, dkv, w, x2, g, pre_g3, layer):
    tokens = x2.shape[0]
    tm = _token_tile(tokens)

    def body(da_ref, dqz_ref, dkv_ref, w_ref, x_ref, g_ref, pg_ref, gn_ref, dpg_ref):
        @pl.when(pl.program_id(0) == 0)
        def _():
            dpg_ref[...] = jnp.zeros_like(dpg_ref)

        dh = _mm(da_ref[...], w_ref[0:PA_WIDTH, :])
        dh += _mm(dqz_ref[:, 0:B_WIDTH], w_ref[PA_WIDTH:PA_WIDTH + B_WIDTH, :])
        dh += _mm(dkv_ref[...], w_ref[PA_WIDTH + K_OFF:PA_WIDTH + BZ_OFF, :])
        dh += _mm(dqz_ref[:, B_WIDTH:2 * B_WIDTH], w_ref[PA_WIDTH + BZ_OFF:IN_WIDTH, :])
        x = x_ref[...]
        r = lax.rsqrt(jnp.mean(x * x, axis=-1, keepdims=True) + NORM_EPS)
        xhat = x * r
        dhg = dh * pg_ref[...]
        dpg_ref[...] += jnp.sum(dh * xhat, axis=0, keepdims=True)
        gn_ref[...] = g_ref[...] + r * (dhg - xhat * jnp.mean(dhg * xhat, axis=-1, keepdims=True))

    blocks = [_nbytes((tm, PA_WIDTH), BF16), _nbytes((tm, 2 * B_WIDTH), BF16), _nbytes((tm, 2 * KV_WIDTH), BF16),
              _nbytes((D_MODEL, IN_WIDTH), BF16), 3 * _nbytes((tm, D_MODEL), F32)]
    return pl.pallas_call(
        body, name=f"inproj_bwd_dx_{layer}", grid=(tokens // tm,),
        in_specs=[pl.BlockSpec((tm, PA_WIDTH), lambda i: (i, 0)),
                  pl.BlockSpec((tm, 2 * B_WIDTH), lambda i: (i, 0)),
                  pl.BlockSpec((pl.Element(tm), pl.Element(2 * KV_WIDTH)), lambda i: (pl.multiple_of(i * tm + BLOCK, BLOCK), 0)),
                  pl.BlockSpec((IN_WIDTH, D_MODEL), lambda i: (0, 0)),
                  pl.BlockSpec((tm, D_MODEL), lambda i: (i, 0)),
                  pl.BlockSpec((tm, D_MODEL), lambda i: (i, 0)),
                  pl.BlockSpec((None, 1, D_MODEL), lambda i: (layer, 0, 0))],
        out_specs=[pl.BlockSpec((tm, D_MODEL), lambda i: (i, 0)),
                   pl.BlockSpec((1, D_MODEL), lambda i: (0, 0))],
        out_shape=[pltpu.HBM((tokens,D_MODEL), F32),
                   pltpu.HBM((1,D_MODEL), F32)],
        compiler_params=pltpu.CompilerParams(dimension_semantics=("arbitrary",),
                                             vmem_limit_bytes=_vmem_limit(blocks)),
    )(_hbm(da), _hbm(dqz), _hbm(dkv), _hbm(w), _hbm(x2), _hbm(g), pre_g3)


def _rel_bias_grad(dbias, col_bucket, flip):
    def body(db_ref, cb_ref, flip_ref, out_ref):
        cb = cb_ref[...]
        anti = flip_ref[...]
        sums = []
        for h in range(B_HEADS):
            x = db_ref[h]
            hi = x.astype(BF16)
            rest = x - hi.astype(F32)
            mid = rest.astype(BF16)
            low = (rest - mid.astype(F32)).astype(BF16)
            reversed_x = _mm(hi, anti) + _mm(mid, anti) + _mm(low, anti)
            rolled = pltpu.roll(reversed_x, 0, 1, stride=1, stride_axis=0)
            sums.append(jnp.sum(rolled, axis=0, keepdims=True))
        per_dist = jnp.concatenate(sums, axis=0)
        lane = lax.broadcasted_iota(jnp.int32, (B_HEADS, LANES), 1)
        out = jnp.zeros((B_HEADS, LANES), F32)
        for b in range(REL_BUCKETS):
            s = jnp.sum(jnp.where(cb == b, per_dist, 0.0), axis=-1, keepdims=True)
            out = jnp.where(lane == b, s, out)
        out_ref[...] = out

    vm = pl.BlockSpec(memory_space=pltpu.VMEM)
    return pl.pallas_call(
        body, name="rel_bias_grad",
        out_shape=jax.ShapeDtypeStruct((B_HEADS, LANES), F32),
        in_specs=[vm, vm, vm], out_specs=vm,
    )(dbias, col_bucket, flip)


def _reduce_adamw(slots, w, m, v, name):
    rows, cols = w.shape
    tr = _row_tile(rows)
    c1 = 1.0 / (1.0 - ADAM_B1 ** ADAM_STEP)
    c2 = 1.0 / (1.0 - ADAM_B2 ** ADAM_STEP)

    def body(s_ref, w_ref, m_ref, v_ref, g_ref, d_ref, nm_ref, nv_ref):
        g = s_ref[0].astype(F32)
        for i in range(1, N_DEV):
            g = g + s_ref[i].astype(F32)
        nm = ADAM_B1 * m_ref[...] + (1.0 - ADAM_B1) * g
        nv = ADAM_B2 * v_ref[...] + (1.0 - ADAM_B2) * (g * g)
        g_ref[...] = g
        nm_ref[...] = nm
        nv_ref[...] = nv
        d_ref[...] = -ADAM_LR * ((nm * c1) / (jnp.sqrt(nv * c2) + ADAM_EPS) + ADAM_WD * w_ref[...])

    blocks = [_nbytes((N_DEV, tr, cols), slots.dtype), 7 * _nbytes((tr, cols), F32)]
    tile = pl.BlockSpec((tr, cols), lambda i: (i, 0))
    return pl.pallas_call(
        body, name=name, grid=(rows // tr,),
        in_specs=[pl.BlockSpec((N_DEV, tr, cols), lambda i: (0, i, 0)), tile, tile, tile],
        out_specs=[tile] * 4,
        out_shape=[pltpu.HBM((rows, cols), F32)] * 4,
        compiler_params=pltpu.CompilerParams(dimension_semantics=("parallel",),
                                             vmem_limit_bytes=_vmem_limit(blocks)),
    )(_hbm(slots), _hbm(w), _hbm(m), _hbm(v))


_SMALL = ("pre_norm_g","post_norm_g", "ln_v_g", "ln_v_b", "b_spatial", "sinks", "rel_bias")


def _pack_small(parts):
    slabs = []
    for name in _SMALL:
        flat = parts[name].astype(F32).reshape(-1)
        pad = (-flat.shape[0]) % (8 * LANES)
        slabs.append(jnp.pad(flat, (0, pad)).reshape(-1, LANES))
    return jnp.concatenate(slabs, axis=0)


def _unpack_small(slab, shapes):
    out, row = {}, 0
    for name in _SMALL:
        size = int(np.prod(shapes[name]))
        rows = -(-size // (8 * LANES)) * 8
        out[name] = slab[row:row + rows].reshape(-1)[:size].reshape(shapes[name])
        row += rows
    return out


def kernel(x, pre_norm_g, w_in, ln_v_g, ln_v_b, w_spatial, b_spatial, sinks, rel_bias, w_out, post_norm_g, loss_target, m_pre_norm_g, m_w_in, m_ln_v_g, m_ln_v_b, m_w_spatial, m_b_spatial, m_sinks, m_rel_bias, m_w_out, m_post_norm_g, v_pre_norm_g, v_w_in, v_ln_v_g, v_ln_v_b, v_w_spatial, v_b_spatial, v_sinks, v_rel_bias, v_w_out, v_post_norm_g):
    b_loc, seq, _ = x.shape
    tokens = b_loc * seq
    depth = w_in.shape[0]
    in_shard = w_in.shape[2]
    out_shard = w_out.shape[1]
    assert in_shard * N_DEV == IN_WIDTH and out_shard * N_DEV == D_MODEL and seq % BLOCK == 0

    me = _slot(lax.axis_index("x"), lax.axis_index("y"), lax.axis_index("c"))
    w_in_t, m_w_in_t, v_w_in_t = (jnp.swapaxes(a, 1, 2) for a in (w_in, m_w_in, v_w_in))
    w_in16, w_out16 = w_in_t.astype(BF16), w_out.astype(BF16)

    def gather_start(layer, not_before):
        zones = [lax.empty((N_DEV, in_shard, D_MODEL), BF16), lax.empty((N_DEV, out_shard, D_MODEL), BF16)]
        sems, sends, zones, pre_g = _exchange_start(
            [w_in16[layer], w_out16[layer]], zones, [0, 0], f"weights_send_{layer}",
            pre_norm_g.reshape(depth, 1, D_MODEL), after=not_before)
        return (sems, sends, [0, 0]), zones, pre_g

    def full_weights(gin, gout):
        return gin.reshape(IN_WIDTH, D_MODEL), gout.reshape(D_MODEL, D_MODEL)

    def gather_wait(layer, started, zones, after):
        gin, gout = _exchange_wait([started], zones, after, f"weights_wait_{layer}")
        gin = lax.dynamic_update_index_in_dim(gin, w_in16[layer], me, 0)
        gout = lax.dynamic_update_index_in_dim(gout, w_out16[layer], me, 0)
        return full_weights(gin, gout)

    causal = jnp.tril(jnp.ones((CHUNK, CHUNK), dtype=bool))
    wm = jnp.where(causal, w_spatial, 0.0).astype(BF16)
    wmt = _hbm(jnp.swapaxes(wm, -1, -2))
    wm = _hbm(wm)
    bsb = _hbm(jnp.repeat(jnp.swapaxes(b_spatial, -1, -2), HEAD_DIM, axis=-1))
    pre_g3 = _hbm(pre_norm_g.reshape(depth, 1, D_MODEL))
    post_g3 = _hbm(post_norm_g.reshape(depth, 1, D_MODEL))
    ln_g3 = _hbm(ln_v_g.reshape(depth, 1, A_WIDTH))
    ln_b3 = _hbm(ln_v_b.reshape(depth, 1, A_WIDTH))
    bucket = jnp.asarray(_bucket_table())
    bias = _bias_table(rel_bias, bucket)

    xs, saved, weights = [x.reshape(tokens, D_MODEL)], [], []
    pending = None
    for layer in range(depth):
        if layer == 0:
            w, wo = full_weights(*_all_gather([w_in16[0], w_out16[0]], "weights_gather_0"))
        else:
            w, wo = gather_wait(layer, pending[0], pending[1], xs[-1])
        pre_g_fwd = pre_g3
        if layer + 1 < depth:
            pending = gather_start(layer + 1, w)
            pre_g_fwd = _hbm(pending[2])
        h, pa, pb, ya, kept, rstd_w = _inproj_gmlp_fwd(xs[-1], pre_g_fwd, w, ln_g3, ln_b3, wm, bsb, layer)
        if layer + 1 < depth:
            yb, pw, spw, y, x_next = _attn_outproj_fwd(pb, bias, sinks, ya, wo, xs[-1], post_g3, b_loc, layer)
            xs.append(x_next)
        else:
            yb, pw, spw, y, g, loss_part = _attn_outproj_fwd(pb, bias, sinks, ya, wo, xs[-1], post_g3, b_loc, layer,
                                                    target=loss_target.reshape(tokens, D_MODEL))
        saved.append((h, pa, pb, ya, yb, y, pw, spw, kept, rstd_w))
        weights.append((w, wo))
    loss = lax.psum(loss_part[0, 0], ("x", "y", "c"))

    grads = {name: [None] * depth for name in ("pre_norm_g","post_norm_g", "ln_v_g", "ln_v_b",
                                               "b_spatial", "sinks", "dbias")}
    zone_in = lax.empty((N_DEV, depth * in_shard, D_MODEL), BF16)
    zone_out = lax.empty((N_DEV, depth * out_shard, D_MODEL), BF16)
    zone_ws = lax.empty((N_DEV, depth * A_GROUPS * CHUNK, CHUNK), F32)
    started_in, started_out, own_in, own_out = [], [], [None] * depth, [None] * depth
    own_ws = [None] * depth
    dbias = jnp.zeros((B_HEADS, BLOCK, BLOCK), F32)
    for layer in reversed(range(depth)):
        h, pa, pb, ya, yb, y, pw, spw, kept, rstd_w = saved[layer]
        w, wo = weights[layer]
        dya, dyb, dwo, dpost = _outproj_bwd(g, y, ya, yb, wo, post_g3, layer)
        send_out = dwo.reshape(N_DEV, out_shard, D_MODEL)
        own_out[layer] = lax.dynamic_index_in_dim(send_out, me, 0, keepdims=False)
        sems, sends, (zone_out,), ln_g_bwd = _exchange_start(
            [send_out], [zone_out], [layer * out_shard], f"grads_send_out_{layer}", ln_v_g.reshape(depth, 1, A_WIDTH))
        started_out.append((sems, sends, [layer * out_shard]))
        dqz, dkv, dbias, dsink = _attn_bwd(pb, dyb, pw, spw, dbias, layer)
        da, dw, dws, dbs, dlg, dlb = _gmlp_bwd_dw(pa, kept, rstd_w, dya, h, dqz, dkv, _hbm(ln_g_bwd), ln_b3, wmt, layer)
        own_ws[layer] = dws.reshape(A_GROUPS * CHUNK, CHUNK)
        send_in = dw.reshape(N_DEV, in_shard, D_MODEL)
        own_in[layer] = lax.dynamic_index_in_dim(send_in, me, 0, keepdims=False)
        row_offs = [layer * A_GROUPS * CHUNK, layer * in_shard]
        sems, sends, (zone_ws, zone_in), pre_g_bwd = _exchange_start(
            [own_ws[layer], send_in], [zone_ws, zone_in], row_offs, f"grads_send_in_{layer}",
            pre_norm_g.reshape(depth, 1, D_MODEL))
        started_in.append((sems, sends, row_offs))
        g, dpre = _inproj_bwd_dx(da, dqz, dkv, w, xs[layer], g, _hbm(pre_g_bwd), layer)
        grads["b_spatial"][layer] = dbs[:, :A_GROUPS].T
        grads["ln_v_g"][layer] = dlg[0]
        grads["ln_v_b"][layer] = dlb[0]
        grads["sinks"][layer] = dsink[0, :B_HEADS]
        grads["pre_norm_g"][layer] = dpre[0]
        grads["post_norm_g"][layer] = dpost[0]
    grad_x = g.reshape(x.shape)
    col_bucket = jnp.asarray(np.broadcast_to(_bucket_table()[0:1, ::-1], (B_HEADS, BLOCK)))
    flip = jnp.asarray(np.eye(BLOCK, dtype=np.float32)[::-1], dtype=BF16)
    drel = _rel_bias_grad(dbias, col_bucket, flip)[:, :REL_BUCKETS].T

    (recv_out,) = _exchange_wait(started_out, [zone_out], g, "grads_wait_out")
    recv_out = lax.dynamic_update_index_in_dim(recv_out, jnp.concatenate(own_out, axis=0), me, 0)
    res_out = _reduce_adamw(recv_out, w_out.reshape(-1, D_MODEL), m_w_out.reshape(-1, D_MODEL),
                            v_w_out.reshape(-1, D_MODEL), "adamw_w_out")
    res_out = [r.reshape(w_out.shape) for r in res_out]

    small_w = dict(pre_norm_g=pre_norm_g, post_norm_g=post_norm_g, ln_v_g=ln_v_g,
                   ln_v_b=ln_v_b, b_spatial=b_spatial, sinks=sinks, rel_bias=rel_bias)
    small_m = dict(pre_norm_g=m_pre_norm_g, post_norm_g=m_post_norm_g, ln_v_g=m_ln_v_g,
                   ln_v_b=m_ln_v_b, b_spatial=m_b_spatial, sinks=m_sinks, rel_bias=m_rel_bias)
    small_v = dict(pre_norm_g=v_pre_norm_g, post_norm_g=v_post_norm_g, ln_v_g=v_ln_v_g,
                   ln_v_b=v_ln_v_b, b_spatial=v_b_spatial, sinks=v_sinks, rel_bias=v_rel_bias)
    small_g = {name: jnp.stack(grads[name]) for name in _SMALL if name != "rel_bias"}
    small_g["rel_bias"] = drel
    shapes = {name: small_w[name].shape for name in _SMALL}
    (slots,) = _all_gather([_pack_small(small_g)], "small_grads_all_gather")
    res_small = _reduce_adamw(slots, _pack_small(small_w), _pack_small(small_m), _pack_small(small_v), "adamw_small")

    recv_ws, recv_in = _exchange_wait(started_in, [zone_ws, zone_in], res_small[0], "grads_wait_in")
    recv_ws = lax.dynamic_update_index_in_dim(recv_ws, jnp.concatenate(own_ws, axis=0), me, 0)
    res_ws = _reduce_adamw(recv_ws, w_spatial.reshape(-1, CHUNK), m_w_spatial.reshape(-1, CHUNK),
                           v_w_spatial.reshape(-1, CHUNK), "adamw_w_spatial")
    res_ws = [r.reshape(w_spatial.shape) for r in res_ws]
    recv_in = lax.dynamic_update_index_in_dim(recv_in, jnp.concatenate(own_in, axis=0), me, 0)
    res_in = _reduce_adamw(recv_in, w_in_t.reshape(-1, D_MODEL), m_w_in_t.reshape(-1, D_MODEL),
                           v_w_in_t.reshape(-1, D_MODEL), "adamw_w_in")
    res_in = [jnp.swapaxes(r.reshape(w_in_t.shape), 1, 2) for r in res_in]
    res_small = [_unpack_small(r, shapes) for r in res_small]

    order = ("pre_norm_g", "w_in", "ln_v_g", "ln_v_b", "w_spatial", "b_spatial", "sinks", "rel_bias", "w_out",
             "post_norm_g")
    outs = [loss, grad_x]
    for kind in range(4):
        for name in order:
            if name == "w_in":
                outs.append(res_in[kind])
            elif name == "w_out":
                outs.append(res_out[kind])
            elif name == "w_spatial":
                outs.append(res_ws[kind])
            else:
                outs.append(res_small[kind][name])
    return tuple(outs)
```

```python
import math

import numpy as np
import jax
import jax.numpy as jnp
from jax import lax
from jax.experimental import pallas as pl
from jax.experimental.pallas import tpu as pltpu

F32 = jnp.float32
BF16 = jnp.bfloat16

D_MODEL = 1024
A_WIDTH = 512
A_GROUPS = 8
CHUNK = 128
B_HEADS = 8
HEAD_DIM = 64
B_WIDTH = 512
KV_WIDTH = 128
BLOCK = 128
REL_BUCKETS = 32
REL_MAX_DIST = 128
NORM_EPS = 1e-6
PA_WIDTH = 3 * A_WIDTH
PB_WIDTH = 2 * B_WIDTH + 2 * KV_WIDTH
IN_WIDTH = PA_WIDTH + PB_WIDTH
K_OFF, V_OFF, BZ_OFF = B_WIDTH, B_WIDTH + KV_WIDTH, B_WIDTH + 2 * KV_WIDTH
KEPT_WIDTH = 5 * A_WIDTH
SCALE = HEAD_DIM ** -0.5
NEG = -1e30
N_DEV = 8
LANES = 128

ADAM_LR = 0.001
ADAM_B1 = 0.9
ADAM_B2 = 0.999
ADAM_EPS = 1e-08
ADAM_WD = 0.01
ADAM_STEP = 10

V7X_VMEM_BYTES = 64 * 1024 * 1024
VMEM_TEMP_BYTES = 12 * 1024 * 1024
MESH = pl.DeviceIdType.MESH


def _vmem_limit(block_bytes, scratch_bytes=0):
    need = 2 * sum(block_bytes) + scratch_bytes + VMEM_TEMP_BYTES
    return int(min(need, V7X_VMEM_BYTES - 8 * 1024 * 1024))


def _nbytes(shape, dtype):
    return int(np.prod(shape)) * jnp.dtype(dtype).itemsize


def _token_tile(tokens, cap=512):
    tile = min(cap, tokens // 2)
    assert tokens % tile == 0 and tile % CHUNK == 0, tokens
    return tile


def _row_tile(rows, cap=512):
    best = 8
    for t in range(8, cap + 1, 8):
        if rows % t == 0:
            best = t
    assert rows % best == 0, rows
    return best


def _mm(a, b):
    return lax.dot_general(a, b, (((1,), (0,)), ((), ())), preferred_element_type=F32)


def _mm_nt(a, b):
    return lax.dot_general(a, b, (((1,), (1,)), ((), ())), preferred_element_type=F32)


def _mm_tn(a, b):
    return lax.dot_general(a, b, (((0,), (0,)), ((), ())), preferred_element_type=F32)


_GELU_C = math.sqrt(2.0 / math.pi)


_GELU_A = _GELU_C * 0.044715


def _gelu_parts(x):
    x2 = x * x
    t = jnp.tanh(x * (_GELU_C + _GELU_A * x2))
    return x2, t, 0.5 + 0.5 * t


def _gelu(x):
    return x * _gelu_parts(x)[2]


def _gelu_and_grad(x):
    x2, t, half_plus = _gelu_parts(x)
    grad = half_plus + (0.5 * x) * (1.0 - t * t) * (_GELU_C + (3.0 * _GELU_A) * x2)
    return x * half_plus, grad


def _sigmoid(x):
    return 0.5 + 0.5 * jnp.tanh(0.5 * x)


def _bucket_table():
    q = np.arange(BLOCK)[:, None]
    k = np.arange(BLOCK)[None, :]
    dist = np.where(k <= q, q - k, q + BLOCK - k)
    max_exact = REL_BUCKETS // 2
    safe = np.maximum(dist, 1).astype(np.float32)
    large = max_exact + (np.log(safe / np.float32(max_exact)) / np.float32(math.log(REL_MAX_DIST / max_exact))
                         * np.float32(REL_BUCKETS - max_exact)).astype(np.int32)
    large = np.minimum(large, REL_BUCKETS - 1)
    assert dist.min() >= 0 and dist.max() < BLOCK
    return np.where(dist < max_exact, dist, large).astype(np.int32)


def _hbm(x):
    return pltpu.with_memory_space_constraint(x, pltpu.HBM)


def _slot(px, py, pc):
    return 4 * px + 2 * py + pc


def _all_gather(arrs, name):
    n = len(arrs)

    def body(*refs):
        ins, outs = refs[:n], refs[n:2 * n]
        send_sems, recv_sems, local_sems = refs[2 * n:]
        x, y, c = lax.axis_index("x"), lax.axis_index("y"), lax.axis_index("c")
        me, sibling = (x, y, c), (x, y, 1 - c)
        chips = [(1 - x, y), (x, 1 - y), (1 - x, 1 - y)]

        def copy(a, k, block, to, src=None):
            dst = outs[a].at[_slot(*block)]
            return pltpu.make_async_remote_copy(
                src_ref=dst if src is None else src, dst_ref=dst,
                send_sem=send_sems.at[7 * a + k], recv_sem=recv_sems.at[7 * a + k],
                device_id=to, device_id_type=MESH)

        mine = [pltpu.make_async_copy(ins[a], outs[a].at[_slot(*me)], local_sems.at[a]) for a in range(n)]
        for cp in mine:
            cp.start()
        first = []
        for a in range(n):
            first.append(copy(a, 0, me, sibling, src=ins[a]))
            first += [copy(a, 1 + j, me, (*chip, c), src=ins[a]) for j, chip in enumerate(chips)]
        for cp in first:
            cp.start()
        passed = []
        for j, chip in enumerate(chips):
            for a in range(n):
                copy(a, 1 + j, (*chip, c), me).wait_recv()
                fwd = copy(a, 4 + j, (*chip, c), sibling)
                fwd.start()
                passed.append(fwd)
        for a in range(n):
            copy(a, 0, sibling, me).wait_recv()
            for j, chip in enumerate(chips):
                copy(a, 4 + j, (*chip, 1 - c), me).wait_recv()
        for cp in first + passed:
            cp.wait_send()
        for cp in mine:
            cp.wait()

    any_spec = pl.BlockSpec(memory_space=pl.ANY)
    return pl.pallas_call(
        body, name=name,
        out_shape=[jax.ShapeDtypeStruct((N_DEV,) + a.shape, a.dtype) for a in arrs],
        in_specs=[any_spec] * n, out_specs=[any_spec] * n,
        scratch_shapes=[pltpu.SemaphoreType.DMA((7 * n,)), pltpu.SemaphoreType.DMA((7 * n,)),
                        pltpu.SemaphoreType.DMA((n,))],
    )(*arrs)


def _all_gather_direct(arr, name):
    def body(in_ref, out_ref, send_sems, recv_sems, local_sem):
        me, peers = _peers()
        mine = pltpu.make_async_copy(in_ref, out_ref.at[_slot(*me)], local_sem)
        mine.start()

        def copy(k, origin, to):
            return pltpu.make_async_remote_copy(
                src_ref=in_ref, dst_ref=out_ref.at[_slot(*origin)], send_sem=send_sems.at[k],
                recv_sem=recv_sems.at[k], device_id=to, device_id_type=MESH)

        sends = [copy(k, me, peer) for k, peer in enumerate(peers)]
        for cp in sends:
            cp.start()
        for k, peer in enumerate(peers):
            copy(k, peer, me).wait_recv()
        for cp in sends:
            cp.wait_send()
        mine.wait()

    any_spec = pl.BlockSpec(memory_space=pl.ANY)
    return pl.pallas_call(
        body, name=name,
        out_shape=jax.ShapeDtypeStruct((N_DEV,) + arr.shape, arr.dtype),
        in_specs=[any_spec], out_specs=any_spec,
        scratch_shapes=[pltpu.SemaphoreType.DMA((7,)), pltpu.SemaphoreType.DMA((7,)), pltpu.SemaphoreType.DMA],
    )(arr)


_HBM_SPEC = pl.BlockSpec(memory_space=pltpu.HBM)
_SEM_SPEC = pl.BlockSpec(memory_space=pltpu.SEMAPHORE)
_DATAFLOW = pltpu.SideEffectType.DATAFLOW_SIDE_EFFECTING


def _peers():
    x, y, c = lax.axis_index("x"), lax.axis_index("y"), lax.axis_index("c")
    peers = []
    for k in range(1, N_DEV):
        fx, fy, fc = (k >> 2) & 1, (k >> 1) & 1, k & 1
        peers.append((1 - x if fx else x, 1 - y if fy else y, 1 - c if fc else c))
    return (x, y, c), peers


def _exchange_copy(send_ref, land_ref, row_off, src_slot, dst_slot, sems, idx, peer):
    rows = send_ref.shape[-2]
    src = send_ref.at[src_slot] if len(send_ref.shape) == 3 else send_ref
    return pltpu.make_async_remote_copy(
        src_ref=src, dst_ref=land_ref.at[dst_slot, pl.ds(row_off, rows), :],
        send_sem=sems[0].at[idx], recv_sem=sems[1].at[idx], device_id=peer, device_id_type=MESH)


def _exchange_start(sends, lands, row_offs, name, carry, after=None):
    n = len(sends)

    def body(*refs):
        ins, zones, carry_ref = refs[:n], refs[n:2 * n], refs[2 * n]
        first_out = 2 * n + 1 + (after is not None)
        sems = refs[first_out:first_out + 2]
        carry_out = refs[-1]
        me, peers = _peers()
        for a in range(n):
            for k, peer in enumerate(peers):
                _exchange_copy(ins[a], zones[a], row_offs[a], _slot(*peer), _slot(*me), sems, 7 * a + k, peer).start()
        carry_out[...] = carry_ref[...]

    arrays = [_hbm(a) for a in list(sends) + list(lands)]
    vmem = pl.BlockSpec(memory_space=pltpu.VMEM)
    out = pl.pallas_call(
        body, name=name,
        out_shape=(pltpu.SemaphoreType.DMA((7 * n,)), pltpu.SemaphoreType.DMA((7 * n,)),
                   *[pltpu.HBM(a.shape, a.dtype) for a in arrays], jax.ShapeDtypeStruct(carry.shape, carry.dtype)),
        in_specs=[_HBM_SPEC] * (2 * n) + [vmem] + ([pl.BlockSpec(memory_space=pl.ANY)] if after is not None else []),
        out_specs=(_SEM_SPEC, _SEM_SPEC, *[_HBM_SPEC] * (2 * n), vmem),
        input_output_aliases={i: 2 + i for i in range(2 * n)},
        compiler_params=pltpu.CompilerParams(has_side_effects=_DATAFLOW),
    )(*arrays, carry, *([after] if after is not None else []))
    return (out[0], out[1]), list(out[2:2 + n]), list(out[2 + n:2 + 2 * n]), out[-1]


def _exchange_wait(started, lands, after, name):
    n = len(lands)
    flat_sends = [s for _, sends, _ in started for s in sends]
    flat_sems = [s for sems, _, _ in started for s in sems]
    ns = len(flat_sends)

    def body(*refs):
        ins, zones = refs[:ns], refs[ns:ns + n]
        sem_refs = refs[ns + n:ns + n + len(flat_sems)]
        me, peers = _peers()
        pos = 0
        for call, (_, sends, row_offs) in enumerate(started):
            sems = sem_refs[2 * call:2 * call + 2]
            for a in range(len(sends)):
                for k, peer in enumerate(peers):
                    cp = _exchange_copy(ins[pos + a], zones[a], row_offs[a], _slot(*peer), _slot(*peer), sems,
                                        7 * a + k, peer)
                    cp.wait_send()
                    cp.wait_recv()
            pos += len(sends)

    arrays = list(flat_sends) + list(lands)
    out = pl.pallas_call(
        body, name=name,
        out_shape=tuple(pltpu.HBM(a.shape, a.dtype) for a in arrays),
        in_specs=[_HBM_SPEC] * len(arrays) + [_SEM_SPEC] * len(flat_sems) + [pl.BlockSpec(memory_space=pl.ANY)],
        out_specs=tuple([_HBM_SPEC] * len(arrays)),
        input_output_aliases={i: i for i in range(len(arrays))},
        compiler_params=pltpu.CompilerParams(has_side_effects=_DATAFLOW),
    )(*arrays, *flat_sems, after)
    return list(out[ns:])


def _bias_table(rel_bias, bucket):
    def body(rel_ref, bucket_ref, out_ref):
        bk = bucket_ref[...]
        for h in range(B_HEADS):
            def pick(b, acc, h=h):
                return jnp.where(bk == b, rel_ref[b, h], acc)
            out_ref[h] = lax.fori_loop(0, REL_BUCKETS, pick, jnp.zeros((BLOCK, BLOCK), F32))

    return pl.pallas_call(
        body, name="bias_table",
        out_shape=jax.ShapeDtypeStruct((B_HEADS, BLOCK, BLOCK), F32),
        in_specs=[pl.BlockSpec(memory_space=pltpu.SMEM), pl.BlockSpec(memory_space=pltpu.VMEM)],
        out_specs=pl.BlockSpec(memory_space=pltpu.VMEM),
    )(rel_bias, bucket)


def _inproj_gmlp_fwd(x2, pre_g3, w, ln_g3, ln_b3, wm, bsb, layer):
    tokens = x2.shape[0]
    tm = _token_tile(tokens)

    def body(x_ref, g_ref, w_ref, lg_ref, lb_ref, wm_ref, bsb_ref, h_ref, pa_ref, pb_ref, ya_ref, kept_ref, rstd_ref):
        x = x_ref[...]
        r = lax.rsqrt(jnp.mean(x * x, axis=-1, keepdims=True) + NORM_EPS)
        h = (x * r * g_ref[...]).astype(BF16)
        h_ref[...] = h
        pa_ref[...] = _mm_nt(h, w_ref[0:PA_WIDTH, :]).astype(BF16)
        pb_ref[...] = _mm_nt(h, w_ref[PA_WIDTH:IN_WIDTH, :]).astype(BF16)
        for ci in range(tm // CHUNK):
            rows = slice(ci * CHUNK, (ci + 1) * CHUNK)
            gu, gv, pz, u, vhat, rstd, _, mixed, sg = _gmlp_forward_chunk(
                pa_ref[rows, :], lg_ref[...], lb_ref[...], wm_ref, bsb_ref[...])
            ya_ref[rows, :] = (u * mixed * (pz * sg)).astype(BF16)
            for k, val in enumerate((u, gu, gv, vhat, mixed)):
                kept_ref[rows, A_WIDTH * k:A_WIDTH * (k + 1)] = val.astype(BF16)
            rstd_ref[rows, :] = jnp.broadcast_to(rstd, (CHUNK, LANES))

    blocks = [_nbytes((tm, D_MODEL), F32), _nbytes((D_MODEL, IN_WIDTH), BF16),
              _nbytes((tm, D_MODEL), BF16), _nbytes((tm, PA_WIDTH), BF16), _nbytes((tm, PB_WIDTH), BF16),
              _nbytes((A_GROUPS, CHUNK, CHUNK), BF16), _nbytes((CHUNK, A_WIDTH), F32), _nbytes((tm, A_WIDTH), BF16)]
    return pl.pallas_call(
        body, name=f"inproj_gmlp_fwd_{layer}", grid=(tokens // tm,),
        in_specs=[pl.BlockSpec((tm, D_MODEL), lambda i: (i, 0)),
                  pl.BlockSpec((None, 1, D_MODEL), lambda i: (layer, 0, 0)),
                  pl.BlockSpec((IN_WIDTH, D_MODEL), lambda i: (0, 0)),
                  pl.BlockSpec((None, 1, A_WIDTH), lambda i: (layer, 0, 0)),
                  pl.BlockSpec((None, 1, A_WIDTH), lambda i: (layer, 0, 0)),
                  pl.BlockSpec((None, A_GROUPS, CHUNK, CHUNK), lambda i: (layer, 0, 0, 0)),
                  pl.BlockSpec((None, CHUNK, A_WIDTH), lambda i: (layer, 0, 0))],
        out_specs=[pl.BlockSpec((tm, D_MODEL), lambda i: (i, 0)),
                   pl.BlockSpec((tm, PA_WIDTH), lambda i: (i, 0)),
                   pl.BlockSpec((tm, PB_WIDTH), lambda i: (i, 0)),
                   pl.BlockSpec((tm, A_WIDTH), lambda i: (i, 0)),
                   pl.BlockSpec((tm, KEPT_WIDTH), lambda i: (i, 0)),
                   pl.BlockSpec((tm, LANES), lambda i: (i, 0))],
        out_shape=[pltpu.HBM((tokens, D_MODEL), BF16),
                   pltpu.HBM((tokens, PA_WIDTH), BF16),
                   pltpu.HBM((tokens, PB_WIDTH), BF16),
                   pltpu.HBM((tokens, A_WIDTH), BF16),
                   pltpu.HBM((tokens, KEPT_WIDTH), BF16),
                   pltpu.HBM((tokens, LANES), F32)],
        compiler_params=pltpu.CompilerParams(dimension_semantics=("parallel",),
                                             vmem_limit_bytes=_vmem_limit(blocks + [_nbytes((tm, KEPT_WIDTH), BF16),
                                                                                    _nbytes((tm, LANES), F32)])),
    )(_hbm(x2), pre_g3, _hbm(w), ln_g3, ln_b3, wm, bsb)


def _gmlp_forward_chunk(pa, ln_g, ln_b, wm_ref, bsb):
    pu = pa[:, 0:A_WIDTH].astype(F32)
    pv = pa[:, A_WIDTH:2 * A_WIDTH].astype(F32)
    pz = pa[:, 2 * A_WIDTH:3 * A_WIDTH].astype(F32)
    u, gu = _gelu_and_grad(pu)
    vv, gv = _gelu_and_grad(pv)
    mu = jnp.mean(vv, axis=-1, keepdims=True)
    xc = vv - mu
    rstd = lax.rsqrt(jnp.mean(xc * xc, axis=-1, keepdims=True) + NORM_EPS)
    vhat = xc * rstd
    vnb = (vhat * ln_g + ln_b).astype(BF16)
    low = lax.broadcasted_iota(jnp.int32, (CHUNK, LANES), 1) < HEAD_DIM
    parts = []
    for p in range(A_GROUPS // 2):
        vp = vnb[:, LANES * p:LANES * (p + 1)]
        parts.append(jnp.where(low, _mm(wm_ref[2 * p], vp), _mm(wm_ref[2 * p + 1], vp)))
    mixed = jnp.concatenate(parts, axis=1) + bsb
    sg = _sigmoid(pz)
    return gu, gv, pz, u, vhat, rstd, vnb, mixed, sg


def _kv_variants(kv):
    t = kv.astype(F32)
    rolled = pltpu.roll(t, HEAD_DIM, 1)
    low = lax.broadcasted_iota(jnp.int32, t.shape, 1) < HEAD_DIM
    zero = jnp.zeros_like(t)
    head0 = (jnp.where(low, t, zero).astype(BF16), jnp.where(low, zero, rolled).astype(BF16))
    head1 = (jnp.where(low, rolled, zero).astype(BF16), jnp.where(low, zero, t).astype(BF16))
    return head0, head1


def _band_masks():
    row = lax.broadcasted_iota(jnp.int32, (BLOCK, BLOCK), 0)
    col = lax.broadcasted_iota(jnp.int32, (BLOCK, BLOCK), 1)
    return col <= row


def _wrap(full, tri):
    return jnp.where(tri, full[:, BLOCK:2 * BLOCK], full[:, 0:BLOCK])


def _attn_probs(sf, bias_h, sink, tri, kill):
    s = _wrap(sf, tri) + bias_h
    s = jnp.where(kill, NEG, s)
    m = jnp.maximum(jnp.max(s, axis=-1, keepdims=True), sink)
    e = jnp.exp(s - m)
    es = jnp.exp(sink - m)
    inv = 1.0 / (jnp.sum(e, axis=-1, keepdims=True) + es)
    return e * inv, es * inv


def _unwrap16(p, tri):
    p = p.astype(BF16)
    zero = jnp.zeros_like(p)
    return jnp.concatenate([jnp.where(tri, zero, p), jnp.where(tri, p, zero)], axis=1)


def _fill_kv(kv_ref, prev_ref, cur_ref):
    kv_ref[0:BLOCK, :] = prev_ref[...]
    kv_ref[BLOCK:, :] = cur_ref[:, K_OFF:K_OFF + 2 * KV_WIDTH]


def _attn_outproj_fwd(pb, bias, sinks, ya, wo, x2, post_g3, b_loc, layer, target=None):
    tokens = pb.shape[0]
    nb = tokens // b_loc // BLOCK
    tq = _token_tile(tokens)
    per_tile = tq // BLOCK
    steps = tokens // tq
    with_loss = target is not None
    half = D_MODEL // 2

    def body(*refs):
        sink_ref, cur_ref, prev_ref, bias_ref, ya_ref, woa_ref, wob_ref, x_ref, g_ref = refs[:9]
        if with_loss:
            t_ref, yb_ref, pw_ref, spw_ref, y_ref, grad_ref, loss_ref, kv_ref, acc_ref = refs[9:]
        else:
            yb_ref, pw_ref, spw_ref, y_ref, xn_ref, kv_ref = refs[9:]
        t = pl.program_id(0)
        _fill_kv(kv_ref, prev_ref, cur_ref)
        tri = _band_masks()
        head_lane = lax.broadcasted_iota(jnp.int32, (BLOCK, LANES), 1)
        y_from_a = _mm(ya_ref[...], woa_ref[...])

        def block(i, carry):
            start = i * BLOCK
            rows = pl.ds(start, BLOCK)
            first = lax.rem(t * per_tile + i, nb) == 0
            kill = jnp.logical_and(first, jnp.logical_not(tri))
            kv = kv_ref[pl.ds(start, 2 * BLOCK), :]
            k_ops = _kv_variants(kv[:, 0:KV_WIDTH])
            v_ops = _kv_variants(kv[:, KV_WIDTH:2 * KV_WIDTH])
            p16 = {}
            sink_cols = jnp.zeros((BLOCK, LANES), F32)
            for kvh in range(2):
                pairs = (2 * kvh, 2 * kvh + 1)
                qs = jnp.concatenate([cur_ref[rows, LANES * p:LANES * (p + 1)] for p in pairs], axis=0) * SCALE
                for j in range(2):
                    sf = _mm_nt(qs, k_ops[kvh][j])
                    for r, p in enumerate(pairs):
                        hd = 2 * p + j
                        probs, sink_p = _attn_probs(sf[BLOCK * r:BLOCK * (r + 1)], bias_ref[hd],
                                                    sink_ref[layer, hd], tri, kill)
                        probs = probs.astype(BF16)
                        p16[hd] = _unwrap16(probs, tri)
                        pw_ref[rows, LANES * hd:LANES * (hd + 1)] = probs
                        sink_cols = jnp.where(head_lane == hd, sink_p, sink_cols)
            spw_ref[rows, :] = sink_cols
            for kvh in range(2):
                pairs = (2 * kvh, 2 * kvh + 1)
                out = jnp.zeros((2 * BLOCK, LANES), F32)
                for j in range(2):
                    out = out + _mm(jnp.concatenate([p16[2 * p + j] for p in pairs], axis=0), v_ops[kvh][j])
                for r, p in enumerate(pairs):
                    bz = cur_ref[rows, BZ_OFF + LANES * p:BZ_OFF + LANES * (p + 1)].astype(F32)
                    yb_ref[rows, LANES * p:LANES * (p + 1)] = (
                        out[BLOCK * r:BLOCK * (r + 1)] * (bz * _sigmoid(bz))).astype(BF16)
            return carry

        for i in range(per_tile):
            block(i, 0)

        y = y_from_a + _mm(yb_ref[...], wob_ref[...])
        r = lax.rsqrt(jnp.mean(y * y, axis=-1, keepdims=True) + NORM_EPS)
        x_next = x_ref[...] + y * r * g_ref[...]
        y_ref[...] = y.astype(BF16)
        if not with_loss:
            xn_ref[...] = x_next
            return

        @pl.when(t == 0)
        def _():
            acc_ref[...] = jnp.zeros_like(acc_ref)

        err = x_next - t_ref[...]
        grad_ref[...] = err * (1.0 / D_MODEL)
        acc_ref[...] += jnp.sum(err * err, axis=0, keepdims=True)

        @pl.when(t == steps - 1)
        def _():
            total = jnp.sum(acc_ref[...], axis=-1, keepdims=True) * (0.5 / D_MODEL)
            loss_ref[...] = jnp.broadcast_to(total, loss_ref.shape)

    tile = pl.BlockSpec((tq, D_MODEL), lambda t: (t, 0))
    blocks = [_nbytes((tq, PB_WIDTH), BF16), _nbytes((BLOCK, 2 * KV_WIDTH), BF16),
              _nbytes((B_HEADS, BLOCK, BLOCK), F32), 2 * _nbytes((tq, B_WIDTH), BF16),
              2 * _nbytes((half, D_MODEL), BF16), 3 * _nbytes((tq, D_MODEL), F32), _nbytes((tq, D_MODEL), BF16),
              _nbytes((tq, B_HEADS * BLOCK), BF16), _nbytes((tq, LANES), F32)]
    scratch = _nbytes((tq + BLOCK, 2 * KV_WIDTH), BF16)
    in_specs = [pl.BlockSpec(memory_space=pltpu.SMEM),
                pl.BlockSpec((tq, PB_WIDTH), lambda t: (t, 0)),
                pl.BlockSpec((BLOCK, 2 * KV_WIDTH),
                             lambda t: (jnp.maximum(t * per_tile - 1, 0), K_OFF // (2 * KV_WIDTH))),
                pl.BlockSpec((B_HEADS, BLOCK, BLOCK), lambda t: (0, 0, 0)),
                pl.BlockSpec((tq, half), lambda t: (t, 0)),
                pl.BlockSpec((half, D_MODEL), lambda t: (0, 0)),
                pl.BlockSpec((half, D_MODEL), lambda t: (1, 0)),
                tile,
                pl.BlockSpec((None, 1, D_MODEL), lambda t: (layer, 0, 0))]
    out_specs = [pl.BlockSpec((tq, B_WIDTH), lambda t: (t, 0)), pl.BlockSpec((tq, B_HEADS * BLOCK), lambda t: (t, 0)),
                 pl.BlockSpec((tq, LANES), lambda t: (t, 0)), tile, tile]
    out_shape = [pltpu.HBM((tokens, B_WIDTH), BF16), pltpu.HBM((tokens, B_HEADS * BLOCK), BF16),
                 pltpu.HBM((tokens, LANES), F32), pltpu.HBM((tokens, D_MODEL), BF16),
                 pltpu.HBM((tokens, D_MODEL), F32)]
    scratch_shapes = [pltpu.VMEM((tq + BLOCK, 2 * KV_WIDTH), BF16)]
    operands = [sinks, _hbm(pb), _hbm(pb), _hbm(bias), _hbm(ya), _hbm(wo), _hbm(wo), _hbm(x2), post_g3]
    if with_loss:
        in_specs.append(tile)
        out_specs.append(pl.BlockSpec((1, LANES), lambda t: (0, 0)))
        out_shape.append(pltpu.HBM((1, LANES), F32))
        scratch_shapes.append(pltpu.VMEM((1, D_MODEL), F32))
        operands.append(_hbm(target))
    return pl.pallas_call(
        body, name=f"attn_outproj_fwd_{layer}", grid=(steps,),
        in_specs=in_specs, out_specs=out_specs, out_shape=out_shape, scratch_shapes=scratch_shapes,
        compiler_params=pltpu.CompilerParams(dimension_semantics=("arbitrary" if with_loss else "parallel",),
                                             vmem_limit_bytes=_vmem_limit(blocks, scratch)),
    )(*operands)


def _outproj_bwd(g, y, ya, yb, wo, post_g3, layer):
    tokens = g.shape[0]
    tm = _token_tile(tokens, cap=1024)
    half = D_MODEL // 2
    steps = tokens // tm

    def body(g_ref, y_ref, ya_ref, yb_ref, woa_ref, wob_ref, pg_ref, dya_ref, dyb_ref, dwo16_ref, dpg_ref, dwo_ref):
        @pl.when(pl.program_id(0) == 0)
        def _():
            dwo_ref[...] = jnp.zeros_like(dwo_ref)
            dpg_ref[...] = jnp.zeros_like(dpg_ref)

        gv = g_ref[...]
        yf = y_ref[...].astype(F32)
        r = lax.rsqrt(jnp.mean(yf * yf, axis=-1, keepdims=True) + NORM_EPS)
        yhat = yf * r
        gg = gv * pg_ref[...]
        dy = r * (gg - yhat * jnp.mean(gg * yhat, axis=-1, keepdims=True))
        dpg_ref[...] += jnp.sum(gv * yhat, axis=0, keepdims=True)
        dyb16 = dy.astype(BF16)
        dya_ref[...] = _mm_nt(dyb16, woa_ref[...]).astype(BF16)
        dyb_ref[...] = _mm_nt(dyb16, wob_ref[...]).astype(BF16)
        dwo_ref[0:half, :] += _mm_tn(ya_ref[...], dyb16)
        dwo_ref[half:D_MODEL, :] += _mm_tn(yb_ref[...], dyb16)

        @pl.when(pl.program_id(0) == steps - 1)
        def _():
            dwo16_ref[...] = dwo_ref[...].astype(BF16)

    blocks = [_nbytes((tm, D_MODEL), F32), _nbytes((tm, D_MODEL), BF16), 4 * _nbytes((tm, half), BF16),
              2 * _nbytes((half, D_MODEL), BF16), _nbytes((D_MODEL, D_MODEL), BF16)]
    return pl.pallas_call(
        body, name=f"outproj_bwd_{layer}", grid=(tokens // tm,),
        in_specs=[pl.BlockSpec((tm, D_MODEL), lambda i: (i, 0)),
                  pl.BlockSpec((tm, D_MODEL), lambda i: (i, 0)),
                  pl.BlockSpec((tm, half), lambda i: (i, 0)),
                  pl.BlockSpec((tm, half), lambda i: (i, 0)),
                  pl.BlockSpec((half, D_MODEL), lambda i: (0, 0)),
                  pl.BlockSpec((half, D_MODEL), lambda i: (1, 0)),
                  pl.BlockSpec((None, 1, D_MODEL), lambda i: (layer, 0, 0))],
        out_specs=[pl.BlockSpec((tm, half), lambda i: (i, 0)),
                   pl.BlockSpec((tm, half), lambda i: (i, 0)),
                   pl.BlockSpec((D_MODEL, D_MODEL), lambda i: (0, 0)),
                   pl.BlockSpec((1, D_MODEL), lambda i: (0, 0))],
        out_shape=[pltpu.HBM((tokens,half), BF16),
                   pltpu.HBM((tokens,half), BF16),
                   pltpu.HBM((D_MODEL,D_MODEL), BF16),
                   pltpu.HBM((1,D_MODEL), F32)],
        scratch_shapes=[pltpu.VMEM((D_MODEL, D_MODEL), F32)],
        compiler_params=pltpu.CompilerParams(dimension_semantics=("arbitrary",),
                                             vmem_limit_bytes=_vmem_limit(blocks, _nbytes((D_MODEL, D_MODEL), F32))),
    )(_hbm(g), _hbm(y), _hbm(ya), _hbm(yb), _hbm(wo), _hbm(wo), post_g3)


def _gmlp_bwd_dw(pa, kept, rstd_w, dya, h, dqz, dkv, ln_g3, ln_b3, wmt, layer):
    tokens = pa.shape[0]
    tc = _token_tile(tokens)
    steps = tokens // tc

    def body(pz_ref, kept_ref, rstd_ref, dya_ref, h_ref, dqz_ref, dkv_ref, lg_ref, lb_ref, wmt_ref,
             da_ref, dw_ref, dws_ref, dbs_ref, dlg_ref, dlb_ref, acc_ref, dbsb_ref):
        i = pl.program_id(0)

        @pl.when(i == 0)
        def _():
            acc_ref[...] = jnp.zeros_like(acc_ref)
            dws_ref[...] = jnp.zeros_like(dws_ref)
            dlg_ref[...] = jnp.zeros_like(dlg_ref)
            dlb_ref[...] = jnp.zeros_like(dlb_ref)
            dbsb_ref[...] = jnp.zeros_like(dbsb_ref)

        ln_g = lg_ref[...]
        low = lax.broadcasted_iota(jnp.int32, (CHUNK, LANES), 1) < HEAD_DIM
        hv = h_ref[...]
        acc_ref[:, PA_WIDTH:PA_WIDTH + B_WIDTH] += _mm_tn(hv, dqz_ref[:, 0:B_WIDTH])
        acc_ref[:, PA_WIDTH + K_OFF:PA_WIDTH + BZ_OFF] += _mm_tn(hv, dkv_ref[...])
        acc_ref[:, PA_WIDTH + BZ_OFF:IN_WIDTH] += _mm_tn(hv, dqz_ref[:, B_WIDTH:2 * B_WIDTH])

        for ci in range(tc // CHUNK):
            rows = slice(ci * CHUNK, (ci + 1) * CHUNK)
            u, gu, gv, vhat, mixed = (kept_ref[rows, A_WIDTH * k:A_WIDTH * (k + 1)].astype(F32) for k in range(5))
            rstd = jnp.concatenate([rstd_ref[rows, :]] * (A_WIDTH // LANES), axis=1)
            vnb = (vhat * ln_g + lb_ref[...]).astype(BF16)
            pz = pz_ref[rows, :].astype(F32)
            sg = _sigmoid(pz)
            dy = dya_ref[rows, :].astype(F32)
            sz = pz * sg
            dy_sz = dy * sz
            du = dy_sz * mixed
            dmixed = dy_sz * u
            dz = dy * (u * mixed) * (sg + sz * (1.0 - sg))
            dbsb_ref[...] += dmixed
            dmb = dmixed.astype(BF16)
            zero = jnp.zeros((CHUNK, LANES), BF16)
            parts = []
            for p in range(A_GROUPS // 2):
                dmp = dmb[:, LANES * p:LANES * (p + 1)]
                vp = vnb[:, LANES * p:LANES * (p + 1)]
                parts.append(jnp.where(low, _mm(wmt_ref[2 * p], dmp), _mm(wmt_ref[2 * p + 1], dmp)))
                dws_ref[2 * p] += _mm_nt(jnp.where(low, dmp, zero), vp)
                dws_ref[2 * p + 1] += _mm_nt(jnp.where(low, zero, dmp), vp)
            dvn = jnp.concatenate(parts, axis=1)
            dlg_ref[...] += jnp.sum(dvn * vhat, axis=0, keepdims=True)
            dlb_ref[...] += jnp.sum(dvn, axis=0, keepdims=True)
            dvh = dvn * ln_g
            dvv = rstd * (dvh - jnp.mean(dvh, axis=-1, keepdims=True)
                          - vhat * jnp.mean(dvh * vhat, axis=-1, keepdims=True))
            da_ref[rows, 0:A_WIDTH] = (du * gu).astype(BF16)
            da_ref[rows, A_WIDTH:2 * A_WIDTH] = (dvv * gv).astype(BF16)
            da_ref[rows, 2 * A_WIDTH:3 * A_WIDTH] = dz.astype(BF16)

        acc_ref[:, 0:PA_WIDTH] += _mm_tn(hv, da_ref[...])

        @pl.when(i == steps - 1)
        def _():
            for c in range(IN_WIDTH // LANES):
                cols = slice(LANES * c, LANES * (c + 1))
                dw_ref[cols, :] = acc_ref[:, cols].T.astype(BF16)
            causal = (lax.broadcasted_iota(jnp.int32, (CHUNK, CHUNK), 0)
                      >= lax.broadcasted_iota(jnp.int32, (CHUNK, CHUNK), 1))
            for h in range(A_GROUPS):
                dws_ref[h] = jnp.where(causal, dws_ref[h], 0.0)
            acc = dbsb_ref[...]
            lane_full = lax.broadcasted_iota(jnp.int32, (CHUNK, A_WIDTH), 1)
            lane_out = lax.broadcasted_iota(jnp.int32, (CHUNK, LANES), 1)
            out = jnp.zeros((CHUNK, LANES), F32)
            for h in range(A_GROUPS):
                in_group = jnp.logical_and(lane_full >= HEAD_DIM * h, lane_full < HEAD_DIM * (h + 1))
                s = jnp.sum(jnp.where(in_group, acc, 0.0), axis=-1, keepdims=True)
                out = jnp.where(lane_out == h, s, out)
            dbs_ref[...] = out

    blocks = [_nbytes((tc, KEPT_WIDTH), BF16), 2 * _nbytes((tc, A_WIDTH), BF16), _nbytes((A_GROUPS, CHUNK, CHUNK), BF16),
              _nbytes((tc, LANES), F32), _nbytes((tc, PA_WIDTH), BF16), _nbytes((A_GROUPS, CHUNK, CHUNK), F32),
              _nbytes((CHUNK, LANES), F32), _nbytes((tc, D_MODEL), BF16), _nbytes((tc, 2 * B_WIDTH), BF16),
              _nbytes((tc, 2 * KV_WIDTH), BF16), _nbytes((D_MODEL, IN_WIDTH), BF16)]
    scratch = _nbytes((D_MODEL, IN_WIDTH), F32) + _nbytes((CHUNK, A_WIDTH), F32)
    in_specs = [pl.BlockSpec((tc, A_WIDTH), lambda i: (i, 2)),
                pl.BlockSpec((tc, KEPT_WIDTH), lambda i: (i, 0)),
                pl.BlockSpec((tc, LANES), lambda i: (i, 0)),
                pl.BlockSpec((tc, A_WIDTH), lambda i: (i, 0)),
                pl.BlockSpec((tc, D_MODEL), lambda i: (i, 0)),
                pl.BlockSpec((tc, 2 * B_WIDTH), lambda i: (i, 0)),
                pl.BlockSpec((pl.Element(tc), pl.Element(2 * KV_WIDTH)),
                             lambda i: (pl.multiple_of(i * tc + BLOCK, BLOCK), 0)),
                pl.BlockSpec((None, 1, A_WIDTH), lambda i: (layer, 0, 0)),
                pl.BlockSpec((None, 1, A_WIDTH), lambda i: (layer, 0, 0)),
                pl.BlockSpec((None, A_GROUPS, CHUNK, CHUNK), lambda i: (layer, 0, 0, 0))]
    operands = [_hbm(pa), _hbm(kept), _hbm(rstd_w), _hbm(dya), _hbm(h), _hbm(dqz), _hbm(dkv), ln_g3, ln_b3, wmt]
    out_specs = [pl.BlockSpec((tc, PA_WIDTH), lambda i: (i, 0)),
                 pl.BlockSpec((IN_WIDTH, D_MODEL), lambda i: (0, 0)),
                 pl.BlockSpec((A_GROUPS, CHUNK, CHUNK), lambda i: (0, 0, 0)),
                 pl.BlockSpec((CHUNK, LANES), lambda i: (0, 0)),
                 pl.BlockSpec((1, A_WIDTH), lambda i: (0, 0)),
                 pl.BlockSpec((1, A_WIDTH), lambda i: (0, 0))]
    out_shape = [pltpu.HBM((tokens, PA_WIDTH), BF16),
                 pltpu.HBM((IN_WIDTH, D_MODEL), BF16),
                 pltpu.HBM((A_GROUPS, CHUNK, CHUNK), F32),
                 pltpu.HBM((CHUNK, LANES), F32),
                 pltpu.HBM((1, A_WIDTH), F32),
                 pltpu.HBM((1, A_WIDTH), F32)]
    scratch_shapes = [pltpu.VMEM((D_MODEL, IN_WIDTH), F32), pltpu.VMEM((CHUNK, A_WIDTH), F32)]
    return pl.pallas_call(
        body, name=f"gmlp_bwd_dw_{layer}", grid=(steps,),
        in_specs=in_specs, out_specs=out_specs, out_shape=out_shape, scratch_shapes=scratch_shapes,
        compiler_params=pltpu.CompilerParams(dimension_semantics=("arbitrary",),
                                             vmem_limit_bytes=_vmem_limit(blocks, scratch)),
    )(*operands)


def _attn_bwd(pb, dyb, probs_w, sink_w, dbias_in, layer):
    tokens = pb.shape[0]
    qz_width = 2 * B_WIDTH
    tq = _token_tile(tokens)
    per_tile = tq // BLOCK
    nt = tokens // tq

    def body(cur_ref, prev_ref, dyb_ref, pw_ref, spw_ref, dbias_in_ref, dqz_ref, dkv_ref, dbias_ref, dsink_ref,
             kv_ref, acc_ref):
        t = pl.program_id(0)

        @pl.when(t == 0)
        def _():
            dbias_ref[...] = dbias_in_ref[...]
            dsink_ref[...] = jnp.zeros_like(dsink_ref)
            acc_ref[0:BLOCK, :] = jnp.zeros((BLOCK, 2 * KV_WIDTH), F32)

        @pl.when(t < nt)
        def _():
            acc_ref[BLOCK:, :] = jnp.zeros((tq, 2 * KV_WIDTH), F32)
            _fill_kv(kv_ref, prev_ref, cur_ref)
            tri = _band_masks()
            low = lax.broadcasted_iota(jnp.int32, (BLOCK, LANES), 1) < HEAD_DIM
            low_kv = lax.broadcasted_iota(jnp.int32, (2 * BLOCK, LANES), 1) < HEAD_DIM

            head_lane = lax.broadcasted_iota(jnp.int32, (BLOCK, LANES), 1)

            def block(i, carry):
                start = i * BLOCK
                rows = pl.ds(start, BLOCK)
                kv = kv_ref[pl.ds(start, 2 * BLOCK), :]
                k_ops = _kv_variants(kv[:, 0:KV_WIDTH])
                v_ops = _kv_variants(kv[:, KV_WIDTH:2 * KV_WIDTH])
                probs, p16, qs, dos, delta, ds16 = {}, {}, {}, {}, {}, {}
                for hd in range(B_HEADS):
                    kept_p = pw_ref[rows, LANES * hd:LANES * (hd + 1)]
                    probs[hd] = kept_p.astype(F32)
                    p16[hd] = _unwrap16(kept_p, tri)
                for kvh in range(2):
                    qs[kvh] = jnp.concatenate(
                        [cur_ref[rows, LANES * p:LANES * (p + 1)] for p in (2 * kvh, 2 * kvh + 1)], axis=0)
                for kvh in range(2):
                    pairs = (2 * kvh, 2 * kvh + 1)
                    out = jnp.zeros((2 * BLOCK, LANES), F32)
                    for j in range(2):
                        out = out + _mm(jnp.concatenate([p16[2 * p + j] for p in pairs], axis=0), v_ops[kvh][j])
                    d_outs = []
                    for r, p in enumerate(pairs):
                        bz = cur_ref[rows, BZ_OFF + LANES * p:BZ_OFF + LANES * (p + 1)].astype(F32)
                        sg = _sigmoid(bz)
                        dyp = dyb_ref[rows, LANES * p:LANES * (p + 1)].astype(F32)
                        out_p = out[BLOCK * r:BLOCK * (r + 1)]
                        d_out = dyp * (bz * sg)
                        dqz_ref[rows, B_WIDTH + LANES * p:B_WIDTH + LANES * (p + 1)] = (
                            dyp * out_p * (sg * (1.0 + bz * (1.0 - sg)))).astype(BF16)
                        dod = d_out * out_p
                        delta[2 * p] = jnp.sum(jnp.where(low, dod, 0.0), axis=-1, keepdims=True)
                        delta[2 * p + 1] = jnp.sum(jnp.where(low, 0.0, dod), axis=-1, keepdims=True)
                        d_outs.append(d_out.astype(BF16))
                    dos[kvh] = jnp.concatenate(d_outs, axis=0)
                delta_cols = jnp.zeros((BLOCK, LANES), F32)
                for hd in range(B_HEADS):
                    delta_cols = jnp.where(head_lane == hd, delta[hd], delta_cols)
                dsink_ref[0:1, :] += jnp.sum(-(spw_ref[rows, :] * delta_cols), axis=0, keepdims=True)
                for kvh in range(2):
                    pairs = (2 * kvh, 2 * kvh + 1)
                    for j in range(2):
                        dpf = _mm_nt(dos[kvh], v_ops[kvh][j])
                        for r, p in enumerate(pairs):
                            hd = 2 * p + j
                            ds = probs[hd] * (_wrap(dpf[BLOCK * r:BLOCK * (r + 1)], tri) - delta[hd])
                            dbias_ref[hd] += ds
                            ds16[hd] = _unwrap16(ds, tri)
                dk_acc = [[None, None], [None, None]]
                dv_acc = [[None, None], [None, None]]
                for kvh in range(2):
                    pairs = (2 * kvh, 2 * kvh + 1)
                    dq = jnp.zeros((2 * BLOCK, LANES), F32)
                    for j in range(2):
                        dss = jnp.concatenate([ds16[2 * p + j] for p in pairs], axis=0)
                        pss = jnp.concatenate([p16[2 * p + j] for p in pairs], axis=0)
                        dq = dq + _mm(dss, k_ops[kvh][j])
                        dk_acc[kvh][j] = _mm_tn(dss, qs[kvh])
                        dv_acc[kvh][j] = _mm_tn(pss, dos[kvh])
                    for r, p in enumerate(pairs):
                        dqz_ref[rows, LANES * p:LANES * (p + 1)] = (dq[BLOCK * r:BLOCK * (r + 1)] * SCALE).astype(BF16)

                def fold(acc):
                    return jnp.where(low_kv,
                                     acc[0][0] + pltpu.roll(acc[0][1], HEAD_DIM, 1),
                                     pltpu.roll(acc[1][0], HEAD_DIM, 1) + acc[1][1])

                acc_ref[pl.ds(start, 2 * BLOCK), :] += jnp.concatenate(
                    [fold(dk_acc) * SCALE, fold(dv_acc)], axis=1)
                return carry

            for i in range(per_tile):
                block(i, 0)
            dkv_ref[...] = acc_ref[0:tq, :].astype(BF16)
            acc_ref[0:BLOCK, :] = acc_ref[tq:tq + BLOCK, :]

        @pl.when(t == nt)
        def _():
            dkv_ref[0:BLOCK, :] = acc_ref[0:BLOCK, :].astype(BF16)
            dkv_ref[BLOCK:, :] = jnp.zeros((tq - BLOCK, 2 * KV_WIDTH), BF16)

    def cur_map(t):
        return (jnp.minimum(t, nt - 1), 0)

    def prev_map(t):
        return (jnp.maximum(jnp.minimum(t, nt - 1) * per_tile - 1, 0), K_OFF // (2 * KV_WIDTH))

    blocks = [_nbytes((tq, PB_WIDTH), BF16), _nbytes((BLOCK, 2 * KV_WIDTH), BF16), _nbytes((tq, B_WIDTH), BF16),
              2 * _nbytes((B_HEADS, BLOCK, BLOCK), F32), _nbytes((tq, qz_width), BF16),
              _nbytes((tq, 2 * KV_WIDTH), BF16), _nbytes((B_HEADS, LANES), F32),
              _nbytes((tq, B_HEADS * BLOCK), BF16), _nbytes((tq, LANES), F32)]
    scratch = _nbytes((tq + BLOCK, 2 * KV_WIDTH), BF16) + _nbytes((tq + BLOCK, 2 * KV_WIDTH), F32)
    return pl.pallas_call(
        body, name=f"attn_bwd_{layer}", grid=(nt + 1,),
        in_specs=[pl.BlockSpec((tq, PB_WIDTH), cur_map),
                  pl.BlockSpec((BLOCK, 2 * KV_WIDTH), prev_map),
                  pl.BlockSpec((tq, B_WIDTH), cur_map),
                  pl.BlockSpec((tq, B_HEADS * BLOCK), cur_map),
                  pl.BlockSpec((tq, LANES), cur_map),
                  pl.BlockSpec((B_HEADS, BLOCK, BLOCK), lambda t: (0, 0, 0))],
        out_specs=[pl.BlockSpec((tq, qz_width), cur_map),
                   pl.BlockSpec((tq, 2 * KV_WIDTH), lambda t: (t, 0)),
                   pl.BlockSpec((B_HEADS, BLOCK, BLOCK), lambda t: (0, 0, 0)),
                   pl.BlockSpec((B_HEADS, LANES), lambda t: (0, 0))],
        out_shape=[pltpu.HBM((tokens, qz_width), BF16),
                   pltpu.HBM((tokens + tq, 2 * KV_WIDTH), BF16),
                   pltpu.HBM((B_HEADS, BLOCK, BLOCK), F32),
                   pltpu.HBM((B_HEADS, LANES), F32)],
        scratch_shapes=[pltpu.VMEM((tq + BLOCK, 2 * KV_WIDTH), BF16),
                        pltpu.VMEM((tq + BLOCK, 2 * KV_WIDTH), F32)],
        compiler_params=pltpu.CompilerParams(dimension_semantics=("arbitrary",),
                                             vmem_limit_bytes=_vmem_limit(blocks, scratch)),
    )(_hbm(pb), _hbm(pb), _hbm(dyb), _hbm(probs_w), _hbm(sink_w), _hbm(dbias_in))


def _inproj_bwd_dx(da, dqz, dkv, w, x2, g, pre_g3, layer):
    tokens = x2.shape[0]
    tm = _token_tile(tokens)

    def body(da_ref, dqz_ref, dkv_ref, w_ref, x_ref, g_ref, pg_ref, gn_ref, dpg_ref):
        @pl.when(pl.program_id(0) == 0)
        def _():
            dpg_ref[...] = jnp.zeros_like(dpg_ref)

        dh = _mm(da_ref[...], w_ref[0:PA_WIDTH, :])
        dh += _mm(dqz_ref[:, 0:B_WIDTH], w_ref[PA_WIDTH:PA_WIDTH + B_WIDTH, :])
        dh += _mm(dkv_ref[...], w_ref[PA_WIDTH + K_OFF:PA_WIDTH + BZ_OFF, :])
        dh += _mm(dqz_ref[:, B_WIDTH:2 * B_WIDTH], w_ref[PA_WIDTH + BZ_OFF:IN_WIDTH, :])
        x = x_ref[...]
        r = lax.rsqrt(jnp.mean(x * x, axis=-1, keepdims=True) + NORM_EPS)
        xhat = x * r
        dhg = dh * pg_ref[...]
        dpg_ref[...] += jnp.sum(dh * xhat, axis=0, keepdims=True)
        gn_ref[...] = g_ref[...] + r * (dhg - xhat * jnp.mean(dhg * xhat, axis=-1, keepdims=True))

    blocks = [_nbytes((tm, PA_WIDTH), BF16), _nbytes((tm, 2 * B_WIDTH), BF16), _nbytes((tm, 2 * KV_WIDTH), BF16),
              _nbytes((D_MODEL, IN_WIDTH), BF16), 3 * _nbytes((tm, D_MODEL), F32)]
    return pl.pallas_call(
        body, name=f"inproj_bwd_dx_{layer}", grid=(tokens // tm,),
        in_specs=[pl.BlockSpec((tm, PA_WIDTH), lambda i: (i, 0)),
                  pl.BlockSpec((tm, 2 * B_WIDTH), lambda i: (i, 0)),
                  pl.BlockSpec((pl.Element(tm), pl.Element(2 * KV_WIDTH)), lambda i: (pl.multiple_of(i * tm + BLOCK, BLOCK), 0)),
                  pl.BlockSpec((IN_WIDTH, D_MODEL), lambda i: (0, 0)),
                  pl.BlockSpec((tm, D_MODEL), lambda i: (i, 0)),
                  pl.BlockSpec((tm, D_MODEL), lambda i: (i, 0)),
                  pl.BlockSpec((None, 1, D_MODEL), lambda i: (layer, 0, 0))],
        out_specs=[pl.BlockSpec((tm, D_MODEL), lambda i: (i, 0)),
                   pl.BlockSpec((1, D_MODEL), lambda i: (0, 0))],
        out_shape=[pltpu.HBM((tokens,D_MODEL), F32),
                   pltpu.HBM((1,D_MODEL), F32)],
        compiler_params=pltpu.CompilerParams(dimension_semantics=("arbitrary",),
                                             vmem_limit_bytes=_vmem_limit(blocks)),
    )(_hbm(da), _hbm(dqz), _hbm(dkv), _hbm(w), _hbm(x2), _hbm(g), pre_g3)


def _rel_bias_grad(dbias, col_bucket, flip):
    def body(db_ref, cb_ref, flip_ref, out_ref):
        cb = cb_ref[...]
        anti = flip_ref[...]
        sums = []
        for h in range(B_HEADS):
            x = db_ref[h]
            hi = x.astype(BF16)
            rest = x - hi.astype(F32)
            mid = rest.astype(BF16)
            low = (rest - mid.astype(F32)).astype(BF16)
            reversed_x = _mm(hi, anti) + _mm(mid, anti) + _mm(low, anti)
            rolled = pltpu.roll(reversed_x, 0, 1, stride=1, stride_axis=0)
            sums.append(jnp.sum(rolled, axis=0, keepdims=True))
        per_dist = jnp.concatenate(sums, axis=0)
        lane = lax.broadcasted_iota(jnp.int32, (B_HEADS, LANES), 1)
        out = jnp.zeros((B_HEADS, LANES), F32)
        for b in range(REL_BUCKETS):
            s = jnp.sum(jnp.where(cb == b, per_dist, 0.0), axis=-1, keepdims=True)
            out = jnp.where(lane == b, s, out)
        out_ref[...] = out

    vm = pl.BlockSpec(memory_space=pltpu.VMEM)
    return pl.pallas_call(
        body, name="rel_bias_grad",
        out_shape=jax.ShapeDtypeStruct((B_HEADS, LANES), F32),
        in_specs=[vm, vm, vm], out_specs=vm,
    )(dbias, col_bucket, flip)


def _reduce_adamw(slots, w, m, v, name):
    rows, cols = w.shape
    tr = _row_tile(rows)
    c1 = 1.0 / (1.0 - ADAM_B1 ** ADAM_STEP)
    c2 = 1.0 / (1.0 - ADAM_B2 ** ADAM_STEP)

    def body(s_ref, w_ref, m_ref, v_ref, g_ref, d_ref, nm_ref, nv_ref):
        g = s_ref[0].astype(F32)
        for i in range(1, N_DEV):
            g = g + s_ref[i].astype(F32)
        nm = ADAM_B1 * m_ref[...] + (1.0 - ADAM_B1) * g
        nv = ADAM_B2 * v_ref[...] + (1.0 - ADAM_B2) * (g * g)
        g_ref[...] = g
        nm_ref[...] = nm
        nv_ref[...] = nv
        d_ref[...] = -ADAM_LR * ((nm * c1) / (jnp.sqrt(nv * c2) + ADAM_EPS) + ADAM_WD * w_ref[...])

    blocks = [_nbytes((N_DEV, tr, cols), slots.dtype), 7 * _nbytes((tr, cols), F32)]
    tile = pl.BlockSpec((tr, cols), lambda i: (i, 0))
    return pl.pallas_call(
        body, name=name, grid=(rows // tr,),
        in_specs=[pl.BlockSpec((N_DEV, tr, cols), lambda i: (0, i, 0)), tile, tile, tile],
        out_specs=[tile] * 4,
        out_shape=[pltpu.HBM((rows, cols), F32)] * 4,
        compiler_params=pltpu.CompilerParams(dimension_semantics=("parallel",),
                                             vmem_limit_bytes=_vmem_limit(blocks)),
    )(_hbm(slots), _hbm(w), _hbm(m), _hbm(v))


_SMALL = ("pre_norm_g","post_norm_g", "ln_v_g", "ln_v_b", "b_spatial", "sinks", "rel_bias", "loss")


def _pack_small(parts):
    slabs = []
    for name in _SMALL:
        flat = parts[name].astype(F32).reshape(-1)
        pad = (-flat.shape[0]) % (8 * LANES)
        slabs.append(jnp.pad(flat, (0, pad)).reshape(-1, LANES))
    return jnp.concatenate(slabs, axis=0)


def _unpack_small(slab, shapes):
    out, row = {}, 0
    for name in _SMALL:
        size = int(np.prod(shapes[name]))
        rows = -(-size // (8 * LANES)) * 8
        out[name] = slab[row:row + rows].reshape(-1)[:size].reshape(shapes[name])
        row += rows
    return out


def kernel(x, pre_norm_g, w_in, ln_v_g, ln_v_b, w_spatial, b_spatial, sinks, rel_bias, w_out, post_norm_g, loss_target, m_pre_norm_g, m_w_in, m_ln_v_g, m_ln_v_b, m_w_spatial, m_b_spatial, m_sinks, m_rel_bias, m_w_out, m_post_norm_g, v_pre_norm_g, v_w_in, v_ln_v_g, v_ln_v_b, v_w_spatial, v_b_spatial, v_sinks, v_rel_bias, v_w_out, v_post_norm_g):
    b_loc, seq, _ = x.shape
    tokens = b_loc * seq
    depth = w_in.shape[0]
    in_shard = w_in.shape[2]
    out_shard = w_out.shape[1]
    assert in_shard * N_DEV == IN_WIDTH and out_shard * N_DEV == D_MODEL and seq % BLOCK == 0

    me = _slot(lax.axis_index("x"), lax.axis_index("y"), lax.axis_index("c"))
    w_in_t, m_w_in_t, v_w_in_t = (jnp.swapaxes(a, 1, 2) for a in (w_in, m_w_in, v_w_in))
    w_in16, w_out16 = w_in_t.astype(BF16), w_out.astype(BF16)

    def gather_start(layer, not_before):
        zones = [lax.empty((N_DEV, in_shard, D_MODEL), BF16), lax.empty((N_DEV, out_shard, D_MODEL), BF16)]
        sems, sends, zones, pre_g = _exchange_start(
            [w_in16[layer], w_out16[layer]], zones, [0, 0], f"weights_send_{layer}",
            pre_norm_g.reshape(depth, 1, D_MODEL), after=not_before)
        return (sems, sends, [0, 0]), zones, pre_g

    def full_weights(gin, gout):
        return gin.reshape(IN_WIDTH, D_MODEL), gout.reshape(D_MODEL, D_MODEL)

    def gather_wait(layer, started, zones, after):
        gin, gout = _exchange_wait([started], zones, after, f"weights_wait_{layer}")
        gin = lax.dynamic_update_index_in_dim(gin, w_in16[layer], me, 0)
        gout = lax.dynamic_update_index_in_dim(gout, w_out16[layer], me, 0)
        return full_weights(gin, gout)

    causal = jnp.tril(jnp.ones((CHUNK, CHUNK), dtype=bool))
    wm = jnp.where(causal, w_spatial, 0.0).astype(BF16)
    wmt = _hbm(jnp.swapaxes(wm, -1, -2))
    wm = _hbm(wm)
    bsb = _hbm(jnp.repeat(jnp.swapaxes(b_spatial, -1, -2), HEAD_DIM, axis=-1))
    pre_g3 = _hbm(pre_norm_g.reshape(depth, 1, D_MODEL))
    post_g3 = _hbm(post_norm_g.reshape(depth, 1, D_MODEL))
    ln_g3 = _hbm(ln_v_g.reshape(depth, 1, A_WIDTH))
    ln_b3 = _hbm(ln_v_b.reshape(depth, 1, A_WIDTH))
    bucket = jnp.asarray(_bucket_table())
    bias = _bias_table(rel_bias, bucket)

    xs, saved, weights = [x.reshape(tokens, D_MODEL)], [], []
    pending = None
    for layer in range(depth):
        if layer == 0:
            w, wo = full_weights(*_all_gather([w_in16[0], w_out16[0]], "weights_gather_0"))
        else:
            w, wo = gather_wait(layer, pending[0], pending[1], xs[-1])
        pre_g_fwd = pre_g3
        if layer + 1 < depth:
            pending = gather_start(layer + 1, w)
            pre_g_fwd = _hbm(pending[2])
        h, pa, pb, ya, kept, rstd_w = _inproj_gmlp_fwd(xs[-1], pre_g_fwd, w, ln_g3, ln_b3, wm, bsb, layer)
        if layer + 1 < depth:
            yb, pw, spw, y, x_next = _attn_outproj_fwd(pb, bias, sinks, ya, wo, xs[-1], post_g3, b_loc, layer)
            xs.append(x_next)
        else:
            yb, pw, spw, y, g, loss_part = _attn_outproj_fwd(pb, bias, sinks, ya, wo, xs[-1], post_g3, b_loc, layer,
                                                    target=loss_target.reshape(tokens, D_MODEL))
        saved.append((h, pa, pb, ya, yb, y, pw, spw, kept, rstd_w))
        weights.append((w, wo))

    grads = {name: [None] * depth for name in ("pre_norm_g","post_norm_g", "ln_v_g", "ln_v_b",
                                               "b_spatial", "sinks", "dbias")}
    zone_in = lax.empty((N_DEV, depth * in_shard, D_MODEL), BF16)
    zone_out = lax.empty((N_DEV, depth * out_shard, D_MODEL), BF16)
    zone_ws = lax.empty((N_DEV, depth * A_GROUPS * CHUNK, CHUNK), F32)
    started_in, started_out, own_in, own_out = [], [], [None] * depth, [None] * depth
    own_ws = [None] * depth
    dbias = jnp.zeros((B_HEADS, BLOCK, BLOCK), F32)
    for layer in reversed(range(depth)):
        h, pa, pb, ya, yb, y, pw, spw, kept, rstd_w = saved[layer]
        w, wo = weights[layer]
        dya, dyb, dwo, dpost = _outproj_bwd(g, y, ya, yb, wo, post_g3, layer)
        send_out = dwo.reshape(N_DEV, out_shard, D_MODEL)
        own_out[layer] = lax.dynamic_index_in_dim(send_out, me, 0, keepdims=False)
        sems, sends, (zone_out,), ln_g_bwd = _exchange_start(
            [send_out], [zone_out], [layer * out_shard], f"grads_send_out_{layer}", ln_v_g.reshape(depth, 1, A_WIDTH))
        started_out.append((sems, sends, [layer * out_shard]))
        dqz, dkv, dbias, dsink = _attn_bwd(pb, dyb, pw, spw, dbias, layer)
        da, dw, dws, dbs, dlg, dlb = _gmlp_bwd_dw(pa, kept, rstd_w, dya, h, dqz, dkv, _hbm(ln_g_bwd), ln_b3, wmt, layer)
        own_ws[layer] = dws.reshape(A_GROUPS * CHUNK, CHUNK)
        send_in = dw.reshape(N_DEV, in_shard, D_MODEL)
        own_in[layer] = lax.dynamic_index_in_dim(send_in, me, 0, keepdims=False)
        row_offs = [layer * A_GROUPS * CHUNK, layer * in_shard]
        sems, sends, (zone_ws, zone_in), pre_g_bwd = _exchange_start(
            [own_ws[layer], send_in], [zone_ws, zone_in], row_offs, f"grads_send_in_{layer}",
            pre_norm_g.reshape(depth, 1, D_MODEL))
        started_in.append((sems, sends, row_offs))
        g, dpre = _inproj_bwd_dx(da, dqz, dkv, w, xs[layer], g, _hbm(pre_g_bwd), layer)
        grads["b_spatial"][layer] = dbs[:, :A_GROUPS].T
        grads["ln_v_g"][layer] = dlg[0]
        grads["ln_v_b"][layer] = dlb[0]
        grads["sinks"][layer] = dsink[0, :B_HEADS]
        grads["pre_norm_g"][layer] = dpre[0]
        grads["post_norm_g"][layer] = dpost[0]
    grad_x = g.reshape(x.shape)
    col_bucket = jnp.asarray(np.broadcast_to(_bucket_table()[0:1, ::-1], (B_HEADS, BLOCK)))
    flip = jnp.asarray(np.eye(BLOCK, dtype=np.float32)[::-1], dtype=BF16)
    drel = _rel_bias_grad(dbias, col_bucket, flip)[:, :REL_BUCKETS].T

    (recv_out,) = _exchange_wait(started_out, [zone_out], g, "grads_wait_out")
    recv_out = lax.dynamic_update_index_in_dim(recv_out, jnp.concatenate(own_out, axis=0), me, 0)
    res_out = _reduce_adamw(recv_out, w_out.reshape(-1, D_MODEL), m_w_out.reshape(-1, D_MODEL),
                            v_w_out.reshape(-1, D_MODEL), "adamw_w_out")
    res_out = [r.reshape(w_out.shape) for r in res_out]

    no_state = jnp.zeros((1,), F32)
    small_w = dict(pre_norm_g=pre_norm_g, post_norm_g=post_norm_g, ln_v_g=ln_v_g,
                   ln_v_b=ln_v_b, b_spatial=b_spatial, sinks=sinks, rel_bias=rel_bias, loss=no_state)
    small_m = dict(pre_norm_g=m_pre_norm_g, post_norm_g=m_post_norm_g, ln_v_g=m_ln_v_g,
                   ln_v_b=m_ln_v_b, b_spatial=m_b_spatial, sinks=m_sinks, rel_bias=m_rel_bias, loss=no_state)
    small_v = dict(pre_norm_g=v_pre_norm_g, post_norm_g=v_post_norm_g, ln_v_g=v_ln_v_g,
                   ln_v_b=v_ln_v_b, b_spatial=v_b_spatial, sinks=v_sinks, rel_bias=v_rel_bias, loss=no_state)
    small_g = {name: jnp.stack(grads[name]) for name in _SMALL if name not in ("rel_bias", "loss")}
    small_g["rel_bias"] = drel
    small_g["loss"] = loss_part[0, :1]
    shapes = {name: small_w[name].shape for name in _SMALL}
    slots = _all_gather_direct(_pack_small(small_g), "small_grads_all_gather")
    res_small = _reduce_adamw(slots, _pack_small(small_w), _pack_small(small_m), _pack_small(small_v), "adamw_small")

    recv_ws, recv_in = _exchange_wait(started_in, [zone_ws, zone_in], res_small[0], "grads_wait_in")
    recv_ws = lax.dynamic_update_index_in_dim(recv_ws, jnp.concatenate(own_ws, axis=0), me, 0)
    res_ws = _reduce_adamw(recv_ws, w_spatial.reshape(-1, CHUNK), m_w_spatial.reshape(-1, CHUNK),
                           v_w_spatial.reshape(-1, CHUNK), "adamw_w_spatial")
    res_ws = [r.reshape(w_spatial.shape) for r in res_ws]
    recv_in = lax.dynamic_update_index_in_dim(recv_in, jnp.concatenate(own_in, axis=0), me, 0)
    res_in = _reduce_adamw(recv_in, w_in_t.reshape(-1, D_MODEL), m_w_in_t.reshape(-1, D_MODEL),
                           v_w_in_t.reshape(-1, D_MODEL), "adamw_w_in")
    res_in = [jnp.swapaxes(r.reshape(w_in_t.shape), 1, 2) for r in res_in]
    res_small = [_unpack_small(r, shapes) for r in res_small]

    order = ("pre_norm_g", "w_in", "ln_v_g", "ln_v_b", "w_spatial", "b_spatial", "sinks", "rel_bias", "w_out",
             "post_norm_g")
    outs = [res_small[0]["loss"][0], grad_x]
    for kind in range(4):
        for name in order:
            if name == "w_in":
                outs.append(res_in[kind])
            elif name == "w_out":
                outs.append(res_out[kind])
            elif name == "w_spatial":
                outs.append(res_ws[kind])
            else:
                outs.append(res_small[kind][name])
    return tuple(outs)
```

```python
import math

import numpy as np
import jax
import jax.numpy as jnp
from jax import lax
from jax.experimental import pallas as pl
from jax.experimental.pallas import tpu as pltpu

F32 = jnp.float32
BF16 = jnp.bfloat16

D_MODEL = 1024
A_WIDTH = 512
A_GROUPS = 8
CHUNK = 128
B_HEADS = 8
HEAD_DIM = 64
B_WIDTH = 512
KV_WIDTH = 128
BLOCK = 128
REL_BUCKETS = 32
REL_MAX_DIST = 128
NORM_EPS = 1e-6
PA_WIDTH = 3 * A_WIDTH
PB_WIDTH = 2 * B_WIDTH + 2 * KV_WIDTH
IN_WIDTH = PA_WIDTH + PB_WIDTH
K_OFF, V_OFF, BZ_OFF = B_WIDTH, B_WIDTH + KV_WIDTH, B_WIDTH + 2 * KV_WIDTH
KEPT_WIDTH = 5 * A_WIDTH
SCALE = HEAD_DIM ** -0.5
NEG = -1e30
N_DEV = 8
LANES = 128

ADAM_LR = 0.001
ADAM_B1 = 0.9
ADAM_B2 = 0.999
ADAM_EPS = 1e-08
ADAM_WD = 0.01
ADAM_STEP = 10

V7X_VMEM_BYTES = 64 * 1024 * 1024
VMEM_TEMP_BYTES = 12 * 1024 * 1024
MESH = pl.DeviceIdType.MESH


def _vmem_limit(block_bytes, scratch_bytes=0):
    need = 2 * sum(block_bytes) + scratch_bytes + VMEM_TEMP_BYTES
    return int(min(need, V7X_VMEM_BYTES - 8 * 1024 * 1024))


def _nbytes(shape, dtype):
    return int(np.prod(shape)) * jnp.dtype(dtype).itemsize


def _token_tile(tokens, cap=512):
    tile = min(cap, tokens // 2)
    assert tokens % tile == 0 and tile % CHUNK == 0, tokens
    return tile


def _row_tile(rows, cap=512):
    best = 8
    for t in range(8, cap + 1, 8):
        if rows % t == 0:
            best = t
    assert rows % best == 0, rows
    return best


def _mm(a, b):
    return lax.dot_general(a, b, (((1,), (0,)), ((), ())), preferred_element_type=F32)


def _mm_nt(a, b):
    return lax.dot_general(a, b, (((1,), (1,)), ((), ())), preferred_element_type=F32)


def _mm_tn(a, b):
    return lax.dot_general(a, b, (((0,), (0,)), ((), ())), preferred_element_type=F32)


_GELU_C = math.sqrt(2.0 / math.pi)


_GELU_A = _GELU_C * 0.044715


def _gelu_parts(x):
    x2 = x * x
    t = jnp.tanh(x * (_GELU_C + _GELU_A * x2))
    return x2, t, 0.5 + 0.5 * t


def _gelu(x):
    return x * _gelu_parts(x)[2]


def _gelu_and_grad(x):
    x2, t, half_plus = _gelu_parts(x)
    grad = half_plus + (0.5 * x) * (1.0 - t * t) * (_GELU_C + (3.0 * _GELU_A) * x2)
    return x * half_plus, grad


def _sigmoid(x):
    return 0.5 + 0.5 * jnp.tanh(0.5 * x)


def _bucket_table():
    q = np.arange(BLOCK)[:, None]
    k = np.arange(BLOCK)[None, :]
    dist = np.where(k <= q, q - k, q + BLOCK - k)
    max_exact = REL_BUCKETS // 2
    safe = np.maximum(dist, 1).astype(np.float32)
    large = max_exact + (np.log(safe / np.float32(max_exact)) / np.float32(math.log(REL_MAX_DIST / max_exact))
                         * np.float32(REL_BUCKETS - max_exact)).astype(np.int32)
    large = np.minimum(large, REL_BUCKETS - 1)
    assert dist.min() >= 0 and dist.max() < BLOCK
    return np.where(dist < max_exact, dist, large).astype(np.int32)


def _hbm(x):
    return pltpu.with_memory_space_constraint(x, pltpu.HBM)


def _slot(px, py, pc):
    return 4 * px + 2 * py + pc


def _all_gather(arrs, name):
    n = len(arrs)

    def body(*refs):
        ins, outs = refs[:n], refs[n:2 * n]
        send_sems, recv_sems, local_sems = refs[2 * n:]
        x, y, c = lax.axis_index("x"), lax.axis_index("y"), lax.axis_index("c")
        me, sibling = (x, y, c), (x, y, 1 - c)
        chips = [(1 - x, y), (x, 1 - y), (1 - x, 1 - y)]

        def copy(a, k, block, to, src=None):
            dst = outs[a].at[_slot(*block)]
            return pltpu.make_async_remote_copy(
                src_ref=dst if src is None else src, dst_ref=dst,
                send_sem=send_sems.at[7 * a + k], recv_sem=recv_sems.at[7 * a + k],
                device_id=to, device_id_type=MESH)

        mine = [pltpu.make_async_copy(ins[a], outs[a].at[_slot(*me)], local_sems.at[a]) for a in range(n)]
        for cp in mine:
            cp.start()
        first = []
        for a in range(n):
            first.append(copy(a, 0, me, sibling, src=ins[a]))
            first += [copy(a, 1 + j, me, (*chip, c), src=ins[a]) for j, chip in enumerate(chips)]
        for cp in first:
            cp.start()
        passed = []
        for j, chip in enumerate(chips):
            for a in range(n):
                copy(a, 1 + j, (*chip, c), me).wait_recv()
                fwd = copy(a, 4 + j, (*chip, c), sibling)
                fwd.start()
                passed.append(fwd)
        for a in range(n):
            copy(a, 0, sibling, me).wait_recv()
            for j, chip in enumerate(chips):
                copy(a, 4 + j, (*chip, 1 - c), me).wait_recv()
        for cp in first + passed:
            cp.wait_send()
        for cp in mine:
            cp.wait()

    any_spec = pl.BlockSpec(memory_space=pl.ANY)
    return pl.pallas_call(
        body, name=name,
        out_shape=[jax.ShapeDtypeStruct((N_DEV,) + a.shape, a.dtype) for a in arrs],
        in_specs=[any_spec] * n, out_specs=[any_spec] * n,
        scratch_shapes=[pltpu.SemaphoreType.DMA((7 * n,)), pltpu.SemaphoreType.DMA((7 * n,)),
                        pltpu.SemaphoreType.DMA((n,))],
    )(*arrs)


def _all_gather_direct(arr, name):
    def body(in_ref, out_ref, send_sems, recv_sems, local_sem):
        me, peers = _peers()
        mine = pltpu.make_async_copy(in_ref, out_ref.at[_slot(*me)], local_sem)
        mine.start()

        def copy(k, origin, to):
            return pltpu.make_async_remote_copy(
                src_ref=in_ref, dst_ref=out_ref.at[_slot(*origin)], send_sem=send_sems.at[k],
                recv_sem=recv_sems.at[k], device_id=to, device_id_type=MESH)

        sends = [copy(k, me, peer) for k, peer in enumerate(peers)]
        for cp in sends:
            cp.start()
        for k, peer in enumerate(peers):
            copy(k, peer, me).wait_recv()
        for cp in sends:
            cp.wait_send()
        mine.wait()

    any_spec = pl.BlockSpec(memory_space=pl.ANY)
    return pl.pallas_call(
        body, name=name,
        out_shape=jax.ShapeDtypeStruct((N_DEV,) + arr.shape, arr.dtype),
        in_specs=[any_spec], out_specs=any_spec,
        scratch_shapes=[pltpu.SemaphoreType.DMA((7,)), pltpu.SemaphoreType.DMA((7,)), pltpu.SemaphoreType.DMA],
    )(arr)


_HBM_SPEC = pl.BlockSpec(memory_space=pltpu.HBM)
_SEM_SPEC = pl.BlockSpec(memory_space=pltpu.SEMAPHORE)
_DATAFLOW = pltpu.SideEffectType.DATAFLOW_SIDE_EFFECTING


def _peers():
    x, y, c = lax.axis_index("x"), lax.axis_index("y"), lax.axis_index("c")
    peers = []
    for k in range(1, N_DEV):
        fx, fy, fc = (k >> 2) & 1, (k >> 1) & 1, k & 1
        peers.append((1 - x if fx else x, 1 - y if fy else y, 1 - c if fc else c))
    return (x, y, c), peers


def _exchange_copy(send_ref, land_ref, row_off, src_slot, dst_slot, sems, idx, peer):
    rows = send_ref.shape[-2]
    src = send_ref.at[src_slot] if len(send_ref.shape) == 3 else send_ref
    return pltpu.make_async_remote_copy(
        src_ref=src, dst_ref=land_ref.at[dst_slot, pl.ds(row_off, rows), :],
        send_sem=sems[0].at[idx], recv_sem=sems[1].at[idx], device_id=peer, device_id_type=MESH)


def _exchange_start(sends, lands, row_offs, name, carry, after=None):
    n = len(sends)

    def body(*refs):
        ins, zones, carry_ref = refs[:n], refs[n:2 * n], refs[2 * n]
        first_out = 2 * n + 1 + (after is not None)
        sems = refs[first_out:first_out + 2]
        carry_out = refs[-1]
        me, peers = _peers()
        for a in range(n):
            for k, peer in enumerate(peers):
                _exchange_copy(ins[a], zones[a], row_offs[a], _slot(*peer), _slot(*me), sems, 7 * a + k, peer).start()
        carry_out[...] = carry_ref[...]

    arrays = [_hbm(a) for a in list(sends) + list(lands)]
    vmem = pl.BlockSpec(memory_space=pltpu.VMEM)
    out = pl.pallas_call(
        body, name=name,
        out_shape=(pltpu.SemaphoreType.DMA((7 * n,)), pltpu.SemaphoreType.DMA((7 * n,)),
                   *[pltpu.HBM(a.shape, a.dtype) for a in arrays], jax.ShapeDtypeStruct(carry.shape, carry.dtype)),
        in_specs=[_HBM_SPEC] * (2 * n) + [vmem] + ([pl.BlockSpec(memory_space=pl.ANY)] if after is not None else []),
        out_specs=(_SEM_SPEC, _SEM_SPEC, *[_HBM_SPEC] * (2 * n), vmem),
        input_output_aliases={i: 2 + i for i in range(2 * n)},
        compiler_params=pltpu.CompilerParams(has_side_effects=_DATAFLOW),
    )(*arrays, carry, *([after] if after is not None else []))
    return (out[0], out[1]), list(out[2:2 + n]), list(out[2 + n:2 + 2 * n]), out[-1]


def _exchange_wait(started, lands, after, name):
    n = len(lands)
    flat_sends = [s for _, sends, _ in started for s in sends]
    flat_sems = [s for sems, _, _ in started for s in sems]
    ns = len(flat_sends)

    def body(*refs):
        ins, zones = refs[:ns], refs[ns:ns + n]
        sem_refs = refs[ns + n:ns + n + len(flat_sems)]
        me, peers = _peers()
        pos = 0
        for call, (_, sends, row_offs) in enumerate(started):
            sems = sem_refs[2 * call:2 * call + 2]
            for a in range(len(sends)):
                for k, peer in enumerate(peers):
                    cp = _exchange_copy(ins[pos + a], zones[a], row_offs[a], _slot(*peer), _slot(*peer), sems,
                                        7 * a + k, peer)
                    cp.wait_send()
                    cp.wait_recv()
            pos += len(sends)

    arrays = list(flat_sends) + list(lands)
    out = pl.pallas_call(
        body, name=name,
        out_shape=tuple(pltpu.HBM(a.shape, a.dtype) for a in arrays),
        in_specs=[_HBM_SPEC] * len(arrays) + [_SEM_SPEC] * len(flat_sems) + [pl.BlockSpec(memory_space=pl.ANY)],
        out_specs=tuple([_HBM_SPEC] * len(arrays)),
        input_output_aliases={i: i for i in range(len(arrays))},
        compiler_params=pltpu.CompilerParams(has_side_effects=_DATAFLOW),
    )(*arrays, *flat_sems, after)
    return list(out[ns:])


def _bias_table(rel_bias, bucket):
    def body(rel_ref, bucket_ref, out_ref):
        bk = bucket_ref[...]
        for h in range(B_HEADS):
            def pick(b, acc, h=h):
                return jnp.where(bk == b, rel_ref[b, h], acc)
            out_ref[h] = lax.fori_loop(0, REL_BUCKETS, pick, jnp.zeros((BLOCK, BLOCK), F32))

    return pl.pallas_call(
        body, name="bias_table",
        out_shape=jax.ShapeDtypeStruct((B_HEADS, BLOCK, BLOCK), F32),
        in_specs=[pl.BlockSpec(memory_space=pltpu.SMEM), pl.BlockSpec(memory_space=pltpu.VMEM)],
        out_specs=pl.BlockSpec(memory_space=pltpu.VMEM),
    )(rel_bias, bucket)


def _inproj_gmlp_fwd(x2, pre_g3, w, ln_g3, ln_b3, wm, bsb, layer):
    tokens = x2.shape[0]
    tm = _token_tile(tokens)

    def body(x_ref, g_ref, w_ref, lg_ref, lb_ref, wm_ref, bsb_ref, h_ref, pa_ref, pb_ref, ya_ref, kept_ref, rstd_ref):
        x = x_ref[...]
        r = lax.rsqrt(jnp.mean(x * x, axis=-1, keepdims=True) + NORM_EPS)
        h = (x * r * g_ref[...]).astype(BF16)
        h_ref[...] = h
        pa_ref[...] = _mm_nt(h, w_ref[0:PA_WIDTH, :]).astype(BF16)
        pb_ref[...] = _mm_nt(h, w_ref[PA_WIDTH:IN_WIDTH, :]).astype(BF16)
        for ci in range(tm // CHUNK):
            rows = slice(ci * CHUNK, (ci + 1) * CHUNK)
            gu, gv, pz, u, vhat, rstd, _, mixed, sg = _gmlp_forward_chunk(
                pa_ref[rows, :], lg_ref[...], lb_ref[...], wm_ref, bsb_ref[...])
            ya_ref[rows, :] = (u * mixed * (pz * sg)).astype(BF16)
            for k, val in enumerate((u, gu, gv, vhat, mixed)):
                kept_ref[rows, A_WIDTH * k:A_WIDTH * (k + 1)] = val.astype(BF16)
            rstd_ref[rows, :] = jnp.broadcast_to(rstd, (CHUNK, LANES))

    blocks = [_nbytes((tm, D_MODEL), F32), _nbytes((D_MODEL, IN_WIDTH), BF16),
              _nbytes((tm, D_MODEL), BF16), _nbytes((tm, PA_WIDTH), BF16), _nbytes((tm, PB_WIDTH), BF16),
              _nbytes((A_GROUPS, CHUNK, CHUNK), BF16), _nbytes((CHUNK, A_WIDTH), F32), _nbytes((tm, A_WIDTH), BF16)]
    return pl.pallas_call(
        body, name=f"inproj_gmlp_fwd_{layer}", grid=(tokens // tm,),
        in_specs=[pl.BlockSpec((tm, D_MODEL), lambda i: (i, 0)),
                  pl.BlockSpec((None, 1, D_MODEL), lambda i: (layer, 0, 0)),
                  pl.BlockSpec((IN_WIDTH, D_MODEL), lambda i: (0, 0)),
                  pl.BlockSpec((None, 1, A_WIDTH), lambda i: (layer, 0, 0)),
                  pl.BlockSpec((None, 1, A_WIDTH), lambda i: (layer, 0, 0)),
                  pl.BlockSpec((None, A_GROUPS, CHUNK, CHUNK), lambda i: (layer, 0, 0, 0)),
                  pl.BlockSpec((None, CHUNK, A_WIDTH), lambda i: (layer, 0, 0))],
        out_specs=[pl.BlockSpec((tm, D_MODEL), lambda i: (i, 0)),
                   pl.BlockSpec((tm, PA_WIDTH), lambda i: (i, 0)),
                   pl.BlockSpec((tm, PB_WIDTH), lambda i: (i, 0)),
                   pl.BlockSpec((tm, A_WIDTH), lambda i: (i, 0)),
                   pl.BlockSpec((tm, KEPT_WIDTH), lambda i: (i, 0)),
                   pl.BlockSpec((tm, LANES), lambda i: (i, 0))],
        out_shape=[pltpu.HBM((tokens, D_MODEL), BF16),
                   pltpu.HBM((tokens, PA_WIDTH), BF16),
                   pltpu.HBM((tokens, PB_WIDTH), BF16),
                   pltpu.HBM((tokens, A_WIDTH), BF16),
                   pltpu.HBM((tokens, KEPT_WIDTH), BF16),
                   pltpu.HBM((tokens, LANES), F32)],
        compiler_params=pltpu.CompilerParams(dimension_semantics=("parallel",),
                                             vmem_limit_bytes=_vmem_limit(blocks + [_nbytes((tm, KEPT_WIDTH), BF16),
                                                                                    _nbytes((tm, LANES), F32)])),
    )(_hbm(x2), pre_g3, _hbm(w), ln_g3, ln_b3, wm, bsb)


def _gmlp_forward_chunk(pa, ln_g, ln_b, wm_ref, bsb):
    pu = pa[:, 0:A_WIDTH].astype(F32)
    pv = pa[:, A_WIDTH:2 * A_WIDTH].astype(F32)
    pz = pa[:, 2 * A_WIDTH:3 * A_WIDTH].astype(F32)
    u, gu = _gelu_and_grad(pu)
    vv, gv = _gelu_and_grad(pv)
    mu = jnp.mean(vv, axis=-1, keepdims=True)
    xc = vv - mu
    rstd = lax.rsqrt(jnp.mean(xc * xc, axis=-1, keepdims=True) + NORM_EPS)
    vhat = xc * rstd
    vnb = (vhat * ln_g + ln_b).astype(BF16)
    low = lax.broadcasted_iota(jnp.int32, (CHUNK, LANES), 1) < HEAD_DIM
    parts = []
    for p in range(A_GROUPS // 2):
        vp = vnb[:, LANES * p:LANES * (p + 1)]
        parts.append(jnp.where(low, _mm(wm_ref[2 * p], vp), _mm(wm_ref[2 * p + 1], vp)))
    mixed = jnp.concatenate(parts, axis=1) + bsb
    sg = _sigmoid(pz)
    return gu, gv, pz, u, vhat, rstd, vnb, mixed, sg


def _kv_variants(kv):
    t = kv.astype(F32)
    rolled = pltpu.roll(t, HEAD_DIM, 1)
    low = lax.broadcasted_iota(jnp.int32, t.shape, 1) < HEAD_DIM
    zero = jnp.zeros_like(t)
    head0 = (jnp.where(low, t, zero).astype(BF16), jnp.where(low, zero, rolled).astype(BF16))
    head1 = (jnp.where(low, rolled, zero).astype(BF16), jnp.where(low, zero, t).astype(BF16))
    return head0, head1


def _band_masks():
    row = lax.broadcasted_iota(jnp.int32, (BLOCK, BLOCK), 0)
    col = lax.broadcasted_iota(jnp.int32, (BLOCK, BLOCK), 1)
    return col <= row


def _wrap(full, tri):
    return jnp.where(tri, full[:, BLOCK:2 * BLOCK], full[:, 0:BLOCK])


def _attn_probs(sf, bias_h, sink, tri, kill):
    s = _wrap(sf, tri) + bias_h
    s = jnp.where(kill, NEG, s)
    m = jnp.maximum(jnp.max(s, axis=-1, keepdims=True), sink)
    e = jnp.exp(s - m)
    es = jnp.exp(sink - m)
    inv = 1.0 / (jnp.sum(e, axis=-1, keepdims=True) + es)
    return e * inv, es * inv


def _unwrap16(p, tri):
    p = p.astype(BF16)
    zero = jnp.zeros_like(p)
    return jnp.concatenate([jnp.where(tri, zero, p), jnp.where(tri, p, zero)], axis=1)


def _fill_kv(kv_ref, prev_ref, cur_ref):
    kv_ref[0:BLOCK, :] = prev_ref[...]
    kv_ref[BLOCK:, :] = cur_ref[:, K_OFF:K_OFF + 2 * KV_WIDTH]


def _attn_outproj_fwd(pb, bias, sinks, ya, wo, x2, post_g3, b_loc, layer, target=None):
    tokens = pb.shape[0]
    nb = tokens // b_loc // BLOCK
    tq = _token_tile(tokens)
    per_tile = tq // BLOCK
    steps = tokens // tq
    with_loss = target is not None
    half = D_MODEL // 2

    def body(*refs):
        sink_ref, cur_ref, prev_ref, bias_ref, ya_ref, woa_ref, wob_ref, x_ref, g_ref = refs[:9]
        if with_loss:
            t_ref, yb_ref, pw_ref, spw_ref, y_ref, grad_ref, loss_ref, kv_ref, acc_ref = refs[9:]
        else:
            yb_ref, pw_ref, spw_ref, y_ref, xn_ref, kv_ref = refs[9:]
        t = pl.program_id(0)
        _fill_kv(kv_ref, prev_ref, cur_ref)
        tri = _band_masks()
        head_lane = lax.broadcasted_iota(jnp.int32, (BLOCK, LANES), 1)

        def block(i, carry):
            start = i * BLOCK
            rows = pl.ds(start, BLOCK)
            first = lax.rem(t * per_tile + i, nb) == 0
            kill = jnp.logical_and(first, jnp.logical_not(tri))
            kv = kv_ref[pl.ds(start, 2 * BLOCK), :]
            k_ops = _kv_variants(kv[:, 0:KV_WIDTH])
            v_ops = _kv_variants(kv[:, KV_WIDTH:2 * KV_WIDTH])
            p16 = {}
            sink_cols = jnp.zeros((BLOCK, LANES), F32)
            for kvh in range(2):
                pairs = (2 * kvh, 2 * kvh + 1)
                qs = jnp.concatenate([cur_ref[rows, LANES * p:LANES * (p + 1)] for p in pairs], axis=0) * SCALE
                for j in range(2):
                    sf = _mm_nt(qs, k_ops[kvh][j])
                    for r, p in enumerate(pairs):
                        hd = 2 * p + j
                        probs, sink_p = _attn_probs(sf[BLOCK * r:BLOCK * (r + 1)], bias_ref[hd],
                                                    sink_ref[layer, hd], tri, kill)
                        probs = probs.astype(BF16)
                        p16[hd] = _unwrap16(probs, tri)
                        pw_ref[rows, LANES * hd:LANES * (hd + 1)] = probs
                        sink_cols = jnp.where(head_lane == hd, sink_p, sink_cols)
            spw_ref[rows, :] = sink_cols
            for kvh in range(2):
                pairs = (2 * kvh, 2 * kvh + 1)
                out = jnp.zeros((2 * BLOCK, LANES), F32)
                for j in range(2):
                    out = out + _mm(jnp.concatenate([p16[2 * p + j] for p in pairs], axis=0), v_ops[kvh][j])
                for r, p in enumerate(pairs):
                    bz = cur_ref[rows, BZ_OFF + LANES * p:BZ_OFF + LANES * (p + 1)].astype(F32)
                    yb_ref[rows, LANES * p:LANES * (p + 1)] = (
                        out[BLOCK * r:BLOCK * (r + 1)] * (bz * _sigmoid(bz))).astype(BF16)
            y = _mm(ya_ref[rows, :], woa_ref[...]) + _mm(yb_ref[rows, :], wob_ref[...])
            r = lax.rsqrt(jnp.mean(y * y, axis=-1, keepdims=True) + NORM_EPS)
            x_next = x_ref[rows, :] + y * r * g_ref[...]
            y_ref[rows, :] = y.astype(BF16)
            if not with_loss:
                xn_ref[rows, :] = x_next
            else:
                err = x_next - t_ref[rows, :]
                grad_ref[rows, :] = err * (1.0 / D_MODEL)
                acc_ref[...] += jnp.sum(err * err, axis=0, keepdims=True)
            return carry

        if with_loss:
            @pl.when(t == 0)
            def _():
                acc_ref[...] = jnp.zeros_like(acc_ref)

        for i in range(per_tile):
            block(i, 0)

        if not with_loss:
            return

        @pl.when(t == steps - 1)
        def _():
            total = jnp.sum(acc_ref[...], axis=-1, keepdims=True) * (0.5 / D_MODEL)
            loss_ref[...] = jnp.broadcast_to(total, loss_ref.shape)

    tile = pl.BlockSpec((tq, D_MODEL), lambda t: (t, 0))
    blocks = [_nbytes((tq, PB_WIDTH), BF16), _nbytes((BLOCK, 2 * KV_WIDTH), BF16),
              _nbytes((B_HEADS, BLOCK, BLOCK), F32), 2 * _nbytes((tq, B_WIDTH), BF16),
              2 * _nbytes((half, D_MODEL), BF16), 3 * _nbytes((tq, D_MODEL), F32), _nbytes((tq, D_MODEL), BF16),
              _nbytes((tq, B_HEADS * BLOCK), BF16), _nbytes((tq, LANES), F32)]
    scratch = _nbytes((tq + BLOCK, 2 * KV_WIDTH), BF16)
    in_specs = [pl.BlockSpec(memory_space=pltpu.SMEM),
                pl.BlockSpec((tq, PB_WIDTH), lambda t: (t, 0)),
                pl.BlockSpec((BLOCK, 2 * KV_WIDTH),
                             lambda t: (jnp.maximum(t * per_tile - 1, 0), K_OFF // (2 * KV_WIDTH))),
                pl.BlockSpec((B_HEADS, BLOCK, BLOCK), lambda t: (0, 0, 0)),
                pl.BlockSpec((tq, half), lambda t: (t, 0)),
                pl.BlockSpec((half, D_MODEL), lambda t: (0, 0)),
                pl.BlockSpec((half, D_MODEL), lambda t: (1, 0)),
                tile,
                pl.BlockSpec((None, 1, D_MODEL), lambda t: (layer, 0, 0))]
    out_specs = [pl.BlockSpec((tq, B_WIDTH), lambda t: (t, 0)), pl.BlockSpec((tq, B_HEADS * BLOCK), lambda t: (t, 0)),
                 pl.BlockSpec((tq, LANES), lambda t: (t, 0)), tile, tile]
    out_shape = [pltpu.HBM((tokens, B_WIDTH), BF16), pltpu.HBM((tokens, B_HEADS * BLOCK), BF16),
                 pltpu.HBM((tokens, LANES), F32), pltpu.HBM((tokens, D_MODEL), BF16),
                 pltpu.HBM((tokens, D_MODEL), F32)]
    scratch_shapes = [pltpu.VMEM((tq + BLOCK, 2 * KV_WIDTH), BF16)]
    operands = [sinks, _hbm(pb), _hbm(pb), _hbm(bias), _hbm(ya), _hbm(wo), _hbm(wo), _hbm(x2), post_g3]
    if with_loss:
        in_specs.append(tile)
        out_specs.append(pl.BlockSpec((1, LANES), lambda t: (0, 0)))
        out_shape.append(pltpu.HBM((1, LANES), F32))
        scratch_shapes.append(pltpu.VMEM((1, D_MODEL), F32))
        operands.append(_hbm(target))
    return pl.pallas_call(
        body, name=f"attn_outproj_fwd_{layer}", grid=(steps,),
        in_specs=in_specs, out_specs=out_specs, out_shape=out_shape, scratch_shapes=scratch_shapes,
        compiler_params=pltpu.CompilerParams(dimension_semantics=("arbitrary" if with_loss else "parallel",),
                                             vmem_limit_bytes=_vmem_limit(blocks, scratch)),
    )(*operands)


def _outproj_bwd(g, y, ya, yb, wo, post_g3, layer):
    tokens = g.shape[0]
    tm = _token_tile(tokens, cap=1024)
    half = D_MODEL // 2
    steps = tokens // tm

    def body(g_ref, y_ref, ya_ref, yb_ref, woa_ref, wob_ref, pg_ref, dya_ref, dyb_ref, dwo16_ref, dpg_ref, dwo_ref):
        @pl.when(pl.program_id(0) == 0)
        def _():
            dwo_ref[...] = jnp.zeros_like(dwo_ref)
            dpg_ref[...] = jnp.zeros_like(dpg_ref)

        gv = g_ref[...]
        yf = y_ref[...].astype(F32)
        r = lax.rsqrt(jnp.mean(yf * yf, axis=-1, keepdims=True) + NORM_EPS)
        yhat = yf * r
        gg = gv * pg_ref[...]
        dy = r * (gg - yhat * jnp.mean(gg * yhat, axis=-1, keepdims=True))
        dpg_ref[...] += jnp.sum(gv * yhat, axis=0, keepdims=True)
        dyb16 = dy.astype(BF16)
        dya_ref[...] = _mm_nt(dyb16, woa_ref[...]).astype(BF16)
        dyb_ref[...] = _mm_nt(dyb16, wob_ref[...]).astype(BF16)
        dwo_ref[0:half, :] += _mm_tn(ya_ref[...], dyb16)
        dwo_ref[half:D_MODEL, :] += _mm_tn(yb_ref[...], dyb16)

        @pl.when(pl.program_id(0) == steps - 1)
        def _():
            dwo16_ref[...] = dwo_ref[...].astype(BF16)

    blocks = [_nbytes((tm, D_MODEL), F32), _nbytes((tm, D_MODEL), BF16), 4 * _nbytes((tm, half), BF16),
              2 * _nbytes((half, D_MODEL), BF16), _nbytes((D_MODEL, D_MODEL), BF16)]
    return pl.pallas_call(
        body, name=f"outproj_bwd_{layer}", grid=(tokens // tm,),
        in_specs=[pl.BlockSpec((tm, D_MODEL), lambda i: (i, 0)),
                  pl.BlockSpec((tm, D_MODEL), lambda i: (i, 0)),
                  pl.BlockSpec((tm, half), lambda i: (i, 0)),
                  pl.BlockSpec((tm, half), lambda i: (i, 0)),
                  pl.BlockSpec((half, D_MODEL), lambda i: (0, 0)),
                  pl.BlockSpec((half, D_MODEL), lambda i: (1, 0)),
                  pl.BlockSpec((None, 1, D_MODEL), lambda i: (layer, 0, 0))],
        out_specs=[pl.BlockSpec((tm, half), lambda i: (i, 0)),
                   pl.BlockSpec((tm, half), lambda i: (i, 0)),
                   pl.BlockSpec((D_MODEL, D_MODEL), lambda i: (0, 0)),
                   pl.BlockSpec((1, D_MODEL), lambda i: (0, 0))],
        out_shape=[pltpu.HBM((tokens,half), BF16),
                   pltpu.HBM((tokens,half), BF16),
                   pltpu.HBM((D_MODEL,D_MODEL), BF16),
                   pltpu.HBM((1,D_MODEL), F32)],
        scratch_shapes=[pltpu.VMEM((D_MODEL, D_MODEL), F32)],
        compiler_params=pltpu.CompilerParams(dimension_semantics=("arbitrary",),
                                             vmem_limit_bytes=_vmem_limit(blocks, _nbytes((D_MODEL, D_MODEL), F32))),
    )(_hbm(g), _hbm(y), _hbm(ya), _hbm(yb), _hbm(wo), _hbm(wo), post_g3)


def _gmlp_bwd_dw(pa, kept, rstd_w, dya, h, dqz, dkv, ln_g3, ln_b3, wmt, layer):
    tokens = pa.shape[0]
    tc = _token_tile(tokens)
    steps = tokens // tc

    def body(pz_ref, kept_ref, rstd_ref, dya_ref, h_ref, dqz_ref, dkv_ref, lg_ref, lb_ref, wmt_ref,
             da_ref, dw_ref, dws_ref, dbs_ref, dlg_ref, dlb_ref, acc_ref, dbsb_ref):
        i = pl.program_id(0)

        @pl.when(i == 0)
        def _():
            acc_ref[...] = jnp.zeros_like(acc_ref)
            dws_ref[...] = jnp.zeros_like(dws_ref)
            dlg_ref[...] = jnp.zeros_like(dlg_ref)
            dlb_ref[...] = jnp.zeros_like(dlb_ref)
            dbsb_ref[...] = jnp.zeros_like(dbsb_ref)

        ln_g = lg_ref[...]
        low = lax.broadcasted_iota(jnp.int32, (CHUNK, LANES), 1) < HEAD_DIM
        hv = h_ref[...]
        acc_ref[:, PA_WIDTH:PA_WIDTH + B_WIDTH] += _mm_tn(hv, dqz_ref[:, 0:B_WIDTH])
        acc_ref[:, PA_WIDTH + K_OFF:PA_WIDTH + BZ_OFF] += _mm_tn(hv, dkv_ref[...])
        acc_ref[:, PA_WIDTH + BZ_OFF:IN_WIDTH] += _mm_tn(hv, dqz_ref[:, B_WIDTH:2 * B_WIDTH])

        for ci in range(tc // CHUNK):
            rows = slice(ci * CHUNK, (ci + 1) * CHUNK)
            u, gu, gv, vhat, mixed = (kept_ref[rows, A_WIDTH * k:A_WIDTH * (k + 1)].astype(F32) for k in range(5))
            rstd = jnp.concatenate([rstd_ref[rows, :]] * (A_WIDTH // LANES), axis=1)
            vnb = (vhat * ln_g + lb_ref[...]).astype(BF16)
            pz = pz_ref[rows, :].astype(F32)
            sg = _sigmoid(pz)
            dy = dya_ref[rows, :].astype(F32)
            sz = pz * sg
            dy_sz = dy * sz
            du = dy_sz * mixed
            dmixed = dy_sz * u
            dz = dy * (u * mixed) * (sg + sz * (1.0 - sg))
            dbsb_ref[...] += dmixed
            dmb = dmixed.astype(BF16)
            zero = jnp.zeros((CHUNK, LANES), BF16)
            parts = []
            for p in range(A_GROUPS // 2):
                dmp = dmb[:, LANES * p:LANES * (p + 1)]
                vp = vnb[:, LANES * p:LANES * (p + 1)]
                parts.append(jnp.where(low, _mm(wmt_ref[2 * p], dmp), _mm(wmt_ref[2 * p + 1], dmp)))
                dws_ref[2 * p] += _mm_nt(jnp.where(low, dmp, zero), vp)
                dws_ref[2 * p + 1] += _mm_nt(jnp.where(low, zero, dmp), vp)
            dvn = jnp.concatenate(parts, axis=1)
            dlg_ref[...] += jnp.sum(dvn * vhat, axis=0, keepdims=True)
            dlb_ref[...] += jnp.sum(dvn, axis=0, keepdims=True)
            dvh = dvn * ln_g
            dvv = rstd * (dvh - jnp.mean(dvh, axis=-1, keepdims=True)
                          - vhat * jnp.mean(dvh * vhat, axis=-1, keepdims=True))
            da_ref[rows, 0:A_WIDTH] = (du * gu).astype(BF16)
            da_ref[rows, A_WIDTH:2 * A_WIDTH] = (dvv * gv).astype(BF16)
            da_ref[rows, 2 * A_WIDTH:3 * A_WIDTH] = dz.astype(BF16)

        acc_ref[:, 0:PA_WIDTH] += _mm_tn(hv, da_ref[...])

        @pl.when(i == steps - 1)
        def _():
            for c in range(IN_WIDTH // LANES):
                cols = slice(LANES * c, LANES * (c + 1))
                dw_ref[cols, :] = acc_ref[:, cols].T.astype(BF16)
            causal = (lax.broadcasted_iota(jnp.int32, (CHUNK, CHUNK), 0)
                      >= lax.broadcasted_iota(jnp.int32, (CHUNK, CHUNK), 1))
            for h in range(A_GROUPS):
                dws_ref[h] = jnp.where(causal, dws_ref[h], 0.0)
            acc = dbsb_ref[...]
            lane_full = lax.broadcasted_iota(jnp.int32, (CHUNK, A_WIDTH), 1)
            lane_out = lax.broadcasted_iota(jnp.int32, (CHUNK, LANES), 1)
            out = jnp.zeros((CHUNK, LANES), F32)
            for h in range(A_GROUPS):
                in_group = jnp.logical_and(lane_full >= HEAD_DIM * h, lane_full < HEAD_DIM * (h + 1))
                s = jnp.sum(jnp.where(in_group, acc, 0.0), axis=-1, keepdims=True)
                out = jnp.where(lane_out == h, s, out)
            dbs_ref[...] = out

    blocks = [_nbytes((tc, KEPT_WIDTH), BF16), 2 * _nbytes((tc, A_WIDTH), BF16), _nbytes((A_GROUPS, CHUNK, CHUNK), BF16),
              _nbytes((tc, LANES), F32), _nbytes((tc, PA_WIDTH), BF16), _nbytes((A_GROUPS, CHUNK, CHUNK), F32),
              _nbytes((CHUNK, LANES), F32), _nbytes((tc, D_MODEL), BF16), _nbytes((tc, 2 * B_WIDTH), BF16),
              _nbytes((tc, 2 * KV_WIDTH), BF16), _nbytes((D_MODEL, IN_WIDTH), BF16)]
    scratch = _nbytes((D_MODEL, IN_WIDTH), F32) + _nbytes((CHUNK, A_WIDTH), F32)
    in_specs = [pl.BlockSpec((tc, A_WIDTH), lambda i: (i, 2)),
                pl.BlockSpec((tc, KEPT_WIDTH), lambda i: (i, 0)),
                pl.BlockSpec((tc, LANES), lambda i: (i, 0)),
                pl.BlockSpec((tc, A_WIDTH), lambda i: (i, 0)),
                pl.BlockSpec((tc, D_MODEL), lambda i: (i, 0)),
                pl.BlockSpec((tc, 2 * B_WIDTH), lambda i: (i, 0)),
                pl.BlockSpec((pl.Element(tc), pl.Element(2 * KV_WIDTH)),
                             lambda i: (pl.multiple_of(i * tc + BLOCK, BLOCK), 0)),
                pl.BlockSpec((None, 1, A_WIDTH), lambda i: (layer, 0, 0)),
                pl.BlockSpec((None, 1, A_WIDTH), lambda i: (layer, 0, 0)),
                pl.BlockSpec((None, A_GROUPS, CHUNK, CHUNK), lambda i: (layer, 0, 0, 0))]
    operands = [_hbm(pa), _hbm(kept), _hbm(rstd_w), _hbm(dya), _hbm(h), _hbm(dqz), _hbm(dkv), ln_g3, ln_b3, wmt]
    out_specs = [pl.BlockSpec((tc, PA_WIDTH), lambda i: (i, 0)),
                 pl.BlockSpec((IN_WIDTH, D_MODEL), lambda i: (0, 0)),
                 pl.BlockSpec((A_GROUPS, CHUNK, CHUNK), lambda i: (0, 0, 0)),
                 pl.BlockSpec((CHUNK, LANES), lambda i: (0, 0)),
                 pl.BlockSpec((1, A_WIDTH), lambda i: (0, 0)),
                 pl.BlockSpec((1, A_WIDTH), lambda i: (0, 0))]
    out_shape = [pltpu.HBM((tokens, PA_WIDTH), BF16),
                 pltpu.HBM((IN_WIDTH, D_MODEL), BF16),
                 pltpu.HBM((A_GROUPS, CHUNK, CHUNK), F32),
                 pltpu.HBM((CHUNK, LANES), F32),
                 pltpu.HBM((1, A_WIDTH), F32),
                 pltpu.HBM((1, A_WIDTH), F32)]
    scratch_shapes = [pltpu.VMEM((D_MODEL, IN_WIDTH), F32), pltpu.VMEM((CHUNK, A_WIDTH), F32)]
    return pl.pallas_call(
        body, name=f"gmlp_bwd_dw_{layer}", grid=(steps,),
        in_specs=in_specs, out_specs=out_specs, out_shape=out_shape, scratch_shapes=scratch_shapes,
        compiler_params=pltpu.CompilerParams(dimension_semantics=("arbitrary",),
                                             vmem_limit_bytes=_vmem_limit(blocks, scratch)),
    )(*operands)


def _attn_bwd(pb, dyb, probs_w, sink_w, dbias_in, layer):
    tokens = pb.shape[0]
    qz_width = 2 * B_WIDTH
    tq = _token_tile(tokens)
    per_tile = tq // BLOCK
    nt = tokens // tq

    def body(cur_ref, prev_ref, dyb_ref, pw_ref, spw_ref, dbias_in_ref, dqz_ref, dkv_ref, dbias_ref, dsink_ref,
             kv_ref, acc_ref):
        t = pl.program_id(0)

        @pl.when(t == 0)
        def _():
            dbias_ref[...] = dbias_in_ref[...]
            dsink_ref[...] = jnp.zeros_like(dsink_ref)
            acc_ref[0:BLOCK, :] = jnp.zeros((BLOCK, 2 * KV_WIDTH), F32)

        @pl.when(t < nt)
        def _():
            acc_ref[BLOCK:, :] = jnp.zeros((tq, 2 * KV_WIDTH), F32)
            _fill_kv(kv_ref, prev_ref, cur_ref)
            tri = _band_masks()
            low = lax.broadcasted_iota(jnp.int32, (BLOCK, LANES), 1) < HEAD_DIM
            low_kv = lax.broadcasted_iota(jnp.int32, (2 * BLOCK, LANES), 1) < HEAD_DIM

            head_lane = lax.broadcasted_iota(jnp.int32, (BLOCK, LANES), 1)

            def block(i, carry):
                start = i * BLOCK
                rows = pl.ds(start, BLOCK)
                kv = kv_ref[pl.ds(start, 2 * BLOCK), :]
                k_ops = _kv_variants(kv[:, 0:KV_WIDTH])
                v_ops = _kv_variants(kv[:, KV_WIDTH:2 * KV_WIDTH])
                probs, p16, qs, dos, delta, ds16 = {}, {}, {}, {}, {}, {}
                for hd in range(B_HEADS):
                    kept_p = pw_ref[rows, LANES * hd:LANES * (hd + 1)]
                    probs[hd] = kept_p.astype(F32)
                    p16[hd] = _unwrap16(kept_p, tri)
                for kvh in range(2):
                    qs[kvh] = jnp.concatenate(
                        [cur_ref[rows, LANES * p:LANES * (p + 1)] for p in (2 * kvh, 2 * kvh + 1)], axis=0)
                for kvh in range(2):
                    pairs = (2 * kvh, 2 * kvh + 1)
                    out = jnp.zeros((2 * BLOCK, LANES), F32)
                    for j in range(2):
                        out = out + _mm(jnp.concatenate([p16[2 * p + j] for p in pairs], axis=0), v_ops[kvh][j])
                    d_outs = []
                    for r, p in enumerate(pairs):
                        bz = cur_ref[rows, BZ_OFF + LANES * p:BZ_OFF + LANES * (p + 1)].astype(F32)
                        sg = _sigmoid(bz)
                        dyp = dyb_ref[rows, LANES * p:LANES * (p + 1)].astype(F32)
                        out_p = out[BLOCK * r:BLOCK * (r + 1)]
                        d_out = dyp * (bz * sg)
                        dqz_ref[rows, B_WIDTH + LANES * p:B_WIDTH + LANES * (p + 1)] = (
                            dyp * out_p * (sg * (1.0 + bz * (1.0 - sg)))).astype(BF16)
                        dod = d_out * out_p
                        delta[2 * p] = jnp.sum(jnp.where(low, dod, 0.0), axis=-1, keepdims=True)
                        delta[2 * p + 1] = jnp.sum(jnp.where(low, 0.0, dod), axis=-1, keepdims=True)
                        d_outs.append(d_out.astype(BF16))
                    dos[kvh] = jnp.concatenate(d_outs, axis=0)
                delta_cols = jnp.zeros((BLOCK, LANES), F32)
                for hd in range(B_HEADS):
                    delta_cols = jnp.where(head_lane == hd, delta[hd], delta_cols)
                dsink_ref[0:1, :] += jnp.sum(-(spw_ref[rows, :] * delta_cols), axis=0, keepdims=True)
                for kvh in range(2):
                    pairs = (2 * kvh, 2 * kvh + 1)
                    for j in range(2):
                        dpf = _mm_nt(dos[kvh], v_ops[kvh][j])
                        for r, p in enumerate(pairs):
                            hd = 2 * p + j
                            ds = probs[hd] * (_wrap(dpf[BLOCK * r:BLOCK * (r + 1)], tri) - delta[hd])
                            dbias_ref[hd] += ds
                            ds16[hd] = _unwrap16(ds, tri)
                dk_acc = [[None, None], [None, None]]
                dv_acc = [[None, None], [None, None]]
                for kvh in range(2):
                    pairs = (2 * kvh, 2 * kvh + 1)
                    dq = jnp.zeros((2 * BLOCK, LANES), F32)
                    for j in range(2):
                        dss = jnp.concatenate([ds16[2 * p + j] for p in pairs], axis=0)
                        pss = jnp.concatenate([p16[2 * p + j] for p in pairs], axis=0)
                        dq = dq + _mm(dss, k_ops[kvh][j])
                        dk_acc[kvh][j] = _mm_tn(dss, qs[kvh])
                        dv_acc[kvh][j] = _mm_tn(pss, dos[kvh])
                    for r, p in enumerate(pairs):
                        dqz_ref[rows, LANES * p:LANES * (p + 1)] = (dq[BLOCK * r:BLOCK * (r + 1)] * SCALE).astype(BF16)

                def fold(acc):
                    return jnp.where(low_kv,
                                     acc[0][0] + pltpu.roll(acc[0][1], HEAD_DIM, 1),
                                     pltpu.roll(acc[1][0], HEAD_DIM, 1) + acc[1][1])

                acc_ref[pl.ds(start, 2 * BLOCK), :] += jnp.concatenate(
                    [fold(dk_acc) * SCALE, fold(dv_acc)], axis=1)
                return carry

            for i in range(per_tile):
                block(i, 0)
            dkv_ref[...] = acc_ref[0:tq, :].astype(BF16)
            acc_ref[0:BLOCK, :] = acc_ref[tq:tq + BLOCK, :]

        @pl.when(t == nt)
        def _():
            dkv_ref[0:BLOCK, :] = acc_ref[0:BLOCK, :].astype(BF16)
            dkv_ref[BLOCK:, :] = jnp.zeros((tq - BLOCK, 2 * KV_WIDTH), BF16)

    def cur_map(t):
        return (jnp.minimum(t, nt - 1), 0)

    def prev_map(t):
        return (jnp.maximum(jnp.minimum(t, nt - 1) * per_tile - 1, 0), K_OFF // (2 * KV_WIDTH))

    blocks = [_nbytes((tq, PB_WIDTH), BF16), _nbytes((BLOCK, 2 * KV_WIDTH), BF16), _nbytes((tq, B_WIDTH), BF16),
              2 * _nbytes((B_HEADS, BLOCK, BLOCK), F32), _nbytes((tq, qz_width), BF16),
              _nbytes((tq, 2 * KV_WIDTH), BF16), _nbytes((B_HEADS, LANES), F32),
              _nbytes((tq, B_HEADS * BLOCK), BF16), _nbytes((tq, LANES), F32)]
    scratch = _nbytes((tq + BLOCK, 2 * KV_WIDTH), BF16) + _nbytes((tq + BLOCK, 2 * KV_WIDTH), F32)
    return pl.pallas_call(
        body, name=f"attn_bwd_{layer}", grid=(nt + 1,),
        in_specs=[pl.BlockSpec((tq, PB_WIDTH), cur_map),
                  pl.BlockSpec((BLOCK, 2 * KV_WIDTH), prev_map),
                  pl.BlockSpec((tq, B_WIDTH), cur_map),
                  pl.BlockSpec((tq, B_HEADS * BLOCK), cur_map),
                  pl.BlockSpec((tq, LANES), cur_map),
                  pl.BlockSpec((B_HEADS, BLOCK, BLOCK), lambda t: (0, 0, 0))],
        out_specs=[pl.BlockSpec((tq, qz_width), cur_map),
                   pl.BlockSpec((tq, 2 * KV_WIDTH), lambda t: (t, 0)),
                   pl.BlockSpec((B_HEADS, BLOCK, BLOCK), lambda t: (0, 0, 0)),
                   pl.BlockSpec((B_HEADS, LANES), lambda t: (0, 0))],
        out_shape=[pltpu.HBM((tokens, qz_width), BF16),
                   pltpu.HBM((tokens + tq, 2 * KV_WIDTH), BF16),
                   pltpu.HBM((B_HEADS, BLOCK, BLOCK), F32),
                   pltpu.HBM((B_HEADS, LANES), F32)],
        scratch_shapes=[pltpu.VMEM((tq + BLOCK, 2 * KV_WIDTH), BF16),
                        pltpu.VMEM((tq + BLOCK, 2 * KV_WIDTH), F32)],
        compiler_params=pltpu.CompilerParams(dimension_semantics=("arbitrary",),
                                             vmem_limit_bytes=_vmem_limit(blocks, scratch)),
    )(_hbm(pb), _hbm(pb), _hbm(dyb), _hbm(probs_w), _hbm(sink_w), _hbm(dbias_in))


def _inproj_bwd_dx(da, dqz, dkv, w, x2, g, pre_g3, layer):
    tokens = x2.shape[0]
    tm = _token_tile(tokens)

    def body(da_ref, dqz_ref, dkv_ref, w_ref, x_ref, g_ref, pg_ref, gn_ref, dpg_ref):
        @pl.when(pl.program_id(0) == 0)
        def _():
            dpg_ref[...] = jnp.zeros_like(dpg_ref)

        dh = _mm(da_ref[...], w_ref[0:PA_WIDTH, :])
        dh += _mm(dqz_ref[:, 0:B_WIDTH], w_ref[PA_WIDTH:PA_WIDTH + B_WIDTH, :])
        dh += _mm(dkv_ref[...], w_ref[PA_WIDTH + K_OFF:PA_WIDTH + BZ_OFF, :])
        dh += _mm(dqz_ref[:, B_WIDTH:2 * B_WIDTH], w_ref[PA_WIDTH + BZ_OFF:IN_WIDTH, :])
        x = x_ref[...]
        r = lax.rsqrt(jnp.mean(x * x, axis=-1, keepdims=True) + NORM_EPS)
        xhat = x * r
        dhg = dh * pg_ref[...]
        dpg_ref[...] += jnp.sum(dh * xhat, axis=0, keepdims=True)
        gn_ref[...] = g_ref[...] + r * (dhg - xhat * jnp.mean(dhg * xhat, axis=-1, keepdims=True))

    blocks = [_nbytes((tm, PA_WIDTH), BF16), _nbytes((tm, 2 * B_WIDTH), BF16), _nbytes((tm, 2 * KV_WIDTH), BF16),
              _nbytes((D_MODEL, IN_WIDTH), BF16), 3 * _nbytes((tm, D_MODEL), F32)]
    return pl.pallas_call(
        body, name=f"inproj_bwd_dx_{layer}", grid=(tokens // tm,),
        in_specs=[pl.BlockSpec((tm, PA_WIDTH), lambda i: (i, 0)),
                  pl.BlockSpec((tm, 2 * B_WIDTH), lambda i: (i, 0)),
                  pl.BlockSpec((pl.Element(tm), pl.Element(2 * KV_WIDTH)), lambda i: (pl.multiple_of(i * tm + BLOCK, BLOCK), 0)),
                  pl.BlockSpec((IN_WIDTH, D_MODEL), lambda i: (0, 0)),
                  pl.BlockSpec((tm, D_MODEL), lambda i: (i, 0)),
                  pl.BlockSpec((tm, D_MODEL), lambda i: (i, 0)),
                  pl.BlockSpec((None, 1, D_MODEL), lambda i: (layer, 0, 0))],
        out_specs=[pl.BlockSpec((tm, D_MODEL), lambda i: (i, 0)),
                   pl.BlockSpec((1, D_MODEL), lambda i: (0, 0))],
        out_shape=[pltpu.HBM((tokens,D_MODEL), F32),
                   pltpu.HBM((1,D_MODEL), F32)],
        compiler_params=pltpu.CompilerParams(dimension_semantics=("arbitrary",),
                                             vmem_limit_bytes=_vmem_limit(blocks)),
    )(_hbm(da), _hbm(dqz), _hbm(dkv), _hbm(w), _hbm(x2), _hbm(g), pre_g3)


def _rel_bias_grad(dbias, col_bucket, flip):
    def body(db_ref, cb_ref, flip_ref, out_ref):
        cb = cb_ref[...]
        anti = flip_ref[...]
        sums = []
        for h in range(B_HEADS):
            x = db_ref[h]
            hi = x.astype(BF16)
            rest = x - hi.astype(F32)
            mid = rest.astype(BF16)
            low = (rest - mid.astype(F32)).astype(BF16)
            reversed_x = _mm(hi, anti) + _mm(mid, anti) + _mm(low, anti)
            rolled = pltpu.roll(reversed_x, 0, 1, stride=1, stride_axis=0)
            sums.append(jnp.sum(rolled, axis=0, keepdims=True))
        per_dist = jnp.concatenate(sums, axis=0)
        lane = lax.broadcasted_iota(jnp.int32, (B_HEADS, LANES), 1)
        out = jnp.zeros((B_HEADS, LANES), F32)
        for b in range(REL_BUCKETS):
            s = jnp.sum(jnp.where(cb == b, per_dist, 0.0), axis=-1, keepdims=True)
            out = jnp.where(lane == b, s, out)
        out_ref[...] = out

    vm = pl.BlockSpec(memory_space=pltpu.VMEM)
    return pl.pallas_call(
        body, name="rel_bias_grad",
        out_shape=jax.ShapeDtypeStruct((B_HEADS, LANES), F32),
        in_specs=[vm, vm, vm], out_specs=vm,
    )(dbias, col_bucket, flip)


def _reduce_adamw(slots, w, m, v, name):
    rows, cols = w.shape
    tr = _row_tile(rows)
    c1 = 1.0 / (1.0 - ADAM_B1 ** ADAM_STEP)
    c2 = 1.0 / (1.0 - ADAM_B2 ** ADAM_STEP)

    def body(s_ref, w_ref, m_ref, v_ref, g_ref, d_ref, nm_ref, nv_ref):
        g = s_ref[0].astype(F32)
        for i in range(1, N_DEV):
            g = g + s_ref[i].astype(F32)
        nm = ADAM_B1 * m_ref[...] + (1.0 - ADAM_B1) * g
        nv = ADAM_B2 * v_ref[...] + (1.0 - ADAM_B2) * (g * g)
        g_ref[...] = g
        nm_ref[...] = nm
        nv_ref[...] = nv
        d_ref[...] = -ADAM_LR * ((nm * c1) / (jnp.sqrt(nv * c2) + ADAM_EPS) + ADAM_WD * w_ref[...])

    blocks = [_nbytes((N_DEV, tr, cols), slots.dtype), 7 * _nbytes((tr, cols), F32)]
    tile = pl.BlockSpec((tr, cols), lambda i: (i, 0))
    return pl.pallas_call(
        body, name=name, grid=(rows // tr,),
        in_specs=[pl.BlockSpec((N_DEV, tr, cols), lambda i: (0, i, 0)), tile, tile, tile],
        out_specs=[tile] * 4,
        out_shape=[pltpu.HBM((rows, cols), F32)] * 4,
        compiler_params=pltpu.CompilerParams(dimension_semantics=("parallel",),
                                             vmem_limit_bytes=_vmem_limit(blocks)),
    )(_hbm(slots), _hbm(w), _hbm(m), _hbm(v))


_SMALL = ("pre_norm_g","post_norm_g", "ln_v_g", "ln_v_b", "b_spatial", "sinks", "rel_bias", "loss")


def _pack_small(parts):
    slabs = []
    for name in _SMALL:
        flat = parts[name].astype(F32).reshape(-1)
        pad = (-flat.shape[0]) % (8 * LANES)
        slabs.append(jnp.pad(flat, (0, pad)).reshape(-1, LANES))
    return jnp.concatenate(slabs, axis=0)


def _unpack_small(slab, shapes):
    out, row = {}, 0
    for name in _SMALL:
        size = int(np.prod(shapes[name]))
        rows = -(-size // (8 * LANES)) * 8
        out[name] = slab[row:row + rows].reshape(-1)[:size].reshape(shapes[name])
        row += rows
    return out


def kernel(x, pre_norm_g, w_in, ln_v_g, ln_v_b, w_spatial, b_spatial, sinks, rel_bias, w_out, post_norm_g, loss_target, m_pre_norm_g, m_w_in, m_ln_v_g, m_ln_v_b, m_w_spatial, m_b_spatial, m_sinks, m_rel_bias, m_w_out, m_post_norm_g, v_pre_norm_g, v_w_in, v_ln_v_g, v_ln_v_b, v_w_spatial, v_b_spatial, v_sinks, v_rel_bias, v_w_out, v_post_norm_g):
    b_loc, seq, _ = x.shape
    tokens = b_loc * seq
    depth = w_in.shape[0]
    in_shard = w_in.shape[2]
    out_shard = w_out.shape[1]
    assert in_shard * N_DEV == IN_WIDTH and out_shard * N_DEV == D_MODEL and seq % BLOCK == 0

    me = _slot(lax.axis_index("x"), lax.axis_index("y"), lax.axis_index("c"))
    w_in_t, m_w_in_t, v_w_in_t = (jnp.swapaxes(a, 1, 2) for a in (w_in, m_w_in, v_w_in))
    w_in16, w_out16 = w_in_t.astype(BF16), w_out.astype(BF16)

    def gather_start(layer, not_before):
        zones = [lax.empty((N_DEV, in_shard, D_MODEL), BF16), lax.empty((N_DEV, out_shard, D_MODEL), BF16)]
        sems, sends, zones, pre_g = _exchange_start(
            [w_in16[layer], w_out16[layer]], zones, [0, 0], f"weights_send_{layer}",
            pre_norm_g.reshape(depth, 1, D_MODEL), after=not_before)
        return (sems, sends, [0, 0]), zones, pre_g

    def full_weights(gin, gout):
        return gin.reshape(IN_WIDTH, D_MODEL), gout.reshape(D_MODEL, D_MODEL)

    def gather_wait(layer, started, zones, after):
        gin, gout = _exchange_wait([started], zones, after, f"weights_wait_{layer}")
        gin = lax.dynamic_update_index_in_dim(gin, w_in16[layer], me, 0)
        gout = lax.dynamic_update_index_in_dim(gout, w_out16[layer], me, 0)
        return full_weights(gin, gout)

    causal = jnp.tril(jnp.ones((CHUNK, CHUNK), dtype=bool))
    wm = jnp.where(causal, w_spatial, 0.0).astype(BF16)
    wmt = _hbm(jnp.swapaxes(wm, -1, -2))
    wm = _hbm(wm)
    bsb = _hbm(jnp.repeat(jnp.swapaxes(b_spatial, -1, -2), HEAD_DIM, axis=-1))
    pre_g3 = _hbm(pre_norm_g.reshape(depth, 1, D_MODEL))
    post_g3 = _hbm(post_norm_g.reshape(depth, 1, D_MODEL))
    ln_g3 = _hbm(ln_v_g.reshape(depth, 1, A_WIDTH))
    ln_b3 = _hbm(ln_v_b.reshape(depth, 1, A_WIDTH))
    bucket = jnp.asarray(_bucket_table())
    bias = _bias_table(rel_bias, bucket)

    xs, saved, weights = [x.reshape(tokens, D_MODEL)], [], []
    pending = None
    for layer in range(depth):
        if layer == 0:
            w, wo = full_weights(*_all_gather([w_in16[0], w_out16[0]], "weights_gather_0"))
        else:
            w, wo = gather_wait(layer, pending[0], pending[1], xs[-1])
        pre_g_fwd = pre_g3
        if layer + 1 < depth:
            pending = gather_start(layer + 1, w)
            pre_g_fwd = _hbm(pending[2])
        h, pa, pb, ya, kept, rstd_w = _inproj_gmlp_fwd(xs[-1], pre_g_fwd, w, ln_g3, ln_b3, wm, bsb, layer)
        if layer + 1 < depth:
            yb, pw, spw, y, x_next = _attn_outproj_fwd(pb, bias, sinks, ya, wo, xs[-1], post_g3, b_loc, layer)
            xs.append(x_next)
        else:
            yb, pw, spw, y, g, loss_part = _attn_outproj_fwd(pb, bias, sinks, ya, wo, xs[-1], post_g3, b_loc, layer,
                                                    target=loss_target.reshape(tokens, D_MODEL))
        saved.append((h, pa, pb, ya, yb, y, pw, spw, kept, rstd_w))
        weights.append((w, wo))

    grads = {name: [None] * depth for name in ("pre_norm_g","post_norm_g", "ln_v_g", "ln_v_b",
                                               "b_spatial", "sinks", "dbias")}
    zone_in = lax.empty((N_DEV, depth * in_shard, D_MODEL), BF16)
    zone_out = lax.empty((N_DEV, depth * out_shard, D_MODEL), BF16)
    zone_ws = lax.empty((N_DEV, depth * A_GROUPS * CHUNK, CHUNK), F32)
    started_in, started_out, own_in, own_out = [], [], [None] * depth, [None] * depth
    own_ws = [None] * depth
    dbias = jnp.zeros((B_HEADS, BLOCK, BLOCK), F32)
    for layer in reversed(range(depth)):
        h, pa, pb, ya, yb, y, pw, spw, kept, rstd_w = saved[layer]
        w, wo = weights[layer]
        dya, dyb, dwo, dpost = _outproj_bwd(g, y, ya, yb, wo, post_g3, layer)
        send_out = dwo.reshape(N_DEV, out_shard, D_MODEL)
        own_out[layer] = lax.dynamic_index_in_dim(send_out, me, 0, keepdims=False)
        sems, sends, (zone_out,), ln_g_bwd = _exchange_start(
            [send_out], [zone_out], [layer * out_shard], f"grads_send_out_{layer}", ln_v_g.reshape(depth, 1, A_WIDTH))
        started_out.append((sems, sends, [layer * out_shard]))
        dqz, dkv, dbias, dsink = _attn_bwd(pb, dyb, pw, spw, dbias, layer)
        da, dw, dws, dbs, dlg, dlb = _gmlp_bwd_dw(pa, kept, rstd_w, dya, h, dqz, dkv, _hbm(ln_g_bwd), ln_b3, wmt, layer)
        own_ws[layer] = dws.reshape(A_GROUPS * CHUNK, CHUNK)
        send_in = dw.reshape(N_DEV, in_shard, D_MODEL)
        own_in[layer] = lax.dynamic_index_in_dim(send_in, me, 0, keepdims=False)
        row_offs = [layer * A_GROUPS * CHUNK, layer * in_shard]
        sems, sends, (zone_ws, zone_in), pre_g_bwd = _exchange_start(
            [own_ws[layer], send_in], [zone_ws, zone_in], row_offs, f"grads_send_in_{layer}",
            pre_norm_g.reshape(depth, 1, D_MODEL))
        started_in.append((sems, sends, row_offs))
        g, dpre = _inproj_bwd_dx(da, dqz, dkv, w, xs[layer], g, _hbm(pre_g_bwd), layer)
        grads["b_spatial"][layer] = dbs[:, :A_GROUPS].T
        grads["ln_v_g"][layer] = dlg[0]
        grads["ln_v_b"][layer] = dlb[0]
        grads["sinks"][layer] = dsink[0, :B_HEADS]
        grads["pre_norm_g"][layer] = dpre[0]
        grads["post_norm_g"][layer] = dpost[0]
    grad_x = g.reshape(x.shape)
    col_bucket = jnp.asarray(np.broadcast_to(_bucket_table()[0:1, ::-1], (B_HEADS, BLOCK)))
    flip = jnp.asarray(np.eye(BLOCK, dtype=np.float32)[::-1], dtype=BF16)
    drel = _rel_bias_grad(dbias, col_bucket, flip)[:, :REL_BUCKETS].T

    (recv_out,) = _exchange_wait(started_out, [zone_out], g, "grads_wait_out")
    recv_out = lax.dynamic_update_index_in_dim(recv_out, jnp.concatenate(own_out, axis=0), me, 0)
    res_out = _reduce_adamw(recv_out, w_out.reshape(-1, D_MODEL), m_w_out.reshape(-1, D_MODEL),
                            v_w_out.reshape(-1, D_MODEL), "adamw_w_out")
    res_out = [r.reshape(w_out.shape) for r in res_out]

    no_state = jnp.zeros((1,), F32)
    small_w = dict(pre_norm_g=pre_norm_g, post_norm_g=post_norm_g, ln_v_g=ln_v_g,
                   ln_v_b=ln_v_b, b_spatial=b_spatial, sinks=sinks, rel_bias=rel_bias, loss=no_state)
    small_m = dict(pre_norm_g=m_pre_norm_g, post_norm_g=m_post_norm_g, ln_v_g=m_ln_v_g,
                   ln_v_b=m_ln_v_b, b_spatial=m_b_spatial, sinks=m_sinks, rel_bias=m_rel_bias, loss=no_state)
    small_v = dict(pre_norm_g=v_pre_norm_g, post_norm_g=v_post_norm_g, ln_v_g=v_ln_v_g,
                   ln_v_b=v_ln_v_b, b_spatial=v_b_spatial, sinks=v_sinks, rel_bias=v_rel_bias, loss=no_state)
    small_g = {name: jnp.stack(grads[name]) for name in _SMALL if name not in ("rel_bias", "loss")}
    small_g["rel_bias"] = drel
    small_g["loss"] = loss_part[0, :1]
    shapes = {name: small_w[name].shape for name in _SMALL}
    slots = _all_gather_direct(_pack_small(small_g), "small_grads_all_gather")
    res_small = _reduce_adamw(slots, _pack_small(small_w), _pack_small(small_m), _pack_small(small_v), "adamw_small")

    recv_ws, recv_in = _exchange_wait(started_in, [zone_ws, zone_in], res_small[0], "grads_wait_in")
    recv_ws = lax.dynamic_update_index_in_dim(recv_ws, jnp.concatenate(own_ws, axis=0), me, 0)
    res_ws = _reduce_adamw(recv_ws, w_spatial.reshape(-1, CHUNK), m_w_spatial.reshape(-1, CHUNK),
                           v_w_spatial.reshape(-1, CHUNK), "adamw_w_spatial")
    res_ws = [r.reshape(w_spatial.shape) for r in res_ws]
    recv_in = lax.dynamic_update_index_in_dim(recv_in, jnp.concatenate(own_in, axis=0), me, 0)
    res_in = _reduce_adamw(recv_in, w_in_t.reshape(-1, D_MODEL), m_w_in_t.reshape(-1, D_MODEL),
                           v_w_in_t.reshape(-1, D_MODEL), "adamw_w_in")
    res_in = [jnp.swapaxes(r.reshape(w_in_t.shape), 1, 2) for r in res_in]
    res_small = [_unpack_small(r, shapes) for r in res_small]

    order = ("pre_norm_g", "w_in", "ln_v_g", "ln_v_b", "w_spatial", "b_spatial", "sinks", "rel_bias", "w_out",
             "post_norm_g")
    outs = [res_small[0]["loss"][0], grad_x]
    for kind in range(4):
        for name in order:
            if name == "w_in":
                outs.append(res_in[kind])
            elif name == "w_out":
                outs.append(res_out[kind])
            elif name == "w_spatial":
                outs.append(res_ws[kind])
            else:
                outs.append(res_small[kind][name])
    return tuple(outs)
```

```python
import math

import numpy as np
import jax
import jax.numpy as jnp
from jax import lax
from jax.experimental import pallas as pl
from jax.experimental.pallas import tpu as pltpu

F32 = jnp.float32
BF16 = jnp.bfloat16

D_MODEL = 1024
A_WIDTH = 512
A_GROUPS = 8
CHUNK = 128
B_HEADS = 8
HEAD_DIM = 64
B_WIDTH = 512
KV_WIDTH = 128
BLOCK = 128
REL_BUCKETS = 32
REL_MAX_DIST = 128
NORM_EPS = 1e-6
PA_WIDTH = 3 * A_WIDTH
PB_WIDTH = 2 * B_WIDTH + 2 * KV_WIDTH
IN_WIDTH = PA_WIDTH + PB_WIDTH
K_OFF, BZ_OFF = B_WIDTH, B_WIDTH + 2 * KV_WIDTH
KEPT_WIDTH = 5 * A_WIDTH
SCALE = HEAD_DIM ** -0.5
NEG = -1e30
N_DEV = 8
LANES = 128

ADAM_LR = 0.001
ADAM_B1 = 0.9
ADAM_B2 = 0.999
ADAM_EPS = 1e-08
ADAM_WD = 0.01
ADAM_STEP = 10

V7X_VMEM_BYTES = 64 * 1024 * 1024
VMEM_TEMP_BYTES = 12 * 1024 * 1024
MESH = pl.DeviceIdType.MESH


def _vmem_limit(block_bytes, scratch_bytes=0):
    need = 2 * sum(block_bytes) + scratch_bytes + VMEM_TEMP_BYTES
    return int(min(need, V7X_VMEM_BYTES - 8 * 1024 * 1024))


def _nbytes(shape, dtype):
    return int(np.prod(shape)) * jnp.dtype(dtype).itemsize


def _token_tile(tokens, cap=512):
    tile = min(cap, tokens // 2)
    assert tokens % tile == 0 and tile % CHUNK == 0, tokens
    return tile


def _row_tile(rows, cap=512):
    best = 8
    for t in range(8, cap + 1, 8):
        if rows % t == 0:
            best = t
    assert rows % best == 0, rows
    return best


def _mm(a, b):
    return lax.dot_general(a, b, (((1,), (0,)), ((), ())), preferred_element_type=F32)


def _mm_nt(a, b):
    return lax.dot_general(a, b, (((1,), (1,)), ((), ())), preferred_element_type=F32)


def _mm_tn(a, b):
    return lax.dot_general(a, b, (((0,), (0,)), ((), ())), preferred_element_type=F32)


_GELU_C = math.sqrt(2.0 / math.pi)


_GELU_A = _GELU_C * 0.044715


def _gelu_parts(x):
    x2 = x * x
    t = jnp.tanh(x * (_GELU_C + _GELU_A * x2))
    return x2, t, 0.5 + 0.5 * t


def _gelu_and_grad(x):
    x2, t, half_plus = _gelu_parts(x)
    grad = half_plus + (0.5 * x) * (1.0 - t * t) * (_GELU_C + (3.0 * _GELU_A) * x2)
    return x * half_plus, grad


def _sigmoid(x):
    return 0.5 + 0.5 * jnp.tanh(0.5 * x)


def _bucket_table():
    q = np.arange(BLOCK)[:, None]
    k = np.arange(BLOCK)[None, :]
    dist = np.where(k <= q, q - k, q + BLOCK - k)
    max_exact = REL_BUCKETS // 2
    safe = np.maximum(dist, 1).astype(np.float32)
    large = max_exact + (np.log(safe / np.float32(max_exact)) / np.float32(math.log(REL_MAX_DIST / max_exact))
                         * np.float32(REL_BUCKETS - max_exact)).astype(np.int32)
    large = np.minimum(large, REL_BUCKETS - 1)
    assert dist.min() >= 0 and dist.max() < BLOCK
    return np.where(dist < max_exact, dist, large).astype(np.int32)


def _hbm(x):
    return pltpu.with_memory_space_constraint(x, pltpu.HBM)


def _slot(px, py, pc):
    return 4 * px + 2 * py + pc


def _all_gather(arrs, name):
    n = len(arrs)

    def body(*refs):
        ins, outs = refs[:n], refs[n:2 * n]
        send_sems, recv_sems, local_sems = refs[2 * n:]
        x, y, c = lax.axis_index("x"), lax.axis_index("y"), lax.axis_index("c")
        me, sibling = (x, y, c), (x, y, 1 - c)
        chips = [(1 - x, y), (x, 1 - y), (1 - x, 1 - y)]

        def copy(a, k, block, to, src=None):
            dst = outs[a].at[_slot(*block)]
            return pltpu.make_async_remote_copy(
                src_ref=dst if src is None else src, dst_ref=dst,
                send_sem=send_sems.at[7 * a + k], recv_sem=recv_sems.at[7 * a + k],
                device_id=to, device_id_type=MESH)

        mine = [pltpu.make_async_copy(ins[a], outs[a].at[_slot(*me)], local_sems.at[a]) for a in range(n)]
        for cp in mine:
            cp.start()
        first = []
        for a in range(n):
            first.append(copy(a, 0, me, sibling, src=ins[a]))
            first += [copy(a, 1 + j, me, (*chip, c), src=ins[a]) for j, chip in enumerate(chips)]
        for cp in first:
            cp.start()
        passed = []
        for j, chip in enumerate(chips):
            for a in range(n):
                copy(a, 1 + j, (*chip, c), me).wait_recv()
                fwd = copy(a, 4 + j, (*chip, c), sibling)
                fwd.start()
                passed.append(fwd)
        for a in range(n):
            copy(a, 0, sibling, me).wait_recv()
            for j, chip in enumerate(chips):
                copy(a, 4 + j, (*chip, 1 - c), me).wait_recv()
        for cp in first + passed:
            cp.wait_send()
        for cp in mine:
            cp.wait()

    any_spec = pl.BlockSpec(memory_space=pl.ANY)
    return pl.pallas_call(
        body, name=name,
        out_shape=[jax.ShapeDtypeStruct((N_DEV,) + a.shape, a.dtype) for a in arrs],
        in_specs=[any_spec] * n, out_specs=[any_spec] * n,
        scratch_shapes=[pltpu.SemaphoreType.DMA((7 * n,)), pltpu.SemaphoreType.DMA((7 * n,)),
                        pltpu.SemaphoreType.DMA((n,))],
    )(*arrs)


def _all_gather_direct(arr, name):
    def body(in_ref, out_ref, send_sems, recv_sems, local_sem):
        me, peers = _peers()
        mine = pltpu.make_async_copy(in_ref, out_ref.at[_slot(*me)], local_sem)
        mine.start()

        def copy(k, origin, to):
            return pltpu.make_async_remote_copy(
                src_ref=in_ref, dst_ref=out_ref.at[_slot(*origin)], send_sem=send_sems.at[k],
                recv_sem=recv_sems.at[k], device_id=to, device_id_type=MESH)

        sends = [copy(k, me, peer) for k, peer in enumerate(peers)]
        for cp in sends:
            cp.start()
        for k, peer in enumerate(peers):
            copy(k, peer, me).wait_recv()
        for cp in sends:
            cp.wait_send()
        mine.wait()

    any_spec = pl.BlockSpec(memory_space=pl.ANY)
    return pl.pallas_call(
        body, name=name,
        out_shape=jax.ShapeDtypeStruct((N_DEV,) + arr.shape, arr.dtype),
        in_specs=[any_spec], out_specs=any_spec,
        scratch_shapes=[pltpu.SemaphoreType.DMA((7,)), pltpu.SemaphoreType.DMA((7,)), pltpu.SemaphoreType.DMA],
    )(arr)


_HBM_SPEC = pl.BlockSpec(memory_space=pltpu.HBM)
_SEM_SPEC = pl.BlockSpec(memory_space=pltpu.SEMAPHORE)
_DATAFLOW = pltpu.SideEffectType.DATAFLOW_SIDE_EFFECTING


def _peers():
    x, y, c = lax.axis_index("x"), lax.axis_index("y"), lax.axis_index("c")
    peers = []
    for k in range(1, N_DEV):
        fx, fy, fc = (k >> 2) & 1, (k >> 1) & 1, k & 1
        peers.append((1 - x if fx else x, 1 - y if fy else y, 1 - c if fc else c))
    return (x, y, c), peers


def _exchange_copy(send_ref, land_ref, row_off, src_slot, dst_slot, sems, idx, peer):
    rows = send_ref.shape[-2]
    src = send_ref.at[src_slot] if len(send_ref.shape) == 3 else send_ref
    return pltpu.make_async_remote_copy(
        src_ref=src, dst_ref=land_ref.at[dst_slot, pl.ds(row_off, rows), :],
        send_sem=sems[0].at[idx], recv_sem=sems[1].at[idx], device_id=peer, device_id_type=MESH)


def _exchange_start(sends, lands, row_offs, name, carry, after=None):
    n = len(sends)

    def body(*refs):
        ins, zones, carry_ref = refs[:n], refs[n:2 * n], refs[2 * n]
        first_out = 2 * n + 1 + (after is not None)
        sems = refs[first_out:first_out + 2]
        carry_out = refs[-1]
        me, peers = _peers()
        for a in range(n):
            for k, peer in enumerate(peers):
                _exchange_copy(ins[a], zones[a], row_offs[a], _slot(*peer), _slot(*me), sems, 7 * a + k, peer).start()
        carry_out[...] = carry_ref[...]

    arrays = [_hbm(a) for a in list(sends) + list(lands)]
    vmem = pl.BlockSpec(memory_space=pltpu.VMEM)
    out = pl.pallas_call(
        body, name=name,
        out_shape=(pltpu.SemaphoreType.DMA((7 * n,)), pltpu.SemaphoreType.DMA((7 * n,)),
                   *[pltpu.HBM(a.shape, a.dtype) for a in arrays], jax.ShapeDtypeStruct(carry.shape, carry.dtype)),
        in_specs=[_HBM_SPEC] * (2 * n) + [vmem] + ([pl.BlockSpec(memory_space=pl.ANY)] if after is not None else []),
        out_specs=(_SEM_SPEC, _SEM_SPEC, *[_HBM_SPEC] * (2 * n), vmem),
        input_output_aliases={i: 2 + i for i in range(2 * n)},
        compiler_params=pltpu.CompilerParams(has_side_effects=_DATAFLOW),
    )(*arrays, carry, *([after] if after is not None else []))
    return (out[0], out[1]), list(out[2:2 + n]), list(out[2 + n:2 + 2 * n]), out[-1]


def _exchange_wait(started, lands, after, name):
    n = len(lands)
    flat_sends = [s for _, sends, _ in started for s in sends]
    flat_sems = [s for sems, _, _ in started for s in sems]
    ns = len(flat_sends)

    def body(*refs):
        ins, zones = refs[:ns], refs[ns:ns + n]
        sem_refs = refs[ns + n:ns + n + len(flat_sems)]
        me, peers = _peers()
        pos = 0
        for call, (_, sends, row_offs) in enumerate(started):
            sems = sem_refs[2 * call:2 * call + 2]
            for a in range(len(sends)):
                for k, peer in enumerate(peers):
                    cp = _exchange_copy(ins[pos + a], zones[a], row_offs[a], _slot(*peer), _slot(*peer), sems,
                                        7 * a + k, peer)
                    cp.wait_send()
                    cp.wait_recv()
            pos += len(sends)

    arrays = list(flat_sends) + list(lands)
    out = pl.pallas_call(
        body, name=name,
        out_shape=tuple(pltpu.HBM(a.shape, a.dtype) for a in arrays),
        in_specs=[_HBM_SPEC] * len(arrays) + [_SEM_SPEC] * len(flat_sems) + [pl.BlockSpec(memory_space=pl.ANY)],
        out_specs=tuple([_HBM_SPEC] * len(arrays)),
        input_output_aliases={i: i for i in range(len(arrays))},
        compiler_params=pltpu.CompilerParams(has_side_effects=_DATAFLOW),
    )(*arrays, *flat_sems, after)
    return list(out[ns:])


def _bias_table(rel_bias, bucket):
    def body(rel_ref, bucket_ref, out_ref):
        bk = bucket_ref[...]
        for h in range(B_HEADS):
            def pick(b, acc, h=h):
                return jnp.where(bk == b, rel_ref[b, h], acc)
            out_ref[h] = lax.fori_loop(0, REL_BUCKETS, pick, jnp.zeros((BLOCK, BLOCK), F32))

    return pl.pallas_call(
        body, name="bias_table",
        out_shape=jax.ShapeDtypeStruct((B_HEADS, BLOCK, BLOCK), F32),
        in_specs=[pl.BlockSpec(memory_space=pltpu.SMEM), pl.BlockSpec(memory_space=pltpu.VMEM)],
        out_specs=pl.BlockSpec(memory_space=pltpu.VMEM),
    )(rel_bias, bucket)


def _inproj_gmlp_fwd(x2, pre_g3, w, ln_g3, ln_b3, wm, bsb, layer):
    tokens = x2.shape[0]
    tm = _token_tile(tokens)

    def body(x_ref, g_ref, w_ref, lg_ref, lb_ref, wm_ref, bsb_ref, h_ref, pa_ref, pb_ref, ya_ref, kept_ref, rstd_ref):
        x = x_ref[...]
        r = lax.rsqrt(jnp.mean(x * x, axis=-1, keepdims=True) + NORM_EPS)
        h = (x * r * g_ref[...]).astype(BF16)
        h_ref[...] = h
        pa_ref[...] = _mm_nt(h, w_ref[0:PA_WIDTH, :]).astype(BF16)
        pb_ref[...] = _mm_nt(h, w_ref[PA_WIDTH:IN_WIDTH, :]).astype(BF16)
        for ci in range(tm // CHUNK):
            rows = slice(ci * CHUNK, (ci + 1) * CHUNK)
            gu, gv, pz, u, vhat, rstd, _, mixed, sg = _gmlp_forward_chunk(
                pa_ref[rows, :], lg_ref[...], lb_ref[...], wm_ref, bsb_ref[...])
            ya_ref[rows, :] = (u * mixed * (pz * sg)).astype(BF16)
            for k, val in enumerate((u, gu, gv, vhat, mixed)):
                kept_ref[rows, A_WIDTH * k:A_WIDTH * (k + 1)] = val.astype(BF16)
            rstd_ref[rows, :] = jnp.broadcast_to(rstd, (CHUNK, LANES))

    blocks = [_nbytes((tm, D_MODEL), F32), _nbytes((D_MODEL, IN_WIDTH), BF16),
              _nbytes((tm, D_MODEL), BF16), _nbytes((tm, PA_WIDTH), BF16), _nbytes((tm, PB_WIDTH), BF16),
              _nbytes((A_GROUPS, CHUNK, CHUNK), BF16), _nbytes((CHUNK, A_WIDTH), F32), _nbytes((tm, A_WIDTH), BF16)]
    return pl.pallas_call(
        body, name=f"inproj_gmlp_fwd_{layer}", grid=(tokens // tm,),
        in_specs=[pl.BlockSpec((tm, D_MODEL), lambda i: (i, 0)),
                  pl.BlockSpec((None, 1, D_MODEL), lambda i: (layer, 0, 0)),
                  pl.BlockSpec((IN_WIDTH, D_MODEL), lambda i: (0, 0)),
                  pl.BlockSpec((None, 1, A_WIDTH), lambda i: (layer, 0, 0)),
                  pl.BlockSpec((None, 1, A_WIDTH), lambda i: (layer, 0, 0)),
                  pl.BlockSpec((None, A_GROUPS, CHUNK, CHUNK), lambda i: (layer, 0, 0, 0)),
                  pl.BlockSpec((None, CHUNK, A_WIDTH), lambda i: (layer, 0, 0))],
        out_specs=[pl.BlockSpec((tm, D_MODEL), lambda i: (i, 0)),
                   pl.BlockSpec((tm, PA_WIDTH), lambda i: (i, 0)),
                   pl.BlockSpec((tm, PB_WIDTH), lambda i: (i, 0)),
                   pl.BlockSpec((tm, A_WIDTH), lambda i: (i, 0)),
                   pl.BlockSpec((tm, KEPT_WIDTH), lambda i: (i, 0)),
                   pl.BlockSpec((tm, LANES), lambda i: (i, 0))],
        out_shape=[pltpu.HBM((tokens, D_MODEL), BF16),
                   pltpu.HBM((tokens, PA_WIDTH), BF16),
                   pltpu.HBM((tokens, PB_WIDTH), BF16),
                   pltpu.HBM((tokens, A_WIDTH), BF16),
                   pltpu.HBM((tokens, KEPT_WIDTH), BF16),
                   pltpu.HBM((tokens, LANES), F32)],
        compiler_params=pltpu.CompilerParams(dimension_semantics=("parallel",),
                                             vmem_limit_bytes=_vmem_limit(blocks + [_nbytes((tm, KEPT_WIDTH), BF16),
                                                                                    _nbytes((tm, LANES), F32)])),
    )(_hbm(x2), pre_g3, _hbm(w), ln_g3, ln_b3, wm, bsb)


def _gmlp_forward_chunk(pa, ln_g, ln_b, wm_ref, bsb):
    pu = pa[:, 0:A_WIDTH].astype(F32)
    pv = pa[:, A_WIDTH:2 * A_WIDTH].astype(F32)
    pz = pa[:, 2 * A_WIDTH:3 * A_WIDTH].astype(F32)
    u, gu = _gelu_and_grad(pu)
    vv, gv = _gelu_and_grad(pv)
    mu = jnp.mean(vv, axis=-1, keepdims=True)
    xc = vv - mu
    rstd = lax.rsqrt(jnp.mean(xc * xc, axis=-1, keepdims=True) + NORM_EPS)
    vhat = xc * rstd
    vnb = (vhat * ln_g + ln_b).astype(BF16)
    low = lax.broadcasted_iota(jnp.int32, (CHUNK, LANES), 1) < HEAD_DIM
    parts = []
    for p in range(A_GROUPS // 2):
        vp = vnb[:, LANES * p:LANES * (p + 1)]
        parts.append(jnp.where(low, _mm(wm_ref[2 * p], vp), _mm(wm_ref[2 * p + 1], vp)))
    mixed = jnp.concatenate(parts, axis=1) + bsb
    sg = _sigmoid(pz)
    return gu, gv, pz, u, vhat, rstd, vnb, mixed, sg


def _kv_variants(kv):
    t = kv.astype(F32)
    rolled = pltpu.roll(t, HEAD_DIM, 1)
    low = lax.broadcasted_iota(jnp.int32, t.shape, 1) < HEAD_DIM
    zero = jnp.zeros_like(t)
    head0 = (jnp.where(low, t, zero).astype(BF16), jnp.where(low, zero, rolled).astype(BF16))
    head1 = (jnp.where(low, rolled, zero).astype(BF16), jnp.where(low, zero, t).astype(BF16))
    return head0, head1


def _band_masks():
    row = lax.broadcasted_iota(jnp.int32, (BLOCK, BLOCK), 0)
    col = lax.broadcasted_iota(jnp.int32, (BLOCK, BLOCK), 1)
    return col <= row


def _wrap(full, tri):
    return jnp.where(tri, full[:, BLOCK:2 * BLOCK], full[:, 0:BLOCK])


def _attn_probs(sf, bias_h, sink, tri, kill):
    s = _wrap(sf, tri) + bias_h
    s = jnp.where(kill, NEG, s)
    m = jnp.maximum(jnp.max(s, axis=-1, keepdims=True), sink)
    e = jnp.exp(s - m)
    es = jnp.exp(sink - m)
    inv = 1.0 / (jnp.sum(e, axis=-1, keepdims=True) + es)
    return e * inv, es * inv


def _unwrap16(p, tri):
    p = p.astype(BF16)
    zero = jnp.zeros_like(p)
    return jnp.concatenate([jnp.where(tri, zero, p), jnp.where(tri, p, zero)], axis=1)


def _fill_kv(kv_ref, prev_ref, cur_ref):
    kv_ref[0:BLOCK, :] = prev_ref[...]
    kv_ref[BLOCK:, :] = cur_ref[:, K_OFF:K_OFF + 2 * KV_WIDTH]


def _attn_outproj_fwd(pb, bias, sinks, ya, wo, x2, post_g3, b_loc, layer, target=None):
    tokens = pb.shape[0]
    nb = tokens // b_loc // BLOCK
    tq = _token_tile(tokens)
    per_tile = tq // BLOCK
    steps = tokens // tq
    with_loss = target is not None
    half = D_MODEL // 2

    def body(*refs):
        sink_ref, cur_ref, prev_ref, bias_ref, ya_ref, woa_ref, wob_ref, x_ref, g_ref = refs[:9]
        if with_loss:
            t_ref, yb_ref, pw_ref, spw_ref, y_ref, grad_ref, loss_ref, kv_ref, acc_ref = refs[9:]
        else:
            yb_ref, pw_ref, spw_ref, y_ref, xn_ref, kv_ref = refs[9:]
        t = pl.program_id(0)
        _fill_kv(kv_ref, prev_ref, cur_ref)
        tri = _band_masks()
        head_lane = lax.broadcasted_iota(jnp.int32, (BLOCK, LANES), 1)

        def block(i, carry):
            start = i * BLOCK
            rows = pl.ds(start, BLOCK)
            first = lax.rem(t * per_tile + i, nb) == 0
            kill = jnp.logical_and(first, jnp.logical_not(tri))
            kv = kv_ref[pl.ds(start, 2 * BLOCK), :]
            k_ops = _kv_variants(kv[:, 0:KV_WIDTH])
            v_ops = _kv_variants(kv[:, KV_WIDTH:2 * KV_WIDTH])
            p16 = {}
            sink_cols = jnp.zeros((BLOCK, LANES), F32)
            for kvh in range(2):
                pairs = (2 * kvh, 2 * kvh + 1)
                qs = jnp.concatenate([cur_ref[rows, LANES * p:LANES * (p + 1)] for p in pairs], axis=0) * SCALE
                for j in range(2):
                    sf = _mm_nt(qs, k_ops[kvh][j])
                    for r, p in enumerate(pairs):
                        hd = 2 * p + j
                        probs, sink_p = _attn_probs(sf[BLOCK * r:BLOCK * (r + 1)], bias_ref[hd],
                                                    sink_ref[layer, hd], tri, kill)
                        probs = probs.astype(BF16)
                        p16[hd] = _unwrap16(probs, tri)
                        pw_ref[rows, LANES * hd:LANES * (hd + 1)] = probs
                        sink_cols = jnp.where(head_lane == hd, sink_p, sink_cols)
            spw_ref[rows, :] = sink_cols
            for kvh in range(2):
                pairs = (2 * kvh, 2 * kvh + 1)
                out = jnp.zeros((2 * BLOCK, LANES), F32)
                for j in range(2):
                    out = out + _mm(jnp.concatenate([p16[2 * p + j] for p in pairs], axis=0), v_ops[kvh][j])
                for r, p in enumerate(pairs):
                    bz = cur_ref[rows, BZ_OFF + LANES * p:BZ_OFF + LANES * (p + 1)].astype(F32)
                    yb_ref[rows, LANES * p:LANES * (p + 1)] = (
                        out[BLOCK * r:BLOCK * (r + 1)] * (bz * _sigmoid(bz))).astype(BF16)
            y = _mm(ya_ref[rows, :], woa_ref[...]) + _mm(yb_ref[rows, :], wob_ref[...])
            r = lax.rsqrt(jnp.mean(y * y, axis=-1, keepdims=True) + NORM_EPS)
            x_next = x_ref[rows, :] + y * r * g_ref[...]
            y_ref[rows, :] = y.astype(BF16)
            if not with_loss:
                xn_ref[rows, :] = x_next
            else:
                err = x_next - t_ref[rows, :]
                grad_ref[rows, :] = err * (1.0 / D_MODEL)
                acc_ref[...] += jnp.sum(err * err, axis=0, keepdims=True)
            return carry

        if with_loss:
            @pl.when(t == 0)
            def _():
                acc_ref[...] = jnp.zeros_like(acc_ref)

        for i in range(per_tile):
            block(i, 0)

        if not with_loss:
            return

        @pl.when(t == steps - 1)
        def _():
            total = jnp.sum(acc_ref[...], axis=-1, keepdims=True) * (0.5 / D_MODEL)
            loss_ref[...] = jnp.broadcast_to(total, loss_ref.shape)

    tile = pl.BlockSpec((tq, D_MODEL), lambda t: (t, 0))
    blocks = [_nbytes((tq, PB_WIDTH), BF16), _nbytes((BLOCK, 2 * KV_WIDTH), BF16),
              _nbytes((B_HEADS, BLOCK, BLOCK), F32), 2 * _nbytes((tq, B_WIDTH), BF16),
              2 * _nbytes((half, D_MODEL), BF16), 3 * _nbytes((tq, D_MODEL), F32), _nbytes((tq, D_MODEL), BF16),
              _nbytes((tq, B_HEADS * BLOCK), BF16), _nbytes((tq, LANES), F32)]
    scratch = _nbytes((tq + BLOCK, 2 * KV_WIDTH), BF16)
    in_specs = [pl.BlockSpec(memory_space=pltpu.SMEM),
                pl.BlockSpec((tq, PB_WIDTH), lambda t: (t, 0)),
                pl.BlockSpec((BLOCK, 2 * KV_WIDTH),
                             lambda t: (jnp.maximum(t * per_tile - 1, 0), K_OFF // (2 * KV_WIDTH))),
                pl.BlockSpec((B_HEADS, BLOCK, BLOCK), lambda t: (0, 0, 0)),
                pl.BlockSpec((tq, half), lambda t: (t, 0)),
                pl.BlockSpec((half, D_MODEL), lambda t: (0, 0)),
                pl.BlockSpec((half, D_MODEL), lambda t: (1, 0)),
                tile,
                pl.BlockSpec((None, 1, D_MODEL), lambda t: (layer, 0, 0))]
    out_specs = [pl.BlockSpec((tq, B_WIDTH), lambda t: (t, 0)), pl.BlockSpec((tq, B_HEADS * BLOCK), lambda t: (t, 0)),
                 pl.BlockSpec((tq, LANES), lambda t: (t, 0)), tile, tile]
    out_shape = [pltpu.HBM((tokens, B_WIDTH), BF16), pltpu.HBM((tokens, B_HEADS * BLOCK), BF16),
                 pltpu.HBM((tokens, LANES), F32), pltpu.HBM((tokens, D_MODEL), BF16),
                 pltpu.HBM((tokens, D_MODEL), F32)]
    scratch_shapes = [pltpu.VMEM((tq + BLOCK, 2 * KV_WIDTH), BF16)]
    operands = [sinks, _hbm(pb), _hbm(pb), _hbm(bias), _hbm(ya), _hbm(wo), _hbm(wo), _hbm(x2), post_g3]
    if with_loss:
        in_specs.append(tile)
        out_specs.append(pl.BlockSpec((1, LANES), lambda t: (0, 0)))
        out_shape.append(pltpu.HBM((1, LANES), F32))
        scratch_shapes.append(pltpu.VMEM((1, D_MODEL), F32))
        operands.append(_hbm(target))
    return pl.pallas_call(
        body, name=f"attn_outproj_fwd_{layer}", grid=(steps,),
        in_specs=in_specs, out_specs=out_specs, out_shape=out_shape, scratch_shapes=scratch_shapes,
        compiler_params=pltpu.CompilerParams(dimension_semantics=("arbitrary" if with_loss else "parallel",),
                                             vmem_limit_bytes=_vmem_limit(blocks, scratch)),
    )(*operands)


def _outproj_bwd(g, y, ya, yb, wo, post_g3, layer):
    tokens = g.shape[0]
    tm = _token_tile(tokens, cap=1024)
    half = D_MODEL // 2
    steps = tokens // tm

    def body(g_ref, y_ref, ya_ref, yb_ref, woa_ref, wob_ref, pg_ref, dya_ref, dyb_ref, dwo16_ref, dpg_ref, dwo_ref):
        @pl.when(pl.program_id(0) == 0)
        def _():
            dwo_ref[...] = jnp.zeros_like(dwo_ref)
            dpg_ref[...] = jnp.zeros_like(dpg_ref)

        gv = g_ref[...]
        yf = y_ref[...].astype(F32)
        r = lax.rsqrt(jnp.mean(yf * yf, axis=-1, keepdims=True) + NORM_EPS)
        yhat = yf * r
        gg = gv * pg_ref[...]
        dy = r * (gg - yhat * jnp.mean(gg * yhat, axis=-1, keepdims=True))
        dpg_ref[...] += jnp.sum(gv * yhat, axis=0, keepdims=True)
        dyb16 = dy.astype(BF16)
        dya_ref[...] = _mm_nt(dyb16, woa_ref[...]).astype(BF16)
        dyb_ref[...] = _mm_nt(dyb16, wob_ref[...]).astype(BF16)
        dwo_ref[0:half, :] += _mm_tn(ya_ref[...], dyb16)
        dwo_ref[half:D_MODEL, :] += _mm_tn(yb_ref[...], dyb16)

        @pl.when(pl.program_id(0) == steps - 1)
        def _():
            dwo16_ref[...] = dwo_ref[...].astype(BF16)

    blocks = [_nbytes((tm, D_MODEL), F32), _nbytes((tm, D_MODEL), BF16), 4 * _nbytes((tm, half), BF16),
              2 * _nbytes((half, D_MODEL), BF16), _nbytes((D_MODEL, D_MODEL), BF16)]
    return pl.pallas_call(
        body, name=f"outproj_bwd_{layer}", grid=(tokens // tm,),
        in_specs=[pl.BlockSpec((tm, D_MODEL), lambda i: (i, 0)),
                  pl.BlockSpec((tm, D_MODEL), lambda i: (i, 0)),
                  pl.BlockSpec((tm, half), lambda i: (i, 0)),
                  pl.BlockSpec((tm, half), lambda i: (i, 0)),
                  pl.BlockSpec((half, D_MODEL), lambda i: (0, 0)),
                  pl.BlockSpec((half, D_MODEL), lambda i: (1, 0)),
                  pl.BlockSpec((None, 1, D_MODEL), lambda i: (layer, 0, 0))],
        out_specs=[pl.BlockSpec((tm, half), lambda i: (i, 0)),
                   pl.BlockSpec((tm, half), lambda i: (i, 0)),
                   pl.BlockSpec((D_MODEL, D_MODEL), lambda i: (0, 0)),
                   pl.BlockSpec((1, D_MODEL), lambda i: (0, 0))],
        out_shape=[pltpu.HBM((tokens,half), BF16),
                   pltpu.HBM((tokens,half), BF16),
                   pltpu.HBM((D_MODEL,D_MODEL), BF16),
                   pltpu.HBM((1,D_MODEL), F32)],
        scratch_shapes=[pltpu.VMEM((D_MODEL, D_MODEL), F32)],
        compiler_params=pltpu.CompilerParams(dimension_semantics=("arbitrary",),
                                             vmem_limit_bytes=_vmem_limit(blocks, _nbytes((D_MODEL, D_MODEL), F32))),
    )(_hbm(g), _hbm(y), _hbm(ya), _hbm(yb), _hbm(wo), _hbm(wo), post_g3)


def _gmlp_bwd_dw(pa, kept, rstd_w, dya, h, dqz, dkv, ln_g3, ln_b3, wmt, layer):
    tokens = pa.shape[0]
    tc = _token_tile(tokens)
    steps = tokens // tc

    def body(pz_ref, kept_ref, rstd_ref, dya_ref, h_ref, dqz_ref, dkv_ref, lg_ref, lb_ref, wmt_ref,
             da_ref, dw_ref, dws_ref, dbs_ref, dlg_ref, dlb_ref, acc_ref, dbsb_ref):
        i = pl.program_id(0)

        @pl.when(i == 0)
        def _():
            acc_ref[...] = jnp.zeros_like(acc_ref)
            dws_ref[...] = jnp.zeros_like(dws_ref)
            dlg_ref[...] = jnp.zeros_like(dlg_ref)
            dlb_ref[...] = jnp.zeros_like(dlb_ref)
            dbsb_ref[...] = jnp.zeros_like(dbsb_ref)

        ln_g = lg_ref[...]
        low = lax.broadcasted_iota(jnp.int32, (CHUNK, LANES), 1) < HEAD_DIM
        hv = h_ref[...]
        acc_ref[:, PA_WIDTH:PA_WIDTH + B_WIDTH] += _mm_tn(hv, dqz_ref[:, 0:B_WIDTH])
        acc_ref[:, PA_WIDTH + K_OFF:PA_WIDTH + BZ_OFF] += _mm_tn(hv, dkv_ref[...])
        acc_ref[:, PA_WIDTH + BZ_OFF:IN_WIDTH] += _mm_tn(hv, dqz_ref[:, B_WIDTH:2 * B_WIDTH])

        for ci in range(tc // CHUNK):
            rows = slice(ci * CHUNK, (ci + 1) * CHUNK)
            u, gu, gv, vhat, mixed = (kept_ref[rows, A_WIDTH * k:A_WIDTH * (k + 1)].astype(F32) for k in range(5))
            rstd = jnp.concatenate([rstd_ref[rows, :]] * (A_WIDTH // LANES), axis=1)
            vnb = (vhat * ln_g + lb_ref[...]).astype(BF16)
            pz = pz_ref[rows, :].astype(F32)
            sg = _sigmoid(pz)
            dy = dya_ref[rows, :].astype(F32)
            sz = pz * sg
            dy_sz = dy * sz
            du = dy_sz * mixed
            dmixed = dy_sz * u
            dz = dy * (u * mixed) * (sg + sz * (1.0 - sg))
            dbsb_ref[...] += dmixed
            dmb = dmixed.astype(BF16)
            zero = jnp.zeros((CHUNK, LANES), BF16)
            parts = []
            for p in range(A_GROUPS // 2):
                dmp = dmb[:, LANES * p:LANES * (p + 1)]
                vp = vnb[:, LANES * p:LANES * (p + 1)]
                parts.append(jnp.where(low, _mm(wmt_ref[2 * p], dmp), _mm(wmt_ref[2 * p + 1], dmp)))
                dws_ref[2 * p] += _mm_nt(jnp.where(low, dmp, zero), vp)
                dws_ref[2 * p + 1] += _mm_nt(jnp.where(low, zero, dmp), vp)
            dvn = jnp.concatenate(parts, axis=1)
            dlg_ref[...] += jnp.sum(dvn * vhat, axis=0, keepdims=True)
            dlb_ref[...] += jnp.sum(dvn, axis=0, keepdims=True)
            dvh = dvn * ln_g
            dvv = rstd * (dvh - jnp.mean(dvh, axis=-1, keepdims=True)
                          - vhat * jnp.mean(dvh * vhat, axis=-1, keepdims=True))
            da_ref[rows, 0:A_WIDTH] = (du * gu).astype(BF16)
            da_ref[rows, A_WIDTH:2 * A_WIDTH] = (dvv * gv).astype(BF16)
            da_ref[rows, 2 * A_WIDTH:3 * A_WIDTH] = dz.astype(BF16)

        acc_ref[:, 0:PA_WIDTH] += _mm_tn(hv, da_ref[...])

        @pl.when(i == steps - 1)
        def _():
            for c in range(IN_WIDTH // LANES):
                cols = slice(LANES * c, LANES * (c + 1))
                dw_ref[cols, :] = acc_ref[:, cols].T.astype(BF16)
            causal = (lax.broadcasted_iota(jnp.int32, (CHUNK, CHUNK), 0)
                      >= lax.broadcasted_iota(jnp.int32, (CHUNK, CHUNK), 1))
            for h in range(A_GROUPS):
                dws_ref[h] = jnp.where(causal, dws_ref[h], 0.0)
            acc = dbsb_ref[...]
            lane_full = lax.broadcasted_iota(jnp.int32, (CHUNK, A_WIDTH), 1)
            lane_out = lax.broadcasted_iota(jnp.int32, (CHUNK, LANES), 1)
            out = jnp.zeros((CHUNK, LANES), F32)
            for h in range(A_GROUPS):
                in_group = jnp.logical_and(lane_full >= HEAD_DIM * h, lane_full < HEAD_DIM * (h + 1))
                s = jnp.sum(jnp.where(in_group, acc, 0.0), axis=-1, keepdims=True)
                out = jnp.where(lane_out == h, s, out)
            dbs_ref[...] = out

    blocks = [_nbytes((tc, KEPT_WIDTH), BF16), 2 * _nbytes((tc, A_WIDTH), BF16), _nbytes((A_GROUPS, CHUNK, CHUNK), BF16),
              _nbytes((tc, LANES), F32), _nbytes((tc, PA_WIDTH), BF16), _nbytes((A_GROUPS, CHUNK, CHUNK), F32),
              _nbytes((CHUNK, LANES), F32), _nbytes((tc, D_MODEL), BF16), _nbytes((tc, 2 * B_WIDTH), BF16),
              _nbytes((tc, 2 * KV_WIDTH), BF16), _nbytes((D_MODEL, IN_WIDTH), BF16)]
    scratch = _nbytes((D_MODEL, IN_WIDTH), F32) + _nbytes((CHUNK, A_WIDTH), F32)
    in_specs = [pl.BlockSpec((tc, A_WIDTH), lambda i: (i, 2)),
                pl.BlockSpec((tc, KEPT_WIDTH), lambda i: (i, 0)),
                pl.BlockSpec((tc, LANES), lambda i: (i, 0)),
                pl.BlockSpec((tc, A_WIDTH), lambda i: (i, 0)),
                pl.BlockSpec((tc, D_MODEL), lambda i: (i, 0)),
                pl.BlockSpec((tc, 2 * B_WIDTH), lambda i: (i, 0)),
                pl.BlockSpec((pl.Element(tc), pl.Element(2 * KV_WIDTH)),
                             lambda i: (pl.multiple_of(i * tc + BLOCK, BLOCK), 0)),
                pl.BlockSpec((None, 1, A_WIDTH), lambda i: (layer, 0, 0)),
                pl.BlockSpec((None, 1, A_WIDTH), lambda i: (layer, 0, 0)),
                pl.BlockSpec((None, A_GROUPS, CHUNK, CHUNK), lambda i: (layer, 0, 0, 0))]
    operands = [_hbm(pa), _hbm(kept), _hbm(rstd_w), _hbm(dya), _hbm(h), _hbm(dqz), _hbm(dkv), ln_g3, ln_b3, wmt]
    out_specs = [pl.BlockSpec((tc, PA_WIDTH), lambda i: (i, 0)),
                 pl.BlockSpec((IN_WIDTH, D_MODEL), lambda i: (0, 0)),
                 pl.BlockSpec((A_GROUPS, CHUNK, CHUNK), lambda i: (0, 0, 0)),
                 pl.BlockSpec((CHUNK, LANES), lambda i: (0, 0)),
                 pl.BlockSpec((1, A_WIDTH), lambda i: (0, 0)),
                 pl.BlockSpec((1, A_WIDTH), lambda i: (0, 0))]
    out_shape = [pltpu.HBM((tokens, PA_WIDTH), BF16),
                 pltpu.HBM((IN_WIDTH, D_MODEL), BF16),
                 pltpu.HBM((A_GROUPS, CHUNK, CHUNK), F32),
                 pltpu.HBM((CHUNK, LANES), F32),
                 pltpu.HBM((1, A_WIDTH), F32),
                 pltpu.HBM((1, A_WIDTH), F32)]
    scratch_shapes = [pltpu.VMEM((D_MODEL, IN_WIDTH), F32), pltpu.VMEM((CHUNK, A_WIDTH), F32)]
    return pl.pallas_call(
        body, name=f"gmlp_bwd_dw_{layer}", grid=(steps,),
        in_specs=in_specs, out_specs=out_specs, out_shape=out_shape, scratch_shapes=scratch_shapes,
        compiler_params=pltpu.CompilerParams(dimension_semantics=("arbitrary",),
                                             vmem_limit_bytes=_vmem_limit(blocks, scratch)),
    )(*operands)


def _attn_bwd(pb, dyb, probs_w, sink_w, dbias_in, layer):
    tokens = pb.shape[0]
    qz_width = 2 * B_WIDTH
    tq = _token_tile(tokens)
    per_tile = tq // BLOCK
    nt = tokens // tq

    def body(cur_ref, prev_ref, dyb_ref, pw_ref, spw_ref, dbias_in_ref, dqz_ref, dkv_ref, dbias_ref, dsink_ref,
             kv_ref, acc_ref):
        t = pl.program_id(0)

        @pl.when(t == 0)
        def _():
            dbias_ref[...] = dbias_in_ref[...]
            dsink_ref[...] = jnp.zeros_like(dsink_ref)
            acc_ref[0:BLOCK, :] = jnp.zeros((BLOCK, 2 * KV_WIDTH), F32)

        @pl.when(t < nt)
        def _():
            acc_ref[BLOCK:, :] = jnp.zeros((tq, 2 * KV_WIDTH), F32)
            _fill_kv(kv_ref, prev_ref, cur_ref)
            tri = _band_masks()
            low = lax.broadcasted_iota(jnp.int32, (BLOCK, LANES), 1) < HEAD_DIM
            low_kv = lax.broadcasted_iota(jnp.int32, (2 * BLOCK, LANES), 1) < HEAD_DIM

            head_lane = lax.broadcasted_iota(jnp.int32, (BLOCK, LANES), 1)

            def block(i, carry):
                start = i * BLOCK
                rows = pl.ds(start, BLOCK)
                kv = kv_ref[pl.ds(start, 2 * BLOCK), :]
                k_ops = _kv_variants(kv[:, 0:KV_WIDTH])
                v_ops = _kv_variants(kv[:, KV_WIDTH:2 * KV_WIDTH])
                probs, p16, qs, dos, delta, ds16 = {}, {}, {}, {}, {}, {}
                for hd in range(B_HEADS):
                    kept_p = pw_ref[rows, LANES * hd:LANES * (hd + 1)]
                    probs[hd] = kept_p.astype(F32)
                    p16[hd] = _unwrap16(kept_p, tri)
                for kvh in range(2):
                    qs[kvh] = jnp.concatenate(
                        [cur_ref[rows, LANES * p:LANES * (p + 1)] for p in (2 * kvh, 2 * kvh + 1)], axis=0)
                for kvh in range(2):
                    pairs = (2 * kvh, 2 * kvh + 1)
                    out = jnp.zeros((2 * BLOCK, LANES), F32)
                    for j in range(2):
                        out = out + _mm(jnp.concatenate([p16[2 * p + j] for p in pairs], axis=0), v_ops[kvh][j])
                    d_outs = []
                    for r, p in enumerate(pairs):
                        bz = cur_ref[rows, BZ_OFF + LANES * p:BZ_OFF + LANES * (p + 1)].astype(F32)
                        sg = _sigmoid(bz)
                        dyp = dyb_ref[rows, LANES * p:LANES * (p + 1)].astype(F32)
                        out_p = out[BLOCK * r:BLOCK * (r + 1)]
                        d_out = dyp * (bz * sg)
                        dqz_ref[rows, B_WIDTH + LANES * p:B_WIDTH + LANES * (p + 1)] = (
                            dyp * out_p * (sg * (1.0 + bz * (1.0 - sg)))).astype(BF16)
                        dod = d_out * out_p
                        delta[2 * p] = jnp.sum(jnp.where(low, dod, 0.0), axis=-1, keepdims=True)
                        delta[2 * p + 1] = jnp.sum(jnp.where(low, 0.0, dod), axis=-1, keepdims=True)
                        d_outs.append(d_out.astype(BF16))
                    dos[kvh] = jnp.concatenate(d_outs, axis=0)
                delta_cols = jnp.zeros((BLOCK, LANES), F32)
                for hd in range(B_HEADS):
                    delta_cols = jnp.where(head_lane == hd, delta[hd], delta_cols)
                dsink_ref[0:1, :] += jnp.sum(-(spw_ref[rows, :] * delta_cols), axis=0, keepdims=True)
                for kvh in range(2):
                    pairs = (2 * kvh, 2 * kvh + 1)
                    for j in range(2):
                        dpf = _mm_nt(dos[kvh], v_ops[kvh][j])
                        for r, p in enumerate(pairs):
                            hd = 2 * p + j
                            ds = probs[hd] * (_wrap(dpf[BLOCK * r:BLOCK * (r + 1)], tri) - delta[hd])
                            dbias_ref[hd] += ds
                            ds16[hd] = _unwrap16(ds, tri)
                dk_acc = [[None, None], [None, None]]
                dv_acc = [[None, None], [None, None]]
                for kvh in range(2):
                    pairs = (2 * kvh, 2 * kvh + 1)
                    dq = jnp.zeros((2 * BLOCK, LANES), F32)
                    for j in range(2):
                        dss = jnp.concatenate([ds16[2 * p + j] for p in pairs], axis=0)
                        pss = jnp.concatenate([p16[2 * p + j] for p in pairs], axis=0)
                        dq = dq + _mm(dss, k_ops[kvh][j])
                        dk_acc[kvh][j] = _mm_tn(dss, qs[kvh])
                        dv_acc[kvh][j] = _mm_tn(pss, dos[kvh])
                    for r, p in enumerate(pairs):
                        dqz_ref[rows, LANES * p:LANES * (p + 1)] = (dq[BLOCK * r:BLOCK * (r + 1)] * SCALE).astype(BF16)

                def fold(acc):
                    return jnp.where(low_kv,
                                     acc[0][0] + pltpu.roll(acc[0][1], HEAD_DIM, 1),
                                     pltpu.roll(acc[1][0], HEAD_DIM, 1) + acc[1][1])

                acc_ref[pl.ds(start, 2 * BLOCK), :] += jnp.concatenate(
                    [fold(dk_acc) * SCALE, fold(dv_acc)], axis=1)
                return carry

            for i in range(per_tile):
                block(i, 0)
            dkv_ref[...] = acc_ref[0:tq, :].astype(BF16)
            acc_ref[0:BLOCK, :] = acc_ref[tq:tq + BLOCK, :]

        @pl.when(t == nt)
        def _():
            dkv_ref[0:BLOCK, :] = acc_ref[0:BLOCK, :].astype(BF16)
            dkv_ref[BLOCK:, :] = jnp.zeros((tq - BLOCK, 2 * KV_WIDTH), BF16)

    def cur_map(t):
        return (jnp.minimum(t, nt - 1), 0)

    def prev_map(t):
        return (jnp.maximum(jnp.minimum(t, nt - 1) * per_tile - 1, 0), K_OFF // (2 * KV_WIDTH))

    blocks = [_nbytes((tq, PB_WIDTH), BF16), _nbytes((BLOCK, 2 * KV_WIDTH), BF16), _nbytes((tq, B_WIDTH), BF16),
              2 * _nbytes((B_HEADS, BLOCK, BLOCK), F32), _nbytes((tq, qz_width), BF16),
              _nbytes((tq, 2 * KV_WIDTH), BF16), _nbytes((B_HEADS, LANES), F32),
              _nbytes((tq, B_HEADS * BLOCK), BF16), _nbytes((tq, LANES), F32)]
    scratch = _nbytes((tq + BLOCK, 2 * KV_WIDTH), BF16) + _nbytes((tq + BLOCK, 2 * KV_WIDTH), F32)
    return pl.pallas_call(
        body, name=f"attn_bwd_{layer}", grid=(nt + 1,),
        in_specs=[pl.BlockSpec((tq, PB_WIDTH), cur_map),
                  pl.BlockSpec((BLOCK, 2 * KV_WIDTH), prev_map),
                  pl.BlockSpec((tq, B_WIDTH), cur_map),
                  pl.BlockSpec((tq, B_HEADS * BLOCK), cur_map),
                  pl.BlockSpec((tq, LANES), cur_map),
                  pl.BlockSpec((B_HEADS, BLOCK, BLOCK), lambda t: (0, 0, 0))],
        out_specs=[pl.BlockSpec((tq, qz_width), cur_map),
                   pl.BlockSpec((tq, 2 * KV_WIDTH), lambda t: (t, 0)),
                   pl.BlockSpec((B_HEADS, BLOCK, BLOCK), lambda t: (0, 0, 0)),
                   pl.BlockSpec((B_HEADS, LANES), lambda t: (0, 0))],
        out_shape=[pltpu.HBM((tokens, qz_width), BF16),
                   pltpu.HBM((tokens + tq, 2 * KV_WIDTH), BF16),
                   pltpu.HBM((B_HEADS, BLOCK, BLOCK), F32),
                   pltpu.HBM((B_HEADS, LANES), F32)],
        scratch_shapes=[pltpu.VMEM((tq + BLOCK, 2 * KV_WIDTH), BF16),
                        pltpu.VMEM((tq + BLOCK, 2 * KV_WIDTH), F32)],
        compiler_params=pltpu.CompilerParams(dimension_semantics=("arbitrary",),
                                             vmem_limit_bytes=_vmem_limit(blocks, scratch)),
    )(_hbm(pb), _hbm(pb), _hbm(dyb), _hbm(probs_w), _hbm(sink_w), _hbm(dbias_in))


def _inproj_bwd_dx(da, dqz, dkv, w, x2, g, pre_g3, layer):
    tokens = x2.shape[0]
    tm = _token_tile(tokens)

    def body(da_ref, dqz_ref, dkv_ref, w_ref, x_ref, g_ref, pg_ref, gn_ref, dpg_ref):
        @pl.when(pl.program_id(0) == 0)
        def _():
            dpg_ref[...] = jnp.zeros_like(dpg_ref)

        dh = _mm(da_ref[...], w_ref[0:PA_WIDTH, :])
        dh += _mm(dqz_ref[:, 0:B_WIDTH], w_ref[PA_WIDTH:PA_WIDTH + B_WIDTH, :])
        dh += _mm(dkv_ref[...], w_ref[PA_WIDTH + K_OFF:PA_WIDTH + BZ_OFF, :])
        dh += _mm(dqz_ref[:, B_WIDTH:2 * B_WIDTH], w_ref[PA_WIDTH + BZ_OFF:IN_WIDTH, :])
        x = x_ref[...]
        r = lax.rsqrt(jnp.mean(x * x, axis=-1, keepdims=True) + NORM_EPS)
        xhat = x * r
        dhg = dh * pg_ref[...]
        dpg_ref[...] += jnp.sum(dh * xhat, axis=0, keepdims=True)
        gn_ref[...] = g_ref[...] + r * (dhg - xhat * jnp.mean(dhg * xhat, axis=-1, keepdims=True))

    blocks = [_nbytes((tm, PA_WIDTH), BF16), _nbytes((tm, 2 * B_WIDTH), BF16), _nbytes((tm, 2 * KV_WIDTH), BF16),
              _nbytes((D_MODEL, IN_WIDTH), BF16), 3 * _nbytes((tm, D_MODEL), F32)]
    return pl.pallas_call(
        body, name=f"inproj_bwd_dx_{layer}", grid=(tokens // tm,),
        in_specs=[pl.BlockSpec((tm, PA_WIDTH), lambda i: (i, 0)),
                  pl.BlockSpec((tm, 2 * B_WIDTH), lambda i: (i, 0)),
                  pl.BlockSpec((pl.Element(tm), pl.Element(2 * KV_WIDTH)), lambda i: (pl.multiple_of(i * tm + BLOCK, BLOCK), 0)),
                  pl.BlockSpec((IN_WIDTH, D_MODEL), lambda i: (0, 0)),
                  pl.BlockSpec((tm, D_MODEL), lambda i: (i, 0)),
                  pl.BlockSpec((tm, D_MODEL), lambda i: (i, 0)),
                  pl.BlockSpec((None, 1, D_MODEL), lambda i: (layer, 0, 0))],
        out_specs=[pl.BlockSpec((tm, D_MODEL), lambda i: (i, 0)),
                   pl.BlockSpec((1, D_MODEL), lambda i: (0, 0))],
        out_shape=[pltpu.HBM((tokens,D_MODEL), F32),
                   pltpu.HBM((1,D_MODEL), F32)],
        compiler_params=pltpu.CompilerParams(dimension_semantics=("arbitrary",),
                                             vmem_limit_bytes=_vmem_limit(blocks)),
    )(_hbm(da), _hbm(dqz), _hbm(dkv), _hbm(w), _hbm(x2), _hbm(g), pre_g3)


def _rel_bias_grad(dbias, col_bucket, flip):
    def body(db_ref, cb_ref, flip_ref, out_ref):
        cb = cb_ref[...]
        anti = flip_ref[...]
        sums = []
        for h in range(B_HEADS):
            x = db_ref[h]
            hi = x.astype(BF16)
            rest = x - hi.astype(F32)
            mid = rest.astype(BF16)
            low = (rest - mid.astype(F32)).astype(BF16)
            reversed_x = _mm(hi, anti) + _mm(mid, anti) + _mm(low, anti)
            rolled = pltpu.roll(reversed_x, 0, 1, stride=1, stride_axis=0)
            sums.append(jnp.sum(rolled, axis=0, keepdims=True))
        per_dist = jnp.concatenate(sums, axis=0)
        lane = lax.broadcasted_iota(jnp.int32, (B_HEADS, LANES), 1)
        out = jnp.zeros((B_HEADS, LANES), F32)
        for b in range(REL_BUCKETS):
            s = jnp.sum(jnp.where(cb == b, per_dist, 0.0), axis=-1, keepdims=True)
            out = jnp.where(lane == b, s, out)
        out_ref[...] = out

    vm = pl.BlockSpec(memory_space=pltpu.VMEM)
    return pl.pallas_call(
        body, name="rel_bias_grad",
        out_shape=jax.ShapeDtypeStruct((B_HEADS, LANES), F32),
        in_specs=[vm, vm, vm], out_specs=vm,
    )(dbias, col_bucket, flip)


def _reduce_adamw(slots, w, m, v, name):
    rows, cols = w.shape
    tr = _row_tile(rows)
    c1 = 1.0 / (1.0 - ADAM_B1 ** ADAM_STEP)
    c2 = 1.0 / (1.0 - ADAM_B2 ** ADAM_STEP)

    def body(s_ref, w_ref, m_ref, v_ref, g_ref, d_ref, nm_ref, nv_ref):
        g = s_ref[0].astype(F32)
        for i in range(1, N_DEV):
            g = g + s_ref[i].astype(F32)
        nm = ADAM_B1 * m_ref[...] + (1.0 - ADAM_B1) * g
        nv = ADAM_B2 * v_ref[...] + (1.0 - ADAM_B2) * (g * g)
        g_ref[...] = g
        nm_ref[...] = nm
        nv_ref[...] = nv
        d_ref[...] = -ADAM_LR * ((nm * c1) / (jnp.sqrt(nv * c2) + ADAM_EPS) + ADAM_WD * w_ref[...])

    blocks = [_nbytes((N_DEV, tr, cols), slots.dtype), 7 * _nbytes((tr, cols), F32)]
    tile = pl.BlockSpec((tr, cols), lambda i: (i, 0))
    return pl.pallas_call(
        body, name=name, grid=(rows // tr,),
        in_specs=[pl.BlockSpec((N_DEV, tr, cols), lambda i: (0, i, 0)), tile, tile, tile],
        out_specs=[tile] * 4,
        out_shape=[pltpu.HBM((rows, cols), F32)] * 4,
        compiler_params=pltpu.CompilerParams(dimension_semantics=("parallel",),
                                             vmem_limit_bytes=_vmem_limit(blocks)),
    )(_hbm(slots), _hbm(w), _hbm(m), _hbm(v))


_SMALL = ("pre_norm_g", "post_norm_g", "ln_v_g", "ln_v_b", "b_spatial", "sinks", "rel_bias", "loss")


def _pack_small(parts):
    slabs = []
    for name in _SMALL:
        flat = parts[name].astype(F32).reshape(-1)
        pad = (-flat.shape[0]) % (8 * LANES)
        slabs.append(jnp.pad(flat, (0, pad)).reshape(-1, LANES))
    return jnp.concatenate(slabs, axis=0)


def _unpack_small(slab, shapes):
    out, row = {}, 0
    for name in _SMALL:
        size = int(np.prod(shapes[name]))
        rows = -(-size // (8 * LANES)) * 8
        out[name] = slab[row:row + rows].reshape(-1)[:size].reshape(shapes[name])
        row += rows
    return out


def kernel(x, pre_norm_g, w_in, ln_v_g, ln_v_b, w_spatial, b_spatial, sinks, rel_bias, w_out, post_norm_g, loss_target, m_pre_norm_g, m_w_in, m_ln_v_g, m_ln_v_b, m_w_spatial, m_b_spatial, m_sinks, m_rel_bias, m_w_out, m_post_norm_g, v_pre_norm_g, v_w_in, v_ln_v_g, v_ln_v_b, v_w_spatial, v_b_spatial, v_sinks, v_rel_bias, v_w_out, v_post_norm_g):
    b_loc, seq, _ = x.shape
    tokens = b_loc * seq
    depth = w_in.shape[0]
    in_shard = w_in.shape[2]
    out_shard = w_out.shape[1]
    assert in_shard * N_DEV == IN_WIDTH and out_shard * N_DEV == D_MODEL and seq % BLOCK == 0

    me = _slot(lax.axis_index("x"), lax.axis_index("y"), lax.axis_index("c"))
    w_in_t, m_w_in_t, v_w_in_t = (jnp.swapaxes(a, 1, 2) for a in (w_in, m_w_in, v_w_in))
    w_in16, w_out16 = w_in_t.astype(BF16), w_out.astype(BF16)

    def gather_start(layer, not_before, pre_g):
        zones = [lax.empty((N_DEV, in_shard, D_MODEL), BF16), lax.empty((N_DEV, out_shard, D_MODEL), BF16)]
        sems, sends, zones, pre_g = _exchange_start(
            [w_in16[layer], w_out16[layer]], zones, [0, 0], f"weights_send_{layer}", pre_g, after=not_before)
        return (sems, sends, [0, 0]), zones, pre_g

    def full_weights(gin, gout):
        return gin.reshape(IN_WIDTH, D_MODEL), gout.reshape(D_MODEL, D_MODEL)

    def gather_wait(layer, started, zones, after):
        gin, gout = _exchange_wait([started], zones, after, f"weights_wait_{layer}")
        gin = lax.dynamic_update_index_in_dim(gin, w_in16[layer], me, 0)
        gout = lax.dynamic_update_index_in_dim(gout, w_out16[layer], me, 0)
        return full_weights(gin, gout)

    causal = jnp.tril(jnp.ones((CHUNK, CHUNK), dtype=bool))
    wm = jnp.where(causal, w_spatial, 0.0).astype(BF16)
    wmt = _hbm(jnp.swapaxes(wm, -1, -2))
    wm = _hbm(wm)
    bsb = _hbm(jnp.repeat(jnp.swapaxes(b_spatial, -1, -2), HEAD_DIM, axis=-1))
    post_g3 = _hbm(post_norm_g.reshape(depth, 1, D_MODEL))
    ln_g3 = _hbm(ln_v_g.reshape(depth, 1, A_WIDTH))
    ln_b3 = _hbm(ln_v_b.reshape(depth, 1, A_WIDTH))
    bucket = jnp.asarray(_bucket_table())
    bias = _bias_table(rel_bias, bucket)

    xs, saved, weights = [x.reshape(tokens, D_MODEL)], [], []
    pending = None
    for layer in range(depth):
        pre_g_fwd = pre_norm_g.reshape(depth, 1, D_MODEL)
        if layer == 0:
            (gin,) = _all_gather([w_in16[0]], "weights_gather_0")
            w = gin.reshape(IN_WIDTH, D_MODEL)
            out0 = _exchange_start([w_out16[0]], [lax.empty((N_DEV, out_shard, D_MODEL), BF16)], [0],
                                   "weights_send_out_0", pre_g_fwd, after=w)
            pre_g_fwd = out0[3]
        else:
            w, wo = gather_wait(layer, pending[0], pending[1], xs[-1])
        if layer + 1 < depth:
            pending = gather_start(layer + 1, w, pre_g_fwd)
            pre_g_fwd = pending[2]
        h, pa, pb, ya, kept, rstd_w = _inproj_gmlp_fwd(xs[-1], _hbm(pre_g_fwd), w, ln_g3, ln_b3, wm, bsb, layer)
        if layer == 0:
            (gout,) = _exchange_wait([(out0[0], out0[1], [0])], out0[2], h, "weights_wait_out_0")
            wo = lax.dynamic_update_index_in_dim(gout, w_out16[0], me, 0).reshape(D_MODEL, D_MODEL)
        if layer + 1 < depth:
            yb, pw, spw, y, x_next = _attn_outproj_fwd(pb, bias, sinks, ya, wo, xs[-1], post_g3, b_loc, layer)
            xs.append(x_next)
        else:
            yb, pw, spw, y, g, loss_part = _attn_outproj_fwd(pb, bias, sinks, ya, wo, xs[-1], post_g3, b_loc, layer,
                                                    target=loss_target.reshape(tokens, D_MODEL))
        saved.append((h, pa, pb, ya, yb, y, pw, spw, kept, rstd_w))
        weights.append((w, wo))

    grads = {name: [None] * depth for name in ("pre_norm_g", "post_norm_g", "ln_v_g", "ln_v_b", "b_spatial", "sinks")}
    zone_in = lax.empty((N_DEV, depth * in_shard, D_MODEL), BF16)
    zone_out = lax.empty((N_DEV, depth * out_shard, D_MODEL), BF16)
    zone_ws = lax.empty((N_DEV, depth * A_GROUPS * CHUNK, CHUNK), F32)
    started_in, started_out, own_in, own_out = [], [], [None] * depth, [None] * depth
    own_ws = [None] * depth
    dbias = jnp.zeros((B_HEADS, BLOCK, BLOCK), F32)
    for layer in reversed(range(depth)):
        h, pa, pb, ya, yb, y, pw, spw, kept, rstd_w = saved[layer]
        w, wo = weights[layer]
        dya, dyb, dwo, dpost = _outproj_bwd(g, y, ya, yb, wo, post_g3, layer)
        send_out = dwo.reshape(N_DEV, out_shard, D_MODEL)
        own_out[layer] = lax.dynamic_index_in_dim(send_out, me, 0, keepdims=False)
        sems, sends, (zone_out,), ln_g_bwd = _exchange_start(
            [send_out], [zone_out], [layer * out_shard], f"grads_send_out_{layer}", ln_v_g.reshape(depth, 1, A_WIDTH))
        started_out.append((sems, sends, [layer * out_shard]))
        dqz, dkv, dbias, dsink = _attn_bwd(pb, dyb, pw, spw, dbias, layer)
        da, dw, dws, dbs, dlg, dlb = _gmlp_bwd_dw(pa, kept, rstd_w, dya, h, dqz, dkv, _hbm(ln_g_bwd), ln_b3, wmt, layer)
        own_ws[layer] = dws.reshape(A_GROUPS * CHUNK, CHUNK)
        send_in = dw.reshape(N_DEV, in_shard, D_MODEL)
        own_in[layer] = lax.dynamic_index_in_dim(send_in, me, 0, keepdims=False)
        row_offs = [layer * A_GROUPS * CHUNK, layer * in_shard]
        sems, sends, (zone_ws, zone_in), pre_g_bwd = _exchange_start(
            [own_ws[layer], send_in], [zone_ws, zone_in], row_offs, f"grads_send_in_{layer}",
            pre_norm_g.reshape(depth, 1, D_MODEL))
        started_in.append((sems, sends, row_offs))
        g, dpre = _inproj_bwd_dx(da, dqz, dkv, w, xs[layer], g, _hbm(pre_g_bwd), layer)
        grads["b_spatial"][layer] = dbs[:, :A_GROUPS].T
        grads["ln_v_g"][layer] = dlg[0]
        grads["ln_v_b"][layer] = dlb[0]
        grads["sinks"][layer] = dsink[0, :B_HEADS]
        grads["pre_norm_g"][layer] = dpre[0]
        grads["post_norm_g"][layer] = dpost[0]
    grad_x = g.reshape(x.shape)
    col_bucket = jnp.asarray(np.broadcast_to(_bucket_table()[0:1, ::-1], (B_HEADS, BLOCK)))
    flip = jnp.asarray(np.eye(BLOCK, dtype=np.float32)[::-1], dtype=BF16)
    drel = _rel_bias_grad(dbias, col_bucket, flip)[:, :REL_BUCKETS].T

    (recv_out,) = _exchange_wait(started_out, [zone_out], g, "grads_wait_out")
    recv_out = lax.dynamic_update_index_in_dim(recv_out, jnp.concatenate(own_out, axis=0), me, 0)
    res_out = _reduce_adamw(recv_out, w_out.reshape(-1, D_MODEL), m_w_out.reshape(-1, D_MODEL),
                            v_w_out.reshape(-1, D_MODEL), "adamw_w_out")
    res_out = [r.reshape(w_out.shape) for r in res_out]

    no_state = jnp.zeros((1,), F32)
    small_w = dict(pre_norm_g=pre_norm_g, post_norm_g=post_norm_g, ln_v_g=ln_v_g,
                   ln_v_b=ln_v_b, b_spatial=b_spatial, sinks=sinks, rel_bias=rel_bias, loss=no_state)
    small_m = dict(pre_norm_g=m_pre_norm_g, post_norm_g=m_post_norm_g, ln_v_g=m_ln_v_g,
                   ln_v_b=m_ln_v_b, b_spatial=m_b_spatial, sinks=m_sinks, rel_bias=m_rel_bias, loss=no_state)
    small_v = dict(pre_norm_g=v_pre_norm_g, post_norm_g=v_post_norm_g, ln_v_g=v_ln_v_g,
                   ln_v_b=v_ln_v_b, b_spatial=v_b_spatial, sinks=v_sinks, rel_bias=v_rel_bias, loss=no_state)
    small_g = {name: jnp.stack(grads[name]) for name in _SMALL if name not in ("rel_bias", "loss")}
    small_g["rel_bias"] = drel
    small_g["loss"] = loss_part[0, :1]
    shapes = {name: small_w[name].shape for name in _SMALL}
    slots = _all_gather_direct(_pack_small(small_g), "small_grads_all_gather")
    res_small = _reduce_adamw(slots, _pack_small(small_w), _pack_small(small_m), _pack_small(small_v), "adamw_small")

    recv_ws, recv_in = _exchange_wait(started_in, [zone_ws, zone_in], res_small[0], "grads_wait_in")
    recv_ws = lax.dynamic_update_index_in_dim(recv_ws, jnp.concatenate(own_ws, axis=0), me, 0)
    res_ws = _reduce_adamw(recv_ws, w_spatial.reshape(-1, CHUNK), m_w_spatial.reshape(-1, CHUNK),
                           v_w_spatial.reshape(-1, CHUNK), "adamw_w_spatial")
    res_ws = [r.reshape(w_spatial.shape) for r in res_ws]
    recv_in = lax.dynamic_update_index_in_dim(recv_in, jnp.concatenate(own_in, axis=0), me, 0)
    res_in = _reduce_adamw(recv_in, w_in_t.reshape(-1, D_MODEL), m_w_in_t.reshape(-1, D_MODEL),
                           v_w_in_t.reshape(-1, D_MODEL), "adamw_w_in")
    res_in = [jnp.swapaxes(r.reshape(w_in_t.shape), 1, 2) for r in res_in]
    res_small = [_unpack_small(r, shapes) for r in res_small]

    order = ("pre_norm_g", "w_in", "ln_v_g", "ln_v_b", "w_spatial", "b_spatial", "sinks", "rel_bias", "w_out",
             "post_norm_g")
    outs = [res_small[0]["loss"][0], grad_x]
    for kind in range(4):
        for name in order:
            if name == "w_in":
                outs.append(res_in[kind])
            elif name == "w_out":
                outs.append(res_out[kind])
            elif name == "w_spatial":
                outs.append(res_ws[kind])
            else:
                outs.append(res_small[kind][name])
    return tuple(outs)
```

```python
import math

import numpy as np
import jax
import jax.numpy as jnp
from jax import lax
from jax.experimental import pallas as pl
from jax.experimental.pallas import tpu as pltpu

F32 = jnp.float32
BF16 = jnp.bfloat16

D_MODEL = 1024
A_WIDTH = 512
A_GROUPS = 8
CHUNK = 128
B_HEADS = 8
HEAD_DIM = 64
B_WIDTH = 512
KV_WIDTH = 128
BLOCK = 128
REL_BUCKETS = 32
REL_MAX_DIST = 128
NORM_EPS = 1e-6
PA_WIDTH = 3 * A_WIDTH
PB_WIDTH = 2 * B_WIDTH + 2 * KV_WIDTH
IN_WIDTH = PA_WIDTH + PB_WIDTH
K_OFF, BZ_OFF = B_WIDTH, B_WIDTH + 2 * KV_WIDTH
KEPT_WIDTH = 5 * A_WIDTH
SCALE = HEAD_DIM ** -0.5
NEG = -1e30
N_DEV = 8
LANES = 128

ADAM_LR = 0.001
ADAM_B1 = 0.9
ADAM_B2 = 0.999
ADAM_EPS = 1e-08
ADAM_WD = 0.01
ADAM_STEP = 10

V7X_VMEM_BYTES = 64 * 1024 * 1024
VMEM_TEMP_BYTES = 12 * 1024 * 1024
MESH = pl.DeviceIdType.MESH


def _vmem_limit(block_bytes, scratch_bytes=0):
    need = 2 * sum(block_bytes) + scratch_bytes + VMEM_TEMP_BYTES
    return int(min(need, V7X_VMEM_BYTES - 8 * 1024 * 1024))


def _nbytes(shape, dtype):
    return int(np.prod(shape)) * jnp.dtype(dtype).itemsize


def _token_tile(tokens, cap=512):
    tile = min(cap, tokens // 2)
    assert tokens % tile == 0 and tile % CHUNK == 0, tokens
    return tile


def _row_tile(rows, cap=512):
    best = 8
    for t in range(8, cap + 1, 8):
        if rows % t == 0:
            best = t
    assert rows % best == 0, rows
    return best


def _mm(a, b):
    return lax.dot_general(a, b, (((1,), (0,)), ((), ())), preferred_element_type=F32)


def _mm_nt(a, b):
    return lax.dot_general(a, b, (((1,), (1,)), ((), ())), preferred_element_type=F32)


def _mm_tn(a, b):
    return lax.dot_general(a, b, (((0,), (0,)), ((), ())), preferred_element_type=F32)


_GELU_C = math.sqrt(2.0 / math.pi)


_GELU_A = _GELU_C * 0.044715


def _gelu_parts(x):
    x2 = x * x
    t = jnp.tanh(x * (_GELU_C + _GELU_A * x2))
    return x2, t, 0.5 + 0.5 * t


def _gelu_and_grad(x):
    x2, t, half_plus = _gelu_parts(x)
    grad = half_plus + (0.5 * x) * (1.0 - t * t) * (_GELU_C + (3.0 * _GELU_A) * x2)
    return x * half_plus, grad


def _sigmoid(x):
    return 0.5 + 0.5 * jnp.tanh(0.5 * x)


def _bucket_table():
    q = np.arange(BLOCK)[:, None]
    k = np.arange(BLOCK)[None, :]
    dist = np.where(k <= q, q - k, q + BLOCK - k)
    max_exact = REL_BUCKETS // 2
    safe = np.maximum(dist, 1).astype(np.float32)
    large = max_exact + (np.log(safe / np.float32(max_exact)) / np.float32(math.log(REL_MAX_DIST / max_exact))
                         * np.float32(REL_BUCKETS - max_exact)).astype(np.int32)
    large = np.minimum(large, REL_BUCKETS - 1)
    assert dist.min() >= 0 and dist.max() < BLOCK
    return np.where(dist < max_exact, dist, large).astype(np.int32)


def _hbm(x):
    return pltpu.with_memory_space_constraint(x, pltpu.HBM)


def _slot(px, py, pc):
    return 4 * px + 2 * py + pc


def _all_gather(arrs, name):
    n = len(arrs)

    def body(*refs):
        ins, outs = refs[:n], refs[n:2 * n]
        send_sems, recv_sems, local_sems = refs[2 * n:]
        x, y, c = lax.axis_index("x"), lax.axis_index("y"), lax.axis_index("c")
        me, sibling = (x, y, c), (x, y, 1 - c)
        chips = [(1 - x, y), (x, 1 - y), (1 - x, 1 - y)]

        def copy(a, k, block, to, src=None):
            dst = outs[a].at[_slot(*block)]
            return pltpu.make_async_remote_copy(
                src_ref=dst if src is None else src, dst_ref=dst,
                send_sem=send_sems.at[7 * a + k], recv_sem=recv_sems.at[7 * a + k],
                device_id=to, device_id_type=MESH)

        mine = [pltpu.make_async_copy(ins[a], outs[a].at[_slot(*me)], local_sems.at[a]) for a in range(n)]
        for cp in mine:
            cp.start()
        first = []
        for a in range(n):
            first.append(copy(a, 0, me, sibling, src=ins[a]))
            first += [copy(a, 1 + j, me, (*chip, c), src=ins[a]) for j, chip in enumerate(chips)]
        for cp in first:
            cp.start()
        passed = []
        for j, chip in enumerate(chips):
            for a in range(n):
                copy(a, 1 + j, (*chip, c), me).wait_recv()
                fwd = copy(a, 4 + j, (*chip, c), sibling)
                fwd.start()
                passed.append(fwd)
        for a in range(n):
            copy(a, 0, sibling, me).wait_recv()
            for j, chip in enumerate(chips):
                copy(a, 4 + j, (*chip, 1 - c), me).wait_recv()
        for cp in first + passed:
            cp.wait_send()
        for cp in mine:
            cp.wait()

    any_spec = pl.BlockSpec(memory_space=pl.ANY)
    return pl.pallas_call(
        body, name=name,
        out_shape=[jax.ShapeDtypeStruct((N_DEV,) + a.shape, a.dtype) for a in arrs],
        in_specs=[any_spec] * n, out_specs=[any_spec] * n,
        scratch_shapes=[pltpu.SemaphoreType.DMA((7 * n,)), pltpu.SemaphoreType.DMA((7 * n,)),
                        pltpu.SemaphoreType.DMA((n,))],
    )(*arrs)


def _all_gather_direct(arr, name):
    def body(in_ref, out_ref, send_sems, recv_sems, local_sem):
        me, peers = _peers()
        mine = pltpu.make_async_copy(in_ref, out_ref.at[_slot(*me)], local_sem)
        mine.start()

        def copy(k, origin, to):
            return pltpu.make_async_remote_copy(
                src_ref=in_ref, dst_ref=out_ref.at[_slot(*origin)], send_sem=send_sems.at[k],
                recv_sem=recv_sems.at[k], device_id=to, device_id_type=MESH)

        sends = [copy(k, me, peer) for k, peer in enumerate(peers)]
        for cp in sends:
            cp.start()
        for k, peer in enumerate(peers):
            copy(k, peer, me).wait_recv()
        for cp in sends:
            cp.wait_send()
        mine.wait()

    any_spec = pl.BlockSpec(memory_space=pl.ANY)
    return pl.pallas_call(
        body, name=name,
        out_shape=jax.ShapeDtypeStruct((N_DEV,) + arr.shape, arr.dtype),
        in_specs=[any_spec], out_specs=any_spec,
        scratch_shapes=[pltpu.SemaphoreType.DMA((7,)), pltpu.SemaphoreType.DMA((7,)), pltpu.SemaphoreType.DMA],
    )(arr)


_HBM_SPEC = pl.BlockSpec(memory_space=pltpu.HBM)
_SEM_SPEC = pl.BlockSpec(memory_space=pltpu.SEMAPHORE)
_DATAFLOW = pltpu.SideEffectType.DATAFLOW_SIDE_EFFECTING


def _peers():
    x, y, c = lax.axis_index("x"), lax.axis_index("y"), lax.axis_index("c")
    peers = []
    for k in range(1, N_DEV):
        fx, fy, fc = (k >> 2) & 1, (k >> 1) & 1, k & 1
        peers.append((1 - x if fx else x, 1 - y if fy else y, 1 - c if fc else c))
    return (x, y, c), peers


def _exchange_copy(send_ref, land_ref, row_off, src_slot, dst_slot, sems, idx, peer):
    rows = send_ref.shape[-2]
    src = send_ref.at[src_slot] if len(send_ref.shape) == 3 else send_ref
    return pltpu.make_async_remote_copy(
        src_ref=src, dst_ref=land_ref.at[dst_slot, pl.ds(row_off, rows), :],
        send_sem=sems[0].at[idx], recv_sem=sems[1].at[idx], device_id=peer, device_id_type=MESH)


def _own_copy(send_ref, land_ref, row_off, slot, sem):
    rows = send_ref.shape[-2]
    src = send_ref.at[slot] if len(send_ref.shape) == 3 else send_ref
    return pltpu.make_async_copy(src, land_ref.at[slot, pl.ds(row_off, rows), :], sem)


def _exchange_start(sends, lands, row_offs, name, carry, after=None):
    n = len(sends)

    def body(*refs):
        ins, zones, carry_ref = refs[:n], refs[n:2 * n], refs[2 * n]
        first_out = 2 * n + 1 + (after is not None)
        sems = refs[first_out:first_out + 3]
        carry_out = refs[-1]
        me, peers = _peers()
        for a in range(n):
            for k, peer in enumerate(peers):
                _exchange_copy(ins[a], zones[a], row_offs[a], _slot(*peer), _slot(*me), sems, 7 * a + k, peer).start()
            _own_copy(ins[a], zones[a], row_offs[a], _slot(*me), sems[2].at[a]).start()
        carry_out[...] = carry_ref[...]

    arrays = [_hbm(a) for a in list(sends) + list(lands)]
    vmem = pl.BlockSpec(memory_space=pltpu.VMEM)
    out = pl.pallas_call(
        body, name=name,
        out_shape=(pltpu.SemaphoreType.DMA((7 * n,)), pltpu.SemaphoreType.DMA((7 * n,)), pltpu.SemaphoreType.DMA((n,)),
                   *[pltpu.HBM(a.shape, a.dtype) for a in arrays], jax.ShapeDtypeStruct(carry.shape, carry.dtype)),
        in_specs=[_HBM_SPEC] * (2 * n) + [vmem] + ([pl.BlockSpec(memory_space=pl.ANY)] if after is not None else []),
        out_specs=(_SEM_SPEC, _SEM_SPEC, _SEM_SPEC, *[_HBM_SPEC] * (2 * n), vmem),
        input_output_aliases={i: 3 + i for i in range(2 * n)},
        compiler_params=pltpu.CompilerParams(has_side_effects=_DATAFLOW),
    )(*arrays, carry, *([after] if after is not None else []))
    return (out[0], out[1], out[2]), list(out[3:3 + n]), list(out[3 + n:3 + 2 * n]), out[-1]


def _exchange_wait(started, lands, after, name):
    n = len(lands)
    flat_sends = [s for _, sends, _ in started for s in sends]
    flat_sems = [s for sems, _, _ in started for s in sems]
    ns = len(flat_sends)

    def body(*refs):
        ins, zones = refs[:ns], refs[ns:ns + n]
        sem_refs = refs[ns + n:ns + n + len(flat_sems)]
        me, peers = _peers()
        pos = 0
        for call, (_, sends, row_offs) in enumerate(started):
            sems = sem_refs[3 * call:3 * call + 3]
            for a in range(len(sends)):
                for k, peer in enumerate(peers):
                    cp = _exchange_copy(ins[pos + a], zones[a], row_offs[a], _slot(*peer), _slot(*peer), sems,
                                        7 * a + k, peer)
                    cp.wait_send()
                    cp.wait_recv()
                _own_copy(ins[pos + a], zones[a], row_offs[a], _slot(*me), sems[2].at[a]).wait()
            pos += len(sends)

    arrays = list(flat_sends) + list(lands)
    out = pl.pallas_call(
        body, name=name,
        out_shape=tuple(pltpu.HBM(a.shape, a.dtype) for a in arrays),
        in_specs=[_HBM_SPEC] * len(arrays) + [_SEM_SPEC] * len(flat_sems) + [pl.BlockSpec(memory_space=pl.ANY)],
        out_specs=tuple([_HBM_SPEC] * len(arrays)),
        input_output_aliases={i: i for i in range(len(arrays))},
        compiler_params=pltpu.CompilerParams(has_side_effects=_DATAFLOW),
    )(*arrays, *flat_sems, after)
    return list(out[ns:])


def _bias_table(rel_bias, bucket):
    def body(rel_ref, bucket_ref, out_ref):
        bk = bucket_ref[...]
        for h in range(B_HEADS):
            def pick(b, acc, h=h):
                return jnp.where(bk == b, rel_ref[b, h], acc)
            out_ref[h] = lax.fori_loop(0, REL_BUCKETS, pick, jnp.zeros((BLOCK, BLOCK), F32))

    return pl.pallas_call(
        body, name="bias_table",
        out_shape=jax.ShapeDtypeStruct((B_HEADS, BLOCK, BLOCK), F32),
        in_specs=[pl.BlockSpec(memory_space=pltpu.SMEM), pl.BlockSpec(memory_space=pltpu.VMEM)],
        out_specs=pl.BlockSpec(memory_space=pltpu.VMEM),
    )(rel_bias, bucket)


def _inproj_gmlp_fwd(x2, pre_g3, w, ln_g3, ln_b3, wm, bsb, layer):
    tokens = x2.shape[0]
    tm = _token_tile(tokens)

    def body(x_ref, g_ref, w_ref, lg_ref, lb_ref, wm_ref, bsb_ref, h_ref, pa_ref, pb_ref, ya_ref, kept_ref, rstd_ref):
        x = x_ref[...]
        r = lax.rsqrt(jnp.mean(x * x, axis=-1, keepdims=True) + NORM_EPS)
        h = (x * r * g_ref[...]).astype(BF16)
        h_ref[...] = h
        pa_ref[...] = _mm_nt(h, w_ref[0:PA_WIDTH, :]).astype(BF16)
        pb_ref[...] = _mm_nt(h, w_ref[PA_WIDTH:IN_WIDTH, :]).astype(BF16)
        for ci in range(tm // CHUNK):
            rows = slice(ci * CHUNK, (ci + 1) * CHUNK)
            gu, gv, pz, u, vhat, rstd, _, mixed, sg = _gmlp_forward_chunk(
                pa_ref[rows, :], lg_ref[...], lb_ref[...], wm_ref, bsb_ref[...])
            ya_ref[rows, :] = (u * mixed * (pz * sg)).astype(BF16)
            for k, val in enumerate((u, gu, gv, vhat, mixed)):
                kept_ref[rows, A_WIDTH * k:A_WIDTH * (k + 1)] = val.astype(BF16)
            rstd_ref[rows, :] = jnp.broadcast_to(rstd, (CHUNK, LANES))

    blocks = [_nbytes((tm, D_MODEL), F32), _nbytes((D_MODEL, IN_WIDTH), BF16),
              _nbytes((tm, D_MODEL), BF16), _nbytes((tm, PA_WIDTH), BF16), _nbytes((tm, PB_WIDTH), BF16),
              _nbytes((A_GROUPS, CHUNK, CHUNK), BF16), _nbytes((CHUNK, A_WIDTH), F32), _nbytes((tm, A_WIDTH), BF16)]
    return pl.pallas_call(
        body, name=f"inproj_gmlp_fwd_{layer}", grid=(tokens // tm,),
        in_specs=[pl.BlockSpec((tm, D_MODEL), lambda i: (i, 0)),
                  pl.BlockSpec((None, 1, D_MODEL), lambda i: (layer, 0, 0)),
                  pl.BlockSpec((IN_WIDTH, D_MODEL), lambda i: (0, 0)),
                  pl.BlockSpec((None, 1, A_WIDTH), lambda i: (layer, 0, 0)),
                  pl.BlockSpec((None, 1, A_WIDTH), lambda i: (layer, 0, 0)),
                  pl.BlockSpec((None, A_GROUPS, CHUNK, CHUNK), lambda i: (layer, 0, 0, 0)),
                  pl.BlockSpec((None, CHUNK, A_WIDTH), lambda i: (layer, 0, 0))],
        out_specs=[pl.BlockSpec((tm, D_MODEL), lambda i: (i, 0)),
                   pl.BlockSpec((tm, PA_WIDTH), lambda i: (i, 0)),
                   pl.BlockSpec((tm, PB_WIDTH), lambda i: (i, 0)),
                   pl.BlockSpec((tm, A_WIDTH), lambda i: (i, 0)),
                   pl.BlockSpec((tm, KEPT_WIDTH), lambda i: (i, 0)),
                   pl.BlockSpec((tm, LANES), lambda i: (i, 0))],
        out_shape=[pltpu.HBM((tokens, D_MODEL), BF16),
                   pltpu.HBM((tokens, PA_WIDTH), BF16),
                   pltpu.HBM((tokens, PB_WIDTH), BF16),
                   pltpu.HBM((tokens, A_WIDTH), BF16),
                   pltpu.HBM((tokens, KEPT_WIDTH), BF16),
                   pltpu.HBM((tokens, LANES), F32)],
        compiler_params=pltpu.CompilerParams(dimension_semantics=("parallel",),
                                             vmem_limit_bytes=_vmem_limit(blocks + [_nbytes((tm, KEPT_WIDTH), BF16),
                                                                                    _nbytes((tm, LANES), F32)])),
    )(_hbm(x2), pre_g3, _hbm(w), ln_g3, ln_b3, wm, bsb)


def _gmlp_forward_chunk(pa, ln_g, ln_b, wm_ref, bsb):
    pu = pa[:, 0:A_WIDTH].astype(F32)
    pv = pa[:, A_WIDTH:2 * A_WIDTH].astype(F32)
    pz = pa[:, 2 * A_WIDTH:3 * A_WIDTH].astype(F32)
    u, gu = _gelu_and_grad(pu)
    vv, gv = _gelu_and_grad(pv)
    mu = jnp.mean(vv, axis=-1, keepdims=True)
    xc = vv - mu
    rstd = lax.rsqrt(jnp.mean(xc * xc, axis=-1, keepdims=True) + NORM_EPS)
    vhat = xc * rstd
    vnb = (vhat * ln_g + ln_b).astype(BF16)
    low = lax.broadcasted_iota(jnp.int32, (CHUNK, LANES), 1) < HEAD_DIM
    parts = []
    for p in range(A_GROUPS // 2):
        vp = vnb[:, LANES * p:LANES * (p + 1)]
        parts.append(jnp.where(low, _mm(wm_ref[2 * p], vp), _mm(wm_ref[2 * p + 1], vp)))
    mixed = jnp.concatenate(parts, axis=1) + bsb
    sg = _sigmoid(pz)
    return gu, gv, pz, u, vhat, rstd, vnb, mixed, sg


def _kv_variants(kv):
    t = kv.astype(F32)
    rolled = pltpu.roll(t, HEAD_DIM, 1)
    low = lax.broadcasted_iota(jnp.int32, t.shape, 1) < HEAD_DIM
    zero = jnp.zeros_like(t)
    head0 = (jnp.where(low, t, zero).astype(BF16), jnp.where(low, zero, rolled).astype(BF16))
    head1 = (jnp.where(low, rolled, zero).astype(BF16), jnp.where(low, zero, t).astype(BF16))
    return head0, head1


def _band_masks():
    row = lax.broadcasted_iota(jnp.int32, (BLOCK, BLOCK), 0)
    col = lax.broadcasted_iota(jnp.int32, (BLOCK, BLOCK), 1)
    return col <= row


def _wrap(full, tri):
    return jnp.where(tri, full[:, BLOCK:2 * BLOCK], full[:, 0:BLOCK])


def _attn_probs(sf, bias_h, sink, tri, kill):
    s = _wrap(sf, tri) + bias_h
    s = jnp.where(kill, NEG, s)
    m = jnp.maximum(jnp.max(s, axis=-1, keepdims=True), sink)
    e = jnp.exp(s - m)
    es = jnp.exp(sink - m)
    inv = 1.0 / (jnp.sum(e, axis=-1, keepdims=True) + es)
    return e * inv, es * inv


def _unwrap16(p, tri):
    p = p.astype(BF16)
    zero = jnp.zeros_like(p)
    return jnp.concatenate([jnp.where(tri, zero, p), jnp.where(tri, p, zero)], axis=1)


def _fill_kv(kv_ref, prev_ref, cur_ref):
    kv_ref[0:BLOCK, :] = prev_ref[...]
    kv_ref[BLOCK:, :] = cur_ref[:, K_OFF:K_OFF + 2 * KV_WIDTH]


def _attn_outproj_fwd(pb, bias, sinks, ya, wo, x2, post_g3, b_loc, layer, target=None):
    tokens = pb.shape[0]
    nb = tokens // b_loc // BLOCK
    tq = _token_tile(tokens)
    per_tile = tq // BLOCK
    steps = tokens // tq
    with_loss = target is not None
    half = D_MODEL // 2

    def body(*refs):
        sink_ref, cur_ref, prev_ref, bias_ref, ya_ref, woa_ref, wob_ref, x_ref, g_ref = refs[:9]
        if with_loss:
            t_ref, yb_ref, pw_ref, spw_ref, y_ref, grad_ref, loss_ref, kv_ref, acc_ref = refs[9:]
        else:
            yb_ref, pw_ref, spw_ref, y_ref, xn_ref, kv_ref = refs[9:]
        t = pl.program_id(0)
        _fill_kv(kv_ref, prev_ref, cur_ref)
        tri = _band_masks()
        head_lane = lax.broadcasted_iota(jnp.int32, (BLOCK, LANES), 1)

        def block(i, carry):
            start = i * BLOCK
            rows = pl.ds(start, BLOCK)
            first = lax.rem(t * per_tile + i, nb) == 0
            kill = jnp.logical_and(first, jnp.logical_not(tri))
            kv = kv_ref[pl.ds(start, 2 * BLOCK), :]
            k_ops = _kv_variants(kv[:, 0:KV_WIDTH])
            v_ops = _kv_variants(kv[:, KV_WIDTH:2 * KV_WIDTH])
            p16 = {}
            sink_cols = jnp.zeros((BLOCK, LANES), F32)
            for kvh in range(2):
                pairs = (2 * kvh, 2 * kvh + 1)
                qs = jnp.concatenate([cur_ref[rows, LANES * p:LANES * (p + 1)] for p in pairs], axis=0) * SCALE
                for j in range(2):
                    sf = _mm_nt(qs, k_ops[kvh][j])
                    for r, p in enumerate(pairs):
                        hd = 2 * p + j
                        probs, sink_p = _attn_probs(sf[BLOCK * r:BLOCK * (r + 1)], bias_ref[hd],
                                                    sink_ref[layer, hd], tri, kill)
                        probs = probs.astype(BF16)
                        p16[hd] = _unwrap16(probs, tri)
                        pw_ref[rows, LANES * hd:LANES * (hd + 1)] = probs
                        sink_cols = jnp.where(head_lane == hd, sink_p, sink_cols)
            spw_ref[rows, :] = sink_cols
            for kvh in range(2):
                pairs = (2 * kvh, 2 * kvh + 1)
                out = jnp.zeros((2 * BLOCK, LANES), F32)
                for j in range(2):
                    out = out + _mm(jnp.concatenate([p16[2 * p + j] for p in pairs], axis=0), v_ops[kvh][j])
                for r, p in enumerate(pairs):
                    bz = cur_ref[rows, BZ_OFF + LANES * p:BZ_OFF + LANES * (p + 1)].astype(F32)
                    yb_ref[rows, LANES * p:LANES * (p + 1)] = (
                        out[BLOCK * r:BLOCK * (r + 1)] * (bz * _sigmoid(bz))).astype(BF16)
            y = _mm(ya_ref[rows, :], woa_ref[...]) + _mm(yb_ref[rows, :], wob_ref[...])
            r = lax.rsqrt(jnp.mean(y * y, axis=-1, keepdims=True) + NORM_EPS)
            x_next = x_ref[rows, :] + y * r * g_ref[...]
            y_ref[rows, :] = y.astype(BF16)
            if not with_loss:
                xn_ref[rows, :] = x_next
            else:
                err = x_next - t_ref[rows, :]
                grad_ref[rows, :] = err * (1.0 / D_MODEL)
                acc_ref[...] += jnp.sum(err * err, axis=0, keepdims=True)
            return carry

        if with_loss:
            @pl.when(t == 0)
            def _():
                acc_ref[...] = jnp.zeros_like(acc_ref)

        for i in range(per_tile):
            block(i, 0)

        if not with_loss:
            return

        @pl.when(t == steps - 1)
        def _():
            total = jnp.sum(acc_ref[...], axis=-1, keepdims=True) * (0.5 / D_MODEL)
            loss_ref[...] = jnp.broadcast_to(total, loss_ref.shape)

    tile = pl.BlockSpec((tq, D_MODEL), lambda t: (t, 0))
    blocks = [_nbytes((tq, PB_WIDTH), BF16), _nbytes((BLOCK, 2 * KV_WIDTH), BF16),
              _nbytes((B_HEADS, BLOCK, BLOCK), F32), 2 * _nbytes((tq, B_WIDTH), BF16),
              2 * _nbytes((half, D_MODEL), BF16), 3 * _nbytes((tq, D_MODEL), F32), _nbytes((tq, D_MODEL), BF16),
              _nbytes((tq, B_HEADS * BLOCK), BF16), _nbytes((tq, LANES), F32)]
    scratch = _nbytes((tq + BLOCK, 2 * KV_WIDTH), BF16)
    in_specs = [pl.BlockSpec(memory_space=pltpu.SMEM),
                pl.BlockSpec((tq, PB_WIDTH), lambda t: (t, 0)),
                pl.BlockSpec((BLOCK, 2 * KV_WIDTH),
                             lambda t: (jnp.maximum(t * per_tile - 1, 0), K_OFF // (2 * KV_WIDTH))),
                pl.BlockSpec((B_HEADS, BLOCK, BLOCK), lambda t: (0, 0, 0)),
                pl.BlockSpec((tq, half), lambda t: (t, 0)),
                pl.BlockSpec((half, D_MODEL), lambda t: (0, 0)),
                pl.BlockSpec((half, D_MODEL), lambda t: (1, 0)),
                tile,
                pl.BlockSpec((None, 1, D_MODEL), lambda t: (layer, 0, 0))]
    out_specs = [pl.BlockSpec((tq, B_WIDTH), lambda t: (t, 0)), pl.BlockSpec((tq, B_HEADS * BLOCK), lambda t: (t, 0)),
                 pl.BlockSpec((tq, LANES), lambda t: (t, 0)), tile, tile]
    out_shape = [pltpu.HBM((tokens, B_WIDTH), BF16), pltpu.HBM((tokens, B_HEADS * BLOCK), BF16),
                 pltpu.HBM((tokens, LANES), F32), pltpu.HBM((tokens, D_MODEL), BF16),
                 pltpu.HBM((tokens, D_MODEL), F32)]
    scratch_shapes = [pltpu.VMEM((tq + BLOCK, 2 * KV_WIDTH), BF16)]
    operands = [sinks, _hbm(pb), _hbm(pb), _hbm(bias), _hbm(ya), _hbm(wo), _hbm(wo), _hbm(x2), post_g3]
    if with_loss:
        in_specs.append(tile)
        out_specs.append(pl.BlockSpec((1, LANES), lambda t: (0, 0)))
        out_shape.append(pltpu.HBM((1, LANES), F32))
        scratch_shapes.append(pltpu.VMEM((1, D_MODEL), F32))
        operands.append(_hbm(target))
    return pl.pallas_call(
        body, name=f"attn_outproj_fwd_{layer}", grid=(steps,),
        in_specs=in_specs, out_specs=out_specs, out_shape=out_shape, scratch_shapes=scratch_shapes,
        compiler_params=pltpu.CompilerParams(dimension_semantics=("arbitrary" if with_loss else "parallel",),
                                             vmem_limit_bytes=_vmem_limit(blocks, scratch)),
    )(*operands)


def _outproj_bwd(g, y, ya, yb, wo, post_g3, layer):
    tokens = g.shape[0]
    tm = _token_tile(tokens, cap=1024)
    half = D_MODEL // 2
    steps = tokens // tm

    def body(g_ref, y_ref, ya_ref, yb_ref, woa_ref, wob_ref, pg_ref, dya_ref, dyb_ref, dwo16_ref, dpg_ref, dwo_ref):
        @pl.when(pl.program_id(0) == 0)
        def _():
            dwo_ref[...] = jnp.zeros_like(dwo_ref)
            dpg_ref[...] = jnp.zeros_like(dpg_ref)

        gv = g_ref[...]
        yf = y_ref[...].astype(F32)
        r = lax.rsqrt(jnp.mean(yf * yf, axis=-1, keepdims=True) + NORM_EPS)
        yhat = yf * r
        gg = gv * pg_ref[...]
        dy = r * (gg - yhat * jnp.mean(gg * yhat, axis=-1, keepdims=True))
        dpg_ref[...] += jnp.sum(gv * yhat, axis=0, keepdims=True)
        dyb16 = dy.astype(BF16)
        dya_ref[...] = _mm_nt(dyb16, woa_ref[...]).astype(BF16)
        dyb_ref[...] = _mm_nt(dyb16, wob_ref[...]).astype(BF16)
        dwo_ref[0:half, :] += _mm_tn(ya_ref[...], dyb16)
        dwo_ref[half:D_MODEL, :] += _mm_tn(yb_ref[...], dyb16)

        @pl.when(pl.program_id(0) == steps - 1)
        def _():
            dwo16_ref[...] = dwo_ref[...].astype(BF16)

    blocks = [_nbytes((tm, D_MODEL), F32), _nbytes((tm, D_MODEL), BF16), 4 * _nbytes((tm, half), BF16),
              2 * _nbytes((half, D_MODEL), BF16), _nbytes((D_MODEL, D_MODEL), BF16)]
    return pl.pallas_call(
        body, name=f"outproj_bwd_{layer}", grid=(tokens // tm,),
        in_specs=[pl.BlockSpec((tm, D_MODEL), lambda i: (i, 0)),
                  pl.BlockSpec((tm, D_MODEL), lambda i: (i, 0)),
                  pl.BlockSpec((tm, half), lambda i: (i, 0)),
                  pl.BlockSpec((tm, half), lambda i: (i, 0)),
                  pl.BlockSpec((half, D_MODEL), lambda i: (0, 0)),
                  pl.BlockSpec((half, D_MODEL), lambda i: (1, 0)),
                  pl.BlockSpec((None, 1, D_MODEL), lambda i: (layer, 0, 0))],
        out_specs=[pl.BlockSpec((tm, half), lambda i: (i, 0)),
                   pl.BlockSpec((tm, half), lambda i: (i, 0)),
                   pl.BlockSpec((D_MODEL, D_MODEL), lambda i: (0, 0)),
                   pl.BlockSpec((1, D_MODEL), lambda i: (0, 0))],
        out_shape=[pltpu.HBM((tokens,half), BF16),
                   pltpu.HBM((tokens,half), BF16),
                   pltpu.HBM((D_MODEL,D_MODEL), BF16),
                   pltpu.HBM((1,D_MODEL), F32)],
        scratch_shapes=[pltpu.VMEM((D_MODEL, D_MODEL), F32)],
        compiler_params=pltpu.CompilerParams(dimension_semantics=("arbitrary",),
                                             vmem_limit_bytes=_vmem_limit(blocks, _nbytes((D_MODEL, D_MODEL), F32))),
    )(_hbm(g), _hbm(y), _hbm(ya), _hbm(yb), _hbm(wo), _hbm(wo), post_g3)


def _gmlp_bwd_dw(pa, kept, rstd_w, dya, h, dqz, dkv, ln_g3, ln_b3, wmt, layer):
    tokens = pa.shape[0]
    tc = _token_tile(tokens)
    steps = tokens // tc

    def body(pz_ref, kept_ref, rstd_ref, dya_ref, h_ref, dqz_ref, dkv_ref, lg_ref, lb_ref, wmt_ref,
             da_ref, dw_ref, dws_ref, dbs_ref, dlg_ref, dlb_ref, acc_ref, dbsb_ref):
        i = pl.program_id(0)

        @pl.when(i == 0)
        def _():
            acc_ref[...] = jnp.zeros_like(acc_ref)
            dws_ref[...] = jnp.zeros_like(dws_ref)
            dlg_ref[...] = jnp.zeros_like(dlg_ref)
            dlb_ref[...] = jnp.zeros_like(dlb_ref)
            dbsb_ref[...] = jnp.zeros_like(dbsb_ref)

        ln_g = lg_ref[...]
        low = lax.broadcasted_iota(jnp.int32, (CHUNK, LANES), 1) < HEAD_DIM
        hv = h_ref[...]
        acc_ref[:, PA_WIDTH:PA_WIDTH + B_WIDTH] += _mm_tn(hv, dqz_ref[:, 0:B_WIDTH])
        acc_ref[:, PA_WIDTH + K_OFF:PA_WIDTH + BZ_OFF] += _mm_tn(hv, dkv_ref[...])
        acc_ref[:, PA_WIDTH + BZ_OFF:IN_WIDTH] += _mm_tn(hv, dqz_ref[:, B_WIDTH:2 * B_WIDTH])

        for ci in range(tc // CHUNK):
            rows = slice(ci * CHUNK, (ci + 1) * CHUNK)
            u, gu, gv, vhat, mixed = (kept_ref[rows, A_WIDTH * k:A_WIDTH * (k + 1)].astype(F32) for k in range(5))
            rstd = jnp.concatenate([rstd_ref[rows, :]] * (A_WIDTH // LANES), axis=1)
            vnb = (vhat * ln_g + lb_ref[...]).astype(BF16)
            pz = pz_ref[rows, :].astype(F32)
            sg = _sigmoid(pz)
            dy = dya_ref[rows, :].astype(F32)
            sz = pz * sg
            dy_sz = dy * sz
            du = dy_sz * mixed
            dmixed = dy_sz * u
            dz = dy * (u * mixed) * (sg + sz * (1.0 - sg))
            dbsb_ref[...] += dmixed
            dmb = dmixed.astype(BF16)
            zero = jnp.zeros((CHUNK, LANES), BF16)
            parts = []
            for p in range(A_GROUPS // 2):
                dmp = dmb[:, LANES * p:LANES * (p + 1)]
                vp = vnb[:, LANES * p:LANES * (p + 1)]
                parts.append(jnp.where(low, _mm(wmt_ref[2 * p], dmp), _mm(wmt_ref[2 * p + 1], dmp)))
                dws_ref[2 * p] += _mm_nt(jnp.where(low, dmp, zero), vp)
                dws_ref[2 * p + 1] += _mm_nt(jnp.where(low, zero, dmp), vp)
            dvn = jnp.concatenate(parts, axis=1)
            dlg_ref[...] += jnp.sum(dvn * vhat, axis=0, keepdims=True)
            dlb_ref[...] += jnp.sum(dvn, axis=0, keepdims=True)
            dvh = dvn * ln_g
            dvv = rstd * (dvh - jnp.mean(dvh, axis=-1, keepdims=True)
                          - vhat * jnp.mean(dvh * vhat, axis=-1, keepdims=True))
            da_ref[rows, 0:A_WIDTH] = (du * gu).astype(BF16)
            da_ref[rows, A_WIDTH:2 * A_WIDTH] = (dvv * gv).astype(BF16)
            da_ref[rows, 2 * A_WIDTH:3 * A_WIDTH] = dz.astype(BF16)

        acc_ref[:, 0:PA_WIDTH] += _mm_tn(hv, da_ref[...])

        @pl.when(i == steps - 1)
        def _():
            for c in range(IN_WIDTH // LANES):
                cols = slice(LANES * c, LANES * (c + 1))
                dw_ref[cols, :] = acc_ref[:, cols].T.astype(BF16)
            causal = (lax.broadcasted_iota(jnp.int32, (CHUNK, CHUNK), 0)
                      >= lax.broadcasted_iota(jnp.int32, (CHUNK, CHUNK), 1))
            for h in range(A_GROUPS):
                dws_ref[h] = jnp.where(causal, dws_ref[h], 0.0)
            acc = dbsb_ref[...]
            lane_full = lax.broadcasted_iota(jnp.int32, (CHUNK, A_WIDTH), 1)
            lane_out = lax.broadcasted_iota(jnp.int32, (CHUNK, LANES), 1)
            out = jnp.zeros((CHUNK, LANES), F32)
            for h in range(A_GROUPS):
                in_group = jnp.logical_and(lane_full >= HEAD_DIM * h, lane_full < HEAD_DIM * (h + 1))
                s = jnp.sum(jnp.where(in_group, acc, 0.0), axis=-1, keepdims=True)
                out = jnp.where(lane_out == h, s, out)
            dbs_ref[...] = out

    blocks = [_nbytes((tc, KEPT_WIDTH), BF16), 2 * _nbytes((tc, A_WIDTH), BF16), _nbytes((A_GROUPS, CHUNK, CHUNK), BF16),
              _nbytes((tc, LANES), F32), _nbytes((tc, PA_WIDTH), BF16), _nbytes((A_GROUPS, CHUNK, CHUNK), F32),
              _nbytes((CHUNK, LANES), F32), _nbytes((tc, D_MODEL), BF16), _nbytes((tc, 2 * B_WIDTH), BF16),
              _nbytes((tc, 2 * KV_WIDTH), BF16), _nbytes((D_MODEL, IN_WIDTH), BF16)]
    scratch = _nbytes((D_MODEL, IN_WIDTH), F32) + _nbytes((CHUNK, A_WIDTH), F32)
    in_specs = [pl.BlockSpec((tc, A_WIDTH), lambda i: (i, 2)),
                pl.BlockSpec((tc, KEPT_WIDTH), lambda i: (i, 0)),
                pl.BlockSpec((tc, LANES), lambda i: (i, 0)),
                pl.BlockSpec((tc, A_WIDTH), lambda i: (i, 0)),
                pl.BlockSpec((tc, D_MODEL), lambda i: (i, 0)),
                pl.BlockSpec((tc, 2 * B_WIDTH), lambda i: (i, 0)),
                pl.BlockSpec((pl.Element(tc), pl.Element(2 * KV_WIDTH)),
                             lambda i: (pl.multiple_of(i * tc + BLOCK, BLOCK), 0)),
                pl.BlockSpec((None, 1, A_WIDTH), lambda i: (layer, 0, 0)),
                pl.BlockSpec((None, 1, A_WIDTH), lambda i: (layer, 0, 0)),
                pl.BlockSpec((None, A_GROUPS, CHUNK, CHUNK), lambda i: (layer, 0, 0, 0))]
    operands = [_hbm(pa), _hbm(kept), _hbm(rstd_w), _hbm(dya), _hbm(h), _hbm(dqz), _hbm(dkv), ln_g3, ln_b3, wmt]
    out_specs = [pl.BlockSpec((tc, PA_WIDTH), lambda i: (i, 0)),
                 pl.BlockSpec((IN_WIDTH, D_MODEL), lambda i: (0, 0)),
                 pl.BlockSpec((A_GROUPS, CHUNK, CHUNK), lambda i: (0, 0, 0)),
                 pl.BlockSpec((CHUNK, LANES), lambda i: (0, 0)),
                 pl.BlockSpec((1, A_WIDTH), lambda i: (0, 0)),
                 pl.BlockSpec((1, A_WIDTH), lambda i: (0, 0))]
    out_shape = [pltpu.HBM((tokens, PA_WIDTH), BF16),
                 pltpu.HBM((IN_WIDTH, D_MODEL), BF16),
                 pltpu.HBM((A_GROUPS, CHUNK, CHUNK), F32),
                 pltpu.HBM((CHUNK, LANES), F32),
                 pltpu.HBM((1, A_WIDTH), F32),
                 pltpu.HBM((1, A_WIDTH), F32)]
    scratch_shapes = [pltpu.VMEM((D_MODEL, IN_WIDTH), F32), pltpu.VMEM((CHUNK, A_WIDTH), F32)]
    return pl.pallas_call(
        body, name=f"gmlp_bwd_dw_{layer}", grid=(steps,),
        in_specs=in_specs, out_specs=out_specs, out_shape=out_shape, scratch_shapes=scratch_shapes,
        compiler_params=pltpu.CompilerParams(dimension_semantics=("arbitrary",),
                                             vmem_limit_bytes=_vmem_limit(blocks, scratch)),
    )(*operands)


def _attn_bwd(pb, dyb, probs_w, sink_w, dbias_in, layer):
    tokens = pb.shape[0]
    qz_width = 2 * B_WIDTH
    tq = _token_tile(tokens)
    per_tile = tq // BLOCK
    nt = tokens // tq

    def body(cur_ref, prev_ref, dyb_ref, pw_ref, spw_ref, dbias_in_ref, dqz_ref, dkv_ref, dbias_ref, dsink_ref,
             kv_ref, acc_ref):
        t = pl.program_id(0)

        @pl.when(t == 0)
        def _():
            dbias_ref[...] = dbias_in_ref[...]
            dsink_ref[...] = jnp.zeros_like(dsink_ref)
            acc_ref[0:BLOCK, :] = jnp.zeros((BLOCK, 2 * KV_WIDTH), F32)

        @pl.when(t < nt)
        def _():
            acc_ref[BLOCK:, :] = jnp.zeros((tq, 2 * KV_WIDTH), F32)
            _fill_kv(kv_ref, prev_ref, cur_ref)
            tri = _band_masks()
            low = lax.broadcasted_iota(jnp.int32, (BLOCK, LANES), 1) < HEAD_DIM
            low_kv = lax.broadcasted_iota(jnp.int32, (2 * BLOCK, LANES), 1) < HEAD_DIM

            head_lane = lax.broadcasted_iota(jnp.int32, (BLOCK, LANES), 1)

            def block(i, carry):
                start = i * BLOCK
                rows = pl.ds(start, BLOCK)
                kv = kv_ref[pl.ds(start, 2 * BLOCK), :]
                k_ops = _kv_variants(kv[:, 0:KV_WIDTH])
                v_ops = _kv_variants(kv[:, KV_WIDTH:2 * KV_WIDTH])
                probs, p16, qs, dos, delta, ds16 = {}, {}, {}, {}, {}, {}
                for hd in range(B_HEADS):
                    kept_p = pw_ref[rows, LANES * hd:LANES * (hd + 1)]
                    probs[hd] = kept_p.astype(F32)
                    p16[hd] = _unwrap16(kept_p, tri)
                for kvh in range(2):
                    qs[kvh] = jnp.concatenate(
                        [cur_ref[rows, LANES * p:LANES * (p + 1)] for p in (2 * kvh, 2 * kvh + 1)], axis=0)
                for kvh in range(2):
                    pairs = (2 * kvh, 2 * kvh + 1)
                    out = jnp.zeros((2 * BLOCK, LANES), F32)
                    for j in range(2):
                        out = out + _mm(jnp.concatenate([p16[2 * p + j] for p in pairs], axis=0), v_ops[kvh][j])
                    d_outs = []
                    for r, p in enumerate(pairs):
                        bz = cur_ref[rows, BZ_OFF + LANES * p:BZ_OFF + LANES * (p + 1)].astype(F32)
                        sg = _sigmoid(bz)
                        dyp = dyb_ref[rows, LANES * p:LANES * (p + 1)].astype(F32)
                        out_p = out[BLOCK * r:BLOCK * (r + 1)]
                        d_out = dyp * (bz * sg)
                        dqz_ref[rows, B_WIDTH + LANES * p:B_WIDTH + LANES * (p + 1)] = (
                            dyp * out_p * (sg * (1.0 + bz * (1.0 - sg)))).astype(BF16)
                        dod = d_out * out_p
                        delta[2 * p] = jnp.sum(jnp.where(low, dod, 0.0), axis=-1, keepdims=True)
                        delta[2 * p + 1] = jnp.sum(jnp.where(low, 0.0, dod), axis=-1, keepdims=True)
                        d_outs.append(d_out.astype(BF16))
                    dos[kvh] = jnp.concatenate(d_outs, axis=0)
                delta_cols = jnp.zeros((BLOCK, LANES), F32)
                for hd in range(B_HEADS):
                    delta_cols = jnp.where(head_lane == hd, delta[hd], delta_cols)
                dsink_ref[0:1, :] += jnp.sum(-(spw_ref[rows, :] * delta_cols), axis=0, keepdims=True)
                for kvh in range(2):
                    pairs = (2 * kvh, 2 * kvh + 1)
                    for j in range(2):
                        dpf = _mm_nt(dos[kvh], v_ops[kvh][j])
                        for r, p in enumerate(pairs):
                            hd = 2 * p + j
                            ds = probs[hd] * (_wrap(dpf[BLOCK * r:BLOCK * (r + 1)], tri) - delta[hd])
                            dbias_ref[hd] += ds
                            ds16[hd] = _unwrap16(ds, tri)
                dk_acc = [[None, None], [None, None]]
                dv_acc = [[None, None], [None, None]]
                for kvh in range(2):
                    pairs = (2 * kvh, 2 * kvh + 1)
                    dq = jnp.zeros((2 * BLOCK, LANES), F32)
                    for j in range(2):
                        dss = jnp.concatenate([ds16[2 * p + j] for p in pairs], axis=0)
                        pss = jnp.concatenate([p16[2 * p + j] for p in pairs], axis=0)
                        dq = dq + _mm(dss, k_ops[kvh][j])
                        dk_acc[kvh][j] = _mm_tn(dss, qs[kvh])
                        dv_acc[kvh][j] = _mm_tn(pss, dos[kvh])
                    for r, p in enumerate(pairs):
                        dqz_ref[rows, LANES * p:LANES * (p + 1)] = (dq[BLOCK * r:BLOCK * (r + 1)] * SCALE).astype(BF16)

                def fold(acc):
                    return jnp.where(low_kv,
                                     acc[0][0] + pltpu.roll(acc[0][1], HEAD_DIM, 1),
                                     pltpu.roll(acc[1][0], HEAD_DIM, 1) + acc[1][1])

                acc_ref[pl.ds(start, 2 * BLOCK), :] += jnp.concatenate(
                    [fold(dk_acc) * SCALE, fold(dv_acc)], axis=1)
                return carry

            for i in range(per_tile):
                block(i, 0)
            dkv_ref[...] = acc_ref[0:tq, :].astype(BF16)
            acc_ref[0:BLOCK, :] = acc_ref[tq:tq + BLOCK, :]

        @pl.when(t == nt)
        def _():
            dkv_ref[0:BLOCK, :] = acc_ref[0:BLOCK, :].astype(BF16)
            dkv_ref[BLOCK:, :] = jnp.zeros((tq - BLOCK, 2 * KV_WIDTH), BF16)

    def cur_map(t):
        return (jnp.minimum(t, nt - 1), 0)

    def prev_map(t):
        return (jnp.maximum(jnp.minimum(t, nt - 1) * per_tile - 1, 0), K_OFF // (2 * KV_WIDTH))

    blocks = [_nbytes((tq, PB_WIDTH), BF16), _nbytes((BLOCK, 2 * KV_WIDTH), BF16), _nbytes((tq, B_WIDTH), BF16),
              2 * _nbytes((B_HEADS, BLOCK, BLOCK), F32), _nbytes((tq, qz_width), BF16),
              _nbytes((tq, 2 * KV_WIDTH), BF16), _nbytes((B_HEADS, LANES), F32),
              _nbytes((tq, B_HEADS * BLOCK), BF16), _nbytes((tq, LANES), F32)]
    scratch = _nbytes((tq + BLOCK, 2 * KV_WIDTH), BF16) + _nbytes((tq + BLOCK, 2 * KV_WIDTH), F32)
    return pl.pallas_call(
        body, name=f"attn_bwd_{layer}", grid=(nt + 1,),
        in_specs=[pl.BlockSpec((tq, PB_WIDTH), cur_map),
                  pl.BlockSpec((BLOCK, 2 * KV_WIDTH), prev_map),
                  pl.BlockSpec((tq, B_WIDTH), cur_map),
                  pl.BlockSpec((tq, B_HEADS * BLOCK), cur_map),
                  pl.BlockSpec((tq, LANES), cur_map),
                  pl.BlockSpec((B_HEADS, BLOCK, BLOCK), lambda t: (0, 0, 0))],
        out_specs=[pl.BlockSpec((tq, qz_width), cur_map),
                   pl.BlockSpec((tq, 2 * KV_WIDTH), lambda t: (t, 0)),
                   pl.BlockSpec((B_HEADS, BLOCK, BLOCK), lambda t: (0, 0, 0)),
                   pl.BlockSpec((B_HEADS, LANES), lambda t: (0, 0))],
        out_shape=[pltpu.HBM((tokens, qz_width), BF16),
                   pltpu.HBM((tokens + tq, 2 * KV_WIDTH), BF16),
                   pltpu.HBM((B_HEADS, BLOCK, BLOCK), F32),
                   pltpu.HBM((B_HEADS, LANES), F32)],
        scratch_shapes=[pltpu.VMEM((tq + BLOCK, 2 * KV_WIDTH), BF16),
                        pltpu.VMEM((tq + BLOCK, 2 * KV_WIDTH), F32)],
        compiler_params=pltpu.CompilerParams(dimension_semantics=("arbitrary",),
                                             vmem_limit_bytes=_vmem_limit(blocks, scratch)),
    )(_hbm(pb), _hbm(pb), _hbm(dyb), _hbm(probs_w), _hbm(sink_w), _hbm(dbias_in))


def _inproj_bwd_dx(da, dqz, dkv, w, x2, g, pre_g3, layer):
    tokens = x2.shape[0]
    tm = _token_tile(tokens)

    def body(da_ref, dqz_ref, dkv_ref, w_ref, x_ref, g_ref, pg_ref, gn_ref, dpg_ref):
        @pl.when(pl.program_id(0) == 0)
        def _():
            dpg_ref[...] = jnp.zeros_like(dpg_ref)

        dh = _mm(da_ref[...], w_ref[0:PA_WIDTH, :])
        dh += _mm(dqz_ref[:, 0:B_WIDTH], w_ref[PA_WIDTH:PA_WIDTH + B_WIDTH, :])
        dh += _mm(dkv_ref[...], w_ref[PA_WIDTH + K_OFF:PA_WIDTH + BZ_OFF, :])
        dh += _mm(dqz_ref[:, B_WIDTH:2 * B_WIDTH], w_ref[PA_WIDTH + BZ_OFF:IN_WIDTH, :])
        x = x_ref[...]
        r = lax.rsqrt(jnp.mean(x * x, axis=-1, keepdims=True) + NORM_EPS)
        xhat = x * r
        dhg = dh * pg_ref[...]
        dpg_ref[...] += jnp.sum(dh * xhat, axis=0, keepdims=True)
        gn_ref[...] = g_ref[...] + r * (dhg - xhat * jnp.mean(dhg * xhat, axis=-1, keepdims=True))

    blocks = [_nbytes((tm, PA_WIDTH), BF16), _nbytes((tm, 2 * B_WIDTH), BF16), _nbytes((tm, 2 * KV_WIDTH), BF16),
              _nbytes((D_MODEL, IN_WIDTH), BF16), 3 * _nbytes((tm, D_MODEL), F32)]
    return pl.pallas_call(
        body, name=f"inproj_bwd_dx_{layer}", grid=(tokens // tm,),
        in_specs=[pl.BlockSpec((tm, PA_WIDTH), lambda i: (i, 0)),
                  pl.BlockSpec((tm, 2 * B_WIDTH), lambda i: (i, 0)),
                  pl.BlockSpec((pl.Element(tm), pl.Element(2 * KV_WIDTH)), lambda i: (pl.multiple_of(i * tm + BLOCK, BLOCK), 0)),
                  pl.BlockSpec((IN_WIDTH, D_MODEL), lambda i: (0, 0)),
                  pl.BlockSpec((tm, D_MODEL), lambda i: (i, 0)),
                  pl.BlockSpec((tm, D_MODEL), lambda i: (i, 0)),
                  pl.BlockSpec((None, 1, D_MODEL), lambda i: (layer, 0, 0))],
        out_specs=[pl.BlockSpec((tm, D_MODEL), lambda i: (i, 0)),
                   pl.BlockSpec((1, D_MODEL), lambda i: (0, 0))],
        out_shape=[pltpu.HBM((tokens,D_MODEL), F32),
                   pltpu.HBM((1,D_MODEL), F32)],
        compiler_params=pltpu.CompilerParams(dimension_semantics=("arbitrary",),
                                             vmem_limit_bytes=_vmem_limit(blocks)),
    )(_hbm(da), _hbm(dqz), _hbm(dkv), _hbm(w), _hbm(x2), _hbm(g), pre_g3)


def _rel_bias_grad(dbias, col_bucket, flip):
    def body(db_ref, cb_ref, flip_ref, out_ref):
        cb = cb_ref[...]
        anti = flip_ref[...]
        sums = []
        for h in range(B_HEADS):
            x = db_ref[h]
            hi = x.astype(BF16)
            rest = x - hi.astype(F32)
            mid = rest.astype(BF16)
            low = (rest - mid.astype(F32)).astype(BF16)
            reversed_x = _mm(hi, anti) + _mm(mid, anti) + _mm(low, anti)
            rolled = pltpu.roll(reversed_x, 0, 1, stride=1, stride_axis=0)
            sums.append(jnp.sum(rolled, axis=0, keepdims=True))
        per_dist = jnp.concatenate(sums, axis=0)
        lane = lax.broadcasted_iota(jnp.int32, (B_HEADS, LANES), 1)
        out = jnp.zeros((B_HEADS, LANES), F32)
        for b in range(REL_BUCKETS):
            s = jnp.sum(jnp.where(cb == b, per_dist, 0.0), axis=-1, keepdims=True)
            out = jnp.where(lane == b, s, out)
        out_ref[...] = out

    vm = pl.BlockSpec(memory_space=pltpu.VMEM)
    return pl.pallas_call(
        body, name="rel_bias_grad",
        out_shape=jax.ShapeDtypeStruct((B_HEADS, LANES), F32),
        in_specs=[vm, vm, vm], out_specs=vm,
    )(dbias, col_bucket, flip)


def _reduce_adamw(slots, w, m, v, name):
    rows, cols = w.shape
    tr = _row_tile(rows)
    c1 = 1.0 / (1.0 - ADAM_B1 ** ADAM_STEP)
    c2 = 1.0 / (1.0 - ADAM_B2 ** ADAM_STEP)

    def body(s_ref, w_ref, m_ref, v_ref, g_ref, d_ref, nm_ref, nv_ref):
        g = s_ref[0].astype(F32)
        for i in range(1, N_DEV):
            g = g + s_ref[i].astype(F32)
        nm = ADAM_B1 * m_ref[...] + (1.0 - ADAM_B1) * g
        nv = ADAM_B2 * v_ref[...] + (1.0 - ADAM_B2) * (g * g)
        g_ref[...] = g
        nm_ref[...] = nm
        nv_ref[...] = nv
        d_ref[...] = -ADAM_LR * ((nm * c1) / (jnp.sqrt(nv * c2) + ADAM_EPS) + ADAM_WD * w_ref[...])

    blocks = [_nbytes((N_DEV, tr, cols), slots.dtype), 7 * _nbytes((tr, cols), F32)]
    tile = pl.BlockSpec((tr, cols), lambda i: (i, 0))
    return pl.pallas_call(
        body, name=name, grid=(rows // tr,),
        in_specs=[pl.BlockSpec((N_DEV, tr, cols), lambda i: (0, i, 0)), tile, tile, tile],
        out_specs=[tile] * 4,
        out_shape=[pltpu.HBM((rows, cols), F32)] * 4,
        compiler_params=pltpu.CompilerParams(dimension_semantics=("parallel",),
                                             vmem_limit_bytes=_vmem_limit(blocks)),
    )(_hbm(slots), _hbm(w), _hbm(m), _hbm(v))


_SMALL = ("pre_norm_g", "post_norm_g", "ln_v_g", "ln_v_b", "b_spatial", "sinks", "rel_bias", "loss")


def _pack_small(parts):
    slabs = []
    for name in _SMALL:
        flat = parts[name].astype(F32).reshape(-1)
        pad = (-flat.shape[0]) % (8 * LANES)
        slabs.append(jnp.pad(flat, (0, pad)).reshape(-1, LANES))
    return jnp.concatenate(slabs, axis=0)


def _unpack_small(slab, shapes):
    out, row = {}, 0
    for name in _SMALL:
        size = int(np.prod(shapes[name]))
        rows = -(-size // (8 * LANES)) * 8
        out[name] = slab[row:row + rows].reshape(-1)[:size].reshape(shapes[name])
        row += rows
    return out


def kernel(x, pre_norm_g, w_in, ln_v_g, ln_v_b, w_spatial, b_spatial, sinks, rel_bias, w_out, post_norm_g, loss_target, m_pre_norm_g, m_w_in, m_ln_v_g, m_ln_v_b, m_w_spatial, m_b_spatial, m_sinks, m_rel_bias, m_w_out, m_post_norm_g, v_pre_norm_g, v_w_in, v_ln_v_g, v_ln_v_b, v_w_spatial, v_b_spatial, v_sinks, v_rel_bias, v_w_out, v_post_norm_g):
    b_loc, seq, _ = x.shape
    tokens = b_loc * seq
    depth = w_in.shape[0]
    in_shard = w_in.shape[2]
    out_shard = w_out.shape[1]
    assert in_shard * N_DEV == IN_WIDTH and out_shard * N_DEV == D_MODEL and seq % BLOCK == 0

    w_in_t, m_w_in_t, v_w_in_t = (jnp.swapaxes(a, 1, 2) for a in (w_in, m_w_in, v_w_in))
    w_in16, w_out16 = w_in_t.astype(BF16), w_out.astype(BF16)

    def gather_start(layer, not_before):
        zones = [lax.empty((N_DEV, in_shard, D_MODEL), BF16), lax.empty((N_DEV, out_shard, D_MODEL), BF16)]
        sems, sends, zones, pre_g = _exchange_start(
            [w_in16[layer], w_out16[layer]], zones, [0, 0], f"weights_send_{layer}",
            pre_norm_g.reshape(depth, 1, D_MODEL), after=not_before)
        return (sems, sends, [0, 0]), zones, pre_g

    def full_weights(gin, gout):
        return gin.reshape(IN_WIDTH, D_MODEL), gout.reshape(D_MODEL, D_MODEL)

    def gather_wait(layer, started, zones, after):
        return full_weights(*_exchange_wait([started], zones, after, f"weights_wait_{layer}"))

    causal = jnp.tril(jnp.ones((CHUNK, CHUNK), dtype=bool))
    wm = jnp.where(causal, w_spatial, 0.0).astype(BF16)
    wmt = _hbm(jnp.swapaxes(wm, -1, -2))
    wm = _hbm(wm)
    bsb = _hbm(jnp.repeat(jnp.swapaxes(b_spatial, -1, -2), HEAD_DIM, axis=-1))
    pre_g3 = _hbm(pre_norm_g.reshape(depth, 1, D_MODEL))
    post_g3 = _hbm(post_norm_g.reshape(depth, 1, D_MODEL))
    ln_g3 = _hbm(ln_v_g.reshape(depth, 1, A_WIDTH))
    ln_b3 = _hbm(ln_v_b.reshape(depth, 1, A_WIDTH))
    bucket = jnp.asarray(_bucket_table())
    bias = _bias_table(rel_bias, bucket)

    xs, saved, weights = [x.reshape(tokens, D_MODEL)], [], []
    pending = None
    for layer in range(depth):
        if layer == 0:
            w, wo = full_weights(*_all_gather([w_in16[0], w_out16[0]], "weights_gather_0"))
        else:
            w, wo = gather_wait(layer, pending[0], pending[1], xs[-1])
        pre_g_fwd = pre_g3
        if layer + 1 < depth:
            pending = gather_start(layer + 1, w)
            pre_g_fwd = _hbm(pending[2])
        h, pa, pb, ya, kept, rstd_w = _inproj_gmlp_fwd(xs[-1], pre_g_fwd, w, ln_g3, ln_b3, wm, bsb, layer)
        if layer + 1 < depth:
            yb, pw, spw, y, x_next = _attn_outproj_fwd(pb, bias, sinks, ya, wo, xs[-1], post_g3, b_loc, layer)
            xs.append(x_next)
        else:
            yb, pw, spw, y, g, loss_part = _attn_outproj_fwd(pb, bias, sinks, ya, wo, xs[-1], post_g3, b_loc, layer,
                                                    target=loss_target.reshape(tokens, D_MODEL))
        saved.append((h, pa, pb, ya, yb, y, pw, spw, kept, rstd_w))
        weights.append((w, wo))

    grads = {name: [None] * depth for name in ("pre_norm_g", "post_norm_g", "ln_v_g", "ln_v_b", "b_spatial", "sinks")}
    zone_in = lax.empty((N_DEV, depth * in_shard, D_MODEL), BF16)
    zone_out = lax.empty((N_DEV, depth * out_shard, D_MODEL), BF16)
    zone_ws = lax.empty((N_DEV, depth * A_GROUPS * CHUNK, CHUNK), F32)
    started_in, started_out = [], []
    dbias = jnp.zeros((B_HEADS, BLOCK, BLOCK), F32)
    for layer in reversed(range(depth)):
        h, pa, pb, ya, yb, y, pw, spw, kept, rstd_w = saved[layer]
        w, wo = weights[layer]
        dya, dyb, dwo, dpost = _outproj_bwd(g, y, ya, yb, wo, post_g3, layer)
        send_out = dwo.reshape(N_DEV, out_shard, D_MODEL)
        sems, sends, (zone_out,), ln_g_bwd = _exchange_start(
            [send_out], [zone_out], [layer * out_shard], f"grads_send_out_{layer}", ln_v_g.reshape(depth, 1, A_WIDTH))
        started_out.append((sems, sends, [layer * out_shard]))
        dqz, dkv, dbias, dsink = _attn_bwd(pb, dyb, pw, spw, dbias, layer)
        da, dw, dws, dbs, dlg, dlb = _gmlp_bwd_dw(pa, kept, rstd_w, dya, h, dqz, dkv, _hbm(ln_g_bwd), ln_b3, wmt, layer)
        send_in = dw.reshape(N_DEV, in_shard, D_MODEL)
        row_offs = [layer * A_GROUPS * CHUNK, layer * in_shard]
        sems, sends, (zone_ws, zone_in), pre_g_bwd = _exchange_start(
            [dws.reshape(A_GROUPS * CHUNK, CHUNK), send_in], [zone_ws, zone_in], row_offs, f"grads_send_in_{layer}",
            pre_norm_g.reshape(depth, 1, D_MODEL))
        started_in.append((sems, sends, row_offs))
        g, dpre = _inproj_bwd_dx(da, dqz, dkv, w, xs[layer], g, _hbm(pre_g_bwd), layer)
        grads["b_spatial"][layer] = dbs[:, :A_GROUPS].T
        grads["ln_v_g"][layer] = dlg[0]
        grads["ln_v_b"][layer] = dlb[0]
        grads["sinks"][layer] = dsink[0, :B_HEADS]
        grads["pre_norm_g"][layer] = dpre[0]
        grads["post_norm_g"][layer] = dpost[0]
    grad_x = g.reshape(x.shape)
    col_bucket = jnp.asarray(np.broadcast_to(_bucket_table()[0:1, ::-1], (B_HEADS, BLOCK)))
    flip = jnp.asarray(np.eye(BLOCK, dtype=np.float32)[::-1], dtype=BF16)
    drel = _rel_bias_grad(dbias, col_bucket, flip)[:, :REL_BUCKETS].T

    (recv_out,) = _exchange_wait(started_out, [zone_out], g, "grads_wait_out")
    res_out = _reduce_adamw(recv_out, w_out.reshape(-1, D_MODEL), m_w_out.reshape(-1, D_MODEL),
                            v_w_out.reshape(-1, D_MODEL), "adamw_w_out")
    res_out = [r.reshape(w_out.shape) for r in res_out]

    no_state = jnp.zeros((1,), F32)
    small_w = dict(pre_norm_g=pre_norm_g, post_norm_g=post_norm_g, ln_v_g=ln_v_g,
                   ln_v_b=ln_v_b, b_spatial=b_spatial, sinks=sinks, rel_bias=rel_bias, loss=no_state)
    small_m = dict(pre_norm_g=m_pre_norm_g, post_norm_g=m_post_norm_g, ln_v_g=m_ln_v_g,
                   ln_v_b=m_ln_v_b, b_spatial=m_b_spatial, sinks=m_sinks, rel_bias=m_rel_bias, loss=no_state)
    small_v = dict(pre_norm_g=v_pre_norm_g, post_norm_g=v_post_norm_g, ln_v_g=v_ln_v_g,
                   ln_v_b=v_ln_v_b, b_spatial=v_b_spatial, sinks=v_sinks, rel_bias=v_rel_bias, loss=no_state)
    small_g = {name: jnp.stack(grads[name]) for name in _SMALL if name not in ("rel_bias", "loss")}
    small_g["rel_bias"] = drel
    small_g["loss"] = loss_part[0, :1]
    shapes = {name: small_w[name].shape for name in _SMALL}
    slots = _all_gather_direct(_pack_small(small_g), "small_grads_all_gather")
    res_small = _reduce_adamw(slots, _pack_small(small_w), _pack_small(small_m), _pack_small(small_v), "adamw_small")

    recv_ws, recv_in = _exchange_wait(started_in, [zone_ws, zone_in], res_small[0], "grads_wait_in")
    res_ws = _reduce_adamw(recv_ws, w_spatial.reshape(-1, CHUNK), m_w_spatial.reshape(-1, CHUNK),
                           v_w_spatial.reshape(-1, CHUNK), "adamw_w_spatial")
    res_ws = [r.reshape(w_spatial.shape) for r in res_ws]
    res_in = _reduce_adamw(recv_in, w_in_t.reshape(-1, D_MODEL), m_w_in_t.reshape(-1, D_MODEL),
                           v_w_in_t.reshape(-1, D_MODEL), "adamw_w_in")
    res_in = [jnp.swapaxes(r.reshape(w_in_t.shape), 1, 2) for r in res_in]
    res_small = [_unpack_small(r, shapes) for r in res_small]

    order = ("pre_norm_g", "w_in", "ln_v_g", "ln_v_b", "w_spatial", "b_spatial", "sinks", "rel_bias", "w_out",
             "post_norm_g")
    outs = [res_small[0]["loss"][0], grad_x]
    for kind in range(4):
        for name in order:
            if name == "w_in":
                outs.append(res_in[kind])
            elif name == "w_out":
                outs.append(res_out[kind])
            elif name == "w_spatial":
                outs.append(res_ws[kind])
            else:
                outs.append(res_small[kind][name])
    return tuple(outs)
```

```python
import math

import numpy as np
import jax
import jax.numpy as jnp
from jax import lax
from jax.experimental import pallas as pl
from jax.experimental.pallas import tpu as pltpu

F32 = jnp.float32
BF16 = jnp.bfloat16

D_MODEL = 1024
A_WIDTH = 512
A_GROUPS = 8
CHUNK = 128
B_HEADS = 8
HEAD_DIM = 64
B_WIDTH = 512
KV_WIDTH = 128
BLOCK = 128
REL_BUCKETS = 32
REL_MAX_DIST = 128
NORM_EPS = 1e-6
PA_WIDTH = 3 * A_WIDTH
PB_WIDTH = 2 * B_WIDTH + 2 * KV_WIDTH
IN_WIDTH = PA_WIDTH + PB_WIDTH
K_OFF, BZ_OFF = B_WIDTH, B_WIDTH + 2 * KV_WIDTH
KEPT_WIDTH = 5 * A_WIDTH
SCALE = HEAD_DIM ** -0.5
NEG = -1e30
N_DEV = 8
LANES = 128

ADAM_LR = 0.001
ADAM_B1 = 0.9
ADAM_B2 = 0.999
ADAM_EPS = 1e-08
ADAM_WD = 0.01
ADAM_STEP = 10

V7X_VMEM_BYTES = 64 * 1024 * 1024
VMEM_TEMP_BYTES = 12 * 1024 * 1024
MESH = pl.DeviceIdType.MESH


def _vmem_limit(block_bytes, scratch_bytes=0):
    need = 2 * sum(block_bytes) + scratch_bytes + VMEM_TEMP_BYTES
    return int(min(need, V7X_VMEM_BYTES - 8 * 1024 * 1024))


def _nbytes(shape, dtype):
    return int(np.prod(shape)) * jnp.dtype(dtype).itemsize


def _token_tile(tokens, cap=512):
    tile = min(cap, tokens // 2)
    assert tokens % tile == 0 and tile % CHUNK == 0, tokens
    return tile


def _row_tile(rows, cap=512):
    best = 8
    for t in range(8, cap + 1, 8):
        if rows % t == 0:
            best = t
    assert rows % best == 0, rows
    return best


def _mm(a, b):
    return lax.dot_general(a, b, (((1,), (0,)), ((), ())), preferred_element_type=F32)


def _mm_nt(a, b):
    return lax.dot_general(a, b, (((1,), (1,)), ((), ())), preferred_element_type=F32)


def _mm_tn(a, b):
    return lax.dot_general(a, b, (((0,), (0,)), ((), ())), preferred_element_type=F32)


_GELU_C = math.sqrt(2.0 / math.pi)


_GELU_A = _GELU_C * 0.044715


def _gelu_parts(x):
    x2 = x * x
    t = jnp.tanh(x * (_GELU_C + _GELU_A * x2))
    return x2, t, 0.5 + 0.5 * t


def _gelu_and_grad(x):
    x2, t, half_plus = _gelu_parts(x)
    grad = half_plus + (0.5 * x) * (1.0 - t * t) * (_GELU_C + (3.0 * _GELU_A) * x2)
    return x * half_plus, grad


def _sigmoid(x):
    return 0.5 + 0.5 * jnp.tanh(0.5 * x)


def _bucket_table():
    q = np.arange(BLOCK)[:, None]
    k = np.arange(BLOCK)[None, :]
    dist = np.where(k <= q, q - k, q + BLOCK - k)
    max_exact = REL_BUCKETS // 2
    safe = np.maximum(dist, 1).astype(np.float32)
    large = max_exact + (np.log(safe / np.float32(max_exact)) / np.float32(math.log(REL_MAX_DIST / max_exact))
                         * np.float32(REL_BUCKETS - max_exact)).astype(np.int32)
    large = np.minimum(large, REL_BUCKETS - 1)
    assert dist.min() >= 0 and dist.max() < BLOCK
    return np.where(dist < max_exact, dist, large).astype(np.int32)


def _hbm(x):
    return pltpu.with_memory_space_constraint(x, pltpu.HBM)


def _slot(px, py, pc):
    return 4 * px + 2 * py + pc


def _all_gather(arrs, name):
    n = len(arrs)

    def body(*refs):
        ins, outs = refs[:n], refs[n:2 * n]
        send_sems, recv_sems, local_sems = refs[2 * n:]
        x, y, c = lax.axis_index("x"), lax.axis_index("y"), lax.axis_index("c")
        me, sibling = (x, y, c), (x, y, 1 - c)
        chips = [(1 - x, y), (x, 1 - y), (1 - x, 1 - y)]

        def copy(a, k, block, to, src=None):
            dst = outs[a].at[_slot(*block)]
            return pltpu.make_async_remote_copy(
                src_ref=dst if src is None else src, dst_ref=dst,
                send_sem=send_sems.at[7 * a + k], recv_sem=recv_sems.at[7 * a + k],
                device_id=to, device_id_type=MESH)

        mine = [pltpu.make_async_copy(ins[a], outs[a].at[_slot(*me)], local_sems.at[a]) for a in range(n)]
        for cp in mine:
            cp.start()
        first = []
        for a in range(n):
            first.append(copy(a, 0, me, sibling, src=ins[a]))
            first += [copy(a, 1 + j, me, (*chip, c), src=ins[a]) for j, chip in enumerate(chips)]
        for cp in first:
            cp.start()
        passed = []
        for j, chip in enumerate(chips):
            for a in range(n):
                copy(a, 1 + j, (*chip, c), me).wait_recv()
                fwd = copy(a, 4 + j, (*chip, c), sibling)
                fwd.start()
                passed.append(fwd)
        for a in range(n):
            copy(a, 0, sibling, me).wait_recv()
            for j, chip in enumerate(chips):
                copy(a, 4 + j, (*chip, 1 - c), me).wait_recv()
        for cp in first + passed:
            cp.wait_send()
        for cp in mine:
            cp.wait()

    any_spec = pl.BlockSpec(memory_space=pl.ANY)
    return pl.pallas_call(
        body, name=name,
        out_shape=[jax.ShapeDtypeStruct((N_DEV,) + a.shape, a.dtype) for a in arrs],
        in_specs=[any_spec] * n, out_specs=[any_spec] * n,
        scratch_shapes=[pltpu.SemaphoreType.DMA((7 * n,)), pltpu.SemaphoreType.DMA((7 * n,)),
                        pltpu.SemaphoreType.DMA((n,))],
    )(*arrs)


def _all_gather_direct(arr, name):
    def body(in_ref, out_ref, send_sems, recv_sems, local_sem):
        me, peers = _peers()
        mine = pltpu.make_async_copy(in_ref, out_ref.at[_slot(*me)], local_sem)
        mine.start()

        def copy(k, origin, to):
            return pltpu.make_async_remote_copy(
                src_ref=in_ref, dst_ref=out_ref.at[_slot(*origin)], send_sem=send_sems.at[k],
                recv_sem=recv_sems.at[k], device_id=to, device_id_type=MESH)

        sends = [copy(k, me, peer) for k, peer in enumerate(peers)]
        for cp in sends:
            cp.start()
        for k, peer in enumerate(peers):
            copy(k, peer, me).wait_recv()
        for cp in sends:
            cp.wait_send()
        mine.wait()

    any_spec = pl.BlockSpec(memory_space=pl.ANY)
    return pl.pallas_call(
        body, name=name,
        out_shape=jax.ShapeDtypeStruct((N_DEV,) + arr.shape, arr.dtype),
        in_specs=[any_spec], out_specs=any_spec,
        scratch_shapes=[pltpu.SemaphoreType.DMA((7,)), pltpu.SemaphoreType.DMA((7,)), pltpu.SemaphoreType.DMA],
    )(arr)


_HBM_SPEC = pl.BlockSpec(memory_space=pltpu.HBM)
_SEM_SPEC = pl.BlockSpec(memory_space=pltpu.SEMAPHORE)
_DATAFLOW = pltpu.SideEffectType.DATAFLOW_SIDE_EFFECTING


def _peers():
    x, y, c = lax.axis_index("x"), lax.axis_index("y"), lax.axis_index("c")
    peers = []
    for k in range(1, N_DEV):
        fx, fy, fc = (k >> 2) & 1, (k >> 1) & 1, k & 1
        peers.append((1 - x if fx else x, 1 - y if fy else y, 1 - c if fc else c))
    return (x, y, c), peers


def _exchange_copy(send_ref, land_ref, row_off, src_slot, dst_slot, sems, idx, peer):
    rows = send_ref.shape[-2]
    src = send_ref.at[src_slot] if len(send_ref.shape) == 3 else send_ref
    return pltpu.make_async_remote_copy(
        src_ref=src, dst_ref=land_ref.at[dst_slot, pl.ds(row_off, rows), :],
        send_sem=sems[0].at[idx], recv_sem=sems[1].at[idx], device_id=peer, device_id_type=MESH)


def _own_copy(send_ref, land_ref, row_off, slot, sem):
    rows = send_ref.shape[-2]
    src = send_ref.at[slot] if len(send_ref.shape) == 3 else send_ref
    return pltpu.make_async_copy(src, land_ref.at[slot, pl.ds(row_off, rows), :], sem)


def _exchange_start(sends, lands, row_offs, name, carry, after=None):
    n = len(sends)

    def body(*refs):
        ins, zones, carry_ref = refs[:n], refs[n:2 * n], refs[2 * n]
        first_out = 2 * n + 1 + (after is not None)
        sems = refs[first_out:first_out + 3]
        carry_out = refs[-1]
        me, peers = _peers()
        for a in range(n):
            for k, peer in enumerate(peers):
                _exchange_copy(ins[a], zones[a], row_offs[a], _slot(*peer), _slot(*me), sems, 7 * a + k, peer).start()
            _own_copy(ins[a], zones[a], row_offs[a], _slot(*me), sems[2].at[a]).start()
        carry_out[...] = carry_ref[...]

    arrays = [_hbm(a) for a in list(sends) + list(lands)]
    vmem = pl.BlockSpec(memory_space=pltpu.VMEM)
    out = pl.pallas_call(
        body, name=name,
        out_shape=(pltpu.SemaphoreType.DMA((7 * n,)), pltpu.SemaphoreType.DMA((7 * n,)), pltpu.SemaphoreType.DMA((n,)),
                   *[pltpu.HBM(a.shape, a.dtype) for a in arrays], jax.ShapeDtypeStruct(carry.shape, carry.dtype)),
        in_specs=[_HBM_SPEC] * (2 * n) + [vmem] + ([pl.BlockSpec(memory_space=pl.ANY)] if after is not None else []),
        out_specs=(_SEM_SPEC, _SEM_SPEC, _SEM_SPEC, *[_HBM_SPEC] * (2 * n), vmem),
        input_output_aliases={i: 3 + i for i in range(2 * n)},
        compiler_params=pltpu.CompilerParams(has_side_effects=_DATAFLOW),
    )(*arrays, carry, *([after] if after is not None else []))
    return (out[0], out[1], out[2]), list(out[3:3 + n]), list(out[3 + n:3 + 2 * n]), out[-1]


def _exchange_wait(started, lands, after, name):
    n = len(lands)
    flat_sends = [s for _, sends, _, _ in started for s in sends]
    flat_sems = [s for sems, _, _, _ in started for s in sems]
    ns = len(flat_sends)

    def body(*refs):
        ins, zones = refs[:ns], refs[ns:ns + n]
        sem_refs = refs[ns + n:ns + n + len(flat_sems)]
        me, peers = _peers()
        pos = 0
        for call, (_, sends, row_offs, zone_ids) in enumerate(started):
            sems = sem_refs[3 * call:3 * call + 3]
            for a in range(len(sends)):
                zone = zones[zone_ids[a]]
                for k, peer in enumerate(peers):
                    cp = _exchange_copy(ins[pos + a], zone, row_offs[a], _slot(*peer), _slot(*peer), sems,
                                        7 * a + k, peer)
                    cp.wait_send()
                    cp.wait_recv()
                _own_copy(ins[pos + a], zone, row_offs[a], _slot(*me), sems[2].at[a]).wait()
            pos += len(sends)

    arrays = list(flat_sends) + list(lands)
    out = pl.pallas_call(
        body, name=name,
        out_shape=tuple(pltpu.HBM(a.shape, a.dtype) for a in arrays),
        in_specs=[_HBM_SPEC] * len(arrays) + [_SEM_SPEC] * len(flat_sems) + [pl.BlockSpec(memory_space=pl.ANY)],
        out_specs=tuple([_HBM_SPEC] * len(arrays)),
        input_output_aliases={i: i for i in range(len(arrays))},
        compiler_params=pltpu.CompilerParams(has_side_effects=_DATAFLOW),
    )(*arrays, *flat_sems, after)
    return list(out[ns:])


def _bias_table(rel_bias, bucket):
    def body(rel_ref, bucket_ref, out_ref):
        bk = bucket_ref[...]
        for h in range(B_HEADS):
            def pick(b, acc, h=h):
                return jnp.where(bk == b, rel_ref[b, h], acc)
            out_ref[h] = lax.fori_loop(0, REL_BUCKETS, pick, jnp.zeros((BLOCK, BLOCK), F32))

    return pl.pallas_call(
        body, name="bias_table",
        out_shape=jax.ShapeDtypeStruct((B_HEADS, BLOCK, BLOCK), F32),
        in_specs=[pl.BlockSpec(memory_space=pltpu.SMEM), pl.BlockSpec(memory_space=pltpu.VMEM)],
        out_specs=pl.BlockSpec(memory_space=pltpu.VMEM),
    )(rel_bias, bucket)


def _inproj_gmlp_fwd(x2, pre_g3, w, ln_g3, ln_b3, wm, bsb, layer):
    tokens = x2.shape[0]
    tm = _token_tile(tokens)

    def body(x_ref, g_ref, w_ref, lg_ref, lb_ref, wm_ref, bsb_ref, h_ref, pa_ref, pb_ref, ya_ref, kept_ref, rstd_ref):
        x = x_ref[...]
        r = lax.rsqrt(jnp.mean(x * x, axis=-1, keepdims=True) + NORM_EPS)
        h = (x * r * g_ref[...]).astype(BF16)
        h_ref[...] = h
        pa_ref[...] = _mm_nt(h, w_ref[0:PA_WIDTH, :]).astype(BF16)
        pb_ref[...] = _mm_nt(h, w_ref[PA_WIDTH:IN_WIDTH, :]).astype(BF16)
        for ci in range(tm // CHUNK):
            rows = slice(ci * CHUNK, (ci + 1) * CHUNK)
            gu, gv, pz, u, vhat, rstd, _, mixed, sg = _gmlp_forward_chunk(
                pa_ref[rows, :], lg_ref[...], lb_ref[...], wm_ref, bsb_ref[...])
            ya_ref[rows, :] = (u * mixed * (pz * sg)).astype(BF16)
            for k, val in enumerate((u, gu, gv, vhat, mixed)):
                kept_ref[rows, A_WIDTH * k:A_WIDTH * (k + 1)] = val.astype(BF16)
            rstd_ref[rows, :] = jnp.broadcast_to(rstd, (CHUNK, LANES))

    blocks = [_nbytes((tm, D_MODEL), F32), _nbytes((D_MODEL, IN_WIDTH), BF16),
              _nbytes((tm, D_MODEL), BF16), _nbytes((tm, PA_WIDTH), BF16), _nbytes((tm, PB_WIDTH), BF16),
              _nbytes((A_GROUPS, CHUNK, CHUNK), BF16), _nbytes((CHUNK, A_WIDTH), F32), _nbytes((tm, A_WIDTH), BF16)]
    return pl.pallas_call(
        body, name=f"inproj_gmlp_fwd_{layer}", grid=(tokens // tm,),
        in_specs=[pl.BlockSpec((tm, D_MODEL), lambda i: (i, 0)),
                  pl.BlockSpec((None, 1, D_MODEL), lambda i: (layer, 0, 0)),
                  pl.BlockSpec((IN_WIDTH, D_MODEL), lambda i: (0, 0)),
                  pl.BlockSpec((None, 1, A_WIDTH), lambda i: (layer, 0, 0)),
                  pl.BlockSpec((None, 1, A_WIDTH), lambda i: (layer, 0, 0)),
                  pl.BlockSpec((None, A_GROUPS, CHUNK, CHUNK), lambda i: (layer, 0, 0, 0)),
                  pl.BlockSpec((None, CHUNK, A_WIDTH), lambda i: (layer, 0, 0))],
        out_specs=[pl.BlockSpec((tm, D_MODEL), lambda i: (i, 0)),
                   pl.BlockSpec((tm, PA_WIDTH), lambda i: (i, 0)),
                   pl.BlockSpec((tm, PB_WIDTH), lambda i: (i, 0)),
                   pl.BlockSpec((tm, A_WIDTH), lambda i: (i, 0)),
                   pl.BlockSpec((tm, KEPT_WIDTH), lambda i: (i, 0)),
                   pl.BlockSpec((tm, LANES), lambda i: (i, 0))],
        out_shape=[pltpu.HBM((tokens, D_MODEL), BF16),
                   pltpu.HBM((tokens, PA_WIDTH), BF16),
                   pltpu.HBM((tokens, PB_WIDTH), BF16),
                   pltpu.HBM((tokens, A_WIDTH), BF16),
                   pltpu.HBM((tokens, KEPT_WIDTH), BF16),
                   pltpu.HBM((tokens, LANES), F32)],
        compiler_params=pltpu.CompilerParams(dimension_semantics=("parallel",),
                                             vmem_limit_bytes=_vmem_limit(blocks + [_nbytes((tm, KEPT_WIDTH), BF16),
                                                                                    _nbytes((tm, LANES), F32)])),
    )(_hbm(x2), pre_g3, _hbm(w), ln_g3, ln_b3, wm, bsb)


def _gmlp_forward_chunk(pa, ln_g, ln_b, wm_ref, bsb):
    pu = pa[:, 0:A_WIDTH].astype(F32)
    pv = pa[:, A_WIDTH:2 * A_WIDTH].astype(F32)
    pz = pa[:, 2 * A_WIDTH:3 * A_WIDTH].astype(F32)
    u, gu = _gelu_and_grad(pu)
    vv, gv = _gelu_and_grad(pv)
    mu = jnp.mean(vv, axis=-1, keepdims=True)
    xc = vv - mu
    rstd = lax.rsqrt(jnp.mean(xc * xc, axis=-1, keepdims=True) + NORM_EPS)
    vhat = xc * rstd
    vnb = (vhat * ln_g + ln_b).astype(BF16)
    low = lax.broadcasted_iota(jnp.int32, (CHUNK, LANES), 1) < HEAD_DIM
    parts = []
    for p in range(A_GROUPS // 2):
        vp = vnb[:, LANES * p:LANES * (p + 1)]
        parts.append(jnp.where(low, _mm(wm_ref[2 * p], vp), _mm(wm_ref[2 * p + 1], vp)))
    mixed = jnp.concatenate(parts, axis=1) + bsb
    sg = _sigmoid(pz)
    return gu, gv, pz, u, vhat, rstd, vnb, mixed, sg


def _kv_variants(kv):
    t = kv.astype(F32)
    rolled = pltpu.roll(t, HEAD_DIM, 1)
    low = lax.broadcasted_iota(jnp.int32, t.shape, 1) < HEAD_DIM
    zero = jnp.zeros_like(t)
    head0 = (jnp.where(low, t, zero).astype(BF16), jnp.where(low, zero, rolled).astype(BF16))
    head1 = (jnp.where(low, rolled, zero).astype(BF16), jnp.where(low, zero, t).astype(BF16))
    return head0, head1


def _band_masks():
    row = lax.broadcasted_iota(jnp.int32, (BLOCK, BLOCK), 0)
    col = lax.broadcasted_iota(jnp.int32, (BLOCK, BLOCK), 1)
    return col <= row


def _wrap(full, tri):
    return jnp.where(tri, full[:, BLOCK:2 * BLOCK], full[:, 0:BLOCK])


def _attn_probs(sf, bias_h, sink, tri, kill):
    s = _wrap(sf, tri) + bias_h
    s = jnp.where(kill, NEG, s)
    m = jnp.maximum(jnp.max(s, axis=-1, keepdims=True), sink)
    e = jnp.exp(s - m)
    es = jnp.exp(sink - m)
    inv = 1.0 / (jnp.sum(e, axis=-1, keepdims=True) + es)
    return e * inv, es * inv


def _unwrap16(p, tri):
    p = p.astype(BF16)
    zero = jnp.zeros_like(p)
    return jnp.concatenate([jnp.where(tri, zero, p), jnp.where(tri, p, zero)], axis=1)


def _fill_kv(kv_ref, prev_ref, cur_ref):
    kv_ref[0:BLOCK, :] = prev_ref[...]
    kv_ref[BLOCK:, :] = cur_ref[:, K_OFF:K_OFF + 2 * KV_WIDTH]


def _attn_outproj_fwd(pb, bias, sinks, ya, wo, x2, post_g3, b_loc, layer, target=None):
    tokens = pb.shape[0]
    nb = tokens // b_loc // BLOCK
    tq = _token_tile(tokens)
    per_tile = tq // BLOCK
    steps = tokens // tq
    with_loss = target is not None
    half = D_MODEL // 2

    def body(*refs):
        sink_ref, cur_ref, prev_ref, bias_ref, ya_ref, woa_ref, wob_ref, x_ref, g_ref = refs[:9]
        if with_loss:
            t_ref, yb_ref, pw_ref, spw_ref, y_ref, grad_ref, loss_ref, kv_ref, acc_ref = refs[9:]
        else:
            yb_ref, pw_ref, spw_ref, y_ref, xn_ref, kv_ref = refs[9:]
        t = pl.program_id(0)
        _fill_kv(kv_ref, prev_ref, cur_ref)
        tri = _band_masks()
        head_lane = lax.broadcasted_iota(jnp.int32, (BLOCK, LANES), 1)

        def block(i, carry):
            start = i * BLOCK
            rows = pl.ds(start, BLOCK)
            first = lax.rem(t * per_tile + i, nb) == 0
            kill = jnp.logical_and(first, jnp.logical_not(tri))
            kv = kv_ref[pl.ds(start, 2 * BLOCK), :]
            k_ops = _kv_variants(kv[:, 0:KV_WIDTH])
            v_ops = _kv_variants(kv[:, KV_WIDTH:2 * KV_WIDTH])
            p16 = {}
            sink_cols = jnp.zeros((BLOCK, LANES), F32)
            for kvh in range(2):
                pairs = (2 * kvh, 2 * kvh + 1)
                qs = jnp.concatenate([cur_ref[rows, LANES * p:LANES * (p + 1)] for p in pairs], axis=0) * SCALE
                for j in range(2):
                    sf = _mm_nt(qs, k_ops[kvh][j])
                    for r, p in enumerate(pairs):
                        hd = 2 * p + j
                        probs, sink_p = _attn_probs(sf[BLOCK * r:BLOCK * (r + 1)], bias_ref[hd],
                                                    sink_ref[layer, hd], tri, kill)
                        probs = probs.astype(BF16)
                        p16[hd] = _unwrap16(probs, tri)
                        pw_ref[rows, LANES * hd:LANES * (hd + 1)] = probs
                        sink_cols = jnp.where(head_lane == hd, sink_p, sink_cols)
            spw_ref[rows, :] = sink_cols
            for kvh in range(2):
                pairs = (2 * kvh, 2 * kvh + 1)
                out = jnp.zeros((2 * BLOCK, LANES), F32)
                for j in range(2):
                    out = out + _mm(jnp.concatenate([p16[2 * p + j] for p in pairs], axis=0), v_ops[kvh][j])
                for r, p in enumerate(pairs):
                    bz = cur_ref[rows, BZ_OFF + LANES * p:BZ_OFF + LANES * (p + 1)].astype(F32)
                    yb_ref[rows, LANES * p:LANES * (p + 1)] = (
                        out[BLOCK * r:BLOCK * (r + 1)] * (bz * _sigmoid(bz))).astype(BF16)
            y = _mm(ya_ref[rows, :], woa_ref[...]) + _mm(yb_ref[rows, :], wob_ref[...])
            r = lax.rsqrt(jnp.mean(y * y, axis=-1, keepdims=True) + NORM_EPS)
            x_next = x_ref[rows, :] + y * r * g_ref[...]
            y_ref[rows, :] = y.astype(BF16)
            if not with_loss:
                xn_ref[rows, :] = x_next
            else:
                err = x_next - t_ref[rows, :]
                grad_ref[rows, :] = err * (1.0 / D_MODEL)
                acc_ref[...] += jnp.sum(err * err, axis=0, keepdims=True)
            return carry

        if with_loss:
            @pl.when(t == 0)
            def _():
                acc_ref[...] = jnp.zeros_like(acc_ref)

        for i in range(per_tile):
            block(i, 0)

        if not with_loss:
            return

        @pl.when(t == steps - 1)
        def _():
            total = jnp.sum(acc_ref[...], axis=-1, keepdims=True) * (0.5 / D_MODEL)
            loss_ref[...] = jnp.broadcast_to(total, loss_ref.shape)

    tile = pl.BlockSpec((tq, D_MODEL), lambda t: (t, 0))
    blocks = [_nbytes((tq, PB_WIDTH), BF16), _nbytes((BLOCK, 2 * KV_WIDTH), BF16),
              _nbytes((B_HEADS, BLOCK, BLOCK), F32), 2 * _nbytes((tq, B_WIDTH), BF16),
              2 * _nbytes((half, D_MODEL), BF16), 3 * _nbytes((tq, D_MODEL), F32), _nbytes((tq, D_MODEL), BF16),
              _nbytes((tq, B_HEADS * BLOCK), BF16), _nbytes((tq, LANES), F32)]
    scratch = _nbytes((tq + BLOCK, 2 * KV_WIDTH), BF16)
    in_specs = [pl.BlockSpec(memory_space=pltpu.SMEM),
                pl.BlockSpec((tq, PB_WIDTH), lambda t: (t, 0)),
                pl.BlockSpec((BLOCK, 2 * KV_WIDTH),
                             lambda t: (jnp.maximum(t * per_tile - 1, 0), K_OFF // (2 * KV_WIDTH))),
                pl.BlockSpec((B_HEADS, BLOCK, BLOCK), lambda t: (0, 0, 0)),
                pl.BlockSpec((tq, half), lambda t: (t, 0)),
                pl.BlockSpec((half, D_MODEL), lambda t: (0, 0)),
                pl.BlockSpec((half, D_MODEL), lambda t: (1, 0)),
                tile,
                pl.BlockSpec((None, 1, D_MODEL), lambda t: (layer, 0, 0))]
    out_specs = [pl.BlockSpec((tq, B_WIDTH), lambda t: (t, 0)), pl.BlockSpec((tq, B_HEADS * BLOCK), lambda t: (t, 0)),
                 pl.BlockSpec((tq, LANES), lambda t: (t, 0)), tile, tile]
    out_shape = [pltpu.HBM((tokens, B_WIDTH), BF16), pltpu.HBM((tokens, B_HEADS * BLOCK), BF16),
                 pltpu.HBM((tokens, LANES), F32), pltpu.HBM((tokens, D_MODEL), BF16),
                 pltpu.HBM((tokens, D_MODEL), F32)]
    scratch_shapes = [pltpu.VMEM((tq + BLOCK, 2 * KV_WIDTH), BF16)]
    operands = [sinks, _hbm(pb), _hbm(pb), _hbm(bias), _hbm(ya), _hbm(wo), _hbm(wo), _hbm(x2), post_g3]
    if with_loss:
        in_specs.append(tile)
        out_specs.append(pl.BlockSpec((1, LANES), lambda t: (0, 0)))
        out_shape.append(pltpu.HBM((1, LANES), F32))
        scratch_shapes.append(pltpu.VMEM((1, D_MODEL), F32))
        operands.append(_hbm(target))
    return pl.pallas_call(
        body, name=f"attn_outproj_fwd_{layer}", grid=(steps,),
        in_specs=in_specs, out_specs=out_specs, out_shape=out_shape, scratch_shapes=scratch_shapes,
        compiler_params=pltpu.CompilerParams(dimension_semantics=("arbitrary" if with_loss else "parallel",),
                                             vmem_limit_bytes=_vmem_limit(blocks, scratch)),
    )(*operands)


def _outproj_bwd(g, y, ya, yb, wo, post_g3, layer):
    tokens = g.shape[0]
    tm = _token_tile(tokens, cap=1024)
    half = D_MODEL // 2
    steps = tokens // tm

    def body(g_ref, y_ref, ya_ref, yb_ref, woa_ref, wob_ref, pg_ref, dya_ref, dyb_ref, dwo16_ref, dpg_ref, dwo_ref):
        @pl.when(pl.program_id(0) == 0)
        def _():
            dwo_ref[...] = jnp.zeros_like(dwo_ref)
            dpg_ref[...] = jnp.zeros_like(dpg_ref)

        gv = g_ref[...]
        yf = y_ref[...].astype(F32)
        r = lax.rsqrt(jnp.mean(yf * yf, axis=-1, keepdims=True) + NORM_EPS)
        yhat = yf * r
        gg = gv * pg_ref[...]
        dy = r * (gg - yhat * jnp.mean(gg * yhat, axis=-1, keepdims=True))
        dpg_ref[...] += jnp.sum(gv * yhat, axis=0, keepdims=True)
        dyb16 = dy.astype(BF16)
        dya_ref[...] = _mm_nt(dyb16, woa_ref[...]).astype(BF16)
        dyb_ref[...] = _mm_nt(dyb16, wob_ref[...]).astype(BF16)
        dwo_ref[0:half, :] += _mm_tn(ya_ref[...], dyb16)
        dwo_ref[half:D_MODEL, :] += _mm_tn(yb_ref[...], dyb16)

        @pl.when(pl.program_id(0) == steps - 1)
        def _():
            dwo16_ref[...] = dwo_ref[...].astype(BF16)

    blocks = [_nbytes((tm, D_MODEL), F32), _nbytes((tm, D_MODEL), BF16), 4 * _nbytes((tm, half), BF16),
              2 * _nbytes((half, D_MODEL), BF16), _nbytes((D_MODEL, D_MODEL), BF16)]
    return pl.pallas_call(
        body, name=f"outproj_bwd_{layer}", grid=(tokens // tm,),
        in_specs=[pl.BlockSpec((tm, D_MODEL), lambda i: (i, 0)),
                  pl.BlockSpec((tm, D_MODEL), lambda i: (i, 0)),
                  pl.BlockSpec((tm, half), lambda i: (i, 0)),
                  pl.BlockSpec((tm, half), lambda i: (i, 0)),
                  pl.BlockSpec((half, D_MODEL), lambda i: (0, 0)),
                  pl.BlockSpec((half, D_MODEL), lambda i: (1, 0)),
                  pl.BlockSpec((None, 1, D_MODEL), lambda i: (layer, 0, 0))],
        out_specs=[pl.BlockSpec((tm, half), lambda i: (i, 0)),
                   pl.BlockSpec((tm, half), lambda i: (i, 0)),
                   pl.BlockSpec((D_MODEL, D_MODEL), lambda i: (0, 0)),
                   pl.BlockSpec((1, D_MODEL), lambda i: (0, 0))],
        out_shape=[pltpu.HBM((tokens,half), BF16),
                   pltpu.HBM((tokens,half), BF16),
                   pltpu.HBM((D_MODEL,D_MODEL), BF16),
                   pltpu.HBM((1,D_MODEL), F32)],
        scratch_shapes=[pltpu.VMEM((D_MODEL, D_MODEL), F32)],
        compiler_params=pltpu.CompilerParams(dimension_semantics=("arbitrary",),
                                             vmem_limit_bytes=_vmem_limit(blocks, _nbytes((D_MODEL, D_MODEL), F32))),
    )(_hbm(g), _hbm(y), _hbm(ya), _hbm(yb), _hbm(wo), _hbm(wo), post_g3)


def _gmlp_bwd_dw(pa, kept, rstd_w, dya, h, dqz, dkv, ln_g3, ln_b3, wmt, layer):
    tokens = pa.shape[0]
    tc = _token_tile(tokens)
    steps = tokens // tc

    def body(pz_ref, kept_ref, rstd_ref, dya_ref, h_ref, dqz_ref, dkv_ref, lg_ref, lb_ref, wmt_ref,
             da_ref, dw_ref, dws_ref, dbs_ref, dlg_ref, dlb_ref, acc_ref, dbsb_ref):
        i = pl.program_id(0)

        @pl.when(i == 0)
        def _():
            acc_ref[...] = jnp.zeros_like(acc_ref)
            dws_ref[...] = jnp.zeros_like(dws_ref)
            dlg_ref[...] = jnp.zeros_like(dlg_ref)
            dlb_ref[...] = jnp.zeros_like(dlb_ref)
            dbsb_ref[...] = jnp.zeros_like(dbsb_ref)

        ln_g = lg_ref[...]
        low = lax.broadcasted_iota(jnp.int32, (CHUNK, LANES), 1) < HEAD_DIM
        hv = h_ref[...]
        acc_ref[:, PA_WIDTH:PA_WIDTH + B_WIDTH] += _mm_tn(hv, dqz_ref[:, 0:B_WIDTH])
        acc_ref[:, PA_WIDTH + K_OFF:PA_WIDTH + BZ_OFF] += _mm_tn(hv, dkv_ref[...])
        acc_ref[:, PA_WIDTH + BZ_OFF:IN_WIDTH] += _mm_tn(hv, dqz_ref[:, B_WIDTH:2 * B_WIDTH])

        for ci in range(tc // CHUNK):
            rows = slice(ci * CHUNK, (ci + 1) * CHUNK)
            u, gu, gv, vhat, mixed = (kept_ref[rows, A_WIDTH * k:A_WIDTH * (k + 1)].astype(F32) for k in range(5))
            rstd = jnp.concatenate([rstd_ref[rows, :]] * (A_WIDTH // LANES), axis=1)
            vnb = (vhat * ln_g + lb_ref[...]).astype(BF16)
            pz = pz_ref[rows, :].astype(F32)
            sg = _sigmoid(pz)
            dy = dya_ref[rows, :].astype(F32)
            sz = pz * sg
            dy_sz = dy * sz
            du = dy_sz * mixed
            dmixed = dy_sz * u
            dz = dy * (u * mixed) * (sg + sz * (1.0 - sg))
            dbsb_ref[...] += dmixed
            dmb = dmixed.astype(BF16)
            zero = jnp.zeros((CHUNK, LANES), BF16)
            parts = []
            for p in range(A_GROUPS // 2):
                dmp = dmb[:, LANES * p:LANES * (p + 1)]
                vp = vnb[:, LANES * p:LANES * (p + 1)]
                parts.append(jnp.where(low, _mm(wmt_ref[2 * p], dmp), _mm(wmt_ref[2 * p + 1], dmp)))
                dws_ref[2 * p] += _mm_nt(jnp.where(low, dmp, zero), vp)
                dws_ref[2 * p + 1] += _mm_nt(jnp.where(low, zero, dmp), vp)
            dvn = jnp.concatenate(parts, axis=1)
            dlg_ref[...] += jnp.sum(dvn * vhat, axis=0, keepdims=True)
            dlb_ref[...] += jnp.sum(dvn, axis=0, keepdims=True)
            dvh = dvn * ln_g
            dvv = rstd * (dvh - jnp.mean(dvh, axis=-1, keepdims=True)
                          - vhat * jnp.mean(dvh * vhat, axis=-1, keepdims=True))
            da_ref[rows, 0:A_WIDTH] = (du * gu).astype(BF16)
            da_ref[rows, A_WIDTH:2 * A_WIDTH] = (dvv * gv).astype(BF16)
            da_ref[rows, 2 * A_WIDTH:3 * A_WIDTH] = dz.astype(BF16)

        acc_ref[:, 0:PA_WIDTH] += _mm_tn(hv, da_ref[...])

        @pl.when(i == steps - 1)
        def _():
            for c in range(IN_WIDTH // LANES):
                cols = slice(LANES * c, LANES * (c + 1))
                dw_ref[cols, :] = acc_ref[:, cols].T.astype(BF16)
            causal = (lax.broadcasted_iota(jnp.int32, (CHUNK, CHUNK), 0)
                      >= lax.broadcasted_iota(jnp.int32, (CHUNK, CHUNK), 1))
            for h in range(A_GROUPS):
                dws_ref[h] = jnp.where(causal, dws_ref[h], 0.0)
            acc = dbsb_ref[...]
            lane_full = lax.broadcasted_iota(jnp.int32, (CHUNK, A_WIDTH), 1)
            lane_out = lax.broadcasted_iota(jnp.int32, (CHUNK, LANES), 1)
            out = jnp.zeros((CHUNK, LANES), F32)
            for h in range(A_GROUPS):
                in_group = jnp.logical_and(lane_full >= HEAD_DIM * h, lane_full < HEAD_DIM * (h + 1))
                s = jnp.sum(jnp.where(in_group, acc, 0.0), axis=-1, keepdims=True)
                out = jnp.where(lane_out == h, s, out)
            dbs_ref[...] = out

    blocks = [_nbytes((tc, KEPT_WIDTH), BF16), 2 * _nbytes((tc, A_WIDTH), BF16), _nbytes((A_GROUPS, CHUNK, CHUNK), BF16),
              _nbytes((tc, LANES), F32), _nbytes((tc, PA_WIDTH), BF16), _nbytes((A_GROUPS, CHUNK, CHUNK), F32),
              _nbytes((CHUNK, LANES), F32), _nbytes((tc, D_MODEL), BF16), _nbytes((tc, 2 * B_WIDTH), BF16),
              _nbytes((tc, 2 * KV_WIDTH), BF16), _nbytes((D_MODEL, IN_WIDTH), BF16)]
    scratch = _nbytes((D_MODEL, IN_WIDTH), F32) + _nbytes((CHUNK, A_WIDTH), F32)
    in_specs = [pl.BlockSpec((tc, A_WIDTH), lambda i: (i, 2)),
                pl.BlockSpec((tc, KEPT_WIDTH), lambda i: (i, 0)),
                pl.BlockSpec((tc, LANES), lambda i: (i, 0)),
                pl.BlockSpec((tc, A_WIDTH), lambda i: (i, 0)),
                pl.BlockSpec((tc, D_MODEL), lambda i: (i, 0)),
                pl.BlockSpec((tc, 2 * B_WIDTH), lambda i: (i, 0)),
                pl.BlockSpec((pl.Element(tc), pl.Element(2 * KV_WIDTH)),
                             lambda i: (pl.multiple_of(i * tc + BLOCK, BLOCK), 0)),
                pl.BlockSpec((None, 1, A_WIDTH), lambda i: (layer, 0, 0)),
                pl.BlockSpec((None, 1, A_WIDTH), lambda i: (layer, 0, 0)),
                pl.BlockSpec((None, A_GROUPS, CHUNK, CHUNK), lambda i: (layer, 0, 0, 0))]
    operands = [_hbm(pa), _hbm(kept), _hbm(rstd_w), _hbm(dya), _hbm(h), _hbm(dqz), _hbm(dkv), ln_g3, ln_b3, wmt]
    out_specs = [pl.BlockSpec((tc, PA_WIDTH), lambda i: (i, 0)),
                 pl.BlockSpec((IN_WIDTH, D_MODEL), lambda i: (0, 0)),
                 pl.BlockSpec((A_GROUPS, CHUNK, CHUNK), lambda i: (0, 0, 0)),
                 pl.BlockSpec((CHUNK, LANES), lambda i: (0, 0)),
                 pl.BlockSpec((1, A_WIDTH), lambda i: (0, 0)),
                 pl.BlockSpec((1, A_WIDTH), lambda i: (0, 0))]
    out_shape = [pltpu.HBM((tokens, PA_WIDTH), BF16),
                 pltpu.HBM((IN_WIDTH, D_MODEL), BF16),
                 pltpu.HBM((A_GROUPS, CHUNK, CHUNK), F32),
                 pltpu.HBM((CHUNK, LANES), F32),
                 pltpu.HBM((1, A_WIDTH), F32),
                 pltpu.HBM((1, A_WIDTH), F32)]
    scratch_shapes = [pltpu.VMEM((D_MODEL, IN_WIDTH), F32), pltpu.VMEM((CHUNK, A_WIDTH), F32)]
    return pl.pallas_call(
        body, name=f"gmlp_bwd_dw_{layer}", grid=(steps,),
        in_specs=in_specs, out_specs=out_specs, out_shape=out_shape, scratch_shapes=scratch_shapes,
        compiler_params=pltpu.CompilerParams(dimension_semantics=("arbitrary",),
                                             vmem_limit_bytes=_vmem_limit(blocks, scratch)),
    )(*operands)


def _attn_bwd(pb, dyb, probs_w, sink_w, dbias_in, layer):
    tokens = pb.shape[0]
    qz_width = 2 * B_WIDTH
    tq = _token_tile(tokens)
    per_tile = tq // BLOCK
    nt = tokens // tq

    def body(cur_ref, prev_ref, dyb_ref, pw_ref, spw_ref, dbias_in_ref, dqz_ref, dkv_ref, dbias_ref, dsink_ref,
             kv_ref, acc_ref):
        t = pl.program_id(0)

        @pl.when(t == 0)
        def _():
            dbias_ref[...] = dbias_in_ref[...]
            dsink_ref[...] = jnp.zeros_like(dsink_ref)
            acc_ref[0:BLOCK, :] = jnp.zeros((BLOCK, 2 * KV_WIDTH), F32)

        @pl.when(t < nt)
        def _():
            acc_ref[BLOCK:, :] = jnp.zeros((tq, 2 * KV_WIDTH), F32)
            _fill_kv(kv_ref, prev_ref, cur_ref)
            tri = _band_masks()
            low = lax.broadcasted_iota(jnp.int32, (BLOCK, LANES), 1) < HEAD_DIM
            low_kv = lax.broadcasted_iota(jnp.int32, (2 * BLOCK, LANES), 1) < HEAD_DIM

            head_lane = lax.broadcasted_iota(jnp.int32, (BLOCK, LANES), 1)

            def block(i, carry):
                start = i * BLOCK
                rows = pl.ds(start, BLOCK)
                kv = kv_ref[pl.ds(start, 2 * BLOCK), :]
                k_ops = _kv_variants(kv[:, 0:KV_WIDTH])
                v_ops = _kv_variants(kv[:, KV_WIDTH:2 * KV_WIDTH])
                probs, p16, qs, dos, delta, ds16 = {}, {}, {}, {}, {}, {}
                for hd in range(B_HEADS):
                    kept_p = pw_ref[rows, LANES * hd:LANES * (hd + 1)]
                    probs[hd] = kept_p.astype(F32)
                    p16[hd] = _unwrap16(kept_p, tri)
                for kvh in range(2):
                    qs[kvh] = jnp.concatenate(
                        [cur_ref[rows, LANES * p:LANES * (p + 1)] for p in (2 * kvh, 2 * kvh + 1)], axis=0)
                for kvh in range(2):
                    pairs = (2 * kvh, 2 * kvh + 1)
                    out = jnp.zeros((2 * BLOCK, LANES), F32)
                    for j in range(2):
                        out = out + _mm(jnp.concatenate([p16[2 * p + j] for p in pairs], axis=0), v_ops[kvh][j])
                    d_outs = []
                    for r, p in enumerate(pairs):
                        bz = cur_ref[rows, BZ_OFF + LANES * p:BZ_OFF + LANES * (p + 1)].astype(F32)
                        sg = _sigmoid(bz)
                        dyp = dyb_ref[rows, LANES * p:LANES * (p + 1)].astype(F32)
                        out_p = out[BLOCK * r:BLOCK * (r + 1)]
                        d_out = dyp * (bz * sg)
                        dqz_ref[rows, B_WIDTH + LANES * p:B_WIDTH + LANES * (p + 1)] = (
                            dyp * out_p * (sg * (1.0 + bz * (1.0 - sg)))).astype(BF16)
                        dod = d_out * out_p
                        delta[2 * p] = jnp.sum(jnp.where(low, dod, 0.0), axis=-1, keepdims=True)
                        delta[2 * p + 1] = jnp.sum(jnp.where(low, 0.0, dod), axis=-1, keepdims=True)
                        d_outs.append(d_out.astype(BF16))
                    dos[kvh] = jnp.concatenate(d_outs, axis=0)
                delta_cols = jnp.zeros((BLOCK, LANES), F32)
                for hd in range(B_HEADS):
                    delta_cols = jnp.where(head_lane == hd, delta[hd], delta_cols)
                dsink_ref[0:1, :] += jnp.sum(-(spw_ref[rows, :] * delta_cols), axis=0, keepdims=True)
                for kvh in range(2):
                    pairs = (2 * kvh, 2 * kvh + 1)
                    for j in range(2):
                        dpf = _mm_nt(dos[kvh], v_ops[kvh][j])
                        for r, p in enumerate(pairs):
                            hd = 2 * p + j
                            ds = probs[hd] * (_wrap(dpf[BLOCK * r:BLOCK * (r + 1)], tri) - delta[hd])
                            dbias_ref[hd] += ds
                            ds16[hd] = _unwrap16(ds, tri)
                dk_acc = [[None, None], [None, None]]
                dv_acc = [[None, None], [None, None]]
                for kvh in range(2):
                    pairs = (2 * kvh, 2 * kvh + 1)
                    dq = jnp.zeros((2 * BLOCK, LANES), F32)
                    for j in range(2):
                        dss = jnp.concatenate([ds16[2 * p + j] for p in pairs], axis=0)
                        pss = jnp.concatenate([p16[2 * p + j] for p in pairs], axis=0)
                        dq = dq + _mm(dss, k_ops[kvh][j])
                        dk_acc[kvh][j] = _mm_tn(dss, qs[kvh])
                        dv_acc[kvh][j] = _mm_tn(pss, dos[kvh])
                    for r, p in enumerate(pairs):
                        dqz_ref[rows, LANES * p:LANES * (p + 1)] = (dq[BLOCK * r:BLOCK * (r + 1)] * SCALE).astype(BF16)

                def fold(acc):
                    return jnp.where(low_kv,
                                     acc[0][0] + pltpu.roll(acc[0][1], HEAD_DIM, 1),
                                     pltpu.roll(acc[1][0], HEAD_DIM, 1) + acc[1][1])

                acc_ref[pl.ds(start, 2 * BLOCK), :] += jnp.concatenate(
                    [fold(dk_acc) * SCALE, fold(dv_acc)], axis=1)
                return carry

            for i in range(per_tile):
                block(i, 0)
            dkv_ref[...] = acc_ref[0:tq, :].astype(BF16)
            acc_ref[0:BLOCK, :] = acc_ref[tq:tq + BLOCK, :]

        @pl.when(t == nt)
        def _():
            dkv_ref[0:BLOCK, :] = acc_ref[0:BLOCK, :].astype(BF16)
            dkv_ref[BLOCK:, :] = jnp.zeros((tq - BLOCK, 2 * KV_WIDTH), BF16)

    def cur_map(t):
        return (jnp.minimum(t, nt - 1), 0)

    def prev_map(t):
        return (jnp.maximum(jnp.minimum(t, nt - 1) * per_tile - 1, 0), K_OFF // (2 * KV_WIDTH))

    blocks = [_nbytes((tq, PB_WIDTH), BF16), _nbytes((BLOCK, 2 * KV_WIDTH), BF16), _nbytes((tq, B_WIDTH), BF16),
              2 * _nbytes((B_HEADS, BLOCK, BLOCK), F32), _nbytes((tq, qz_width), BF16),
              _nbytes((tq, 2 * KV_WIDTH), BF16), _nbytes((B_HEADS, LANES), F32),
              _nbytes((tq, B_HEADS * BLOCK), BF16), _nbytes((tq, LANES), F32)]
    scratch = _nbytes((tq + BLOCK, 2 * KV_WIDTH), BF16) + _nbytes((tq + BLOCK, 2 * KV_WIDTH), F32)
    return pl.pallas_call(
        body, name=f"attn_bwd_{layer}", grid=(nt + 1,),
        in_specs=[pl.BlockSpec((tq, PB_WIDTH), cur_map),
                  pl.BlockSpec((BLOCK, 2 * KV_WIDTH), prev_map),
                  pl.BlockSpec((tq, B_WIDTH), cur_map),
                  pl.BlockSpec((tq, B_HEADS * BLOCK), cur_map),
                  pl.BlockSpec((tq, LANES), cur_map),
                  pl.BlockSpec((B_HEADS, BLOCK, BLOCK), lambda t: (0, 0, 0))],
        out_specs=[pl.BlockSpec((tq, qz_width), cur_map),
                   pl.BlockSpec((tq, 2 * KV_WIDTH), lambda t: (t, 0)),
                   pl.BlockSpec((B_HEADS, BLOCK, BLOCK), lambda t: (0, 0, 0)),
                   pl.BlockSpec((B_HEADS, LANES), lambda t: (0, 0))],
        out_shape=[pltpu.HBM((tokens, qz_width), BF16),
                   pltpu.HBM((tokens + tq, 2 * KV_WIDTH), BF16),
                   pltpu.HBM((B_HEADS, BLOCK, BLOCK), F32),
                   pltpu.HBM((B_HEADS, LANES), F32)],
        scratch_shapes=[pltpu.VMEM((tq + BLOCK, 2 * KV_WIDTH), BF16),
                        pltpu.VMEM((tq + BLOCK, 2 * KV_WIDTH), F32)],
        compiler_params=pltpu.CompilerParams(dimension_semantics=("arbitrary",),
                                             vmem_limit_bytes=_vmem_limit(blocks, scratch)),
    )(_hbm(pb), _hbm(pb), _hbm(dyb), _hbm(probs_w), _hbm(sink_w), _hbm(dbias_in))


def _inproj_bwd_dx(da, dqz, dkv, w, x2, g, pre_g3, layer):
    tokens = x2.shape[0]
    tm = _token_tile(tokens)

    def body(da_ref, dqz_ref, dkv_ref, w_ref, x_ref, g_ref, pg_ref, gn_ref, dpg_ref):
        @pl.when(pl.program_id(0) == 0)
        def _():
            dpg_ref[...] = jnp.zeros_like(dpg_ref)

        dh = _mm(da_ref[...], w_ref[0:PA_WIDTH, :])
        dh += _mm(dqz_ref[:, 0:B_WIDTH], w_ref[PA_WIDTH:PA_WIDTH + B_WIDTH, :])
        dh += _mm(dkv_ref[...], w_ref[PA_WIDTH + K_OFF:PA_WIDTH + BZ_OFF, :])
        dh += _mm(dqz_ref[:, B_WIDTH:2 * B_WIDTH], w_ref[PA_WIDTH + BZ_OFF:IN_WIDTH, :])
        x = x_ref[...]
        r = lax.rsqrt(jnp.mean(x * x, axis=-1, keepdims=True) + NORM_EPS)
        xhat = x * r
        dhg = dh * pg_ref[...]
        dpg_ref[...] += jnp.sum(dh * xhat, axis=0, keepdims=True)
        gn_ref[...] = g_ref[...] + r * (dhg - xhat * jnp.mean(dhg * xhat, axis=-1, keepdims=True))

    blocks = [_nbytes((tm, PA_WIDTH), BF16), _nbytes((tm, 2 * B_WIDTH), BF16), _nbytes((tm, 2 * KV_WIDTH), BF16),
              _nbytes((D_MODEL, IN_WIDTH), BF16), 3 * _nbytes((tm, D_MODEL), F32)]
    return pl.pallas_call(
        body, name=f"inproj_bwd_dx_{layer}", grid=(tokens // tm,),
        in_specs=[pl.BlockSpec((tm, PA_WIDTH), lambda i: (i, 0)),
                  pl.BlockSpec((tm, 2 * B_WIDTH), lambda i: (i, 0)),
                  pl.BlockSpec((pl.Element(tm), pl.Element(2 * KV_WIDTH)), lambda i: (pl.multiple_of(i * tm + BLOCK, BLOCK), 0)),
                  pl.BlockSpec((IN_WIDTH, D_MODEL), lambda i: (0, 0)),
                  pl.BlockSpec((tm, D_MODEL), lambda i: (i, 0)),
                  pl.BlockSpec((tm, D_MODEL), lambda i: (i, 0)),
                  pl.BlockSpec((None, 1, D_MODEL), lambda i: (layer, 0, 0))],
        out_specs=[pl.BlockSpec((tm, D_MODEL), lambda i: (i, 0)),
                   pl.BlockSpec((1, D_MODEL), lambda i: (0, 0))],
        out_shape=[pltpu.HBM((tokens,D_MODEL), F32),
                   pltpu.HBM((1,D_MODEL), F32)],
        compiler_params=pltpu.CompilerParams(dimension_semantics=("arbitrary",),
                                             vmem_limit_bytes=_vmem_limit(blocks)),
    )(_hbm(da), _hbm(dqz), _hbm(dkv), _hbm(w), _hbm(x2), _hbm(g), pre_g3)


def _rel_bias_grad(dbias, col_bucket, flip):
    def body(db_ref, cb_ref, flip_ref, out_ref):
        cb = cb_ref[...]
        anti = flip_ref[...]
        sums = []
        for h in range(B_HEADS):
            x = db_ref[h]
            hi = x.astype(BF16)
            rest = x - hi.astype(F32)
            mid = rest.astype(BF16)
            low = (rest - mid.astype(F32)).astype(BF16)
            reversed_x = _mm(hi, anti) + _mm(mid, anti) + _mm(low, anti)
            rolled = pltpu.roll(reversed_x, 0, 1, stride=1, stride_axis=0)
            sums.append(jnp.sum(rolled, axis=0, keepdims=True))
        per_dist = jnp.concatenate(sums, axis=0)
        lane = lax.broadcasted_iota(jnp.int32, (B_HEADS, LANES), 1)
        out = jnp.zeros((B_HEADS, LANES), F32)
        for b in range(REL_BUCKETS):
            s = jnp.sum(jnp.where(cb == b, per_dist, 0.0), axis=-1, keepdims=True)
            out = jnp.where(lane == b, s, out)
        out_ref[...] = out

    vm = pl.BlockSpec(memory_space=pltpu.VMEM)
    return pl.pallas_call(
        body, name="rel_bias_grad",
        out_shape=jax.ShapeDtypeStruct((B_HEADS, LANES), F32),
        in_specs=[vm, vm, vm], out_specs=vm,
    )(dbias, col_bucket, flip)


def _reduce_adamw(slots, w, m, v, name):
    rows, cols = w.shape
    tr = _row_tile(rows)
    c1 = 1.0 / (1.0 - ADAM_B1 ** ADAM_STEP)
    c2 = 1.0 / (1.0 - ADAM_B2 ** ADAM_STEP)

    def body(s_ref, w_ref, m_ref, v_ref, g_ref, d_ref, nm_ref, nv_ref):
        g = s_ref[0].astype(F32)
        for i in range(1, N_DEV):
            g = g + s_ref[i].astype(F32)
        nm = ADAM_B1 * m_ref[...] + (1.0 - ADAM_B1) * g
        nv = ADAM_B2 * v_ref[...] + (1.0 - ADAM_B2) * (g * g)
        g_ref[...] = g
        nm_ref[...] = nm
        nv_ref[...] = nv
        d_ref[...] = -ADAM_LR * ((nm * c1) / (jnp.sqrt(nv * c2) + ADAM_EPS) + ADAM_WD * w_ref[...])

    blocks = [_nbytes((N_DEV, tr, cols), slots.dtype), 7 * _nbytes((tr, cols), F32)]
    tile = pl.BlockSpec((tr, cols), lambda i: (i, 0))
    return pl.pallas_call(
        body, name=name, grid=(rows // tr,),
        in_specs=[pl.BlockSpec((N_DEV, tr, cols), lambda i: (0, i, 0)), tile, tile, tile],
        out_specs=[tile] * 4,
        out_shape=[pltpu.HBM((rows, cols), F32)] * 4,
        compiler_params=pltpu.CompilerParams(dimension_semantics=("parallel",),
                                             vmem_limit_bytes=_vmem_limit(blocks)),
    )(_hbm(slots), _hbm(w), _hbm(m), _hbm(v))


_SMALL = ("pre_norm_g", "post_norm_g", "ln_v_g", "ln_v_b", "b_spatial", "sinks", "rel_bias", "loss")


def _pack_small(parts):
    slabs = []
    for name in _SMALL:
        flat = parts[name].astype(F32).reshape(-1)
        pad = (-flat.shape[0]) % (8 * LANES)
        slabs.append(jnp.pad(flat, (0, pad)).reshape(-1, LANES))
    return jnp.concatenate(slabs, axis=0)


def _unpack_small(slab, shapes):
    out, row = {}, 0
    for name in _SMALL:
        size = int(np.prod(shapes[name]))
        rows = -(-size // (8 * LANES)) * 8
        out[name] = slab[row:row + rows].reshape(-1)[:size].reshape(shapes[name])
        row += rows
    return out


def kernel(x, pre_norm_g, w_in, ln_v_g, ln_v_b, w_spatial, b_spatial, sinks, rel_bias, w_out, post_norm_g, loss_target, m_pre_norm_g, m_w_in, m_ln_v_g, m_ln_v_b, m_w_spatial, m_b_spatial, m_sinks, m_rel_bias, m_w_out, m_post_norm_g, v_pre_norm_g, v_w_in, v_ln_v_g, v_ln_v_b, v_w_spatial, v_b_spatial, v_sinks, v_rel_bias, v_w_out, v_post_norm_g):
    b_loc, seq, _ = x.shape
    tokens = b_loc * seq
    depth = w_in.shape[0]
    in_shard = w_in.shape[2]
    out_shard = w_out.shape[1]
    assert in_shard * N_DEV == IN_WIDTH and out_shard * N_DEV == D_MODEL and seq % BLOCK == 0

    w_in_t, m_w_in_t, v_w_in_t = (jnp.swapaxes(a, 1, 2) for a in (w_in, m_w_in, v_w_in))
    w_in16, w_out16 = w_in_t.astype(BF16), w_out.astype(BF16)

    def gather_start(layer, not_before):
        zones = [lax.empty((N_DEV, in_shard, D_MODEL), BF16), lax.empty((N_DEV, out_shard, D_MODEL), BF16)]
        sems, sends, zones, pre_g = _exchange_start(
            [w_in16[layer], w_out16[layer]], zones, [0, 0], f"weights_send_{layer}",
            pre_norm_g.reshape(depth, 1, D_MODEL), after=not_before)
        return (sems, sends, [0, 0], [0, 1]), zones, pre_g

    def full_weights(gin, gout):
        return gin.reshape(IN_WIDTH, D_MODEL), gout.reshape(D_MODEL, D_MODEL)

    def gather_wait(layer, started, zones, after):
        return full_weights(*_exchange_wait([started], zones, after, f"weights_wait_{layer}"))

    causal = jnp.tril(jnp.ones((CHUNK, CHUNK), dtype=bool))
    wm = jnp.where(causal, w_spatial, 0.0).astype(BF16)
    wmt = _hbm(jnp.swapaxes(wm, -1, -2))
    wm = _hbm(wm)
    bsb = _hbm(jnp.repeat(jnp.swapaxes(b_spatial, -1, -2), HEAD_DIM, axis=-1))
    pre_g3 = _hbm(pre_norm_g.reshape(depth, 1, D_MODEL))
    post_g3 = _hbm(post_norm_g.reshape(depth, 1, D_MODEL))
    ln_g3 = _hbm(ln_v_g.reshape(depth, 1, A_WIDTH))
    ln_b3 = _hbm(ln_v_b.reshape(depth, 1, A_WIDTH))
    bucket = jnp.asarray(_bucket_table())
    bias = _bias_table(rel_bias, bucket)

    xs, saved, weights = [x.reshape(tokens, D_MODEL)], [], []
    pending = None
    for layer in range(depth):
        if layer == 0:
            w, wo = full_weights(*_all_gather([w_in16[0], w_out16[0]], "weights_gather_0"))
        else:
            w, wo = gather_wait(layer, pending[0], pending[1], xs[-1])
        pre_g_fwd = pre_g3
        if layer + 1 < depth:
            pending = gather_start(layer + 1, w)
            pre_g_fwd = _hbm(pending[2])
        h, pa, pb, ya, kept, rstd_w = _inproj_gmlp_fwd(xs[-1], pre_g_fwd, w, ln_g3, ln_b3, wm, bsb, layer)
        if layer + 1 < depth:
            yb, pw, spw, y, x_next = _attn_outproj_fwd(pb, bias, sinks, ya, wo, xs[-1], post_g3, b_loc, layer)
            xs.append(x_next)
        else:
            yb, pw, spw, y, g, loss_part = _attn_outproj_fwd(pb, bias, sinks, ya, wo, xs[-1], post_g3, b_loc, layer,
                                                    target=loss_target.reshape(tokens, D_MODEL))
        saved.append((h, pa, pb, ya, yb, y, pw, spw, kept, rstd_w))
        weights.append((w, wo))

    grads = {name: [None] * depth for name in ("pre_norm_g", "post_norm_g", "ln_v_g", "ln_v_b", "b_spatial", "sinks")}
    zone_in = lax.empty((N_DEV, depth * in_shard, D_MODEL), BF16)
    zone_out = lax.empty((N_DEV, depth * out_shard, D_MODEL), BF16)
    zone_ws = lax.empty((N_DEV, depth * A_GROUPS * CHUNK, CHUNK), F32)
    started = []
    dbias = jnp.zeros((B_HEADS, BLOCK, BLOCK), F32)
    for layer in reversed(range(depth)):
        h, pa, pb, ya, yb, y, pw, spw, kept, rstd_w = saved[layer]
        w, wo = weights[layer]
        dya, dyb, dwo, dpost = _outproj_bwd(g, y, ya, yb, wo, post_g3, layer)
        send_out = dwo.reshape(N_DEV, out_shard, D_MODEL)
        ln_g_bwd = ln_g3
        early_out = layer == 0
        if early_out:
            sems, sends, (zone_out,), ln_g_bwd = _exchange_start(
                [send_out], [zone_out], [layer * out_shard], f"grads_send_out_{layer}",
                ln_v_g.reshape(depth, 1, A_WIDTH))
            started.append((sems, sends, [layer * out_shard], [0]))
            ln_g_bwd = _hbm(ln_g_bwd)
        dqz, dkv, dbias, dsink = _attn_bwd(pb, dyb, pw, spw, dbias, layer)
        da, dw, dws, dbs, dlg, dlb = _gmlp_bwd_dw(pa, kept, rstd_w, dya, h, dqz, dkv, ln_g_bwd, ln_b3, wmt, layer)
        slabs = [dws.reshape(A_GROUPS * CHUNK, CHUNK), dw.reshape(N_DEV, in_shard, D_MODEL)]
        row_offs = [layer * A_GROUPS * CHUNK, layer * in_shard]
        if early_out:
            sems, sends, (zone_ws, zone_in), pre_g_bwd = _exchange_start(
                slabs, [zone_ws, zone_in], row_offs, f"grads_send_{layer}", pre_norm_g.reshape(depth, 1, D_MODEL))
            started.append((sems, sends, row_offs, [1, 2]))
        else:
            row_offs = [layer * out_shard] + row_offs
            sems, sends, (zone_out, zone_ws, zone_in), pre_g_bwd = _exchange_start(
                [send_out] + slabs, [zone_out, zone_ws, zone_in], row_offs, f"grads_send_{layer}",
                pre_norm_g.reshape(depth, 1, D_MODEL))
            started.append((sems, sends, row_offs, [0, 1, 2]))
        g, dpre = _inproj_bwd_dx(da, dqz, dkv, w, xs[layer], g, _hbm(pre_g_bwd), layer)
        grads["b_spatial"][layer] = dbs[:, :A_GROUPS].T
        grads["ln_v_g"][layer] = dlg[0]
        grads["ln_v_b"][layer] = dlb[0]
        grads["sinks"][layer] = dsink[0, :B_HEADS]
        grads["pre_norm_g"][layer] = dpre[0]
        grads["post_norm_g"][layer] = dpost[0]
    grad_x = g.reshape(x.shape)
    col_bucket = jnp.asarray(np.broadcast_to(_bucket_table()[0:1, ::-1], (B_HEADS, BLOCK)))
    flip = jnp.asarray(np.eye(BLOCK, dtype=np.float32)[::-1], dtype=BF16)
    drel = _rel_bias_grad(dbias, col_bucket, flip)[:, :REL_BUCKETS].T

    no_state = jnp.zeros((1,), F32)
    small_w = dict(pre_norm_g=pre_norm_g, post_norm_g=post_norm_g, ln_v_g=ln_v_g,
                   ln_v_b=ln_v_b, b_spatial=b_spatial, sinks=sinks, rel_bias=rel_bias, loss=no_state)
    small_m = dict(pre_norm_g=m_pre_norm_g, post_norm_g=m_post_norm_g, ln_v_g=m_ln_v_g,
                   ln_v_b=m_ln_v_b, b_spatial=m_b_spatial, sinks=m_sinks, rel_bias=m_rel_bias, loss=no_state)
    small_v = dict(pre_norm_g=v_pre_norm_g, post_norm_g=v_post_norm_g, ln_v_g=v_ln_v_g,
                   ln_v_b=v_ln_v_b, b_spatial=v_b_spatial, sinks=v_sinks, rel_bias=v_rel_bias, loss=no_state)
    small_g = {name: jnp.stack(grads[name]) for name in _SMALL if name not in ("rel_bias", "loss")}
    small_g["rel_bias"] = drel
    small_g["loss"] = loss_part[0, :1]
    shapes = {name: small_w[name].shape for name in _SMALL}
    slots = _all_gather_direct(_pack_small(small_g), "small_grads_all_gather")
    res_small = _reduce_adamw(slots, _pack_small(small_w), _pack_small(small_m), _pack_small(small_v), "adamw_small")

    recv_out, recv_ws, recv_in = _exchange_wait(started, [zone_out, zone_ws, zone_in], res_small[0], "grads_wait")
    res_out = _reduce_adamw(recv_out, w_out.reshape(-1, D_MODEL), m_w_out.reshape(-1, D_MODEL),
                            v_w_out.reshape(-1, D_MODEL), "adamw_w_out")
    res_out = [r.reshape(w_out.shape) for r in res_out]
    res_ws = _reduce_adamw(recv_ws, w_spatial.reshape(-1, CHUNK), m_w_spatial.reshape(-1, CHUNK),
                           v_w_spatial.reshape(-1, CHUNK), "adamw_w_spatial")
    res_ws = [r.reshape(w_spatial.shape) for r in res_ws]
    res_in = _reduce_adamw(recv_in, w_in_t.reshape(-1, D_MODEL), m_w_in_t.reshape(-1, D_MODEL),
                           v_w_in_t.reshape(-1, D_MODEL), "adamw_w_in")
    res_in = [jnp.swapaxes(r.reshape(w_in_t.shape), 1, 2) for r in res_in]
    res_small = [_unpack_small(r, shapes) for r in res_small]

    order = ("pre_norm_g", "w_in", "ln_v_g", "ln_v_b", "w_spatial", "b_spatial", "sinks", "rel_bias", "w_out",
             "post_norm_g")
    outs = [res_small[0]["loss"][0], grad_x]
    for kind in range(4):
        for name in order:
            if name == "w_in":
                outs.append(res_in[kind])
            elif name == "w_out":
                outs.append(res_out[kind])
            elif name == "w_spatial":
                outs.append(res_ws[kind])
            else:
                outs.append(res_small[kind][name])
    return tuple(outs)
```

```python
import math

import numpy as np
import jax
import jax.numpy as jnp
from jax import lax
from jax.experimental import pallas as pl
from jax.experimental.pallas import tpu as pltpu

F32 = jnp.float32
BF16 = jnp.bfloat16

D_MODEL = 1024
A_WIDTH = 512
A_GROUPS = 8
CHUNK = 128
B_HEADS = 8
HEAD_DIM = 64
B_WIDTH = 512
KV_WIDTH = 128
BLOCK = 128
REL_BUCKETS = 32
REL_MAX_DIST = 128
NORM_EPS = 1e-6
PA_WIDTH = 3 * A_WIDTH
PB_WIDTH = 2 * B_WIDTH + 2 * KV_WIDTH
IN_WIDTH = PA_WIDTH + PB_WIDTH
K_OFF, BZ_OFF = B_WIDTH, B_WIDTH + 2 * KV_WIDTH
KEPT_WIDTH = 5 * A_WIDTH
SCALE = HEAD_DIM ** -0.5
NEG = -1e30
N_DEV = 8
LANES = 128

ADAM_LR = 0.001
ADAM_B1 = 0.9
ADAM_B2 = 0.999
ADAM_EPS = 1e-08
ADAM_WD = 0.01
ADAM_STEP = 10

V7X_VMEM_BYTES = 64 * 1024 * 1024
VMEM_TEMP_BYTES = 12 * 1024 * 1024
MESH = pl.DeviceIdType.MESH


def _vmem_limit(block_bytes, scratch_bytes=0):
    need = 2 * sum(block_bytes) + scratch_bytes + VMEM_TEMP_BYTES
    return int(min(need, V7X_VMEM_BYTES - 8 * 1024 * 1024))


def _nbytes(shape, dtype):
    return int(np.prod(shape)) * jnp.dtype(dtype).itemsize


def _token_tile(tokens, cap=512):
    tile = min(cap, tokens // 2)
    assert tokens % tile == 0 and tile % CHUNK == 0, tokens
    return tile


def _row_tile(rows, cap=512):
    best = 8
    for t in range(8, cap + 1, 8):
        if rows % t == 0:
            best = t
    assert rows % best == 0, rows
    return best


def _mm(a, b):
    return lax.dot_general(a, b, (((1,), (0,)), ((), ())), preferred_element_type=F32)


def _mm_nt(a, b):
    return lax.dot_general(a, b, (((1,), (1,)), ((), ())), preferred_element_type=F32)


def _mm_tn(a, b):
    return lax.dot_general(a, b, (((0,), (0,)), ((), ())), preferred_element_type=F32)


_GELU_C = math.sqrt(2.0 / math.pi)


_GELU_A = _GELU_C * 0.044715


def _gelu_parts(x):
    x2 = x * x
    t = jnp.tanh(x * (_GELU_C + _GELU_A * x2))
    return x2, t, 0.5 + 0.5 * t


def _gelu_and_grad(x):
    x2, t, half_plus = _gelu_parts(x)
    grad = half_plus + (0.5 * x) * (1.0 - t * t) * (_GELU_C + (3.0 * _GELU_A) * x2)
    return x * half_plus, grad


def _sigmoid(x):
    return 0.5 + 0.5 * jnp.tanh(0.5 * x)


def _bucket_table():
    q = np.arange(BLOCK)[:, None]
    k = np.arange(BLOCK)[None, :]
    dist = np.where(k <= q, q - k, q + BLOCK - k)
    max_exact = REL_BUCKETS // 2
    safe = np.maximum(dist, 1).astype(np.float32)
    large = max_exact + (np.log(safe / np.float32(max_exact)) / np.float32(math.log(REL_MAX_DIST / max_exact))
                         * np.float32(REL_BUCKETS - max_exact)).astype(np.int32)
    large = np.minimum(large, REL_BUCKETS - 1)
    assert dist.min() >= 0 and dist.max() < BLOCK
    return np.where(dist < max_exact, dist, large).astype(np.int32)


def _hbm(x):
    return pltpu.with_memory_space_constraint(x, pltpu.HBM)


def _slot(px, py, pc):
    return 4 * px + 2 * py + pc


def _all_gather(arrs, name):
    n = len(arrs)

    def body(*refs):
        ins, outs = refs[:n], refs[n:2 * n]
        send_sems, recv_sems, local_sems = refs[2 * n:]
        x, y, c = lax.axis_index("x"), lax.axis_index("y"), lax.axis_index("c")
        me, sibling = (x, y, c), (x, y, 1 - c)
        chips = [(1 - x, y), (x, 1 - y), (1 - x, 1 - y)]

        def copy(a, k, block, to, src=None):
            dst = outs[a].at[_slot(*block)]
            return pltpu.make_async_remote_copy(
                src_ref=dst if src is None else src, dst_ref=dst,
                send_sem=send_sems.at[7 * a + k], recv_sem=recv_sems.at[7 * a + k],
                device_id=to, device_id_type=MESH)

        mine = [pltpu.make_async_copy(ins[a], outs[a].at[_slot(*me)], local_sems.at[a]) for a in range(n)]
        for cp in mine:
            cp.start()
        first = []
        for a in range(n):
            first.append(copy(a, 0, me, sibling, src=ins[a]))
            first += [copy(a, 1 + j, me, (*chip, c), src=ins[a]) for j, chip in enumerate(chips)]
        for cp in first:
            cp.start()
        passed = []
        for j, chip in enumerate(chips):
            for a in range(n):
                copy(a, 1 + j, (*chip, c), me).wait_recv()
                fwd = copy(a, 4 + j, (*chip, c), sibling)
                fwd.start()
                passed.append(fwd)
        for a in range(n):
            copy(a, 0, sibling, me).wait_recv()
            for j, chip in enumerate(chips):
                copy(a, 4 + j, (*chip, 1 - c), me).wait_recv()
        for cp in first + passed:
            cp.wait_send()
        for cp in mine:
            cp.wait()

    any_spec = pl.BlockSpec(memory_space=pl.ANY)
    return pl.pallas_call(
        body, name=name,
        out_shape=[jax.ShapeDtypeStruct((N_DEV,) + a.shape, a.dtype) for a in arrs],
        in_specs=[any_spec] * n, out_specs=[any_spec] * n,
        scratch_shapes=[pltpu.SemaphoreType.DMA((7 * n,)), pltpu.SemaphoreType.DMA((7 * n,)),
                        pltpu.SemaphoreType.DMA((n,))],
    )(*arrs)


def _all_gather_direct(arr, name):
    def body(in_ref, out_ref, send_sems, recv_sems, local_sem):
        me, peers = _peers()
        mine = pltpu.make_async_copy(in_ref, out_ref.at[_slot(*me)], local_sem)
        mine.start()

        def copy(k, origin, to):
            return pltpu.make_async_remote_copy(
                src_ref=in_ref, dst_ref=out_ref.at[_slot(*origin)], send_sem=send_sems.at[k],
                recv_sem=recv_sems.at[k], device_id=to, device_id_type=MESH)

        sends = [copy(k, me, peer) for k, peer in enumerate(peers)]
        for cp in sends:
            cp.start()
        for k, peer in enumerate(peers):
            copy(k, peer, me).wait_recv()
        for cp in sends:
            cp.wait_send()
        mine.wait()

    any_spec = pl.BlockSpec(memory_space=pl.ANY)
    return pl.pallas_call(
        body, name=name,
        out_shape=jax.ShapeDtypeStruct((N_DEV,) + arr.shape, arr.dtype),
        in_specs=[any_spec], out_specs=any_spec,
        scratch_shapes=[pltpu.SemaphoreType.DMA((7,)), pltpu.SemaphoreType.DMA((7,)), pltpu.SemaphoreType.DMA],
    )(arr)


_HBM_SPEC = pl.BlockSpec(memory_space=pltpu.HBM)
_SEM_SPEC = pl.BlockSpec(memory_space=pltpu.SEMAPHORE)
_DATAFLOW = pltpu.SideEffectType.DATAFLOW_SIDE_EFFECTING


def _peers():
    x, y, c = lax.axis_index("x"), lax.axis_index("y"), lax.axis_index("c")
    peers = []
    for k in range(1, N_DEV):
        fx, fy, fc = (k >> 2) & 1, (k >> 1) & 1, k & 1
        peers.append((1 - x if fx else x, 1 - y if fy else y, 1 - c if fc else c))
    return (x, y, c), peers


def _exchange_copy(send_ref, land_ref, row_off, src_slot, dst_slot, sems, idx, peer):
    rows = send_ref.shape[-2]
    src = send_ref.at[src_slot] if len(send_ref.shape) == 3 else send_ref
    return pltpu.make_async_remote_copy(
        src_ref=src, dst_ref=land_ref.at[dst_slot, pl.ds(row_off, rows), :],
        send_sem=sems[0].at[idx], recv_sem=sems[1].at[idx], device_id=peer, device_id_type=MESH)


def _own_copy(send_ref, land_ref, row_off, slot, sem):
    rows = send_ref.shape[-2]
    src = send_ref.at[slot] if len(send_ref.shape) == 3 else send_ref
    return pltpu.make_async_copy(src, land_ref.at[slot, pl.ds(row_off, rows), :], sem)


def _exchange_start(sends, lands, row_offs, name, carry, after=None):
    n = len(sends)

    def body(*refs):
        ins, zones, carry_ref = refs[:n], refs[n:2 * n], refs[2 * n]
        first_out = 2 * n + 1 + (after is not None)
        sems = refs[first_out:first_out + 3]
        carry_out = refs[-1]
        me, peers = _peers()
        for a in range(n):
            for k, peer in enumerate(peers):
                _exchange_copy(ins[a], zones[a], row_offs[a], _slot(*peer), _slot(*me), sems, 7 * a + k, peer).start()
            _own_copy(ins[a], zones[a], row_offs[a], _slot(*me), sems[2].at[a]).start()
        carry_out[...] = carry_ref[...]

    arrays = [_hbm(a) for a in list(sends) + list(lands)]
    vmem = pl.BlockSpec(memory_space=pltpu.VMEM)
    out = pl.pallas_call(
        body, name=name,
        out_shape=(pltpu.SemaphoreType.DMA((7 * n,)), pltpu.SemaphoreType.DMA((7 * n,)), pltpu.SemaphoreType.DMA((n,)),
                   *[pltpu.HBM(a.shape, a.dtype) for a in arrays], jax.ShapeDtypeStruct(carry.shape, carry.dtype)),
        in_specs=[_HBM_SPEC] * (2 * n) + [vmem] + ([pl.BlockSpec(memory_space=pl.ANY)] if after is not None else []),
        out_specs=(_SEM_SPEC, _SEM_SPEC, _SEM_SPEC, *[_HBM_SPEC] * (2 * n), vmem),
        input_output_aliases={i: 3 + i for i in range(2 * n)},
        compiler_params=pltpu.CompilerParams(has_side_effects=_DATAFLOW),
    )(*arrays, carry, *([after] if after is not None else []))
    return (out[0], out[1], out[2]), list(out[3:3 + n]), list(out[3 + n:3 + 2 * n]), out[-1]


def _exchange_wait(started, lands, after, name):
    n = len(lands)
    flat_sends = [s for _, sends, _ in started for s in sends]
    flat_sems = [s for sems, _, _ in started for s in sems]
    ns = len(flat_sends)

    def body(*refs):
        ins, zones = refs[:ns], refs[ns:ns + n]
        sem_refs = refs[ns + n:ns + n + len(flat_sems)]
        me, peers = _peers()
        pos = 0
        for call, (_, sends, row_offs) in enumerate(started):
            sems = sem_refs[3 * call:3 * call + 3]
            for a in range(len(sends)):
                for k, peer in enumerate(peers):
                    cp = _exchange_copy(ins[pos + a], zones[a], row_offs[a], _slot(*peer), _slot(*peer), sems,
                                        7 * a + k, peer)
                    cp.wait_send()
                    cp.wait_recv()
                _own_copy(ins[pos + a], zones[a], row_offs[a], _slot(*me), sems[2].at[a]).wait()
            pos += len(sends)

    arrays = list(flat_sends) + list(lands)
    out = pl.pallas_call(
        body, name=name,
        out_shape=tuple(pltpu.HBM(a.shape, a.dtype) for a in arrays),
        in_specs=[_HBM_SPEC] * len(arrays) + [_SEM_SPEC] * len(flat_sems) + [pl.BlockSpec(memory_space=pl.ANY)],
        out_specs=tuple([_HBM_SPEC] * len(arrays)),
        input_output_aliases={i: i for i in range(len(arrays))},
        compiler_params=pltpu.CompilerParams(has_side_effects=_DATAFLOW),
    )(*arrays, *flat_sems, after)
    return list(out[ns:])


def _bias_table(rel_bias, bucket):
    def body(rel_ref, bucket_ref, out_ref):
        bk = bucket_ref[...]
        for h in range(B_HEADS):
            def pick(b, acc, h=h):
                return jnp.where(bk == b, rel_ref[b, h], acc)
            out_ref[h] = lax.fori_loop(0, REL_BUCKETS, pick, jnp.zeros((BLOCK, BLOCK), F32))

    return pl.pallas_call(
        body, name="bias_table",
        out_shape=jax.ShapeDtypeStruct((B_HEADS, BLOCK, BLOCK), F32),
        in_specs=[pl.BlockSpec(memory_space=pltpu.SMEM), pl.BlockSpec(memory_space=pltpu.VMEM)],
        out_specs=pl.BlockSpec(memory_space=pltpu.VMEM),
    )(rel_bias, bucket)


def _inproj_gmlp_fwd(x2, pre_g3, w, ln_g3, ln_b3, wm, bsb, layer):
    tokens = x2.shape[0]
    tm = _token_tile(tokens)

    def body(x_ref, g_ref, w_ref, lg_ref, lb_ref, wm_ref, bsb_ref, h_ref, pa_ref, pb_ref, ya_ref, kept_ref, rstd_ref):
        x = x_ref[...]
        r = lax.rsqrt(jnp.mean(x * x, axis=-1, keepdims=True) + NORM_EPS)
        h = (x * r * g_ref[...]).astype(BF16)
        h_ref[...] = h
        pa_ref[...] = _mm_nt(h, w_ref[0:PA_WIDTH, :]).astype(BF16)
        pb_ref[...] = _mm_nt(h, w_ref[PA_WIDTH:IN_WIDTH, :]).astype(BF16)
        for ci in range(tm // CHUNK):
            rows = slice(ci * CHUNK, (ci + 1) * CHUNK)
            gu, gv, pz, u, vhat, rstd, _, mixed, sg = _gmlp_forward_chunk(
                pa_ref[rows, :], lg_ref[...], lb_ref[...], wm_ref, bsb_ref[...])
            ya_ref[rows, :] = (u * mixed * (pz * sg)).astype(BF16)
            for k, val in enumerate((u, gu, gv, vhat, mixed)):
                kept_ref[rows, A_WIDTH * k:A_WIDTH * (k + 1)] = val.astype(BF16)
            rstd_ref[rows, :] = jnp.broadcast_to(rstd, (CHUNK, LANES))

    blocks = [_nbytes((tm, D_MODEL), F32), _nbytes((D_MODEL, IN_WIDTH), BF16),
              _nbytes((tm, D_MODEL), BF16), _nbytes((tm, PA_WIDTH), BF16), _nbytes((tm, PB_WIDTH), BF16),
              _nbytes((A_GROUPS, CHUNK, CHUNK), BF16), _nbytes((CHUNK, A_WIDTH), F32), _nbytes((tm, A_WIDTH), BF16)]
    return pl.pallas_call(
        body, name=f"inproj_gmlp_fwd_{layer}", grid=(tokens // tm,),
        in_specs=[pl.BlockSpec((tm, D_MODEL), lambda i: (i, 0)),
                  pl.BlockSpec((None, 1, D_MODEL), lambda i: (layer, 0, 0)),
                  pl.BlockSpec((IN_WIDTH, D_MODEL), lambda i: (0, 0)),
                  pl.BlockSpec((None, 1, A_WIDTH), lambda i: (layer, 0, 0)),
                  pl.BlockSpec((None, 1, A_WIDTH), lambda i: (layer, 0, 0)),
                  pl.BlockSpec((None, A_GROUPS, CHUNK, CHUNK), lambda i: (layer, 0, 0, 0)),
                  pl.BlockSpec((None, CHUNK, A_WIDTH), lambda i: (layer, 0, 0))],
        out_specs=[pl.BlockSpec((tm, D_MODEL), lambda i: (i, 0)),
                   pl.BlockSpec((tm, PA_WIDTH), lambda i: (i, 0)),
                   pl.BlockSpec((tm, PB_WIDTH), lambda i: (i, 0)),
                   pl.BlockSpec((tm, A_WIDTH), lambda i: (i, 0)),
                   pl.BlockSpec((tm, KEPT_WIDTH), lambda i: (i, 0)),
                   pl.BlockSpec((tm, LANES), lambda i: (i, 0))],
        out_shape=[pltpu.HBM((tokens, D_MODEL), BF16),
                   pltpu.HBM((tokens, PA_WIDTH), BF16),
                   pltpu.HBM((tokens, PB_WIDTH), BF16),
                   pltpu.HBM((tokens, A_WIDTH), BF16),
                   pltpu.HBM((tokens, KEPT_WIDTH), BF16),
                   pltpu.HBM((tokens, LANES), F32)],
        compiler_params=pltpu.CompilerParams(dimension_semantics=("parallel",),
                                             vmem_limit_bytes=_vmem_limit(blocks + [_nbytes((tm, KEPT_WIDTH), BF16),
                                                                                    _nbytes((tm, LANES), F32)])),
    )(_hbm(x2), pre_g3, _hbm(w), ln_g3, ln_b3, wm, bsb)


def _gmlp_forward_chunk(pa, ln_g, ln_b, wm_ref, bsb):
    pu = pa[:, 0:A_WIDTH].astype(F32)
    pv = pa[:, A_WIDTH:2 * A_WIDTH].astype(F32)
    pz = pa[:, 2 * A_WIDTH:3 * A_WIDTH].astype(F32)
    u, gu = _gelu_and_grad(pu)
    vv, gv = _gelu_and_grad(pv)
    mu = jnp.mean(vv, axis=-1, keepdims=True)
    xc = vv - mu
    rstd = lax.rsqrt(jnp.mean(xc * xc, axis=-1, keepdims=True) + NORM_EPS)
    vhat = xc * rstd
    vnb = (vhat * ln_g + ln_b).astype(BF16)
    low = lax.broadcasted_iota(jnp.int32, (CHUNK, LANES), 1) < HEAD_DIM
    parts = []
    for p in range(A_GROUPS // 2):
        vp = vnb[:, LANES * p:LANES * (p + 1)]
        parts.append(jnp.where(low, _mm(wm_ref[2 * p], vp), _mm(wm_ref[2 * p + 1], vp)))
    mixed = jnp.concatenate(parts, axis=1) + bsb
    sg = _sigmoid(pz)
    return gu, gv, pz, u, vhat, rstd, vnb, mixed, sg


def _kv_variants(kv):
    t = kv.astype(F32)
    rolled = pltpu.roll(t, HEAD_DIM, 1)
    low = lax.broadcasted_iota(jnp.int32, t.shape, 1) < HEAD_DIM
    zero = jnp.zeros_like(t)
    head0 = (jnp.where(low, t, zero).astype(BF16), jnp.where(low, zero, rolled).astype(BF16))
    head1 = (jnp.where(low, rolled, zero).astype(BF16), jnp.where(low, zero, t).astype(BF16))
    return head0, head1


def _band_masks():
    row = lax.broadcasted_iota(jnp.int32, (BLOCK, BLOCK), 0)
    col = lax.broadcasted_iota(jnp.int32, (BLOCK, BLOCK), 1)
    return col <= row


def _wrap(full, tri):
    return jnp.where(tri, full[:, BLOCK:2 * BLOCK], full[:, 0:BLOCK])


def _attn_probs(sf, bias_h, sink, tri, kill):
    s = _wrap(sf, tri) + bias_h
    s = jnp.where(kill, NEG, s)
    m = jnp.maximum(jnp.max(s, axis=-1, keepdims=True), sink)
    e = jnp.exp(s - m)
    es = jnp.exp(sink - m)
    inv = 1.0 / (jnp.sum(e, axis=-1, keepdims=True) + es)
    return e * inv, es * inv


def _unwrap16(p, tri):
    p = p.astype(BF16)
    zero = jnp.zeros_like(p)
    return jnp.concatenate([jnp.where(tri, zero, p), jnp.where(tri, p, zero)], axis=1)


def _fill_kv(kv_ref, prev_ref, cur_ref):
    kv_ref[0:BLOCK, :] = prev_ref[...]
    kv_ref[BLOCK:, :] = cur_ref[:, K_OFF:K_OFF + 2 * KV_WIDTH]


def _attn_outproj_fwd(pb, bias, sinks, ya, wo, x2, post_g3, b_loc, layer, target=None):
    tokens = pb.shape[0]
    nb = tokens // b_loc // BLOCK
    tq = _token_tile(tokens)
    per_tile = tq // BLOCK
    steps = tokens // tq
    with_loss = target is not None
    half = D_MODEL // 2

    def body(*refs):
        sink_ref, cur_ref, prev_ref, bias_ref, ya_ref, woa_ref, wob_ref, x_ref, g_ref = refs[:9]
        if with_loss:
            t_ref, yb_ref, pw_ref, spw_ref, y_ref, grad_ref, loss_ref, kv_ref, acc_ref = refs[9:]
        else:
            yb_ref, pw_ref, spw_ref, y_ref, xn_ref, kv_ref = refs[9:]
        t = pl.program_id(0)
        _fill_kv(kv_ref, prev_ref, cur_ref)
        tri = _band_masks()
        head_lane = lax.broadcasted_iota(jnp.int32, (BLOCK, LANES), 1)

        def block(i, carry):
            start = i * BLOCK
            rows = pl.ds(start, BLOCK)
            first = lax.rem(t * per_tile + i, nb) == 0
            kill = jnp.logical_and(first, jnp.logical_not(tri))
            kv = kv_ref[pl.ds(start, 2 * BLOCK), :]
            k_ops = _kv_variants(kv[:, 0:KV_WIDTH])
            v_ops = _kv_variants(kv[:, KV_WIDTH:2 * KV_WIDTH])
            p16 = {}
            sink_cols = jnp.zeros((BLOCK, LANES), F32)
            for kvh in range(2):
                pairs = (2 * kvh, 2 * kvh + 1)
                qs = jnp.concatenate([cur_ref[rows, LANES * p:LANES * (p + 1)] for p in pairs], axis=0) * SCALE
                for j in range(2):
                    sf = _mm_nt(qs, k_ops[kvh][j])
                    for r, p in enumerate(pairs):
                        hd = 2 * p + j
                        probs, sink_p = _attn_probs(sf[BLOCK * r:BLOCK * (r + 1)], bias_ref[hd],
                                                    sink_ref[layer, hd], tri, kill)
                        probs = probs.astype(BF16)
                        p16[hd] = _unwrap16(probs, tri)
                        pw_ref[rows, LANES * hd:LANES * (hd + 1)] = probs
                        sink_cols = jnp.where(head_lane == hd, sink_p, sink_cols)
            spw_ref[rows, :] = sink_cols
            for kvh in range(2):
                pairs = (2 * kvh, 2 * kvh + 1)
                out = jnp.zeros((2 * BLOCK, LANES), F32)
                for j in range(2):
                    out = out + _mm(jnp.concatenate([p16[2 * p + j] for p in pairs], axis=0), v_ops[kvh][j])
                for r, p in enumerate(pairs):
                    bz = cur_ref[rows, BZ_OFF + LANES * p:BZ_OFF + LANES * (p + 1)].astype(F32)
                    yb_ref[rows, LANES * p:LANES * (p + 1)] = (
                        out[BLOCK * r:BLOCK * (r + 1)] * (bz * _sigmoid(bz))).astype(BF16)
            y = _mm(ya_ref[rows, :], woa_ref[...]) + _mm(yb_ref[rows, :], wob_ref[...])
            r = lax.rsqrt(jnp.mean(y * y, axis=-1, keepdims=True) + NORM_EPS)
            x_next = x_ref[rows, :] + y * r * g_ref[...]
            y_ref[rows, :] = y.astype(BF16)
            if not with_loss:
                xn_ref[rows, :] = x_next
            else:
                err = x_next - t_ref[rows, :]
                grad_ref[rows, :] = err * (1.0 / D_MODEL)
                acc_ref[...] += jnp.sum(err * err, axis=0, keepdims=True)
            return carry

        if with_loss:
            @pl.when(t == 0)
            def _():
                acc_ref[...] = jnp.zeros_like(acc_ref)

        for i in range(per_tile):
            block(i, 0)

        if not with_loss:
            return

        @pl.when(t == steps - 1)
        def _():
            total = jnp.sum(acc_ref[...], axis=-1, keepdims=True) * (0.5 / D_MODEL)
            loss_ref[...] = jnp.broadcast_to(total, loss_ref.shape)

    tile = pl.BlockSpec((tq, D_MODEL), lambda t: (t, 0))
    blocks = [_nbytes((tq, PB_WIDTH), BF16), _nbytes((BLOCK, 2 * KV_WIDTH), BF16),
              _nbytes((B_HEADS, BLOCK, BLOCK), F32), 2 * _nbytes((tq, B_WIDTH), BF16),
              2 * _nbytes((half, D_MODEL), BF16), 3 * _nbytes((tq, D_MODEL), F32), _nbytes((tq, D_MODEL), BF16),
              _nbytes((tq, B_HEADS * BLOCK), BF16), _nbytes((tq, LANES), F32)]
    scratch = _nbytes((tq + BLOCK, 2 * KV_WIDTH), BF16)
    in_specs = [pl.BlockSpec(memory_space=pltpu.SMEM),
                pl.BlockSpec((tq, PB_WIDTH), lambda t: (t, 0)),
                pl.BlockSpec((BLOCK, 2 * KV_WIDTH),
                             lambda t: (jnp.maximum(t * per_tile - 1, 0), K_OFF // (2 * KV_WIDTH))),
                pl.BlockSpec((B_HEADS, BLOCK, BLOCK), lambda t: (0, 0, 0)),
                pl.BlockSpec((tq, half), lambda t: (t, 0)),
                pl.BlockSpec((half, D_MODEL), lambda t: (0, 0)),
                pl.BlockSpec((half, D_MODEL), lambda t: (1, 0)),
                tile,
                pl.BlockSpec((None, 1, D_MODEL), lambda t: (layer, 0, 0))]
    out_specs = [pl.BlockSpec((tq, B_WIDTH), lambda t: (t, 0)), pl.BlockSpec((tq, B_HEADS * BLOCK), lambda t: (t, 0)),
                 pl.BlockSpec((tq, LANES), lambda t: (t, 0)), tile, tile]
    out_shape = [pltpu.HBM((tokens, B_WIDTH), BF16), pltpu.HBM((tokens, B_HEADS * BLOCK), BF16),
                 pltpu.HBM((tokens, LANES), F32), pltpu.HBM((tokens, D_MODEL), BF16),
                 pltpu.HBM((tokens, D_MODEL), F32)]
    scratch_shapes = [pltpu.VMEM((tq + BLOCK, 2 * KV_WIDTH), BF16)]
    operands = [sinks, _hbm(pb), _hbm(pb), _hbm(bias), _hbm(ya), _hbm(wo), _hbm(wo), _hbm(x2), post_g3]
    if with_loss:
        in_specs.append(tile)
        out_specs.append(pl.BlockSpec((1, LANES), lambda t: (0, 0)))
        out_shape.append(pltpu.HBM((1, LANES), F32))
        scratch_shapes.append(pltpu.VMEM((1, D_MODEL), F32))
        operands.append(_hbm(target))
    return pl.pallas_call(
        body, name=f"attn_outproj_fwd_{layer}", grid=(steps,),
        in_specs=in_specs, out_specs=out_specs, out_shape=out_shape, scratch_shapes=scratch_shapes,
        compiler_params=pltpu.CompilerParams(dimension_semantics=("arbitrary" if with_loss else "parallel",),
                                             vmem_limit_bytes=_vmem_limit(blocks, scratch)),
    )(*operands)


def _outproj_bwd(g, y, ya, yb, wo, post_g3, layer):
    tokens = g.shape[0]
    tm = _token_tile(tokens, cap=1024)
    half = D_MODEL // 2
    steps = tokens // tm

    def body(g_ref, y_ref, ya_ref, yb_ref, woa_ref, wob_ref, pg_ref, dya_ref, dyb_ref, dwo16_ref, dpg_ref, dwo_ref):
        @pl.when(pl.program_id(0) == 0)
        def _():
            dwo_ref[...] = jnp.zeros_like(dwo_ref)
            dpg_ref[...] = jnp.zeros_like(dpg_ref)

        gv = g_ref[...]
        yf = y_ref[...].astype(F32)
        r = lax.rsqrt(jnp.mean(yf * yf, axis=-1, keepdims=True) + NORM_EPS)
        yhat = yf * r
        gg = gv * pg_ref[...]
        dy = r * (gg - yhat * jnp.mean(gg * yhat, axis=-1, keepdims=True))
        dpg_ref[...] += jnp.sum(gv * yhat, axis=0, keepdims=True)
        dyb16 = dy.astype(BF16)
        dya_ref[...] = _mm_nt(dyb16, woa_ref[...]).astype(BF16)
        dyb_ref[...] = _mm_nt(dyb16, wob_ref[...]).astype(BF16)
        dwo_ref[0:half, :] += _mm_tn(ya_ref[...], dyb16)
        dwo_ref[half:D_MODEL, :] += _mm_tn(yb_ref[...], dyb16)

        @pl.when(pl.program_id(0) == steps - 1)
        def _():
            dwo16_ref[...] = dwo_ref[...].astype(BF16)

    blocks = [_nbytes((tm, D_MODEL), F32), _nbytes((tm, D_MODEL), BF16), 4 * _nbytes((tm, half), BF16),
              2 * _nbytes((half, D_MODEL), BF16), _nbytes((D_MODEL, D_MODEL), BF16)]
    return pl.pallas_call(
        body, name=f"outproj_bwd_{layer}", grid=(tokens // tm,),
        in_specs=[pl.BlockSpec((tm, D_MODEL), lambda i: (i, 0)),
                  pl.BlockSpec((tm, D_MODEL), lambda i: (i, 0)),
                  pl.BlockSpec((tm, half), lambda i: (i, 0)),
                  pl.BlockSpec((tm, half), lambda i: (i, 0)),
                  pl.BlockSpec((half, D_MODEL), lambda i: (0, 0)),
                  pl.BlockSpec((half, D_MODEL), lambda i: (1, 0)),
                  pl.BlockSpec((None, 1, D_MODEL), lambda i: (layer, 0, 0))],
        out_specs=[pl.BlockSpec((tm, half), lambda i: (i, 0)),
                   pl.BlockSpec((tm, half), lambda i: (i, 0)),
                   pl.BlockSpec((D_MODEL, D_MODEL), lambda i: (0, 0)),
                   pl.BlockSpec((1, D_MODEL), lambda i: (0, 0))],
        out_shape=[pltpu.HBM((tokens,half), BF16),
                   pltpu.HBM((tokens,half), BF16),
                   pltpu.HBM((D_MODEL,D_MODEL), BF16),
                   pltpu.HBM((1,D_MODEL), F32)],
        scratch_shapes=[pltpu.VMEM((D_MODEL, D_MODEL), F32)],
        compiler_params=pltpu.CompilerParams(dimension_semantics=("arbitrary",),
                                             vmem_limit_bytes=_vmem_limit(blocks, _nbytes((D_MODEL, D_MODEL), F32))),
    )(_hbm(g), _hbm(y), _hbm(ya), _hbm(yb), _hbm(wo), _hbm(wo), post_g3)


def _gmlp_bwd_dw(pa, kept, rstd_w, dya, h, dqz, dkv, ln_g3, ln_b3, wmt, layer):
    tokens = pa.shape[0]
    tc = _token_tile(tokens)
    steps = tokens // tc

    def body(pz_ref, kept_ref, rstd_ref, dya_ref, h_ref, dqz_ref, dkv_ref, lg_ref, lb_ref, wmt_ref,
             da_ref, dw_ref, dws_ref, dbs_ref, dlg_ref, dlb_ref, acc_ref, dbsb_ref):
        i = pl.program_id(0)

        @pl.when(i == 0)
        def _():
            acc_ref[...] = jnp.zeros_like(acc_ref)
            dws_ref[...] = jnp.zeros_like(dws_ref)
            dlg_ref[...] = jnp.zeros_like(dlg_ref)
            dlb_ref[...] = jnp.zeros_like(dlb_ref)
            dbsb_ref[...] = jnp.zeros_like(dbsb_ref)

        ln_g = lg_ref[...]
        low = lax.broadcasted_iota(jnp.int32, (CHUNK, LANES), 1) < HEAD_DIM
        hv = h_ref[...]
        acc_ref[:, PA_WIDTH:PA_WIDTH + B_WIDTH] += _mm_tn(hv, dqz_ref[:, 0:B_WIDTH])
        acc_ref[:, PA_WIDTH + K_OFF:PA_WIDTH + BZ_OFF] += _mm_tn(hv, dkv_ref[...])
        acc_ref[:, PA_WIDTH + BZ_OFF:IN_WIDTH] += _mm_tn(hv, dqz_ref[:, B_WIDTH:2 * B_WIDTH])

        for ci in range(tc // CHUNK):
            rows = slice(ci * CHUNK, (ci + 1) * CHUNK)
            u, gu, gv, vhat, mixed = (kept_ref[rows, A_WIDTH * k:A_WIDTH * (k + 1)].astype(F32) for k in range(5))
            rstd = jnp.concatenate([rstd_ref[rows, :]] * (A_WIDTH // LANES), axis=1)
            vnb = (vhat * ln_g + lb_ref[...]).astype(BF16)
            pz = pz_ref[rows, :].astype(F32)
            sg = _sigmoid(pz)
            dy = dya_ref[rows, :].astype(F32)
            sz = pz * sg
            dy_sz = dy * sz
            du = dy_sz * mixed
            dmixed = dy_sz * u
            dz = dy * (u * mixed) * (sg + sz * (1.0 - sg))
            dbsb_ref[...] += dmixed
            dmb = dmixed.astype(BF16)
            zero = jnp.zeros((CHUNK, LANES), BF16)
            parts = []
            for p in range(A_GROUPS // 2):
                dmp = dmb[:, LANES * p:LANES * (p + 1)]
                vp = vnb[:, LANES * p:LANES * (p + 1)]
                parts.append(jnp.where(low, _mm(wmt_ref[2 * p], dmp), _mm(wmt_ref[2 * p + 1], dmp)))
                dws_ref[2 * p] += _mm_nt(jnp.where(low, dmp, zero), vp)
                dws_ref[2 * p + 1] += _mm_nt(jnp.where(low, zero, dmp), vp)
            dvn = jnp.concatenate(parts, axis=1)
            dlg_ref[...] += jnp.sum(dvn * vhat, axis=0, keepdims=True)
            dlb_ref[...] += jnp.sum(dvn, axis=0, keepdims=True)
            dvh = dvn * ln_g
            dvv = rstd * (dvh - jnp.mean(dvh, axis=-1, keepdims=True)
                          - vhat * jnp.mean(dvh * vhat, axis=-1, keepdims=True))
            da_ref[rows, 0:A_WIDTH] = (du * gu).astype(BF16)
            da_ref[rows, A_WIDTH:2 * A_WIDTH] = (dvv * gv).astype(BF16)
            da_ref[rows, 2 * A_WIDTH:3 * A_WIDTH] = dz.astype(BF16)

        acc_ref[:, 0:PA_WIDTH] += _mm_tn(hv, da_ref[...])

        @pl.when(i == steps - 1)
        def _():
            for c in range(IN_WIDTH // LANES):
                cols = slice(LANES * c, LANES * (c + 1))
                dw_ref[cols, :] = acc_ref[:, cols].T.astype(BF16)
            causal = (lax.broadcasted_iota(jnp.int32, (CHUNK, CHUNK), 0)
                      >= lax.broadcasted_iota(jnp.int32, (CHUNK, CHUNK), 1))
            for h in range(A_GROUPS):
                dws_ref[h] = jnp.where(causal, dws_ref[h], 0.0)
            acc = dbsb_ref[...]
            lane_full = lax.broadcasted_iota(jnp.int32, (CHUNK, A_WIDTH), 1)
            lane_out = lax.broadcasted_iota(jnp.int32, (CHUNK, LANES), 1)
            out = jnp.zeros((CHUNK, LANES), F32)
            for h in range(A_GROUPS):
                in_group = jnp.logical_and(lane_full >= HEAD_DIM * h, lane_full < HEAD_DIM * (h + 1))
                s = jnp.sum(jnp.where(in_group, acc, 0.0), axis=-1, keepdims=True)
                out = jnp.where(lane_out == h, s, out)
            dbs_ref[...] = out

    blocks = [_nbytes((tc, KEPT_WIDTH), BF16), 2 * _nbytes((tc, A_WIDTH), BF16), _nbytes((A_GROUPS, CHUNK, CHUNK), BF16),
              _nbytes((tc, LANES), F32), _nbytes((tc, PA_WIDTH), BF16), _nbytes((A_GROUPS, CHUNK, CHUNK), F32),
              _nbytes((CHUNK, LANES), F32), _nbytes((tc, D_MODEL), BF16), _nbytes((tc, 2 * B_WIDTH), BF16),
              _nbytes((tc, 2 * KV_WIDTH), BF16), _nbytes((D_MODEL, IN_WIDTH), BF16)]
    scratch = _nbytes((D_MODEL, IN_WIDTH), F32) + _nbytes((CHUNK, A_WIDTH), F32)
    in_specs = [pl.BlockSpec((tc, A_WIDTH), lambda i: (i, 2)),
                pl.BlockSpec((tc, KEPT_WIDTH), lambda i: (i, 0)),
                pl.BlockSpec((tc, LANES), lambda i: (i, 0)),
                pl.BlockSpec((tc, A_WIDTH), lambda i: (i, 0)),
                pl.BlockSpec((tc, D_MODEL), lambda i: (i, 0)),
                pl.BlockSpec((tc, 2 * B_WIDTH), lambda i: (i, 0)),
                pl.BlockSpec((pl.Element(tc), pl.Element(2 * KV_WIDTH)),
                             lambda i: (pl.multiple_of(i * tc + BLOCK, BLOCK), 0)),
                pl.BlockSpec((None, 1, A_WIDTH), lambda i: (layer, 0, 0)),
                pl.BlockSpec((None, 1, A_WIDTH), lambda i: (layer, 0, 0)),
                pl.BlockSpec((None, A_GROUPS, CHUNK, CHUNK), lambda i: (layer, 0, 0, 0))]
    operands = [_hbm(pa), _hbm(kept), _hbm(rstd_w), _hbm(dya), _hbm(h), _hbm(dqz), _hbm(dkv), ln_g3, ln_b3, wmt]
    out_specs = [pl.BlockSpec((tc, PA_WIDTH), lambda i: (i, 0)),
                 pl.BlockSpec((IN_WIDTH, D_MODEL), lambda i: (0, 0)),
                 pl.BlockSpec((A_GROUPS, CHUNK, CHUNK), lambda i: (0, 0, 0)),
                 pl.BlockSpec((CHUNK, LANES), lambda i: (0, 0)),
                 pl.BlockSpec((1, A_WIDTH), lambda i: (0, 0)),
                 pl.BlockSpec((1, A_WIDTH), lambda i: (0, 0))]
    out_shape = [pltpu.HBM((tokens, PA_WIDTH), BF16),
                 pltpu.HBM((IN_WIDTH, D_MODEL), BF16),
                 pltpu.HBM((A_GROUPS, CHUNK, CHUNK), F32),
                 pltpu.HBM((CHUNK, LANES), F32),
                 pltpu.HBM((1, A_WIDTH), F32),
                 pltpu.HBM((1, A_WIDTH), F32)]
    scratch_shapes = [pltpu.VMEM((D_MODEL, IN_WIDTH), F32), pltpu.VMEM((CHUNK, A_WIDTH), F32)]
    return pl.pallas_call(
        body, name=f"gmlp_bwd_dw_{layer}", grid=(steps,),
        in_specs=in_specs, out_specs=out_specs, out_shape=out_shape, scratch_shapes=scratch_shapes,
        compiler_params=pltpu.CompilerParams(dimension_semantics=("arbitrary",),
                                             vmem_limit_bytes=_vmem_limit(blocks, scratch)),
    )(*operands)


def _attn_bwd(pb, dyb, probs_w, sink_w, dbias_in, layer):
    tokens = pb.shape[0]
    qz_width = 2 * B_WIDTH
    tq = _token_tile(tokens)
    per_tile = tq // BLOCK
    nt = tokens // tq

    def body(cur_ref, prev_ref, dyb_ref, pw_ref, spw_ref, dbias_in_ref, dqz_ref, dkv_ref, dbias_ref, dsink_ref,
             kv_ref, acc_ref):
        t = pl.program_id(0)

        @pl.when(t == 0)
        def _():
            dbias_ref[...] = dbias_in_ref[...]
            dsink_ref[...] = jnp.zeros_like(dsink_ref)
            acc_ref[0:BLOCK, :] = jnp.zeros((BLOCK, 2 * KV_WIDTH), F32)

        @pl.when(t < nt)
        def _():
            acc_ref[BLOCK:, :] = jnp.zeros((tq, 2 * KV_WIDTH), F32)
            _fill_kv(kv_ref, prev_ref, cur_ref)
            tri = _band_masks()
            low = lax.broadcasted_iota(jnp.int32, (BLOCK, LANES), 1) < HEAD_DIM
            low_kv = lax.broadcasted_iota(jnp.int32, (2 * BLOCK, LANES), 1) < HEAD_DIM

            head_lane = lax.broadcasted_iota(jnp.int32, (BLOCK, LANES), 1)

            def block(i, carry):
                start = i * BLOCK
                rows = pl.ds(start, BLOCK)
                kv = kv_ref[pl.ds(start, 2 * BLOCK), :]
                k_ops = _kv_variants(kv[:, 0:KV_WIDTH])
                v_ops = _kv_variants(kv[:, KV_WIDTH:2 * KV_WIDTH])
                probs, p16, qs, dos, delta, ds16 = {}, {}, {}, {}, {}, {}
                for hd in range(B_HEADS):
                    kept_p = pw_ref[rows, LANES * hd:LANES * (hd + 1)]
                    probs[hd] = kept_p.astype(F32)
                    p16[hd] = _unwrap16(kept_p, tri)
                for kvh in range(2):
                    qs[kvh] = jnp.concatenate(
                        [cur_ref[rows, LANES * p:LANES * (p + 1)] for p in (2 * kvh, 2 * kvh + 1)], axis=0)
                for kvh in range(2):
                    pairs = (2 * kvh, 2 * kvh + 1)
                    out = jnp.zeros((2 * BLOCK, LANES), F32)
                    for j in range(2):
                        out = out + _mm(jnp.concatenate([p16[2 * p + j] for p in pairs], axis=0), v_ops[kvh][j])
                    d_outs = []
                    for r, p in enumerate(pairs):
                        bz = cur_ref[rows, BZ_OFF + LANES * p:BZ_OFF + LANES * (p + 1)].astype(F32)
                        sg = _sigmoid(bz)
                        dyp = dyb_ref[rows, LANES * p:LANES * (p + 1)].astype(F32)
                        out_p = out[BLOCK * r:BLOCK * (r + 1)]
                        d_out = dyp * (bz * sg)
                        dqz_ref[rows, B_WIDTH + LANES * p:B_WIDTH + LANES * (p + 1)] = (
                            dyp * out_p * (sg * (1.0 + bz * (1.0 - sg)))).astype(BF16)
                        dod = d_out * out_p
                        delta[2 * p] = jnp.sum(jnp.where(low, dod, 0.0), axis=-1, keepdims=True)
                        delta[2 * p + 1] = jnp.sum(jnp.where(low, 0.0, dod), axis=-1, keepdims=True)
                        d_outs.append(d_out.astype(BF16))
                    dos[kvh] = jnp.concatenate(d_outs, axis=0)
                delta_cols = jnp.zeros((BLOCK, LANES), F32)
                for hd in range(B_HEADS):
                    delta_cols = jnp.where(head_lane == hd, delta[hd], delta_cols)
                dsink_ref[0:1, :] += jnp.sum(-(spw_ref[rows, :] * delta_cols), axis=0, keepdims=True)
                for kvh in range(2):
                    pairs = (2 * kvh, 2 * kvh + 1)
                    for j in range(2):
                        dpf = _mm_nt(dos[kvh], v_ops[kvh][j])
                        for r, p in enumerate(pairs):
                            hd = 2 * p + j
                            ds = probs[hd] * (_wrap(dpf[BLOCK * r:BLOCK * (r + 1)], tri) - delta[hd])
                            dbias_ref[hd] += ds
                            ds16[hd] = _unwrap16(ds, tri)
                dk_acc = [[None, None], [None, None]]
                dv_acc = [[None, None], [None, None]]
                for kvh in range(2):
                    pairs = (2 * kvh, 2 * kvh + 1)
                    dq = jnp.zeros((2 * BLOCK, LANES), F32)
                    for j in range(2):
                        dss = jnp.concatenate([ds16[2 * p + j] for p in pairs], axis=0)
                        pss = jnp.concatenate([p16[2 * p + j] for p in pairs], axis=0)
                        dq = dq + _mm(dss, k_ops[kvh][j])
                        dk_acc[kvh][j] = _mm_tn(dss, qs[kvh])
                        dv_acc[kvh][j] = _mm_tn(pss, dos[kvh])
                    for r, p in enumerate(pairs):
                        dqz_ref[rows, LANES * p:LANES * (p + 1)] = (dq[BLOCK * r:BLOCK * (r + 1)] * SCALE).astype(BF16)

                def fold(acc):
                    return jnp.where(low_kv,
                                     acc[0][0] + pltpu.roll(acc[0][1], HEAD_DIM, 1),
                                     pltpu.roll(acc[1][0], HEAD_DIM, 1) + acc[1][1])

                acc_ref[pl.ds(start, 2 * BLOCK), :] += jnp.concatenate(
                    [fold(dk_acc) * SCALE, fold(dv_acc)], axis=1)
                return carry

            for i in range(per_tile):
                block(i, 0)
            dkv_ref[...] = acc_ref[0:tq, :].astype(BF16)
            acc_ref[0:BLOCK, :] = acc_ref[tq:tq + BLOCK, :]

        @pl.when(t == nt)
        def _():
            dkv_ref[0:BLOCK, :] = acc_ref[0:BLOCK, :].astype(BF16)
            dkv_ref[BLOCK:, :] = jnp.zeros((tq - BLOCK, 2 * KV_WIDTH), BF16)

    def cur_map(t):
        return (jnp.minimum(t, nt - 1), 0)

    def prev_map(t):
        return (jnp.maximum(jnp.minimum(t, nt - 1) * per_tile - 1, 0), K_OFF // (2 * KV_WIDTH))

    blocks = [_nbytes((tq, PB_WIDTH), BF16), _nbytes((BLOCK, 2 * KV_WIDTH), BF16), _nbytes((tq, B_WIDTH), BF16),
              2 * _nbytes((B_HEADS, BLOCK, BLOCK), F32), _nbytes((tq, qz_width), BF16),
              _nbytes((tq, 2 * KV_WIDTH), BF16), _nbytes((B_HEADS, LANES), F32),
              _nbytes((tq, B_HEADS * BLOCK), BF16), _nbytes((tq, LANES), F32)]
    scratch = _nbytes((tq + BLOCK, 2 * KV_WIDTH), BF16) + _nbytes((tq + BLOCK, 2 * KV_WIDTH), F32)
    return pl.pallas_call(
        body, name=f"attn_bwd_{layer}", grid=(nt + 1,),
        in_specs=[pl.BlockSpec((tq, PB_WIDTH), cur_map),
                  pl.BlockSpec((BLOCK, 2 * KV_WIDTH), prev_map),
                  pl.BlockSpec((tq, B_WIDTH), cur_map),
                  pl.BlockSpec((tq, B_HEADS * BLOCK), cur_map),
                  pl.BlockSpec((tq, LANES), cur_map),
                  pl.BlockSpec((B_HEADS, BLOCK, BLOCK), lambda t: (0, 0, 0))],
        out_specs=[pl.BlockSpec((tq, qz_width), cur_map),
                   pl.BlockSpec((tq, 2 * KV_WIDTH), lambda t: (t, 0)),
                   pl.BlockSpec((B_HEADS, BLOCK, BLOCK), lambda t: (0, 0, 0)),
                   pl.BlockSpec((B_HEADS, LANES), lambda t: (0, 0))],
        out_shape=[pltpu.HBM((tokens, qz_width), BF16),
                   pltpu.HBM((tokens + tq, 2 * KV_WIDTH), BF16),
                   pltpu.HBM((B_HEADS, BLOCK, BLOCK), F32),
                   pltpu.HBM((B_HEADS, LANES), F32)],
        scratch_shapes=[pltpu.VMEM((tq + BLOCK, 2 * KV_WIDTH), BF16),
                        pltpu.VMEM((tq + BLOCK, 2 * KV_WIDTH), F32)],
        compiler_params=pltpu.CompilerParams(dimension_semantics=("arbitrary",),
                                             vmem_limit_bytes=_vmem_limit(blocks, scratch)),
    )(_hbm(pb), _hbm(pb), _hbm(dyb), _hbm(probs_w), _hbm(sink_w), _hbm(dbias_in))


def _inproj_bwd_dx(da, dqz, dkv, w, x2, g, pre_g3, layer):
    tokens = x2.shape[0]
    tm = _token_tile(tokens)

    def body(da_ref, dqz_ref, dkv_ref, w_ref, x_ref, g_ref, pg_ref, gn_ref, dpg_ref):
        @pl.when(pl.program_id(0) == 0)
        def _():
            dpg_ref[...] = jnp.zeros_like(dpg_ref)

        dh = _mm(da_ref[...], w_ref[0:PA_WIDTH, :])
        dh += _mm(dqz_ref[:, 0:B_WIDTH], w_ref[PA_WIDTH:PA_WIDTH + B_WIDTH, :])
        dh += _mm(dkv_ref[...], w_ref[PA_WIDTH + K_OFF:PA_WIDTH + BZ_OFF, :])
        dh += _mm(dqz_ref[:, B_WIDTH:2 * B_WIDTH], w_ref[PA_WIDTH + BZ_OFF:IN_WIDTH, :])
        x = x_ref[...]
        r = lax.rsqrt(jnp.mean(x * x, axis=-1, keepdims=True) + NORM_EPS)
        xhat = x * r
        dhg = dh * pg_ref[...]
        dpg_ref[...] += jnp.sum(dh * xhat, axis=0, keepdims=True)
        gn_ref[...] = g_ref[...] + r * (dhg - xhat * jnp.mean(dhg * xhat, axis=-1, keepdims=True))

    blocks = [_nbytes((tm, PA_WIDTH), BF16), _nbytes((tm, 2 * B_WIDTH), BF16), _nbytes((tm, 2 * KV_WIDTH), BF16),
              _nbytes((D_MODEL, IN_WIDTH), BF16), 3 * _nbytes((tm, D_MODEL), F32)]
    return pl.pallas_call(
        body, name=f"inproj_bwd_dx_{layer}", grid=(tokens // tm,),
        in_specs=[pl.BlockSpec((tm, PA_WIDTH), lambda i: (i, 0)),
                  pl.BlockSpec((tm, 2 * B_WIDTH), lambda i: (i, 0)),
                  pl.BlockSpec((pl.Element(tm), pl.Element(2 * KV_WIDTH)), lambda i: (pl.multiple_of(i * tm + BLOCK, BLOCK), 0)),
                  pl.BlockSpec((IN_WIDTH, D_MODEL), lambda i: (0, 0)),
                  pl.BlockSpec((tm, D_MODEL), lambda i: (i, 0)),
                  pl.BlockSpec((tm, D_MODEL), lambda i: (i, 0)),
                  pl.BlockSpec((None, 1, D_MODEL), lambda i: (layer, 0, 0))],
        out_specs=[pl.BlockSpec((tm, D_MODEL), lambda i: (i, 0)),
                   pl.BlockSpec((1, D_MODEL), lambda i: (0, 0))],
        out_shape=[pltpu.HBM((tokens,D_MODEL), F32),
                   pltpu.HBM((1,D_MODEL), F32)],
        compiler_params=pltpu.CompilerParams(dimension_semantics=("arbitrary",),
                                             vmem_limit_bytes=_vmem_limit(blocks)),
    )(_hbm(da), _hbm(dqz), _hbm(dkv), _hbm(w), _hbm(x2), _hbm(g), pre_g3)


def _rel_bias_grad(dbias, col_bucket, flip):
    def body(db_ref, cb_ref, flip_ref, out_ref):
        cb = cb_ref[...]
        anti = flip_ref[...]
        sums = []
        for h in range(B_HEADS):
            x = db_ref[h]
            hi = x.astype(BF16)
            rest = x - hi.astype(F32)
            mid = rest.astype(BF16)
            low = (rest - mid.astype(F32)).astype(BF16)
            reversed_x = _mm(hi, anti) + _mm(mid, anti) + _mm(low, anti)
            rolled = pltpu.roll(reversed_x, 0, 1, stride=1, stride_axis=0)
            sums.append(jnp.sum(rolled, axis=0, keepdims=True))
        per_dist = jnp.concatenate(sums, axis=0)
        lane = lax.broadcasted_iota(jnp.int32, (B_HEADS, LANES), 1)
        out = jnp.zeros((B_HEADS, LANES), F32)
        for b in range(REL_BUCKETS):
            s = jnp.sum(jnp.where(cb == b, per_dist, 0.0), axis=-1, keepdims=True)
            out = jnp.where(lane == b, s, out)
        out_ref[...] = out

    vm = pl.BlockSpec(memory_space=pltpu.VMEM)
    return pl.pallas_call(
        body, name="rel_bias_grad",
        out_shape=jax.ShapeDtypeStruct((B_HEADS, LANES), F32),
        in_specs=[vm, vm, vm], out_specs=vm,
    )(dbias, col_bucket, flip)


def _reduce_adamw(slots, w, m, v, name):
    rows, cols = w.shape
    tr = _row_tile(rows)
    c1 = 1.0 / (1.0 - ADAM_B1 ** ADAM_STEP)
    c2 = 1.0 / (1.0 - ADAM_B2 ** ADAM_STEP)

    def body(s_ref, w_ref, m_ref, v_ref, g_ref, d_ref, nm_ref, nv_ref):
        g = s_ref[0].astype(F32)
        for i in range(1, N_DEV):
            g = g + s_ref[i].astype(F32)
        nm = ADAM_B1 * m_ref[...] + (1.0 - ADAM_B1) * g
        nv = ADAM_B2 * v_ref[...] + (1.0 - ADAM_B2) * (g * g)
        g_ref[...] = g
        nm_ref[...] = nm
        nv_ref[...] = nv
        d_ref[...] = -ADAM_LR * ((nm * c1) / (jnp.sqrt(nv * c2) + ADAM_EPS) + ADAM_WD * w_ref[...])

    blocks = [_nbytes((N_DEV, tr, cols), slots.dtype), 7 * _nbytes((tr, cols), F32)]
    tile = pl.BlockSpec((tr, cols), lambda i: (i, 0))
    return pl.pallas_call(
        body, name=name, grid=(rows // tr,),
        in_specs=[pl.BlockSpec((N_DEV, tr, cols), lambda i: (0, i, 0)), tile, tile, tile],
        out_specs=[tile] * 4,
        out_shape=[pltpu.HBM((rows, cols), F32)] * 4,
        compiler_params=pltpu.CompilerParams(dimension_semantics=("parallel",),
                                             vmem_limit_bytes=_vmem_limit(blocks)),
    )(_hbm(slots), _hbm(w), _hbm(m), _hbm(v))


_SMALL = ("pre_norm_g", "post_norm_g", "ln_v_g", "ln_v_b", "b_spatial", "sinks", "rel_bias", "loss")


def _pack_small(parts):
    slabs = []
    for name in _SMALL:
        flat = parts[name].astype(F32).reshape(-1)
        pad = (-flat.shape[0]) % (8 * LANES)
        slabs.append(jnp.pad(flat, (0, pad)).reshape(-1, LANES))
    return jnp.concatenate(slabs, axis=0)


def _unpack_small(slab, shapes):
    out, row = {}, 0
    for name in _SMALL:
        size = int(np.prod(shapes[name]))
        rows = -(-size // (8 * LANES)) * 8
        out[name] = slab[row:row + rows].reshape(-1)[:size].reshape(shapes[name])
        row += rows
    return out


def kernel(x, pre_norm_g, w_in, ln_v_g, ln_v_b, w_spatial, b_spatial, sinks, rel_bias, w_out, post_norm_g, loss_target, m_pre_norm_g, m_w_in, m_ln_v_g, m_ln_v_b, m_w_spatial, m_b_spatial, m_sinks, m_rel_bias, m_w_out, m_post_norm_g, v_pre_norm_g, v_w_in, v_ln_v_g, v_ln_v_b, v_w_spatial, v_b_spatial, v_sinks, v_rel_bias, v_w_out, v_post_norm_g):
    b_loc, seq, _ = x.shape
    tokens = b_loc * seq
    depth = w_in.shape[0]
    in_shard = w_in.shape[2]
    out_shard = w_out.shape[1]
    assert in_shard * N_DEV == IN_WIDTH and out_shard * N_DEV == D_MODEL and seq % BLOCK == 0

    w_in_t, m_w_in_t, v_w_in_t = (jnp.swapaxes(a, 1, 2) for a in (w_in, m_w_in, v_w_in))
    w_in16, w_out16 = w_in_t.astype(BF16), w_out.astype(BF16)

    def gather_start(layer, not_before):
        zones = [lax.empty((N_DEV, in_shard, D_MODEL), BF16), lax.empty((N_DEV, out_shard, D_MODEL), BF16)]
        sems, sends, zones, pre_g = _exchange_start(
            [w_in16[layer], w_out16[layer]], zones, [0, 0], f"weights_send_{layer}",
            pre_norm_g.reshape(depth, 1, D_MODEL), after=not_before)
        return (sems, sends, [0, 0]), zones, pre_g

    def full_weights(gin, gout):
        return gin.reshape(IN_WIDTH, D_MODEL), gout.reshape(D_MODEL, D_MODEL)

    def gather_wait(layer, started, zones, after):
        return full_weights(*_exchange_wait([started], zones, after, f"weights_wait_{layer}"))

    causal = jnp.tril(jnp.ones((CHUNK, CHUNK), dtype=bool))
    wm = jnp.where(causal, w_spatial, 0.0).astype(BF16)
    wmt = _hbm(jnp.swapaxes(wm, -1, -2))
    wm = _hbm(wm)
    bsb = _hbm(jnp.repeat(jnp.swapaxes(b_spatial, -1, -2), HEAD_DIM, axis=-1))
    pre_g3 = _hbm(pre_norm_g.reshape(depth, 1, D_MODEL))
    post_g3 = _hbm(post_norm_g.reshape(depth, 1, D_MODEL))
    ln_g3 = _hbm(ln_v_g.reshape(depth, 1, A_WIDTH))
    ln_b3 = _hbm(ln_v_b.reshape(depth, 1, A_WIDTH))
    bucket = jnp.asarray(_bucket_table())
    bias = _bias_table(rel_bias, bucket)

    xs, saved, weights = [x.reshape(tokens, D_MODEL)], [], []
    pending = None
    for layer in range(depth):
        if layer == 0:
            w, wo = full_weights(*_all_gather([w_in16[0], w_out16[0]], "weights_gather_0"))
        else:
            w, wo = gather_wait(layer, pending[0], pending[1], xs[-1])
        pre_g_fwd = pre_g3
        if layer + 1 < depth:
            pending = gather_start(layer + 1, w)
            pre_g_fwd = _hbm(pending[2])
        h, pa, pb, ya, kept, rstd_w = _inproj_gmlp_fwd(xs[-1], pre_g_fwd, w, ln_g3, ln_b3, wm, bsb, layer)
        if layer + 1 < depth:
            yb, pw, spw, y, x_next = _attn_outproj_fwd(pb, bias, sinks, ya, wo, xs[-1], post_g3, b_loc, layer)
            xs.append(x_next)
        else:
            yb, pw, spw, y, g, loss_part = _attn_outproj_fwd(pb, bias, sinks, ya, wo, xs[-1], post_g3, b_loc, layer,
                                                    target=loss_target.reshape(tokens, D_MODEL))
        saved.append((h, pa, pb, ya, yb, y, pw, spw, kept, rstd_w))
        weights.append((w, wo))

    grads = {name: [None] * depth for name in ("pre_norm_g", "post_norm_g", "ln_v_g", "ln_v_b", "b_spatial", "sinks")}
    zone_in = lax.empty((N_DEV, depth * in_shard, D_MODEL), BF16)
    zone_out = lax.empty((N_DEV, depth * out_shard, D_MODEL), BF16)
    zone_ws = lax.empty((N_DEV, depth * A_GROUPS * CHUNK, CHUNK), F32)
    started_in, started_out = [], []
    dbias = jnp.zeros((B_HEADS, BLOCK, BLOCK), F32)
    for layer in reversed(range(depth)):
        h, pa, pb, ya, yb, y, pw, spw, kept, rstd_w = saved[layer]
        w, wo = weights[layer]
        dya, dyb, dwo, dpost = _outproj_bwd(g, y, ya, yb, wo, post_g3, layer)
        send_out = dwo.reshape(N_DEV, out_shard, D_MODEL)
        sems, sends, (zone_out,), ln_g_bwd = _exchange_start(
            [send_out], [zone_out], [layer * out_shard], f"grads_send_out_{layer}", ln_v_g.reshape(depth, 1, A_WIDTH))
        started_out.append((sems, sends, [layer * out_shard]))
        dqz, dkv, dbias, dsink = _attn_bwd(pb, dyb, pw, spw, dbias, layer)
        da, dw, dws, dbs, dlg, dlb = _gmlp_bwd_dw(pa, kept, rstd_w, dya, h, dqz, dkv, _hbm(ln_g_bwd), ln_b3, wmt, layer)
        send_in = dw.reshape(N_DEV, in_shard, D_MODEL)
        row_offs = [layer * A_GROUPS * CHUNK, layer * in_shard]
        sems, sends, (zone_ws, zone_in), pre_g_bwd = _exchange_start(
            [dws.reshape(A_GROUPS * CHUNK, CHUNK), send_in], [zone_ws, zone_in], row_offs, f"grads_send_in_{layer}",
            pre_norm_g.reshape(depth, 1, D_MODEL))
        started_in.append((sems, sends, row_offs))
        g, dpre = _inproj_bwd_dx(da, dqz, dkv, w, xs[layer], g, _hbm(pre_g_bwd), layer)
        grads["b_spatial"][layer] = dbs[:, :A_GROUPS].T
        grads["ln_v_g"][layer] = dlg[0]
        grads["ln_v_b"][layer] = dlb[0]
        grads["sinks"][layer] = dsink[0, :B_HEADS]
        grads["pre_norm_g"][layer] = dpre[0]
        grads["post_norm_g"][layer] = dpost[0]
    grad_x = g.reshape(x.shape)
    col_bucket = jnp.asarray(np.broadcast_to(_bucket_table()[0:1, ::-1], (B_HEADS, BLOCK)))
    flip = jnp.asarray(np.eye(BLOCK, dtype=np.float32)[::-1], dtype=BF16)
    drel = _rel_bias_grad(dbias, col_bucket, flip)[:, :REL_BUCKETS].T

    no_state = jnp.zeros((1,), F32)
    small_w = dict(pre_norm_g=pre_norm_g, post_norm_g=post_norm_g, ln_v_g=ln_v_g,
                   ln_v_b=ln_v_b, b_spatial=b_spatial, sinks=sinks, rel_bias=rel_bias, loss=no_state)
    small_m = dict(pre_norm_g=m_pre_norm_g, post_norm_g=m_post_norm_g, ln_v_g=m_ln_v_g,
                   ln_v_b=m_ln_v_b, b_spatial=m_b_spatial, sinks=m_sinks, rel_bias=m_rel_bias, loss=no_state)
    small_v = dict(pre_norm_g=v_pre_norm_g, post_norm_g=v_post_norm_g, ln_v_g=v_ln_v_g,
                   ln_v_b=v_ln_v_b, b_spatial=v_b_spatial, sinks=v_sinks, rel_bias=v_rel_bias, loss=no_state)
    small_g = {name: jnp.stack(grads[name]) for name in _SMALL if name not in ("rel_bias", "loss")}
    small_g["rel_bias"] = drel
    small_g["loss"] = loss_part[0, :1]
    shapes = {name: small_w[name].shape for name in _SMALL}
    slab = _pack_small(small_g)
    sems, sends, zone_small, w_out_rows = _exchange_start(
        [slab], [lax.empty((N_DEV,) + slab.shape, F32)], [0], "small_grads_send", w_out.reshape(-1, D_MODEL))

    (recv_out,) = _exchange_wait(started_out, [zone_out], g, "grads_wait_out")
    res_out = _reduce_adamw(recv_out, w_out_rows, m_w_out.reshape(-1, D_MODEL), v_w_out.reshape(-1, D_MODEL),
                            "adamw_w_out")
    res_out = [r.reshape(w_out.shape) for r in res_out]
    recv_ws, recv_in = _exchange_wait(started_in, [zone_ws, zone_in], res_out[0], "grads_wait_in")
    res_ws = _reduce_adamw(recv_ws, w_spatial.reshape(-1, CHUNK), m_w_spatial.reshape(-1, CHUNK),
                           v_w_spatial.reshape(-1, CHUNK), "adamw_w_spatial")
    res_ws = [r.reshape(w_spatial.shape) for r in res_ws]
    res_in = _reduce_adamw(recv_in, w_in_t.reshape(-1, D_MODEL), m_w_in_t.reshape(-1, D_MODEL),
                           v_w_in_t.reshape(-1, D_MODEL), "adamw_w_in")
    (slots,) = _exchange_wait([(sems, sends, [0])], zone_small, res_in[0], "small_grads_wait")
    res_small = _reduce_adamw(slots, _pack_small(small_w), _pack_small(small_m), _pack_small(small_v), "adamw_small")
    res_in = [jnp.swapaxes(r.reshape(w_in_t.shape), 1, 2) for r in res_in]
    res_small = [_unpack_small(r, shapes) for r in res_small]

    order = ("pre_norm_g", "w_in", "ln_v_g", "ln_v_b", "w_spatial", "b_spatial", "sinks", "rel_bias", "w_out",
             "post_norm_g")
    outs = [res_small[0]["loss"][0], grad_x]
    for kind in range(4):
        for name in order:
            if name == "w_in":
                outs.append(res_in[kind])
            elif name == "w_out":
                outs.append(res_out[kind])
            elif name == "w_spatial":
                outs.append(res_ws[kind])
            else:
                outs.append(res_small[kind][name])
    return tuple(outs)
```

```python
import math

import numpy as np
import jax
import jax.numpy as jnp
from jax import lax
from jax.experimental import pallas as pl
from jax.experimental.pallas import tpu as pltpu

F32 = jnp.float32
BF16 = jnp.bfloat16

D_MODEL = 1024
A_WIDTH = 512
A_GROUPS = 8
CHUNK = 128
B_HEADS = 8
HEAD_DIM = 64
B_WIDTH = 512
KV_WIDTH = 128
BLOCK = 128
REL_BUCKETS = 32
REL_MAX_DIST = 128
NORM_EPS = 1e-6
PA_WIDTH = 3 * A_WIDTH
PB_WIDTH = 2 * B_WIDTH + 2 * KV_WIDTH
IN_WIDTH = PA_WIDTH + PB_WIDTH
K_OFF, BZ_OFF = B_WIDTH, B_WIDTH + 2 * KV_WIDTH
KEPT_WIDTH = 5 * A_WIDTH
SCALE = HEAD_DIM ** -0.5
NEG = -1e30
N_DEV = 8
LANES = 128

ADAM_LR = 0.001
ADAM_B1 = 0.9
ADAM_B2 = 0.999
ADAM_EPS = 1e-08
ADAM_WD = 0.01
ADAM_STEP = 10

V7X_VMEM_BYTES = 64 * 1024 * 1024
VMEM_TEMP_BYTES = 12 * 1024 * 1024
MESH = pl.DeviceIdType.MESH


def _vmem_limit(block_bytes, scratch_bytes=0):
    need = 2 * sum(block_bytes) + scratch_bytes + VMEM_TEMP_BYTES
    return int(min(need, V7X_VMEM_BYTES - 8 * 1024 * 1024))


def _nbytes(shape, dtype):
    return int(np.prod(shape)) * jnp.dtype(dtype).itemsize


def _token_tile(tokens, cap=512):
    tile = min(cap, tokens // 2)
    assert tokens % tile == 0 and tile % CHUNK == 0, tokens
    return tile


def _row_tile(rows, cap=512):
    best = 8
    for t in range(8, cap + 1, 8):
        if rows % t == 0:
            best = t
    assert rows % best == 0, rows
    return best


def _mm(a, b):
    return lax.dot_general(a, b, (((1,), (0,)), ((), ())), preferred_element_type=F32)


def _mm_nt(a, b):
    return lax.dot_general(a, b, (((1,), (1,)), ((), ())), preferred_element_type=F32)


def _mm_tn(a, b):
    return lax.dot_general(a, b, (((0,), (0,)), ((), ())), preferred_element_type=F32)


_GELU_C = math.sqrt(2.0 / math.pi)


_GELU_A = _GELU_C * 0.044715


def _gelu_parts(x):
    x2 = x * x
    t = jnp.tanh(x * (_GELU_C + _GELU_A * x2))
    return x2, t, 0.5 + 0.5 * t


def _gelu_and_grad(x):
    x2, t, half_plus = _gelu_parts(x)
    grad = half_plus + (0.5 * x) * (1.0 - t * t) * (_GELU_C + (3.0 * _GELU_A) * x2)
    return x * half_plus, grad


def _sigmoid(x):
    return 0.5 + 0.5 * jnp.tanh(0.5 * x)


def _bucket_table():
    q = np.arange(BLOCK)[:, None]
    k = np.arange(BLOCK)[None, :]
    dist = np.where(k <= q, q - k, q + BLOCK - k)
    max_exact = REL_BUCKETS // 2
    safe = np.maximum(dist, 1).astype(np.float32)
    large = max_exact + (np.log(safe / np.float32(max_exact)) / np.float32(math.log(REL_MAX_DIST / max_exact))
                         * np.float32(REL_BUCKETS - max_exact)).astype(np.int32)
    large = np.minimum(large, REL_BUCKETS - 1)
    assert dist.min() >= 0 and dist.max() < BLOCK
    return np.where(dist < max_exact, dist, large).astype(np.int32)


def _hbm(x):
    return pltpu.with_memory_space_constraint(x, pltpu.HBM)


def _slot(px, py, pc):
    return 4 * px + 2 * py + pc


def _all_gather(arrs, name):
    n = len(arrs)

    def body(*refs):
        ins, outs = refs[:n], refs[n:2 * n]
        send_sems, recv_sems, local_sems = refs[2 * n:]
        x, y, c = lax.axis_index("x"), lax.axis_index("y"), lax.axis_index("c")
        me, sibling = (x, y, c), (x, y, 1 - c)
        chips = [(1 - x, y), (x, 1 - y), (1 - x, 1 - y)]

        def copy(a, k, block, to, src=None):
            dst = outs[a].at[_slot(*block)]
            return pltpu.make_async_remote_copy(
                src_ref=dst if src is None else src, dst_ref=dst,
                send_sem=send_sems.at[7 * a + k], recv_sem=recv_sems.at[7 * a + k],
                device_id=to, device_id_type=MESH)

        mine = [pltpu.make_async_copy(ins[a], outs[a].at[_slot(*me)], local_sems.at[a]) for a in range(n)]
        for cp in mine:
            cp.start()
        first = []
        for a in range(n):
            first.append(copy(a, 0, me, sibling, src=ins[a]))
            first += [copy(a, 1 + j, me, (*chip, c), src=ins[a]) for j, chip in enumerate(chips)]
        for cp in first:
            cp.start()
        passed = []
        for j, chip in enumerate(chips):
            for a in range(n):
                copy(a, 1 + j, (*chip, c), me).wait_recv()
                fwd = copy(a, 4 + j, (*chip, c), sibling)
                fwd.start()
                passed.append(fwd)
        for a in range(n):
            copy(a, 0, sibling, me).wait_recv()
            for j, chip in enumerate(chips):
                copy(a, 4 + j, (*chip, 1 - c), me).wait_recv()
        for cp in first + passed:
            cp.wait_send()
        for cp in mine:
            cp.wait()

    any_spec = pl.BlockSpec(memory_space=pl.ANY)
    return pl.pallas_call(
        body, name=name,
        out_shape=[jax.ShapeDtypeStruct((N_DEV,) + a.shape, a.dtype) for a in arrs],
        in_specs=[any_spec] * n, out_specs=[any_spec] * n,
        scratch_shapes=[pltpu.SemaphoreType.DMA((7 * n,)), pltpu.SemaphoreType.DMA((7 * n,)),
                        pltpu.SemaphoreType.DMA((n,))],
    )(*arrs)


def _all_gather_direct(arr, name):
    def body(in_ref, out_ref, send_sems, recv_sems, local_sem):
        me, peers = _peers()
        mine = pltpu.make_async_copy(in_ref, out_ref.at[_slot(*me)], local_sem)
        mine.start()

        def copy(k, origin, to):
            return pltpu.make_async_remote_copy(
                src_ref=in_ref, dst_ref=out_ref.at[_slot(*origin)], send_sem=send_sems.at[k],
                recv_sem=recv_sems.at[k], device_id=to, device_id_type=MESH)

        sends = [copy(k, me, peer) for k, peer in enumerate(peers)]
        for cp in sends:
            cp.start()
        for k, peer in enumerate(peers):
            copy(k, peer, me).wait_recv()
        for cp in sends:
            cp.wait_send()
        mine.wait()

    any_spec = pl.BlockSpec(memory_space=pl.ANY)
    return pl.pallas_call(
        body, name=name,
        out_shape=jax.ShapeDtypeStruct((N_DEV,) + arr.shape, arr.dtype),
        in_specs=[any_spec], out_specs=any_spec,
        scratch_shapes=[pltpu.SemaphoreType.DMA((7,)), pltpu.SemaphoreType.DMA((7,)), pltpu.SemaphoreType.DMA],
    )(arr)


_HBM_SPEC = pl.BlockSpec(memory_space=pltpu.HBM)
_SEM_SPEC = pl.BlockSpec(memory_space=pltpu.SEMAPHORE)
_DATAFLOW = pltpu.SideEffectType.DATAFLOW_SIDE_EFFECTING


def _peers():
    x, y, c = lax.axis_index("x"), lax.axis_index("y"), lax.axis_index("c")
    peers = []
    for k in range(1, N_DEV):
        fx, fy, fc = (k >> 2) & 1, (k >> 1) & 1, k & 1
        peers.append((1 - x if fx else x, 1 - y if fy else y, 1 - c if fc else c))
    return (x, y, c), peers


def _exchange_copy(send_ref, land_ref, row_off, src_slot, dst_slot, sems, idx, peer):
    rows = send_ref.shape[-2]
    src = send_ref.at[src_slot] if len(send_ref.shape) == 3 else send_ref
    return pltpu.make_async_remote_copy(
        src_ref=src, dst_ref=land_ref.at[dst_slot, pl.ds(row_off, rows), :],
        send_sem=sems[0].at[idx], recv_sem=sems[1].at[idx], device_id=peer, device_id_type=MESH)


def _own_copy(send_ref, land_ref, row_off, slot, sem):
    rows = send_ref.shape[-2]
    src = send_ref.at[slot] if len(send_ref.shape) == 3 else send_ref
    return pltpu.make_async_copy(src, land_ref.at[slot, pl.ds(row_off, rows), :], sem)


def _exchange_start(sends, lands, row_offs, name, carry, after=None):
    n = len(sends)

    def body(*refs):
        ins, zones, carry_ref = refs[:n], refs[n:2 * n], refs[2 * n]
        first_out = 2 * n + 1 + (after is not None)
        sems = refs[first_out:first_out + 3]
        carry_out = refs[-1]
        me, peers = _peers()
        for a in range(n):
            for k, peer in enumerate(peers):
                _exchange_copy(ins[a], zones[a], row_offs[a], _slot(*peer), _slot(*me), sems, 7 * a + k, peer).start()
            _own_copy(ins[a], zones[a], row_offs[a], _slot(*me), sems[2].at[a]).start()
        carry_out[...] = carry_ref[...]

    arrays = [_hbm(a) for a in list(sends) + list(lands)]
    vmem = pl.BlockSpec(memory_space=pltpu.VMEM)
    out = pl.pallas_call(
        body, name=name,
        out_shape=(pltpu.SemaphoreType.DMA((7 * n,)), pltpu.SemaphoreType.DMA((7 * n,)), pltpu.SemaphoreType.DMA((n,)),
                   *[pltpu.HBM(a.shape, a.dtype) for a in arrays], jax.ShapeDtypeStruct(carry.shape, carry.dtype)),
        in_specs=[_HBM_SPEC] * (2 * n) + [vmem] + ([pl.BlockSpec(memory_space=pl.ANY)] if after is not None else []),
        out_specs=(_SEM_SPEC, _SEM_SPEC, _SEM_SPEC, *[_HBM_SPEC] * (2 * n), vmem),
        input_output_aliases={i: 3 + i for i in range(2 * n)},
        compiler_params=pltpu.CompilerParams(has_side_effects=_DATAFLOW),
    )(*arrays, carry, *([after] if after is not None else []))
    return (out[0], out[1], out[2]), list(out[3:3 + n]), list(out[3 + n:3 + 2 * n]), out[-1]


def _exchange_wait(started, lands, after, name):
    n = len(lands)
    flat_sends = [s for _, sends, _ in started for s in sends]
    flat_sems = [s for sems, _, _ in started for s in sems]
    ns = len(flat_sends)

    def body(*refs):
        ins, zones = refs[:ns], refs[ns:ns + n]
        sem_refs = refs[ns + n:ns + n + len(flat_sems)]
        me, peers = _peers()
        pos = 0
        for call, (_, sends, row_offs) in enumerate(started):
            sems = sem_refs[3 * call:3 * call + 3]
            for a in range(len(sends)):
                for k, peer in enumerate(peers):
                    cp = _exchange_copy(ins[pos + a], zones[a], row_offs[a], _slot(*peer), _slot(*peer), sems,
                                        7 * a + k, peer)
                    cp.wait_send()
                    cp.wait_recv()
                _own_copy(ins[pos + a], zones[a], row_offs[a], _slot(*me), sems[2].at[a]).wait()
            pos += len(sends)

    arrays = list(flat_sends) + list(lands)
    out = pl.pallas_call(
        body, name=name,
        out_shape=tuple(pltpu.HBM(a.shape, a.dtype) for a in arrays),
        in_specs=[_HBM_SPEC] * len(arrays) + [_SEM_SPEC] * len(flat_sems) + [pl.BlockSpec(memory_space=pl.ANY)],
        out_specs=tuple([_HBM_SPEC] * len(arrays)),
        input_output_aliases={i: i for i in range(len(arrays))},
        compiler_params=pltpu.CompilerParams(has_side_effects=_DATAFLOW),
    )(*arrays, *flat_sems, after)
    return list(out[ns:])


def _bias_table(rel_bias, bucket):
    def body(rel_ref, bucket_ref, out_ref):
        bk = bucket_ref[...]
        for h in range(B_HEADS):
            def pick(b, acc, h=h):
                return jnp.where(bk == b, rel_ref[b, h], acc)
            out_ref[h] = lax.fori_loop(0, REL_BUCKETS, pick, jnp.zeros((BLOCK, BLOCK), F32))

    return pl.pallas_call(
        body, name="bias_table",
        out_shape=jax.ShapeDtypeStruct((B_HEADS, BLOCK, BLOCK), F32),
        in_specs=[pl.BlockSpec(memory_space=pltpu.SMEM), pl.BlockSpec(memory_space=pltpu.VMEM)],
        out_specs=pl.BlockSpec(memory_space=pltpu.VMEM),
    )(rel_bias, bucket)


def _inproj_gmlp_fwd(x2, pre_g3, w, ln_g3, ln_b3, wm, bsb, layer):
    tokens = x2.shape[0]
    tm = _token_tile(tokens)

    def body(x_ref, g_ref, w_ref, lg_ref, lb_ref, wm_ref, bsb_ref, h_ref, pa_ref, pb_ref, ya_ref, kept_ref, rstd_ref):
        x = x_ref[...]
        r = lax.rsqrt(jnp.mean(x * x, axis=-1, keepdims=True) + NORM_EPS)
        h = (x * r * g_ref[...]).astype(BF16)
        h_ref[...] = h
        pa_ref[...] = _mm_nt(h, w_ref[0:PA_WIDTH, :]).astype(BF16)
        pb_ref[...] = _mm_nt(h, w_ref[PA_WIDTH:IN_WIDTH, :]).astype(BF16)
        for ci in range(tm // CHUNK):
            rows = slice(ci * CHUNK, (ci + 1) * CHUNK)
            gu, gv, pz, u, vhat, rstd, _, mixed, sg = _gmlp_forward_chunk(
                pa_ref[rows, :], lg_ref[...], lb_ref[...], wm_ref, bsb_ref[...])
            ya_ref[rows, :] = (u * mixed * (pz * sg)).astype(BF16)
            for k, val in enumerate((u, gu, gv, vhat, mixed)):
                kept_ref[rows, A_WIDTH * k:A_WIDTH * (k + 1)] = val.astype(BF16)
            rstd_ref[rows, :] = jnp.broadcast_to(rstd, (CHUNK, LANES))

    blocks = [_nbytes((tm, D_MODEL), F32), _nbytes((D_MODEL, IN_WIDTH), BF16),
              _nbytes((tm, D_MODEL), BF16), _nbytes((tm, PA_WIDTH), BF16), _nbytes((tm, PB_WIDTH), BF16),
              _nbytes((A_GROUPS, CHUNK, CHUNK), BF16), _nbytes((CHUNK, A_WIDTH), F32), _nbytes((tm, A_WIDTH), BF16)]
    return pl.pallas_call(
        body, name=f"inproj_gmlp_fwd_{layer}", grid=(tokens // tm,),
        in_specs=[pl.BlockSpec((tm, D_MODEL), lambda i: (i, 0)),
                  pl.BlockSpec((None, 1, D_MODEL), lambda i: (layer, 0, 0)),
                  pl.BlockSpec((IN_WIDTH, D_MODEL), lambda i: (0, 0)),
                  pl.BlockSpec((None, 1, A_WIDTH), lambda i: (layer, 0, 0)),
                  pl.BlockSpec((None, 1, A_WIDTH), lambda i: (layer, 0, 0)),
                  pl.BlockSpec((None, A_GROUPS, CHUNK, CHUNK), lambda i: (layer, 0, 0, 0)),
                  pl.BlockSpec((None, CHUNK, A_WIDTH), lambda i: (layer, 0, 0))],
        out_specs=[pl.BlockSpec((tm, D_MODEL), lambda i: (i, 0)),
                   pl.BlockSpec((tm, PA_WIDTH), lambda i: (i, 0)),
                   pl.BlockSpec((tm, PB_WIDTH), lambda i: (i, 0)),
                   pl.BlockSpec((tm, A_WIDTH), lambda i: (i, 0)),
                   pl.BlockSpec((tm, KEPT_WIDTH), lambda i: (i, 0)),
                   pl.BlockSpec((tm, LANES), lambda i: (i, 0))],
        out_shape=[pltpu.HBM((tokens, D_MODEL), BF16),
                   pltpu.HBM((tokens, PA_WIDTH), BF16),
                   pltpu.HBM((tokens, PB_WIDTH), BF16),
                   pltpu.HBM((tokens, A_WIDTH), BF16),
                   pltpu.HBM((tokens, KEPT_WIDTH), BF16),
                   pltpu.HBM((tokens, LANES), F32)],
        compiler_params=pltpu.CompilerParams(dimension_semantics=("parallel",),
                                             vmem_limit_bytes=_vmem_limit(blocks + [_nbytes((tm, KEPT_WIDTH), BF16),
                                                                                    _nbytes((tm, LANES), F32)])),
    )(_hbm(x2), pre_g3, _hbm(w), ln_g3, ln_b3, wm, bsb)


def _gmlp_forward_chunk(pa, ln_g, ln_b, wm_ref, bsb):
    pu = pa[:, 0:A_WIDTH].astype(F32)
    pv = pa[:, A_WIDTH:2 * A_WIDTH].astype(F32)
    pz = pa[:, 2 * A_WIDTH:3 * A_WIDTH].astype(F32)
    u, gu = _gelu_and_grad(pu)
    vv, gv = _gelu_and_grad(pv)
    mu = jnp.mean(vv, axis=-1, keepdims=True)
    xc = vv - mu
    rstd = lax.rsqrt(jnp.mean(xc * xc, axis=-1, keepdims=True) + NORM_EPS)
    vhat = xc * rstd
    vnb = (vhat * ln_g + ln_b).astype(BF16)
    low = lax.broadcasted_iota(jnp.int32, (CHUNK, LANES), 1) < HEAD_DIM
    parts = []
    for p in range(A_GROUPS // 2):
        vp = vnb[:, LANES * p:LANES * (p + 1)]
        parts.append(jnp.where(low, _mm(wm_ref[2 * p], vp), _mm(wm_ref[2 * p + 1], vp)))
    mixed = jnp.concatenate(parts, axis=1) + bsb
    sg = _sigmoid(pz)
    return gu, gv, pz, u, vhat, rstd, vnb, mixed, sg


def _kv_variants(kv):
    t = kv.astype(F32)
    rolled = pltpu.roll(t, HEAD_DIM, 1)
    low = lax.broadcasted_iota(jnp.int32, t.shape, 1) < HEAD_DIM
    zero = jnp.zeros_like(t)
    head0 = (jnp.where(low, t, zero).astype(BF16), jnp.where(low, zero, rolled).astype(BF16))
    head1 = (jnp.where(low, rolled, zero).astype(BF16), jnp.where(low, zero, t).astype(BF16))
    return head0, head1


def _band_masks():
    row = lax.broadcasted_iota(jnp.int32, (BLOCK, BLOCK), 0)
    col = lax.broadcasted_iota(jnp.int32, (BLOCK, BLOCK), 1)
    return col <= row


def _wrap(full, tri):
    return jnp.where(tri, full[:, BLOCK:2 * BLOCK], full[:, 0:BLOCK])


def _attn_probs(sf, bias_h, sink, tri, kill):
    s = _wrap(sf, tri) + bias_h
    s = jnp.where(kill, NEG, s)
    m = jnp.maximum(jnp.max(s, axis=-1, keepdims=True), sink)
    e = jnp.exp(s - m)
    es = jnp.exp(sink - m)
    inv = 1.0 / (jnp.sum(e, axis=-1, keepdims=True) + es)
    return e * inv, es * inv


def _unwrap16(p, tri):
    p = p.astype(BF16)
    zero = jnp.zeros_like(p)
    return jnp.concatenate([jnp.where(tri, zero, p), jnp.where(tri, p, zero)], axis=1)


def _fill_kv(kv_ref, prev_ref, cur_ref):
    kv_ref[0:BLOCK, :] = prev_ref[...]
    kv_ref[BLOCK:, :] = cur_ref[:, K_OFF:K_OFF + 2 * KV_WIDTH]


def _attn_outproj_fwd(pb, bias, sinks, ya, wo, x2, post_g3, b_loc, layer, target=None):
    tokens = pb.shape[0]
    nb = tokens // b_loc // BLOCK
    tq = _token_tile(tokens)
    per_tile = tq // BLOCK
    steps = tokens // tq
    with_loss = target is not None
    half = D_MODEL // 2

    def body(*refs):
        sink_ref, cur_ref, prev_ref, bias_ref, ya_ref, woa_ref, wob_ref, x_ref, g_ref = refs[:9]
        if with_loss:
            t_ref, yb_ref, ob_ref, pw_ref, spw_ref, y_ref, grad_ref, loss_ref, kv_ref, acc_ref = refs[9:]
        else:
            yb_ref, ob_ref, pw_ref, spw_ref, y_ref, xn_ref, kv_ref = refs[9:]
        t = pl.program_id(0)
        _fill_kv(kv_ref, prev_ref, cur_ref)
        tri = _band_masks()
        head_lane = lax.broadcasted_iota(jnp.int32, (BLOCK, LANES), 1)

        def block(i, carry):
            start = i * BLOCK
            rows = pl.ds(start, BLOCK)
            first = lax.rem(t * per_tile + i, nb) == 0
            kill = jnp.logical_and(first, jnp.logical_not(tri))
            kv = kv_ref[pl.ds(start, 2 * BLOCK), :]
            k_ops = _kv_variants(kv[:, 0:KV_WIDTH])
            v_ops = _kv_variants(kv[:, KV_WIDTH:2 * KV_WIDTH])
            p16 = {}
            sink_cols = jnp.zeros((BLOCK, LANES), F32)
            for kvh in range(2):
                pairs = (2 * kvh, 2 * kvh + 1)
                qs = jnp.concatenate([cur_ref[rows, LANES * p:LANES * (p + 1)] for p in pairs], axis=0) * SCALE
                for j in range(2):
                    sf = _mm_nt(qs, k_ops[kvh][j])
                    for r, p in enumerate(pairs):
                        hd = 2 * p + j
                        probs, sink_p = _attn_probs(sf[BLOCK * r:BLOCK * (r + 1)], bias_ref[hd],
                                                    sink_ref[layer, hd], tri, kill)
                        probs = probs.astype(BF16)
                        p16[hd] = _unwrap16(probs, tri)
                        pw_ref[rows, LANES * hd:LANES * (hd + 1)] = probs
                        sink_cols = jnp.where(head_lane == hd, sink_p, sink_cols)
            spw_ref[rows, :] = sink_cols
            for kvh in range(2):
                pairs = (2 * kvh, 2 * kvh + 1)
                out = jnp.zeros((2 * BLOCK, LANES), F32)
                for j in range(2):
                    out = out + _mm(jnp.concatenate([p16[2 * p + j] for p in pairs], axis=0), v_ops[kvh][j])
                for r, p in enumerate(pairs):
                    bz = cur_ref[rows, BZ_OFF + LANES * p:BZ_OFF + LANES * (p + 1)].astype(F32)
                    out_p = out[BLOCK * r:BLOCK * (r + 1)]
                    ob_ref[rows, LANES * p:LANES * (p + 1)] = out_p.astype(BF16)
                    yb_ref[rows, LANES * p:LANES * (p + 1)] = (out_p * (bz * _sigmoid(bz))).astype(BF16)
            y = _mm(ya_ref[rows, :], woa_ref[...]) + _mm(yb_ref[rows, :], wob_ref[...])
            r = lax.rsqrt(jnp.mean(y * y, axis=-1, keepdims=True) + NORM_EPS)
            x_next = x_ref[rows, :] + y * r * g_ref[...]
            y_ref[rows, :] = y.astype(BF16)
            if not with_loss:
                xn_ref[rows, :] = x_next
            else:
                err = x_next - t_ref[rows, :]
                grad_ref[rows, :] = err * (1.0 / D_MODEL)
                acc_ref[...] += jnp.sum(err * err, axis=0, keepdims=True)
            return carry

        if with_loss:
            @pl.when(t == 0)
            def _():
                acc_ref[...] = jnp.zeros_like(acc_ref)

        for i in range(per_tile):
            block(i, 0)

        if not with_loss:
            return

        @pl.when(t == steps - 1)
        def _():
            total = jnp.sum(acc_ref[...], axis=-1, keepdims=True) * (0.5 / D_MODEL)
            loss_ref[...] = jnp.broadcast_to(total, loss_ref.shape)

    tile = pl.BlockSpec((tq, D_MODEL), lambda t: (t, 0))
    blocks = [_nbytes((tq, PB_WIDTH), BF16), _nbytes((BLOCK, 2 * KV_WIDTH), BF16),
              _nbytes((B_HEADS, BLOCK, BLOCK), F32), 2 * _nbytes((tq, B_WIDTH), BF16),
              2 * _nbytes((half, D_MODEL), BF16), 3 * _nbytes((tq, D_MODEL), F32), _nbytes((tq, D_MODEL), BF16),
              _nbytes((tq, B_HEADS * BLOCK), BF16), _nbytes((tq, LANES), F32)]
    scratch = _nbytes((tq + BLOCK, 2 * KV_WIDTH), BF16)
    in_specs = [pl.BlockSpec(memory_space=pltpu.SMEM),
                pl.BlockSpec((tq, PB_WIDTH), lambda t: (t, 0)),
                pl.BlockSpec((BLOCK, 2 * KV_WIDTH),
                             lambda t: (jnp.maximum(t * per_tile - 1, 0), K_OFF // (2 * KV_WIDTH))),
                pl.BlockSpec((B_HEADS, BLOCK, BLOCK), lambda t: (0, 0, 0)),
                pl.BlockSpec((tq, half), lambda t: (t, 0)),
                pl.BlockSpec((half, D_MODEL), lambda t: (0, 0)),
                pl.BlockSpec((half, D_MODEL), lambda t: (1, 0)),
                tile,
                pl.BlockSpec((None, 1, D_MODEL), lambda t: (layer, 0, 0))]
    out_specs = [pl.BlockSpec((tq, B_WIDTH), lambda t: (t, 0)), pl.BlockSpec((tq, B_WIDTH), lambda t: (t, 0)),
                 pl.BlockSpec((tq, B_HEADS * BLOCK), lambda t: (t, 0)),
                 pl.BlockSpec((tq, LANES), lambda t: (t, 0)), tile, tile]
    out_shape = [pltpu.HBM((tokens, B_WIDTH), BF16), pltpu.HBM((tokens, B_WIDTH), BF16),
                 pltpu.HBM((tokens, B_HEADS * BLOCK), BF16),
                 pltpu.HBM((tokens, LANES), F32), pltpu.HBM((tokens, D_MODEL), BF16),
                 pltpu.HBM((tokens, D_MODEL), F32)]
    scratch_shapes = [pltpu.VMEM((tq + BLOCK, 2 * KV_WIDTH), BF16)]
    operands = [sinks, _hbm(pb), _hbm(pb), _hbm(bias), _hbm(ya), _hbm(wo), _hbm(wo), _hbm(x2), post_g3]
    if with_loss:
        in_specs.append(tile)
        out_specs.append(pl.BlockSpec((1, LANES), lambda t: (0, 0)))
        out_shape.append(pltpu.HBM((1, LANES), F32))
        scratch_shapes.append(pltpu.VMEM((1, D_MODEL), F32))
        operands.append(_hbm(target))
    return pl.pallas_call(
        body, name=f"attn_outproj_fwd_{layer}", grid=(steps,),
        in_specs=in_specs, out_specs=out_specs, out_shape=out_shape, scratch_shapes=scratch_shapes,
        compiler_params=pltpu.CompilerParams(dimension_semantics=("arbitrary" if with_loss else "parallel",),
                                             vmem_limit_bytes=_vmem_limit(blocks, scratch)),
    )(*operands)


def _outproj_bwd(g, y, ya, yb, wo, post_g3, layer):
    tokens = g.shape[0]
    tm = _token_tile(tokens, cap=1024)
    half = D_MODEL // 2
    steps = tokens // tm

    def body(g_ref, y_ref, ya_ref, yb_ref, woa_ref, wob_ref, pg_ref, dya_ref, dyb_ref, dwo16_ref, dpg_ref, dwo_ref):
        @pl.when(pl.program_id(0) == 0)
        def _():
            dwo_ref[...] = jnp.zeros_like(dwo_ref)
            dpg_ref[...] = jnp.zeros_like(dpg_ref)

        gv = g_ref[...]
        yf = y_ref[...].astype(F32)
        r = lax.rsqrt(jnp.mean(yf * yf, axis=-1, keepdims=True) + NORM_EPS)
        yhat = yf * r
        gg = gv * pg_ref[...]
        dy = r * (gg - yhat * jnp.mean(gg * yhat, axis=-1, keepdims=True))
        dpg_ref[...] += jnp.sum(gv * yhat, axis=0, keepdims=True)
        dyb16 = dy.astype(BF16)
        dya_ref[...] = _mm_nt(dyb16, woa_ref[...]).astype(BF16)
        dyb_ref[...] = _mm_nt(dyb16, wob_ref[...]).astype(BF16)
        dwo_ref[0:half, :] += _mm_tn(ya_ref[...], dyb16)
        dwo_ref[half:D_MODEL, :] += _mm_tn(yb_ref[...], dyb16)

        @pl.when(pl.program_id(0) == steps - 1)
        def _():
            dwo16_ref[...] = dwo_ref[...].astype(BF16)

    blocks = [_nbytes((tm, D_MODEL), F32), _nbytes((tm, D_MODEL), BF16), 4 * _nbytes((tm, half), BF16),
              2 * _nbytes((half, D_MODEL), BF16), _nbytes((D_MODEL, D_MODEL), BF16)]
    return pl.pallas_call(
        body, name=f"outproj_bwd_{layer}", grid=(tokens // tm,),
        in_specs=[pl.BlockSpec((tm, D_MODEL), lambda i: (i, 0)),
                  pl.BlockSpec((tm, D_MODEL), lambda i: (i, 0)),
                  pl.BlockSpec((tm, half), lambda i: (i, 0)),
                  pl.BlockSpec((tm, half), lambda i: (i, 0)),
                  pl.BlockSpec((half, D_MODEL), lambda i: (0, 0)),
                  pl.BlockSpec((half, D_MODEL), lambda i: (1, 0)),
                  pl.BlockSpec((None, 1, D_MODEL), lambda i: (layer, 0, 0))],
        out_specs=[pl.BlockSpec((tm, half), lambda i: (i, 0)),
                   pl.BlockSpec((tm, half), lambda i: (i, 0)),
                   pl.BlockSpec((D_MODEL, D_MODEL), lambda i: (0, 0)),
                   pl.BlockSpec((1, D_MODEL), lambda i: (0, 0))],
        out_shape=[pltpu.HBM((tokens,half), BF16),
                   pltpu.HBM((tokens,half), BF16),
                   pltpu.HBM((D_MODEL,D_MODEL), BF16),
                   pltpu.HBM((1,D_MODEL), F32)],
        scratch_shapes=[pltpu.VMEM((D_MODEL, D_MODEL), F32)],
        compiler_params=pltpu.CompilerParams(dimension_semantics=("arbitrary",),
                                             vmem_limit_bytes=_vmem_limit(blocks, _nbytes((D_MODEL, D_MODEL), F32))),
    )(_hbm(g), _hbm(y), _hbm(ya), _hbm(yb), _hbm(wo), _hbm(wo), post_g3)


def _gmlp_bwd_dw(pa, kept, rstd_w, dya, h, dqz, dkv, ln_g3, ln_b3, wmt, layer):
    tokens = pa.shape[0]
    tc = _token_tile(tokens)
    steps = tokens // tc

    def body(pz_ref, kept_ref, rstd_ref, dya_ref, h_ref, dqz_ref, dkv_ref, lg_ref, lb_ref, wmt_ref,
             da_ref, dw_ref, dws_ref, dbs_ref, dlg_ref, dlb_ref, acc_ref, dbsb_ref):
        i = pl.program_id(0)

        @pl.when(i == 0)
        def _():
            acc_ref[...] = jnp.zeros_like(acc_ref)
            dws_ref[...] = jnp.zeros_like(dws_ref)
            dlg_ref[...] = jnp.zeros_like(dlg_ref)
            dlb_ref[...] = jnp.zeros_like(dlb_ref)
            dbsb_ref[...] = jnp.zeros_like(dbsb_ref)

        ln_g = lg_ref[...]
        low = lax.broadcasted_iota(jnp.int32, (CHUNK, LANES), 1) < HEAD_DIM
        hv = h_ref[...]
        acc_ref[:, PA_WIDTH:PA_WIDTH + B_WIDTH] += _mm_tn(hv, dqz_ref[:, 0:B_WIDTH])
        acc_ref[:, PA_WIDTH + K_OFF:PA_WIDTH + BZ_OFF] += _mm_tn(hv, dkv_ref[...])
        acc_ref[:, PA_WIDTH + BZ_OFF:IN_WIDTH] += _mm_tn(hv, dqz_ref[:, B_WIDTH:2 * B_WIDTH])

        for ci in range(tc // CHUNK):
            rows = slice(ci * CHUNK, (ci + 1) * CHUNK)
            u, gu, gv, vhat, mixed = (kept_ref[rows, A_WIDTH * k:A_WIDTH * (k + 1)].astype(F32) for k in range(5))
            rstd = jnp.concatenate([rstd_ref[rows, :]] * (A_WIDTH // LANES), axis=1)
            vnb = (vhat * ln_g + lb_ref[...]).astype(BF16)
            pz = pz_ref[rows, :].astype(F32)
            sg = _sigmoid(pz)
            dy = dya_ref[rows, :].astype(F32)
            sz = pz * sg
            dy_sz = dy * sz
            du = dy_sz * mixed
            dmixed = dy_sz * u
            dz = dy * (u * mixed) * (sg + sz * (1.0 - sg))
            dbsb_ref[...] += dmixed
            dmb = dmixed.astype(BF16)
            zero = jnp.zeros((CHUNK, LANES), BF16)
            parts = []
            for p in range(A_GROUPS // 2):
                dmp = dmb[:, LANES * p:LANES * (p + 1)]
                vp = vnb[:, LANES * p:LANES * (p + 1)]
                parts.append(jnp.where(low, _mm(wmt_ref[2 * p], dmp), _mm(wmt_ref[2 * p + 1], dmp)))
                dws_ref[2 * p] += _mm_nt(jnp.where(low, dmp, zero), vp)
                dws_ref[2 * p + 1] += _mm_nt(jnp.where(low, zero, dmp), vp)
            dvn = jnp.concatenate(parts, axis=1)
            dlg_ref[...] += jnp.sum(dvn * vhat, axis=0, keepdims=True)
            dlb_ref[...] += jnp.sum(dvn, axis=0, keepdims=True)
            dvh = dvn * ln_g
            dvv = rstd * (dvh - jnp.mean(dvh, axis=-1, keepdims=True)
                          - vhat * jnp.mean(dvh * vhat, axis=-1, keepdims=True))
            da_ref[rows, 0:A_WIDTH] = (du * gu).astype(BF16)
            da_ref[rows, A_WIDTH:2 * A_WIDTH] = (dvv * gv).astype(BF16)
            da_ref[rows, 2 * A_WIDTH:3 * A_WIDTH] = dz.astype(BF16)

        acc_ref[:, 0:PA_WIDTH] += _mm_tn(hv, da_ref[...])

        @pl.when(i == steps - 1)
        def _():
            for c in range(IN_WIDTH // LANES):
                cols = slice(LANES * c, LANES * (c + 1))
                dw_ref[cols, :] = acc_ref[:, cols].T.astype(BF16)
            causal = (lax.broadcasted_iota(jnp.int32, (CHUNK, CHUNK), 0)
                      >= lax.broadcasted_iota(jnp.int32, (CHUNK, CHUNK), 1))
            for h in range(A_GROUPS):
                dws_ref[h] = jnp.where(causal, dws_ref[h], 0.0)
            acc = dbsb_ref[...]
            lane_full = lax.broadcasted_iota(jnp.int32, (CHUNK, A_WIDTH), 1)
            lane_out = lax.broadcasted_iota(jnp.int32, (CHUNK, LANES), 1)
            out = jnp.zeros((CHUNK, LANES), F32)
            for h in range(A_GROUPS):
                in_group = jnp.logical_and(lane_full >= HEAD_DIM * h, lane_full < HEAD_DIM * (h + 1))
                s = jnp.sum(jnp.where(in_group, acc, 0.0), axis=-1, keepdims=True)
                out = jnp.where(lane_out == h, s, out)
            dbs_ref[...] = out

    blocks = [_nbytes((tc, KEPT_WIDTH), BF16), 2 * _nbytes((tc, A_WIDTH), BF16), _nbytes((A_GROUPS, CHUNK, CHUNK), BF16),
              _nbytes((tc, LANES), F32), _nbytes((tc, PA_WIDTH), BF16), _nbytes((A_GROUPS, CHUNK, CHUNK), F32),
              _nbytes((CHUNK, LANES), F32), _nbytes((tc, D_MODEL), BF16), _nbytes((tc, 2 * B_WIDTH), BF16),
              _nbytes((tc, 2 * KV_WIDTH), BF16), _nbytes((D_MODEL, IN_WIDTH), BF16)]
    scratch = _nbytes((D_MODEL, IN_WIDTH), F32) + _nbytes((CHUNK, A_WIDTH), F32)
    in_specs = [pl.BlockSpec((tc, A_WIDTH), lambda i: (i, 2)),
                pl.BlockSpec((tc, KEPT_WIDTH), lambda i: (i, 0)),
                pl.BlockSpec((tc, LANES), lambda i: (i, 0)),
                pl.BlockSpec((tc, A_WIDTH), lambda i: (i, 0)),
                pl.BlockSpec((tc, D_MODEL), lambda i: (i, 0)),
                pl.BlockSpec((tc, 2 * B_WIDTH), lambda i: (i, 0)),
                pl.BlockSpec((pl.Element(tc), pl.Element(2 * KV_WIDTH)),
                             lambda i: (pl.multiple_of(i * tc + BLOCK, BLOCK), 0)),
                pl.BlockSpec((None, 1, A_WIDTH), lambda i: (layer, 0, 0)),
                pl.BlockSpec((None, 1, A_WIDTH), lambda i: (layer, 0, 0)),
                pl.BlockSpec((None, A_GROUPS, CHUNK, CHUNK), lambda i: (layer, 0, 0, 0))]
    operands = [_hbm(pa), _hbm(kept), _hbm(rstd_w), _hbm(dya), _hbm(h), _hbm(dqz), _hbm(dkv), ln_g3, ln_b3, wmt]
    out_specs = [pl.BlockSpec((tc, PA_WIDTH), lambda i: (i, 0)),
                 pl.BlockSpec((IN_WIDTH, D_MODEL), lambda i: (0, 0)),
                 pl.BlockSpec((A_GROUPS, CHUNK, CHUNK), lambda i: (0, 0, 0)),
                 pl.BlockSpec((CHUNK, LANES), lambda i: (0, 0)),
                 pl.BlockSpec((1, A_WIDTH), lambda i: (0, 0)),
                 pl.BlockSpec((1, A_WIDTH), lambda i: (0, 0))]
    out_shape = [pltpu.HBM((tokens, PA_WIDTH), BF16),
                 pltpu.HBM((IN_WIDTH, D_MODEL), BF16),
                 pltpu.HBM((A_GROUPS, CHUNK, CHUNK), F32),
                 pltpu.HBM((CHUNK, LANES), F32),
                 pltpu.HBM((1, A_WIDTH), F32),
                 pltpu.HBM((1, A_WIDTH), F32)]
    scratch_shapes = [pltpu.VMEM((D_MODEL, IN_WIDTH), F32), pltpu.VMEM((CHUNK, A_WIDTH), F32)]
    return pl.pallas_call(
        body, name=f"gmlp_bwd_dw_{layer}", grid=(steps,),
        in_specs=in_specs, out_specs=out_specs, out_shape=out_shape, scratch_shapes=scratch_shapes,
        compiler_params=pltpu.CompilerParams(dimension_semantics=("arbitrary",),
                                             vmem_limit_bytes=_vmem_limit(blocks, scratch)),
    )(*operands)


def _attn_bwd(pb, dyb, out_w, probs_w, sink_w, dbias_in, layer):
    tokens = pb.shape[0]
    qz_width = 2 * B_WIDTH
    tq = _token_tile(tokens)
    per_tile = tq // BLOCK
    nt = tokens // tq

    def body(cur_ref, prev_ref, dyb_ref, ob_ref, pw_ref, spw_ref, dbias_in_ref, dqz_ref, dkv_ref, dbias_ref, dsink_ref,
             kv_ref, acc_ref):
        t = pl.program_id(0)

        @pl.when(t == 0)
        def _():
            dbias_ref[...] = dbias_in_ref[...]
            dsink_ref[...] = jnp.zeros_like(dsink_ref)
            acc_ref[0:BLOCK, :] = jnp.zeros((BLOCK, 2 * KV_WIDTH), F32)

        @pl.when(t < nt)
        def _():
            acc_ref[BLOCK:, :] = jnp.zeros((tq, 2 * KV_WIDTH), F32)
            _fill_kv(kv_ref, prev_ref, cur_ref)
            tri = _band_masks()
            low = lax.broadcasted_iota(jnp.int32, (BLOCK, LANES), 1) < HEAD_DIM
            low_kv = lax.broadcasted_iota(jnp.int32, (2 * BLOCK, LANES), 1) < HEAD_DIM

            head_lane = lax.broadcasted_iota(jnp.int32, (BLOCK, LANES), 1)

            def block(i, carry):
                start = i * BLOCK
                rows = pl.ds(start, BLOCK)
                kv = kv_ref[pl.ds(start, 2 * BLOCK), :]
                k_ops = _kv_variants(kv[:, 0:KV_WIDTH])
                v_ops = _kv_variants(kv[:, KV_WIDTH:2 * KV_WIDTH])
                probs, p16, qs, dos, delta, ds16 = {}, {}, {}, {}, {}, {}
                for hd in range(B_HEADS):
                    kept_p = pw_ref[rows, LANES * hd:LANES * (hd + 1)]
                    probs[hd] = kept_p.astype(F32)
                    p16[hd] = _unwrap16(kept_p, tri)
                for kvh in range(2):
                    qs[kvh] = jnp.concatenate(
                        [cur_ref[rows, LANES * p:LANES * (p + 1)] for p in (2 * kvh, 2 * kvh + 1)], axis=0)
                for kvh in range(2):
                    pairs = (2 * kvh, 2 * kvh + 1)
                    d_outs = []
                    for r, p in enumerate(pairs):
                        bz = cur_ref[rows, BZ_OFF + LANES * p:BZ_OFF + LANES * (p + 1)].astype(F32)
                        sg = _sigmoid(bz)
                        dyp = dyb_ref[rows, LANES * p:LANES * (p + 1)].astype(F32)
                        out_p = ob_ref[rows, LANES * p:LANES * (p + 1)].astype(F32)
                        d_out = dyp * (bz * sg)
                        dqz_ref[rows, B_WIDTH + LANES * p:B_WIDTH + LANES * (p + 1)] = (
                            dyp * out_p * (sg * (1.0 + bz * (1.0 - sg)))).astype(BF16)
                        dod = d_out * out_p
                        delta[2 * p] = jnp.sum(jnp.where(low, dod, 0.0), axis=-1, keepdims=True)
                        delta[2 * p + 1] = jnp.sum(jnp.where(low, 0.0, dod), axis=-1, keepdims=True)
                        d_outs.append(d_out.astype(BF16))
                    dos[kvh] = jnp.concatenate(d_outs, axis=0)
                delta_cols = jnp.zeros((BLOCK, LANES), F32)
                for hd in range(B_HEADS):
                    delta_cols = jnp.where(head_lane == hd, delta[hd], delta_cols)
                dsink_ref[0:1, :] += jnp.sum(-(spw_ref[rows, :] * delta_cols), axis=0, keepdims=True)
                for kvh in range(2):
                    pairs = (2 * kvh, 2 * kvh + 1)
                    for j in range(2):
                        dpf = _mm_nt(dos[kvh], v_ops[kvh][j])
                        for r, p in enumerate(pairs):
                            hd = 2 * p + j
                            ds = probs[hd] * (_wrap(dpf[BLOCK * r:BLOCK * (r + 1)], tri) - delta[hd])
                            dbias_ref[hd] += ds
                            ds16[hd] = _unwrap16(ds, tri)
                dk_acc = [[None, None], [None, None]]
                dv_acc = [[None, None], [None, None]]
                for kvh in range(2):
                    pairs = (2 * kvh, 2 * kvh + 1)
                    dq = jnp.zeros((2 * BLOCK, LANES), F32)
                    for j in range(2):
                        dss = jnp.concatenate([ds16[2 * p + j] for p in pairs], axis=0)
                        pss = jnp.concatenate([p16[2 * p + j] for p in pairs], axis=0)
                        dq = dq + _mm(dss, k_ops[kvh][j])
                        dk_acc[kvh][j] = _mm_tn(dss, qs[kvh])
                        dv_acc[kvh][j] = _mm_tn(pss, dos[kvh])
                    for r, p in enumerate(pairs):
                        dqz_ref[rows, LANES * p:LANES * (p + 1)] = (dq[BLOCK * r:BLOCK * (r + 1)] * SCALE).astype(BF16)

                def fold(acc):
                    return jnp.where(low_kv,
                                     acc[0][0] + pltpu.roll(acc[0][1], HEAD_DIM, 1),
                                     pltpu.roll(acc[1][0], HEAD_DIM, 1) + acc[1][1])

                acc_ref[pl.ds(start, 2 * BLOCK), :] += jnp.concatenate(
                    [fold(dk_acc) * SCALE, fold(dv_acc)], axis=1)
                return carry

            for i in range(per_tile):
                block(i, 0)
            dkv_ref[...] = acc_ref[0:tq, :].astype(BF16)
            acc_ref[0:BLOCK, :] = acc_ref[tq:tq + BLOCK, :]

        @pl.when(t == nt)
        def _():
            dkv_ref[0:BLOCK, :] = acc_ref[0:BLOCK, :].astype(BF16)
            dkv_ref[BLOCK:, :] = jnp.zeros((tq - BLOCK, 2 * KV_WIDTH), BF16)

    def cur_map(t):
        return (jnp.minimum(t, nt - 1), 0)

    def prev_map(t):
        return (jnp.maximum(jnp.minimum(t, nt - 1) * per_tile - 1, 0), K_OFF // (2 * KV_WIDTH))

    blocks = [_nbytes((tq, PB_WIDTH), BF16), _nbytes((BLOCK, 2 * KV_WIDTH), BF16), _nbytes((tq, B_WIDTH), BF16),
              2 * _nbytes((B_HEADS, BLOCK, BLOCK), F32), _nbytes((tq, qz_width), BF16),
              _nbytes((tq, 2 * KV_WIDTH), BF16), _nbytes((B_HEADS, LANES), F32),
              _nbytes((tq, B_HEADS * BLOCK), BF16), _nbytes((tq, LANES), F32)]
    scratch = _nbytes((tq + BLOCK, 2 * KV_WIDTH), BF16) + _nbytes((tq + BLOCK, 2 * KV_WIDTH), F32)
    return pl.pallas_call(
        body, name=f"attn_bwd_{layer}", grid=(nt + 1,),
        in_specs=[pl.BlockSpec((tq, PB_WIDTH), cur_map),
                  pl.BlockSpec((BLOCK, 2 * KV_WIDTH), prev_map),
                  pl.BlockSpec((tq, B_WIDTH), cur_map),
                  pl.BlockSpec((tq, B_WIDTH), cur_map),
                  pl.BlockSpec((tq, B_HEADS * BLOCK), cur_map),
                  pl.BlockSpec((tq, LANES), cur_map),
                  pl.BlockSpec((B_HEADS, BLOCK, BLOCK), lambda t: (0, 0, 0))],
        out_specs=[pl.BlockSpec((tq, qz_width), cur_map),
                   pl.BlockSpec((tq, 2 * KV_WIDTH), lambda t: (t, 0)),
                   pl.BlockSpec((B_HEADS, BLOCK, BLOCK), lambda t: (0, 0, 0)),
                   pl.BlockSpec((B_HEADS, LANES), lambda t: (0, 0))],
        out_shape=[pltpu.HBM((tokens, qz_width), BF16),
                   pltpu.HBM((tokens + tq, 2 * KV_WIDTH), BF16),
                   pltpu.HBM((B_HEADS, BLOCK, BLOCK), F32),
                   pltpu.HBM((B_HEADS, LANES), F32)],
        scratch_shapes=[pltpu.VMEM((tq + BLOCK, 2 * KV_WIDTH), BF16),
                        pltpu.VMEM((tq + BLOCK, 2 * KV_WIDTH), F32)],
        compiler_params=pltpu.CompilerParams(dimension_semantics=("arbitrary",),
                                             vmem_limit_bytes=_vmem_limit(blocks, scratch)),
    )(_hbm(pb), _hbm(pb), _hbm(dyb), _hbm(out_w), _hbm(probs_w), _hbm(sink_w), _hbm(dbias_in))


def _inproj_bwd_dx(da, dqz, dkv, w, x2, g, pre_g3, layer):
    tokens = x2.shape[0]
    tm = _token_tile(tokens)

    def body(da_ref, dqz_ref, dkv_ref, w_ref, x_ref, g_ref, pg_ref, gn_ref, dpg_ref):
        @pl.when(pl.program_id(0) == 0)
        def _():
            dpg_ref[...] = jnp.zeros_like(dpg_ref)

        dh = _mm(da_ref[...], w_ref[0:PA_WIDTH, :])
        dh += _mm(dqz_ref[:, 0:B_WIDTH], w_ref[PA_WIDTH:PA_WIDTH + B_WIDTH, :])
        dh += _mm(dkv_ref[...], w_ref[PA_WIDTH + K_OFF:PA_WIDTH + BZ_OFF, :])
        dh += _mm(dqz_ref[:, B_WIDTH:2 * B_WIDTH], w_ref[PA_WIDTH + BZ_OFF:IN_WIDTH, :])
        x = x_ref[...]
        r = lax.rsqrt(jnp.mean(x * x, axis=-1, keepdims=True) + NORM_EPS)
        xhat = x * r
        dhg = dh * pg_ref[...]
        dpg_ref[...] += jnp.sum(dh * xhat, axis=0, keepdims=True)
        gn_ref[...] = g_ref[...] + r * (dhg - xhat * jnp.mean(dhg * xhat, axis=-1, keepdims=True))

    blocks = [_nbytes((tm, PA_WIDTH), BF16), _nbytes((tm, 2 * B_WIDTH), BF16), _nbytes((tm, 2 * KV_WIDTH), BF16),
              _nbytes((D_MODEL, IN_WIDTH), BF16), 3 * _nbytes((tm, D_MODEL), F32)]
    return pl.pallas_call(
        body, name=f"inproj_bwd_dx_{layer}", grid=(tokens // tm,),
        in_specs=[pl.BlockSpec((tm, PA_WIDTH), lambda i: (i, 0)),
                  pl.BlockSpec((tm, 2 * B_WIDTH), lambda i: (i, 0)),
                  pl.BlockSpec((pl.Element(tm), pl.Element(2 * KV_WIDTH)), lambda i: (pl.multiple_of(i * tm + BLOCK, BLOCK), 0)),
                  pl.BlockSpec((IN_WIDTH, D_MODEL), lambda i: (0, 0)),
                  pl.BlockSpec((tm, D_MODEL), lambda i: (i, 0)),
                  pl.BlockSpec((tm, D_MODEL), lambda i: (i, 0)),
                  pl.BlockSpec((None, 1, D_MODEL), lambda i: (layer, 0, 0))],
        out_specs=[pl.BlockSpec((tm, D_MODEL), lambda i: (i, 0)),
                   pl.BlockSpec((1, D_MODEL), lambda i: (0, 0))],
        out_shape=[pltpu.HBM((tokens,D_MODEL), F32),
                   pltpu.HBM((1,D_MODEL), F32)],
        compiler_params=pltpu.CompilerParams(dimension_semantics=("arbitrary",),
                                             vmem_limit_bytes=_vmem_limit(blocks)),
    )(_hbm(da), _hbm(dqz), _hbm(dkv), _hbm(w), _hbm(x2), _hbm(g), pre_g3)


def _rel_bias_grad(dbias, col_bucket, flip):
    def body(db_ref, cb_ref, flip_ref, out_ref):
        cb = cb_ref[...]
        anti = flip_ref[...]
        sums = []
        for h in range(B_HEADS):
            x = db_ref[h]
            hi = x.astype(BF16)
            rest = x - hi.astype(F32)
            mid = rest.astype(BF16)
            low = (rest - mid.astype(F32)).astype(BF16)
            reversed_x = _mm(hi, anti) + _mm(mid, anti) + _mm(low, anti)
            rolled = pltpu.roll(reversed_x, 0, 1, stride=1, stride_axis=0)
            sums.append(jnp.sum(rolled, axis=0, keepdims=True))
        per_dist = jnp.concatenate(sums, axis=0)
        lane = lax.broadcasted_iota(jnp.int32, (B_HEADS, LANES), 1)
        out = jnp.zeros((B_HEADS, LANES), F32)
        for b in range(REL_BUCKETS):
            s = jnp.sum(jnp.where(cb == b, per_dist, 0.0), axis=-1, keepdims=True)
            out = jnp.where(lane == b, s, out)
        out_ref[...] = out

    vm = pl.BlockSpec(memory_space=pltpu.VMEM)
    return pl.pallas_call(
        body, name="rel_bias_grad",
        out_shape=jax.ShapeDtypeStruct((B_HEADS, LANES), F32),
        in_specs=[vm, vm, vm], out_specs=vm,
    )(dbias, col_bucket, flip)


def _reduce_adamw(slots, w, m, v, name):
    rows, cols = w.shape
    tr = _row_tile(rows)
    c1 = 1.0 / (1.0 - ADAM_B1 ** ADAM_STEP)
    c2 = 1.0 / (1.0 - ADAM_B2 ** ADAM_STEP)

    def body(s_ref, w_ref, m_ref, v_ref, g_ref, d_ref, nm_ref, nv_ref):
        g = s_ref[0].astype(F32)
        for i in range(1, N_DEV):
            g = g + s_ref[i].astype(F32)
        nm = ADAM_B1 * m_ref[...] + (1.0 - ADAM_B1) * g
        nv = ADAM_B2 * v_ref[...] + (1.0 - ADAM_B2) * (g * g)
        g_ref[...] = g
        nm_ref[...] = nm
        nv_ref[...] = nv
        d_ref[...] = -ADAM_LR * ((nm * c1) / (jnp.sqrt(nv * c2) + ADAM_EPS) + ADAM_WD * w_ref[...])

    blocks = [_nbytes((N_DEV, tr, cols), slots.dtype), 7 * _nbytes((tr, cols), F32)]
    tile = pl.BlockSpec((tr, cols), lambda i: (i, 0))
    return pl.pallas_call(
        body, name=name, grid=(rows // tr,),
        in_specs=[pl.BlockSpec((N_DEV, tr, cols), lambda i: (0, i, 0)), tile, tile, tile],
        out_specs=[tile] * 4,
        out_shape=[pltpu.HBM((rows, cols), F32)] * 4,
        compiler_params=pltpu.CompilerParams(dimension_semantics=("parallel",),
                                             vmem_limit_bytes=_vmem_limit(blocks)),
    )(_hbm(slots), _hbm(w), _hbm(m), _hbm(v))


_SMALL = ("pre_norm_g", "post_norm_g", "ln_v_g", "ln_v_b", "b_spatial", "sinks", "rel_bias", "loss")


def _pack_small(parts):
    slabs = []
    for name in _SMALL:
        flat = parts[name].astype(F32).reshape(-1)
        pad = (-flat.shape[0]) % (8 * LANES)
        slabs.append(jnp.pad(flat, (0, pad)).reshape(-1, LANES))
    return jnp.concatenate(slabs, axis=0)


def _unpack_small(slab, shapes):
    out, row = {}, 0
    for name in _SMALL:
        size = int(np.prod(shapes[name]))
        rows = -(-size // (8 * LANES)) * 8
        out[name] = slab[row:row + rows].reshape(-1)[:size].reshape(shapes[name])
        row += rows
    return out


def kernel(x, pre_norm_g, w_in, ln_v_g, ln_v_b, w_spatial, b_spatial, sinks, rel_bias, w_out, post_norm_g, loss_target, m_pre_norm_g, m_w_in, m_ln_v_g, m_ln_v_b, m_w_spatial, m_b_spatial, m_sinks, m_rel_bias, m_w_out, m_post_norm_g, v_pre_norm_g, v_w_in, v_ln_v_g, v_ln_v_b, v_w_spatial, v_b_spatial, v_sinks, v_rel_bias, v_w_out, v_post_norm_g):
    b_loc, seq, _ = x.shape
    tokens = b_loc * seq
    depth = w_in.shape[0]
    in_shard = w_in.shape[2]
    out_shard = w_out.shape[1]
    assert in_shard * N_DEV == IN_WIDTH and out_shard * N_DEV == D_MODEL and seq % BLOCK == 0

    w_in_t, m_w_in_t, v_w_in_t = (jnp.swapaxes(a, 1, 2) for a in (w_in, m_w_in, v_w_in))
    w_in16, w_out16 = w_in_t.astype(BF16), w_out.astype(BF16)

    def gather_start(layer, not_before):
        zones = [lax.empty((N_DEV, in_shard, D_MODEL), BF16), lax.empty((N_DEV, out_shard, D_MODEL), BF16)]
        sems, sends, zones, pre_g = _exchange_start(
            [w_in16[layer], w_out16[layer]], zones, [0, 0], f"weights_send_{layer}",
            pre_norm_g.reshape(depth, 1, D_MODEL), after=not_before)
        return (sems, sends, [0, 0]), zones, pre_g

    def full_weights(gin, gout):
        return gin.reshape(IN_WIDTH, D_MODEL), gout.reshape(D_MODEL, D_MODEL)

    def gather_wait(layer, started, zones, after):
        return full_weights(*_exchange_wait([started], zones, after, f"weights_wait_{layer}"))

    causal = jnp.tril(jnp.ones((CHUNK, CHUNK), dtype=bool))
    wm = jnp.where(causal, w_spatial, 0.0).astype(BF16)
    wmt = _hbm(jnp.swapaxes(wm, -1, -2))
    wm = _hbm(wm)
    bsb = _hbm(jnp.repeat(jnp.swapaxes(b_spatial, -1, -2), HEAD_DIM, axis=-1))
    pre_g3 = _hbm(pre_norm_g.reshape(depth, 1, D_MODEL))
    post_g3 = _hbm(post_norm_g.reshape(depth, 1, D_MODEL))
    ln_g3 = _hbm(ln_v_g.reshape(depth, 1, A_WIDTH))
    ln_b3 = _hbm(ln_v_b.reshape(depth, 1, A_WIDTH))
    bucket = jnp.asarray(_bucket_table())
    bias = _bias_table(rel_bias, bucket)

    xs, saved, weights = [x.reshape(tokens, D_MODEL)], [], []
    pending = None
    for layer in range(depth):
        if layer == 0:
            w, wo = full_weights(*_all_gather([w_in16[0], w_out16[0]], "weights_gather_0"))
        else:
            w, wo = gather_wait(layer, pending[0], pending[1], xs[-1])
        pre_g_fwd = pre_g3
        if layer + 1 < depth:
            pending = gather_start(layer + 1, w)
            pre_g_fwd = _hbm(pending[2])
        h, pa, pb, ya, kept, rstd_w = _inproj_gmlp_fwd(xs[-1], pre_g_fwd, w, ln_g3, ln_b3, wm, bsb, layer)
        if layer + 1 < depth:
            yb, ob, pw, spw, y, x_next = _attn_outproj_fwd(pb, bias, sinks, ya, wo, xs[-1], post_g3, b_loc, layer)
            xs.append(x_next)
        else:
            yb, ob, pw, spw, y, g, loss_part = _attn_outproj_fwd(pb, bias, sinks, ya, wo, xs[-1], post_g3, b_loc, layer,
                                                    target=loss_target.reshape(tokens, D_MODEL))
        saved.append((h, pa, pb, ya, yb, y, ob, pw, spw, kept, rstd_w))
        weights.append((w, wo))

    grads = {name: [None] * depth for name in ("pre_norm_g", "post_norm_g", "ln_v_g", "ln_v_b", "b_spatial", "sinks")}
    zone_in = lax.empty((N_DEV, depth * in_shard, D_MODEL), BF16)
    zone_out = lax.empty((N_DEV, depth * out_shard, D_MODEL), BF16)
    zone_ws = lax.empty((N_DEV, depth * A_GROUPS * CHUNK, CHUNK), F32)
    started_in, started_out = [], []
    dbias = jnp.zeros((B_HEADS, BLOCK, BLOCK), F32)
    for layer in reversed(range(depth)):
        h, pa, pb, ya, yb, y, ob, pw, spw, kept, rstd_w = saved[layer]
        w, wo = weights[layer]
        dya, dyb, dwo, dpost = _outproj_bwd(g, y, ya, yb, wo, post_g3, layer)
        send_out = dwo.reshape(N_DEV, out_shard, D_MODEL)
        sems, sends, (zone_out,), ln_g_bwd = _exchange_start(
            [send_out], [zone_out], [layer * out_shard], f"grads_send_out_{layer}", ln_v_g.reshape(depth, 1, A_WIDTH))
        started_out.append((sems, sends, [layer * out_shard]))
        dqz, dkv, dbias, dsink = _attn_bwd(pb, dyb, ob, pw, spw, dbias, layer)
        da, dw, dws, dbs, dlg, dlb = _gmlp_bwd_dw(pa, kept, rstd_w, dya, h, dqz, dkv, _hbm(ln_g_bwd), ln_b3, wmt, layer)
        send_in = dw.reshape(N_DEV, in_shard, D_MODEL)
        row_offs = [layer * A_GROUPS * CHUNK, layer * in_shard]
        sems, sends, (zone_ws, zone_in), pre_g_bwd = _exchange_start(
            [dws.reshape(A_GROUPS * CHUNK, CHUNK), send_in], [zone_ws, zone_in], row_offs, f"grads_send_in_{layer}",
            pre_norm_g.reshape(depth, 1, D_MODEL))
        started_in.append((sems, sends, row_offs))
        g, dpre = _inproj_bwd_dx(da, dqz, dkv, w, xs[layer], g, _hbm(pre_g_bwd), layer)
        grads["b_spatial"][layer] = dbs[:, :A_GROUPS].T
        grads["ln_v_g"][layer] = dlg[0]
        grads["ln_v_b"][layer] = dlb[0]
        grads["sinks"][layer] = dsink[0, :B_HEADS]
        grads["pre_norm_g"][layer] = dpre[0]
        grads["post_norm_g"][layer] = dpost[0]
    grad_x = g.reshape(x.shape)
    col_bucket = jnp.asarray(np.broadcast_to(_bucket_table()[0:1, ::-1], (B_HEADS, BLOCK)))
    flip = jnp.asarray(np.eye(BLOCK, dtype=np.float32)[::-1], dtype=BF16)
    drel = _rel_bias_grad(dbias, col_bucket, flip)[:, :REL_BUCKETS].T

    (recv_out,) = _exchange_wait(started_out, [zone_out], g, "grads_wait_out")
    res_out = _reduce_adamw(recv_out, w_out.reshape(-1, D_MODEL), m_w_out.reshape(-1, D_MODEL),
                            v_w_out.reshape(-1, D_MODEL), "adamw_w_out")
    res_out = [r.reshape(w_out.shape) for r in res_out]

    no_state = jnp.zeros((1,), F32)
    small_w = dict(pre_norm_g=pre_norm_g, post_norm_g=post_norm_g, ln_v_g=ln_v_g,
                   ln_v_b=ln_v_b, b_spatial=b_spatial, sinks=sinks, rel_bias=rel_bias, loss=no_state)
    small_m = dict(pre_norm_g=m_pre_norm_g, post_norm_g=m_post_norm_g, ln_v_g=m_ln_v_g,
                   ln_v_b=m_ln_v_b, b_spatial=m_b_spatial, sinks=m_sinks, rel_bias=m_rel_bias, loss=no_state)
    small_v = dict(pre_norm_g=v_pre_norm_g, post_norm_g=v_post_norm_g, ln_v_g=v_ln_v_g,
                   ln_v_b=v_ln_v_b, b_spatial=v_b_spatial, sinks=v_sinks, rel_bias=v_rel_bias, loss=no_state)
    small_g = {name: jnp.stack(grads[name]) for name in _SMALL if name not in ("rel_bias", "loss")}
    small_g["rel_bias"] = drel
    small_g["loss"] = loss_part[0, :1]
    shapes = {name: small_w[name].shape for name in _SMALL}
    slots = _all_gather_direct(_pack_small(small_g), "small_grads_all_gather")
    res_small = _reduce_adamw(slots, _pack_small(small_w), _pack_small(small_m), _pack_small(small_v), "adamw_small")

    recv_ws, recv_in = _exchange_wait(started_in, [zone_ws, zone_in], res_small[0], "grads_wait_in")
    res_ws = _reduce_adamw(recv_ws, w_spatial.reshape(-1, CHUNK), m_w_spatial.reshape(-1, CHUNK),
                           v_w_spatial.reshape(-1, CHUNK), "adamw_w_spatial")
    res_ws = [r.reshape(w_spatial.shape) for r in res_ws]
    res_in = _reduce_adamw(recv_in, w_in_t.reshape(-1, D_MODEL), m_w_in_t.reshape(-1, D_MODEL),
                           v_w_in_t.reshape(-1, D_MODEL), "adamw_w_in")
    res_in = [jnp.swapaxes(r.reshape(w_in_t.shape), 1, 2) for r in res_in]
    res_small = [_unpack_small(r, shapes) for r in res_small]

    order = ("pre_norm_g", "w_in", "ln_v_g", "ln_v_b", "w_spatial", "b_spatial", "sinks", "rel_bias", "w_out",
             "post_norm_g")
    outs = [res_small[0]["loss"][0], grad_x]
    for kind in range(4):
        for name in order:
            if name == "w_in":
                outs.append(res_in[kind])
            elif name == "w_out":
                outs.append(res_out[kind])
            elif name == "w_spatial":
                outs.append(res_ws[kind])
            else:
                outs.append(res_small[kind][name])
    return tuple(outs)
```

```python
import math

import numpy as np
import jax
import jax.numpy as jnp
from jax import lax
from jax.experimental import pallas as pl
from jax.experimental.pallas import tpu as pltpu

F32 = jnp.float32
BF16 = jnp.bfloat16

D_MODEL = 1024
A_WIDTH = 512
A_GROUPS = 8
CHUNK = 128
B_HEADS = 8
HEAD_DIM = 64
B_WIDTH = 512
KV_WIDTH = 128
BLOCK = 128
REL_BUCKETS = 32
REL_MAX_DIST = 128
NORM_EPS = 1e-6
PA_WIDTH = 3 * A_WIDTH
PB_WIDTH = 2 * B_WIDTH + 2 * KV_WIDTH
IN_WIDTH = PA_WIDTH + PB_WIDTH
K_OFF, BZ_OFF = B_WIDTH, B_WIDTH + 2 * KV_WIDTH
KEPT_WIDTH = 5 * A_WIDTH
SCALE = HEAD_DIM ** -0.5
NEG = -1e30
N_DEV = 8
LANES = 128

ADAM_LR = 0.001
ADAM_B1 = 0.9
ADAM_B2 = 0.999
ADAM_EPS = 1e-08
ADAM_WD = 0.01
ADAM_STEP = 10

V7X_VMEM_BYTES = 64 * 1024 * 1024
VMEM_TEMP_BYTES = 12 * 1024 * 1024
MESH = pl.DeviceIdType.MESH


def _vmem_limit(block_bytes, scratch_bytes=0):
    need = 2 * sum(block_bytes) + scratch_bytes + VMEM_TEMP_BYTES
    return int(min(need, V7X_VMEM_BYTES - 8 * 1024 * 1024))


def _nbytes(shape, dtype):
    return int(np.prod(shape)) * jnp.dtype(dtype).itemsize


def _token_tile(tokens, cap=512):
    tile = min(cap, tokens // 2)
    assert tokens % tile == 0 and tile % CHUNK == 0, tokens
    return tile


def _row_tile(rows, cap=512):
    best = 8
    for t in range(8, cap + 1, 8):
        if rows % t == 0:
            best = t
    assert rows % best == 0, rows
    return best


def _mm(a, b):
    return lax.dot_general(a, b, (((1,), (0,)), ((), ())), preferred_element_type=F32)


def _mm_nt(a, b):
    return lax.dot_general(a, b, (((1,), (1,)), ((), ())), preferred_element_type=F32)


def _mm_tn(a, b):
    return lax.dot_general(a, b, (((0,), (0,)), ((), ())), preferred_element_type=F32)


_GELU_C = math.sqrt(2.0 / math.pi)


_GELU_A = _GELU_C * 0.044715


def _gelu_parts(x):
    x2 = x * x
    t = jnp.tanh(x * (_GELU_C + _GELU_A * x2))
    return x2, t, 0.5 + 0.5 * t


def _gelu_and_grad(x):
    x2, t, half_plus = _gelu_parts(x)
    grad = half_plus + (0.5 * x) * (1.0 - t * t) * (_GELU_C + (3.0 * _GELU_A) * x2)
    return x * half_plus, grad


def _sigmoid(x):
    return 0.5 + 0.5 * jnp.tanh(0.5 * x)


def _bucket_table():
    q = np.arange(BLOCK)[:, None]
    k = np.arange(BLOCK)[None, :]
    dist = np.where(k <= q, q - k, q + BLOCK - k)
    max_exact = REL_BUCKETS // 2
    safe = np.maximum(dist, 1).astype(np.float32)
    large = max_exact + (np.log(safe / np.float32(max_exact)) / np.float32(math.log(REL_MAX_DIST / max_exact))
                         * np.float32(REL_BUCKETS - max_exact)).astype(np.int32)
    large = np.minimum(large, REL_BUCKETS - 1)
    assert dist.min() >= 0 and dist.max() < BLOCK
    return np.where(dist < max_exact, dist, large).astype(np.int32)


def _hbm(x):
    return pltpu.with_memory_space_constraint(x, pltpu.HBM)


def _slot(px, py, pc):
    return 4 * px + 2 * py + pc


def _all_gather(arrs, name):
    n = len(arrs)

    def body(*refs):
        ins, outs = refs[:n], refs[n:2 * n]
        send_sems, recv_sems, local_sems = refs[2 * n:]
        x, y, c = lax.axis_index("x"), lax.axis_index("y"), lax.axis_index("c")
        me, sibling = (x, y, c), (x, y, 1 - c)
        chips = [(1 - x, y), (x, 1 - y), (1 - x, 1 - y)]

        def copy(a, k, block, to, src=None):
            dst = outs[a].at[_slot(*block)]
            return pltpu.make_async_remote_copy(
                src_ref=dst if src is None else src, dst_ref=dst,
                send_sem=send_sems.at[7 * a + k], recv_sem=recv_sems.at[7 * a + k],
                device_id=to, device_id_type=MESH)

        mine = [pltpu.make_async_copy(ins[a], outs[a].at[_slot(*me)], local_sems.at[a]) for a in range(n)]
        for cp in mine:
            cp.start()
        first = []
        for a in range(n):
            first.append(copy(a, 0, me, sibling, src=ins[a]))
            first += [copy(a, 1 + j, me, (*chip, c), src=ins[a]) for j, chip in enumerate(chips)]
        for cp in first:
            cp.start()
        passed = []
        for j, chip in enumerate(chips):
            for a in range(n):
                copy(a, 1 + j, (*chip, c), me).wait_recv()
                fwd = copy(a, 4 + j, (*chip, c), sibling)
                fwd.start()
                passed.append(fwd)
        for a in range(n):
            copy(a, 0, sibling, me).wait_recv()
            for j, chip in enumerate(chips):
                copy(a, 4 + j, (*chip, 1 - c), me).wait_recv()
        for cp in first + passed:
            cp.wait_send()
        for cp in mine:
            cp.wait()

    any_spec = pl.BlockSpec(memory_space=pl.ANY)
    return pl.pallas_call(
        body, name=name,
        out_shape=[jax.ShapeDtypeStruct((N_DEV,) + a.shape, a.dtype) for a in arrs],
        in_specs=[any_spec] * n, out_specs=[any_spec] * n,
        scratch_shapes=[pltpu.SemaphoreType.DMA((7 * n,)), pltpu.SemaphoreType.DMA((7 * n,)),
                        pltpu.SemaphoreType.DMA((n,))],
    )(*arrs)


def _all_gather_direct(arr, name):
    def body(in_ref, out_ref, send_sems, recv_sems, local_sem):
        me, peers = _peers()
        mine = pltpu.make_async_copy(in_ref, out_ref.at[_slot(*me)], local_sem)
        mine.start()

        def copy(k, origin, to):
            return pltpu.make_async_remote_copy(
                src_ref=in_ref, dst_ref=out_ref.at[_slot(*origin)], send_sem=send_sems.at[k],
                recv_sem=recv_sems.at[k], device_id=to, device_id_type=MESH)

        sends = [copy(k, me, peer) for k, peer in enumerate(peers)]
        for cp in sends:
            cp.start()
        for k, peer in enumerate(peers):
            copy(k, peer, me).wait_recv()
        for cp in sends:
            cp.wait_send()
        mine.wait()

    any_spec = pl.BlockSpec(memory_space=pl.ANY)
    return pl.pallas_call(
        body, name=name,
        out_shape=jax.ShapeDtypeStruct((N_DEV,) + arr.shape, arr.dtype),
        in_specs=[any_spec], out_specs=any_spec,
        scratch_shapes=[pltpu.SemaphoreType.DMA((7,)), pltpu.SemaphoreType.DMA((7,)), pltpu.SemaphoreType.DMA],
    )(arr)


_HBM_SPEC = pl.BlockSpec(memory_space=pltpu.HBM)
_SEM_SPEC = pl.BlockSpec(memory_space=pltpu.SEMAPHORE)
_DATAFLOW = pltpu.SideEffectType.DATAFLOW_SIDE_EFFECTING


def _peers():
    x, y, c = lax.axis_index("x"), lax.axis_index("y"), lax.axis_index("c")
    peers = []
    for k in range(1, N_DEV):
        fx, fy, fc = (k >> 2) & 1, (k >> 1) & 1, k & 1
        peers.append((1 - x if fx else x, 1 - y if fy else y, 1 - c if fc else c))
    return (x, y, c), peers


def _exchange_copy(send_ref, land_ref, row_off, src_slot, dst_slot, sems, idx, peer):
    rows = send_ref.shape[-2]
    src = send_ref.at[src_slot] if len(send_ref.shape) == 3 else send_ref
    return pltpu.make_async_remote_copy(
        src_ref=src, dst_ref=land_ref.at[dst_slot, pl.ds(row_off, rows), :],
        send_sem=sems[0].at[idx], recv_sem=sems[1].at[idx], device_id=peer, device_id_type=MESH)


def _own_copy(send_ref, land_ref, row_off, slot, sem):
    rows = send_ref.shape[-2]
    src = send_ref.at[slot] if len(send_ref.shape) == 3 else send_ref
    return pltpu.make_async_copy(src, land_ref.at[slot, pl.ds(row_off, rows), :], sem)


def _exchange_start(sends, lands, row_offs, name, carry, after=None):
    n = len(sends)

    def body(*refs):
        ins, zones, carry_ref = refs[:n], refs[n:2 * n], refs[2 * n]
        first_out = 2 * n + 1 + (after is not None)
        sems = refs[first_out:first_out + 3]
        carry_out = refs[-1]
        me, peers = _peers()
        for a in range(n):
            for k, peer in enumerate(peers):
                _exchange_copy(ins[a], zones[a], row_offs[a], _slot(*peer), _slot(*me), sems, 7 * a + k, peer).start()
            _own_copy(ins[a], zones[a], row_offs[a], _slot(*me), sems[2].at[a]).start()
        carry_out[...] = carry_ref[...]

    arrays = [_hbm(a) for a in list(sends) + list(lands)]
    vmem = pl.BlockSpec(memory_space=pltpu.VMEM)
    out = pl.pallas_call(
        body, name=name,
        out_shape=(pltpu.SemaphoreType.DMA((7 * n,)), pltpu.SemaphoreType.DMA((7 * n,)), pltpu.SemaphoreType.DMA((n,)),
                   *[pltpu.HBM(a.shape, a.dtype) for a in arrays], jax.ShapeDtypeStruct(carry.shape, carry.dtype)),
        in_specs=[_HBM_SPEC] * (2 * n) + [vmem] + ([pl.BlockSpec(memory_space=pl.ANY)] if after is not None else []),
        out_specs=(_SEM_SPEC, _SEM_SPEC, _SEM_SPEC, *[_HBM_SPEC] * (2 * n), vmem),
        input_output_aliases={i: 3 + i for i in range(2 * n)},
        compiler_params=pltpu.CompilerParams(has_side_effects=_DATAFLOW),
    )(*arrays, carry, *([after] if after is not None else []))
    return (out[0], out[1], out[2]), list(out[3:3 + n]), list(out[3 + n:3 + 2 * n]), out[-1]


def _exchange_wait(started, lands, after, name):
    n = len(lands)
    flat_sends = [s for _, sends, _ in started for s in sends]
    flat_sems = [s for sems, _, _ in started for s in sems]
    ns = len(flat_sends)

    def body(*refs):
        ins, zones = refs[:ns], refs[ns:ns + n]
        sem_refs = refs[ns + n:ns + n + len(flat_sems)]
        me, peers = _peers()
        pos = 0
        for call, (_, sends, row_offs) in enumerate(started):
            sems = sem_refs[3 * call:3 * call + 3]
            for a in range(len(sends)):
                for k, peer in enumerate(peers):
                    cp = _exchange_copy(ins[pos + a], zones[a], row_offs[a], _slot(*peer), _slot(*peer), sems,
                                        7 * a + k, peer)
                    cp.wait_send()
                    cp.wait_recv()
                _own_copy(ins[pos + a], zones[a], row_offs[a], _slot(*me), sems[2].at[a]).wait()
            pos += len(sends)

    arrays = list(flat_sends) + list(lands)
    out = pl.pallas_call(
        body, name=name,
        out_shape=tuple(pltpu.HBM(a.shape, a.dtype) for a in arrays),
        in_specs=[_HBM_SPEC] * len(arrays) + [_SEM_SPEC] * len(flat_sems) + [pl.BlockSpec(memory_space=pl.ANY)],
        out_specs=tuple([_HBM_SPEC] * len(arrays)),
        input_output_aliases={i: i for i in range(len(arrays))},
        compiler_params=pltpu.CompilerParams(has_side_effects=_DATAFLOW),
    )(*arrays, *flat_sems, after)
    return list(out[ns:])


def _bias_table(rel_bias, bucket):
    def body(rel_ref, bucket_ref, out_ref):
        bk = bucket_ref[...]
        for h in range(B_HEADS):
            def pick(b, acc, h=h):
                return jnp.where(bk == b, rel_ref[b, h], acc)
            out_ref[h] = lax.fori_loop(0, REL_BUCKETS, pick, jnp.zeros((BLOCK, BLOCK), F32))

    return pl.pallas_call(
        body, name="bias_table",
        out_shape=jax.ShapeDtypeStruct((B_HEADS, BLOCK, BLOCK), F32),
        in_specs=[pl.BlockSpec(memory_space=pltpu.SMEM), pl.BlockSpec(memory_space=pltpu.VMEM)],
        out_specs=pl.BlockSpec(memory_space=pltpu.VMEM),
    )(rel_bias, bucket)


def _inproj_gmlp_fwd(x2, pre_g3, w, ln_g3, ln_b3, wm, bsb, layer):
    tokens = x2.shape[0]
    tm = _token_tile(tokens)

    def body(x_ref, g_ref, w_ref, lg_ref, lb_ref, wm_ref, bsb_ref, h_ref, pa_ref, pb_ref, ya_ref, kept_ref, rstd_ref):
        x = x_ref[...]
        r = lax.rsqrt(jnp.mean(x * x, axis=-1, keepdims=True) + NORM_EPS)
        h = (x * r * g_ref[...]).astype(BF16)
        h_ref[...] = h
        pa_ref[...] = _mm_nt(h, w_ref[0:PA_WIDTH, :]).astype(BF16)
        pb_ref[...] = _mm_nt(h, w_ref[PA_WIDTH:IN_WIDTH, :]).astype(BF16)
        for ci in range(tm // CHUNK):
            rows = slice(ci * CHUNK, (ci + 1) * CHUNK)
            gu, gv, pz, u, vhat, rstd, _, mixed, sg = _gmlp_forward_chunk(
                pa_ref[rows, :], lg_ref[...], lb_ref[...], wm_ref, bsb_ref[...])
            ya_ref[rows, :] = (u * mixed * (pz * sg)).astype(BF16)
            for k, val in enumerate((u, gu, gv, vhat, mixed)):
                kept_ref[rows, A_WIDTH * k:A_WIDTH * (k + 1)] = val.astype(BF16)
            rstd_ref[rows, :] = jnp.broadcast_to(rstd, (CHUNK, LANES))

    blocks = [_nbytes((tm, D_MODEL), F32), _nbytes((D_MODEL, IN_WIDTH), BF16),
              _nbytes((tm, D_MODEL), BF16), _nbytes((tm, PA_WIDTH), BF16), _nbytes((tm, PB_WIDTH), BF16),
              _nbytes((A_GROUPS, CHUNK, CHUNK), BF16), _nbytes((CHUNK, A_WIDTH), F32), _nbytes((tm, A_WIDTH), BF16)]
    return pl.pallas_call(
        body, name=f"inproj_gmlp_fwd_{layer}", grid=(tokens // tm,),
        in_specs=[pl.BlockSpec((tm, D_MODEL), lambda i: (i, 0)),
                  pl.BlockSpec((None, 1, D_MODEL), lambda i: (layer, 0, 0)),
                  pl.BlockSpec((IN_WIDTH, D_MODEL), lambda i: (0, 0)),
                  pl.BlockSpec((None, 1, A_WIDTH), lambda i: (layer, 0, 0)),
                  pl.BlockSpec((None, 1, A_WIDTH), lambda i: (layer, 0, 0)),
                  pl.BlockSpec((None, A_GROUPS, CHUNK, CHUNK), lambda i: (layer, 0, 0, 0)),
                  pl.BlockSpec((None, CHUNK, A_WIDTH), lambda i: (layer, 0, 0))],
        out_specs=[pl.BlockSpec((tm, D_MODEL), lambda i: (i, 0)),
                   pl.BlockSpec((tm, PA_WIDTH), lambda i: (i, 0)),
                   pl.BlockSpec((tm, PB_WIDTH), lambda i: (i, 0)),
                   pl.BlockSpec((tm, A_WIDTH), lambda i: (i, 0)),
                   pl.BlockSpec((tm, KEPT_WIDTH), lambda i: (i, 0)),
                   pl.BlockSpec((tm, LANES), lambda i: (i, 0))],
        out_shape=[pltpu.HBM((tokens, D_MODEL), BF16),
                   pltpu.HBM((tokens, PA_WIDTH), BF16),
                   pltpu.HBM((tokens, PB_WIDTH), BF16),
                   pltpu.HBM((tokens, A_WIDTH), BF16),
                   pltpu.HBM((tokens, KEPT_WIDTH), BF16),
                   pltpu.HBM((tokens, LANES), F32)],
        compiler_params=pltpu.CompilerParams(dimension_semantics=("parallel",),
                                             vmem_limit_bytes=_vmem_limit(blocks + [_nbytes((tm, KEPT_WIDTH), BF16),
                                                                                    _nbytes((tm, LANES), F32)])),
    )(_hbm(x2), pre_g3, _hbm(w), ln_g3, ln_b3, wm, bsb)


def _gmlp_forward_chunk(pa, ln_g, ln_b, wm_ref, bsb):
    pu = pa[:, 0:A_WIDTH].astype(F32)
    pv = pa[:, A_WIDTH:2 * A_WIDTH].astype(F32)
    pz = pa[:, 2 * A_WIDTH:3 * A_WIDTH].astype(F32)
    u, gu = _gelu_and_grad(pu)
    vv, gv = _gelu_and_grad(pv)
    mu = jnp.mean(vv, axis=-1, keepdims=True)
    xc = vv - mu
    rstd = lax.rsqrt(jnp.mean(xc * xc, axis=-1, keepdims=True) + NORM_EPS)
    vhat = xc * rstd
    vnb = (vhat * ln_g + ln_b).astype(BF16)
    low = lax.broadcasted_iota(jnp.int32, (CHUNK, LANES), 1) < HEAD_DIM
    parts = []
    for p in range(A_GROUPS // 2):
        vp = vnb[:, LANES * p:LANES * (p + 1)]
        parts.append(jnp.where(low, _mm(wm_ref[2 * p], vp), _mm(wm_ref[2 * p + 1], vp)))
    mixed = jnp.concatenate(parts, axis=1) + bsb
    sg = _sigmoid(pz)
    return gu, gv, pz, u, vhat, rstd, vnb, mixed, sg


def _kv_variants(kv):
    t = kv.astype(F32)
    rolled = pltpu.roll(t, HEAD_DIM, 1)
    low = lax.broadcasted_iota(jnp.int32, t.shape, 1) < HEAD_DIM
    zero = jnp.zeros_like(t)
    head0 = (jnp.where(low, t, zero).astype(BF16), jnp.where(low, zero, rolled).astype(BF16))
    head1 = (jnp.where(low, rolled, zero).astype(BF16), jnp.where(low, zero, t).astype(BF16))
    return head0, head1


def _kv_slabs(kv_ref, off, count):
    return [_kv_variants(kv_ref[BLOCK * s:BLOCK * (s + 1), off:off + KV_WIDTH]) for s in range(count)]


def _block_operands(slabs, i):
    return tuple(tuple(jnp.concatenate([slabs[i][kvh][j], slabs[i + 1][kvh][j]], axis=0) for j in range(2))
                 for kvh in range(2))


def _band_masks():
    row = lax.broadcasted_iota(jnp.int32, (BLOCK, BLOCK), 0)
    col = lax.broadcasted_iota(jnp.int32, (BLOCK, BLOCK), 1)
    return col <= row


def _wrap(full, tri):
    return jnp.where(tri, full[:, BLOCK:2 * BLOCK], full[:, 0:BLOCK])


def _attn_probs(sf, bias_h, sink, tri, kill):
    s = _wrap(sf, tri) + bias_h
    s = jnp.where(kill, NEG, s)
    m = jnp.maximum(jnp.max(s, axis=-1, keepdims=True), sink)
    e = jnp.exp(s - m)
    es = jnp.exp(sink - m)
    inv = 1.0 / (jnp.sum(e, axis=-1, keepdims=True) + es)
    return e * inv, es * inv


def _unwrap16(p, tri):
    p = p.astype(BF16)
    zero = jnp.zeros_like(p)
    return jnp.concatenate([jnp.where(tri, zero, p), jnp.where(tri, p, zero)], axis=1)


def _fill_kv(kv_ref, prev_ref, cur_ref):
    kv_ref[0:BLOCK, :] = prev_ref[...]
    kv_ref[BLOCK:, :] = cur_ref[:, K_OFF:K_OFF + 2 * KV_WIDTH]


def _attn_outproj_fwd(pb, bias, sinks, ya, wo, x2, post_g3, b_loc, layer, target=None):
    tokens = pb.shape[0]
    nb = tokens // b_loc // BLOCK
    tq = _token_tile(tokens)
    per_tile = tq // BLOCK
    steps = tokens // tq
    with_loss = target is not None
    half = D_MODEL // 2

    def body(*refs):
        sink_ref, cur_ref, prev_ref, bias_ref, ya_ref, woa_ref, wob_ref, x_ref, g_ref = refs[:9]
        if with_loss:
            t_ref, yb_ref, ob_ref, pw_ref, spw_ref, y_ref, grad_ref, loss_ref, kv_ref, acc_ref = refs[9:]
        else:
            yb_ref, ob_ref, pw_ref, spw_ref, y_ref, xn_ref, kv_ref = refs[9:]
        t = pl.program_id(0)
        _fill_kv(kv_ref, prev_ref, cur_ref)
        k_slabs = _kv_slabs(kv_ref, 0, per_tile + 1)
        v_slabs = _kv_slabs(kv_ref, KV_WIDTH, per_tile + 1)
        tri = _band_masks()
        head_lane = lax.broadcasted_iota(jnp.int32, (BLOCK, LANES), 1)

        def block(i, carry):
            start = i * BLOCK
            rows = pl.ds(start, BLOCK)
            first = lax.rem(t * per_tile + i, nb) == 0
            kill = jnp.logical_and(first, jnp.logical_not(tri))
            k_ops = _block_operands(k_slabs, i)
            v_ops = _block_operands(v_slabs, i)
            p16 = {}
            sink_cols = jnp.zeros((BLOCK, LANES), F32)
            for kvh in range(2):
                pairs = (2 * kvh, 2 * kvh + 1)
                qs = jnp.concatenate([cur_ref[rows, LANES * p:LANES * (p + 1)] for p in pairs], axis=0) * SCALE
                for j in range(2):
                    sf = _mm_nt(qs, k_ops[kvh][j])
                    for r, p in enumerate(pairs):
                        hd = 2 * p + j
                        probs, sink_p = _attn_probs(sf[BLOCK * r:BLOCK * (r + 1)], bias_ref[hd],
                                                    sink_ref[layer, hd], tri, kill)
                        probs = probs.astype(BF16)
                        p16[hd] = _unwrap16(probs, tri)
                        pw_ref[rows, LANES * hd:LANES * (hd + 1)] = probs
                        sink_cols = jnp.where(head_lane == hd, sink_p, sink_cols)
            spw_ref[rows, :] = sink_cols
            for kvh in range(2):
                pairs = (2 * kvh, 2 * kvh + 1)
                out = jnp.zeros((2 * BLOCK, LANES), F32)
                for j in range(2):
                    out = out + _mm(jnp.concatenate([p16[2 * p + j] for p in pairs], axis=0), v_ops[kvh][j])
                for r, p in enumerate(pairs):
                    bz = cur_ref[rows, BZ_OFF + LANES * p:BZ_OFF + LANES * (p + 1)].astype(F32)
                    out_p = out[BLOCK * r:BLOCK * (r + 1)]
                    ob_ref[rows, LANES * p:LANES * (p + 1)] = out_p.astype(BF16)
                    yb_ref[rows, LANES * p:LANES * (p + 1)] = (out_p * (bz * _sigmoid(bz))).astype(BF16)
            y = _mm(ya_ref[rows, :], woa_ref[...]) + _mm(yb_ref[rows, :], wob_ref[...])
            r = lax.rsqrt(jnp.mean(y * y, axis=-1, keepdims=True) + NORM_EPS)
            x_next = x_ref[rows, :] + y * r * g_ref[...]
            y_ref[rows, :] = y.astype(BF16)
            if not with_loss:
                xn_ref[rows, :] = x_next
            else:
                err = x_next - t_ref[rows, :]
                grad_ref[rows, :] = err * (1.0 / D_MODEL)
                acc_ref[...] += jnp.sum(err * err, axis=0, keepdims=True)
            return carry

        if with_loss:
            @pl.when(t == 0)
            def _():
                acc_ref[...] = jnp.zeros_like(acc_ref)

        for i in range(per_tile):
            block(i, 0)

        if not with_loss:
            return

        @pl.when(t == steps - 1)
        def _():
            total = jnp.sum(acc_ref[...], axis=-1, keepdims=True) * (0.5 / D_MODEL)
            loss_ref[...] = jnp.broadcast_to(total, loss_ref.shape)

    tile = pl.BlockSpec((tq, D_MODEL), lambda t: (t, 0))
    blocks = [_nbytes((tq, PB_WIDTH), BF16), _nbytes((BLOCK, 2 * KV_WIDTH), BF16),
              _nbytes((B_HEADS, BLOCK, BLOCK), F32), 2 * _nbytes((tq, B_WIDTH), BF16),
              2 * _nbytes((half, D_MODEL), BF16), 3 * _nbytes((tq, D_MODEL), F32), _nbytes((tq, D_MODEL), BF16),
              _nbytes((tq, B_HEADS * BLOCK), BF16), _nbytes((tq, LANES), F32)]
    scratch = _nbytes((tq + BLOCK, 2 * KV_WIDTH), BF16)
    in_specs = [pl.BlockSpec(memory_space=pltpu.SMEM),
                pl.BlockSpec((tq, PB_WIDTH), lambda t: (t, 0)),
                pl.BlockSpec((BLOCK, 2 * KV_WIDTH),
                             lambda t: (jnp.maximum(t * per_tile - 1, 0), K_OFF // (2 * KV_WIDTH))),
                pl.BlockSpec((B_HEADS, BLOCK, BLOCK), lambda t: (0, 0, 0)),
                pl.BlockSpec((tq, half), lambda t: (t, 0)),
                pl.BlockSpec((half, D_MODEL), lambda t: (0, 0)),
                pl.BlockSpec((half, D_MODEL), lambda t: (1, 0)),
                tile,
                pl.BlockSpec((None, 1, D_MODEL), lambda t: (layer, 0, 0))]
    out_specs = [pl.BlockSpec((tq, B_WIDTH), lambda t: (t, 0)), pl.BlockSpec((tq, B_WIDTH), lambda t: (t, 0)),
                 pl.BlockSpec((tq, B_HEADS * BLOCK), lambda t: (t, 0)),
                 pl.BlockSpec((tq, LANES), lambda t: (t, 0)), tile, tile]
    out_shape = [pltpu.HBM((tokens, B_WIDTH), BF16), pltpu.HBM((tokens, B_WIDTH), BF16),
                 pltpu.HBM((tokens, B_HEADS * BLOCK), BF16),
                 pltpu.HBM((tokens, LANES), F32), pltpu.HBM((tokens, D_MODEL), BF16),
                 pltpu.HBM((tokens, D_MODEL), F32)]
    scratch_shapes = [pltpu.VMEM((tq + BLOCK, 2 * KV_WIDTH), BF16)]
    operands = [sinks, _hbm(pb), _hbm(pb), _hbm(bias), _hbm(ya), _hbm(wo), _hbm(wo), _hbm(x2), post_g3]
    if with_loss:
        in_specs.append(tile)
        out_specs.append(pl.BlockSpec((1, LANES), lambda t: (0, 0)))
        out_shape.append(pltpu.HBM((1, LANES), F32))
        scratch_shapes.append(pltpu.VMEM((1, D_MODEL), F32))
        operands.append(_hbm(target))
    return pl.pallas_call(
        body, name=f"attn_outproj_fwd_{layer}", grid=(steps,),
        in_specs=in_specs, out_specs=out_specs, out_shape=out_shape, scratch_shapes=scratch_shapes,
        compiler_params=pltpu.CompilerParams(dimension_semantics=("arbitrary" if with_loss else "parallel",),
                                             vmem_limit_bytes=_vmem_limit(blocks, scratch)),
    )(*operands)


def _outproj_bwd(g, y, ya, yb, wo, post_g3, layer):
    tokens = g.shape[0]
    tm = _token_tile(tokens, cap=1024)
    half = D_MODEL // 2
    steps = tokens // tm

    def body(g_ref, y_ref, ya_ref, yb_ref, woa_ref, wob_ref, pg_ref, dya_ref, dyb_ref, dwo16_ref, dpg_ref, dwo_ref):
        @pl.when(pl.program_id(0) == 0)
        def _():
            dwo_ref[...] = jnp.zeros_like(dwo_ref)
            dpg_ref[...] = jnp.zeros_like(dpg_ref)

        gv = g_ref[...]
        yf = y_ref[...].astype(F32)
        r = lax.rsqrt(jnp.mean(yf * yf, axis=-1, keepdims=True) + NORM_EPS)
        yhat = yf * r
        gg = gv * pg_ref[...]
        dy = r * (gg - yhat * jnp.mean(gg * yhat, axis=-1, keepdims=True))
        dpg_ref[...] += jnp.sum(gv * yhat, axis=0, keepdims=True)
        dyb16 = dy.astype(BF16)
        dya_ref[...] = _mm_nt(dyb16, woa_ref[...]).astype(BF16)
        dyb_ref[...] = _mm_nt(dyb16, wob_ref[...]).astype(BF16)
        dwo_ref[0:half, :] += _mm_tn(ya_ref[...], dyb16)
        dwo_ref[half:D_MODEL, :] += _mm_tn(yb_ref[...], dyb16)

        @pl.when(pl.program_id(0) == steps - 1)
        def _():
            dwo16_ref[...] = dwo_ref[...].astype(BF16)

    blocks = [_nbytes((tm, D_MODEL), F32), _nbytes((tm, D_MODEL), BF16), 4 * _nbytes((tm, half), BF16),
              2 * _nbytes((half, D_MODEL), BF16), _nbytes((D_MODEL, D_MODEL), BF16)]
    return pl.pallas_call(
        body, name=f"outproj_bwd_{layer}", grid=(tokens // tm,),
        in_specs=[pl.BlockSpec((tm, D_MODEL), lambda i: (i, 0)),
                  pl.BlockSpec((tm, D_MODEL), lambda i: (i, 0)),
                  pl.BlockSpec((tm, half), lambda i: (i, 0)),
                  pl.BlockSpec((tm, half), lambda i: (i, 0)),
                  pl.BlockSpec((half, D_MODEL), lambda i: (0, 0)),
                  pl.BlockSpec((half, D_MODEL), lambda i: (1, 0)),
                  pl.BlockSpec((None, 1, D_MODEL), lambda i: (layer, 0, 0))],
        out_specs=[pl.BlockSpec((tm, half), lambda i: (i, 0)),
                   pl.BlockSpec((tm, half), lambda i: (i, 0)),
                   pl.BlockSpec((D_MODEL, D_MODEL), lambda i: (0, 0)),
                   pl.BlockSpec((1, D_MODEL), lambda i: (0, 0))],
        out_shape=[pltpu.HBM((tokens,half), BF16),
                   pltpu.HBM((tokens,half), BF16),
                   pltpu.HBM((D_MODEL,D_MODEL), BF16),
                   pltpu.HBM((1,D_MODEL), F32)],
        scratch_shapes=[pltpu.VMEM((D_MODEL, D_MODEL), F32)],
        compiler_params=pltpu.CompilerParams(dimension_semantics=("arbitrary",),
                                             vmem_limit_bytes=_vmem_limit(blocks, _nbytes((D_MODEL, D_MODEL), F32))),
    )(_hbm(g), _hbm(y), _hbm(ya), _hbm(yb), _hbm(wo), _hbm(wo), post_g3)


def _gmlp_bwd_dw(pa, kept, rstd_w, dya, h, dqz, dkv, ln_g3, ln_b3, wmt, layer):
    tokens = pa.shape[0]
    tc = _token_tile(tokens)
    steps = tokens // tc

    def body(pz_ref, kept_ref, rstd_ref, dya_ref, h_ref, dqz_ref, dkv_ref, lg_ref, lb_ref, wmt_ref,
             da_ref, dw_ref, dws_ref, dbs_ref, dlg_ref, dlb_ref, acc_ref, dbsb_ref):
        i = pl.program_id(0)

        @pl.when(i == 0)
        def _():
            acc_ref[...] = jnp.zeros_like(acc_ref)
            dws_ref[...] = jnp.zeros_like(dws_ref)
            dlg_ref[...] = jnp.zeros_like(dlg_ref)
            dlb_ref[...] = jnp.zeros_like(dlb_ref)
            dbsb_ref[...] = jnp.zeros_like(dbsb_ref)

        ln_g = lg_ref[...]
        low = lax.broadcasted_iota(jnp.int32, (CHUNK, LANES), 1) < HEAD_DIM
        hv = h_ref[...]
        acc_ref[:, PA_WIDTH:PA_WIDTH + B_WIDTH] += _mm_tn(hv, dqz_ref[:, 0:B_WIDTH])
        acc_ref[:, PA_WIDTH + K_OFF:PA_WIDTH + BZ_OFF] += _mm_tn(hv, dkv_ref[...])
        acc_ref[:, PA_WIDTH + BZ_OFF:IN_WIDTH] += _mm_tn(hv, dqz_ref[:, B_WIDTH:2 * B_WIDTH])

        for ci in range(tc // CHUNK):
            rows = slice(ci * CHUNK, (ci + 1) * CHUNK)
            u, gu, gv, vhat, mixed = (kept_ref[rows, A_WIDTH * k:A_WIDTH * (k + 1)].astype(F32) for k in range(5))
            rstd = jnp.concatenate([rstd_ref[rows, :]] * (A_WIDTH // LANES), axis=1)
            vnb = (vhat * ln_g + lb_ref[...]).astype(BF16)
            pz = pz_ref[rows, :].astype(F32)
            sg = _sigmoid(pz)
            dy = dya_ref[rows, :].astype(F32)
            sz = pz * sg
            dy_sz = dy * sz
            du = dy_sz * mixed
            dmixed = dy_sz * u
            dz = dy * (u * mixed) * (sg + sz * (1.0 - sg))
            dbsb_ref[...] += dmixed
            dmb = dmixed.astype(BF16)
            zero = jnp.zeros((CHUNK, LANES), BF16)
            parts = []
            for p in range(A_GROUPS // 2):
                dmp = dmb[:, LANES * p:LANES * (p + 1)]
                vp = vnb[:, LANES * p:LANES * (p + 1)]
                parts.append(jnp.where(low, _mm(wmt_ref[2 * p], dmp), _mm(wmt_ref[2 * p + 1], dmp)))
                dws_ref[2 * p] += _mm_nt(jnp.where(low, dmp, zero), vp)
                dws_ref[2 * p + 1] += _mm_nt(jnp.where(low, zero, dmp), vp)
            dvn = jnp.concatenate(parts, axis=1)
            dlg_ref[...] += jnp.sum(dvn * vhat, axis=0, keepdims=True)
            dlb_ref[...] += jnp.sum(dvn, axis=0, keepdims=True)
            dvh = dvn * ln_g
            dvv = rstd * (dvh - jnp.mean(dvh, axis=-1, keepdims=True)
                          - vhat * jnp.mean(dvh * vhat, axis=-1, keepdims=True))
            da_ref[rows, 0:A_WIDTH] = (du * gu).astype(BF16)
            da_ref[rows, A_WIDTH:2 * A_WIDTH] = (dvv * gv).astype(BF16)
            da_ref[rows, 2 * A_WIDTH:3 * A_WIDTH] = dz.astype(BF16)

        acc_ref[:, 0:PA_WIDTH] += _mm_tn(hv, da_ref[...])

        @pl.when(i == steps - 1)
        def _():
            for c in range(IN_WIDTH // LANES):
                cols = slice(LANES * c, LANES * (c + 1))
                dw_ref[cols, :] = acc_ref[:, cols].T.astype(BF16)
            causal = (lax.broadcasted_iota(jnp.int32, (CHUNK, CHUNK), 0)
                      >= lax.broadcasted_iota(jnp.int32, (CHUNK, CHUNK), 1))
            for h in range(A_GROUPS):
                dws_ref[h] = jnp.where(causal, dws_ref[h], 0.0)
            acc = dbsb_ref[...]
            lane_full = lax.broadcasted_iota(jnp.int32, (CHUNK, A_WIDTH), 1)
            lane_out = lax.broadcasted_iota(jnp.int32, (CHUNK, LANES), 1)
            out = jnp.zeros((CHUNK, LANES), F32)
            for h in range(A_GROUPS):
                in_group = jnp.logical_and(lane_full >= HEAD_DIM * h, lane_full < HEAD_DIM * (h + 1))
                s = jnp.sum(jnp.where(in_group, acc, 0.0), axis=-1, keepdims=True)
                out = jnp.where(lane_out == h, s, out)
            dbs_ref[...] = out

    blocks = [_nbytes((tc, KEPT_WIDTH), BF16), 2 * _nbytes((tc, A_WIDTH), BF16), _nbytes((A_GROUPS, CHUNK, CHUNK), BF16),
              _nbytes((tc, LANES), F32), _nbytes((tc, PA_WIDTH), BF16), _nbytes((A_GROUPS, CHUNK, CHUNK), F32),
              _nbytes((CHUNK, LANES), F32), _nbytes((tc, D_MODEL), BF16), _nbytes((tc, 2 * B_WIDTH), BF16),
              _nbytes((tc, 2 * KV_WIDTH), BF16), _nbytes((D_MODEL, IN_WIDTH), BF16)]
    scratch = _nbytes((D_MODEL, IN_WIDTH), F32) + _nbytes((CHUNK, A_WIDTH), F32)
    in_specs = [pl.BlockSpec((tc, A_WIDTH), lambda i: (i, 2)),
                pl.BlockSpec((tc, KEPT_WIDTH), lambda i: (i, 0)),
                pl.BlockSpec((tc, LANES), lambda i: (i, 0)),
                pl.BlockSpec((tc, A_WIDTH), lambda i: (i, 0)),
                pl.BlockSpec((tc, D_MODEL), lambda i: (i, 0)),
                pl.BlockSpec((tc, 2 * B_WIDTH), lambda i: (i, 0)),
                pl.BlockSpec((pl.Element(tc), pl.Element(2 * KV_WIDTH)),
                             lambda i: (pl.multiple_of(i * tc + BLOCK, BLOCK), 0)),
                pl.BlockSpec((None, 1, A_WIDTH), lambda i: (layer, 0, 0)),
                pl.BlockSpec((None, 1, A_WIDTH), lambda i: (layer, 0, 0)),
                pl.BlockSpec((None, A_GROUPS, CHUNK, CHUNK), lambda i: (layer, 0, 0, 0))]
    operands = [_hbm(pa), _hbm(kept), _hbm(rstd_w), _hbm(dya), _hbm(h), _hbm(dqz), _hbm(dkv), ln_g3, ln_b3, wmt]
    out_specs = [pl.BlockSpec((tc, PA_WIDTH), lambda i: (i, 0)),
                 pl.BlockSpec((IN_WIDTH, D_MODEL), lambda i: (0, 0)),
                 pl.BlockSpec((A_GROUPS, CHUNK, CHUNK), lambda i: (0, 0, 0)),
                 pl.BlockSpec((CHUNK, LANES), lambda i: (0, 0)),
                 pl.BlockSpec((1, A_WIDTH), lambda i: (0, 0)),
                 pl.BlockSpec((1, A_WIDTH), lambda i: (0, 0))]
    out_shape = [pltpu.HBM((tokens, PA_WIDTH), BF16),
                 pltpu.HBM((IN_WIDTH, D_MODEL), BF16),
                 pltpu.HBM((A_GROUPS, CHUNK, CHUNK), F32),
                 pltpu.HBM((CHUNK, LANES), F32),
                 pltpu.HBM((1, A_WIDTH), F32),
                 pltpu.HBM((1, A_WIDTH), F32)]
    scratch_shapes = [pltpu.VMEM((D_MODEL, IN_WIDTH), F32), pltpu.VMEM((CHUNK, A_WIDTH), F32)]
    return pl.pallas_call(
        body, name=f"gmlp_bwd_dw_{layer}", grid=(steps,),
        in_specs=in_specs, out_specs=out_specs, out_shape=out_shape, scratch_shapes=scratch_shapes,
        compiler_params=pltpu.CompilerParams(dimension_semantics=("arbitrary",),
                                             vmem_limit_bytes=_vmem_limit(blocks, scratch)),
    )(*operands)


def _attn_bwd(pb, dyb, out_w, probs_w, sink_w, dbias_in, layer):
    tokens = pb.shape[0]
    qz_width = 2 * B_WIDTH
    tq = _token_tile(tokens)
    per_tile = tq // BLOCK
    nt = tokens // tq

    def body(cur_ref, prev_ref, dyb_ref, ob_ref, pw_ref, spw_ref, dbias_in_ref, dqz_ref, dkv_ref, dbias_ref, dsink_ref,
             kv_ref, acc_ref):
        t = pl.program_id(0)

        @pl.when(t == 0)
        def _():
            dbias_ref[...] = dbias_in_ref[...]
            dsink_ref[...] = jnp.zeros_like(dsink_ref)
            acc_ref[0:BLOCK, :] = jnp.zeros((BLOCK, 2 * KV_WIDTH), F32)

        @pl.when(t < nt)
        def _():
            acc_ref[BLOCK:, :] = jnp.zeros((tq, 2 * KV_WIDTH), F32)
            _fill_kv(kv_ref, prev_ref, cur_ref)
            k_slabs = _kv_slabs(kv_ref, 0, per_tile + 1)
            v_slabs = _kv_slabs(kv_ref, KV_WIDTH, per_tile + 1)
            tri = _band_masks()
            low = lax.broadcasted_iota(jnp.int32, (BLOCK, LANES), 1) < HEAD_DIM
            low_kv = lax.broadcasted_iota(jnp.int32, (2 * BLOCK, LANES), 1) < HEAD_DIM

            head_lane = lax.broadcasted_iota(jnp.int32, (BLOCK, LANES), 1)

            def block(i, carry):
                start = i * BLOCK
                rows = pl.ds(start, BLOCK)
                k_ops = _block_operands(k_slabs, i)
                v_ops = _block_operands(v_slabs, i)
                probs, p16, qs, dos, delta, ds16 = {}, {}, {}, {}, {}, {}
                for hd in range(B_HEADS):
                    kept_p = pw_ref[rows, LANES * hd:LANES * (hd + 1)]
                    probs[hd] = kept_p.astype(F32)
                    p16[hd] = _unwrap16(kept_p, tri)
                for kvh in range(2):
                    qs[kvh] = jnp.concatenate(
                        [cur_ref[rows, LANES * p:LANES * (p + 1)] for p in (2 * kvh, 2 * kvh + 1)], axis=0)
                for kvh in range(2):
                    pairs = (2 * kvh, 2 * kvh + 1)
                    d_outs = []
                    for r, p in enumerate(pairs):
                        bz = cur_ref[rows, BZ_OFF + LANES * p:BZ_OFF + LANES * (p + 1)].astype(F32)
                        sg = _sigmoid(bz)
                        dyp = dyb_ref[rows, LANES * p:LANES * (p + 1)].astype(F32)
                        out_p = ob_ref[rows, LANES * p:LANES * (p + 1)].astype(F32)
                        d_out = dyp * (bz * sg)
                        dqz_ref[rows, B_WIDTH + LANES * p:B_WIDTH + LANES * (p + 1)] = (
                            dyp * out_p * (sg * (1.0 + bz * (1.0 - sg)))).astype(BF16)
                        dod = d_out * out_p
                        delta[2 * p] = jnp.sum(jnp.where(low, dod, 0.0), axis=-1, keepdims=True)
                        delta[2 * p + 1] = jnp.sum(jnp.where(low, 0.0, dod), axis=-1, keepdims=True)
                        d_outs.append(d_out.astype(BF16))
                    dos[kvh] = jnp.concatenate(d_outs, axis=0)
                delta_cols = jnp.zeros((BLOCK, LANES), F32)
                for hd in range(B_HEADS):
                    delta_cols = jnp.where(head_lane == hd, delta[hd], delta_cols)
                dsink_ref[0:1, :] += jnp.sum(-(spw_ref[rows, :] * delta_cols), axis=0, keepdims=True)
                for kvh in range(2):
                    pairs = (2 * kvh, 2 * kvh + 1)
                    for j in range(2):
                        dpf = _mm_nt(dos[kvh], v_ops[kvh][j])
                        for r, p in enumerate(pairs):
                            hd = 2 * p + j
                            ds = probs[hd] * (_wrap(dpf[BLOCK * r:BLOCK * (r + 1)], tri) - delta[hd])
                            dbias_ref[hd] += ds
                            ds16[hd] = _unwrap16(ds, tri)
                dk_acc = [[None, None], [None, None]]
                dv_acc = [[None, None], [None, None]]
                for kvh in range(2):
                    pairs = (2 * kvh, 2 * kvh + 1)
                    dq = jnp.zeros((2 * BLOCK, LANES), F32)
                    for j in range(2):
                        dss = jnp.concatenate([ds16[2 * p + j] for p in pairs], axis=0)
                        pss = jnp.concatenate([p16[2 * p + j] for p in pairs], axis=0)
                        dq = dq + _mm(dss, k_ops[kvh][j])
                        dk_acc[kvh][j] = _mm_tn(dss, qs[kvh])
                        dv_acc[kvh][j] = _mm_tn(pss, dos[kvh])
                    for r, p in enumerate(pairs):
                        dqz_ref[rows, LANES * p:LANES * (p + 1)] = (dq[BLOCK * r:BLOCK * (r + 1)] * SCALE).astype(BF16)

                def fold(acc):
                    return jnp.where(low_kv,
                                     acc[0][0] + pltpu.roll(acc[0][1], HEAD_DIM, 1),
                                     pltpu.roll(acc[1][0], HEAD_DIM, 1) + acc[1][1])

                acc_ref[pl.ds(start, 2 * BLOCK), :] += jnp.concatenate(
                    [fold(dk_acc) * SCALE, fold(dv_acc)], axis=1)
                return carry

            for i in range(per_tile):
                block(i, 0)
            dkv_ref[...] = acc_ref[0:tq, :].astype(BF16)
            acc_ref[0:BLOCK, :] = acc_ref[tq:tq + BLOCK, :]

        @pl.when(t == nt)
        def _():
            dkv_ref[0:BLOCK, :] = acc_ref[0:BLOCK, :].astype(BF16)
            dkv_ref[BLOCK:, :] = jnp.zeros((tq - BLOCK, 2 * KV_WIDTH), BF16)

    def cur_map(t):
        return (jnp.minimum(t, nt - 1), 0)

    def prev_map(t):
        return (jnp.maximum(jnp.minimum(t, nt - 1) * per_tile - 1, 0), K_OFF // (2 * KV_WIDTH))

    blocks = [_nbytes((tq, PB_WIDTH), BF16), _nbytes((BLOCK, 2 * KV_WIDTH), BF16), _nbytes((tq, B_WIDTH), BF16),
              2 * _nbytes((B_HEADS, BLOCK, BLOCK), F32), _nbytes((tq, qz_width), BF16),
              _nbytes((tq, 2 * KV_WIDTH), BF16), _nbytes((B_HEADS, LANES), F32),
              _nbytes((tq, B_HEADS * BLOCK), BF16), _nbytes((tq, LANES), F32)]
    scratch = _nbytes((tq + BLOCK, 2 * KV_WIDTH), BF16) + _nbytes((tq + BLOCK, 2 * KV_WIDTH), F32)
    return pl.pallas_call(
        body, name=f"attn_bwd_{layer}", grid=(nt + 1,),
        in_specs=[pl.BlockSpec((tq, PB_WIDTH), cur_map),
                  pl.BlockSpec((BLOCK, 2 * KV_WIDTH), prev_map),
                  pl.BlockSpec((tq, B_WIDTH), cur_map),
                  pl.BlockSpec((tq, B_WIDTH), cur_map),
                  pl.BlockSpec((tq, B_HEADS * BLOCK), cur_map),
                  pl.BlockSpec((tq, LANES), cur_map),
                  pl.BlockSpec((B_HEADS, BLOCK, BLOCK), lambda t: (0, 0, 0))],
        out_specs=[pl.BlockSpec((tq, qz_width), cur_map),
                   pl.BlockSpec((tq, 2 * KV_WIDTH), lambda t: (t, 0)),
                   pl.BlockSpec((B_HEADS, BLOCK, BLOCK), lambda t: (0, 0, 0)),
                   pl.BlockSpec((B_HEADS, LANES), lambda t: (0, 0))],
        out_shape=[pltpu.HBM((tokens, qz_width), BF16),
                   pltpu.HBM((tokens + tq, 2 * KV_WIDTH), BF16),
                   pltpu.HBM((B_HEADS, BLOCK, BLOCK), F32),
                   pltpu.HBM((B_HEADS, LANES), F32)],
        scratch_shapes=[pltpu.VMEM((tq + BLOCK, 2 * KV_WIDTH), BF16),
                        pltpu.VMEM((tq + BLOCK, 2 * KV_WIDTH), F32)],
        compiler_params=pltpu.CompilerParams(dimension_semantics=("arbitrary",),
                                             vmem_limit_bytes=_vmem_limit(blocks, scratch)),
    )(_hbm(pb), _hbm(pb), _hbm(dyb), _hbm(out_w), _hbm(probs_w), _hbm(sink_w), _hbm(dbias_in))


def _inproj_bwd_dx(da, dqz, dkv, w, x2, g, pre_g3, layer):
    tokens = x2.shape[0]
    tm = _token_tile(tokens)

    def body(da_ref, dqz_ref, dkv_ref, w_ref, x_ref, g_ref, pg_ref, gn_ref, dpg_ref):
        @pl.when(pl.program_id(0) == 0)
        def _():
            dpg_ref[...] = jnp.zeros_like(dpg_ref)

        dh = _mm(da_ref[...], w_ref[0:PA_WIDTH, :])
        dh += _mm(dqz_ref[:, 0:B_WIDTH], w_ref[PA_WIDTH:PA_WIDTH + B_WIDTH, :])
        dh += _mm(dkv_ref[...], w_ref[PA_WIDTH + K_OFF:PA_WIDTH + BZ_OFF, :])
        dh += _mm(dqz_ref[:, B_WIDTH:2 * B_WIDTH], w_ref[PA_WIDTH + BZ_OFF:IN_WIDTH, :])
        x = x_ref[...]
        r = lax.rsqrt(jnp.mean(x * x, axis=-1, keepdims=True) + NORM_EPS)
        xhat = x * r
        dhg = dh * pg_ref[...]
        dpg_ref[...] += jnp.sum(dh * xhat, axis=0, keepdims=True)
        gn_ref[...] = g_ref[...] + r * (dhg - xhat * jnp.mean(dhg * xhat, axis=-1, keepdims=True))

    blocks = [_nbytes((tm, PA_WIDTH), BF16), _nbytes((tm, 2 * B_WIDTH), BF16), _nbytes((tm, 2 * KV_WIDTH), BF16),
              _nbytes((D_MODEL, IN_WIDTH), BF16), 3 * _nbytes((tm, D_MODEL), F32)]
    return pl.pallas_call(
        body, name=f"inproj_bwd_dx_{layer}", grid=(tokens // tm,),
        in_specs=[pl.BlockSpec((tm, PA_WIDTH), lambda i: (i, 0)),
                  pl.BlockSpec((tm, 2 * B_WIDTH), lambda i: (i, 0)),
                  pl.BlockSpec((pl.Element(tm), pl.Element(2 * KV_WIDTH)), lambda i: (pl.multiple_of(i * tm + BLOCK, BLOCK), 0)),
                  pl.BlockSpec((IN_WIDTH, D_MODEL), lambda i: (0, 0)),
                  pl.BlockSpec((tm, D_MODEL), lambda i: (i, 0)),
                  pl.BlockSpec((tm, D_MODEL), lambda i: (i, 0)),
                  pl.BlockSpec((None, 1, D_MODEL), lambda i: (layer, 0, 0))],
        out_specs=[pl.BlockSpec((tm, D_MODEL), lambda i: (i, 0)),
                   pl.BlockSpec((1, D_MODEL), lambda i: (0, 0))],
        out_shape=[pltpu.HBM((tokens,D_MODEL), F32),
                   pltpu.HBM((1,D_MODEL), F32)],
        compiler_params=pltpu.CompilerParams(dimension_semantics=("arbitrary",),
                                             vmem_limit_bytes=_vmem_limit(blocks)),
    )(_hbm(da), _hbm(dqz), _hbm(dkv), _hbm(w), _hbm(x2), _hbm(g), pre_g3)


def _rel_bias_grad(dbias, col_bucket, flip):
    def body(db_ref, cb_ref, flip_ref, out_ref):
        cb = cb_ref[...]
        anti = flip_ref[...]
        sums = []
        for h in range(B_HEADS):
            x = db_ref[h]
            hi = x.astype(BF16)
            rest = x - hi.astype(F32)
            mid = rest.astype(BF16)
            low = (rest - mid.astype(F32)).astype(BF16)
            reversed_x = _mm(hi, anti) + _mm(mid, anti) + _mm(low, anti)
            rolled = pltpu.roll(reversed_x, 0, 1, stride=1, stride_axis=0)
            sums.append(jnp.sum(rolled, axis=0, keepdims=True))
        per_dist = jnp.concatenate(sums, axis=0)
        lane = lax.broadcasted_iota(jnp.int32, (B_HEADS, LANES), 1)
        out = jnp.zeros((B_HEADS, LANES), F32)
        for b in range(REL_BUCKETS):
            s = jnp.sum(jnp.where(cb == b, per_dist, 0.0), axis=-1, keepdims=True)
            out = jnp.where(lane == b, s, out)
        out_ref[...] = out

    vm = pl.BlockSpec(memory_space=pltpu.VMEM)
    return pl.pallas_call(
        body, name="rel_bias_grad",
        out_shape=jax.ShapeDtypeStruct((B_HEADS, LANES), F32),
        in_specs=[vm, vm, vm], out_specs=vm,
    )(dbias, col_bucket, flip)


def _reduce_adamw(slots, w, m, v, name):
    rows, cols = w.shape
    tr = _row_tile(rows)
    c1 = 1.0 / (1.0 - ADAM_B1 ** ADAM_STEP)
    c2 = 1.0 / (1.0 - ADAM_B2 ** ADAM_STEP)

    def body(s_ref, w_ref, m_ref, v_ref, g_ref, d_ref, nm_ref, nv_ref):
        g = s_ref[0].astype(F32)
        for i in range(1, N_DEV):
            g = g + s_ref[i].astype(F32)
        nm = ADAM_B1 * m_ref[...] + (1.0 - ADAM_B1) * g
        nv = ADAM_B2 * v_ref[...] + (1.0 - ADAM_B2) * (g * g)
        g_ref[...] = g
        nm_ref[...] = nm
        nv_ref[...] = nv
        d_ref[...] = -ADAM_LR * ((nm * c1) / (jnp.sqrt(nv * c2) + ADAM_EPS) + ADAM_WD * w_ref[...])

    blocks = [_nbytes((N_DEV, tr, cols), slots.dtype), 7 * _nbytes((tr, cols), F32)]
    tile = pl.BlockSpec((tr, cols), lambda i: (i, 0))
    return pl.pallas_call(
        body, name=name, grid=(rows // tr,),
        in_specs=[pl.BlockSpec((N_DEV, tr, cols), lambda i: (0, i, 0)), tile, tile, tile],
        out_specs=[tile] * 4,
        out_shape=[pltpu.HBM((rows, cols), F32)] * 4,
        compiler_params=pltpu.CompilerParams(dimension_semantics=("parallel",),
                                             vmem_limit_bytes=_vmem_limit(blocks)),
    )(_hbm(slots), _hbm(w), _hbm(m), _hbm(v))


_SMALL = ("pre_norm_g", "post_norm_g", "ln_v_g", "ln_v_b", "b_spatial", "sinks", "rel_bias", "loss")


def _pack_small(parts):
    slabs = []
    for name in _SMALL:
        flat = parts[name].astype(F32).reshape(-1)
        pad = (-flat.shape[0]) % (8 * LANES)
        slabs.append(jnp.pad(flat, (0, pad)).reshape(-1, LANES))
    return jnp.concatenate(slabs, axis=0)


def _unpack_small(slab, shapes):
    out, row = {}, 0
    for name in _SMALL:
        size = int(np.prod(shapes[name]))
        rows = -(-size // (8 * LANES)) * 8
        out[name] = slab[row:row + rows].reshape(-1)[:size].reshape(shapes[name])
        row += rows
    return out


def kernel(x, pre_norm_g, w_in, ln_v_g, ln_v_b, w_spatial, b_spatial, sinks, rel_bias, w_out, post_norm_g, loss_target, m_pre_norm_g, m_w_in, m_ln_v_g, m_ln_v_b, m_w_spatial, m_b_spatial, m_sinks, m_rel_bias, m_w_out, m_post_norm_g, v_pre_norm_g, v_w_in, v_ln_v_g, v_ln_v_b, v_w_spatial, v_b_spatial, v_sinks, v_rel_bias, v_w_out, v_post_norm_g):
    b_loc, seq, _ = x.shape
    tokens = b_loc * seq
    depth = w_in.shape[0]
    in_shard = w_in.shape[2]
    out_shard = w_out.shape[1]
    assert in_shard * N_DEV == IN_WIDTH and out_shard * N_DEV == D_MODEL and seq % BLOCK == 0

    w_in_t, m_w_in_t, v_w_in_t = (jnp.swapaxes(a, 1, 2) for a in (w_in, m_w_in, v_w_in))
    w_in16, w_out16 = w_in_t.astype(BF16), w_out.astype(BF16)

    def gather_start(layer, not_before):
        zones = [lax.empty((N_DEV, in_shard, D_MODEL), BF16), lax.empty((N_DEV, out_shard, D_MODEL), BF16)]
        sems, sends, zones, pre_g = _exchange_start(
            [w_in16[layer], w_out16[layer]], zones, [0, 0], f"weights_send_{layer}",
            pre_norm_g.reshape(depth, 1, D_MODEL), after=not_before)
        return (sems, sends, [0, 0]), zones, pre_g

    def full_weights(gin, gout):
        return gin.reshape(IN_WIDTH, D_MODEL), gout.reshape(D_MODEL, D_MODEL)

    def gather_wait(layer, started, zones, after):
        return full_weights(*_exchange_wait([started], zones, after, f"weights_wait_{layer}"))

    causal = jnp.tril(jnp.ones((CHUNK, CHUNK), dtype=bool))
    wm = jnp.where(causal, w_spatial, 0.0).astype(BF16)
    wmt = _hbm(jnp.swapaxes(wm, -1, -2))
    wm = _hbm(wm)
    bsb = _hbm(jnp.repeat(jnp.swapaxes(b_spatial, -1, -2), HEAD_DIM, axis=-1))
    pre_g3 = _hbm(pre_norm_g.reshape(depth, 1, D_MODEL))
    post_g3 = _hbm(post_norm_g.reshape(depth, 1, D_MODEL))
    ln_g3 = _hbm(ln_v_g.reshape(depth, 1, A_WIDTH))
    ln_b3 = _hbm(ln_v_b.reshape(depth, 1, A_WIDTH))
    bucket = jnp.asarray(_bucket_table())
    bias = _bias_table(rel_bias, bucket)

    xs, saved, weights = [x.reshape(tokens, D_MODEL)], [], []
    pending = None
    for layer in range(depth):
        if layer == 0:
            w, wo = full_weights(*_all_gather([w_in16[0], w_out16[0]], "weights_gather_0"))
        else:
            w, wo = gather_wait(layer, pending[0], pending[1], xs[-1])
        pre_g_fwd = pre_g3
        if layer + 1 < depth:
            pending = gather_start(layer + 1, w)
            pre_g_fwd = _hbm(pending[2])
        h, pa, pb, ya, kept, rstd_w = _inproj_gmlp_fwd(xs[-1], pre_g_fwd, w, ln_g3, ln_b3, wm, bsb, layer)
        if layer + 1 < depth:
            yb, ob, pw, spw, y, x_next = _attn_outproj_fwd(pb, bias, sinks, ya, wo, xs[-1], post_g3, b_loc, layer)
            xs.append(x_next)
        else:
            yb, ob, pw, spw, y, g, loss_part = _attn_outproj_fwd(pb, bias, sinks, ya, wo, xs[-1], post_g3, b_loc, layer,
                                                    target=loss_target.reshape(tokens, D_MODEL))
        saved.append((h, pa, pb, ya, yb, y, ob, pw, spw, kept, rstd_w))
        weights.append((w, wo))

    grads = {name: [None] * depth for name in ("pre_norm_g", "post_norm_g", "ln_v_g", "ln_v_b", "b_spatial", "sinks")}
    zone_in = lax.empty((N_DEV, depth * in_shard, D_MODEL), BF16)
    zone_out = lax.empty((N_DEV, depth * out_shard, D_MODEL), BF16)
    zone_ws = lax.empty((N_DEV, depth * A_GROUPS * CHUNK, CHUNK), F32)
    started_in, started_out = [], []
    dbias = jnp.zeros((B_HEADS, BLOCK, BLOCK), F32)
    for layer in reversed(range(depth)):
        h, pa, pb, ya, yb, y, ob, pw, spw, kept, rstd_w = saved[layer]
        w, wo = weights[layer]
        dya, dyb, dwo, dpost = _outproj_bwd(g, y, ya, yb, wo, post_g3, layer)
        send_out = dwo.reshape(N_DEV, out_shard, D_MODEL)
        sems, sends, (zone_out,), ln_g_bwd = _exchange_start(
            [send_out], [zone_out], [layer * out_shard], f"grads_send_out_{layer}", ln_v_g.reshape(depth, 1, A_WIDTH))
        started_out.append((sems, sends, [layer * out_shard]))
        dqz, dkv, dbias, dsink = _attn_bwd(pb, dyb, ob, pw, spw, dbias, layer)
        da, dw, dws, dbs, dlg, dlb = _gmlp_bwd_dw(pa, kept, rstd_w, dya, h, dqz, dkv, _hbm(ln_g_bwd), ln_b3, wmt, layer)
        send_in = dw.reshape(N_DEV, in_shard, D_MODEL)
        row_offs = [layer * A_GROUPS * CHUNK, layer * in_shard]
        sems, sends, (zone_ws, zone_in), pre_g_bwd = _exchange_start(
            [dws.reshape(A_GROUPS * CHUNK, CHUNK), send_in], [zone_ws, zone_in], row_offs, f"grads_send_in_{layer}",
            pre_norm_g.reshape(depth, 1, D_MODEL))
        started_in.append((sems, sends, row_offs))
        g, dpre = _inproj_bwd_dx(da, dqz, dkv, w, xs[layer], g, _hbm(pre_g_bwd), layer)
        grads["b_spatial"][layer] = dbs[:, :A_GROUPS].T
        grads["ln_v_g"][layer] = dlg[0]
        grads["ln_v_b"][layer] = dlb[0]
        grads["sinks"][layer] = dsink[0, :B_HEADS]
        grads["pre_norm_g"][layer] = dpre[0]
        grads["post_norm_g"][layer] = dpost[0]
    grad_x = g.reshape(x.shape)
    col_bucket = jnp.asarray(np.broadcast_to(_bucket_table()[0:1, ::-1], (B_HEADS, BLOCK)))
    flip = jnp.asarray(np.eye(BLOCK, dtype=np.float32)[::-1], dtype=BF16)
    drel = _rel_bias_grad(dbias, col_bucket, flip)[:, :REL_BUCKETS].T

    (recv_out,) = _exchange_wait(started_out, [zone_out], g, "grads_wait_out")
    res_out = _reduce_adamw(recv_out, w_out.reshape(-1, D_MODEL), m_w_out.reshape(-1, D_MODEL),
                            v_w_out.reshape(-1, D_MODEL), "adamw_w_out")
    res_out = [r.reshape(w_out.shape) for r in res_out]

    no_state = jnp.zeros((1,), F32)
    small_w = dict(pre_norm_g=pre_norm_g, post_norm_g=post_norm_g, ln_v_g=ln_v_g,
                   ln_v_b=ln_v_b, b_spatial=b_spatial, sinks=sinks, rel_bias=rel_bias, loss=no_state)
    small_m = dict(pre_norm_g=m_pre_norm_g, post_norm_g=m_post_norm_g, ln_v_g=m_ln_v_g,
                   ln_v_b=m_ln_v_b, b_spatial=m_b_spatial, sinks=m_sinks, rel_bias=m_rel_bias, loss=no_state)
    small_v = dict(pre_norm_g=v_pre_norm_g, post_norm_g=v_post_norm_g, ln_v_g=v_ln_v_g,
                   ln_v_b=v_ln_v_b, b_spatial=v_b_spatial, sinks=v_sinks, rel_bias=v_rel_bias, loss=no_state)
    small_g = {name: jnp.stack(grads[name]) for name in _SMALL if name not in ("rel_bias", "loss")}
    small_g["rel_bias"] = drel
    small_g["loss"] = loss_part[0, :1]
    shapes = {name: small_w[name].shape for name in _SMALL}
    slots = _all_gather_direct(_pack_small(small_g), "small_grads_all_gather")
    res_small = _reduce_adamw(slots, _pack_small(small_w), _pack_small(small_m), _pack_small(small_v), "adamw_small")

    recv_ws, recv_in = _exchange_wait(started_in, [zone_ws, zone_in], res_small[0], "grads_wait_in")
    res_ws = _reduce_adamw(recv_ws, w_spatial.reshape(-1, CHUNK), m_w_spatial.reshape(-1, CHUNK),
                           v_w_spatial.reshape(-1, CHUNK), "adamw_w_spatial")
    res_ws = [r.reshape(w_spatial.shape) for r in res_ws]
    res_in = _reduce_adamw(recv_in, w_in_t.reshape(-1, D_MODEL), m_w_in_t.reshape(-1, D_MODEL),
                           v_w_in_t.reshape(-1, D_MODEL), "adamw_w_in")
    res_in = [jnp.swapaxes(r.reshape(w_in_t.shape), 1, 2) for r in res_in]
    res_small = [_unpack_small(r, shapes) for r in res_small]

    order = ("pre_norm_g", "w_in", "ln_v_g", "ln_v_b", "w_spatial", "b_spatial", "sinks", "rel_bias", "w_out",
             "post_norm_g")
    outs = [res_small[0]["loss"][0], grad_x]
    for kind in range(4):
        for name in order:
            if name == "w_in":
                outs.append(res_in[kind])
            elif name == "w_out":
                outs.append(res_out[kind])
            elif name == "w_spatial":
                outs.append(res_ws[kind])
            else:
                outs.append(res_small[kind][name])
    return tuple(outs)
```

```python
import math

import numpy as np
import jax
import jax.numpy as jnp
from jax import lax
from jax.experimental import pallas as pl
from jax.experimental.pallas import tpu as pltpu

F32 = jnp.float32
BF16 = jnp.bfloat16

D_MODEL = 1024
A_WIDTH = 512
A_GROUPS = 8
CHUNK = 128
B_HEADS = 8
HEAD_DIM = 64
B_WIDTH = 512
KV_WIDTH = 128
BLOCK = 128
REL_BUCKETS = 32
REL_MAX_DIST = 128
NORM_EPS = 1e-6
PA_WIDTH = 3 * A_WIDTH
PB_WIDTH = 2 * B_WIDTH + 2 * KV_WIDTH
IN_WIDTH = PA_WIDTH + PB_WIDTH
K_OFF, BZ_OFF = B_WIDTH, B_WIDTH + 2 * KV_WIDTH
KEPT_WIDTH = 5 * A_WIDTH
SCALE = HEAD_DIM ** -0.5
NEG = -1e30
N_DEV = 8
LANES = 128

ADAM_LR = 0.001
ADAM_B1 = 0.9
ADAM_B2 = 0.999
ADAM_EPS = 1e-08
ADAM_WD = 0.01
ADAM_STEP = 10

V7X_VMEM_BYTES = 64 * 1024 * 1024
VMEM_TEMP_BYTES = 12 * 1024 * 1024
MESH = pl.DeviceIdType.MESH


def _vmem_limit(block_bytes, scratch_bytes=0):
    need = 2 * sum(block_bytes) + scratch_bytes + VMEM_TEMP_BYTES
    return int(min(need, V7X_VMEM_BYTES - 8 * 1024 * 1024))


def _nbytes(shape, dtype):
    return int(np.prod(shape)) * jnp.dtype(dtype).itemsize


def _token_tile(tokens, cap=512):
    tile = min(cap, tokens // 2)
    assert tokens % tile == 0 and tile % CHUNK == 0, tokens
    return tile


def _row_tile(rows, cap=512):
    best = 8
    for t in range(8, cap + 1, 8):
        if rows % t == 0:
            best = t
    assert rows % best == 0, rows
    return best


def _mm(a, b):
    return lax.dot_general(a, b, (((1,), (0,)), ((), ())), preferred_element_type=F32)


def _mm_nt(a, b):
    return lax.dot_general(a, b, (((1,), (1,)), ((), ())), preferred_element_type=F32)


def _mm_tn(a, b):
    return lax.dot_general(a, b, (((0,), (0,)), ((), ())), preferred_element_type=F32)


_GELU_C = math.sqrt(2.0 / math.pi)


_GELU_A = _GELU_C * 0.044715


def _gelu_parts(x):
    x2 = x * x
    t = jnp.tanh(x * (_GELU_C + _GELU_A * x2))
    return x2, t, 0.5 + 0.5 * t


def _gelu_and_grad(x):
    x2, t, half_plus = _gelu_parts(x)
    grad = half_plus + (0.5 * x) * (1.0 - t * t) * (_GELU_C + (3.0 * _GELU_A) * x2)
    return x * half_plus, grad


def _sigmoid(x):
    return 0.5 + 0.5 * jnp.tanh(0.5 * x)


def _bucket_table():
    q = np.arange(BLOCK)[:, None]
    k = np.arange(BLOCK)[None, :]
    dist = np.where(k <= q, q - k, q + BLOCK - k)
    max_exact = REL_BUCKETS // 2
    safe = np.maximum(dist, 1).astype(np.float32)
    large = max_exact + (np.log(safe / np.float32(max_exact)) / np.float32(math.log(REL_MAX_DIST / max_exact))
                         * np.float32(REL_BUCKETS - max_exact)).astype(np.int32)
    large = np.minimum(large, REL_BUCKETS - 1)
    assert dist.min() >= 0 and dist.max() < BLOCK
    return np.where(dist < max_exact, dist, large).astype(np.int32)


def _hbm(x):
    return pltpu.with_memory_space_constraint(x, pltpu.HBM)


def _slot(px, py, pc):
    return 4 * px + 2 * py + pc


def _all_gather(arrs, name):
    n = len(arrs)

    def body(*refs):
        ins, outs = refs[:n], refs[n:2 * n]
        send_sems, recv_sems, local_sems = refs[2 * n:]
        x, y, c = lax.axis_index("x"), lax.axis_index("y"), lax.axis_index("c")
        me, sibling = (x, y, c), (x, y, 1 - c)
        chips = [(1 - x, y), (x, 1 - y), (1 - x, 1 - y)]

        def copy(a, k, block, to, src=None):
            dst = outs[a].at[_slot(*block)]
            return pltpu.make_async_remote_copy(
                src_ref=dst if src is None else src, dst_ref=dst,
                send_sem=send_sems.at[7 * a + k], recv_sem=recv_sems.at[7 * a + k],
                device_id=to, device_id_type=MESH)

        mine = [pltpu.make_async_copy(ins[a], outs[a].at[_slot(*me)], local_sems.at[a]) for a in range(n)]
        for cp in mine:
            cp.start()
        first = []
        for a in range(n):
            first.append(copy(a, 0, me, sibling, src=ins[a]))
            first += [copy(a, 1 + j, me, (*chip, c), src=ins[a]) for j, chip in enumerate(chips)]
        for cp in first:
            cp.start()
        passed = []
        for j, chip in enumerate(chips):
            for a in range(n):
                copy(a, 1 + j, (*chip, c), me).wait_recv()
                fwd = copy(a, 4 + j, (*chip, c), sibling)
                fwd.start()
                passed.append(fwd)
        for a in range(n):
            copy(a, 0, sibling, me).wait_recv()
            for j, chip in enumerate(chips):
                copy(a, 4 + j, (*chip, 1 - c), me).wait_recv()
        for cp in first + passed:
            cp.wait_send()
        for cp in mine:
            cp.wait()

    any_spec = pl.BlockSpec(memory_space=pl.ANY)
    return pl.pallas_call(
        body, name=name,
        out_shape=[jax.ShapeDtypeStruct((N_DEV,) + a.shape, a.dtype) for a in arrs],
        in_specs=[any_spec] * n, out_specs=[any_spec] * n,
        scratch_shapes=[pltpu.SemaphoreType.DMA((7 * n,)), pltpu.SemaphoreType.DMA((7 * n,)),
                        pltpu.SemaphoreType.DMA((n,))],
    )(*arrs)


def _all_gather_direct(arr, name):
    def body(in_ref, out_ref, send_sems, recv_sems, local_sem):
        me, peers = _peers()
        mine = pltpu.make_async_copy(in_ref, out_ref.at[_slot(*me)], local_sem)
        mine.start()

        def copy(k, origin, to):
            return pltpu.make_async_remote_copy(
                src_ref=in_ref, dst_ref=out_ref.at[_slot(*origin)], send_sem=send_sems.at[k],
                recv_sem=recv_sems.at[k], device_id=to, device_id_type=MESH)

        sends = [copy(k, me, peer) for k, peer in enumerate(peers)]
        for cp in sends:
            cp.start()
        for k, peer in enumerate(peers):
            copy(k, peer, me).wait_recv()
        for cp in sends:
            cp.wait_send()
        mine.wait()

    any_spec = pl.BlockSpec(memory_space=pl.ANY)
    return pl.pallas_call(
        body, name=name,
        out_shape=jax.ShapeDtypeStruct((N_DEV,) + arr.shape, arr.dtype),
        in_specs=[any_spec], out_specs=any_spec,
        scratch_shapes=[pltpu.SemaphoreType.DMA((7,)), pltpu.SemaphoreType.DMA((7,)), pltpu.SemaphoreType.DMA],
    )(arr)


_HBM_SPEC = pl.BlockSpec(memory_space=pltpu.HBM)
_SEM_SPEC = pl.BlockSpec(memory_space=pltpu.SEMAPHORE)
_DATAFLOW = pltpu.SideEffectType.DATAFLOW_SIDE_EFFECTING


def _peers():
    x, y, c = lax.axis_index("x"), lax.axis_index("y"), lax.axis_index("c")
    peers = []
    for k in range(1, N_DEV):
        fx, fy, fc = (k >> 2) & 1, (k >> 1) & 1, k & 1
        peers.append((1 - x if fx else x, 1 - y if fy else y, 1 - c if fc else c))
    return (x, y, c), peers


def _exchange_copy(send_ref, land_ref, row_off, src_slot, dst_slot, sems, idx, peer):
    rows = send_ref.shape[-2]
    src = send_ref.at[src_slot] if len(send_ref.shape) == 3 else send_ref
    return pltpu.make_async_remote_copy(
        src_ref=src, dst_ref=land_ref.at[dst_slot, pl.ds(row_off, rows), :],
        send_sem=sems[0].at[idx], recv_sem=sems[1].at[idx], device_id=peer, device_id_type=MESH)


def _own_copy(send_ref, land_ref, row_off, slot, sem):
    rows = send_ref.shape[-2]
    src = send_ref.at[slot] if len(send_ref.shape) == 3 else send_ref
    return pltpu.make_async_copy(src, land_ref.at[slot, pl.ds(row_off, rows), :], sem)


def _exchange_start(sends, lands, row_offs, name, carry, after=None):
    n = len(sends)

    def body(*refs):
        ins, zones, carry_ref = refs[:n], refs[n:2 * n], refs[2 * n]
        first_out = 2 * n + 1 + (after is not None)
        sems = refs[first_out:first_out + 3]
        carry_out = refs[-1]
        me, peers = _peers()
        for a in range(n):
            for k, peer in enumerate(peers):
                _exchange_copy(ins[a], zones[a], row_offs[a], _slot(*peer), _slot(*me), sems, 7 * a + k, peer).start()
            _own_copy(ins[a], zones[a], row_offs[a], _slot(*me), sems[2].at[a]).start()
        carry_out[...] = carry_ref[...]

    arrays = [_hbm(a) for a in list(sends) + list(lands)]
    vmem = pl.BlockSpec(memory_space=pltpu.VMEM)
    out = pl.pallas_call(
        body, name=name,
        out_shape=(pltpu.SemaphoreType.DMA((7 * n,)), pltpu.SemaphoreType.DMA((7 * n,)), pltpu.SemaphoreType.DMA((n,)),
                   *[pltpu.HBM(a.shape, a.dtype) for a in arrays], jax.ShapeDtypeStruct(carry.shape, carry.dtype)),
        in_specs=[_HBM_SPEC] * (2 * n) + [vmem] + ([pl.BlockSpec(memory_space=pl.ANY)] if after is not None else []),
        out_specs=(_SEM_SPEC, _SEM_SPEC, _SEM_SPEC, *[_HBM_SPEC] * (2 * n), vmem),
        input_output_aliases={i: 3 + i for i in range(2 * n)},
        compiler_params=pltpu.CompilerParams(has_side_effects=_DATAFLOW),
    )(*arrays, carry, *([after] if after is not None else []))
    return (out[0], out[1], out[2]), list(out[3:3 + n]), list(out[3 + n:3 + 2 * n]), out[-1]


def _exchange_wait(started, lands, after, name):
    n = len(lands)
    flat_sends = [s for _, sends, _ in started for s in sends]
    flat_sems = [s for sems, _, _ in started for s in sems]
    ns = len(flat_sends)

    def body(*refs):
        ins, zones = refs[:ns], refs[ns:ns + n]
        sem_refs = refs[ns + n:ns + n + len(flat_sems)]
        me, peers = _peers()
        pos = 0
        for call, (_, sends, row_offs) in enumerate(started):
            sems = sem_refs[3 * call:3 * call + 3]
            for a in range(len(sends)):
                for k, peer in enumerate(peers):
                    cp = _exchange_copy(ins[pos + a], zones[a], row_offs[a], _slot(*peer), _slot(*peer), sems,
                                        7 * a + k, peer)
                    cp.wait_send()
                    cp.wait_recv()
                _own_copy(ins[pos + a], zones[a], row_offs[a], _slot(*me), sems[2].at[a]).wait()
            pos += len(sends)

    arrays = list(flat_sends) + list(lands)
    out = pl.pallas_call(
        body, name=name,
        out_shape=tuple(pltpu.HBM(a.shape, a.dtype) for a in arrays),
        in_specs=[_HBM_SPEC] * len(arrays) + [_SEM_SPEC] * len(flat_sems) + [pl.BlockSpec(memory_space=pl.ANY)],
        out_specs=tuple([_HBM_SPEC] * len(arrays)),
        input_output_aliases={i: i for i in range(len(arrays))},
        compiler_params=pltpu.CompilerParams(has_side_effects=_DATAFLOW),
    )(*arrays, *flat_sems, after)
    return list(out[ns:])


def _bias_table(rel_bias, bucket):
    def body(rel_ref, bucket_ref, out_ref):
        bk = bucket_ref[...]
        for h in range(B_HEADS):
            def pick(b, acc, h=h):
                return jnp.where(bk == b, rel_ref[b, h], acc)
            out_ref[h] = lax.fori_loop(0, REL_BUCKETS, pick, jnp.zeros((BLOCK, BLOCK), F32))

    return pl.pallas_call(
        body, name="bias_table",
        out_shape=jax.ShapeDtypeStruct((B_HEADS, BLOCK, BLOCK), F32),
        in_specs=[pl.BlockSpec(memory_space=pltpu.SMEM), pl.BlockSpec(memory_space=pltpu.VMEM)],
        out_specs=pl.BlockSpec(memory_space=pltpu.VMEM),
    )(rel_bias, bucket)


def _inproj_gmlp_fwd(x2, pre_g3, w, ln_g3, ln_b3, wm, bsb, layer):
    tokens = x2.shape[0]
    tm = _token_tile(tokens)

    def body(x_ref, g_ref, w_ref, lg_ref, lb_ref, wm_ref, bsb_ref, h_ref, pa_ref, pb_ref, ya_ref, kept_ref, rstd_ref):
        x = x_ref[...]
        r = lax.rsqrt(jnp.mean(x * x, axis=-1, keepdims=True) + NORM_EPS)
        h = (x * r * g_ref[...]).astype(BF16)
        h_ref[...] = h
        pa_ref[...] = _mm_nt(h, w_ref[0:PA_WIDTH, :]).astype(BF16)
        pb_ref[...] = _mm_nt(h, w_ref[PA_WIDTH:IN_WIDTH, :]).astype(BF16)
        for ci in range(tm // CHUNK):
            rows = slice(ci * CHUNK, (ci + 1) * CHUNK)
            gu, gv, pz, u, vhat, rstd, _, mixed, sg = _gmlp_forward_chunk(
                pa_ref[rows, :], lg_ref[...], lb_ref[...], wm_ref, bsb_ref[...])
            ya_ref[rows, :] = (u * mixed * (pz * sg)).astype(BF16)
            for k, val in enumerate((u, gu, gv, vhat, mixed)):
                kept_ref[rows, A_WIDTH * k:A_WIDTH * (k + 1)] = val.astype(BF16)
            rstd_ref[rows, :] = jnp.broadcast_to(rstd, (CHUNK, LANES))

    blocks = [_nbytes((tm, D_MODEL), F32), _nbytes((D_MODEL, IN_WIDTH), BF16),
              _nbytes((tm, D_MODEL), BF16), _nbytes((tm, PA_WIDTH), BF16), _nbytes((tm, PB_WIDTH), BF16),
              _nbytes((A_GROUPS, CHUNK, CHUNK), BF16), _nbytes((CHUNK, A_WIDTH), F32), _nbytes((tm, A_WIDTH), BF16)]
    return pl.pallas_call(
        body, name=f"inproj_gmlp_fwd_{layer}", grid=(tokens // tm,),
        in_specs=[pl.BlockSpec((tm, D_MODEL), lambda i: (i, 0)),
                  pl.BlockSpec((None, 1, D_MODEL), lambda i: (layer, 0, 0)),
                  pl.BlockSpec((IN_WIDTH, D_MODEL), lambda i: (0, 0)),
                  pl.BlockSpec((None, 1, A_WIDTH), lambda i: (layer, 0, 0)),
                  pl.BlockSpec((None, 1, A_WIDTH), lambda i: (layer, 0, 0)),
                  pl.BlockSpec((None, A_GROUPS, CHUNK, CHUNK), lambda i: (layer, 0, 0, 0)),
                  pl.BlockSpec((None, CHUNK, A_WIDTH), lambda i: (layer, 0, 0))],
        out_specs=[pl.BlockSpec((tm, D_MODEL), lambda i: (i, 0)),
                   pl.BlockSpec((tm, PA_WIDTH), lambda i: (i, 0)),
                   pl.BlockSpec((tm, PB_WIDTH), lambda i: (i, 0)),
                   pl.BlockSpec((tm, A_WIDTH), lambda i: (i, 0)),
                   pl.BlockSpec((tm, KEPT_WIDTH), lambda i: (i, 0)),
                   pl.BlockSpec((tm, LANES), lambda i: (i, 0))],
        out_shape=[pltpu.HBM((tokens, D_MODEL), BF16),
                   pltpu.HBM((tokens, PA_WIDTH), BF16),
                   pltpu.HBM((tokens, PB_WIDTH), BF16),
                   pltpu.HBM((tokens, A_WIDTH), BF16),
                   pltpu.HBM((tokens, KEPT_WIDTH), BF16),
                   pltpu.HBM((tokens, LANES), F32)],
        compiler_params=pltpu.CompilerParams(dimension_semantics=("parallel",),
                                             vmem_limit_bytes=_vmem_limit(blocks + [_nbytes((tm, KEPT_WIDTH), BF16),
                                                                                    _nbytes((tm, LANES), F32)])),
    )(_hbm(x2), pre_g3, _hbm(w), ln_g3, ln_b3, wm, bsb)


def _gmlp_forward_chunk(pa, ln_g, ln_b, wm_ref, bsb):
    pu = pa[:, 0:A_WIDTH].astype(F32)
    pv = pa[:, A_WIDTH:2 * A_WIDTH].astype(F32)
    pz = pa[:, 2 * A_WIDTH:3 * A_WIDTH].astype(F32)
    u, gu = _gelu_and_grad(pu)
    vv, gv = _gelu_and_grad(pv)
    mu = jnp.mean(vv, axis=-1, keepdims=True)
    xc = vv - mu
    rstd = lax.rsqrt(jnp.mean(xc * xc, axis=-1, keepdims=True) + NORM_EPS)
    vhat = xc * rstd
    vnb = (vhat * ln_g + ln_b).astype(BF16)
    low = lax.broadcasted_iota(jnp.int32, (CHUNK, LANES), 1) < HEAD_DIM
    parts = []
    for p in range(A_GROUPS // 2):
        vp = vnb[:, LANES * p:LANES * (p + 1)]
        parts.append(jnp.where(low, _mm(wm_ref[2 * p], vp), _mm(wm_ref[2 * p + 1], vp)))
    mixed = jnp.concatenate(parts, axis=1) + bsb
    sg = _sigmoid(pz)
    return gu, gv, pz, u, vhat, rstd, vnb, mixed, sg


def _kv_variants(kv):
    t = kv.astype(F32)
    rolled = pltpu.roll(t, HEAD_DIM, 1)
    low = lax.broadcasted_iota(jnp.int32, t.shape, 1) < HEAD_DIM
    zero = jnp.zeros_like(t)
    head0 = (jnp.where(low, t, zero).astype(BF16), jnp.where(low, zero, rolled).astype(BF16))
    head1 = (jnp.where(low, rolled, zero).astype(BF16), jnp.where(low, zero, t).astype(BF16))
    return head0, head1


def _band_masks():
    row = lax.broadcasted_iota(jnp.int32, (BLOCK, BLOCK), 0)
    col = lax.broadcasted_iota(jnp.int32, (BLOCK, BLOCK), 1)
    return col <= row


def _wrap(full, tri):
    return jnp.where(tri, full[:, BLOCK:2 * BLOCK], full[:, 0:BLOCK])


def _attn_probs(sf, bias_h, sink, tri, kill):
    s = _wrap(sf, tri) + bias_h
    s = jnp.where(kill, NEG, s)
    m = jnp.maximum(jnp.max(s, axis=-1, keepdims=True), sink)
    e = jnp.exp(s - m)
    es = jnp.exp(sink - m)
    inv = 1.0 / (jnp.sum(e, axis=-1, keepdims=True) + es)
    return e * inv, es * inv


def _unwrap16(p, tri):
    p = p.astype(BF16)
    zero = jnp.zeros_like(p)
    return jnp.concatenate([jnp.where(tri, zero, p), jnp.where(tri, p, zero)], axis=1)


def _fill_kv(kv_ref, prev_ref, cur_ref):
    kv_ref[0:BLOCK, :] = prev_ref[...]
    kv_ref[BLOCK:, :] = cur_ref[:, K_OFF:K_OFF + 2 * KV_WIDTH]


def _attn_outproj_fwd(pb, bias, sinks, ya, wo, x2, post_g3, b_loc, layer, target=None):
    tokens = pb.shape[0]
    nb = tokens // b_loc // BLOCK
    tq = _token_tile(tokens)
    per_tile = tq // BLOCK
    steps = tokens // tq
    with_loss = target is not None
    half = D_MODEL // 2

    def body(*refs):
        sink_ref, cur_ref, prev_ref, bias_ref, ya_ref, woa_ref, wob_ref, x_ref, g_ref = refs[:9]
        if with_loss:
            t_ref, yb_ref, ob_ref, pw_ref, spw_ref, y_ref, grad_ref, loss_ref, kv_ref, acc_ref = refs[9:]
        else:
            yb_ref, ob_ref, pw_ref, spw_ref, y_ref, xn_ref, kv_ref = refs[9:]
        t = pl.program_id(0)
        _fill_kv(kv_ref, prev_ref, cur_ref)
        tri = _band_masks()
        head_lane = lax.broadcasted_iota(jnp.int32, (BLOCK, LANES), 1)

        def block(i, carry):
            start = i * BLOCK
            rows = pl.ds(start, BLOCK)
            first = lax.rem(t * per_tile + i, nb) == 0
            kill = jnp.logical_and(first, jnp.logical_not(tri))
            kv = kv_ref[pl.ds(start, 2 * BLOCK), :]
            k_ops = _kv_variants(kv[:, 0:KV_WIDTH])
            v_ops = _kv_variants(kv[:, KV_WIDTH:2 * KV_WIDTH])
            p16 = {}
            sink_cols = jnp.zeros((BLOCK, LANES), F32)
            for kvh in range(2):
                pairs = (2 * kvh, 2 * kvh + 1)
                qs = jnp.concatenate([cur_ref[rows, LANES * p:LANES * (p + 1)] for p in pairs], axis=0) * SCALE
                for j in range(2):
                    sf = _mm_nt(qs, k_ops[kvh][j])
                    for r, p in enumerate(pairs):
                        hd = 2 * p + j
                        probs, sink_p = _attn_probs(sf[BLOCK * r:BLOCK * (r + 1)], bias_ref[hd],
                                                    sink_ref[layer, hd], tri, kill)
                        probs = probs.astype(BF16)
                        p16[hd] = _unwrap16(probs, tri)
                        pw_ref[rows, LANES * hd:LANES * (hd + 1)] = probs
                        sink_cols = jnp.where(head_lane == hd, sink_p, sink_cols)
            spw_ref[rows, :] = sink_cols
            for kvh in range(2):
                pairs = (2 * kvh, 2 * kvh + 1)
                out = jnp.zeros((2 * BLOCK, LANES), F32)
                for j in range(2):
                    out = out + _mm(jnp.concatenate([p16[2 * p + j] for p in pairs], axis=0), v_ops[kvh][j])
                for r, p in enumerate(pairs):
                    bz = cur_ref[rows, BZ_OFF + LANES * p:BZ_OFF + LANES * (p + 1)].astype(F32)
                    out_p = out[BLOCK * r:BLOCK * (r + 1)]
                    ob_ref[rows, LANES * p:LANES * (p + 1)] = out_p.astype(BF16)
                    yb_ref[rows, LANES * p:LANES * (p + 1)] = (out_p * (bz * _sigmoid(bz))).astype(BF16)
            y = _mm(ya_ref[rows, :], woa_ref[...]) + _mm(yb_ref[rows, :], wob_ref[...])
            r = lax.rsqrt(jnp.mean(y * y, axis=-1, keepdims=True) + NORM_EPS)
            x_next = x_ref[rows, :] + y * r * g_ref[...]
            y_ref[rows, :] = y.astype(BF16)
            if not with_loss:
                xn_ref[rows, :] = x_next
            else:
                err = x_next - t_ref[rows, :]
                grad_ref[rows, :] = err * (1.0 / D_MODEL)
                acc_ref[...] += jnp.sum(err * err, axis=0, keepdims=True)
            return carry

        if with_loss:
            @pl.when(t == 0)
            def _():
                acc_ref[...] = jnp.zeros_like(acc_ref)

        for i in range(per_tile):
            block(i, 0)

        if not with_loss:
            return

        @pl.when(t == steps - 1)
        def _():
            total = jnp.sum(acc_ref[...], axis=-1, keepdims=True) * (0.5 / D_MODEL)
            loss_ref[...] = jnp.broadcast_to(total, loss_ref.shape)

    tile = pl.BlockSpec((tq, D_MODEL), lambda t: (t, 0))
    blocks = [_nbytes((tq, PB_WIDTH), BF16), _nbytes((BLOCK, 2 * KV_WIDTH), BF16),
              _nbytes((B_HEADS, BLOCK, BLOCK), F32), 2 * _nbytes((tq, B_WIDTH), BF16),
              2 * _nbytes((half, D_MODEL), BF16), 3 * _nbytes((tq, D_MODEL), F32), _nbytes((tq, D_MODEL), BF16),
              _nbytes((tq, B_HEADS * BLOCK), BF16), _nbytes((tq, LANES), F32)]
    scratch = _nbytes((tq + BLOCK, 2 * KV_WIDTH), BF16)
    in_specs = [pl.BlockSpec(memory_space=pltpu.SMEM),
                pl.BlockSpec((tq, PB_WIDTH), lambda t: (t, 0)),
                pl.BlockSpec((BLOCK, 2 * KV_WIDTH),
                             lambda t: (jnp.maximum(t * per_tile - 1, 0), K_OFF // (2 * KV_WIDTH))),
                pl.BlockSpec((B_HEADS, BLOCK, BLOCK), lambda t: (0, 0, 0)),
                pl.BlockSpec((tq, half), lambda t: (t, 0)),
                pl.BlockSpec((half, D_MODEL), lambda t: (0, 0)),
                pl.BlockSpec((half, D_MODEL), lambda t: (1, 0)),
                tile,
                pl.BlockSpec((None, 1, D_MODEL), lambda t: (layer, 0, 0))]
    out_specs = [pl.BlockSpec((tq, B_WIDTH), lambda t: (t, 0)), pl.BlockSpec((tq, B_WIDTH), lambda t: (t, 0)),
                 pl.BlockSpec((tq, B_HEADS * BLOCK), lambda t: (t, 0)),
                 pl.BlockSpec((tq, LANES), lambda t: (t, 0)), tile, tile]
    out_shape = [pltpu.HBM((tokens, B_WIDTH), BF16), pltpu.HBM((tokens, B_WIDTH), BF16),
                 pltpu.HBM((tokens, B_HEADS * BLOCK), BF16),
                 pltpu.HBM((tokens, LANES), F32), pltpu.HBM((tokens, D_MODEL), BF16),
                 pltpu.HBM((tokens, D_MODEL), F32)]
    scratch_shapes = [pltpu.VMEM((tq + BLOCK, 2 * KV_WIDTH), BF16)]
    operands = [sinks, _hbm(pb), _hbm(pb), _hbm(bias), _hbm(ya), _hbm(wo), _hbm(wo), _hbm(x2), post_g3]
    if with_loss:
        in_specs.append(tile)
        out_specs.append(pl.BlockSpec((1, LANES), lambda t: (0, 0)))
        out_shape.append(pltpu.HBM((1, LANES), F32))
        scratch_shapes.append(pltpu.VMEM((1, D_MODEL), F32))
        operands.append(_hbm(target))
    return pl.pallas_call(
        body, name=f"attn_outproj_fwd_{layer}", grid=(steps,),
        in_specs=in_specs, out_specs=out_specs, out_shape=out_shape, scratch_shapes=scratch_shapes,
        compiler_params=pltpu.CompilerParams(dimension_semantics=("arbitrary" if with_loss else "parallel",),
                                             vmem_limit_bytes=_vmem_limit(blocks, scratch)),
    )(*operands)


def _outproj_bwd(g, y, ya, yb, wo, post_g3, layer):
    tokens = g.shape[0]
    tm = _token_tile(tokens, cap=1024)
    half = D_MODEL // 2
    steps = tokens // tm

    def body(g_ref, y_ref, ya_ref, yb_ref, woa_ref, wob_ref, pg_ref, dya_ref, dyb_ref, dwo16_ref, dpg_ref, dwo_ref):
        @pl.when(pl.program_id(0) == 0)
        def _():
            dwo_ref[...] = jnp.zeros_like(dwo_ref)
            dpg_ref[...] = jnp.zeros_like(dpg_ref)

        gv = g_ref[...]
        yf = y_ref[...].astype(F32)
        r = lax.rsqrt(jnp.mean(yf * yf, axis=-1, keepdims=True) + NORM_EPS)
        yhat = yf * r
        gg = gv * pg_ref[...]
        dy = r * (gg - yhat * jnp.mean(gg * yhat, axis=-1, keepdims=True))
        dpg_ref[...] += jnp.sum(gv * yhat, axis=0, keepdims=True)
        dyb16 = dy.astype(BF16)
        dya_ref[...] = _mm_nt(dyb16, woa_ref[...]).astype(BF16)
        dyb_ref[...] = _mm_nt(dyb16, wob_ref[...]).astype(BF16)
        dwo_ref[0:half, :] += _mm_tn(ya_ref[...], dyb16)
        dwo_ref[half:D_MODEL, :] += _mm_tn(yb_ref[...], dyb16)

        @pl.when(pl.program_id(0) == steps - 1)
        def _():
            dwo16_ref[...] = dwo_ref[...].astype(BF16)

    blocks = [_nbytes((tm, D_MODEL), F32), _nbytes((tm, D_MODEL), BF16), 4 * _nbytes((tm, half), BF16),
              2 * _nbytes((half, D_MODEL), BF16), _nbytes((D_MODEL, D_MODEL), BF16)]
    return pl.pallas_call(
        body, name=f"outproj_bwd_{layer}", grid=(tokens // tm,),
        in_specs=[pl.BlockSpec((tm, D_MODEL), lambda i: (i, 0)),
                  pl.BlockSpec((tm, D_MODEL), lambda i: (i, 0)),
                  pl.BlockSpec((tm, half), lambda i: (i, 0)),
                  pl.BlockSpec((tm, half), lambda i: (i, 0)),
                  pl.BlockSpec((half, D_MODEL), lambda i: (0, 0)),
                  pl.BlockSpec((half, D_MODEL), lambda i: (1, 0)),
                  pl.BlockSpec((None, 1, D_MODEL), lambda i: (layer, 0, 0))],
        out_specs=[pl.BlockSpec((tm, half), lambda i: (i, 0)),
                   pl.BlockSpec((tm, half), lambda i: (i, 0)),
                   pl.BlockSpec((D_MODEL, D_MODEL), lambda i: (0, 0)),
                   pl.BlockSpec((1, D_MODEL), lambda i: (0, 0))],
        out_shape=[pltpu.HBM((tokens,half), BF16),
                   pltpu.HBM((tokens,half), BF16),
                   pltpu.HBM((D_MODEL,D_MODEL), BF16),
                   pltpu.HBM((1,D_MODEL), F32)],
        scratch_shapes=[pltpu.VMEM((D_MODEL, D_MODEL), F32)],
        compiler_params=pltpu.CompilerParams(dimension_semantics=("arbitrary",),
                                             vmem_limit_bytes=_vmem_limit(blocks, _nbytes((D_MODEL, D_MODEL), F32))),
    )(_hbm(g), _hbm(y), _hbm(ya), _hbm(yb), _hbm(wo), _hbm(wo), post_g3)


def _gmlp_bwd_dw(pa, kept, rstd_w, dya, h, dqz, dkv, ln_g3, ln_b3, wmt, layer):
    tokens = pa.shape[0]
    tc = _token_tile(tokens)
    steps = tokens // tc

    def body(pz_ref, kept_ref, rstd_ref, dya_ref, h_ref, dqz_ref, dkv_ref, lg_ref, lb_ref, wmt_ref,
             da_ref, dw_ref, dws_ref, dbs_ref, dlg_ref, dlb_ref, acc_ref, dbsb_ref):
        i = pl.program_id(0)

        @pl.when(i == 0)
        def _():
            acc_ref[...] = jnp.zeros_like(acc_ref)
            dws_ref[...] = jnp.zeros_like(dws_ref)
            dlg_ref[...] = jnp.zeros_like(dlg_ref)
            dlb_ref[...] = jnp.zeros_like(dlb_ref)
            dbsb_ref[...] = jnp.zeros_like(dbsb_ref)

        ln_g = lg_ref[...]
        low = lax.broadcasted_iota(jnp.int32, (CHUNK, LANES), 1) < HEAD_DIM
        hv = h_ref[...]
        acc_ref[:, PA_WIDTH:PA_WIDTH + B_WIDTH] += _mm_tn(hv, dqz_ref[:, 0:B_WIDTH])
        acc_ref[:, PA_WIDTH + K_OFF:PA_WIDTH + BZ_OFF] += _mm_tn(hv, dkv_ref[...])
        acc_ref[:, PA_WIDTH + BZ_OFF:IN_WIDTH] += _mm_tn(hv, dqz_ref[:, B_WIDTH:2 * B_WIDTH])

        for ci in range(tc // CHUNK):
            rows = slice(ci * CHUNK, (ci + 1) * CHUNK)
            u, gu, gv, vhat, mixed = (kept_ref[rows, A_WIDTH * k:A_WIDTH * (k + 1)].astype(F32) for k in range(5))
            rstd = jnp.concatenate([rstd_ref[rows, :]] * (A_WIDTH // LANES), axis=1)
            vnb = (vhat * ln_g + lb_ref[...]).astype(BF16)
            pz = pz_ref[rows, :].astype(F32)
            sg = _sigmoid(pz)
            dy = dya_ref[rows, :].astype(F32)
            sz = pz * sg
            dy_sz = dy * sz
            du = dy_sz * mixed
            dmixed = dy_sz * u
            dz = dy * (u * mixed) * (sg + sz * (1.0 - sg))
            dbsb_ref[...] += dmixed
            dmb = dmixed.astype(BF16)
            zero = jnp.zeros((CHUNK, LANES), BF16)
            parts = []
            for p in range(A_GROUPS // 2):
                dmp = dmb[:, LANES * p:LANES * (p + 1)]
                vp = vnb[:, LANES * p:LANES * (p + 1)]
                parts.append(jnp.where(low, _mm(wmt_ref[2 * p], dmp), _mm(wmt_ref[2 * p + 1], dmp)))
                dws_ref[2 * p] += _mm_nt(jnp.where(low, dmp, zero), vp)
                dws_ref[2 * p + 1] += _mm_nt(jnp.where(low, zero, dmp), vp)
            dvn = jnp.concatenate(parts, axis=1)
            dlg_ref[...] += jnp.sum(dvn * vhat, axis=0, keepdims=True)
            dlb_ref[...] += jnp.sum(dvn, axis=0, keepdims=True)
            dvh = dvn * ln_g
            dvv = rstd * (dvh - jnp.mean(dvh, axis=-1, keepdims=True)
                          - vhat * jnp.mean(dvh * vhat, axis=-1, keepdims=True))
            da_ref[rows, 0:A_WIDTH] = (du * gu).astype(BF16)
            da_ref[rows, A_WIDTH:2 * A_WIDTH] = (dvv * gv).astype(BF16)
            da_ref[rows, 2 * A_WIDTH:3 * A_WIDTH] = dz.astype(BF16)

        acc_ref[:, 0:PA_WIDTH] += _mm_tn(hv, da_ref[...])

        @pl.when(i == steps - 1)
        def _():
            for c in range(IN_WIDTH // LANES):
                cols = slice(LANES * c, LANES * (c + 1))
                dw_ref[cols, :] = acc_ref[:, cols].T.astype(BF16)
            causal = (lax.broadcasted_iota(jnp.int32, (CHUNK, CHUNK), 0)
                      >= lax.broadcasted_iota(jnp.int32, (CHUNK, CHUNK), 1))
            for h in range(A_GROUPS):
                dws_ref[h] = jnp.where(causal, dws_ref[h], 0.0)
            acc = dbsb_ref[...]
            lane_full = lax.broadcasted_iota(jnp.int32, (CHUNK, A_WIDTH), 1)
            lane_out = lax.broadcasted_iota(jnp.int32, (CHUNK, LANES), 1)
            out = jnp.zeros((CHUNK, LANES), F32)
            for h in range(A_GROUPS):
                in_group = jnp.logical_and(lane_full >= HEAD_DIM * h, lane_full < HEAD_DIM * (h + 1))
                s = jnp.sum(jnp.where(in_group, acc, 0.0), axis=-1, keepdims=True)
                out = jnp.where(lane_out == h, s, out)
            dbs_ref[...] = out

    blocks = [_nbytes((tc, KEPT_WIDTH), BF16), 2 * _nbytes((tc, A_WIDTH), BF16), _nbytes((A_GROUPS, CHUNK, CHUNK), BF16),
              _nbytes((tc, LANES), F32), _nbytes((tc, PA_WIDTH), BF16), _nbytes((A_GROUPS, CHUNK, CHUNK), F32),
              _nbytes((CHUNK, LANES), F32), _nbytes((tc, D_MODEL), BF16), _nbytes((tc, 2 * B_WIDTH), BF16),
              _nbytes((tc, 2 * KV_WIDTH), BF16), _nbytes((D_MODEL, IN_WIDTH), BF16)]
    scratch = _nbytes((D_MODEL, IN_WIDTH), F32) + _nbytes((CHUNK, A_WIDTH), F32)
    in_specs = [pl.BlockSpec((tc, A_WIDTH), lambda i: (i, 2)),
                pl.BlockSpec((tc, KEPT_WIDTH), lambda i: (i, 0)),
                pl.BlockSpec((tc, LANES), lambda i: (i, 0)),
                pl.BlockSpec((tc, A_WIDTH), lambda i: (i, 0)),
                pl.BlockSpec((tc, D_MODEL), lambda i: (i, 0)),
                pl.BlockSpec((tc, 2 * B_WIDTH), lambda i: (i, 0)),
                pl.BlockSpec((pl.Element(tc), pl.Element(2 * KV_WIDTH)),
                             lambda i: (pl.multiple_of(i * tc + BLOCK, BLOCK), 0)),
                pl.BlockSpec((None, 1, A_WIDTH), lambda i: (layer, 0, 0)),
                pl.BlockSpec((None, 1, A_WIDTH), lambda i: (layer, 0, 0)),
                pl.BlockSpec((None, A_GROUPS, CHUNK, CHUNK), lambda i: (layer, 0, 0, 0))]
    operands = [_hbm(pa), _hbm(kept), _hbm(rstd_w), _hbm(dya), _hbm(h), _hbm(dqz), _hbm(dkv), ln_g3, ln_b3, wmt]
    out_specs = [pl.BlockSpec((tc, PA_WIDTH), lambda i: (i, 0)),
                 pl.BlockSpec((IN_WIDTH, D_MODEL), lambda i: (0, 0)),
                 pl.BlockSpec((A_GROUPS, CHUNK, CHUNK), lambda i: (0, 0, 0)),
                 pl.BlockSpec((CHUNK, LANES), lambda i: (0, 0)),
                 pl.BlockSpec((1, A_WIDTH), lambda i: (0, 0)),
                 pl.BlockSpec((1, A_WIDTH), lambda i: (0, 0))]
    out_shape = [pltpu.HBM((tokens, PA_WIDTH), BF16),
                 pltpu.HBM((IN_WIDTH, D_MODEL), BF16),
                 pltpu.HBM((A_GROUPS, CHUNK, CHUNK), F32),
                 pltpu.HBM((CHUNK, LANES), F32),
                 pltpu.HBM((1, A_WIDTH), F32),
                 pltpu.HBM((1, A_WIDTH), F32)]
    scratch_shapes = [pltpu.VMEM((D_MODEL, IN_WIDTH), F32), pltpu.VMEM((CHUNK, A_WIDTH), F32)]
    return pl.pallas_call(
        body, name=f"gmlp_bwd_dw_{layer}", grid=(steps,),
        in_specs=in_specs, out_specs=out_specs, out_shape=out_shape, scratch_shapes=scratch_shapes,
        compiler_params=pltpu.CompilerParams(dimension_semantics=("arbitrary",),
                                             vmem_limit_bytes=_vmem_limit(blocks, scratch)),
    )(*operands)


def _attn_bwd(pb, dyb, out_w, probs_w, sink_w, dbias_in, layer):
    tokens = pb.shape[0]
    qz_width = 2 * B_WIDTH
    tq = _token_tile(tokens)
    per_tile = tq // BLOCK
    nt = tokens // tq

    def body(cur_ref, prev_ref, dyb_ref, ob_ref, pw_ref, spw_ref, dbias_in_ref, dqz_ref, dkv_ref, dbias_ref, dsink_ref,
             kv_ref, acc_ref):
        t = pl.program_id(0)

        @pl.when(t == 0)
        def _():
            dbias_ref[...] = dbias_in_ref[...]
            dsink_ref[...] = jnp.zeros_like(dsink_ref)
            acc_ref[0:BLOCK, :] = jnp.zeros((BLOCK, 2 * KV_WIDTH), F32)

        @pl.when(t < nt)
        def _():
            acc_ref[BLOCK:, :] = jnp.zeros((tq, 2 * KV_WIDTH), F32)
            _fill_kv(kv_ref, prev_ref, cur_ref)
            tri = _band_masks()
            low = lax.broadcasted_iota(jnp.int32, (BLOCK, LANES), 1) < HEAD_DIM
            low_kv = lax.broadcasted_iota(jnp.int32, (2 * BLOCK, LANES), 1) < HEAD_DIM

            head_lane = lax.broadcasted_iota(jnp.int32, (BLOCK, LANES), 1)

            def block(i, carry):
                start = i * BLOCK
                rows = pl.ds(start, BLOCK)
                kv = kv_ref[pl.ds(start, 2 * BLOCK), :]
                k_ops = _kv_variants(kv[:, 0:KV_WIDTH])
                v_ops = _kv_variants(kv[:, KV_WIDTH:2 * KV_WIDTH])
                probs, p16, qs, dos, delta, ds16 = {}, {}, {}, {}, {}, {}
                for hd in range(B_HEADS):
                    kept_p = pw_ref[rows, LANES * hd:LANES * (hd + 1)]
                    probs[hd] = kept_p.astype(F32)
                    p16[hd] = _unwrap16(kept_p, tri)
                for kvh in range(2):
                    qs[kvh] = jnp.concatenate(
                        [cur_ref[rows, LANES * p:LANES * (p + 1)] for p in (2 * kvh, 2 * kvh + 1)], axis=0)
                for kvh in range(2):
                    pairs = (2 * kvh, 2 * kvh + 1)
                    d_outs = []
                    for r, p in enumerate(pairs):
                        bz = cur_ref[rows, BZ_OFF + LANES * p:BZ_OFF + LANES * (p + 1)].astype(F32)
                        sg = _sigmoid(bz)
                        dyp = dyb_ref[rows, LANES * p:LANES * (p + 1)].astype(F32)
                        out_p = ob_ref[rows, LANES * p:LANES * (p + 1)].astype(F32)
                        d_out = dyp * (bz * sg)
                        dqz_ref[rows, B_WIDTH + LANES * p:B_WIDTH + LANES * (p + 1)] = (
                            dyp * out_p * (sg * (1.0 + bz * (1.0 - sg)))).astype(BF16)
                        dod = d_out * out_p
                        delta[2 * p] = jnp.sum(jnp.where(low, dod, 0.0), axis=-1, keepdims=True)
                        delta[2 * p + 1] = jnp.sum(jnp.where(low, 0.0, dod), axis=-1, keepdims=True)
                        d_outs.append(d_out.astype(BF16))
                    dos[kvh] = jnp.concatenate(d_outs, axis=0)
                delta_cols = jnp.zeros((BLOCK, LANES), F32)
                for hd in range(B_HEADS):
                    delta_cols = jnp.where(head_lane == hd, delta[hd], delta_cols)
                dsink_ref[0:1, :] += jnp.sum(-(spw_ref[rows, :] * delta_cols), axis=0, keepdims=True)
                for kvh in range(2):
                    pairs = (2 * kvh, 2 * kvh + 1)
                    for j in range(2):
                        dpf = _mm_nt(dos[kvh], v_ops[kvh][j])
                        for r, p in enumerate(pairs):
                            hd = 2 * p + j
                            ds = probs[hd] * (_wrap(dpf[BLOCK * r:BLOCK * (r + 1)], tri) - delta[hd])
                            dbias_ref[hd] += ds
                            ds16[hd] = _unwrap16(ds, tri)
                dk_acc = [[None, None], [None, None]]
                dv_acc = [[None, None], [None, None]]
                for kvh in range(2):
                    pairs = (2 * kvh, 2 * kvh + 1)
                    dq = jnp.zeros((2 * BLOCK, LANES), F32)
                    for j in range(2):
                        dss = jnp.concatenate([ds16[2 * p + j] for p in pairs], axis=0)
                        pss = jnp.concatenate([p16[2 * p + j] for p in pairs], axis=0)
                        dq = dq + _mm(dss, k_ops[kvh][j])
                        dk_acc[kvh][j] = _mm_tn(dss, qs[kvh])
                        dv_acc[kvh][j] = _mm_tn(pss, dos[kvh])
                    for r, p in enumerate(pairs):
                        dqz_ref[rows, LANES * p:LANES * (p + 1)] = (dq[BLOCK * r:BLOCK * (r + 1)] * SCALE).astype(BF16)

                def fold(acc):
                    return jnp.where(low_kv,
                                     acc[0][0] + pltpu.roll(acc[0][1], HEAD_DIM, 1),
                                     pltpu.roll(acc[1][0], HEAD_DIM, 1) + acc[1][1])

                acc_ref[pl.ds(start, 2 * BLOCK), :] += jnp.concatenate(
                    [fold(dk_acc) * SCALE, fold(dv_acc)], axis=1)
                return carry

            for i in range(per_tile):
                block(i, 0)
            dkv_ref[...] = acc_ref[0:tq, :].astype(BF16)
            acc_ref[0:BLOCK, :] = acc_ref[tq:tq + BLOCK, :]

        @pl.when(t == nt)
        def _():
            dkv_ref[0:BLOCK, :] = acc_ref[0:BLOCK, :].astype(BF16)
            dkv_ref[BLOCK:, :] = jnp.zeros((tq - BLOCK, 2 * KV_WIDTH), BF16)

    def cur_map(t):
        return (jnp.minimum(t, nt - 1), 0)

    def prev_map(t):
        return (jnp.maximum(jnp.minimum(t, nt - 1) * per_tile - 1, 0), K_OFF // (2 * KV_WIDTH))

    blocks = [_nbytes((tq, PB_WIDTH), BF16), _nbytes((BLOCK, 2 * KV_WIDTH), BF16), _nbytes((tq, B_WIDTH), BF16),
              2 * _nbytes((B_HEADS, BLOCK, BLOCK), F32), _nbytes((tq, qz_width), BF16),
              _nbytes((tq, 2 * KV_WIDTH), BF16), _nbytes((B_HEADS, LANES), F32),
              _nbytes((tq, B_HEADS * BLOCK), BF16), _nbytes((tq, LANES), F32)]
    scratch = _nbytes((tq + BLOCK, 2 * KV_WIDTH), BF16) + _nbytes((tq + BLOCK, 2 * KV_WIDTH), F32)
    return pl.pallas_call(
        body, name=f"attn_bwd_{layer}", grid=(nt + 1,),
        in_specs=[pl.BlockSpec((tq, PB_WIDTH), cur_map),
                  pl.BlockSpec((BLOCK, 2 * KV_WIDTH), prev_map),
                  pl.BlockSpec((tq, B_WIDTH), cur_map),
                  pl.BlockSpec((tq, B_WIDTH), cur_map),
                  pl.BlockSpec((tq, B_HEADS * BLOCK), cur_map),
                  pl.BlockSpec((tq, LANES), cur_map),
                  pl.BlockSpec((B_HEADS, BLOCK, BLOCK), lambda t: (0, 0, 0))],
        out_specs=[pl.BlockSpec((tq, qz_width), cur_map),
                   pl.BlockSpec((tq, 2 * KV_WIDTH), lambda t: (t, 0)),
                   pl.BlockSpec((B_HEADS, BLOCK, BLOCK), lambda t: (0, 0, 0)),
                   pl.BlockSpec((B_HEADS, LANES), lambda t: (0, 0))],
        out_shape=[pltpu.HBM((tokens, qz_width), BF16),
                   pltpu.HBM((tokens + tq, 2 * KV_WIDTH), BF16),
                   pltpu.HBM((B_HEADS, BLOCK, BLOCK), F32),
                   pltpu.HBM((B_HEADS, LANES), F32)],
        scratch_shapes=[pltpu.VMEM((tq + BLOCK, 2 * KV_WIDTH), BF16),
                        pltpu.VMEM((tq + BLOCK, 2 * KV_WIDTH), F32)],
        compiler_params=pltpu.CompilerParams(dimension_semantics=("arbitrary",),
                                             vmem_limit_bytes=_vmem_limit(blocks, scratch)),
    )(_hbm(pb), _hbm(pb), _hbm(dyb), _hbm(out_w), _hbm(probs_w), _hbm(sink_w), _hbm(dbias_in))


def _inproj_bwd_dx(da, dqz, dkv, w, x2, g, pre_g3, layer):
    tokens = x2.shape[0]
    tm = _token_tile(tokens)

    def body(da_ref, dqz_ref, dkv_ref, w_ref, x_ref, g_ref, pg_ref, gn_ref, dpg_ref):
        @pl.when(pl.program_id(0) == 0)
        def _():
            dpg_ref[...] = jnp.zeros_like(dpg_ref)

        dh = _mm(da_ref[...], w_ref[0:PA_WIDTH, :])
        dh += _mm(dqz_ref[:, 0:B_WIDTH], w_ref[PA_WIDTH:PA_WIDTH + B_WIDTH, :])
        dh += _mm(dkv_ref[...], w_ref[PA_WIDTH + K_OFF:PA_WIDTH + BZ_OFF, :])
        dh += _mm(dqz_ref[:, B_WIDTH:2 * B_WIDTH], w_ref[PA_WIDTH + BZ_OFF:IN_WIDTH, :])
        x = x_ref[...]
        r = lax.rsqrt(jnp.mean(x * x, axis=-1, keepdims=True) + NORM_EPS)
        xhat = x * r
        dhg = dh * pg_ref[...]
        dpg_ref[...] += jnp.sum(dh * xhat, axis=0, keepdims=True)
        gn_ref[...] = g_ref[...] + r * (dhg - xhat * jnp.mean(dhg * xhat, axis=-1, keepdims=True))

    blocks = [_nbytes((tm, PA_WIDTH), BF16), _nbytes((tm, 2 * B_WIDTH), BF16), _nbytes((tm, 2 * KV_WIDTH), BF16),
              _nbytes((D_MODEL, IN_WIDTH), BF16), 3 * _nbytes((tm, D_MODEL), F32)]
    return pl.pallas_call(
        body, name=f"inproj_bwd_dx_{layer}", grid=(tokens // tm,),
        in_specs=[pl.BlockSpec((tm, PA_WIDTH), lambda i: (i, 0)),
                  pl.BlockSpec((tm, 2 * B_WIDTH), lambda i: (i, 0)),
                  pl.BlockSpec((pl.Element(tm), pl.Element(2 * KV_WIDTH)), lambda i: (pl.multiple_of(i * tm + BLOCK, BLOCK), 0)),
                  pl.BlockSpec((IN_WIDTH, D_MODEL), lambda i: (0, 0)),
                  pl.BlockSpec((tm, D_MODEL), lambda i: (i, 0)),
                  pl.BlockSpec((tm, D_MODEL), lambda i: (i, 0)),
                  pl.BlockSpec((None, 1, D_MODEL), lambda i: (layer, 0, 0))],
        out_specs=[pl.BlockSpec((tm, D_MODEL), lambda i: (i, 0)),
                   pl.BlockSpec((1, D_MODEL), lambda i: (0, 0))],
        out_shape=[pltpu.HBM((tokens,D_MODEL), F32),
                   pltpu.HBM((1,D_MODEL), F32)],
        compiler_params=pltpu.CompilerParams(dimension_semantics=("arbitrary",),
                                             vmem_limit_bytes=_vmem_limit(blocks)),
    )(_hbm(da), _hbm(dqz), _hbm(dkv), _hbm(w), _hbm(x2), _hbm(g), pre_g3)


def _rel_bias_grad(dbias, col_bucket, flip):
    def body(db_ref, cb_ref, flip_ref, out_ref):
        cb = cb_ref[...]
        anti = flip_ref[...]
        sums = []
        for h in range(B_HEADS):
            x = db_ref[h]
            hi = x.astype(BF16)
            rest = x - hi.astype(F32)
            mid = rest.astype(BF16)
            low = (rest - mid.astype(F32)).astype(BF16)
            reversed_x = _mm(hi, anti) + _mm(mid, anti) + _mm(low, anti)
            rolled = pltpu.roll(reversed_x, 0, 1, stride=1, stride_axis=0)
            sums.append(jnp.sum(rolled, axis=0, keepdims=True))
        per_dist = jnp.concatenate(sums, axis=0)
        lane = lax.broadcasted_iota(jnp.int32, (B_HEADS, LANES), 1)
        out = jnp.zeros((B_HEADS, LANES), F32)
        for b in range(REL_BUCKETS):
            s = jnp.sum(jnp.where(cb == b, per_dist, 0.0), axis=-1, keepdims=True)
            out = jnp.where(lane == b, s, out)
        out_ref[...] = out

    vm = pl.BlockSpec(memory_space=pltpu.VMEM)
    return pl.pallas_call(
        body, name="rel_bias_grad",
        out_shape=jax.ShapeDtypeStruct((B_HEADS, LANES), F32),
        in_specs=[vm, vm, vm], out_specs=vm,
    )(dbias, col_bucket, flip)


def _reduce_adamw(slots, w, m, v, name):
    rows, cols = w.shape
    tr = _row_tile(rows)
    c1 = 1.0 / (1.0 - ADAM_B1 ** ADAM_STEP)
    c2 = 1.0 / (1.0 - ADAM_B2 ** ADAM_STEP)

    def body(s_ref, w_ref, m_ref, v_ref, g_ref, d_ref, nm_ref, nv_ref):
        g = s_ref[0].astype(F32)
        for i in range(1, N_DEV):
            g = g + s_ref[i].astype(F32)
        nm = ADAM_B1 * m_ref[...] + (1.0 - ADAM_B1) * g
        nv = ADAM_B2 * v_ref[...] + (1.0 - ADAM_B2) * (g * g)
        g_ref[...] = g
        nm_ref[...] = nm
        nv_ref[...] = nv
        d_ref[...] = -ADAM_LR * ((nm * c1) / (jnp.sqrt(nv * c2) + ADAM_EPS) + ADAM_WD * w_ref[...])

    blocks = [_nbytes((N_DEV, tr, cols), slots.dtype), 7 * _nbytes((tr, cols), F32)]
    tile = pl.BlockSpec((tr, cols), lambda i: (i, 0))
    return pl.pallas_call(
        body, name=name, grid=(rows // tr,),
        in_specs=[pl.BlockSpec((N_DEV, tr, cols), lambda i: (0, i, 0)), tile, tile, tile],
        out_specs=[tile] * 4,
        out_shape=[pltpu.HBM((rows, cols), F32)] * 4,
        compiler_params=pltpu.CompilerParams(dimension_semantics=("parallel",),
                                             vmem_limit_bytes=_vmem_limit(blocks)),
    )(_hbm(slots), _hbm(w), _hbm(m), _hbm(v))


_SMALL = ("pre_norm_g", "post_norm_g", "ln_v_g", "ln_v_b", "b_spatial", "sinks", "rel_bias", "loss")


def _pack_small(parts):
    slabs = []
    for name in _SMALL:
        flat = parts[name].astype(F32).reshape(-1)
        pad = (-flat.shape[0]) % (8 * LANES)
        slabs.append(jnp.pad(flat, (0, pad)).reshape(-1, LANES))
    return jnp.concatenate(slabs, axis=0)


def _unpack_small(slab, shapes):
    out, row = {}, 0
    for name in _SMALL:
        size = int(np.prod(shapes[name]))
        rows = -(-size // (8 * LANES)) * 8
        out[name] = slab[row:row + rows].reshape(-1)[:size].reshape(shapes[name])
        row += rows
    return out


def kernel(x, pre_norm_g, w_in, ln_v_g, ln_v_b, w_spatial, b_spatial, sinks, rel_bias, w_out, post_norm_g, loss_target, m_pre_norm_g, m_w_in, m_ln_v_g, m_ln_v_b, m_w_spatial, m_b_spatial, m_sinks, m_rel_bias, m_w_out, m_post_norm_g, v_pre_norm_g, v_w_in, v_ln_v_g, v_ln_v_b, v_w_spatial, v_b_spatial, v_sinks, v_rel_bias, v_w_out, v_post_norm_g):
    b_loc, seq, _ = x.shape
    tokens = b_loc * seq
    depth = w_in.shape[0]
    in_shard = w_in.shape[2]
    out_shard = w_out.shape[1]
    assert in_shard * N_DEV == IN_WIDTH and out_shard * N_DEV == D_MODEL and seq % BLOCK == 0

    w_in_t, m_w_in_t, v_w_in_t = (jnp.swapaxes(a, 1, 2) for a in (w_in, m_w_in, v_w_in))
    w_in16, w_out16 = w_in_t.astype(BF16), w_out.astype(BF16)

    def gather_start(layer, not_before, pre_g):
        zones = [lax.empty((N_DEV, in_shard, D_MODEL), BF16), lax.empty((N_DEV, out_shard, D_MODEL), BF16)]
        sems, sends, zones, pre_g = _exchange_start(
            [w_in16[layer], w_out16[layer]], zones, [0, 0], f"weights_send_{layer}", pre_g, after=not_before)
        return (sems, sends, [0, 0]), zones, pre_g

    def full_weights(gin, gout):
        return gin.reshape(IN_WIDTH, D_MODEL), gout.reshape(D_MODEL, D_MODEL)

    def gather_wait(layer, started, zones, after):
        return full_weights(*_exchange_wait([started], zones, after, f"weights_wait_{layer}"))

    causal = jnp.tril(jnp.ones((CHUNK, CHUNK), dtype=bool))
    wm = jnp.where(causal, w_spatial, 0.0).astype(BF16)
    wmt = _hbm(jnp.swapaxes(wm, -1, -2))
    wm = _hbm(wm)
    bsb = _hbm(jnp.repeat(jnp.swapaxes(b_spatial, -1, -2), HEAD_DIM, axis=-1))
    post_g3 = _hbm(post_norm_g.reshape(depth, 1, D_MODEL))
    ln_g3 = _hbm(ln_v_g.reshape(depth, 1, A_WIDTH))
    ln_b3 = _hbm(ln_v_b.reshape(depth, 1, A_WIDTH))
    bucket = jnp.asarray(_bucket_table())
    bias = _bias_table(rel_bias, bucket)

    xs, saved, weights = [x.reshape(tokens, D_MODEL)], [], []
    pending = None
    for layer in range(depth):
        pre_g_fwd = pre_norm_g.reshape(depth, 1, D_MODEL)
        if layer == 0:
            (gin,) = _all_gather([w_in16[0]], "weights_gather_0")
            w = gin.reshape(IN_WIDTH, D_MODEL)
            out0 = _exchange_start([w_out16[0]], [lax.empty((N_DEV, out_shard, D_MODEL), BF16)], [0],
                                   "weights_send_out_0", pre_g_fwd, after=w)
            pre_g_fwd = out0[3]
        else:
            w, wo = gather_wait(layer, pending[0], pending[1], xs[-1])
        if layer + 1 < depth:
            pending = gather_start(layer + 1, w, pre_g_fwd)
            pre_g_fwd = pending[2]
        h, pa, pb, ya, kept, rstd_w = _inproj_gmlp_fwd(xs[-1], _hbm(pre_g_fwd), w, ln_g3, ln_b3, wm, bsb, layer)
        if layer == 0:
            (gout,) = _exchange_wait([(out0[0], out0[1], [0])], out0[2], h, "weights_wait_out_0")
            wo = gout.reshape(D_MODEL, D_MODEL)
        if layer + 1 < depth:
            yb, ob, pw, spw, y, x_next = _attn_outproj_fwd(pb, bias, sinks, ya, wo, xs[-1], post_g3, b_loc, layer)
            xs.append(x_next)
        else:
            yb, ob, pw, spw, y, g, loss_part = _attn_outproj_fwd(pb, bias, sinks, ya, wo, xs[-1], post_g3, b_loc, layer,
                                                    target=loss_target.reshape(tokens, D_MODEL))
        saved.append((h, pa, pb, ya, yb, y, ob, pw, spw, kept, rstd_w))
        weights.append((w, wo))

    grads = {name: [None] * depth for name in ("pre_norm_g", "post_norm_g", "ln_v_g", "ln_v_b", "b_spatial", "sinks")}
    zone_in = lax.empty((N_DEV, depth * in_shard, D_MODEL), BF16)
    zone_out = lax.empty((N_DEV, depth * out_shard, D_MODEL), BF16)
    zone_ws = lax.empty((N_DEV, depth * A_GROUPS * CHUNK, CHUNK), F32)
    started_in, started_out = [], []
    dbias = jnp.zeros((B_HEADS, BLOCK, BLOCK), F32)
    for layer in reversed(range(depth)):
        h, pa, pb, ya, yb, y, ob, pw, spw, kept, rstd_w = saved[layer]
        w, wo = weights[layer]
        dya, dyb, dwo, dpost = _outproj_bwd(g, y, ya, yb, wo, post_g3, layer)
        send_out = dwo.reshape(N_DEV, out_shard, D_MODEL)
        sems, sends, (zone_out,), ln_g_bwd = _exchange_start(
            [send_out], [zone_out], [layer * out_shard], f"grads_send_out_{layer}", ln_v_g.reshape(depth, 1, A_WIDTH))
        started_out.append((sems, sends, [layer * out_shard]))
        dqz, dkv, dbias, dsink = _attn_bwd(pb, dyb, ob, pw, spw, dbias, layer)
        da, dw, dws, dbs, dlg, dlb = _gmlp_bwd_dw(pa, kept, rstd_w, dya, h, dqz, dkv, _hbm(ln_g_bwd), ln_b3, wmt, layer)
        send_in = dw.reshape(N_DEV, in_shard, D_MODEL)
        row_offs = [layer * A_GROUPS * CHUNK, layer * in_shard]
        sems, sends, (zone_ws, zone_in), pre_g_bwd = _exchange_start(
            [dws.reshape(A_GROUPS * CHUNK, CHUNK), send_in], [zone_ws, zone_in], row_offs, f"grads_send_in_{layer}",
            pre_norm_g.reshape(depth, 1, D_MODEL))
        started_in.append((sems, sends, row_offs))
        g, dpre = _inproj_bwd_dx(da, dqz, dkv, w, xs[layer], g, _hbm(pre_g_bwd), layer)
        grads["b_spatial"][layer] = dbs[:, :A_GROUPS].T
        grads["ln_v_g"][layer] = dlg[0]
        grads["ln_v_b"][layer] = dlb[0]
        grads["sinks"][layer] = dsink[0, :B_HEADS]
        grads["pre_norm_g"][layer] = dpre[0]
        grads["post_norm_g"][layer] = dpost[0]
    grad_x = g.reshape(x.shape)
    col_bucket = jnp.asarray(np.broadcast_to(_bucket_table()[0:1, ::-1], (B_HEADS, BLOCK)))
    flip = jnp.asarray(np.eye(BLOCK, dtype=np.float32)[::-1], dtype=BF16)
    drel = _rel_bias_grad(dbias, col_bucket, flip)[:, :REL_BUCKETS].T

    (recv_out,) = _exchange_wait(started_out, [zone_out], g, "grads_wait_out")
    res_out = _reduce_adamw(recv_out, w_out.reshape(-1, D_MODEL), m_w_out.reshape(-1, D_MODEL),
                            v_w_out.reshape(-1, D_MODEL), "adamw_w_out")
    res_out = [r.reshape(w_out.shape) for r in res_out]

    no_state = jnp.zeros((1,), F32)
    small_w = dict(pre_norm_g=pre_norm_g, post_norm_g=post_norm_g, ln_v_g=ln_v_g,
                   ln_v_b=ln_v_b, b_spatial=b_spatial, sinks=sinks, rel_bias=rel_bias, loss=no_state)
    small_m = dict(pre_norm_g=m_pre_norm_g, post_norm_g=m_post_norm_g, ln_v_g=m_ln_v_g,
                   ln_v_b=m_ln_v_b, b_spatial=m_b_spatial, sinks=m_sinks, rel_bias=m_rel_bias, loss=no_state)
    small_v = dict(pre_norm_g=v_pre_norm_g, post_norm_g=v_post_norm_g, ln_v_g=v_ln_v_g,
                   ln_v_b=v_ln_v_b, b_spatial=v_b_spatial, sinks=v_sinks, rel_bias=v_rel_bias, loss=no_state)
    small_g = {name: jnp.stack(grads[name]) for name in _SMALL if name not in ("rel_bias", "loss")}
    small_g["rel_bias"] = drel
    small_g["loss"] = loss_part[0, :1]
    shapes = {name: small_w[name].shape for name in _SMALL}
    slots = _all_gather_direct(_pack_small(small_g), "small_grads_all_gather")
    res_small = _reduce_adamw(slots, _pack_small(small_w), _pack_small(small_m), _pack_small(small_v), "adamw_small")

    recv_ws, recv_in = _exchange_wait(started_in, [zone_ws, zone_in], res_small[0], "grads_wait_in")
    res_ws = _reduce_adamw(recv_ws, w_spatial.reshape(-1, CHUNK), m_w_spatial.reshape(-1, CHUNK),
                           v_w_spatial.reshape(-1, CHUNK), "adamw_w_spatial")
    res_ws = [r.reshape(w_spatial.shape) for r in res_ws]
    res_in = _reduce_adamw(recv_in, w_in_t.reshape(-1, D_MODEL), m_w_in_t.reshape(-1, D_MODEL),
                           v_w_in_t.reshape(-1, D_MODEL), "adamw_w_in")
    res_in = [jnp.swapaxes(r.reshape(w_in_t.shape), 1, 2) for r in res_in]
    res_small = [_unpack_small(r, shapes) for r in res_small]

    order = ("pre_norm_g", "w_in", "ln_v_g", "ln_v_b", "w_spatial", "b_spatial", "sinks", "rel_bias", "w_out",
             "post_norm_g")
    outs = [res_small[0]["loss"][0], grad_x]
    for kind in range(4):
        for name in order:
            if name == "w_in":
                outs.append(res_in[kind])
            elif name == "w_out":
                outs.append(res_out[kind])
            elif name == "w_spatial":
                outs.append(res_ws[kind])
            else:
                outs.append(res_small[kind][name])
    return tuple(outs)
```

```python
import math

import numpy as np
import jax
import jax.numpy as jnp
from jax import lax
from jax.experimental import pallas as pl
from jax.experimental.pallas import tpu as pltpu

F32 = jnp.float32
BF16 = jnp.bfloat16

D_MODEL = 1024
A_WIDTH = 512
A_GROUPS = 8
CHUNK = 128
B_HEADS = 8
HEAD_DIM = 64
B_WIDTH = 512
KV_WIDTH = 128
BLOCK = 128
REL_BUCKETS = 32
REL_MAX_DIST = 128
NORM_EPS = 1e-6
PA_WIDTH = 3 * A_WIDTH
PB_WIDTH = 2 * B_WIDTH + 2 * KV_WIDTH
IN_WIDTH = PA_WIDTH + PB_WIDTH
K_OFF, BZ_OFF = B_WIDTH, B_WIDTH + 2 * KV_WIDTH
KEPT_WIDTH = 5 * A_WIDTH
SCALE = HEAD_DIM ** -0.5
NEG = -1e30
N_DEV = 8
LANES = 128

ADAM_LR = 0.001
ADAM_B1 = 0.9
ADAM_B2 = 0.999
ADAM_EPS = 1e-08
ADAM_WD = 0.01
ADAM_STEP = 10

V7X_VMEM_BYTES = 64 * 1024 * 1024
VMEM_TEMP_BYTES = 12 * 1024 * 1024
MESH = pl.DeviceIdType.MESH


def _vmem_limit(block_bytes, scratch_bytes=0):
    need = 2 * sum(block_bytes) + scratch_bytes + VMEM_TEMP_BYTES
    return int(min(need, V7X_VMEM_BYTES - 8 * 1024 * 1024))


def _nbytes(shape, dtype):
    return int(np.prod(shape)) * jnp.dtype(dtype).itemsize


def _token_tile(tokens, cap=512):
    tile = min(cap, tokens // 2)
    assert tokens % tile == 0 and tile % CHUNK == 0, tokens
    return tile


def _row_tile(rows, cap=512):
    best = 8
    for t in range(8, cap + 1, 8):
        if rows % t == 0:
            best = t
    assert rows % best == 0, rows
    return best


def _mm(a, b):
    return lax.dot_general(a, b, (((1,), (0,)), ((), ())), preferred_element_type=F32)


def _mm_nt(a, b):
    return lax.dot_general(a, b, (((1,), (1,)), ((), ())), preferred_element_type=F32)


def _mm_tn(a, b):
    return lax.dot_general(a, b, (((0,), (0,)), ((), ())), preferred_element_type=F32)


_GELU_C = math.sqrt(2.0 / math.pi)


_GELU_A = _GELU_C * 0.044715


def _gelu_parts(x):
    x2 = x * x
    t = jnp.tanh(x * (_GELU_C + _GELU_A * x2))
    return x2, t, 0.5 + 0.5 * t


def _gelu_and_grad(x):
    x2, t, half_plus = _gelu_parts(x)
    grad = half_plus + (0.5 * x) * (1.0 - t * t) * (_GELU_C + (3.0 * _GELU_A) * x2)
    return x * half_plus, grad


def _sigmoid(x):
    return 0.5 + 0.5 * jnp.tanh(0.5 * x)


def _bucket_table():
    q = np.arange(BLOCK)[:, None]
    k = np.arange(BLOCK)[None, :]
    dist = np.where(k <= q, q - k, q + BLOCK - k)
    max_exact = REL_BUCKETS // 2
    safe = np.maximum(dist, 1).astype(np.float32)
    large = max_exact + (np.log(safe / np.float32(max_exact)) / np.float32(math.log(REL_MAX_DIST / max_exact))
                         * np.float32(REL_BUCKETS - max_exact)).astype(np.int32)
    large = np.minimum(large, REL_BUCKETS - 1)
    assert dist.min() >= 0 and dist.max() < BLOCK
    return np.where(dist < max_exact, dist, large).astype(np.int32)


def _hbm(x):
    return pltpu.with_memory_space_constraint(x, pltpu.HBM)


def _slot(px, py, pc):
    return 4 * px + 2 * py + pc


def _all_gather(arrs, name):
    n = len(arrs)

    def body(*refs):
        ins, outs = refs[:n], refs[n:2 * n]
        send_sems, recv_sems, local_sems = refs[2 * n:]
        x, y, c = lax.axis_index("x"), lax.axis_index("y"), lax.axis_index("c")
        me, sibling = (x, y, c), (x, y, 1 - c)
        chips = [(1 - x, y), (x, 1 - y), (1 - x, 1 - y)]

        def copy(a, k, block, to, src=None):
            dst = outs[a].at[_slot(*block)]
            return pltpu.make_async_remote_copy(
                src_ref=dst if src is None else src, dst_ref=dst,
                send_sem=send_sems.at[7 * a + k], recv_sem=recv_sems.at[7 * a + k],
                device_id=to, device_id_type=MESH)

        mine = [pltpu.make_async_copy(ins[a], outs[a].at[_slot(*me)], local_sems.at[a]) for a in range(n)]
        for cp in mine:
            cp.start()
        first = []
        for a in range(n):
            first.append(copy(a, 0, me, sibling, src=ins[a]))
            first += [copy(a, 1 + j, me, (*chip, c), src=ins[a]) for j, chip in enumerate(chips)]
        for cp in first:
            cp.start()
        passed = []
        for j, chip in enumerate(chips):
            for a in range(n):
                copy(a, 1 + j, (*chip, c), me).wait_recv()
                fwd = copy(a, 4 + j, (*chip, c), sibling)
                fwd.start()
                passed.append(fwd)
        for a in range(n):
            copy(a, 0, sibling, me).wait_recv()
            for j, chip in enumerate(chips):
                copy(a, 4 + j, (*chip, 1 - c), me).wait_recv()
        for cp in first + passed:
            cp.wait_send()
        for cp in mine:
            cp.wait()

    any_spec = pl.BlockSpec(memory_space=pl.ANY)
    return pl.pallas_call(
        body, name=name,
        out_shape=[jax.ShapeDtypeStruct((N_DEV,) + a.shape, a.dtype) for a in arrs],
        in_specs=[any_spec] * n, out_specs=[any_spec] * n,
        scratch_shapes=[pltpu.SemaphoreType.DMA((7 * n,)), pltpu.SemaphoreType.DMA((7 * n,)),
                        pltpu.SemaphoreType.DMA((n,))],
    )(*arrs)


def _all_gather_direct(arr, name):
    def body(in_ref, out_ref, send_sems, recv_sems, local_sem):
        me, peers = _peers()
        mine = pltpu.make_async_copy(in_ref, out_ref.at[_slot(*me)], local_sem)
        mine.start()

        def copy(k, origin, to):
            return pltpu.make_async_remote_copy(
                src_ref=in_ref, dst_ref=out_ref.at[_slot(*origin)], send_sem=send_sems.at[k],
                recv_sem=recv_sems.at[k], device_id=to, device_id_type=MESH)

        sends = [copy(k, me, peer) for k, peer in enumerate(peers)]
        for cp in sends:
            cp.start()
        for k, peer in enumerate(peers):
            copy(k, peer, me).wait_recv()
        for cp in sends:
            cp.wait_send()
        mine.wait()

    any_spec = pl.BlockSpec(memory_space=pl.ANY)
    return pl.pallas_call(
        body, name=name,
        out_shape=jax.ShapeDtypeStruct((N_DEV,) + arr.shape, arr.dtype),
        in_specs=[any_spec], out_specs=any_spec,
        scratch_shapes=[pltpu.SemaphoreType.DMA((7,)), pltpu.SemaphoreType.DMA((7,)), pltpu.SemaphoreType.DMA],
    )(arr)


_HBM_SPEC = pl.BlockSpec(memory_space=pltpu.HBM)
_SEM_SPEC = pl.BlockSpec(memory_space=pltpu.SEMAPHORE)
_DATAFLOW = pltpu.SideEffectType.DATAFLOW_SIDE_EFFECTING


def _peers():
    x, y, c = lax.axis_index("x"), lax.axis_index("y"), lax.axis_index("c")
    peers = []
    for k in range(1, N_DEV):
        fx, fy, fc = (k >> 2) & 1, (k >> 1) & 1, k & 1
        peers.append((1 - x if fx else x, 1 - y if fy else y, 1 - c if fc else c))
    return (x, y, c), peers


def _exchange_copy(send_ref, land_ref, row_off, src_slot, dst_slot, sems, idx, peer):
    rows = send_ref.shape[-2]
    src = send_ref.at[src_slot] if len(send_ref.shape) == 3 else send_ref
    return pltpu.make_async_remote_copy(
        src_ref=src, dst_ref=land_ref.at[dst_slot, pl.ds(row_off, rows), :],
        send_sem=sems[0].at[idx], recv_sem=sems[1].at[idx], device_id=peer, device_id_type=MESH)


def _own_copy(send_ref, land_ref, row_off, slot, sem):
    rows = send_ref.shape[-2]
    src = send_ref.at[slot] if len(send_ref.shape) == 3 else send_ref
    return pltpu.make_async_copy(src, land_ref.at[slot, pl.ds(row_off, rows), :], sem)


def _exchange_start(sends, lands, row_offs, name, carry, after=None):
    n = len(sends)

    def body(*refs):
        ins, zones, carry_ref = refs[:n], refs[n:2 * n], refs[2 * n]
        first_out = 2 * n + 1 + (after is not None)
        sems = refs[first_out:first_out + 3]
        carry_out = refs[-1]
        me, peers = _peers()
        for a in range(n):
            for k, peer in enumerate(peers):
                _exchange_copy(ins[a], zones[a], row_offs[a], _slot(*peer), _slot(*me), sems, 7 * a + k, peer).start()
            _own_copy(ins[a], zones[a], row_offs[a], _slot(*me), sems[2].at[a]).start()
        carry_out[...] = carry_ref[...]

    arrays = [_hbm(a) for a in list(sends) + list(lands)]
    vmem = pl.BlockSpec(memory_space=pltpu.VMEM)
    out = pl.pallas_call(
        body, name=name,
        out_shape=(pltpu.SemaphoreType.DMA((7 * n,)), pltpu.SemaphoreType.DMA((7 * n,)), pltpu.SemaphoreType.DMA((n,)),
                   *[pltpu.HBM(a.shape, a.dtype) for a in arrays], jax.ShapeDtypeStruct(carry.shape, carry.dtype)),
        in_specs=[_HBM_SPEC] * (2 * n) + [vmem] + ([pl.BlockSpec(memory_space=pl.ANY)] if after is not None else []),
        out_specs=(_SEM_SPEC, _SEM_SPEC, _SEM_SPEC, *[_HBM_SPEC] * (2 * n), vmem),
        input_output_aliases={i: 3 + i for i in range(2 * n)},
        compiler_params=pltpu.CompilerParams(has_side_effects=_DATAFLOW),
    )(*arrays, carry, *([after] if after is not None else []))
    return (out[0], out[1], out[2]), list(out[3:3 + n]), list(out[3 + n:3 + 2 * n]), out[-1]


def _exchange_wait(started, lands, after, name):
    n = len(lands)
    flat_sends = [s for _, sends, _ in started for s in sends]
    flat_sems = [s for sems, _, _ in started for s in sems]
    ns = len(flat_sends)

    def body(*refs):
        ins, zones = refs[:ns], refs[ns:ns + n]
        sem_refs = refs[ns + n:ns + n + len(flat_sems)]
        me, peers = _peers()
        pos = 0
        for call, (_, sends, row_offs) in enumerate(started):
            sems = sem_refs[3 * call:3 * call + 3]
            for a in range(len(sends)):
                for k, peer in enumerate(peers):
                    cp = _exchange_copy(ins[pos + a], zones[a], row_offs[a], _slot(*peer), _slot(*peer), sems,
                                        7 * a + k, peer)
                    cp.wait_send()
                    cp.wait_recv()
                _own_copy(ins[pos + a], zones[a], row_offs[a], _slot(*me), sems[2].at[a]).wait()
            pos += len(sends)

    arrays = list(flat_sends) + list(lands)
    out = pl.pallas_call(
        body, name=name,
        out_shape=tuple(pltpu.HBM(a.shape, a.dtype) for a in arrays),
        in_specs=[_HBM_SPEC] * len(arrays) + [_SEM_SPEC] * len(flat_sems) + [pl.BlockSpec(memory_space=pl.ANY)],
        out_specs=tuple([_HBM_SPEC] * len(arrays)),
        input_output_aliases={i: i for i in range(len(arrays))},
        compiler_params=pltpu.CompilerParams(has_side_effects=_DATAFLOW),
    )(*arrays, *flat_sems, after)
    return list(out[ns:])


def _bias_table(rel_bias, bucket):
    def body(rel_ref, bucket_ref, out_ref):
        bk = bucket_ref[...]
        for h in range(B_HEADS):
            def pick(b, acc, h=h):
                return jnp.where(bk == b, rel_ref[b, h], acc)
            out_ref[h] = lax.fori_loop(0, REL_BUCKETS, pick, jnp.zeros((BLOCK, BLOCK), F32))

    return pl.pallas_call(
        body, name="bias_table",
        out_shape=jax.ShapeDtypeStruct((B_HEADS, BLOCK, BLOCK), F32),
        in_specs=[pl.BlockSpec(memory_space=pltpu.SMEM), pl.BlockSpec(memory_space=pltpu.VMEM)],
        out_specs=pl.BlockSpec(memory_space=pltpu.VMEM),
    )(rel_bias, bucket)


def _inproj_gmlp_fwd(x2, pre_g3, w, ln_g3, ln_b3, wm, bsb, layer):
    tokens = x2.shape[0]
    tm = _token_tile(tokens)

    def body(x_ref, g_ref, w_ref, lg_ref, lb_ref, wm_ref, bsb_ref, h_ref, pa_ref, pb_ref, ya_ref, kept_ref, rstd_ref):
        x = x_ref[...]
        r = lax.rsqrt(jnp.mean(x * x, axis=-1, keepdims=True) + NORM_EPS)
        h = (x * r * g_ref[...]).astype(BF16)
        h_ref[...] = h
        pa_ref[...] = _mm_nt(h, w_ref[0:PA_WIDTH, :]).astype(BF16)
        pb_ref[...] = _mm_nt(h, w_ref[PA_WIDTH:IN_WIDTH, :]).astype(BF16)
        for ci in range(tm // CHUNK):
            rows = slice(ci * CHUNK, (ci + 1) * CHUNK)
            gu, gv, pz, u, vhat, rstd, _, mixed, sg = _gmlp_forward_chunk(
                pa_ref[rows, :], lg_ref[...], lb_ref[...], wm_ref, bsb_ref[...])
            ya_ref[rows, :] = (u * mixed * (pz * sg)).astype(BF16)
            for k, val in enumerate((u, gu, gv, vhat, mixed)):
                kept_ref[rows, A_WIDTH * k:A_WIDTH * (k + 1)] = val.astype(BF16)
            rstd_ref[rows, :] = jnp.broadcast_to(rstd, (CHUNK, LANES))

    blocks = [_nbytes((tm, D_MODEL), F32), _nbytes((D_MODEL, IN_WIDTH), BF16),
              _nbytes((tm, D_MODEL), BF16), _nbytes((tm, PA_WIDTH), BF16), _nbytes((tm, PB_WIDTH), BF16),
              _nbytes((A_GROUPS, CHUNK, CHUNK), BF16), _nbytes((CHUNK, A_WIDTH), F32), _nbytes((tm, A_WIDTH), BF16)]
    return pl.pallas_call(
        body, name=f"inproj_gmlp_fwd_{layer}", grid=(tokens // tm,),
        in_specs=[pl.BlockSpec((tm, D_MODEL), lambda i: (i, 0)),
                  pl.BlockSpec((None, 1, D_MODEL), lambda i: (layer, 0, 0)),
                  pl.BlockSpec((IN_WIDTH, D_MODEL), lambda i: (0, 0)),
                  pl.BlockSpec((None, 1, A_WIDTH), lambda i: (layer, 0, 0)),
                  pl.BlockSpec((None, 1, A_WIDTH), lambda i: (layer, 0, 0)),
                  pl.BlockSpec((None, A_GROUPS, CHUNK, CHUNK), lambda i: (layer, 0, 0, 0)),
                  pl.BlockSpec((None, CHUNK, A_WIDTH), lambda i: (layer, 0, 0))],
        out_specs=[pl.BlockSpec((tm, D_MODEL), lambda i: (i, 0)),
                   pl.BlockSpec((tm, PA_WIDTH), lambda i: (i, 0)),
                   pl.BlockSpec((tm, PB_WIDTH), lambda i: (i, 0)),
                   pl.BlockSpec((tm, A_WIDTH), lambda i: (i, 0)),
                   pl.BlockSpec((tm, KEPT_WIDTH), lambda i: (i, 0)),
                   pl.BlockSpec((tm, LANES), lambda i: (i, 0))],
        out_shape=[pltpu.HBM((tokens, D_MODEL), BF16),
                   pltpu.HBM((tokens, PA_WIDTH), BF16),
                   pltpu.HBM((tokens, PB_WIDTH), BF16),
                   pltpu.HBM((tokens, A_WIDTH), BF16),
                   pltpu.HBM((tokens, KEPT_WIDTH), BF16),
                   pltpu.HBM((tokens, LANES), F32)],
        compiler_params=pltpu.CompilerParams(dimension_semantics=("parallel",),
                                             vmem_limit_bytes=_vmem_limit(blocks + [_nbytes((tm, KEPT_WIDTH), BF16),
                                                                                    _nbytes((tm, LANES), F32)])),
    )(_hbm(x2), pre_g3, _hbm(w), ln_g3, ln_b3, wm, bsb)


def _gmlp_forward_chunk(pa, ln_g, ln_b, wm_ref, bsb):
    pu = pa[:, 0:A_WIDTH].astype(F32)
    pv = pa[:, A_WIDTH:2 * A_WIDTH].astype(F32)
    pz = pa[:, 2 * A_WIDTH:3 * A_WIDTH].astype(F32)
    u, gu = _gelu_and_grad(pu)
    vv, gv = _gelu_and_grad(pv)
    mu = jnp.mean(vv, axis=-1, keepdims=True)
    xc = vv - mu
    rstd = lax.rsqrt(jnp.mean(xc * xc, axis=-1, keepdims=True) + NORM_EPS)
    vhat = xc * rstd
    vnb = (vhat * ln_g + ln_b).astype(BF16)
    low = lax.broadcasted_iota(jnp.int32, (CHUNK, LANES), 1) < HEAD_DIM
    parts = []
    for p in range(A_GROUPS // 2):
        vp = vnb[:, LANES * p:LANES * (p + 1)]
        parts.append(jnp.where(low, _mm(wm_ref[2 * p], vp), _mm(wm_ref[2 * p + 1], vp)))
    mixed = jnp.concatenate(parts, axis=1) + bsb
    sg = _sigmoid(pz)
    return gu, gv, pz, u, vhat, rstd, vnb, mixed, sg


def _kv_variants(kv):
    t = kv.astype(F32)
    rolled = pltpu.roll(t, HEAD_DIM, 1)
    low = lax.broadcasted_iota(jnp.int32, t.shape, 1) < HEAD_DIM
    zero = jnp.zeros_like(t)
    head0 = (jnp.where(low, t, zero).astype(BF16), jnp.where(low, zero, rolled).astype(BF16))
    head1 = (jnp.where(low, rolled, zero).astype(BF16), jnp.where(low, zero, t).astype(BF16))
    return head0, head1


def _band_masks():
    row = lax.broadcasted_iota(jnp.int32, (BLOCK, BLOCK), 0)
    col = lax.broadcasted_iota(jnp.int32, (BLOCK, BLOCK), 1)
    return col <= row


def _wrap(full, tri):
    return jnp.where(tri, full[:, BLOCK:2 * BLOCK], full[:, 0:BLOCK])


def _attn_probs(sf, bias_h, sink, tri, kill):
    s = _wrap(sf, tri) + bias_h
    s = jnp.where(kill, NEG, s)
    m = jnp.maximum(jnp.max(s, axis=-1, keepdims=True), sink)
    e = jnp.exp(s - m)
    es = jnp.exp(sink - m)
    inv = 1.0 / (jnp.sum(e, axis=-1, keepdims=True) + es)
    return e * inv, es * inv


def _unwrap16(p, tri):
    p = p.astype(BF16)
    zero = jnp.zeros_like(p)
    return jnp.concatenate([jnp.where(tri, zero, p), jnp.where(tri, p, zero)], axis=1)


def _fill_kv(kv_ref, prev_ref, cur_ref):
    kv_ref[0:BLOCK, :] = prev_ref[...]
    kv_ref[BLOCK:, :] = cur_ref[:, K_OFF:K_OFF + 2 * KV_WIDTH]


def _attn_outproj_fwd(pb, bias, sinks, ya, wo, x2, post_g3, b_loc, layer, target=None):
    tokens = pb.shape[0]
    nb = tokens // b_loc // BLOCK
    tq = _token_tile(tokens)
    per_tile = tq // BLOCK
    steps = tokens // tq
    with_loss = target is not None
    half = D_MODEL // 2

    def body(*refs):
        sink_ref, cur_ref, prev_ref, bias_ref, ya_ref, woa_ref, wob_ref, x_ref, g_ref = refs[:9]
        if with_loss:
            t_ref, yb_ref, ob_ref, pw_ref, spw_ref, y_ref, grad_ref, loss_ref, kv_ref, acc_ref = refs[9:]
        else:
            yb_ref, ob_ref, pw_ref, spw_ref, y_ref, xn_ref, kv_ref = refs[9:]
        t = pl.program_id(0)
        _fill_kv(kv_ref, prev_ref, cur_ref)
        tri = _band_masks()
        head_lane = lax.broadcasted_iota(jnp.int32, (BLOCK, LANES), 1)

        def block(i, carry):
            start = i * BLOCK
            rows = pl.ds(start, BLOCK)
            first = lax.rem(t * per_tile + i, nb) == 0
            kill = jnp.logical_and(first, jnp.logical_not(tri))
            kv = kv_ref[pl.ds(start, 2 * BLOCK), :]
            k_ops = _kv_variants(kv[:, 0:KV_WIDTH])
            v_ops = _kv_variants(kv[:, KV_WIDTH:2 * KV_WIDTH])
            p16 = {}
            sink_cols = jnp.zeros((BLOCK, LANES), F32)
            for kvh in range(2):
                pairs = (2 * kvh, 2 * kvh + 1)
                qs = jnp.concatenate([cur_ref[rows, LANES * p:LANES * (p + 1)] for p in pairs], axis=0) * SCALE
                for j in range(2):
                    sf = _mm_nt(qs, k_ops[kvh][j])
                    for r, p in enumerate(pairs):
                        hd = 2 * p + j
                        probs, sink_p = _attn_probs(sf[BLOCK * r:BLOCK * (r + 1)], bias_ref[hd],
                                                    sink_ref[layer, hd], tri, kill)
                        probs = probs.astype(BF16)
                        p16[hd] = _unwrap16(probs, tri)
                        pw_ref[rows, LANES * hd:LANES * (hd + 1)] = probs
                        sink_cols = jnp.where(head_lane == hd, sink_p, sink_cols)
            spw_ref[rows, :] = sink_cols
            for kvh in range(2):
                pairs = (2 * kvh, 2 * kvh + 1)
                out = jnp.zeros((2 * BLOCK, LANES), F32)
                for j in range(2):
                    out = out + _mm(jnp.concatenate([p16[2 * p + j] for p in pairs], axis=0), v_ops[kvh][j])
                for r, p in enumerate(pairs):
                    bz = cur_ref[rows, BZ_OFF + LANES * p:BZ_OFF + LANES * (p + 1)].astype(F32)
                    out_p = out[BLOCK * r:BLOCK * (r + 1)]
                    ob_ref[rows, LANES * p:LANES * (p + 1)] = out_p.astype(BF16)
                    yb_ref[rows, LANES * p:LANES * (p + 1)] = (out_p * (bz * _sigmoid(bz))).astype(BF16)
            y = _mm(ya_ref[rows, :], woa_ref[...]) + _mm(yb_ref[rows, :], wob_ref[...])
            r = lax.rsqrt(jnp.mean(y * y, axis=-1, keepdims=True) + NORM_EPS)
            x_next = x_ref[rows, :] + y * r * g_ref[...]
            y_ref[rows, :] = y.astype(BF16)
            if not with_loss:
                xn_ref[rows, :] = x_next
            else:
                err = x_next - t_ref[rows, :]
                grad_ref[rows, :] = err * (1.0 / D_MODEL)
                acc_ref[...] += jnp.sum(err * err, axis=0, keepdims=True)
            return carry

        if with_loss:
            @pl.when(t == 0)
            def _():
                acc_ref[...] = jnp.zeros_like(acc_ref)

        for i in range(per_tile):
            block(i, 0)

        if not with_loss:
            return

        @pl.when(t == steps - 1)
        def _():
            total = jnp.sum(acc_ref[...], axis=-1, keepdims=True) * (0.5 / D_MODEL)
            loss_ref[...] = jnp.broadcast_to(total, loss_ref.shape)

    tile = pl.BlockSpec((tq, D_MODEL), lambda t: (t, 0))
    blocks = [_nbytes((tq, PB_WIDTH), BF16), _nbytes((BLOCK, 2 * KV_WIDTH), BF16),
              _nbytes((B_HEADS, BLOCK, BLOCK), F32), 2 * _nbytes((tq, B_WIDTH), BF16),
              2 * _nbytes((half, D_MODEL), BF16), 3 * _nbytes((tq, D_MODEL), F32), _nbytes((tq, D_MODEL), BF16),
              _nbytes((tq, B_HEADS * BLOCK), BF16), _nbytes((tq, LANES), F32)]
    scratch = _nbytes((tq + BLOCK, 2 * KV_WIDTH), BF16)
    in_specs = [pl.BlockSpec(memory_space=pltpu.SMEM),
                pl.BlockSpec((tq, PB_WIDTH), lambda t: (t, 0)),
                pl.BlockSpec((BLOCK, 2 * KV_WIDTH),
                             lambda t: (jnp.maximum(t * per_tile - 1, 0), K_OFF // (2 * KV_WIDTH))),
                pl.BlockSpec((B_HEADS, BLOCK, BLOCK), lambda t: (0, 0, 0)),
                pl.BlockSpec((tq, half), lambda t: (t, 0)),
                pl.BlockSpec((half, D_MODEL), lambda t: (0, 0)),
                pl.BlockSpec((half, D_MODEL), lambda t: (1, 0)),
                tile,
                pl.BlockSpec((None, 1, D_MODEL), lambda t: (layer, 0, 0))]
    out_specs = [pl.BlockSpec((tq, B_WIDTH), lambda t: (t, 0)), pl.BlockSpec((tq, B_WIDTH), lambda t: (t, 0)),
                 pl.BlockSpec((tq, B_HEADS * BLOCK), lambda t: (t, 0)),
                 pl.BlockSpec((tq, LANES), lambda t: (t, 0)), tile, tile]
    out_shape = [pltpu.HBM((tokens, B_WIDTH), BF16), pltpu.HBM((tokens, B_WIDTH), BF16),
                 pltpu.HBM((tokens, B_HEADS * BLOCK), BF16),
                 pltpu.HBM((tokens, LANES), F32), pltpu.HBM((tokens, D_MODEL), BF16),
                 pltpu.HBM((tokens, D_MODEL), F32)]
    scratch_shapes = [pltpu.VMEM((tq + BLOCK, 2 * KV_WIDTH), BF16)]
    operands = [sinks, _hbm(pb), _hbm(pb), _hbm(bias), _hbm(ya), _hbm(wo), _hbm(wo), _hbm(x2), post_g3]
    if with_loss:
        in_specs.append(tile)
        out_specs.append(pl.BlockSpec((1, LANES), lambda t: (0, 0)))
        out_shape.append(pltpu.HBM((1, LANES), F32))
        scratch_shapes.append(pltpu.VMEM((1, D_MODEL), F32))
        operands.append(_hbm(target))
    return pl.pallas_call(
        body, name=f"attn_outproj_fwd_{layer}", grid=(steps,),
        in_specs=in_specs, out_specs=out_specs, out_shape=out_shape, scratch_shapes=scratch_shapes,
        compiler_params=pltpu.CompilerParams(dimension_semantics=("arbitrary" if with_loss else "parallel",),
                                             vmem_limit_bytes=_vmem_limit(blocks, scratch)),
    )(*operands)


def _outproj_bwd(g, y, ya, yb, wo, post_g3, layer):
    tokens = g.shape[0]
    tm = _token_tile(tokens, cap=1024)
    half = D_MODEL // 2
    steps = tokens // tm

    def body(g_any, y_ref, ya_ref, yb_ref, woa_ref, wob_ref, pg_ref, dya_ref, dyb_ref, dwo16_ref, dpg_ref, dwo_ref,
             gbuf, gsem):
        i = pl.program_id(0)

        def g_copy(step, slot):
            rows = pl.ds(pl.multiple_of(step * tm, tm), tm)
            return pltpu.make_async_copy(g_any.at[rows, :], gbuf.at[slot], gsem.at[slot])

        @pl.when(i == 0)
        def _():
            dwo_ref[...] = jnp.zeros_like(dwo_ref)
            dpg_ref[...] = jnp.zeros_like(dpg_ref)
            g_copy(0, 0).start()
            if steps > 1:
                g_copy(1, 1).start()

        @pl.when(i + 2 < steps)
        def _():
            g_copy(i + 2, (i + 2) % 3).start()

        slot = i % 3
        g_copy(i, slot).wait()
        gv = gbuf[slot]
        yf = y_ref[...].astype(F32)
        r = lax.rsqrt(jnp.mean(yf * yf, axis=-1, keepdims=True) + NORM_EPS)
        yhat = yf * r
        gg = gv * pg_ref[...]
        dy = r * (gg - yhat * jnp.mean(gg * yhat, axis=-1, keepdims=True))
        dpg_ref[...] += jnp.sum(gv * yhat, axis=0, keepdims=True)
        dyb16 = dy.astype(BF16)
        dya_ref[...] = _mm_nt(dyb16, woa_ref[...]).astype(BF16)
        dyb_ref[...] = _mm_nt(dyb16, wob_ref[...]).astype(BF16)
        dwo_ref[0:half, :] += _mm_tn(ya_ref[...], dyb16)
        dwo_ref[half:D_MODEL, :] += _mm_tn(yb_ref[...], dyb16)

        @pl.when(pl.program_id(0) == steps - 1)
        def _():
            dwo16_ref[...] = dwo_ref[...].astype(BF16)

    blocks = [_nbytes((tm, D_MODEL), BF16), 4 * _nbytes((tm, half), BF16),
              2 * _nbytes((half, D_MODEL), BF16), _nbytes((D_MODEL, D_MODEL), BF16)]
    ring = 3 * _nbytes((tm, D_MODEL), F32)
    return pl.pallas_call(
        body, name=f"outproj_bwd_{layer}", grid=(tokens // tm,),
        in_specs=[pl.BlockSpec(memory_space=pl.ANY),
                  pl.BlockSpec((tm, D_MODEL), lambda i: (i, 0)),
                  pl.BlockSpec((tm, half), lambda i: (i, 0)),
                  pl.BlockSpec((tm, half), lambda i: (i, 0)),
                  pl.BlockSpec((half, D_MODEL), lambda i: (0, 0)),
                  pl.BlockSpec((half, D_MODEL), lambda i: (1, 0)),
                  pl.BlockSpec((None, 1, D_MODEL), lambda i: (layer, 0, 0))],
        out_specs=[pl.BlockSpec((tm, half), lambda i: (i, 0)),
                   pl.BlockSpec((tm, half), lambda i: (i, 0)),
                   pl.BlockSpec((D_MODEL, D_MODEL), lambda i: (0, 0)),
                   pl.BlockSpec((1, D_MODEL), lambda i: (0, 0))],
        out_shape=[pltpu.HBM((tokens,half), BF16),
                   pltpu.HBM((tokens,half), BF16),
                   pltpu.HBM((D_MODEL,D_MODEL), BF16),
                   pltpu.HBM((1,D_MODEL), F32)],
        scratch_shapes=[pltpu.VMEM((D_MODEL, D_MODEL), F32), pltpu.VMEM((3, tm, D_MODEL), F32),
                        pltpu.SemaphoreType.DMA((3,))],
        compiler_params=pltpu.CompilerParams(
            dimension_semantics=("arbitrary",),
            vmem_limit_bytes=_vmem_limit(blocks, _nbytes((D_MODEL, D_MODEL), F32) + ring)),
    )(_hbm(g), _hbm(y), _hbm(ya), _hbm(yb), _hbm(wo), _hbm(wo), post_g3)


def _gmlp_bwd_dw(pa, kept, rstd_w, dya, h, dqz, dkv, ln_g3, ln_b3, wmt, layer):
    tokens = pa.shape[0]
    tc = _token_tile(tokens)
    steps = tokens // tc

    def body(pz_ref, kept_ref, rstd_ref, dya_ref, h_ref, dqz_ref, dkv_ref, lg_ref, lb_ref, wmt_ref,
             da_ref, dw_ref, dws_ref, dbs_ref, dlg_ref, dlb_ref, acc_ref, dbsb_ref):
        i = pl.program_id(0)

        @pl.when(i == 0)
        def _():
            acc_ref[...] = jnp.zeros_like(acc_ref)
            dws_ref[...] = jnp.zeros_like(dws_ref)
            dlg_ref[...] = jnp.zeros_like(dlg_ref)
            dlb_ref[...] = jnp.zeros_like(dlb_ref)
            dbsb_ref[...] = jnp.zeros_like(dbsb_ref)

        ln_g = lg_ref[...]
        low = lax.broadcasted_iota(jnp.int32, (CHUNK, LANES), 1) < HEAD_DIM
        hv = h_ref[...]
        acc_ref[:, PA_WIDTH:PA_WIDTH + B_WIDTH] += _mm_tn(hv, dqz_ref[:, 0:B_WIDTH])
        acc_ref[:, PA_WIDTH + K_OFF:PA_WIDTH + BZ_OFF] += _mm_tn(hv, dkv_ref[...])
        acc_ref[:, PA_WIDTH + BZ_OFF:IN_WIDTH] += _mm_tn(hv, dqz_ref[:, B_WIDTH:2 * B_WIDTH])

        for ci in range(tc // CHUNK):
            rows = slice(ci * CHUNK, (ci + 1) * CHUNK)
            u, gu, gv, vhat, mixed = (kept_ref[rows, A_WIDTH * k:A_WIDTH * (k + 1)].astype(F32) for k in range(5))
            rstd = jnp.concatenate([rstd_ref[rows, :]] * (A_WIDTH // LANES), axis=1)
            vnb = (vhat * ln_g + lb_ref[...]).astype(BF16)
            pz = pz_ref[rows, :].astype(F32)
            sg = _sigmoid(pz)
            dy = dya_ref[rows, :].astype(F32)
            sz = pz * sg
            dy_sz = dy * sz
            du = dy_sz * mixed
            dmixed = dy_sz * u
            dz = dy * (u * mixed) * (sg + sz * (1.0 - sg))
            dbsb_ref[...] += dmixed
            dmb = dmixed.astype(BF16)
            zero = jnp.zeros((CHUNK, LANES), BF16)
            parts = []
            for p in range(A_GROUPS // 2):
                dmp = dmb[:, LANES * p:LANES * (p + 1)]
                vp = vnb[:, LANES * p:LANES * (p + 1)]
                parts.append(jnp.where(low, _mm(wmt_ref[2 * p], dmp), _mm(wmt_ref[2 * p + 1], dmp)))
                dws_ref[2 * p] += _mm_nt(jnp.where(low, dmp, zero), vp)
                dws_ref[2 * p + 1] += _mm_nt(jnp.where(low, zero, dmp), vp)
            dvn = jnp.concatenate(parts, axis=1)
            dlg_ref[...] += jnp.sum(dvn * vhat, axis=0, keepdims=True)
            dlb_ref[...] += jnp.sum(dvn, axis=0, keepdims=True)
            dvh = dvn * ln_g
            dvv = rstd * (dvh - jnp.mean(dvh, axis=-1, keepdims=True)
                          - vhat * jnp.mean(dvh * vhat, axis=-1, keepdims=True))
            da_ref[rows, 0:A_WIDTH] = (du * gu).astype(BF16)
            da_ref[rows, A_WIDTH:2 * A_WIDTH] = (dvv * gv).astype(BF16)
            da_ref[rows, 2 * A_WIDTH:3 * A_WIDTH] = dz.astype(BF16)

        acc_ref[:, 0:PA_WIDTH] += _mm_tn(hv, da_ref[...])

        @pl.when(i == steps - 1)
        def _():
            for c in range(IN_WIDTH // LANES):
                cols = slice(LANES * c, LANES * (c + 1))
                dw_ref[cols, :] = acc_ref[:, cols].T.astype(BF16)
            causal = (lax.broadcasted_iota(jnp.int32, (CHUNK, CHUNK), 0)
                      >= lax.broadcasted_iota(jnp.int32, (CHUNK, CHUNK), 1))
            for h in range(A_GROUPS):
                dws_ref[h] = jnp.where(causal, dws_ref[h], 0.0)
            acc = dbsb_ref[...]
            lane_full = lax.broadcasted_iota(jnp.int32, (CHUNK, A_WIDTH), 1)
            lane_out = lax.broadcasted_iota(jnp.int32, (CHUNK, LANES), 1)
            out = jnp.zeros((CHUNK, LANES), F32)
            for h in range(A_GROUPS):
                in_group = jnp.logical_and(lane_full >= HEAD_DIM * h, lane_full < HEAD_DIM * (h + 1))
                s = jnp.sum(jnp.where(in_group, acc, 0.0), axis=-1, keepdims=True)
                out = jnp.where(lane_out == h, s, out)
            dbs_ref[...] = out

    blocks = [_nbytes((tc, KEPT_WIDTH), BF16), 2 * _nbytes((tc, A_WIDTH), BF16), _nbytes((A_GROUPS, CHUNK, CHUNK), BF16),
              _nbytes((tc, LANES), F32), _nbytes((tc, PA_WIDTH), BF16), _nbytes((A_GROUPS, CHUNK, CHUNK), F32),
              _nbytes((CHUNK, LANES), F32), _nbytes((tc, D_MODEL), BF16), _nbytes((tc, 2 * B_WIDTH), BF16),
              _nbytes((tc, 2 * KV_WIDTH), BF16), _nbytes((D_MODEL, IN_WIDTH), BF16)]
    scratch = _nbytes((D_MODEL, IN_WIDTH), F32) + _nbytes((CHUNK, A_WIDTH), F32)
    in_specs = [pl.BlockSpec((tc, A_WIDTH), lambda i: (i, 2)),
                pl.BlockSpec((tc, KEPT_WIDTH), lambda i: (i, 0)),
                pl.BlockSpec((tc, LANES), lambda i: (i, 0)),
                pl.BlockSpec((tc, A_WIDTH), lambda i: (i, 0)),
                pl.BlockSpec((tc, D_MODEL), lambda i: (i, 0)),
                pl.BlockSpec((tc, 2 * B_WIDTH), lambda i: (i, 0)),
                pl.BlockSpec((pl.Element(tc), pl.Element(2 * KV_WIDTH)),
                             lambda i: (pl.multiple_of(i * tc + BLOCK, BLOCK), 0)),
                pl.BlockSpec((None, 1, A_WIDTH), lambda i: (layer, 0, 0)),
                pl.BlockSpec((None, 1, A_WIDTH), lambda i: (layer, 0, 0)),
                pl.BlockSpec((None, A_GROUPS, CHUNK, CHUNK), lambda i: (layer, 0, 0, 0))]
    operands = [_hbm(pa), _hbm(kept), _hbm(rstd_w), _hbm(dya), _hbm(h), _hbm(dqz), _hbm(dkv), ln_g3, ln_b3, wmt]
    out_specs = [pl.BlockSpec((tc, PA_WIDTH), lambda i: (i, 0)),
                 pl.BlockSpec((IN_WIDTH, D_MODEL), lambda i: (0, 0)),
                 pl.BlockSpec((A_GROUPS, CHUNK, CHUNK), lambda i: (0, 0, 0)),
                 pl.BlockSpec((CHUNK, LANES), lambda i: (0, 0)),
                 pl.BlockSpec((1, A_WIDTH), lambda i: (0, 0)),
                 pl.BlockSpec((1, A_WIDTH), lambda i: (0, 0))]
    out_shape = [pltpu.HBM((tokens, PA_WIDTH), BF16),
                 pltpu.HBM((IN_WIDTH, D_MODEL), BF16),
                 pltpu.HBM((A_GROUPS, CHUNK, CHUNK), F32),
                 pltpu.HBM((CHUNK, LANES), F32),
                 pltpu.HBM((1, A_WIDTH), F32),
                 pltpu.HBM((1, A_WIDTH), F32)]
    scratch_shapes = [pltpu.VMEM((D_MODEL, IN_WIDTH), F32), pltpu.VMEM((CHUNK, A_WIDTH), F32)]
    return pl.pallas_call(
        body, name=f"gmlp_bwd_dw_{layer}", grid=(steps,),
        in_specs=in_specs, out_specs=out_specs, out_shape=out_shape, scratch_shapes=scratch_shapes,
        compiler_params=pltpu.CompilerParams(dimension_semantics=("arbitrary",),
                                             vmem_limit_bytes=_vmem_limit(blocks, scratch)),
    )(*operands)


def _attn_bwd(pb, dyb, out_w, probs_w, sink_w, dbias_in, layer):
    tokens = pb.shape[0]
    qz_width = 2 * B_WIDTH
    tq = _token_tile(tokens)
    per_tile = tq // BLOCK
    nt = tokens // tq

    def body(cur_ref, prev_ref, dyb_ref, ob_ref, pw_ref, spw_ref, dbias_in_ref, dqz_ref, dkv_ref, dbias_ref, dsink_ref,
             kv_ref, acc_ref):
        t = pl.program_id(0)

        @pl.when(t == 0)
        def _():
            dbias_ref[...] = dbias_in_ref[...]
            dsink_ref[...] = jnp.zeros_like(dsink_ref)
            acc_ref[0:BLOCK, :] = jnp.zeros((BLOCK, 2 * KV_WIDTH), F32)

        @pl.when(t < nt)
        def _():
            acc_ref[BLOCK:, :] = jnp.zeros((tq, 2 * KV_WIDTH), F32)
            _fill_kv(kv_ref, prev_ref, cur_ref)
            tri = _band_masks()
            low = lax.broadcasted_iota(jnp.int32, (BLOCK, LANES), 1) < HEAD_DIM
            low_kv = lax.broadcasted_iota(jnp.int32, (2 * BLOCK, LANES), 1) < HEAD_DIM

            head_lane = lax.broadcasted_iota(jnp.int32, (BLOCK, LANES), 1)

            def block(i, carry):
                start = i * BLOCK
                rows = pl.ds(start, BLOCK)
                kv = kv_ref[pl.ds(start, 2 * BLOCK), :]
                k_ops = _kv_variants(kv[:, 0:KV_WIDTH])
                v_ops = _kv_variants(kv[:, KV_WIDTH:2 * KV_WIDTH])
                probs, p16, qs, dos, delta, ds16 = {}, {}, {}, {}, {}, {}
                for hd in range(B_HEADS):
                    kept_p = pw_ref[rows, LANES * hd:LANES * (hd + 1)]
                    probs[hd] = kept_p.astype(F32)
                    p16[hd] = _unwrap16(kept_p, tri)
                for kvh in range(2):
                    qs[kvh] = jnp.concatenate(
                        [cur_ref[rows, LANES * p:LANES * (p + 1)] for p in (2 * kvh, 2 * kvh + 1)], axis=0)
                for kvh in range(2):
                    pairs = (2 * kvh, 2 * kvh + 1)
                    d_outs = []
                    for r, p in enumerate(pairs):
                        bz = cur_ref[rows, BZ_OFF + LANES * p:BZ_OFF + LANES * (p + 1)].astype(F32)
                        sg = _sigmoid(bz)
                        dyp = dyb_ref[rows, LANES * p:LANES * (p + 1)].astype(F32)
                        out_p = ob_ref[rows, LANES * p:LANES * (p + 1)].astype(F32)
                        d_out = dyp * (bz * sg)
                        dqz_ref[rows, B_WIDTH + LANES * p:B_WIDTH + LANES * (p + 1)] = (
                            dyp * out_p * (sg * (1.0 + bz * (1.0 - sg)))).astype(BF16)
                        dod = d_out * out_p
                        delta[2 * p] = jnp.sum(jnp.where(low, dod, 0.0), axis=-1, keepdims=True)
                        delta[2 * p + 1] = jnp.sum(jnp.where(low, 0.0, dod), axis=-1, keepdims=True)
                        d_outs.append(d_out.astype(BF16))
                    dos[kvh] = jnp.concatenate(d_outs, axis=0)
                delta_cols = jnp.zeros((BLOCK, LANES), F32)
                for hd in range(B_HEADS):
                    delta_cols = jnp.where(head_lane == hd, delta[hd], delta_cols)
                dsink_ref[0:1, :] += jnp.sum(-(spw_ref[rows, :] * delta_cols), axis=0, keepdims=True)
                for kvh in range(2):
                    pairs = (2 * kvh, 2 * kvh + 1)
                    for j in range(2):
                        dpf = _mm_nt(dos[kvh], v_ops[kvh][j])
                        for r, p in enumerate(pairs):
                            hd = 2 * p + j
                            ds = probs[hd] * (_wrap(dpf[BLOCK * r:BLOCK * (r + 1)], tri) - delta[hd])
                            dbias_ref[hd] += ds
                            ds16[hd] = _unwrap16(ds, tri)
                dk_acc = [[None, None], [None, None]]
                dv_acc = [[None, None], [None, None]]
                for kvh in range(2):
                    pairs = (2 * kvh, 2 * kvh + 1)
                    dq = jnp.zeros((2 * BLOCK, LANES), F32)
                    for j in range(2):
                        dss = jnp.concatenate([ds16[2 * p + j] for p in pairs], axis=0)
                        pss = jnp.concatenate([p16[2 * p + j] for p in pairs], axis=0)
                        dq = dq + _mm(dss, k_ops[kvh][j])
                        dk_acc[kvh][j] = _mm_tn(dss, qs[kvh])
                        dv_acc[kvh][j] = _mm_tn(pss, dos[kvh])
                    for r, p in enumerate(pairs):
                        dqz_ref[rows, LANES * p:LANES * (p + 1)] = (dq[BLOCK * r:BLOCK * (r + 1)] * SCALE).astype(BF16)

                def fold(acc):
                    return jnp.where(low_kv,
                                     acc[0][0] + pltpu.roll(acc[0][1], HEAD_DIM, 1),
                                     pltpu.roll(acc[1][0], HEAD_DIM, 1) + acc[1][1])

                acc_ref[pl.ds(start, 2 * BLOCK), :] += jnp.concatenate(
                    [fold(dk_acc) * SCALE, fold(dv_acc)], axis=1)
                return carry

            for i in range(per_tile):
                block(i, 0)
            dkv_ref[...] = acc_ref[0:tq, :].astype(BF16)
            acc_ref[0:BLOCK, :] = acc_ref[tq:tq + BLOCK, :]

        @pl.when(t == nt)
        def _():
            dkv_ref[0:BLOCK, :] = acc_ref[0:BLOCK, :].astype(BF16)
            dkv_ref[BLOCK:, :] = jnp.zeros((tq - BLOCK, 2 * KV_WIDTH), BF16)

    def cur_map(t):
        return (jnp.minimum(t, nt - 1), 0)

    def prev_map(t):
        return (jnp.maximum(jnp.minimum(t, nt - 1) * per_tile - 1, 0), K_OFF // (2 * KV_WIDTH))

    blocks = [_nbytes((tq, PB_WIDTH), BF16), _nbytes((BLOCK, 2 * KV_WIDTH), BF16), _nbytes((tq, B_WIDTH), BF16),
              2 * _nbytes((B_HEADS, BLOCK, BLOCK), F32), _nbytes((tq, qz_width), BF16),
              _nbytes((tq, 2 * KV_WIDTH), BF16), _nbytes((B_HEADS, LANES), F32),
              _nbytes((tq, B_HEADS * BLOCK), BF16), _nbytes((tq, LANES), F32)]
    scratch = _nbytes((tq + BLOCK, 2 * KV_WIDTH), BF16) + _nbytes((tq + BLOCK, 2 * KV_WIDTH), F32)
    return pl.pallas_call(
        body, name=f"attn_bwd_{layer}", grid=(nt + 1,),
        in_specs=[pl.BlockSpec((tq, PB_WIDTH), cur_map),
                  pl.BlockSpec((BLOCK, 2 * KV_WIDTH), prev_map),
                  pl.BlockSpec((tq, B_WIDTH), cur_map),
                  pl.BlockSpec((tq, B_WIDTH), cur_map),
                  pl.BlockSpec((tq, B_HEADS * BLOCK), cur_map),
                  pl.BlockSpec((tq, LANES), cur_map),
                  pl.BlockSpec((B_HEADS, BLOCK, BLOCK), lambda t: (0, 0, 0))],
        out_specs=[pl.BlockSpec((tq, qz_width), cur_map),
                   pl.BlockSpec((tq, 2 * KV_WIDTH), lambda t: (t, 0)),
                   pl.BlockSpec((B_HEADS, BLOCK, BLOCK), lambda t: (0, 0, 0)),
                   pl.BlockSpec((B_HEADS, LANES), lambda t: (0, 0))],
        out_shape=[pltpu.HBM((tokens, qz_width), BF16),
                   pltpu.HBM((tokens + tq, 2 * KV_WIDTH), BF16),
                   pltpu.HBM((B_HEADS, BLOCK, BLOCK), F32),
                   pltpu.HBM((B_HEADS, LANES), F32)],
        scratch_shapes=[pltpu.VMEM((tq + BLOCK, 2 * KV_WIDTH), BF16),
                        pltpu.VMEM((tq + BLOCK, 2 * KV_WIDTH), F32)],
        compiler_params=pltpu.CompilerParams(dimension_semantics=("arbitrary",),
                                             vmem_limit_bytes=_vmem_limit(blocks, scratch)),
    )(_hbm(pb), _hbm(pb), _hbm(dyb), _hbm(out_w), _hbm(probs_w), _hbm(sink_w), _hbm(dbias_in))


def _inproj_bwd_dx(da, dqz, dkv, w, x2, g, pre_g3, layer):
    tokens = x2.shape[0]
    tm = _token_tile(tokens)

    def body(da_ref, dqz_ref, dkv_ref, w_ref, x_ref, g_ref, pg_ref, gn_ref, dpg_ref):
        @pl.when(pl.program_id(0) == 0)
        def _():
            dpg_ref[...] = jnp.zeros_like(dpg_ref)

        dh = _mm(da_ref[...], w_ref[0:PA_WIDTH, :])
        dh += _mm(dqz_ref[:, 0:B_WIDTH], w_ref[PA_WIDTH:PA_WIDTH + B_WIDTH, :])
        dh += _mm(dkv_ref[...], w_ref[PA_WIDTH + K_OFF:PA_WIDTH + BZ_OFF, :])
        dh += _mm(dqz_ref[:, B_WIDTH:2 * B_WIDTH], w_ref[PA_WIDTH + BZ_OFF:IN_WIDTH, :])
        x = x_ref[...]
        r = lax.rsqrt(jnp.mean(x * x, axis=-1, keepdims=True) + NORM_EPS)
        xhat = x * r
        dhg = dh * pg_ref[...]
        dpg_ref[...] += jnp.sum(dh * xhat, axis=0, keepdims=True)
        gn_ref[...] = g_ref[...] + r * (dhg - xhat * jnp.mean(dhg * xhat, axis=-1, keepdims=True))

    blocks = [_nbytes((tm, PA_WIDTH), BF16), _nbytes((tm, 2 * B_WIDTH), BF16), _nbytes((tm, 2 * KV_WIDTH), BF16),
              _nbytes((D_MODEL, IN_WIDTH), BF16), 3 * _nbytes((tm, D_MODEL), F32)]
    return pl.pallas_call(
        body, name=f"inproj_bwd_dx_{layer}", grid=(tokens // tm,),
        in_specs=[pl.BlockSpec((tm, PA_WIDTH), lambda i: (i, 0)),
                  pl.BlockSpec((tm, 2 * B_WIDTH), lambda i: (i, 0)),
                  pl.BlockSpec((pl.Element(tm), pl.Element(2 * KV_WIDTH)), lambda i: (pl.multiple_of(i * tm + BLOCK, BLOCK), 0)),
                  pl.BlockSpec((IN_WIDTH, D_MODEL), lambda i: (0, 0)),
                  pl.BlockSpec((tm, D_MODEL), lambda i: (i, 0)),
                  pl.BlockSpec((tm, D_MODEL), lambda i: (i, 0)),
                  pl.BlockSpec((None, 1, D_MODEL), lambda i: (layer, 0, 0))],
        out_specs=[pl.BlockSpec((tm, D_MODEL), lambda i: (i, 0)),
                   pl.BlockSpec((1, D_MODEL), lambda i: (0, 0))],
        out_shape=[pltpu.HBM((tokens,D_MODEL), F32),
                   pltpu.HBM((1,D_MODEL), F32)],
        compiler_params=pltpu.CompilerParams(dimension_semantics=("arbitrary",),
                                             vmem_limit_bytes=_vmem_limit(blocks)),
    )(_hbm(da), _hbm(dqz), _hbm(dkv), _hbm(w), _hbm(x2), _hbm(g), pre_g3)


def _rel_bias_grad(dbias, col_bucket, flip):
    def body(db_ref, cb_ref, flip_ref, out_ref):
        cb = cb_ref[...]
        anti = flip_ref[...]
        sums = []
        for h in range(B_HEADS):
            x = db_ref[h]
            hi = x.astype(BF16)
            rest = x - hi.astype(F32)
            mid = rest.astype(BF16)
            low = (rest - mid.astype(F32)).astype(BF16)
            reversed_x = _mm(hi, anti) + _mm(mid, anti) + _mm(low, anti)
            rolled = pltpu.roll(reversed_x, 0, 1, stride=1, stride_axis=0)
            sums.append(jnp.sum(rolled, axis=0, keepdims=True))
        per_dist = jnp.concatenate(sums, axis=0)
        lane = lax.broadcasted_iota(jnp.int32, (B_HEADS, LANES), 1)
        out = jnp.zeros((B_HEADS, LANES), F32)
        for b in range(REL_BUCKETS):
            s = jnp.sum(jnp.where(cb == b, per_dist, 0.0), axis=-1, keepdims=True)
            out = jnp.where(lane == b, s, out)
        out_ref[...] = out

    vm = pl.BlockSpec(memory_space=pltpu.VMEM)
    return pl.pallas_call(
        body, name="rel_bias_grad",
        out_shape=jax.ShapeDtypeStruct((B_HEADS, LANES), F32),
        in_specs=[vm, vm, vm], out_specs=vm,
    )(dbias, col_bucket, flip)


def _reduce_adamw(slots, w, m, v, name):
    rows, cols = w.shape
    tr = _row_tile(rows)
    c1 = 1.0 / (1.0 - ADAM_B1 ** ADAM_STEP)
    c2 = 1.0 / (1.0 - ADAM_B2 ** ADAM_STEP)

    def body(s_ref, w_ref, m_ref, v_ref, g_ref, d_ref, nm_ref, nv_ref):
        g = s_ref[0].astype(F32)
        for i in range(1, N_DEV):
            g = g + s_ref[i].astype(F32)
        nm = ADAM_B1 * m_ref[...] + (1.0 - ADAM_B1) * g
        nv = ADAM_B2 * v_ref[...] + (1.0 - ADAM_B2) * (g * g)
        g_ref[...] = g
        nm_ref[...] = nm
        nv_ref[...] = nv
        d_ref[...] = -ADAM_LR * ((nm * c1) / (jnp.sqrt(nv * c2) + ADAM_EPS) + ADAM_WD * w_ref[...])

    blocks = [_nbytes((N_DEV, tr, cols), slots.dtype), 7 * _nbytes((tr, cols), F32)]
    tile = pl.BlockSpec((tr, cols), lambda i: (i, 0))
    return pl.pallas_call(
        body, name=name, grid=(rows // tr,),
        in_specs=[pl.BlockSpec((N_DEV, tr, cols), lambda i: (0, i, 0)), tile, tile, tile],
        out_specs=[tile] * 4,
        out_shape=[pltpu.HBM((rows, cols), F32)] * 4,
        compiler_params=pltpu.CompilerParams(dimension_semantics=("parallel",),
                                             vmem_limit_bytes=_vmem_limit(blocks)),
    )(_hbm(slots), _hbm(w), _hbm(m), _hbm(v))


_SMALL = ("pre_norm_g", "post_norm_g", "ln_v_g", "ln_v_b", "b_spatial", "sinks", "rel_bias", "loss")


def _pack_small(parts):
    slabs = []
    for name in _SMALL:
        flat = parts[name].astype(F32).reshape(-1)
        pad = (-flat.shape[0]) % (8 * LANES)
        slabs.append(jnp.pad(flat, (0, pad)).reshape(-1, LANES))
    return jnp.concatenate(slabs, axis=0)


def _unpack_small(slab, shapes):
    out, row = {}, 0
    for name in _SMALL:
        size = int(np.prod(shapes[name]))
        rows = -(-size // (8 * LANES)) * 8
        out[name] = slab[row:row + rows].reshape(-1)[:size].reshape(shapes[name])
        row += rows
    return out


def kernel(x, pre_norm_g, w_in, ln_v_g, ln_v_b, w_spatial, b_spatial, sinks, rel_bias, w_out, post_norm_g, loss_target, m_pre_norm_g, m_w_in, m_ln_v_g, m_ln_v_b, m_w_spatial, m_b_spatial, m_sinks, m_rel_bias, m_w_out, m_post_norm_g, v_pre_norm_g, v_w_in, v_ln_v_g, v_ln_v_b, v_w_spatial, v_b_spatial, v_sinks, v_rel_bias, v_w_out, v_post_norm_g):
    b_loc, seq, _ = x.shape
    tokens = b_loc * seq
    depth = w_in.shape[0]
    in_shard = w_in.shape[2]
    out_shard = w_out.shape[1]
    assert in_shard * N_DEV == IN_WIDTH and out_shard * N_DEV == D_MODEL and seq % BLOCK == 0

    w_in_t, m_w_in_t, v_w_in_t = (jnp.swapaxes(a, 1, 2) for a in (w_in, m_w_in, v_w_in))
    w_in16, w_out16 = w_in_t.astype(BF16), w_out.astype(BF16)

    def gather_start(layer, not_before, pre_g):
        zones = [lax.empty((N_DEV, in_shard, D_MODEL), BF16), lax.empty((N_DEV, out_shard, D_MODEL), BF16)]
        sems, sends, zones, pre_g = _exchange_start(
            [w_in16[layer], w_out16[layer]], zones, [0, 0], f"weights_send_{layer}", pre_g, after=not_before)
        return (sems, sends, [0, 0]), zones, pre_g

    def full_weights(gin, gout):
        return gin.reshape(IN_WIDTH, D_MODEL), gout.reshape(D_MODEL, D_MODEL)

    def gather_wait(layer, started, zones, after):
        return full_weights(*_exchange_wait([started], zones, after, f"weights_wait_{layer}"))

    causal = jnp.tril(jnp.ones((CHUNK, CHUNK), dtype=bool))
    wm = jnp.where(causal, w_spatial, 0.0).astype(BF16)
    wmt = _hbm(jnp.swapaxes(wm, -1, -2))
    wm = _hbm(wm)
    bsb = _hbm(jnp.repeat(jnp.swapaxes(b_spatial, -1, -2), HEAD_DIM, axis=-1))
    post_g3 = _hbm(post_norm_g.reshape(depth, 1, D_MODEL))
    ln_g3 = _hbm(ln_v_g.reshape(depth, 1, A_WIDTH))
    ln_b3 = _hbm(ln_v_b.reshape(depth, 1, A_WIDTH))
    bucket = jnp.asarray(_bucket_table())
    bias = _bias_table(rel_bias, bucket)

    xs, saved, weights = [x.reshape(tokens, D_MODEL)], [], []
    pending = None
    for layer in range(depth):
        pre_g_fwd = pre_norm_g.reshape(depth, 1, D_MODEL)
        if layer == 0:
            (gin,) = _all_gather([w_in16[0]], "weights_gather_0")
            w = gin.reshape(IN_WIDTH, D_MODEL)
            out0 = _exchange_start([w_out16[0]], [lax.empty((N_DEV, out_shard, D_MODEL), BF16)], [0],
                                   "weights_send_out_0", pre_g_fwd, after=w)
            pre_g_fwd = out0[3]
        else:
            w, wo = gather_wait(layer, pending[0], pending[1], xs[-1])
        if layer + 1 < depth:
            pending = gather_start(layer + 1, w, pre_g_fwd)
            pre_g_fwd = pending[2]
        h, pa, pb, ya, kept, rstd_w = _inproj_gmlp_fwd(xs[-1], _hbm(pre_g_fwd), w, ln_g3, ln_b3, wm, bsb, layer)
        if layer == 0:
            (gout,) = _exchange_wait([(out0[0], out0[1], [0])], out0[2], h, "weights_wait_out_0")
            wo = gout.reshape(D_MODEL, D_MODEL)
        if layer + 1 < depth:
            yb, ob, pw, spw, y, x_next = _attn_outproj_fwd(pb, bias, sinks, ya, wo, xs[-1], post_g3, b_loc, layer)
            xs.append(x_next)
        else:
            yb, ob, pw, spw, y, g, loss_part = _attn_outproj_fwd(pb, bias, sinks, ya, wo, xs[-1], post_g3, b_loc, layer,
                                                    target=loss_target.reshape(tokens, D_MODEL))
        saved.append((h, pa, pb, ya, yb, y, ob, pw, spw, kept, rstd_w))
        weights.append((w, wo))

    grads = {name: [None] * depth for name in ("pre_norm_g", "post_norm_g", "ln_v_g", "ln_v_b", "b_spatial", "sinks")}
    zone_in = lax.empty((N_DEV, depth * in_shard, D_MODEL), BF16)
    zone_out = lax.empty((N_DEV, depth * out_shard, D_MODEL), BF16)
    zone_ws = lax.empty((N_DEV, depth * A_GROUPS * CHUNK, CHUNK), F32)
    started_in, started_out = [], []
    dbias = jnp.zeros((B_HEADS, BLOCK, BLOCK), F32)
    for layer in reversed(range(depth)):
        h, pa, pb, ya, yb, y, ob, pw, spw, kept, rstd_w = saved[layer]
        w, wo = weights[layer]
        dya, dyb, dwo, dpost = _outproj_bwd(g, y, ya, yb, wo, post_g3, layer)
        send_out = dwo.reshape(N_DEV, out_shard, D_MODEL)
        sems, sends, (zone_out,), ln_g_bwd = _exchange_start(
            [send_out], [zone_out], [layer * out_shard], f"grads_send_out_{layer}", ln_v_g.reshape(depth, 1, A_WIDTH))
        started_out.append((sems, sends, [layer * out_shard]))
        dqz, dkv, dbias, dsink = _attn_bwd(pb, dyb, ob, pw, spw, dbias, layer)
        da, dw, dws, dbs, dlg, dlb = _gmlp_bwd_dw(pa, kept, rstd_w, dya, h, dqz, dkv, _hbm(ln_g_bwd), ln_b3, wmt, layer)
        send_in = dw.reshape(N_DEV, in_shard, D_MODEL)
        row_offs = [layer * A_GROUPS * CHUNK, layer * in_shard]
        sems, sends, (zone_ws, zone_in), pre_g_bwd = _exchange_start(
            [dws.reshape(A_GROUPS * CHUNK, CHUNK), send_in], [zone_ws, zone_in], row_offs, f"grads_send_in_{layer}",
            pre_norm_g.reshape(depth, 1, D_MODEL))
        started_in.append((sems, sends, row_offs))
        g, dpre = _inproj_bwd_dx(da, dqz, dkv, w, xs[layer], g, _hbm(pre_g_bwd), layer)
        grads["b_spatial"][layer] = dbs[:, :A_GROUPS].T
        grads["ln_v_g"][layer] = dlg[0]
        grads["ln_v_b"][layer] = dlb[0]
        grads["sinks"][layer] = dsink[0, :B_HEADS]
        grads["pre_norm_g"][layer] = dpre[0]
        grads["post_norm_g"][layer] = dpost[0]
    grad_x = g.reshape(x.shape)
    col_bucket = jnp.asarray(np.broadcast_to(_bucket_table()[0:1, ::-1], (B_HEADS, BLOCK)))
    flip = jnp.asarray(np.eye(BLOCK, dtype=np.float32)[::-1], dtype=BF16)
    drel = _rel_bias_grad(dbias, col_bucket, flip)[:, :REL_BUCKETS].T

    (recv_out,) = _exchange_wait(started_out, [zone_out], g, "grads_wait_out")
    res_out = _reduce_adamw(recv_out, w_out.reshape(-1, D_MODEL), m_w_out.reshape(-1, D_MODEL),
                            v_w_out.reshape(-1, D_MODEL), "adamw_w_out")
    res_out = [r.reshape(w_out.shape) for r in res_out]

    no_state = jnp.zeros((1,), F32)
    small_w = dict(pre_norm_g=pre_norm_g, post_norm_g=post_norm_g, ln_v_g=ln_v_g,
                   ln_v_b=ln_v_b, b_spatial=b_spatial, sinks=sinks, rel_bias=rel_bias, loss=no_state)
    small_m = dict(pre_norm_g=m_pre_norm_g, post_norm_g=m_post_norm_g, ln_v_g=m_ln_v_g,
                   ln_v_b=m_ln_v_b, b_spatial=m_b_spatial, sinks=m_sinks, rel_bias=m_rel_bias, loss=no_state)
    small_v = dict(pre_norm_g=v_pre_norm_g, post_norm_g=v_post_norm_g, ln_v_g=v_ln_v_g,
                   ln_v_b=v_ln_v_b, b_spatial=v_b_spatial, sinks=v_sinks, rel_bias=v_rel_bias, loss=no_state)
    small_g = {name: jnp.stack(grads[name]) for name in _SMALL if name not in ("rel_bias", "loss")}
    small_g["rel_bias"] = drel
    small_g["loss"] = loss_part[0, :1]
    shapes = {name: small_w[name].shape for name in _SMALL}
    slots = _all_gather_direct(_pack_small(small_g), "small_grads_all_gather")
    res_small = _reduce_adamw(slots, _pack_small(small_w), _pack_small(small_m), _pack_small(small_v), "adamw_small")

    recv_ws, recv_in = _exchange_wait(started_in, [zone_ws, zone_in], res_small[0], "grads_wait_in")
    res_ws = _reduce_adamw(recv_ws, w_spatial.reshape(-1, CHUNK), m_w_spatial.reshape(-1, CHUNK),
                           v_w_spatial.reshape(-1, CHUNK), "adamw_w_spatial")
    res_ws = [r.reshape(w_spatial.shape) for r in res_ws]
    res_in = _reduce_adamw(recv_in, w_in_t.reshape(-1, D_MODEL), m_w_in_t.reshape(-1, D_MODEL),
                           v_w_in_t.reshape(-1, D_MODEL), "adamw_w_in")
    res_in = [jnp.swapaxes(r.reshape(w_in_t.shape), 1, 2) for r in res_in]
    res_small = [_unpack_small(r, shapes) for r in res_small]

    order = ("pre_norm_g", "w_in", "ln_v_g", "ln_v_b", "w_spatial", "b_spatial", "sinks", "rel_bias", "w_out",
             "post_norm_g")
    outs = [res_small[0]["loss"][0], grad_x]
    for kind in range(4):
        for name in order:
            if name == "w_in":
                outs.append(res_in[kind])
            elif name == "w_out":
                outs.append(res_out[kind])
            elif name == "w_spatial":
                outs.append(res_ws[kind])
            else:
                outs.append(res_small[kind][name])
    return tuple(outs)
```

```python
import math

import numpy as np
import jax
import jax.numpy as jnp
from jax import lax
from jax.experimental import pallas as pl
from jax.experimental.pallas import tpu as pltpu

F32 = jnp.float32
BF16 = jnp.bfloat16

D_MODEL = 1024
A_WIDTH = 512
A_GROUPS = 8
CHUNK = 128
B_HEADS = 8
HEAD_DIM = 64
B_WIDTH = 512
KV_WIDTH = 128
BLOCK = 128
REL_BUCKETS = 32
REL_MAX_DIST = 128
NORM_EPS = 1e-6
PA_WIDTH = 3 * A_WIDTH
PB_WIDTH = 2 * B_WIDTH + 2 * KV_WIDTH
IN_WIDTH = PA_WIDTH + PB_WIDTH
K_OFF, BZ_OFF = B_WIDTH, B_WIDTH + 2 * KV_WIDTH
KEPT_WIDTH = 5 * A_WIDTH
SCALE = HEAD_DIM ** -0.5
NEG = -1e30
N_DEV = 8
LANES = 128

ADAM_LR = 0.001
ADAM_B1 = 0.9
ADAM_B2 = 0.999
ADAM_EPS = 1e-08
ADAM_WD = 0.01
ADAM_STEP = 10

V7X_VMEM_BYTES = 64 * 1024 * 1024
VMEM_TEMP_BYTES = 12 * 1024 * 1024
MESH = pl.DeviceIdType.MESH


def _vmem_limit(block_bytes, scratch_bytes=0):
    need = 2 * sum(block_bytes) + scratch_bytes + VMEM_TEMP_BYTES
    return int(min(need, V7X_VMEM_BYTES - 8 * 1024 * 1024))


def _nbytes(shape, dtype):
    return int(np.prod(shape)) * jnp.dtype(dtype).itemsize


def _token_tile(tokens, cap=512):
    tile = min(cap, tokens // 2)
    assert tokens % tile == 0 and tile % CHUNK == 0, tokens
    return tile


def _row_tile(rows, cap=512):
    best = 8
    for t in range(8, cap + 1, 8):
        if rows % t == 0:
            best = t
    assert rows % best == 0, rows
    return best


def _mm(a, b):
    return lax.dot_general(a, b, (((1,), (0,)), ((), ())), preferred_element_type=F32)


def _mm_nt(a, b):
    return lax.dot_general(a, b, (((1,), (1,)), ((), ())), preferred_element_type=F32)


def _mm_tn(a, b):
    return lax.dot_general(a, b, (((0,), (0,)), ((), ())), preferred_element_type=F32)


_GELU_C = math.sqrt(2.0 / math.pi)


_GELU_A = _GELU_C * 0.044715


def _gelu_parts(x):
    x2 = x * x
    t = jnp.tanh(x * (_GELU_C + _GELU_A * x2))
    return x2, t, 0.5 + 0.5 * t


def _gelu_and_grad(x):
    x2, t, half_plus = _gelu_parts(x)
    grad = half_plus + (0.5 * x) * (1.0 - t * t) * (_GELU_C + (3.0 * _GELU_A) * x2)
    return x * half_plus, grad


def _sigmoid(x):
    return 0.5 + 0.5 * jnp.tanh(0.5 * x)


def _bucket_table():
    q = np.arange(BLOCK)[:, None]
    k = np.arange(BLOCK)[None, :]
    dist = np.where(k <= q, q - k, q + BLOCK - k)
    max_exact = REL_BUCKETS // 2
    safe = np.maximum(dist, 1).astype(np.float32)
    large = max_exact + (np.log(safe / np.float32(max_exact)) / np.float32(math.log(REL_MAX_DIST / max_exact))
                         * np.float32(REL_BUCKETS - max_exact)).astype(np.int32)
    large = np.minimum(large, REL_BUCKETS - 1)
    assert dist.min() >= 0 and dist.max() < BLOCK
    return np.where(dist < max_exact, dist, large).astype(np.int32)


def _hbm(x):
    return pltpu.with_memory_space_constraint(x, pltpu.HBM)


def _slot(px, py, pc):
    return 4 * px + 2 * py + pc


def _all_gather(arrs, name):
    n = len(arrs)

    def body(*refs):
        ins, outs = refs[:n], refs[n:2 * n]
        send_sems, recv_sems, local_sems = refs[2 * n:]
        x, y, c = lax.axis_index("x"), lax.axis_index("y"), lax.axis_index("c")
        me, sibling = (x, y, c), (x, y, 1 - c)
        chips = [(1 - x, y), (x, 1 - y), (1 - x, 1 - y)]

        def copy(a, k, block, to, src=None):
            dst = outs[a].at[_slot(*block)]
            return pltpu.make_async_remote_copy(
                src_ref=dst if src is None else src, dst_ref=dst,
                send_sem=send_sems.at[7 * a + k], recv_sem=recv_sems.at[7 * a + k],
                device_id=to, device_id_type=MESH)

        mine = [pltpu.make_async_copy(ins[a], outs[a].at[_slot(*me)], local_sems.at[a]) for a in range(n)]
        for cp in mine:
            cp.start()
        first = []
        for a in range(n):
            first.append(copy(a, 0, me, sibling, src=ins[a]))
            first += [copy(a, 1 + j, me, (*chip, c), src=ins[a]) for j, chip in enumerate(chips)]
        for cp in first:
            cp.start()
        passed = []
        for j, chip in enumerate(chips):
            for a in range(n):
                copy(a, 1 + j, (*chip, c), me).wait_recv()
                fwd = copy(a, 4 + j, (*chip, c), sibling)
                fwd.start()
                passed.append(fwd)
        for a in range(n):
            copy(a, 0, sibling, me).wait_recv()
            for j, chip in enumerate(chips):
                copy(a, 4 + j, (*chip, 1 - c), me).wait_recv()
        for cp in first + passed:
            cp.wait_send()
        for cp in mine:
            cp.wait()

    any_spec = pl.BlockSpec(memory_space=pl.ANY)
    return pl.pallas_call(
        body, name=name,
        out_shape=[jax.ShapeDtypeStruct((N_DEV,) + a.shape, a.dtype) for a in arrs],
        in_specs=[any_spec] * n, out_specs=[any_spec] * n,
        scratch_shapes=[pltpu.SemaphoreType.DMA((7 * n,)), pltpu.SemaphoreType.DMA((7 * n,)),
                        pltpu.SemaphoreType.DMA((n,))],
    )(*arrs)


def _all_gather_direct(arr, name):
    def body(in_ref, out_ref, send_sems, recv_sems, local_sem):
        me, peers = _peers()
        mine = pltpu.make_async_copy(in_ref, out_ref.at[_slot(*me)], local_sem)
        mine.start()

        def copy(k, origin, to):
            return pltpu.make_async_remote_copy(
                src_ref=in_ref, dst_ref=out_ref.at[_slot(*origin)], send_sem=send_sems.at[k],
                recv_sem=recv_sems.at[k], device_id=to, device_id_type=MESH)

        sends = [copy(k, me, peer) for k, peer in enumerate(peers)]
        for cp in sends:
            cp.start()
        for k, peer in enumerate(peers):
            copy(k, peer, me).wait_recv()
        for cp in sends:
            cp.wait_send()
        mine.wait()

    any_spec = pl.BlockSpec(memory_space=pl.ANY)
    return pl.pallas_call(
        body, name=name,
        out_shape=jax.ShapeDtypeStruct((N_DEV,) + arr.shape, arr.dtype),
        in_specs=[any_spec], out_specs=any_spec,
        scratch_shapes=[pltpu.SemaphoreType.DMA((7,)), pltpu.SemaphoreType.DMA((7,)), pltpu.SemaphoreType.DMA],
    )(arr)


_HBM_SPEC = pl.BlockSpec(memory_space=pltpu.HBM)
_SEM_SPEC = pl.BlockSpec(memory_space=pltpu.SEMAPHORE)
_DATAFLOW = pltpu.SideEffectType.DATAFLOW_SIDE_EFFECTING


def _peers():
    x, y, c = lax.axis_index("x"), lax.axis_index("y"), lax.axis_index("c")
    peers = []
    for k in range(1, N_DEV):
        fx, fy, fc = (k >> 2) & 1, (k >> 1) & 1, k & 1
        peers.append((1 - x if fx else x, 1 - y if fy else y, 1 - c if fc else c))
    return (x, y, c), peers


def _exchange_copy(send_ref, land_ref, row_off, src_slot, dst_slot, sems, idx, peer):
    rows = send_ref.shape[-2]
    src = send_ref.at[src_slot] if len(send_ref.shape) == 3 else send_ref
    return pltpu.make_async_remote_copy(
        src_ref=src, dst_ref=land_ref.at[dst_slot, pl.ds(row_off, rows), :],
        send_sem=sems[0].at[idx], recv_sem=sems[1].at[idx], device_id=peer, device_id_type=MESH)


def _own_copy(send_ref, land_ref, row_off, slot, sem):
    rows = send_ref.shape[-2]
    src = send_ref.at[slot] if len(send_ref.shape) == 3 else send_ref
    return pltpu.make_async_copy(src, land_ref.at[slot, pl.ds(row_off, rows), :], sem)


def _exchange_start(sends, lands, row_offs, name, carry, after=None):
    n = len(sends)

    def body(*refs):
        ins, zones, carry_ref = refs[:n], refs[n:2 * n], refs[2 * n]
        first_out = 2 * n + 1 + (after is not None)
        sems = refs[first_out:first_out + 3]
        carry_out = refs[-1]
        me, peers = _peers()
        for a in range(n):
            for k, peer in enumerate(peers):
                _exchange_copy(ins[a], zones[a], row_offs[a], _slot(*peer), _slot(*me), sems, 7 * a + k, peer).start()
            _own_copy(ins[a], zones[a], row_offs[a], _slot(*me), sems[2].at[a]).start()
        carry_out[...] = carry_ref[...]

    arrays = [_hbm(a) for a in list(sends) + list(lands)]
    vmem = pl.BlockSpec(memory_space=pltpu.VMEM)
    out = pl.pallas_call(
        body, name=name,
        out_shape=(pltpu.SemaphoreType.DMA((7 * n,)), pltpu.SemaphoreType.DMA((7 * n,)), pltpu.SemaphoreType.DMA((n,)),
                   *[pltpu.HBM(a.shape, a.dtype) for a in arrays], jax.ShapeDtypeStruct(carry.shape, carry.dtype)),
        in_specs=[_HBM_SPEC] * (2 * n) + [vmem] + ([pl.BlockSpec(memory_space=pl.ANY)] if after is not None else []),
        out_specs=(_SEM_SPEC, _SEM_SPEC, _SEM_SPEC, *[_HBM_SPEC] * (2 * n), vmem),
        input_output_aliases={i: 3 + i for i in range(2 * n)},
        compiler_params=pltpu.CompilerParams(has_side_effects=_DATAFLOW),
    )(*arrays, carry, *([after] if after is not None else []))
    return (out[0], out[1], out[2]), list(out[3:3 + n]), list(out[3 + n:3 + 2 * n]), out[-1]


def _exchange_wait(started, lands, after, name):
    n = len(lands)
    flat_sends = [s for _, sends, _ in started for s in sends]
    flat_sems = [s for sems, _, _ in started for s in sems]
    ns = len(flat_sends)

    def body(*refs):
        ins, zones = refs[:ns], refs[ns:ns + n]
        sem_refs = refs[ns + n:ns + n + len(flat_sems)]
        me, peers = _peers()
        pos = 0
        for call, (_, sends, row_offs) in enumerate(started):
            sems = sem_refs[3 * call:3 * call + 3]
            for a in range(len(sends)):
                for k, peer in enumerate(peers):
                    cp = _exchange_copy(ins[pos + a], zones[a], row_offs[a], _slot(*peer), _slot(*peer), sems,
                                        7 * a + k, peer)
                    cp.wait_send()
                    cp.wait_recv()
                _own_copy(ins[pos + a], zones[a], row_offs[a], _slot(*me), sems[2].at[a]).wait()
            pos += len(sends)

    arrays = list(flat_sends) + list(lands)
    out = pl.pallas_call(
        body, name=name,
        out_shape=tuple(pltpu.HBM(a.shape, a.dtype) for a in arrays),
        in_specs=[_HBM_SPEC] * len(arrays) + [_SEM_SPEC] * len(flat_sems) + [pl.BlockSpec(memory_space=pl.ANY)],
        out_specs=tuple([_HBM_SPEC] * len(arrays)),
        input_output_aliases={i: i for i in range(len(arrays))},
        compiler_params=pltpu.CompilerParams(has_side_effects=_DATAFLOW),
    )(*arrays, *flat_sems, after)
    return list(out[ns:])


def _bias_table(rel_bias, bucket):
    def body(rel_ref, bucket_ref, out_ref):
        bk = bucket_ref[...]
        for h in range(B_HEADS):
            def pick(b, acc, h=h):
                return jnp.where(bk == b, rel_ref[b, h], acc)
            out_ref[h] = lax.fori_loop(0, REL_BUCKETS, pick, jnp.zeros((BLOCK, BLOCK), F32))

    return pl.pallas_call(
        body, name="bias_table",
        out_shape=jax.ShapeDtypeStruct((B_HEADS, BLOCK, BLOCK), F32),
        in_specs=[pl.BlockSpec(memory_space=pltpu.SMEM), pl.BlockSpec(memory_space=pltpu.VMEM)],
        out_specs=pl.BlockSpec(memory_space=pltpu.VMEM),
    )(rel_bias, bucket)


def _inproj_gmlp_fwd(x2, pre_g3, w, ln_g3, ln_b3, wm, bsb, layer):
    tokens = x2.shape[0]
    tm = _token_tile(tokens)

    def body(x_ref, g_ref, w_ref, lg_ref, lb_ref, wm_ref, bsb_ref, h_ref, pa_ref, pb_ref, ya_ref, kept_ref, rstd_ref):
        x = x_ref[...]
        r = lax.rsqrt(jnp.mean(x * x, axis=-1, keepdims=True) + NORM_EPS)
        h = (x * r * g_ref[...]).astype(BF16)
        h_ref[...] = h
        pa_ref[...] = _mm_nt(h, w_ref[0:PA_WIDTH, :]).astype(BF16)
        pb_ref[...] = _mm_nt(h, w_ref[PA_WIDTH:IN_WIDTH, :]).astype(BF16)
        for ci in range(tm // CHUNK):
            rows = slice(ci * CHUNK, (ci + 1) * CHUNK)
            gu, gv, pz, u, vhat, rstd, _, mixed, sg = _gmlp_forward_chunk(
                pa_ref[rows, :], lg_ref[...], lb_ref[...], wm_ref, bsb_ref[...])
            ya_ref[rows, :] = (u * mixed * (pz * sg)).astype(BF16)
            for k, val in enumerate((u, gu, gv, vhat, mixed)):
                kept_ref[rows, A_WIDTH * k:A_WIDTH * (k + 1)] = val.astype(BF16)
            rstd_ref[rows, :] = jnp.broadcast_to(rstd, (CHUNK, LANES))

    blocks = [_nbytes((tm, D_MODEL), F32), _nbytes((D_MODEL, IN_WIDTH), BF16),
              _nbytes((tm, D_MODEL), BF16), _nbytes((tm, PA_WIDTH), BF16), _nbytes((tm, PB_WIDTH), BF16),
              _nbytes((A_GROUPS, CHUNK, CHUNK), BF16), _nbytes((CHUNK, A_WIDTH), F32), _nbytes((tm, A_WIDTH), BF16)]
    return pl.pallas_call(
        body, name=f"inproj_gmlp_fwd_{layer}", grid=(tokens // tm,),
        in_specs=[pl.BlockSpec((tm, D_MODEL), lambda i: (i, 0)),
                  pl.BlockSpec((None, 1, D_MODEL), lambda i: (layer, 0, 0)),
                  pl.BlockSpec((IN_WIDTH, D_MODEL), lambda i: (0, 0)),
                  pl.BlockSpec((None, 1, A_WIDTH), lambda i: (layer, 0, 0)),
                  pl.BlockSpec((None, 1, A_WIDTH), lambda i: (layer, 0, 0)),
                  pl.BlockSpec((None, A_GROUPS, CHUNK, CHUNK), lambda i: (layer, 0, 0, 0)),
                  pl.BlockSpec((None, CHUNK, A_WIDTH), lambda i: (layer, 0, 0))],
        out_specs=[pl.BlockSpec((tm, D_MODEL), lambda i: (i, 0)),
                   pl.BlockSpec((tm, PA_WIDTH), lambda i: (i, 0)),
                   pl.BlockSpec((tm, PB_WIDTH), lambda i: (i, 0)),
                   pl.BlockSpec((tm, A_WIDTH), lambda i: (i, 0)),
                   pl.BlockSpec((tm, KEPT_WIDTH), lambda i: (i, 0)),
                   pl.BlockSpec((tm, LANES), lambda i: (i, 0))],
        out_shape=[pltpu.HBM((tokens, D_MODEL), BF16),
                   pltpu.HBM((tokens, PA_WIDTH), BF16),
                   pltpu.HBM((tokens, PB_WIDTH), BF16),
                   pltpu.HBM((tokens, A_WIDTH), BF16),
                   pltpu.HBM((tokens, KEPT_WIDTH), BF16),
                   pltpu.HBM((tokens, LANES), F32)],
        compiler_params=pltpu.CompilerParams(dimension_semantics=("parallel",),
                                             vmem_limit_bytes=_vmem_limit(blocks + [_nbytes((tm, KEPT_WIDTH), BF16),
                                                                                    _nbytes((tm, LANES), F32)])),
    )(_hbm(x2), pre_g3, _hbm(w), ln_g3, ln_b3, wm, bsb)


def _gmlp_forward_chunk(pa, ln_g, ln_b, wm_ref, bsb):
    pu = pa[:, 0:A_WIDTH].astype(F32)
    pv = pa[:, A_WIDTH:2 * A_WIDTH].astype(F32)
    pz = pa[:, 2 * A_WIDTH:3 * A_WIDTH].astype(F32)
    u, gu = _gelu_and_grad(pu)
    vv, gv = _gelu_and_grad(pv)
    mu = jnp.mean(vv, axis=-1, keepdims=True)
    xc = vv - mu
    rstd = lax.rsqrt(jnp.mean(xc * xc, axis=-1, keepdims=True) + NORM_EPS)
    vhat = xc * rstd
    vnb = (vhat * ln_g + ln_b).astype(BF16)
    low = lax.broadcasted_iota(jnp.int32, (CHUNK, LANES), 1) < HEAD_DIM
    parts = []
    for p in range(A_GROUPS // 2):
        vp = vnb[:, LANES * p:LANES * (p + 1)]
        parts.append(jnp.where(low, _mm(wm_ref[2 * p], vp), _mm(wm_ref[2 * p + 1], vp)))
    mixed = jnp.concatenate(parts, axis=1) + bsb
    sg = _sigmoid(pz)
    return gu, gv, pz, u, vhat, rstd, vnb, mixed, sg


def _kv_variants(kv):
    t = kv.astype(F32)
    rolled = pltpu.roll(t, HEAD_DIM, 1)
    low = lax.broadcasted_iota(jnp.int32, t.shape, 1) < HEAD_DIM
    zero = jnp.zeros_like(t)
    head0 = (jnp.where(low, t, zero).astype(BF16), jnp.where(low, zero, rolled).astype(BF16))
    head1 = (jnp.where(low, rolled, zero).astype(BF16), jnp.where(low, zero, t).astype(BF16))
    return head0, head1


def _band_masks():
    row = lax.broadcasted_iota(jnp.int32, (BLOCK, BLOCK), 0)
    col = lax.broadcasted_iota(jnp.int32, (BLOCK, BLOCK), 1)
    return col <= row


def _wrap(full, tri):
    return jnp.where(tri, full[:, BLOCK:2 * BLOCK], full[:, 0:BLOCK])


def _attn_probs(sf, bias_h, sink, tri, kill):
    s = _wrap(sf, tri) + bias_h
    s = jnp.where(kill, NEG, s)
    m = jnp.maximum(jnp.max(s, axis=-1, keepdims=True), sink)
    e = jnp.exp(s - m)
    es = jnp.exp(sink - m)
    inv = 1.0 / (jnp.sum(e, axis=-1, keepdims=True) + es)
    return e * inv, es * inv


def _unwrap16(p, tri):
    p = p.astype(BF16)
    zero = jnp.zeros_like(p)
    return jnp.concatenate([jnp.where(tri, zero, p), jnp.where(tri, p, zero)], axis=1)


def _fill_kv(kv_ref, prev_ref, cur_ref):
    kv_ref[0:BLOCK, :] = prev_ref[...]
    kv_ref[BLOCK:, :] = cur_ref[:, K_OFF:K_OFF + 2 * KV_WIDTH]


def _attn_outproj_fwd(pb, bias, sinks, ya, wo, x2, post_g3, b_loc, layer, target=None):
    tokens = pb.shape[0]
    nb = tokens // b_loc // BLOCK
    tq = _token_tile(tokens)
    per_tile = tq // BLOCK
    steps = tokens // tq
    with_loss = target is not None
    half = D_MODEL // 2

    def body(*refs):
        sink_ref, cur_ref, prev_ref, bias_ref, ya_ref, woa_ref, wob_ref, x_ref, g_ref = refs[:9]
        if with_loss:
            t_ref, yb_ref, ob_ref, pw_ref, spw_ref, y_ref, grad_ref, loss_ref, kv_ref, acc_ref = refs[9:]
        else:
            yb_ref, ob_ref, pw_ref, spw_ref, y_ref, xn_ref, kv_ref = refs[9:]
        t = pl.program_id(0)
        _fill_kv(kv_ref, prev_ref, cur_ref)
        tri = _band_masks()
        head_lane = lax.broadcasted_iota(jnp.int32, (BLOCK, LANES), 1)

        def block(i, carry):
            start = i * BLOCK
            rows = pl.ds(start, BLOCK)
            first = lax.rem(t * per_tile + i, nb) == 0
            kill = jnp.logical_and(first, jnp.logical_not(tri))
            kv = kv_ref[pl.ds(start, 2 * BLOCK), :]
            k_ops = _kv_variants(kv[:, 0:KV_WIDTH])
            v_ops = _kv_variants(kv[:, KV_WIDTH:2 * KV_WIDTH])
            p16 = {}
            sink_cols = jnp.zeros((BLOCK, LANES), F32)
            for kvh in range(2):
                pairs = (2 * kvh, 2 * kvh + 1)
                qs = jnp.concatenate([cur_ref[rows, LANES * p:LANES * (p + 1)] for p in pairs], axis=0) * SCALE
                for j in range(2):
                    sf = _mm_nt(qs, k_ops[kvh][j])
                    for r, p in enumerate(pairs):
                        hd = 2 * p + j
                        probs, sink_p = _attn_probs(sf[BLOCK * r:BLOCK * (r + 1)], bias_ref[hd],
                                                    sink_ref[layer, hd], tri, kill)
                        probs = probs.astype(BF16)
                        p16[hd] = _unwrap16(probs, tri)
                        pw_ref[rows, LANES * hd:LANES * (hd + 1)] = probs
                        sink_cols = jnp.where(head_lane == hd, sink_p, sink_cols)
            spw_ref[rows, :] = sink_cols
            for kvh in range(2):
                pairs = (2 * kvh, 2 * kvh + 1)
                out = jnp.zeros((2 * BLOCK, LANES), F32)
                for j in range(2):
                    out = out + _mm(jnp.concatenate([p16[2 * p + j] for p in pairs], axis=0), v_ops[kvh][j])
                for r, p in enumerate(pairs):
                    bz = cur_ref[rows, BZ_OFF + LANES * p:BZ_OFF + LANES * (p + 1)].astype(F32)
                    out_p = out[BLOCK * r:BLOCK * (r + 1)]
                    ob_ref[rows, LANES * p:LANES * (p + 1)] = out_p.astype(BF16)
                    yb_ref[rows, LANES * p:LANES * (p + 1)] = (out_p * (bz * _sigmoid(bz))).astype(BF16)
            y = _mm(ya_ref[rows, :], woa_ref[...]) + _mm(yb_ref[rows, :], wob_ref[...])
            r = lax.rsqrt(jnp.mean(y * y, axis=-1, keepdims=True) + NORM_EPS)
            x_next = x_ref[rows, :] + y * r * g_ref[...]
            y_ref[rows, :] = y.astype(BF16)
            if not with_loss:
                xn_ref[rows, :] = x_next
            else:
                err = x_next - t_ref[rows, :]
                grad_ref[rows, :] = err * (1.0 / D_MODEL)
                acc_ref[...] += jnp.sum(err * err, axis=0, keepdims=True)
            return carry

        if with_loss:
            @pl.when(t == 0)
            def _():
                acc_ref[...] = jnp.zeros_like(acc_ref)

        for i in range(per_tile):
            block(i, 0)

        if not with_loss:
            return

        @pl.when(t == steps - 1)
        def _():
            total = jnp.sum(acc_ref[...], axis=-1, keepdims=True) * (0.5 / D_MODEL)
            loss_ref[...] = jnp.broadcast_to(total, loss_ref.shape)

    tile = pl.BlockSpec((tq, D_MODEL), lambda t: (t, 0))
    blocks = [_nbytes((tq, PB_WIDTH), BF16), _nbytes((BLOCK, 2 * KV_WIDTH), BF16),
              _nbytes((B_HEADS, BLOCK, BLOCK), F32), 2 * _nbytes((tq, B_WIDTH), BF16),
              2 * _nbytes((half, D_MODEL), BF16), 3 * _nbytes((tq, D_MODEL), F32), _nbytes((tq, D_MODEL), BF16),
              _nbytes((tq, B_HEADS * BLOCK), BF16), _nbytes((tq, LANES), F32)]
    scratch = _nbytes((tq + BLOCK, 2 * KV_WIDTH), BF16)
    in_specs = [pl.BlockSpec(memory_space=pltpu.SMEM),
                pl.BlockSpec((tq, PB_WIDTH), lambda t: (t, 0)),
                pl.BlockSpec((BLOCK, 2 * KV_WIDTH),
                             lambda t: (jnp.maximum(t * per_tile - 1, 0), K_OFF // (2 * KV_WIDTH))),
                pl.BlockSpec((B_HEADS, BLOCK, BLOCK), lambda t: (0, 0, 0)),
                pl.BlockSpec((tq, half), lambda t: (t, 0)),
                pl.BlockSpec((half, D_MODEL), lambda t: (0, 0)),
                pl.BlockSpec((half, D_MODEL), lambda t: (1, 0)),
                tile,
                pl.BlockSpec((None, 1, D_MODEL), lambda t: (layer, 0, 0))]
    out_specs = [pl.BlockSpec((tq, B_WIDTH), lambda t: (t, 0)), pl.BlockSpec((tq, B_WIDTH), lambda t: (t, 0)),
                 pl.BlockSpec((tq, B_HEADS * BLOCK), lambda t: (t, 0)),
                 pl.BlockSpec((tq, LANES), lambda t: (t, 0)), tile, tile]
    out_shape = [pltpu.HBM((tokens, B_WIDTH), BF16), pltpu.HBM((tokens, B_WIDTH), BF16),
                 pltpu.HBM((tokens, B_HEADS * BLOCK), BF16),
                 pltpu.HBM((tokens, LANES), F32), pltpu.HBM((tokens, D_MODEL), BF16),
                 pltpu.HBM((tokens, D_MODEL), F32)]
    scratch_shapes = [pltpu.VMEM((tq + BLOCK, 2 * KV_WIDTH), BF16)]
    operands = [sinks, _hbm(pb), _hbm(pb), _hbm(bias), _hbm(ya), _hbm(wo), _hbm(wo), _hbm(x2), post_g3]
    if with_loss:
        in_specs.append(tile)
        out_specs.append(pl.BlockSpec((1, LANES), lambda t: (0, 0)))
        out_shape.append(pltpu.HBM((1, LANES), F32))
        scratch_shapes.append(pltpu.VMEM((1, D_MODEL), F32))
        operands.append(_hbm(target))
    return pl.pallas_call(
        body, name=f"attn_outproj_fwd_{layer}", grid=(steps,),
        in_specs=in_specs, out_specs=out_specs, out_shape=out_shape, scratch_shapes=scratch_shapes,
        compiler_params=pltpu.CompilerParams(dimension_semantics=("arbitrary" if with_loss else "parallel",),
                                             vmem_limit_bytes=_vmem_limit(blocks, scratch)),
    )(*operands)


def _outproj_bwd(g, y, ya, yb, wo, post_g3, layer):
    tokens = g.shape[0]
    tm = _token_tile(tokens, cap=1024)
    half = D_MODEL // 2
    steps = tokens // tm

    def body(g_ref, y_ref, ya_ref, yb_ref, woa_ref, wob_ref, pg_ref, dya_ref, dyb_ref, dwo16_ref, dpg_ref, dwo_ref):
        @pl.when(pl.program_id(0) == 0)
        def _():
            dwo_ref[...] = jnp.zeros_like(dwo_ref)
            dpg_ref[...] = jnp.zeros_like(dpg_ref)

        gv = g_ref[...]
        yf = y_ref[...].astype(F32)
        r = lax.rsqrt(jnp.mean(yf * yf, axis=-1, keepdims=True) + NORM_EPS)
        yhat = yf * r
        gg = gv * pg_ref[...]
        dy = r * (gg - yhat * jnp.mean(gg * yhat, axis=-1, keepdims=True))
        dpg_ref[...] += jnp.sum(gv * yhat, axis=0, keepdims=True)
        dyb16 = dy.astype(BF16)
        dya_ref[...] = _mm_nt(dyb16, woa_ref[...]).astype(BF16)
        dyb_ref[...] = _mm_nt(dyb16, wob_ref[...]).astype(BF16)
        dwo_ref[0:half, :] += _mm_tn(ya_ref[...], dyb16)
        dwo_ref[half:D_MODEL, :] += _mm_tn(yb_ref[...], dyb16)

        @pl.when(pl.program_id(0) == steps - 1)
        def _():
            dwo16_ref[...] = dwo_ref[...].astype(BF16)

    blocks = [_nbytes((tm, D_MODEL), F32), _nbytes((tm, D_MODEL), BF16), 4 * _nbytes((tm, half), BF16),
              2 * _nbytes((half, D_MODEL), BF16), _nbytes((D_MODEL, D_MODEL), BF16)]
    return pl.pallas_call(
        body, name=f"outproj_bwd_{layer}", grid=(tokens // tm,),
        in_specs=[pl.BlockSpec((tm, D_MODEL), lambda i: (i, 0)),
                  pl.BlockSpec((tm, D_MODEL), lambda i: (i, 0)),
                  pl.BlockSpec((tm, half), lambda i: (i, 0)),
                  pl.BlockSpec((tm, half), lambda i: (i, 0)),
                  pl.BlockSpec((half, D_MODEL), lambda i: (0, 0)),
                  pl.BlockSpec((half, D_MODEL), lambda i: (1, 0)),
                  pl.BlockSpec((None, 1, D_MODEL), lambda i: (layer, 0, 0))],
        out_specs=[pl.BlockSpec((tm, half), lambda i: (i, 0)),
                   pl.BlockSpec((tm, half), lambda i: (i, 0)),
                   pl.BlockSpec((D_MODEL, D_MODEL), lambda i: (0, 0)),
                   pl.BlockSpec((1, D_MODEL), lambda i: (0, 0))],
        out_shape=[pltpu.HBM((tokens,half), BF16),
                   pltpu.HBM((tokens,half), BF16),
                   pltpu.HBM((D_MODEL,D_MODEL), BF16),
                   pltpu.HBM((1,D_MODEL), F32)],
        scratch_shapes=[pltpu.VMEM((D_MODEL, D_MODEL), F32)],
        compiler_params=pltpu.CompilerParams(dimension_semantics=("arbitrary",),
                                             vmem_limit_bytes=_vmem_limit(blocks, _nbytes((D_MODEL, D_MODEL), F32))),
    )(_hbm(g), _hbm(y), _hbm(ya), _hbm(yb), _hbm(wo), _hbm(wo), post_g3)


def _gmlp_bwd_dw(pa, kept, rstd_w, dya, h, dqz, dkv, ln_g3, ln_b3, wmt, layer):
    tokens = pa.shape[0]
    tc = _token_tile(tokens)
    steps = tokens // tc

    def body(pz_ref, kept_ref, rstd_ref, dya_ref, h_ref, dqz_ref, dkv_ref, lg_ref, lb_ref, wmt_ref,
             da_ref, dw_ref, dws_ref, dbs_ref, dlg_ref, dlb_ref, acc_ref, dbsb_ref):
        i = pl.program_id(0)

        @pl.when(i == 0)
        def _():
            acc_ref[...] = jnp.zeros_like(acc_ref)
            dws_ref[...] = jnp.zeros_like(dws_ref)
            dlg_ref[...] = jnp.zeros_like(dlg_ref)
            dlb_ref[...] = jnp.zeros_like(dlb_ref)
            dbsb_ref[...] = jnp.zeros_like(dbsb_ref)

        ln_g = lg_ref[...]
        low = lax.broadcasted_iota(jnp.int32, (CHUNK, LANES), 1) < HEAD_DIM
        hv = h_ref[...]
        acc_ref[:, PA_WIDTH:PA_WIDTH + B_WIDTH] += _mm_tn(hv, dqz_ref[:, 0:B_WIDTH])
        acc_ref[:, PA_WIDTH + K_OFF:PA_WIDTH + BZ_OFF] += _mm_tn(hv, dkv_ref[...])
        acc_ref[:, PA_WIDTH + BZ_OFF:IN_WIDTH] += _mm_tn(hv, dqz_ref[:, B_WIDTH:2 * B_WIDTH])

        for ci in range(tc // CHUNK):
            rows = slice(ci * CHUNK, (ci + 1) * CHUNK)
            u, gu, gv, vhat, mixed = (kept_ref[rows, A_WIDTH * k:A_WIDTH * (k + 1)].astype(F32) for k in range(5))
            rstd = jnp.concatenate([rstd_ref[rows, :]] * (A_WIDTH // LANES), axis=1)
            vnb = (vhat * ln_g + lb_ref[...]).astype(BF16)
            pz = pz_ref[rows, :].astype(F32)
            sg = _sigmoid(pz)
            dy = dya_ref[rows, :].astype(F32)
            sz = pz * sg
            dy_sz = dy * sz
            du = dy_sz * mixed
            dmixed = dy_sz * u
            dz = dy * (u * mixed) * (sg + sz * (1.0 - sg))
            dbsb_ref[...] += dmixed
            dmb = dmixed.astype(BF16)
            zero = jnp.zeros((CHUNK, LANES), BF16)
            parts = []
            for p in range(A_GROUPS // 2):
                dmp = dmb[:, LANES * p:LANES * (p + 1)]
                vp = vnb[:, LANES * p:LANES * (p + 1)]
                parts.append(jnp.where(low, _mm(wmt_ref[2 * p], dmp), _mm(wmt_ref[2 * p + 1], dmp)))
                dws_ref[2 * p] += _mm_nt(jnp.where(low, dmp, zero), vp)
                dws_ref[2 * p + 1] += _mm_nt(jnp.where(low, zero, dmp), vp)
            dvn = jnp.concatenate(parts, axis=1)
            dlg_ref[...] += jnp.sum(dvn * vhat, axis=0, keepdims=True)
            dlb_ref[...] += jnp.sum(dvn, axis=0, keepdims=True)
            dvh = dvn * ln_g
            dvv = rstd * (dvh - jnp.mean(dvh, axis=-1, keepdims=True)
                          - vhat * jnp.mean(dvh * vhat, axis=-1, keepdims=True))
            da_ref[rows, 0:A_WIDTH] = (du * gu).astype(BF16)
            da_ref[rows, A_WIDTH:2 * A_WIDTH] = (dvv * gv).astype(BF16)
            da_ref[rows, 2 * A_WIDTH:3 * A_WIDTH] = dz.astype(BF16)

        acc_ref[:, 0:PA_WIDTH] += _mm_tn(hv, da_ref[...])

        @pl.when(i == steps - 1)
        def _():
            for c in range(IN_WIDTH // LANES):
                cols = slice(LANES * c, LANES * (c + 1))
                dw_ref[cols, :] = acc_ref[:, cols].T.astype(BF16)
            causal = (lax.broadcasted_iota(jnp.int32, (CHUNK, CHUNK), 0)
                      >= lax.broadcasted_iota(jnp.int32, (CHUNK, CHUNK), 1))
            for h in range(A_GROUPS):
                dws_ref[h] = jnp.where(causal, dws_ref[h], 0.0)
            acc = dbsb_ref[...]
            lane_full = lax.broadcasted_iota(jnp.int32, (CHUNK, A_WIDTH), 1)
            lane_out = lax.broadcasted_iota(jnp.int32, (CHUNK, LANES), 1)
            out = jnp.zeros((CHUNK, LANES), F32)
            for h in range(A_GROUPS):
                in_group = jnp.logical_and(lane_full >= HEAD_DIM * h, lane_full < HEAD_DIM * (h + 1))
                s = jnp.sum(jnp.where(in_group, acc, 0.0), axis=-1, keepdims=True)
                out = jnp.where(lane_out == h, s, out)
            dbs_ref[...] = out

    blocks = [_nbytes((tc, KEPT_WIDTH), BF16), 2 * _nbytes((tc, A_WIDTH), BF16), _nbytes((A_GROUPS, CHUNK, CHUNK), BF16),
              _nbytes((tc, LANES), F32), _nbytes((tc, PA_WIDTH), BF16), _nbytes((A_GROUPS, CHUNK, CHUNK), F32),
              _nbytes((CHUNK, LANES), F32), _nbytes((tc, D_MODEL), BF16), _nbytes((tc, 2 * B_WIDTH), BF16),
              _nbytes((tc, 2 * KV_WIDTH), BF16), _nbytes((D_MODEL, IN_WIDTH), BF16)]
    scratch = _nbytes((D_MODEL, IN_WIDTH), F32) + _nbytes((CHUNK, A_WIDTH), F32)
    in_specs = [pl.BlockSpec((tc, A_WIDTH), lambda i: (i, 2)),
                pl.BlockSpec((tc, KEPT_WIDTH), lambda i: (i, 0)),
                pl.BlockSpec((tc, LANES), lambda i: (i, 0)),
                pl.BlockSpec((tc, A_WIDTH), lambda i: (i, 0)),
                pl.BlockSpec((tc, D_MODEL), lambda i: (i, 0)),
                pl.BlockSpec((tc, 2 * B_WIDTH), lambda i: (i, 0)),
                pl.BlockSpec((pl.Element(tc), pl.Element(2 * KV_WIDTH)),
                             lambda i: (pl.multiple_of(i * tc + BLOCK, BLOCK), 0)),
                pl.BlockSpec((None, 1, A_WIDTH), lambda i: (layer, 0, 0)),
                pl.BlockSpec((None, 1, A_WIDTH), lambda i: (layer, 0, 0)),
                pl.BlockSpec((None, A_GROUPS, CHUNK, CHUNK), lambda i: (layer, 0, 0, 0))]
    operands = [_hbm(pa), _hbm(kept), _hbm(rstd_w), _hbm(dya), _hbm(h), _hbm(dqz), _hbm(dkv), ln_g3, ln_b3, wmt]
    out_specs = [pl.BlockSpec((tc, PA_WIDTH), lambda i: (i, 0)),
                 pl.BlockSpec((IN_WIDTH, D_MODEL), lambda i: (0, 0)),
                 pl.BlockSpec((A_GROUPS, CHUNK, CHUNK), lambda i: (0, 0, 0)),
                 pl.BlockSpec((CHUNK, LANES), lambda i: (0, 0)),
                 pl.BlockSpec((1, A_WIDTH), lambda i: (0, 0)),
                 pl.BlockSpec((1, A_WIDTH), lambda i: (0, 0))]
    out_shape = [pltpu.HBM((tokens, PA_WIDTH), BF16),
                 pltpu.HBM((IN_WIDTH, D_MODEL), BF16),
                 pltpu.HBM((A_GROUPS, CHUNK, CHUNK), F32),
                 pltpu.HBM((CHUNK, LANES), F32),
                 pltpu.HBM((1, A_WIDTH), F32),
                 pltpu.HBM((1, A_WIDTH), F32)]
    scratch_shapes = [pltpu.VMEM((D_MODEL, IN_WIDTH), F32), pltpu.VMEM((CHUNK, A_WIDTH), F32)]
    return pl.pallas_call(
        body, name=f"gmlp_bwd_dw_{layer}", grid=(steps,),
        in_specs=in_specs, out_specs=out_specs, out_shape=out_shape, scratch_shapes=scratch_shapes,
        compiler_params=pltpu.CompilerParams(dimension_semantics=("arbitrary",),
                                             vmem_limit_bytes=_vmem_limit(blocks, scratch)),
    )(*operands)


def _attn_bwd(pb, dyb, out_w, probs_w, sink_w, dbias_in, layer):
    tokens = pb.shape[0]
    qz_width = 2 * B_WIDTH
    tq = _token_tile(tokens)
    per_tile = tq // BLOCK
    nt = tokens // tq

    def body(cur_ref, prev_ref, dyb_ref, ob_ref, pw_ref, spw_ref, dbias_in_ref, dqz_ref, dkv_ref, dbias_ref, dsink_ref,
             kv_ref, acc_ref):
        t = pl.program_id(0)

        @pl.when(t == 0)
        def _():
            dbias_ref[...] = dbias_in_ref[...]
            dsink_ref[...] = jnp.zeros_like(dsink_ref)
            acc_ref[0:BLOCK, :] = jnp.zeros((BLOCK, 2 * KV_WIDTH), F32)

        @pl.when(t < nt)
        def _():
            acc_ref[BLOCK:, :] = jnp.zeros((tq, 2 * KV_WIDTH), F32)
            _fill_kv(kv_ref, prev_ref, cur_ref)
            tri = _band_masks()
            low = lax.broadcasted_iota(jnp.int32, (BLOCK, LANES), 1) < HEAD_DIM
            low_kv = lax.broadcasted_iota(jnp.int32, (2 * BLOCK, LANES), 1) < HEAD_DIM

            head_lane = lax.broadcasted_iota(jnp.int32, (BLOCK, LANES), 1)

            def block(i, carry):
                start = i * BLOCK
                rows = pl.ds(start, BLOCK)
                kv = kv_ref[pl.ds(start, 2 * BLOCK), :]
                k_ops = _kv_variants(kv[:, 0:KV_WIDTH])
                v_ops = _kv_variants(kv[:, KV_WIDTH:2 * KV_WIDTH])
                probs, p16, qs, dos, delta, ds16 = {}, {}, {}, {}, {}, {}
                for hd in range(B_HEADS):
                    kept_p = pw_ref[rows, LANES * hd:LANES * (hd + 1)]
                    probs[hd] = kept_p.astype(F32)
                    p16[hd] = _unwrap16(kept_p, tri)
                for kvh in range(2):
                    qs[kvh] = jnp.concatenate(
                        [cur_ref[rows, LANES * p:LANES * (p + 1)] for p in (2 * kvh, 2 * kvh + 1)], axis=0)
                for kvh in range(2):
                    pairs = (2 * kvh, 2 * kvh + 1)
                    d_outs = []
                    for r, p in enumerate(pairs):
                        bz = cur_ref[rows, BZ_OFF + LANES * p:BZ_OFF + LANES * (p + 1)].astype(F32)
                        sg = _sigmoid(bz)
                        dyp = dyb_ref[rows, LANES * p:LANES * (p + 1)].astype(F32)
                        out_p = ob_ref[rows, LANES * p:LANES * (p + 1)].astype(F32)
                        d_out = dyp * (bz * sg)
                        dqz_ref[rows, B_WIDTH + LANES * p:B_WIDTH + LANES * (p + 1)] = (
                            dyp * out_p * (sg * (1.0 + bz * (1.0 - sg)))).astype(BF16)
                        dod = d_out * out_p
                        delta[2 * p] = jnp.sum(jnp.where(low, dod, 0.0), axis=-1, keepdims=True)
                        delta[2 * p + 1] = jnp.sum(jnp.where(low, 0.0, dod), axis=-1, keepdims=True)
                        d_outs.append(d_out.astype(BF16))
                    dos[kvh] = jnp.concatenate(d_outs, axis=0)
                delta_cols = jnp.zeros((BLOCK, LANES), F32)
                for hd in range(B_HEADS):
                    delta_cols = jnp.where(head_lane == hd, delta[hd], delta_cols)
                dsink_ref[0:1, :] += jnp.sum(-(spw_ref[rows, :] * delta_cols), axis=0, keepdims=True)
                for kvh in range(2):
                    pairs = (2 * kvh, 2 * kvh + 1)
                    for j in range(2):
                        dpf = _mm_nt(dos[kvh], v_ops[kvh][j])
                        for r, p in enumerate(pairs):
                            hd = 2 * p + j
                            ds = probs[hd] * (_wrap(dpf[BLOCK * r:BLOCK * (r + 1)], tri) - delta[hd])
                            dbias_ref[hd] += ds
                            ds16[hd] = _unwrap16(ds, tri)
                dk_acc = [[None, None], [None, None]]
                dv_acc = [[None, None], [None, None]]
                for kvh in range(2):
                    pairs = (2 * kvh, 2 * kvh + 1)
                    dq = jnp.zeros((2 * BLOCK, LANES), F32)
                    for j in range(2):
                        dss = jnp.concatenate([ds16[2 * p + j] for p in pairs], axis=0)
                        pss = jnp.concatenate([p16[2 * p + j] for p in pairs], axis=0)
                        dq = dq + _mm(dss, k_ops[kvh][j])
                        dk_acc[kvh][j] = _mm_tn(dss, qs[kvh])
                        dv_acc[kvh][j] = _mm_tn(pss, dos[kvh])
                    for r, p in enumerate(pairs):
                        dqz_ref[rows, LANES * p:LANES * (p + 1)] = (dq[BLOCK * r:BLOCK * (r + 1)] * SCALE).astype(BF16)

                def fold(acc):
                    return jnp.where(low_kv,
                                     acc[0][0] + pltpu.roll(acc[0][1], HEAD_DIM, 1),
                                     pltpu.roll(acc[1][0], HEAD_DIM, 1) + acc[1][1])

                acc_ref[pl.ds(start, 2 * BLOCK), :] += jnp.concatenate(
                    [fold(dk_acc) * SCALE, fold(dv_acc)], axis=1)
                return carry

            for i in range(per_tile):
                block(i, 0)
            dkv_ref[...] = acc_ref[0:tq, :].astype(BF16)
            acc_ref[0:BLOCK, :] = acc_ref[tq:tq + BLOCK, :]

        @pl.when(t == nt)
        def _():
            dkv_ref[0:BLOCK, :] = acc_ref[0:BLOCK, :].astype(BF16)
            dkv_ref[BLOCK:, :] = jnp.zeros((tq - BLOCK, 2 * KV_WIDTH), BF16)

    def cur_map(t):
        return (jnp.minimum(t, nt - 1), 0)

    def prev_map(t):
        return (jnp.maximum(jnp.minimum(t, nt - 1) * per_tile - 1, 0), K_OFF // (2 * KV_WIDTH))

    blocks = [_nbytes((tq, PB_WIDTH), BF16), _nbytes((BLOCK, 2 * KV_WIDTH), BF16), _nbytes((tq, B_WIDTH), BF16),
              2 * _nbytes((B_HEADS, BLOCK, BLOCK), F32), _nbytes((tq, qz_width), BF16),
              _nbytes((tq, 2 * KV_WIDTH), BF16), _nbytes((B_HEADS, LANES), F32),
              _nbytes((tq, B_HEADS * BLOCK), BF16), _nbytes((tq, LANES), F32)]
    scratch = _nbytes((tq + BLOCK, 2 * KV_WIDTH), BF16) + _nbytes((tq + BLOCK, 2 * KV_WIDTH), F32)
    return pl.pallas_call(
        body, name=f"attn_bwd_{layer}", grid=(nt + 1,),
        in_specs=[pl.BlockSpec((tq, PB_WIDTH), cur_map),
                  pl.BlockSpec((BLOCK, 2 * KV_WIDTH), prev_map),
                  pl.BlockSpec((tq, B_WIDTH), cur_map),
                  pl.BlockSpec((tq, B_WIDTH), cur_map),
                  pl.BlockSpec((tq, B_HEADS * BLOCK), cur_map),
                  pl.BlockSpec((tq, LANES), cur_map),
                  pl.BlockSpec((B_HEADS, BLOCK, BLOCK), lambda t: (0, 0, 0))],
        out_specs=[pl.BlockSpec((tq, qz_width), cur_map),
                   pl.BlockSpec((tq, 2 * KV_WIDTH), lambda t: (t, 0)),
                   pl.BlockSpec((B_HEADS, BLOCK, BLOCK), lambda t: (0, 0, 0)),
                   pl.BlockSpec((B_HEADS, LANES), lambda t: (0, 0))],
        out_shape=[pltpu.HBM((tokens, qz_width), BF16),
                   pltpu.HBM((tokens + tq, 2 * KV_WIDTH), BF16),
                   pltpu.HBM((B_HEADS, BLOCK, BLOCK), F32),
                   pltpu.HBM((B_HEADS, LANES), F32)],
        scratch_shapes=[pltpu.VMEM((tq + BLOCK, 2 * KV_WIDTH), BF16),
                        pltpu.VMEM((tq + BLOCK, 2 * KV_WIDTH), F32)],
        compiler_params=pltpu.CompilerParams(dimension_semantics=("arbitrary",),
                                             vmem_limit_bytes=_vmem_limit(blocks, scratch)),
    )(_hbm(pb), _hbm(pb), _hbm(dyb), _hbm(out_w), _hbm(probs_w), _hbm(sink_w), _hbm(dbias_in))


def _inproj_bwd_dx(da, dqz, dkv, w, x2, g, pre_g3, layer):
    tokens = x2.shape[0]
    tm = _token_tile(tokens)

    steps = tokens // tm

    def body(da_ref, dqz_ref, dkv_ref, w_ref, x_any, g_any, pg_ref, gn_ref, dpg_ref, xbuf, gbuf, sems):
        i = pl.program_id(0)

        def ring_copies(step, slot):
            rows = pl.ds(pl.multiple_of(step * tm, tm), tm)
            return (pltpu.make_async_copy(x_any.at[rows, :], xbuf.at[slot], sems.at[0, slot]),
                    pltpu.make_async_copy(g_any.at[rows, :], gbuf.at[slot], sems.at[1, slot]))

        @pl.when(i == 0)
        def _():
            dpg_ref[...] = jnp.zeros_like(dpg_ref)
            for s in range(min(2, steps)):
                for c in ring_copies(s, s):
                    c.start()

        @pl.when(i + 2 < steps)
        def _():
            for c in ring_copies(i + 2, (i + 2) % 3):
                c.start()

        slot = i % 3
        dh = _mm(da_ref[...], w_ref[0:PA_WIDTH, :])
        dh += _mm(dqz_ref[:, 0:B_WIDTH], w_ref[PA_WIDTH:PA_WIDTH + B_WIDTH, :])
        dh += _mm(dkv_ref[...], w_ref[PA_WIDTH + K_OFF:PA_WIDTH + BZ_OFF, :])
        dh += _mm(dqz_ref[:, B_WIDTH:2 * B_WIDTH], w_ref[PA_WIDTH + BZ_OFF:IN_WIDTH, :])
        for c in ring_copies(i, slot):
            c.wait()
        x = xbuf[slot]
        r = lax.rsqrt(jnp.mean(x * x, axis=-1, keepdims=True) + NORM_EPS)
        xhat = x * r
        dhg = dh * pg_ref[...]
        dpg_ref[...] += jnp.sum(dh * xhat, axis=0, keepdims=True)
        gn_ref[...] = gbuf[slot] + r * (dhg - xhat * jnp.mean(dhg * xhat, axis=-1, keepdims=True))

    blocks = [_nbytes((tm, PA_WIDTH), BF16), _nbytes((tm, 2 * B_WIDTH), BF16), _nbytes((tm, 2 * KV_WIDTH), BF16),
              _nbytes((D_MODEL, IN_WIDTH), BF16), _nbytes((tm, D_MODEL), F32)]
    ring = 6 * _nbytes((tm, D_MODEL), F32)
    return pl.pallas_call(
        body, name=f"inproj_bwd_dx_{layer}", grid=(tokens // tm,),
        in_specs=[pl.BlockSpec((tm, PA_WIDTH), lambda i: (i, 0)),
                  pl.BlockSpec((tm, 2 * B_WIDTH), lambda i: (i, 0)),
                  pl.BlockSpec((pl.Element(tm), pl.Element(2 * KV_WIDTH)), lambda i: (pl.multiple_of(i * tm + BLOCK, BLOCK), 0)),
                  pl.BlockSpec((IN_WIDTH, D_MODEL), lambda i: (0, 0)),
                  pl.BlockSpec(memory_space=pl.ANY),
                  pl.BlockSpec(memory_space=pl.ANY),
                  pl.BlockSpec((None, 1, D_MODEL), lambda i: (layer, 0, 0))],
        out_specs=[pl.BlockSpec((tm, D_MODEL), lambda i: (i, 0)),
                   pl.BlockSpec((1, D_MODEL), lambda i: (0, 0))],
        out_shape=[pltpu.HBM((tokens,D_MODEL), F32),
                   pltpu.HBM((1,D_MODEL), F32)],
        scratch_shapes=[pltpu.VMEM((3, tm, D_MODEL), F32), pltpu.VMEM((3, tm, D_MODEL), F32),
                        pltpu.SemaphoreType.DMA((2, 3))],
        compiler_params=pltpu.CompilerParams(dimension_semantics=("arbitrary",),
                                             vmem_limit_bytes=_vmem_limit(blocks, ring)),
    )(_hbm(da), _hbm(dqz), _hbm(dkv), _hbm(w), _hbm(x2), _hbm(g), pre_g3)


def _rel_bias_grad(dbias, col_bucket, flip):
    def body(db_ref, cb_ref, flip_ref, out_ref):
        cb = cb_ref[...]
        anti = flip_ref[...]
        sums = []
        for h in range(B_HEADS):
            x = db_ref[h]
            hi = x.astype(BF16)
            rest = x - hi.astype(F32)
            mid = rest.astype(BF16)
            low = (rest - mid.astype(F32)).astype(BF16)
            reversed_x = _mm(hi, anti) + _mm(mid, anti) + _mm(low, anti)
            rolled = pltpu.roll(reversed_x, 0, 1, stride=1, stride_axis=0)
            sums.append(jnp.sum(rolled, axis=0, keepdims=True))
        per_dist = jnp.concatenate(sums, axis=0)
        lane = lax.broadcasted_iota(jnp.int32, (B_HEADS, LANES), 1)
        out = jnp.zeros((B_HEADS, LANES), F32)
        for b in range(REL_BUCKETS):
            s = jnp.sum(jnp.where(cb == b, per_dist, 0.0), axis=-1, keepdims=True)
            out = jnp.where(lane == b, s, out)
        out_ref[...] = out

    vm = pl.BlockSpec(memory_space=pltpu.VMEM)
    return pl.pallas_call(
        body, name="rel_bias_grad",
        out_shape=jax.ShapeDtypeStruct((B_HEADS, LANES), F32),
        in_specs=[vm, vm, vm], out_specs=vm,
    )(dbias, col_bucket, flip)


def _reduce_adamw(slots, w, m, v, name):
    rows, cols = w.shape
    tr = _row_tile(rows)
    c1 = 1.0 / (1.0 - ADAM_B1 ** ADAM_STEP)
    c2 = 1.0 / (1.0 - ADAM_B2 ** ADAM_STEP)

    def body(s_ref, w_ref, m_ref, v_ref, g_ref, d_ref, nm_ref, nv_ref):
        g = s_ref[0].astype(F32)
        for i in range(1, N_DEV):
            g = g + s_ref[i].astype(F32)
        nm = ADAM_B1 * m_ref[...] + (1.0 - ADAM_B1) * g
        nv = ADAM_B2 * v_ref[...] + (1.0 - ADAM_B2) * (g * g)
        g_ref[...] = g
        nm_ref[...] = nm
        nv_ref[...] = nv
        d_ref[...] = -ADAM_LR * ((nm * c1) / (jnp.sqrt(nv * c2) + ADAM_EPS) + ADAM_WD * w_ref[...])

    blocks = [_nbytes((N_DEV, tr, cols), slots.dtype), 7 * _nbytes((tr, cols), F32)]
    tile = pl.BlockSpec((tr, cols), lambda i: (i, 0))
    return pl.pallas_call(
        body, name=name, grid=(rows // tr,),
        in_specs=[pl.BlockSpec((N_DEV, tr, cols), lambda i: (0, i, 0)), tile, tile, tile],
        out_specs=[tile] * 4,
        out_shape=[pltpu.HBM((rows, cols), F32)] * 4,
        compiler_params=pltpu.CompilerParams(dimension_semantics=("parallel",),
                                             vmem_limit_bytes=_vmem_limit(blocks)),
    )(_hbm(slots), _hbm(w), _hbm(m), _hbm(v))


_SMALL = ("pre_norm_g", "post_norm_g", "ln_v_g", "ln_v_b", "b_spatial", "sinks", "rel_bias", "loss")


def _pack_small(parts):
    slabs = []
    for name in _SMALL:
        flat = parts[name].astype(F32).reshape(-1)
        pad = (-flat.shape[0]) % (8 * LANES)
        slabs.append(jnp.pad(flat, (0, pad)).reshape(-1, LANES))
    return jnp.concatenate(slabs, axis=0)


def _unpack_small(slab, shapes):
    out, row = {}, 0
    for name in _SMALL:
        size = int(np.prod(shapes[name]))
        rows = -(-size // (8 * LANES)) * 8
        out[name] = slab[row:row + rows].reshape(-1)[:size].reshape(shapes[name])
        row += rows
    return out


def kernel(x, pre_norm_g, w_in, ln_v_g, ln_v_b, w_spatial, b_spatial, sinks, rel_bias, w_out, post_norm_g, loss_target, m_pre_norm_g, m_w_in, m_ln_v_g, m_ln_v_b, m_w_spatial, m_b_spatial, m_sinks, m_rel_bias, m_w_out, m_post_norm_g, v_pre_norm_g, v_w_in, v_ln_v_g, v_ln_v_b, v_w_spatial, v_b_spatial, v_sinks, v_rel_bias, v_w_out, v_post_norm_g):
    b_loc, seq, _ = x.shape
    tokens = b_loc * seq
    depth = w_in.shape[0]
    in_shard = w_in.shape[2]
    out_shard = w_out.shape[1]
    assert in_shard * N_DEV == IN_WIDTH and out_shard * N_DEV == D_MODEL and seq % BLOCK == 0

    w_in_t, m_w_in_t, v_w_in_t = (jnp.swapaxes(a, 1, 2) for a in (w_in, m_w_in, v_w_in))
    w_in16, w_out16 = w_in_t.astype(BF16), w_out.astype(BF16)

    def gather_start(layer, not_before, pre_g):
        zones = [lax.empty((N_DEV, in_shard, D_MODEL), BF16), lax.empty((N_DEV, out_shard, D_MODEL), BF16)]
        sems, sends, zones, pre_g = _exchange_start(
            [w_in16[layer], w_out16[layer]], zones, [0, 0], f"weights_send_{layer}", pre_g, after=not_before)
        return (sems, sends, [0, 0]), zones, pre_g

    def full_weights(gin, gout):
        return gin.reshape(IN_WIDTH, D_MODEL), gout.reshape(D_MODEL, D_MODEL)

    def gather_wait(layer, started, zones, after):
        return full_weights(*_exchange_wait([started], zones, after, f"weights_wait_{layer}"))

    causal = jnp.tril(jnp.ones((CHUNK, CHUNK), dtype=bool))
    wm = jnp.where(causal, w_spatial, 0.0).astype(BF16)
    wmt = _hbm(jnp.swapaxes(wm, -1, -2))
    wm = _hbm(wm)
    bsb = _hbm(jnp.repeat(jnp.swapaxes(b_spatial, -1, -2), HEAD_DIM, axis=-1))
    post_g3 = _hbm(post_norm_g.reshape(depth, 1, D_MODEL))
    ln_g3 = _hbm(ln_v_g.reshape(depth, 1, A_WIDTH))
    ln_b3 = _hbm(ln_v_b.reshape(depth, 1, A_WIDTH))
    bucket = jnp.asarray(_bucket_table())
    bias = _bias_table(rel_bias, bucket)

    xs, saved, weights = [x.reshape(tokens, D_MODEL)], [], []
    pending = None
    for layer in range(depth):
        pre_g_fwd = pre_norm_g.reshape(depth, 1, D_MODEL)
        if layer == 0:
            (gin,) = _all_gather([w_in16[0]], "weights_gather_0")
            w = gin.reshape(IN_WIDTH, D_MODEL)
            out0 = _exchange_start([w_out16[0]], [lax.empty((N_DEV, out_shard, D_MODEL), BF16)], [0],
                                   "weights_send_out_0", pre_g_fwd, after=w)
            pre_g_fwd = out0[3]
        else:
            w, wo = gather_wait(layer, pending[0], pending[1], xs[-1])
        if layer + 1 < depth:
            pending = gather_start(layer + 1, w, pre_g_fwd)
            pre_g_fwd = pending[2]
        h, pa, pb, ya, kept, rstd_w = _inproj_gmlp_fwd(xs[-1], _hbm(pre_g_fwd), w, ln_g3, ln_b3, wm, bsb, layer)
        if layer == 0:
            (gout,) = _exchange_wait([(out0[0], out0[1], [0])], out0[2], h, "weights_wait_out_0")
            wo = gout.reshape(D_MODEL, D_MODEL)
        if layer + 1 < depth:
            yb, ob, pw, spw, y, x_next = _attn_outproj_fwd(pb, bias, sinks, ya, wo, xs[-1], post_g3, b_loc, layer)
            xs.append(x_next)
        else:
            yb, ob, pw, spw, y, g, loss_part = _attn_outproj_fwd(pb, bias, sinks, ya, wo, xs[-1], post_g3, b_loc, layer,
                                                    target=loss_target.reshape(tokens, D_MODEL))
        saved.append((h, pa, pb, ya, yb, y, ob, pw, spw, kept, rstd_w))
        weights.append((w, wo))

    grads = {name: [None] * depth for name in ("pre_norm_g", "post_norm_g", "ln_v_g", "ln_v_b", "b_spatial", "sinks")}
    zone_in = lax.empty((N_DEV, depth * in_shard, D_MODEL), BF16)
    zone_out = lax.empty((N_DEV, depth * out_shard, D_MODEL), BF16)
    zone_ws = lax.empty((N_DEV, depth * A_GROUPS * CHUNK, CHUNK), F32)
    started_in, started_out = [], []
    dbias = jnp.zeros((B_HEADS, BLOCK, BLOCK), F32)
    for layer in reversed(range(depth)):
        h, pa, pb, ya, yb, y, ob, pw, spw, kept, rstd_w = saved[layer]
        w, wo = weights[layer]
        dya, dyb, dwo, dpost = _outproj_bwd(g, y, ya, yb, wo, post_g3, layer)
        send_out = dwo.reshape(N_DEV, out_shard, D_MODEL)
        sems, sends, (zone_out,), ln_g_bwd = _exchange_start(
            [send_out], [zone_out], [layer * out_shard], f"grads_send_out_{layer}", ln_v_g.reshape(depth, 1, A_WIDTH))
        started_out.append((sems, sends, [layer * out_shard]))
        dqz, dkv, dbias, dsink = _attn_bwd(pb, dyb, ob, pw, spw, dbias, layer)
        da, dw, dws, dbs, dlg, dlb = _gmlp_bwd_dw(pa, kept, rstd_w, dya, h, dqz, dkv, _hbm(ln_g_bwd), ln_b3, wmt, layer)
        send_in = dw.reshape(N_DEV, in_shard, D_MODEL)
        row_offs = [layer * A_GROUPS * CHUNK, layer * in_shard]
        sems, sends, (zone_ws, zone_in), pre_g_bwd = _exchange_start(
            [dws.reshape(A_GROUPS * CHUNK, CHUNK), send_in], [zone_ws, zone_in], row_offs, f"grads_send_in_{layer}",
            pre_norm_g.reshape(depth, 1, D_MODEL))
        started_in.append((sems, sends, row_offs))
        g, dpre = _inproj_bwd_dx(da, dqz, dkv, w, xs[layer], g, _hbm(pre_g_bwd), layer)
        grads["b_spatial"][layer] = dbs[:, :A_GROUPS].T
        grads["ln_v_g"][layer] = dlg[0]
        grads["ln_v_b"][layer] = dlb[0]
        grads["sinks"][layer] = dsink[0, :B_HEADS]
        grads["pre_norm_g"][layer] = dpre[0]
        grads["post_norm_g"][layer] = dpost[0]
    grad_x = g.reshape(x.shape)
    col_bucket = jnp.asarray(np.broadcast_to(_bucket_table()[0:1, ::-1], (B_HEADS, BLOCK)))
    flip = jnp.asarray(np.eye(BLOCK, dtype=np.float32)[::-1], dtype=BF16)
    drel = _rel_bias_grad(dbias, col_bucket, flip)[:, :REL_BUCKETS].T

    (recv_out,) = _exchange_wait(started_out, [zone_out], g, "grads_wait_out")
    res_out = _reduce_adamw(recv_out, w_out.reshape(-1, D_MODEL), m_w_out.reshape(-1, D_MODEL),
                            v_w_out.reshape(-1, D_MODEL), "adamw_w_out")
    res_out = [r.reshape(w_out.shape) for r in res_out]

    no_state = jnp.zeros((1,), F32)
    small_w = dict(pre_norm_g=pre_norm_g, post_norm_g=post_norm_g, ln_v_g=ln_v_g,
                   ln_v_b=ln_v_b, b_spatial=b_spatial, sinks=sinks, rel_bias=rel_bias, loss=no_state)
    small_m = dict(pre_norm_g=m_pre_norm_g, post_norm_g=m_post_norm_g, ln_v_g=m_ln_v_g,
                   ln_v_b=m_ln_v_b, b_spatial=m_b_spatial, sinks=m_sinks, rel_bias=m_rel_bias, loss=no_state)
    small_v = dict(pre_norm_g=v_pre_norm_g, post_norm_g=v_post_norm_g, ln_v_g=v_ln_v_g,
                   ln_v_b=v_ln_v_b, b_spatial=v_b_spatial, sinks=v_sinks, rel_bias=v_rel_bias, loss=no_state)
    small_g = {name: jnp.stack(grads[name]) for name in _SMALL if name not in ("rel_bias", "loss")}
    small_g["rel_bias"] = drel
    small_g["loss"] = loss_part[0, :1]
    shapes = {name: small_w[name].shape for name in _SMALL}
    slots = _all_gather_direct(_pack_small(small_g), "small_grads_all_gather")
    res_small = _reduce_adamw(slots, _pack_small(small_w), _pack_small(small_m), _pack_small(small_v), "adamw_small")

    recv_ws, recv_in = _exchange_wait(started_in, [zone_ws, zone_in], res_small[0], "grads_wait_in")
    res_ws = _reduce_adamw(recv_ws, w_spatial.reshape(-1, CHUNK), m_w_spatial.reshape(-1, CHUNK),
                           v_w_spatial.reshape(-1, CHUNK), "adamw_w_spatial")
    res_ws = [r.reshape(w_spatial.shape) for r in res_ws]
    res_in = _reduce_adamw(recv_in, w_in_t.reshape(-1, D_MODEL), m_w_in_t.reshape(-1, D_MODEL),
                           v_w_in_t.reshape(-1, D_MODEL), "adamw_w_in")
    res_in = [jnp.swapaxes(r.reshape(w_in_t.shape), 1, 2) for r in res_in]
    res_small = [_unpack_small(r, shapes) for r in res_small]

    order = ("pre_norm_g", "w_in", "ln_v_g", "ln_v_b", "w_spatial", "b_spatial", "sinks", "rel_bias", "w_out",
             "post_norm_g")
    outs = [res_small[0]["loss"][0], grad_x]
    for kind in range(4):
        for name in order:
            if name == "w_in":
                outs.append(res_in[kind])
            elif name == "w_out":
                outs.append(res_out[kind])
            elif name == "w_spatial":
                outs.append(res_ws[kind])
            else:
                outs.append(res_small[kind][name])
    return tuple(outs)
```
